```python
import math
import jax, jax.numpy as jnp
from jax import lax
import numpy as np

D_MODEL = 2048
BATCH = 8
SEQ = 8192
DEPTH = 1

D_MIX = D_MODEL
CONV_CH = D_MIX // 2
CONV_GROUPS = 16
CONV_WIDTH = 31
N_HEADS = 16
HEAD_DIM = 64
ATT_WIDTH = N_HEADS * HEAD_DIM
D_IN = 2 * CONV_CH + 3 * ATT_WIDTH
D_FF = 5632
DILATED_PATTERNS = ((128, 1), (512, 4), (2048, 16))
ALIBI_MAX_BIAS = 8.0
EPS = 1e-6

kernel_name = "hymba_conformer_dilated_alibi_layer"


def rms_norm(x, g):
    xf = x.astype(jnp.float32)
    y = xf * lax.rsqrt(jnp.mean(xf * xf, axis=-1, keepdims=True) + EPS)
    return (y * g.astype(jnp.float32)).astype(x.dtype)


def layer_norm(x, g, b):
    xf = x.astype(jnp.float32)
    mu = jnp.mean(xf, axis=-1, keepdims=True)
    var = jnp.mean(jnp.square(xf - mu), axis=-1, keepdims=True)
    y = (xf - mu) * lax.rsqrt(var + EPS)
    return (y * g.astype(jnp.float32) + b.astype(jnp.float32)).astype(x.dtype)


def swiglu_ffn(x, w_gate, w_up, w_down):
    return (jax.nn.silu(x @ w_gate) * (x @ w_up)) @ w_down


def alibi_slopes(n_heads):
    return 2.0 ** (-ALIBI_MAX_BIAS * jnp.arange(1, n_heads + 1, dtype=jnp.float32) / n_heads)


def conformer_conv(u, w_dw, b_dw, ln_g, ln_b):
    a, gate = jnp.split(u, 2, axis=-1)
    v = a * jax.nn.sigmoid(gate)
    v_pad = jnp.pad(v, ((0, 0), (CONV_WIDTH - 1, 0), (0, 0)))
    y = lax.conv_general_dilated(
        v_pad, w_dw[:, None, :].astype(v.dtype), window_strides=(1,), padding="VALID",
        dimension_numbers=("NWC", "WIO", "NWC"), feature_group_count=CONV_CH)
    y = y + b_dw.astype(y.dtype)
    y = layer_norm(y, ln_g, ln_b)
    return jax.nn.silu(y)


def dilated_branch(q, k, v, window, dilation, slopes):
    b, h, s, hd = q.shape
    w = window // dilation
    chunk = dilation * w
    s_pad = -(-s // chunk) * chunk
    n_sub = s_pad // dilation
    nb = n_sub // w

    def to_blocks(t):
        t = jnp.pad(t, ((0, 0), (0, 0), (0, s_pad - s), (0, 0)))
        t = t.reshape(b, h, n_sub, dilation, hd).transpose(0, 1, 3, 2, 4)
        return t.reshape(b, h, dilation, nb, w, hd)

    def with_prev(t):
        prev = jnp.pad(t[:, :, :, :-1], ((0, 0), (0, 0), (0, 0), (1, 0), (0, 0), (0, 0)))
        return jnp.concatenate([prev, t], axis=4)

    qb = to_blocks(q)
    kk = with_prev(to_blocks(k))
    vv = with_prev(to_blocks(v))

    scores = jnp.einsum("bhrnqd,bhrnkd->bhrnqk", qb, kk).astype(jnp.float32)
    scores = scores * (1.0 / math.sqrt(hd))
    qi = jnp.arange(w)[:, None]
    kj = jnp.arange(2 * w)[None, :]
    sub_dist = w + qi - kj
    key_sub_pos = (jnp.arange(nb)[:, None, None] - 1) * w + kj[None]
    valid = (sub_dist >= 0)[None] & (sub_dist <= w)[None] & (key_sub_pos >= 0)
    token_dist = (dilation * sub_dist).astype(jnp.float32)
    bias = -slopes[:, None, None] * token_dist[None]
    scores = scores + bias[None, :, None, None]
    scores = jnp.where(valid[None, None, None], scores, -jnp.inf)

    lse = jax.nn.logsumexp(scores, axis=-1)
    p = jnp.exp(scores - lse[..., None])
    out = jnp.einsum("bhrnqk,bhrnkd->bhrnqd", p, vv.astype(jnp.float32))

    out = out.reshape(b, h, dilation, n_sub, hd).transpose(0, 1, 3, 2, 4).reshape(b, h, s_pad, hd)
    lse = lse.reshape(b, h, dilation, n_sub).transpose(0, 1, 3, 2).reshape(b, h, s_pad)
    return out[:, :, :s], lse[:, :, :s]


def dilated_attention(zq, zk, zv, q_norm_g, k_norm_g):
    b, s, _ = zq.shape
    heads = lambda t: t.reshape(b, s, N_HEADS, HEAD_DIM).transpose(0, 2, 1, 3)
    q = rms_norm(heads(zq), q_norm_g)
    k = rms_norm(heads(zk), k_norm_g)
    v = heads(zv)
    slopes = alibi_slopes(N_HEADS)
    outs, lses = [], []
    for window, dilation in DILATED_PATTERNS:
        o, l = dilated_branch(q, k, v, window, dilation, slopes)
        outs.append(o)
        lses.append(l)
    wts = jax.nn.softmax(jnp.stack(lses, axis=0), axis=0)
    out = jnp.sum(wts[..., None] * jnp.stack(outs, axis=0), axis=0)
    return out.transpose(0, 2, 1, 3).reshape(b, s, ATT_WIDTH).astype(zq.dtype)


def _fwd_setup_inputs(seed: int = 0) -> dict:
    key = jax.random.key(seed)
    ks = jax.random.split(key, 20)
    f32 = jnp.float32
    nrm = lambda k, shape, fan_in: jax.random.normal(k, shape, f32) * (fan_in ** -0.5)
    gain = lambda k, shape: 1.0 + 0.02 * jax.random.normal(k, shape, f32)
    small = lambda k, shape: 0.02 * jax.random.normal(k, shape, f32)
    L = DEPTH
    return {
        "x": jax.random.normal(ks[0], (BATCH, SEQ, D_MODEL), f32),
        "ffn1_norm_g": gain(ks[1], (L, D_MODEL)),
        "ffn1_w_gate": nrm(ks[2], (L, D_MODEL, D_FF), D_MODEL),
        "ffn1_w_up": nrm(ks[3], (L, D_MODEL, D_FF), D_MODEL),
        "ffn1_w_down": nrm(ks[4], (L, D_FF, D_MODEL), D_FF),
        "mix_norm_g": gain(ks[5], (L, D_MODEL)),
        "w_in": nrm(ks[6], (L, D_MODEL, D_IN), D_MODEL),
        "conv_w_dw": nrm(ks[7], (L, CONV_WIDTH, CONV_CH), CONV_WIDTH),
        "conv_b_dw": small(ks[8], (L, CONV_CH)),
        "conv_ln_g": gain(ks[9], (L, CONV_CH)),
        "conv_ln_b": small(ks[10], (L, CONV_CH)),
        "q_norm_g": gain(ks[11], (L, HEAD_DIM)),
        "k_norm_g": gain(ks[12], (L, HEAD_DIM)),
        "w_out": nrm(ks[13], (L, D_MIX, D_MODEL), D_MIX),
        "ffn2_norm_g": gain(ks[14], (L, D_MODEL)),
        "ffn2_w_gate": nrm(ks[15], (L, D_MODEL, D_FF), D_MODEL),
        "ffn2_w_up": nrm(ks[16], (L, D_MODEL, D_FF), D_MODEL),
        "ffn2_w_down": nrm(ks[17], (L, D_FF, D_MODEL), D_FF),
    }


def _fwd_reference(x, ffn1_norm_g, ffn1_w_gate, ffn1_w_up, ffn1_w_down, mix_norm_g, w_in,
              conv_w_dw, conv_b_dw, conv_ln_g, conv_ln_b, q_norm_g, k_norm_g, w_out,
              ffn2_norm_g, ffn2_w_gate, ffn2_w_up, ffn2_w_down):
    for l in range(DEPTH):
        x = x + 0.5 * swiglu_ffn(rms_norm(x, ffn1_norm_g[l]), ffn1_w_gate[l], ffn1_w_up[l], ffn1_w_down[l])
        h = rms_norm(x, mix_norm_g[l])
        z = h @ w_in[l]
        c0 = 2 * CONV_CH
        z_conv = z[..., :c0]
        z_q = z[..., c0:c0 + ATT_WIDTH]
        z_k = z[..., c0 + ATT_WIDTH:c0 + 2 * ATT_WIDTH]
        z_v = z[..., c0 + 2 * ATT_WIDTH:]
        y_conv = conformer_conv(z_conv, conv_w_dw[l], conv_b_dw[l], conv_ln_g[l], conv_ln_b[l])
        y_att = dilated_attention(z_q, z_k, z_v, q_norm_g[l], k_norm_g[l])
        y = jnp.concatenate([y_conv, y_att], axis=-1) @ w_out[l]
        x = x + y
        x = x + 0.5 * swiglu_ffn(rms_norm(x, ffn2_norm_g[l]), ffn2_w_gate[l], ffn2_w_up[l], ffn2_w_down[l])
    return x


import jax as _jax
import jax.numpy as _jnp

TWIN_FORMAT = 'train_step'
FWD_PARAMS = ['x', 'ffn1_norm_g', 'ffn1_w_gate', 'ffn1_w_up', 'ffn1_w_down', 'mix_norm_g', 'w_in', 'conv_w_dw', 'conv_b_dw', 'conv_ln_g', 'conv_ln_b', 'q_norm_g', 'k_norm_g', 'w_out', 'ffn2_norm_g', 'ffn2_w_gate', 'ffn2_w_up', 'ffn2_w_down']
TWIN_WEIGHTS = ['ffn1_norm_g', 'ffn1_w_gate', 'ffn1_w_up', 'ffn1_w_down', 'mix_norm_g', 'w_in', 'conv_w_dw', 'conv_b_dw', 'conv_ln_g', 'conv_ln_b', 'q_norm_g', 'k_norm_g', 'w_out', 'ffn2_norm_g', 'ffn2_w_gate', 'ffn2_w_up', 'ffn2_w_down']
TWIN_DIFF_INPUT = 'x'
TWIN_INPUTS = ['x', 'ffn1_norm_g', 'ffn1_w_gate', 'ffn1_w_up', 'ffn1_w_down', 'mix_norm_g', 'w_in', 'conv_w_dw', 'conv_b_dw', 'conv_ln_g', 'conv_ln_b', 'q_norm_g', 'k_norm_g', 'w_out', 'ffn2_norm_g', 'ffn2_w_gate', 'ffn2_w_up', 'ffn2_w_down', 'loss_target', 'm_ffn1_norm_g', 'm_ffn1_w_gate', 'm_ffn1_w_up', 'm_ffn1_w_down', 'm_mix_norm_g', 'm_w_in', 'm_conv_w_dw', 'm_conv_b_dw', 'm_conv_ln_g', 'm_conv_ln_b', 'm_q_norm_g', 'm_k_norm_g', 'm_w_out', 'm_ffn2_norm_g', 'm_ffn2_w_gate', 'm_ffn2_w_up', 'm_ffn2_w_down', 'v_ffn1_norm_g', 'v_ffn1_w_gate', 'v_ffn1_w_up', 'v_ffn1_w_down', 'v_mix_norm_g', 'v_w_in', 'v_conv_w_dw', 'v_conv_b_dw', 'v_conv_ln_g', 'v_conv_ln_b', 'v_q_norm_g', 'v_k_norm_g', 'v_w_out', 'v_ffn2_norm_g', 'v_ffn2_w_gate', 'v_ffn2_w_up', 'v_ffn2_w_down']
TWIN_OUTPUTS = ['loss', 'grad_x', 'grad_ffn1_norm_g', 'grad_ffn1_w_gate', 'grad_ffn1_w_up', 'grad_ffn1_w_down', 'grad_mix_norm_g', 'grad_w_in', 'grad_conv_w_dw', 'grad_conv_b_dw', 'grad_conv_ln_g', 'grad_conv_ln_b', 'grad_q_norm_g', 'grad_k_norm_g', 'grad_w_out', 'grad_ffn2_norm_g', 'grad_ffn2_w_gate', 'grad_ffn2_w_up', 'grad_ffn2_w_down', 'delta_ffn1_norm_g', 'delta_ffn1_w_gate', 'delta_ffn1_w_up', 'delta_ffn1_w_down', 'delta_mix_norm_g', 'delta_w_in', 'delta_conv_w_dw', 'delta_conv_b_dw', 'delta_conv_ln_g', 'delta_conv_ln_b', 'delta_q_norm_g', 'delta_k_norm_g', 'delta_w_out', 'delta_ffn2_norm_g', 'delta_ffn2_w_gate', 'delta_ffn2_w_up', 'delta_ffn2_w_down', 'new_m_ffn1_norm_g', 'new_m_ffn1_w_gate', 'new_m_ffn1_w_up', 'new_m_ffn1_w_down', 'new_m_mix_norm_g', 'new_m_w_in', 'new_m_conv_w_dw', 'new_m_conv_b_dw', 'new_m_conv_ln_g', 'new_m_conv_ln_b', 'new_m_q_norm_g', 'new_m_k_norm_g', 'new_m_w_out', 'new_m_ffn2_norm_g', 'new_m_ffn2_w_gate', 'new_m_ffn2_w_up', 'new_m_ffn2_w_down', 'new_v_ffn1_norm_g', 'new_v_ffn1_w_gate', 'new_v_ffn1_w_up', 'new_v_ffn1_w_down', 'new_v_mix_norm_g', 'new_v_w_in', 'new_v_conv_w_dw', 'new_v_conv_b_dw', 'new_v_conv_ln_g', 'new_v_conv_ln_b', 'new_v_q_norm_g', 'new_v_k_norm_g', 'new_v_w_out', 'new_v_ffn2_norm_g', 'new_v_ffn2_w_gate', 'new_v_ffn2_w_up', 'new_v_ffn2_w_down']
TWIN_LEAF_KINDS = {'loss': 'loss', 'grad_x': 'grad_x', 'grad_ffn1_norm_g': 'grad_w', 'grad_ffn1_w_gate': 'grad_w', 'grad_ffn1_w_up': 'grad_w', 'grad_ffn1_w_down': 'grad_w', 'grad_mix_norm_g': 'grad_w', 'grad_w_in': 'grad_w', 'grad_conv_w_dw': 'grad_w', 'grad_conv_b_dw': 'grad_w', 'grad_conv_ln_g': 'grad_w', 'grad_conv_ln_b': 'grad_w', 'grad_q_norm_g': 'grad_w', 'grad_k_norm_g': 'grad_w', 'grad_w_out': 'grad_w', 'grad_ffn2_norm_g': 'grad_w', 'grad_ffn2_w_gate': 'grad_w', 'grad_ffn2_w_up': 'grad_w', 'grad_ffn2_w_down': 'grad_w', 'delta_ffn1_norm_g': 'delta_w', 'delta_ffn1_w_gate': 'delta_w', 'delta_ffn1_w_up': 'delta_w', 'delta_ffn1_w_down': 'delta_w', 'delta_mix_norm_g': 'delta_w', 'delta_w_in': 'delta_w', 'delta_conv_w_dw': 'delta_w', 'delta_conv_b_dw': 'delta_w', 'delta_conv_ln_g': 'delta_w', 'delta_conv_ln_b': 'delta_w', 'delta_q_norm_g': 'delta_w', 'delta_k_norm_g': 'delta_w', 'delta_w_out': 'delta_w', 'delta_ffn2_norm_g': 'delta_w', 'delta_ffn2_w_gate': 'delta_w', 'delta_ffn2_w_up': 'delta_w', 'delta_ffn2_w_down': 'delta_w', 'new_m_ffn1_norm_g': 'new_m', 'new_m_ffn1_w_gate': 'new_m', 'new_m_ffn1_w_up': 'new_m', 'new_m_ffn1_w_down': 'new_m', 'new_m_mix_norm_g': 'new_m', 'new_m_w_in': 'new_m', 'new_m_conv_w_dw': 'new_m', 'new_m_conv_b_dw': 'new_m', 'new_m_conv_ln_g': 'new_m', 'new_m_conv_ln_b': 'new_m', 'new_m_q_norm_g': 'new_m', 'new_m_k_norm_g': 'new_m', 'new_m_w_out': 'new_m', 'new_m_ffn2_norm_g': 'new_m', 'new_m_ffn2_w_gate': 'new_m', 'new_m_ffn2_w_up': 'new_m', 'new_m_ffn2_w_down': 'new_m', 'new_v_ffn1_norm_g': 'new_v', 'new_v_ffn1_w_gate': 'new_v', 'new_v_ffn1_w_up': 'new_v', 'new_v_ffn1_w_down': 'new_v', 'new_v_mix_norm_g': 'new_v', 'new_v_w_in': 'new_v', 'new_v_conv_w_dw': 'new_v', 'new_v_conv_b_dw': 'new_v', 'new_v_conv_ln_g': 'new_v', 'new_v_conv_ln_b': 'new_v', 'new_v_q_norm_g': 'new_v', 'new_v_k_norm_g': 'new_v', 'new_v_w_out': 'new_v', 'new_v_ffn2_norm_g': 'new_v', 'new_v_ffn2_w_gate': 'new_v', 'new_v_ffn2_w_up': 'new_v', 'new_v_ffn2_w_down': 'new_v'}


def _forward(args):
    return _fwd_reference(*[args[k] for k in FWD_PARAMS])


def _output_shape():
    def fwd():
        inp = _fwd_setup_inputs(0)
        return _fwd_reference(*[inp[k] for k in FWD_PARAMS])
    out = _jax.eval_shape(fwd)
    return out.shape, out.dtype

N_MICROBATCH = 1
ADAM_LR = 0.001
ADAM_B1 = 0.9
ADAM_B2 = 0.999
ADAM_EPS = 1e-08
ADAM_WD = 0.01
ADAM_STEP = 10
PER_EXAMPLE_BATCH_AXIS = {'x': 0, 'loss_target': 0}
SHARED_INPUTS = []
_WEIGHT_DTYPES = {'ffn1_norm_g': _jnp.float32, 'ffn1_w_gate': _jnp.float32, 'ffn1_w_up': _jnp.float32, 'ffn1_w_down': _jnp.float32, 'mix_norm_g': _jnp.float32, 'w_in': _jnp.float32, 'conv_w_dw': _jnp.float32, 'conv_b_dw': _jnp.float32, 'conv_ln_g': _jnp.float32, 'conv_ln_b': _jnp.float32, 'q_norm_g': _jnp.float32, 'k_norm_g': _jnp.float32, 'w_out': _jnp.float32, 'ffn2_norm_g': _jnp.float32, 'ffn2_w_gate': _jnp.float32, 'ffn2_w_up': _jnp.float32, 'ffn2_w_down': _jnp.float32}
MOMENT_SCALE = {'ffn1_norm_g': 6.188923e+00, 'ffn1_w_gate': 5.092557e-02, 'ffn1_w_up': 5.545086e-02, 'ffn1_w_down': 9.077145e-02, 'mix_norm_g': 2.075388e+00, 'w_in': 1.350969e-01, 'conv_w_dw': 2.632020e-01, 'conv_b_dw': 4.947201e+00, 'conv_ln_g': 1.409889e+01, 'conv_ln_b': 9.317559e+00, 'q_norm_g': 1.261814e+01, 'k_norm_g': 1.256109e+01, 'w_out': 6.418544e-01, 'ffn2_norm_g': 6.183227e+00, 'ffn2_w_gate': 8.145655e-02, 'ffn2_w_up': 5.879245e-02, 'ffn2_w_down': 9.387282e-02}


def _to_microbatches(a, axis):
    t = _jnp.moveaxis(a, axis, 0)
    t = t.reshape((N_MICROBATCH, t.shape[0] // N_MICROBATCH) + t.shape[1:])
    return _jnp.moveaxis(t, 1, axis + 1)


def setup_inputs(seed: int = 0) -> dict:
    inp = _fwd_setup_inputs(seed)
    key = _jax.random.fold_in(_jax.random.key(seed), 7919)
    shape, _ = _output_shape()
    out = dict(inp)
    out["loss_target"] = _jax.random.normal(_jax.random.fold_in(key, 0), shape, _jnp.float32)
    for i, name in enumerate(TWIN_WEIGHTS):
        w = inp[name].astype(_jnp.float32)
        if MOMENT_SCALE is None:
            s = _jnp.sqrt(_jnp.mean(_jnp.square(w)) + 1e-30)
        else:
            s = MOMENT_SCALE[name]
        km, kv = _jax.random.split(_jax.random.fold_in(key, i + 1))
        out[name] = w
        out["m_" + name] = s * _jax.random.normal(km, w.shape, _jnp.float32)
        out["v_" + name] = (s * s) * _jax.random.uniform(kv, w.shape, _jnp.float32, 0.5, 1.5)
    if N_MICROBATCH > 1:
        for name, axis in PER_EXAMPLE_BATCH_AXIS.items():
            out[name] = _to_microbatches(out[name], axis)
    return {'x': out['x'], 'ffn1_norm_g': out['ffn1_norm_g'], 'ffn1_w_gate': out['ffn1_w_gate'], 'ffn1_w_up': out['ffn1_w_up'], 'ffn1_w_down': out['ffn1_w_down'], 'mix_norm_g': out['mix_norm_g'], 'w_in': out['w_in'], 'conv_w_dw': out['conv_w_dw'], 'conv_b_dw': out['conv_b_dw'], 'conv_ln_g': out['conv_ln_g'], 'conv_ln_b': out['conv_ln_b'], 'q_norm_g': out['q_norm_g'], 'k_norm_g': out['k_norm_g'], 'w_out': out['w_out'], 'ffn2_norm_g': out['ffn2_norm_g'], 'ffn2_w_gate': out['ffn2_w_gate'], 'ffn2_w_up': out['ffn2_w_up'], 'ffn2_w_down': out['ffn2_w_down'], 'loss_target': out['loss_target'], 'm_ffn1_norm_g': out['m_ffn1_norm_g'], 'm_ffn1_w_gate': out['m_ffn1_w_gate'], 'm_ffn1_w_up': out['m_ffn1_w_up'], 'm_ffn1_w_down': out['m_ffn1_w_down'], 'm_mix_norm_g': out['m_mix_norm_g'], 'm_w_in': out['m_w_in'], 'm_conv_w_dw': out['m_conv_w_dw'], 'm_conv_b_dw': out['m_conv_b_dw'], 'm_conv_ln_g': out['m_conv_ln_g'], 'm_conv_ln_b': out['m_conv_ln_b'], 'm_q_norm_g': out['m_q_norm_g'], 'm_k_norm_g': out['m_k_norm_g'], 'm_w_out': out['m_w_out'], 'm_ffn2_norm_g': out['m_ffn2_norm_g'], 'm_ffn2_w_gate': out['m_ffn2_w_gate'], 'm_ffn2_w_up': out['m_ffn2_w_up'], 'm_ffn2_w_down': out['m_ffn2_w_down'], 'v_ffn1_norm_g': out['v_ffn1_norm_g'], 'v_ffn1_w_gate': out['v_ffn1_w_gate'], 'v_ffn1_w_up': out['v_ffn1_w_up'], 'v_ffn1_w_down': out['v_ffn1_w_down'], 'v_mix_norm_g': out['v_mix_norm_g'], 'v_w_in': out['v_w_in'], 'v_conv_w_dw': out['v_conv_w_dw'], 'v_conv_b_dw': out['v_conv_b_dw'], 'v_conv_ln_g': out['v_conv_ln_g'], 'v_conv_ln_b': out['v_conv_ln_b'], 'v_q_norm_g': out['v_q_norm_g'], 'v_k_norm_g': out['v_k_norm_g'], 'v_w_out': out['v_w_out'], 'v_ffn2_norm_g': out['v_ffn2_norm_g'], 'v_ffn2_w_gate': out['v_ffn2_w_gate'], 'v_ffn2_w_up': out['v_ffn2_w_up'], 'v_ffn2_w_down': out['v_ffn2_w_down']}


def _loss(weights, diff, rest, loss_target):
    with _jax.named_scope("forward"):
        args = {**rest, TWIN_DIFF_INPUT: diff, **{k: w.astype(_WEIGHT_DTYPES[k]) for k, w in weights.items()}}
        y = _forward(args)
    with _jax.named_scope("loss_head"):
        err = _jnp.square(y.astype(_jnp.float32) - loss_target)
        return 0.5 * _jnp.sum(_jnp.mean(err, axis=-1)) if err.ndim else 0.5 * err


def _adamw(w, g, m, v):
    m = ADAM_B1 * m + (1.0 - ADAM_B1) * g
    v = ADAM_B2 * v + (1.0 - ADAM_B2) * _jnp.square(g)
    m_hat = m / (1.0 - ADAM_B1 ** ADAM_STEP)
    v_hat = v / (1.0 - ADAM_B2 ** ADAM_STEP)
    delta = -ADAM_LR * (m_hat / (_jnp.sqrt(v_hat) + ADAM_EPS) + ADAM_WD * w)
    return delta, m, v


def reference(x, ffn1_norm_g, ffn1_w_gate, ffn1_w_up, ffn1_w_down, mix_norm_g, w_in, conv_w_dw, conv_b_dw, conv_ln_g, conv_ln_b, q_norm_g, k_norm_g, w_out, ffn2_norm_g, ffn2_w_gate, ffn2_w_up, ffn2_w_down, loss_target, m_ffn1_norm_g, m_ffn1_w_gate, m_ffn1_w_up, m_ffn1_w_down, m_mix_norm_g, m_w_in, m_conv_w_dw, m_conv_b_dw, m_conv_ln_g, m_conv_ln_b, m_q_norm_g, m_k_norm_g, m_w_out, m_ffn2_norm_g, m_ffn2_w_gate, m_ffn2_w_up, m_ffn2_w_down, v_ffn1_norm_g, v_ffn1_w_gate, v_ffn1_w_up, v_ffn1_w_down, v_mix_norm_g, v_w_in, v_conv_w_dw, v_conv_b_dw, v_conv_ln_g, v_conv_ln_b, v_q_norm_g, v_k_norm_g, v_w_out, v_ffn2_norm_g, v_ffn2_w_gate, v_ffn2_w_up, v_ffn2_w_down):
    given = dict(x=x, ffn1_norm_g=ffn1_norm_g, ffn1_w_gate=ffn1_w_gate, ffn1_w_up=ffn1_w_up, ffn1_w_down=ffn1_w_down, mix_norm_g=mix_norm_g, w_in=w_in, conv_w_dw=conv_w_dw, conv_b_dw=conv_b_dw, conv_ln_g=conv_ln_g, conv_ln_b=conv_ln_b, q_norm_g=q_norm_g, k_norm_g=k_norm_g, w_out=w_out, ffn2_norm_g=ffn2_norm_g, ffn2_w_gate=ffn2_w_gate, ffn2_w_up=ffn2_w_up, ffn2_w_down=ffn2_w_down, loss_target=loss_target, m_ffn1_norm_g=m_ffn1_norm_g, m_ffn1_w_gate=m_ffn1_w_gate, m_ffn1_w_up=m_ffn1_w_up, m_ffn1_w_down=m_ffn1_w_down, m_mix_norm_g=m_mix_norm_g, m_w_in=m_w_in, m_conv_w_dw=m_conv_w_dw, m_conv_b_dw=m_conv_b_dw, m_conv_ln_g=m_conv_ln_g, m_conv_ln_b=m_conv_ln_b, m_q_norm_g=m_q_norm_g, m_k_norm_g=m_k_norm_g, m_w_out=m_w_out, m_ffn2_norm_g=m_ffn2_norm_g, m_ffn2_w_gate=m_ffn2_w_gate, m_ffn2_w_up=m_ffn2_w_up, m_ffn2_w_down=m_ffn2_w_down, v_ffn1_norm_g=v_ffn1_norm_g, v_ffn1_w_gate=v_ffn1_w_gate, v_ffn1_w_up=v_ffn1_w_up, v_ffn1_w_down=v_ffn1_w_down, v_mix_norm_g=v_mix_norm_g, v_w_in=v_w_in, v_conv_w_dw=v_conv_w_dw, v_conv_b_dw=v_conv_b_dw, v_conv_ln_g=v_conv_ln_g, v_conv_ln_b=v_conv_ln_b, v_q_norm_g=v_q_norm_g, v_k_norm_g=v_k_norm_g, v_w_out=v_w_out, v_ffn2_norm_g=v_ffn2_norm_g, v_ffn2_w_gate=v_ffn2_w_gate, v_ffn2_w_up=v_ffn2_w_up, v_ffn2_w_down=v_ffn2_w_down)
    weights = {n: given[n] for n in TWIN_WEIGHTS}
    shared = {n: given[n] for n in SHARED_INPUTS}
    per_example = {n: given[n] for n in ['x']}
    grad_fn = _jax.value_and_grad(_loss, argnums=(0, 1))

    def one_microbatch(ex, loss_target):
        ex = dict(ex)
        diff = ex.pop(TWIN_DIFF_INPUT)
        return grad_fn(weights, diff, {**shared, **ex}, loss_target)

    if N_MICROBATCH == 1:
        loss, (grad_w, grad_x) = one_microbatch(per_example, given["loss_target"])
    else:
        def body(carry, xs):
            loss_sum, grad_sum = carry
            l_k, (gw_k, gx_k) = one_microbatch(xs[0], xs[1])
            with _jax.named_scope("update"):
                return (loss_sum + l_k, _jax.tree.map(_jnp.add, grad_sum, gw_k)), gx_k

        init = (_jnp.zeros((), _jnp.float32), _jax.tree.map(_jnp.zeros_like, weights))
        (loss, grad_w), grad_x = _jax.lax.scan(body, init, (per_example, given["loss_target"]))
    with _jax.named_scope("update"):
        delta_w, new_m, new_v = {}, {}, {}
        for n in TWIN_WEIGHTS:
            delta_w[n], new_m[n], new_v[n] = _adamw(weights[n], grad_w[n], given["m_" + n], given["v_" + n])
    return (loss, grad_x, *[grad_w[n] for n in TWIN_WEIGHTS], *[delta_w[n] for n in TWIN_WEIGHTS],
            *[new_m[n] for n in TWIN_WEIGHTS], *[new_v[n] for n in TWIN_WEIGHTS])
```

```python
import math

import jax
import jax.numpy as jnp
from jax import lax
from jax.experimental import pallas as pl
from jax.experimental.pallas import tpu as pltpu

F32 = jnp.float32
BF16 = jnp.bfloat16

N_DEV = 8
EPS = 1e-6
HEAD_DIM = 64
LANES = 128
CONV_WIDTH = 31
HALO = 32
ROW_CHUNK = 32
ATT_BLOCK = 128
DILATIONS = (1, 4, 16)
ALIBI_MAX_BIAS = 8.0
MASKED = -1e30
VMEM_LIMIT = 56 * 1024 * 1024

ADAM_LR = 0.001
ADAM_B1 = 0.9
ADAM_B2 = 0.999
ADAM_EPS = 1e-08
ADAM_WD = 0.01
ADAM_STEP = 10

MESH_AXES = ("x", "y", "c")


def _sds(shape, dtype):
    return jax.ShapeDtypeStruct(tuple(shape), dtype)


def _params(*sem):
    return pltpu.CompilerParams(dimension_semantics=sem, vmem_limit_bytes=VMEM_LIMIT)


def _sigmoid(v):
    return 1.0 / (1.0 + jnp.exp(-v))


def _row_tile(t, want):
    for cand in range(min(want, t) // 8 * 8, 0, -8):
        if t % cand == 0:
            return cand
    return t


def _mesh_pos():
    return lax.axis_index("x"), lax.axis_index("y"), lax.axis_index("c")


def all_gather(shard, name):
    def body(x_ref, out_ref, send_sems, recv_sems, local_sem):
        x, y, c = _mesh_pos()
        me, sibling = (x, y, c), (x, y, 1 - c)
        chips = [(1 - x, y), (x, 1 - y), (1 - x, 1 - y)]

        def slot(px, py, pc):
            return out_ref.at[4 * px + 2 * py + pc]

        def copy(k, block, to, src=None):
            return pltpu.make_async_remote_copy(
                src_ref=slot(*block) if src is None else src, dst_ref=slot(*block),
                send_sem=send_sems.at[k], recv_sem=recv_sems.at[k],
                device_id=to, device_id_type=pl.DeviceIdType.MESH)

        mine = pltpu.make_async_copy(x_ref, slot(*me), local_sem)
        mine.start()
        first = [copy(0, me, sibling, src=x_ref)]
        first += [copy(1 + j, me, (*chip, c), src=x_ref) for j, chip in enumerate(chips)]
        for cp in first:
            cp.start()
        passed = [copy(4 + j, (*chip, c), sibling) for j, chip in enumerate(chips)]
        for j, chip in enumerate(chips):
            copy(1 + j, (*chip, c), me).wait_recv()
            passed[j].start()
        copy(0, sibling, me).wait_recv()
        for j, chip in enumerate(chips):
            copy(4 + j, (*chip, 1 - c), me).wait_recv()
        for cp in first + passed:
            cp.wait_send()
        mine.wait()

    return pl.pallas_call(
        body, name=name,
        out_shape=_sds((N_DEV,) + shard.shape, shard.dtype),
        in_specs=[pl.BlockSpec(memory_space=pl.ANY)],
        out_specs=pl.BlockSpec(memory_space=pl.ANY),
        scratch_shapes=[pltpu.SemaphoreType.DMA((7,)), pltpu.SemaphoreType.DMA((7,)),
                        pltpu.SemaphoreType.DMA(())],
    )(shard)


def exchange_partials(partial, name):
    def body(p_ref, out_ref, send_sems, recv_sems, local_sem):
        x, y, c = _mesh_pos()
        me = 4 * x + 2 * y + c
        flips = [(fx, fy, fc) for fx in (0, 1) for fy in (0, 1) for fc in (0, 1)][1:]

        def peer(f):
            return (1 - x if f[0] else x, 1 - y if f[1] else y, 1 - c if f[2] else c)

        def copy(k, f):
            px, py, pc = peer(f)
            them = 4 * px + 2 * py + pc
            return (pltpu.make_async_remote_copy(
                        src_ref=p_ref.at[them], dst_ref=out_ref.at[me],
                        send_sem=send_sems.at[k], recv_sem=recv_sems.at[k],
                        device_id=(px, py, pc), device_id_type=pl.DeviceIdType.MESH),
                    pltpu.make_async_remote_copy(
                        src_ref=p_ref.at[them], dst_ref=out_ref.at[them],
                        send_sem=send_sems.at[k], recv_sem=recv_sems.at[k],
                        device_id=(px, py, pc), device_id_type=pl.DeviceIdType.MESH))

        mine = pltpu.make_async_copy(p_ref.at[me], out_ref.at[me], local_sem)
        mine.start()
        copies = [copy(k, f) for k, f in enumerate(flips)]
        for send, _ in copies:
            send.start()
        for send, recv in copies:
            recv.wait_recv()
            send.wait_send()
        mine.wait()

    return pl.pallas_call(
        body, name=name,
        out_shape=_sds(partial.shape, partial.dtype),
        in_specs=[pl.BlockSpec(memory_space=pl.ANY)],
        out_specs=pl.BlockSpec(memory_space=pl.ANY),
        scratch_shapes=[pltpu.SemaphoreType.DMA((7,)), pltpu.SemaphoreType.DMA((7,)),
                        pltpu.SemaphoreType.DMA(())],
    )(partial)


def adamw(w, m, v, parts, name):
    n_parts, rows, cols = parts.shape
    tr = _row_tile(rows, 128)
    c1 = 1.0 - ADAM_B1 ** ADAM_STEP
    c2 = 1.0 - ADAM_B2 ** ADAM_STEP

    def body(w_ref, m_ref, v_ref, p_ref, g_ref, d_ref, nm_ref, nv_ref):
        g = p_ref[0]
        for s in range(1, n_parts):
            g = g + p_ref[s]
        nm = ADAM_B1 * m_ref[...] + (1.0 - ADAM_B1) * g
        nv = ADAM_B2 * v_ref[...] + (1.0 - ADAM_B2) * (g * g)
        delta = -ADAM_LR * ((nm / c1) / (jnp.sqrt(nv / c2) + ADAM_EPS) + ADAM_WD * w_ref[...])
        g_ref[...] = g
        d_ref[...] = delta
        nm_ref[...] = nm
        nv_ref[...] = nv

    mat = pl.BlockSpec((tr, cols), lambda i: (i, 0))
    return pl.pallas_call(
        body, name=name, grid=(rows // tr,),
        in_specs=[mat, mat, mat, pl.BlockSpec((n_parts, tr, cols), lambda i: (0, i, 0))],
        out_specs=[mat, mat, mat, mat],
        out_shape=[_sds((rows, cols), F32)] * 4,
        compiler_params=_params("parallel"),
    )(w, m, v, parts)


_NN = (((1,), (0,)), ((), ()))
_NT = (((1,), (1,)), ((), ()))
_TN = (((0,), (0,)), ((), ()))


def mm_cols(name, a, b_list, b_specs, nt, extras, extra_specs, out_shapes, out_specs, epilogue, n_blk):
    t_len, k_len = a.shape
    tm = _row_tile(t_len, 512)
    nb, ne = len(b_list), len(extras)

    def body(*refs):
        a_ref, b_refs = refs[0], refs[1:1 + nb]
        e_refs, o_refs = refs[1 + nb:1 + nb + ne], refs[1 + nb + ne:]
        av = a_ref[...]
        accs = [lax.dot_general(av, br[...], _NT if nt else _NN, preferred_element_type=F32)
                for br in b_refs]
        epilogue(accs, e_refs, o_refs)

    return pl.pallas_call(
        body, name=name, grid=(n_blk, t_len // tm),
        in_specs=[pl.BlockSpec((tm, k_len), lambda j, t: (t, 0))] + list(b_specs) + list(extra_specs(tm)),
        out_specs=list(out_specs(tm)), out_shape=list(out_shapes),
        compiler_params=_params("parallel", "parallel"),
    )(a, *b_list, *extras)


def mm_reduce(name, a_list, a_specs, b_list, b_specs, nt, res, scale, t_len, n_len, n_blk):
    tm = _row_tile(t_len, 512)
    na = len(a_list)
    has_res = res is not None

    def body(*refs):
        a_refs, b_refs = refs[:na], refs[na:2 * na]
        r_ref = refs[2 * na] if has_res else None
        o_ref, acc = refs[-2], refs[-1]
        j = pl.program_id(1)

        @pl.when(j == 0)
        def _():
            acc[...] = jnp.zeros_like(acc)

        part = None
        for ar, br in zip(a_refs, b_refs):
            d = lax.dot_general(ar[...], br[...], _NT if nt else _NN, preferred_element_type=F32)
            part = d if part is None else part + d
        acc[...] += part

        @pl.when(j == n_blk - 1)
        def _():
            val = acc[...] * scale if scale != 1.0 else acc[...]
            o_ref[...] = r_ref[...] + val if has_res else val

    row = pl.BlockSpec((tm, n_len), lambda t, j: (t, 0))
    return pl.pallas_call(
        body, name=name, grid=(t_len // tm, n_blk),
        in_specs=list(a_specs(tm)) + list(b_specs) + ([row] if has_res else []),
        out_specs=row, out_shape=_sds((t_len, n_len), F32),
        scratch_shapes=[pltpu.VMEM((tm, n_len), F32)],
        compiler_params=_params("parallel", "arbitrary"),
    )(*a_list, *b_list, *([res] if has_res else []))


def mm_tn(name, x, x_spec, dy_list, dy_specs, out_shapes, out_specs, scale, t_len, n_blk):
    tt = _row_tile(t_len, 512)
    nd = len(dy_list)
    nt_steps = t_len // tt

    def body(*refs):
        x_ref, dy_refs, o_refs = refs[0], refs[1:1 + nd], refs[1 + nd:]
        t = pl.program_id(1)
        xv = x_ref[...]
        for dr, orf in zip(dy_refs, o_refs):
            d = lax.dot_general(xv, dr[...], _TN, preferred_element_type=F32)

            @pl.when(t == 0)
            def _():
                orf[...] = d

            @pl.when(t > 0)
            def _():
                orf[...] += d

        if scale != 1.0:
            @pl.when(t == nt_steps - 1)
            def _():
                for orf in o_refs:
                    orf[...] = orf[...] * scale

    return pl.pallas_call(
        body, name=name, grid=(n_blk, nt_steps),
        in_specs=[x_spec(tt)] + list(dy_specs(tt)),
        out_specs=list(out_specs), out_shape=list(out_shapes),
        compiler_params=_params("parallel", "arbitrary"),
    )(x, *dy_list)


def rms_fwd(x, g, name):
    t_len, d = x.shape
    tm = _row_tile(t_len, 512)

    def body(x_ref, g_ref, h_ref):
        xv = x_ref[...]
        r = lax.rsqrt(jnp.mean(xv * xv, axis=-1, keepdims=True) + EPS)
        h_ref[...] = (xv * r * g_ref[...]).astype(BF16)

    row = pl.BlockSpec((tm, d), lambda i: (i, 0))
    return pl.pallas_call(
        body, name=name, grid=(t_len // tm,),
        in_specs=[row, pl.BlockSpec((1, d), lambda i: (0, 0))],
        out_specs=row, out_shape=_sds((t_len, d), BF16),
        compiler_params=_params("parallel"),
    )(x, g)


def rms_bwd(x, g, dh, dres, name):
    t_len, d = x.shape
    tm = _row_tile(t_len, 512)

    def body(x_ref, g_ref, dh_ref, dr_ref, dx_ref, dxb_ref, dg_ref):
        i = pl.program_id(0)
        xv = x_ref[...]
        r = lax.rsqrt(jnp.mean(xv * xv, axis=-1, keepdims=True) + EPS)
        xh = xv * r
        dhv = dh_ref[...]

        @pl.when(i == 0)
        def _():
            dg_ref[...] = jnp.zeros_like(dg_ref)

        dg_ref[...] += jnp.sum(dhv * xh, axis=0, keepdims=True)
        dxh = dhv * g_ref[...]
        dx = dr_ref[...] + r * (dxh - xh * jnp.mean(dxh * xh, axis=-1, keepdims=True))
        dx_ref[...] = dx
        dxb_ref[...] = dx.astype(BF16)

    row = pl.BlockSpec((tm, d), lambda i: (i, 0))
    vec = pl.BlockSpec((1, d), lambda i: (0, 0))
    return pl.pallas_call(
        body, name=name, grid=(t_len // tm,),
        in_specs=[row, vec, row, row],
        out_specs=[row, row, vec],
        out_shape=[_sds((t_len, d), F32), _sds((t_len, d), BF16), _sds((1, d), F32)],
        compiler_params=_params("arbitrary"),
    )(x, g, dh, dres)


def loss_head(y, target, name):
    t_len, d = y.shape
    tm = _row_tile(t_len, 512)

    def body(y_ref, t_ref, l_ref, dy_ref, dyb_ref):
        i = pl.program_id(0)
        err = y_ref[...] - t_ref[...]

        @pl.when(i == 0)
        def _():
            l_ref[...] = jnp.zeros_like(l_ref)

        rows = jnp.sum(err * err, axis=-1, keepdims=True) * (1.0 / d)
        l_ref[...] += 0.5 * jnp.sum(rows, axis=0, keepdims=True)
        dy = err * (1.0 / d)
        dy_ref[...] = dy
        dyb_ref[...] = dy.astype(BF16)

    row = pl.BlockSpec((tm, d), lambda i: (i, 0))
    return pl.pallas_call(
        body, name=name, grid=(t_len // tm,),
        in_specs=[row, row],
        out_specs=[pl.BlockSpec((8, LANES), lambda i: (0, 0)), row, row],
        out_shape=[_sds((8, LANES), F32), _sds((t_len, d), F32), _sds((t_len, d), BF16)],
        compiler_params=_params("arbitrary"),
    )(y, target)


def _conv_specs(tm, ch):
    per = tm // HALO
    cur = lambda cb: pl.BlockSpec((tm, ch), lambda i: (i, cb))
    prev = lambda cb: pl.BlockSpec((HALO, ch), lambda i: (jnp.maximum(i * per - 1, 0), cb))
    return [cur(0), cur(1), prev(0), prev(1)]


def _fill_glu(ext, a_ref, gt_ref, ap_ref, gp_ref, i, tm):
    vp = ap_ref[...] * _sigmoid(gp_ref[...])
    ext[0:HALO, :] = jnp.where(i > 0, vp, 0.0)
    ext[HALO:HALO + tm, :] = a_ref[...] * _sigmoid(gt_ref[...])


def _conv_rows(ext, w_ref, b_ref, r0):
    acc = jnp.broadcast_to(b_ref[...], (ROW_CHUNK, b_ref.shape[1]))
    for k in range(CONV_WIDTH):
        acc = acc + w_ref[k:k + 1, :] * ext[pl.ds(r0 + HALO - (CONV_WIDTH - 1) + k, ROW_CHUNK), :]
    return acc


def _layer_norm(yv):
    mu = jnp.mean(yv, axis=-1, keepdims=True)
    cen = yv - mu
    var = jnp.mean(cen * cen, axis=-1, keepdims=True)
    rstd = lax.rsqrt(var + EPS)
    return cen * rstd, rstd


def conv_fwd(z, w, b, lg, lb, name):
    t_len = z.shape[0]
    ch = w.shape[1]
    tm = _row_tile(t_len, 256)

    def body(a_ref, gt_ref, ap_ref, gp_ref, w_ref, b_ref, lg_ref, lb_ref, y_ref, ext):
        i = pl.program_id(0)
        _fill_glu(ext, a_ref, gt_ref, ap_ref, gp_ref, i, tm)
        for r0 in range(0, tm, ROW_CHUNK):
            xh, _ = _layer_norm(_conv_rows(ext, w_ref, b_ref, r0))
            u = xh * lg_ref[...] + lb_ref[...]
            y_ref[r0:r0 + ROW_CHUNK, :] = (u * _sigmoid(u)).astype(BF16)

    vec = pl.BlockSpec((1, ch), lambda i: (0, 0))
    return pl.pallas_call(
        body, name=name, grid=(t_len // tm,),
        in_specs=_conv_specs(tm, ch) + [pl.BlockSpec((32, ch), lambda i: (0, 0)), vec, vec, vec],
        out_specs=pl.BlockSpec((tm, ch), lambda i: (i, 0)),
        out_shape=_sds((t_len, ch), BF16),
        scratch_shapes=[pltpu.VMEM((HALO + tm, ch), F32)],
        compiler_params=_params("parallel"),
    )(z, z, z, z, w, b, lg, lb)


def conv_bwd_norm(z, dy_cat, w, b, lg, lb, name):
    t_len = z.shape[0]
    ch = w.shape[1]
    tm = _row_tile(t_len, 256)

    def body(a_ref, gt_ref, ap_ref, gp_ref, dy_ref, w_ref, b_ref, lg_ref, lb_ref,
             dc_ref, dlg_ref, dlb_ref, db_ref, ext):
        i = pl.program_id(0)
        _fill_glu(ext, a_ref, gt_ref, ap_ref, gp_ref, i, tm)

        @pl.when(i == 0)
        def _():
            dlg_ref[...] = jnp.zeros_like(dlg_ref)
            dlb_ref[...] = jnp.zeros_like(dlb_ref)
            db_ref[...] = jnp.zeros_like(db_ref)

        for r0 in range(0, tm, ROW_CHUNK):
            xh, rstd = _layer_norm(_conv_rows(ext, w_ref, b_ref, r0))
            u = xh * lg_ref[...] + lb_ref[...]
            sg = _sigmoid(u)
            du = dy_ref[r0:r0 + ROW_CHUNK, :] * (sg * (1.0 + u * (1.0 - sg)))
            dlg_ref[...] += jnp.sum(du * xh, axis=0, keepdims=True)
            dlb_ref[...] += jnp.sum(du, axis=0, keepdims=True)
            dxh = du * lg_ref[...]
            dc = rstd * (dxh - jnp.mean(dxh, axis=-1, keepdims=True)
                         - xh * jnp.mean(dxh * xh, axis=-1, keepdims=True))
            db_ref[...] += jnp.sum(dc, axis=0, keepdims=True)
            dc_ref[r0:r0 + ROW_CHUNK, :] = dc

    vec = pl.BlockSpec((1, ch), lambda i: (0, 0))
    row = pl.BlockSpec((tm, ch), lambda i: (i, 0))
    return pl.pallas_call(
        body, name=name, grid=(t_len // tm,),
        in_specs=_conv_specs(tm, ch) + [row, pl.BlockSpec((32, ch), lambda i: (0, 0)), vec, vec, vec],
        out_specs=[row, vec, vec, vec],
        out_shape=[_sds((t_len, ch), F32)] + [_sds((1, ch), F32)] * 3,
        scratch_shapes=[pltpu.VMEM((HALO + tm, ch), F32)],
        compiler_params=_params("arbitrary"),
    )(z, z, z, z, dy_cat, w, b, lg, lb)


def conv_bwd_taps(z, dc, w, name):
    t_len = z.shape[0]
    ch = w.shape[1]
    tm = _row_tile(t_len, 256)
    per = tm // HALO
    n_tiles = t_len // tm
    last_halo = t_len // HALO - 1

    def body(a_ref, gt_ref, ap_ref, gp_ref, dc_ref, dn_ref, w_ref, dz_a_ref, dz_g_ref, dw_ref, ext, dext):
        i = pl.program_id(0)
        _fill_glu(ext, a_ref, gt_ref, ap_ref, gp_ref, i, tm)
        dext[0:tm, :] = dc_ref[...]
        dext[tm:tm + HALO, :] = jnp.where(i < n_tiles - 1, dn_ref[...], 0.0)

        @pl.when(i == 0)
        def _():
            dw_ref[...] = jnp.zeros_like(dw_ref)

        for r0 in range(0, tm, ROW_CHUNK):
            dcv = dext[r0:r0 + ROW_CHUNK, :]
            dv = jnp.zeros((ROW_CHUNK, ch), F32)
            for k in range(CONV_WIDTH):
                dv = dv + w_ref[k:k + 1, :] * dext[pl.ds(r0 + (CONV_WIDTH - 1) - k, ROW_CHUNK), :]
                prod = dcv * ext[pl.ds(r0 + HALO - (CONV_WIDTH - 1) + k, ROW_CHUNK), :]
                fold = prod[0:8]
                for s in range(8, ROW_CHUNK, 8):
                    fold = fold + prod[s:s + 8]
                dw_ref[k] += fold
            av = a_ref[r0:r0 + ROW_CHUNK, :]
            sg = _sigmoid(gt_ref[r0:r0 + ROW_CHUNK, :])
            dz_a_ref[r0:r0 + ROW_CHUNK, :] = (dv * sg).astype(BF16)
            dz_g_ref[r0:r0 + ROW_CHUNK, :] = (dv * av * sg * (1.0 - sg)).astype(BF16)

    row = pl.BlockSpec((tm, ch), lambda i: (i, 0))
    nxt = pl.BlockSpec((HALO, ch), lambda i: (jnp.minimum((i + 1) * per, last_halo), 0))
    return pl.pallas_call(
        body, name=name, grid=(n_tiles,),
        in_specs=_conv_specs(tm, ch) + [row, nxt, pl.BlockSpec((32, ch), lambda i: (0, 0))],
        out_specs=[row, row, pl.BlockSpec((32, 8, ch), lambda i: (0, 0, 0))],
        out_shape=[_sds((t_len, ch), BF16), _sds((t_len, ch), BF16), _sds((32, 8, ch), F32)],
        scratch_shapes=[pltpu.VMEM((HALO + tm, ch), F32), pltpu.VMEM((tm + HALO, ch), F32)],
        compiler_params=_params("arbitrary"),
    )(z, z, z, z, dc, dc, w)


def _head_masks(rows):
    lane = lax.broadcasted_iota(jnp.int32, (rows, LANES), 1)
    low = lane < HEAD_DIM
    return low, jnp.logical_not(low)


def _per_head_mean(val, low):
    s_low = jnp.sum(jnp.where(low, val, 0.0), axis=-1, keepdims=True)
    s_high = jnp.sum(jnp.where(low, 0.0, val), axis=-1, keepdims=True)
    return jnp.where(low, s_low, s_high) * (1.0 / HEAD_DIM)


def qk_norm_fwd(z, g2, ch, name):
    t_len = z.shape[0]
    tm = _row_tile(t_len, 1024)
    n_col = 2 * ch // LANES
    z_off = 2 * ch // LANES

    def body(z_ref, g_ref, o_ref):
        low, _ = _head_masks(tm)
        xv = z_ref[...]
        r = lax.rsqrt(_per_head_mean(xv * xv, low) + EPS)
        o_ref[...] = xv * r * g_ref[...]

    return pl.pallas_call(
        body, name=name, grid=(t_len // tm, n_col),
        in_specs=[pl.BlockSpec((tm, LANES), lambda i, cb: (i, z_off + cb)),
                  pl.BlockSpec((1, LANES), lambda i, cb: (0, cb))],
        out_specs=pl.BlockSpec((tm, LANES), lambda i, cb: (i, cb)),
        out_shape=_sds((t_len, 2 * ch), F32),
        compiler_params=_params("parallel", "parallel"),
    )(z, g2)


def qk_norm_bwd(z, g, d_list, z_off, ch, name):
    t_len = z.shape[0]
    tm = _row_tile(t_len, 1024)
    n_col = ch // LANES
    nd = len(d_list)

    def body(*refs):
        z_ref, g_ref, d_refs = refs[0], refs[1], refs[2:2 + nd]
        dz_ref, dg_ref = refs[2 + nd], refs[3 + nd]
        first = jnp.logical_and(pl.program_id(0) == 0, pl.program_id(1) == 0)
        low, _ = _head_masks(tm)
        xv = z_ref[...]
        r = lax.rsqrt(_per_head_mean(xv * xv, low) + EPS)
        xh = xv * r
        dy = d_refs[0][...]
        for dr in d_refs[1:]:
            dy = dy + dr[...]

        @pl.when(first)
        def _():
            dg_ref[...] = jnp.zeros_like(dg_ref)

        dg_ref[...] += jnp.sum(dy * xh, axis=0, keepdims=True)
        dxh = dy * g_ref[...]
        dz_ref[...] = (r * (dxh - xh * _per_head_mean(dxh * xh, low))).astype(BF16)

    blk = pl.BlockSpec((tm, LANES), lambda i, cb: (i, cb))
    return pl.pallas_call(
        body, name=name, grid=(t_len // tm, n_col),
        in_specs=[pl.BlockSpec((tm, LANES), lambda i, cb: (i, z_off + cb)),
                  pl.BlockSpec((1, LANES), lambda i, cb: (0, 0))] + [blk] * nd,
        out_specs=[blk, pl.BlockSpec((1, LANES), lambda i, cb: (0, 0))],
        out_shape=[_sds((t_len, ch), BF16), _sds((1, LANES), F32)],
        compiler_params=_params("arbitrary", "arbitrary"),
    )(z, g, *d_list)


def _alibi_bias(n_heads, dilation):
    slopes = 2.0 ** (-ALIBI_MAX_BIAS * jnp.arange(1, n_heads + 1, dtype=F32) / n_heads)
    qi = jnp.arange(ATT_BLOCK)[:, None]
    kj = jnp.arange(ATT_BLOCK)[None, :]
    dist_cur = (qi - kj).astype(F32)
    dist_prev = (ATT_BLOCK + qi - kj).astype(F32)
    cur = jnp.where((qi >= kj)[None], -slopes[:, None, None] * (dilation * dist_cur)[None], MASKED)
    prev = jnp.where((kj >= qi)[None], -slopes[:, None, None] * (dilation * dist_prev)[None], MASKED)
    return prev.astype(F32), cur.astype(F32)


def _scores(qm, kb, bias, scale):
    return lax.dot_general(qm, kb, _NT, preferred_element_type=F32) * scale + bias


def attn_fwd(qk, z, dilation, ch, name):
    t_len = qk.shape[0]
    n_sub = t_len // dilation
    nb = n_sub // ATT_BLOCK
    pairs = ch // LANES
    qk_w, z_w = 2 * pairs, 5 * pairs
    scale = 1.0 / math.sqrt(HEAD_DIM)
    bias_prev, bias_cur = _alibi_bias(2 * pairs, dilation)
    qkv = qk.reshape(n_sub, dilation * 2 * ch)
    zv = z.reshape(n_sub, dilation * 5 * ch)

    def body(q_ref, kc_ref, kp_ref, vc_ref, vp_ref, bp_ref, bc_ref, o_ref, l_ref):
        n = pl.program_id(2)
        low, high = _head_masks(ATT_BLOCK)
        qv = q_ref[...]
        kc, kp = kc_ref[...].astype(BF16), kp_ref[...].astype(BF16)
        vc, vp = vc_ref[...].astype(BF16), vp_ref[...].astype(BF16)
        outs, lses = [], []
        for hh, mask in enumerate((low, high)):
            qm = jnp.where(mask, qv, 0.0).astype(BF16)
            s_c = _scores(qm, kc, bc_ref[hh], scale)
            s_p = jnp.where(n > 0, _scores(qm, kp, bp_ref[hh], scale), MASKED)
            mx = jnp.maximum(jnp.max(s_c, axis=-1, keepdims=True), jnp.max(s_p, axis=-1, keepdims=True))
            e_c, e_p = jnp.exp(s_c - mx), jnp.exp(s_p - mx)
            den = jnp.sum(e_c, axis=-1, keepdims=True) + jnp.sum(e_p, axis=-1, keepdims=True)
            acc = (lax.dot_general(e_c.astype(BF16), vc, _NN, preferred_element_type=F32)
                   + lax.dot_general(e_p.astype(BF16), vp, _NN, preferred_element_type=F32))
            outs.append(acc / den)
            lses.append(mx + jnp.log(den))
        o_ref[...] = jnp.where(low, outs[0], outs[1])
        l_ref[...] = jnp.where(low, lses[0], lses[1])

    blk = (ATT_BLOCK, LANES)
    prev = lambda n: jnp.maximum(n - 1, 0)
    bias_spec = pl.BlockSpec((2, ATT_BLOCK, ATT_BLOCK), lambda r, p, n: (p, 0, 0))
    out_spec = pl.BlockSpec(blk, lambda r, p, n: (n, r * pairs + p))
    o, lse = pl.pallas_call(
        body, name=name, grid=(dilation, pairs, nb),
        in_specs=[pl.BlockSpec(blk, lambda r, p, n: (n, r * qk_w + p)),
                  pl.BlockSpec(blk, lambda r, p, n: (n, r * qk_w + pairs + p)),
                  pl.BlockSpec(blk, lambda r, p, n: (prev(n), r * qk_w + pairs + p)),
                  pl.BlockSpec(blk, lambda r, p, n: (n, r * z_w + 4 * pairs + p)),
                  pl.BlockSpec(blk, lambda r, p, n: (prev(n), r * z_w + 4 * pairs + p)),
                  bias_spec, bias_spec],
        out_specs=[out_spec, out_spec],
        out_shape=[_sds((n_sub, dilation * ch), F32)] * 2,
        compiler_params=_params("parallel", "parallel", "arbitrary"),
    )(qkv, qkv, qkv, zv, zv, bias_prev, bias_cur)
    return o.reshape(t_len, ch), lse.reshape(t_len, ch)


def attn_combine(outs, lses, name):
    t_len, ch = outs[0].shape
    tm = _row_tile(t_len, 512)

    def body(o1, o2, o3, l1, l2, l3, out_ref, outb_ref, lg_ref):
        a, b, c = l1[...], l2[...], l3[...]
        mx = jnp.maximum(jnp.maximum(a, b), c)
        tot = mx + jnp.log(jnp.exp(a - mx) + jnp.exp(b - mx) + jnp.exp(c - mx))
        val = jnp.exp(a - tot) * o1[...] + jnp.exp(b - tot) * o2[...] + jnp.exp(c - tot) * o3[...]
        out_ref[...] = val
        outb_ref[...] = val.astype(BF16)
        lg_ref[...] = tot

    row = pl.BlockSpec((tm, ch), lambda i: (i, 0))
    return pl.pallas_call(
        body, name=name, grid=(t_len // tm,),
        in_specs=[row] * 6, out_specs=[row] * 3,
        out_shape=[_sds((t_len, ch), F32), _sds((t_len, ch), BF16), _sds((t_len, ch), F32)],
        compiler_params=_params("parallel"),
    )(*outs, *lses)


def attn_bwd(qk, z, dy_cat, out, lg, dilation, ch, name):
    t_len = qk.shape[0]
    n_sub = t_len // dilation
    nb = n_sub // ATT_BLOCK
    pairs = ch // LANES
    qk_w, z_w = 2 * pairs, 5 * pairs
    scale = 1.0 / math.sqrt(HEAD_DIM)
    bias_prev, bias_cur = _alibi_bias(2 * pairs, dilation)
    qkv = qk.reshape(n_sub, dilation * 2 * ch)
    zv = z.reshape(n_sub, dilation * 5 * ch)
    dyv = dy_cat.reshape(n_sub, dilation * 2 * ch)
    outv = out.reshape(n_sub, dilation * ch)
    lgv = lg.reshape(n_sub, dilation * ch)

    def body(q_ref, qn_ref, kc_ref, kp_ref, vc_ref, vp_ref, do_ref, don_ref, o_ref, on_ref,
             l_ref, ln_ref, bp_ref, bc_ref, dq_ref, dk_ref, dv_ref):
        n = pl.program_id(2)
        low, high = _head_masks(ATT_BLOCK)
        has_prev = n > 0
        has_next = n < nb - 1
        qv, qnv = q_ref[...], qn_ref[...]
        kc, kp = kc_ref[...].astype(BF16), kp_ref[...].astype(BF16)
        vc, vp = vc_ref[...].astype(BF16), vp_ref[...].astype(BF16)
        dov, donv = do_ref[...], don_ref[...]
        prod, prodn = dov * o_ref[...], donv * on_ref[...]
        lv, lnv = l_ref[...], ln_ref[...]
        dqs, dks, dvs = [], [], []
        for hh, mask in enumerate((low, high)):
            def head_rows(val):
                return jnp.max(jnp.where(mask, val, MASKED), axis=-1, keepdims=True)

            def head_sum(val):
                return jnp.sum(jnp.where(mask, val, 0.0), axis=-1, keepdims=True)

            qm = jnp.where(mask, qv, 0.0).astype(BF16)
            qnm = jnp.where(mask, qnv, 0.0).astype(BF16)
            dom = jnp.where(mask, dov, 0.0).astype(BF16)
            donm = jnp.where(mask, donv, 0.0).astype(BF16)
            lse, lse_n = head_rows(lv), head_rows(lnv)
            delta, delta_n = head_sum(prod), head_sum(prodn)

            p_cc = jnp.exp(_scores(qm, kc, bc_ref[hh], scale) - lse)
            dp = lax.dot_general(dom, vc, _NT, preferred_element_type=F32)
            ds_cc = (p_cc * (dp - delta)).astype(BF16)
            p_cp = jnp.where(has_prev, jnp.exp(_scores(qm, kp, bp_ref[hh], scale) - lse), 0.0)
            dp = lax.dot_general(dom, vp, _NT, preferred_element_type=F32)
            ds_cp = (p_cp * (dp - delta)).astype(BF16)
            p_nc = jnp.where(has_next, jnp.exp(_scores(qnm, kc, bp_ref[hh], scale) - lse_n), 0.0)
            dp = lax.dot_general(donm, vc, _NT, preferred_element_type=F32)
            ds_nc = (p_nc * (dp - delta_n)).astype(BF16)

            dqs.append(scale * (lax.dot_general(ds_cc, kc, _NN, preferred_element_type=F32)
                                + lax.dot_general(ds_cp, kp, _NN, preferred_element_type=F32)))
            dks.append(scale * (lax.dot_general(ds_cc, qm, _TN, preferred_element_type=F32)
                                + lax.dot_general(ds_nc, qnm, _TN, preferred_element_type=F32)))
            dvs.append(lax.dot_general(p_cc.astype(BF16), dom, _TN, preferred_element_type=F32)
                       + lax.dot_general(p_nc.astype(BF16), donm, _TN, preferred_element_type=F32))
        dq_ref[...] = jnp.where(low, dqs[0], dqs[1])
        dk_ref[...] = jnp.where(low, dks[0], dks[1])
        dv_ref[...] = jnp.where(low, dvs[0], dvs[1])

    blk = (ATT_BLOCK, LANES)
    prev = lambda n: jnp.maximum(n - 1, 0)
    nxt = lambda n: jnp.minimum(n + 1, nb - 1)
    spec = lambda row_fn, width, off: pl.BlockSpec(blk, lambda r, p, n: (row_fn(n), r * width + off + p))
    same = lambda n: n
    bias_spec = pl.BlockSpec((2, ATT_BLOCK, ATT_BLOCK), lambda r, p, n: (p, 0, 0))
    out_spec = spec(same, pairs, 0)
    dq, dk, dv = pl.pallas_call(
        body, name=name, grid=(dilation, pairs, nb),
        in_specs=[spec(same, qk_w, 0), spec(nxt, qk_w, 0),
                  spec(same, qk_w, pairs), spec(prev, qk_w, pairs),
                  spec(same, z_w, 4 * pairs), spec(prev, z_w, 4 * pairs),
                  spec(same, qk_w, pairs), spec(nxt, qk_w, pairs),
                  spec(same, pairs, 0), spec(nxt, pairs, 0),
                  spec(same, pairs, 0), spec(nxt, pairs, 0),
                  bias_spec, bias_spec],
        out_specs=[out_spec] * 3,
        out_shape=[_sds((n_sub, dilation * ch), F32)] * 3,
        compiler_params=_params("parallel", "parallel", "arbitrary"),
    )(qkv, qkv, qkv, qkv, zv, zv, dyv, dyv, outv, outv, lgv, lgv, bias_prev, bias_cur)
    return dq.reshape(t_len, ch), dk.reshape(t_len, ch), dv.reshape(t_len, ch)


def sum3_bf16(a, b, c, name):
    t_len, ch = a.shape
    tm = _row_tile(t_len, 512)

    def body(a_ref, b_ref, c_ref, o_ref):
        o_ref[...] = (a_ref[...] + b_ref[...] + c_ref[...]).astype(BF16)

    row = pl.BlockSpec((tm, ch), lambda i: (i, 0))
    return pl.pallas_call(
        body, name=name, grid=(t_len // tm,), in_specs=[row] * 3, out_specs=row,
        out_shape=_sds((t_len, ch), BF16), compiler_params=_params("parallel"),
    )(a, b, c)


def _blk3(rows, cols):
    return pl.BlockSpec((None, rows, cols), lambda j, t: (j, 0, 0))


def ffn_up(h, wgu, name):
    t_len, d = h.shape
    n_blk, _, _, fj = wgu.shape

    def epilogue(accs, e_refs, o_refs):
        gate, up = accs
        o_refs[0][...] = gate.astype(BF16)
        o_refs[1][...] = up.astype(BF16)
        o_refs[2][...] = (gate * _sigmoid(gate) * up).astype(BF16)

    w_spec = lambda i: pl.BlockSpec((None, None, d, fj), lambda j, t: (j, i, 0, 0))
    act = lambda tm: pl.BlockSpec((None, tm, fj), lambda j, t: (j, t, 0))
    return mm_cols(name, h, [wgu, wgu], [w_spec(0), w_spec(1)], False, [], lambda tm: [],
                   [_sds((n_blk, t_len, fj), BF16)] * 3, lambda tm: [act(tm)] * 3, epilogue, n_blk)


def ffn_down(act, wd, res, name):
    n_blk, t_len, fj = act.shape
    d = wd.shape[2]
    return mm_reduce(name, [act], lambda tm: [pl.BlockSpec((None, tm, fj), lambda t, j: (j, t, 0))],
                     [wd], [pl.BlockSpec((None, fj, d), lambda t, j: (j, 0, 0))], False, res, 0.5,
                     t_len, d, n_blk)


def ffn_bwd(h, gate, up, act, wgu, wd, dyb, name):
    t_len, d = h.shape
    n_blk, _, fj = act.shape

    def epilogue(accs, e_refs, o_refs):
        d_act = 0.5 * accs[0]
        gv, uv = e_refs[0][...].astype(F32), e_refs[1][...].astype(F32)
        sg = _sigmoid(gv)
        o_refs[0][...] = (d_act * uv * (sg * (1.0 + gv * (1.0 - sg)))).astype(BF16)
        o_refs[1][...] = (d_act * gv * sg).astype(BF16)

    act_jt = lambda tm: pl.BlockSpec((None, tm, fj), lambda j, t: (j, t, 0))
    d_gate, d_up = mm_cols(name + "_dact", dyb, [wd], [_blk3(fj, d)], True, [gate, up],
                           lambda tm: [act_jt(tm)] * 2, [_sds((n_blk, t_len, fj), BF16)] * 2,
                           lambda tm: [act_jt(tm)] * 2, epilogue, n_blk)

    (d_wd,) = mm_tn(name + "_dwd", act, act_jt, [dyb],
                    lambda tt: [pl.BlockSpec((tt, d), lambda j, t: (t, 0))],
                    [_sds((n_blk, fj, d), F32)], [_blk3(fj, d)], 0.5, t_len, n_blk)

    w_out = lambda i: pl.BlockSpec((None, None, d, fj), lambda j, t: (j, i, 0, 0))
    d_wg, d_wu = mm_tn(name + "_dwgu", h, lambda tt: pl.BlockSpec((tt, d), lambda j, t: (t, 0)),
                       [d_gate, d_up], lambda tt: [act_jt(tt)] * 2,
                       [_sds((n_blk, 1, d, fj), F32)] * 2, [w_out(0), w_out(0)], 1.0, t_len, n_blk)
    d_wgu = jnp.concatenate([d_wg, d_wu], axis=1)

    act_tj = lambda tm: pl.BlockSpec((None, tm, fj), lambda t, j: (j, t, 0))
    w_in = lambda i: pl.BlockSpec((None, None, d, fj), lambda t, j: (j, i, 0, 0))
    dh = mm_reduce(name + "_dh", [d_gate, d_up], lambda tm: [act_tj(tm)] * 2,
                   [wgu, wgu], [w_in(0), w_in(1)], True, None, 1.0, t_len, d, n_blk)
    return dh, d_wgu, d_wd


def local_step(x, target, g1, wgu1, wd1, gmix, win, conv_w, conv_b, ln_g, ln_b, gq, gk, wout, g3, wgu2, wd2):
    t_len, d = x.shape
    ch = d // 2
    n_blk, _, ij = win.shape
    oj = wout.shape[1]

    h1 = rms_fwd(x, g1, "rms1")
    gate1, up1, act1 = ffn_up(h1, wgu1, "ffn1_up")
    x1 = ffn_down(act1, wd1, x, "ffn1_down")

    h2 = rms_fwd(x1, gmix, "rms_mix")

    def store_f32(accs, e_refs, o_refs):
        o_refs[0][...] = accs[0]

    (z,) = mm_cols("w_in", h2, [win], [_blk3(d, ij)], False, [], lambda tm: [],
                   [_sds((t_len, n_blk * ij), F32)],
                   lambda tm: [pl.BlockSpec((tm, ij), lambda j, t: (t, j))], store_f32, n_blk)

    conv_w32 = jnp.pad(conv_w, ((0, 32 - CONV_WIDTH), (0, 0)))
    y_conv = conv_fwd(z, conv_w32, conv_b, ln_g, ln_b, "conv_fwd")

    g2 = jnp.concatenate([jnp.tile(gq, (1, ch // HEAD_DIM)), jnp.tile(gk, (1, ch // HEAD_DIM))], axis=1)
    qk = qk_norm_fwd(z, g2, ch, "qk_norm")
    branch = [attn_fwd(qk, z, dil, ch, "attn_fwd_d%d" % dil) for dil in DILATIONS]
    att, att_b, lg = attn_combine([o for o, _ in branch], [l for _, l in branch], "attn_combine")

    y_cat = jnp.concatenate([y_conv, att_b], axis=1)
    x2 = mm_reduce("w_out", [y_cat], lambda tm: [pl.BlockSpec((tm, oj), lambda t, j: (t, j))],
                   [wout], [pl.BlockSpec((None, oj, d), lambda t, j: (j, 0, 0))], False, x1, 1.0,
                   t_len, d, n_blk)

    h3 = rms_fwd(x2, g3, "rms3")
    gate2, up2, act2 = ffn_up(h3, wgu2, "ffn2_up")
    y = ffn_down(act2, wd2, x2, "ffn2_down")

    loss_tile, dy, dyb = loss_head(y, target, "loss")

    dh3, d_wgu2, d_wd2 = ffn_bwd(h3, gate2, up2, act2, wgu2, wd2, dyb, "ffn2")
    dx2, dx2b, d_g3 = rms_bwd(x2, g3, dh3, dy, "rms3_bwd")

    (dy_cat,) = mm_cols("w_out_dy", dx2b, [wout], [_blk3(oj, d)], True, [], lambda tm: [],
                        [_sds((t_len, n_blk * oj), F32)],
                        lambda tm: [pl.BlockSpec((tm, oj), lambda j, t: (t, j))], store_f32, n_blk)
    (d_wout,) = mm_tn("w_out_dw", y_cat, lambda tt: pl.BlockSpec((tt, oj), lambda j, t: (t, j)),
                      [dx2b], lambda tt: [pl.BlockSpec((tt, d), lambda j, t: (t, 0))],
                      [_sds((n_blk, oj, d), F32)], [_blk3(oj, d)], 1.0, t_len, n_blk)

    dc, d_lg, d_lb, d_cb = conv_bwd_norm(z, dy_cat, conv_w32, conv_b, ln_g, ln_b, "conv_bwd_norm")
    dz_a, dz_g, d_cw8 = conv_bwd_taps(z, dc, conv_w32, "conv_bwd_taps")
    d_cw = jnp.sum(d_cw8, axis=1)[:CONV_WIDTH]

    grads = [attn_bwd(qk, z, dy_cat, att, lg, dil, ch, "attn_bwd_d%d" % dil) for dil in DILATIONS]
    gq_t = jnp.tile(gq, (1, LANES // HEAD_DIM))
    gk_t = jnp.tile(gk, (1, LANES // HEAD_DIM))
    dz_q, d_gq2 = qk_norm_bwd(z, gq_t, [g[0] for g in grads], 2 * ch // LANES, ch, "q_norm_bwd")
    dz_k, d_gk2 = qk_norm_bwd(z, gk_t, [g[1] for g in grads], 3 * ch // LANES, ch, "k_norm_bwd")
    d_gq = d_gq2[:, :HEAD_DIM] + d_gq2[:, HEAD_DIM:]
    d_gk = d_gk2[:, :HEAD_DIM] + d_gk2[:, HEAD_DIM:]
    dz_v = sum3_bf16(grads[0][2], grads[1][2], grads[2][2], "dv_sum")
    dzb = jnp.concatenate([dz_a, dz_g, dz_q, dz_k, dz_v], axis=1)

    (d_win,) = mm_tn("w_in_dw", h2, lambda tt: pl.BlockSpec((tt, d), lambda j, t: (t, 0)),
                     [dzb], lambda tt: [pl.BlockSpec((tt, ij), lambda j, t: (t, j))],
                     [_sds((n_blk, d, ij), F32)], [_blk3(d, ij)], 1.0, t_len, n_blk)
    dh2 = mm_reduce("w_in_dh", [dzb], lambda tm: [pl.BlockSpec((tm, ij), lambda t, j: (t, j))],
                    [win], [pl.BlockSpec((None, d, ij), lambda t, j: (j, 0, 0))], True, None, 1.0,
                    t_len, d, n_blk)
    dx1, dx1b, d_gmix = rms_bwd(x1, gmix, dh2, dx2, "rms_mix_bwd")

    dh1, d_wgu1, d_wd1 = ffn_bwd(h1, gate1, up1, act1, wgu1, wd1, dx1b, "ffn1")
    grad_x, _, d_g1 = rms_bwd(x, g1, dh1, dx1, "rms1_bwd")

    big = dict(wgu1=d_wgu1, wd1=d_wd1, win=d_win, wout=d_wout, wgu2=d_wgu2, wd2=d_wd2)
    small = dict(g1=d_g1, gmix=d_gmix, g3=d_g3, conv_b=d_cb, ln_g=d_lg, ln_b=d_lb, gq=d_gq, gk=d_gk, conv_w=d_cw)
    return loss_tile[0, 0], grad_x, big, small


SMALL_ROWS = 48


def _pack_small(ch, g1, gmix, g3, conv_b, ln_g, ln_b, gq, gk, conv_w):
    pad_head = lambda v: jnp.pad(v, ((0, 0), (0, ch - v.shape[1])))
    rows = [g1.reshape(2, ch), gmix.reshape(2, ch), g3.reshape(2, ch), conv_b, ln_g, ln_b,
            pad_head(gq), pad_head(gk), conv_w]
    packed = jnp.concatenate(rows, axis=0)
    return jnp.pad(packed, ((0, SMALL_ROWS - packed.shape[0]), (0, 0)))


def _unpack_small(packed, d):
    return dict(g1=packed[0:2].reshape(1, d), gmix=packed[2:4].reshape(1, d), g3=packed[4:6].reshape(1, d),
                conv_b=packed[6:7], ln_g=packed[7:8], ln_b=packed[8:9],
                gq=packed[9:10, :HEAD_DIM], gk=packed[10:11, :HEAD_DIM])


def kernel(x, ffn1_norm_g, ffn1_w_gate, ffn1_w_up, ffn1_w_down, mix_norm_g, w_in, conv_w_dw, conv_b_dw, conv_ln_g, conv_ln_b, q_norm_g, k_norm_g, w_out, ffn2_norm_g, ffn2_w_gate, ffn2_w_up, ffn2_w_down, loss_target, m_ffn1_norm_g, m_ffn1_w_gate, m_ffn1_w_up, m_ffn1_w_down, m_mix_norm_g, m_w_in, m_conv_w_dw, m_conv_b_dw, m_conv_ln_g, m_conv_ln_b, m_q_norm_g, m_k_norm_g, m_w_out, m_ffn2_norm_g, m_ffn2_w_gate, m_ffn2_w_up, m_ffn2_w_down, v_ffn1_norm_g, v_ffn1_w_gate, v_ffn1_w_up, v_ffn1_w_down, v_mix_norm_g, v_w_in, v_conv_w_dw, v_conv_b_dw, v_conv_ln_g, v_conv_ln_b, v_q_norm_g, v_k_norm_g, v_w_out, v_ffn2_norm_g, v_ffn2_w_gate, v_ffn2_w_up, v_ffn2_w_down):
    d = x.shape[-1]
    ch = d // 2
    me = 4 * lax.axis_index("x") + 2 * lax.axis_index("y") + lax.axis_index("c")

    gu = lambda wg, wu: jnp.stack([wg[0], wu[0]]).astype(BF16)
    wgu1 = all_gather(gu(ffn1_w_gate, ffn1_w_up), "ag_wgu1")
    wd1 = all_gather(ffn1_w_down[0].astype(BF16), "ag_wd1")
    win = all_gather(w_in[0].astype(BF16), "ag_win")
    wout = all_gather(w_out[0].astype(BF16), "ag_wout")
    wgu2 = all_gather(gu(ffn2_w_gate, ffn2_w_up), "ag_wgu2")
    wd2 = all_gather(ffn2_w_down[0].astype(BF16), "ag_wd2")
    cw_all = all_gather(conv_w_dw[0], "ag_convw")
    conv_w = jnp.transpose(cw_all, (1, 0, 2)).reshape(CONV_WIDTH, ch)

    loss_part, grad_x, big, small = local_step(
        x[0], loss_target[0], ffn1_norm_g, wgu1, wd1, mix_norm_g, win, conv_w, conv_b_dw, conv_ln_g,
        conv_ln_b, q_norm_g, k_norm_g, wout, ffn2_norm_g, wgu2, wd2)
    loss = lax.psum(loss_part, MESH_AXES)

    def update(name, part, w, m, v):
        rows, cols = w.shape
        recv = exchange_partials(part.reshape(N_DEV, rows, cols), "rs_" + name)
        return adamw(w, m, v, recv, "adamw_" + name)

    def stacked(a, b):
        return jnp.concatenate([a[0], b[0]], axis=0)

    out = {}
    for tag, wg, wu, wdn, mg, mu, mdn, vg, vu, vdn in (
            ("1", ffn1_w_gate, ffn1_w_up, ffn1_w_down, m_ffn1_w_gate, m_ffn1_w_up, m_ffn1_w_down,
             v_ffn1_w_gate, v_ffn1_w_up, v_ffn1_w_down),
            ("2", ffn2_w_gate, ffn2_w_up, ffn2_w_down, m_ffn2_w_gate, m_ffn2_w_up, m_ffn2_w_down,
             v_ffn2_w_gate, v_ffn2_w_up, v_ffn2_w_down)):
        res = update("wgu" + tag, big["wgu" + tag], stacked(wg, wu), stacked(mg, mu), stacked(vg, vu))
        out["ffn%s_w_gate" % tag] = [r[:d][None] for r in res]
        out["ffn%s_w_up" % tag] = [r[d:][None] for r in res]
        res = update("wd" + tag, big["wd" + tag], wdn[0], mdn[0], vdn[0])
        out["ffn%s_w_down" % tag] = [r[None] for r in res]
    out["w_in"] = [r[None] for r in update("win", big["win"], w_in[0], m_w_in[0], v_w_in[0])]
    out["w_out"] = [r[None] for r in update("wout", big["wout"], w_out[0], m_w_out[0], v_w_out[0])]

    zero_taps = jnp.zeros((CONV_WIDTH, ch), F32)
    pack = lambda g1, gm, g3, cb, lg, lb, gq, gk: _pack_small(ch, g1, gm, g3, cb, lg, lb, gq, gk, zero_taps)
    small_parts = all_gather(_pack_small(ch, **small), "ag_small_grads")
    s_res = adamw(
        pack(ffn1_norm_g, mix_norm_g, ffn2_norm_g, conv_b_dw, conv_ln_g, conv_ln_b, q_norm_g, k_norm_g),
        pack(m_ffn1_norm_g, m_mix_norm_g, m_ffn2_norm_g, m_conv_b_dw, m_conv_ln_g, m_conv_ln_b, m_q_norm_g, m_k_norm_g),
        pack(v_ffn1_norm_g, v_mix_norm_g, v_ffn2_norm_g, v_conv_b_dw, v_conv_ln_g, v_conv_ln_b, v_q_norm_g, v_k_norm_g),
        small_parts, "adamw_small")
    s_out = [_unpack_small(r, d) for r in s_res]
    names = dict(g1="ffn1_norm_g", gmix="mix_norm_g", g3="ffn2_norm_g", conv_b="conv_b_dw", ln_g="conv_ln_g",
                 ln_b="conv_ln_b", gq="q_norm_g", gk="k_norm_g")
    for key, full in names.items():
        out[full] = [r[key] for r in s_out]

    cshard = ch // N_DEV
    taps_sum = s_res[0][11:11 + CONV_WIDTH]
    taps_mine = lax.dynamic_slice(taps_sum, (0, me * cshard), (CONV_WIDTH, cshard))
    pad_taps = lambda a: jnp.pad(a, ((0, 32 - CONV_WIDTH), (0, 0)))
    c_res = adamw(pad_taps(conv_w_dw[0]), pad_taps(m_conv_w_dw[0]), pad_taps(v_conv_w_dw[0]),
                  pad_taps(taps_mine)[None], "adamw_convw")
    out["conv_w_dw"] = [r[:CONV_WIDTH][None] for r in c_res]

    order = ["ffn1_norm_g", "ffn1_w_gate", "ffn1_w_up", "ffn1_w_down", "mix_norm_g", "w_in", "conv_w_dw",
             "conv_b_dw", "conv_ln_g", "conv_ln_b", "q_norm_g", "k_norm_g", "w_out", "ffn2_norm_g",
             "ffn2_w_gate", "ffn2_w_up", "ffn2_w_down"]
    result = [loss, grad_x[None]]
    for kind in range(4):
        result += [out[n][kind] for n in order]
    return tuple(result)
```

```python
import math
from typing import NamedTuple

import jax
import jax.numpy as jnp
from jax import lax
from jax.experimental import pallas as pl
from jax.experimental.pallas import tpu as pltpu

F32 = jnp.float32
BF16 = jnp.bfloat16

N_DEV = 8
EPS = 1e-6
HEAD_DIM = 64
LANES = 128
CONV_WIDTH = 31
HALO = 32
ROW_CHUNK = 32
ATT_BLOCK = 128
DILATIONS = (1, 4, 16)
ALIBI_MAX_BIAS = 8.0
MASKED = -1e30
VMEM_LIMIT = 56 * 1024 * 1024

ADAM_LR = 0.001
ADAM_B1 = 0.9
ADAM_B2 = 0.999
ADAM_EPS = 1e-08
ADAM_WD = 0.01
ADAM_STEP = 10

MESH_AXES = ("x", "y", "c")
ANY = pl.BlockSpec(memory_space=pl.ANY)


def _sds(shape, dtype):
    return jax.ShapeDtypeStruct(tuple(shape), dtype)


def _params(*sem):
    return pltpu.CompilerParams(dimension_semantics=sem, vmem_limit_bytes=VMEM_LIMIT)


def _sigmoid(v):
    return 1.0 / (1.0 + jnp.exp(-v))


def _row_tile(t, want):
    for cand in range(min(want, t) // 8 * 8, 0, -8):
        if t % cand == 0:
            return cand
    return t


def _mesh_pos():
    return lax.axis_index("x"), lax.axis_index("y"), lax.axis_index("c")


def _comm_sems():
    return [pltpu.SemaphoreType.DMA((7,)), pltpu.SemaphoreType.DMA((7,)), pltpu.SemaphoreType.DMA(())]


def _gather_phases(x_ref, out_ref, send_sems, recv_sems, local_sem):
    x, y, c = _mesh_pos()
    me, sibling = (x, y, c), (x, y, 1 - c)
    chips = [(1 - x, y), (x, 1 - y), (1 - x, 1 - y)]

    def slot(px, py, pc):
        return out_ref.at[4 * px + 2 * py + pc]

    def copy(k, block, to, src=None):
        return pltpu.make_async_remote_copy(
            src_ref=slot(*block) if src is None else src, dst_ref=slot(*block),
            send_sem=send_sems.at[k], recv_sem=recv_sems.at[k],
            device_id=to, device_id_type=pl.DeviceIdType.MESH)

    mine = pltpu.make_async_copy(x_ref, slot(*me), local_sem)
    first = [copy(0, me, sibling, src=x_ref)]
    first += [copy(1 + j, me, (*chip, c), src=x_ref) for j, chip in enumerate(chips)]
    passed = [copy(4 + j, (*chip, c), sibling) for j, chip in enumerate(chips)]

    def start():
        mine.start()
        for cp in first:
            cp.start()

    def forward():
        for j, chip in enumerate(chips):
            copy(1 + j, (*chip, c), me).wait_recv()
            passed[j].start()

    def finish():
        copy(0, sibling, me).wait_recv()
        for j, chip in enumerate(chips):
            copy(4 + j, (*chip, 1 - c), me).wait_recv()
        for cp in first + passed:
            cp.wait_send()
        mine.wait()

    return start, forward, finish


def _scatter_phases(p_ref, out_ref, send_sems, recv_sems, local_sem):
    x, y, c = _mesh_pos()
    me = 4 * x + 2 * y + c
    flips = [(fx, fy, fc) for fx in (0, 1) for fy in (0, 1) for fc in (0, 1)][1:]

    def copy(k, flip, receiving):
        px, py, pc = (1 - x if flip[0] else x, 1 - y if flip[1] else y, 1 - c if flip[2] else c)
        them = 4 * px + 2 * py + pc
        return pltpu.make_async_remote_copy(
            src_ref=p_ref.at[them], dst_ref=out_ref.at[them if receiving else me],
            send_sem=send_sems.at[k], recv_sem=recv_sems.at[k],
            device_id=(px, py, pc), device_id_type=pl.DeviceIdType.MESH)

    mine = pltpu.make_async_copy(p_ref.at[me], out_ref.at[me], local_sem)

    def start():
        mine.start()
        for k, flip in enumerate(flips):
            copy(k, flip, False).start()

    def finish():
        for k, flip in enumerate(flips):
            copy(k, flip, True).wait_recv()
            copy(k, flip, False).wait_send()
        mine.wait()

    return start, None, finish


class Rider(NamedTuple):
    kind: str
    src: jax.Array

    def out_shape(self):
        shape = (N_DEV,) + self.src.shape if self.kind == "gather" else self.src.shape
        return _sds(shape, self.src.dtype)


def _rider_hooks(riders, in_refs, out_refs, sem_refs, step, n_steps):
    phases = [(_gather_phases if r.kind == "gather" else _scatter_phases)(
                  in_refs[i], out_refs[i], *sem_refs[3 * i:3 * i + 3]) for i, r in enumerate(riders)]

    def begin():
        for start, forward, _ in phases:
            pl.when(step == 0)(start)
            if forward is not None:
                pl.when(step == n_steps // 2)(forward)

    def end():
        for _, _, finish in phases:
            pl.when(step == n_steps - 1)(finish)

    return begin, end


def _split_refs(refs, n_in, n_out, n_scratch, n_riders):
    pos, parts = 0, []
    for n in (n_in, n_riders, n_out, n_riders, n_scratch, 3 * n_riders):
        parts.append(refs[pos:pos + n])
        pos += n
    return parts


def all_gather(shard, name):
    def body(x_ref, out_ref, send_sems, recv_sems, local_sem):
        start, forward, finish = _gather_phases(x_ref, out_ref, send_sems, recv_sems, local_sem)
        start()
        forward()
        finish()

    return pl.pallas_call(
        body, name=name, out_shape=_sds((N_DEV,) + shard.shape, shard.dtype),
        in_specs=[ANY], out_specs=ANY, scratch_shapes=_comm_sems(),
    )(shard)


def adamw(w, m, v, parts, name):
    n_parts, rows, cols = parts.shape
    tr = _row_tile(rows, 128)
    c1 = 1.0 - ADAM_B1 ** ADAM_STEP
    c2 = 1.0 - ADAM_B2 ** ADAM_STEP

    def body(w_ref, m_ref, v_ref, p_ref, g_ref, d_ref, nm_ref, nv_ref):
        g = p_ref[0].astype(F32)
        for s in range(1, n_parts):
            g = g + p_ref[s].astype(F32)
        nm = ADAM_B1 * m_ref[...] + (1.0 - ADAM_B1) * g
        nv = ADAM_B2 * v_ref[...] + (1.0 - ADAM_B2) * (g * g)
        delta = -ADAM_LR * ((nm / c1) / (jnp.sqrt(nv / c2) + ADAM_EPS) + ADAM_WD * w_ref[...])
        g_ref[...] = g
        d_ref[...] = delta
        nm_ref[...] = nm
        nv_ref[...] = nv

    mat = pl.BlockSpec((tr, cols), lambda i: (i, 0))
    return pl.pallas_call(
        body, name=name, grid=(rows // tr,),
        in_specs=[mat, mat, mat, pl.BlockSpec((n_parts, tr, cols), lambda i: (0, i, 0))],
        out_specs=[mat, mat, mat, mat],
        out_shape=[_sds((rows, cols), F32)] * 4,
        compiler_params=_params("parallel"),
    )(w, m, v, parts)


_NN = (((1,), (0,)), ((), ()))
_NT = (((1,), (1,)), ((), ()))
_TN = (((0,), (0,)), ((), ()))


def mm_cols(name, a, b_list, b_specs, nt, extras, extra_specs, out_shapes, out_specs, epilogue, n_blk, riders=()):
    t_len, k_len = a.shape
    tm = _row_tile(t_len, 512)
    nb, ne, n_out, nr = len(b_list), len(extras), len(out_shapes), len(riders)
    t_steps = t_len // tm

    def body(*refs):
        ins, r_in, outs, r_out, _, r_sem = _split_refs(refs, 1 + nb + ne, n_out, 0, nr)
        step = pl.program_id(0) * t_steps + pl.program_id(1)
        begin, end = _rider_hooks(riders, r_in, r_out, r_sem, step, n_blk * t_steps)
        begin()
        av = ins[0][...]
        accs = [lax.dot_general(av, br[...], _NT if nt else _NN, preferred_element_type=F32)
                for br in ins[1:1 + nb]]
        epilogue(accs, ins[1 + nb:], outs)
        end()

    res = pl.pallas_call(
        body, name=name, grid=(n_blk, t_steps),
        in_specs=([pl.BlockSpec((tm, k_len), lambda j, t: (t, 0))] + list(b_specs) + list(extra_specs(tm))
                  + [ANY] * nr),
        out_specs=list(out_specs(tm)) + [ANY] * nr,
        out_shape=list(out_shapes) + [r.out_shape() for r in riders],
        scratch_shapes=_comm_sems() * nr,
        compiler_params=_params("arbitrary", "arbitrary"),
    )(a, *b_list, *extras, *[r.src for r in riders])
    return res[:n_out], res[n_out:]


def mm_reduce(name, a_list, a_specs, b_list, b_specs, nt, res, scale, t_len, n_len, n_blk, riders=()):
    tm = _row_tile(t_len, 512)
    na, nr = len(a_list), len(riders)
    has_res = res is not None
    t_steps = t_len // tm

    def body(*refs):
        ins, r_in, outs, r_out, scr, r_sem = _split_refs(refs, 2 * na + has_res, 1, 1, nr)
        o_ref, acc = outs[0], scr[0]
        j = pl.program_id(1)
        step = pl.program_id(0) * n_blk + j
        begin, end = _rider_hooks(riders, r_in, r_out, r_sem, step, t_steps * n_blk)
        begin()

        @pl.when(j == 0)
        def _():
            acc[...] = jnp.zeros_like(acc)

        part = None
        for ar, br in zip(ins[:na], ins[na:2 * na]):
            d = lax.dot_general(ar[...], br[...], _NT if nt else _NN, preferred_element_type=F32)
            part = d if part is None else part + d
        acc[...] += part

        @pl.when(j == n_blk - 1)
        def _():
            val = acc[...] * scale if scale != 1.0 else acc[...]
            o_ref[...] = ins[2 * na][...] + val if has_res else val

        end()

    row = pl.BlockSpec((tm, n_len), lambda t, j: (t, 0))
    out = pl.pallas_call(
        body, name=name, grid=(t_steps, n_blk),
        in_specs=list(a_specs(tm)) + list(b_specs) + ([row] if has_res else []) + [ANY] * nr,
        out_specs=[row] + [ANY] * nr,
        out_shape=[_sds((t_len, n_len), F32)] + [r.out_shape() for r in riders],
        scratch_shapes=[pltpu.VMEM((tm, n_len), F32)] + _comm_sems() * nr,
        compiler_params=_params("arbitrary", "arbitrary"),
    )(*a_list, *b_list, *([res] if has_res else []), *[r.src for r in riders])
    return out[0], out[1:]


def mm_tn(name, x, x_spec, dy_list, dy_specs, out_shapes, out_specs, scale, t_len, n_blk, riders=()):
    tt = _row_tile(t_len, 512)
    nd, nr = len(dy_list), len(riders)
    t_steps = t_len // tt
    acc_shapes = [pltpu.VMEM(spec.block_shape[-2:], F32) for spec in out_specs]

    def body(*refs):
        ins, r_in, outs, r_out, accs, r_sem = _split_refs(refs, 1 + nd, nd, nd, nr)
        t = pl.program_id(1)
        step = pl.program_id(0) * t_steps + t
        begin, end = _rider_hooks(riders, r_in, r_out, r_sem, step, n_blk * t_steps)
        begin()
        xv = ins[0][...]
        for dr, acc in zip(ins[1:], accs):
            d = lax.dot_general(xv, dr[...], _TN, preferred_element_type=F32)

            @pl.when(t == 0)
            def _():
                acc[...] = d

            @pl.when(t > 0)
            def _():
                acc[...] += d

        @pl.when(t == t_steps - 1)
        def _():
            for acc, orf in zip(accs, outs):
                val = acc[...] * scale if scale != 1.0 else acc[...]
                orf[...] = val.astype(orf.dtype)

        end()

    res = pl.pallas_call(
        body, name=name, grid=(n_blk, t_steps),
        in_specs=[x_spec(tt)] + list(dy_specs(tt)) + [ANY] * nr,
        out_specs=list(out_specs) + [ANY] * nr,
        out_shape=list(out_shapes) + [r.out_shape() for r in riders],
        scratch_shapes=acc_shapes + _comm_sems() * nr,
        compiler_params=_params("arbitrary", "arbitrary"),
    )(x, *dy_list, *[r.src for r in riders])
    return res[:nd], res[nd:]


def rms_fwd(x, g, name):
    t_len, d = x.shape
    tm = _row_tile(t_len, 512)

    def body(x_ref, g_ref, h_ref):
        xv = x_ref[...]
        r = lax.rsqrt(jnp.mean(xv * xv, axis=-1, keepdims=True) + EPS)
        h_ref[...] = (xv * r * g_ref[...]).astype(BF16)

    row = pl.BlockSpec((tm, d), lambda i: (i, 0))
    return pl.pallas_call(
        body, name=name, grid=(t_len // tm,),
        in_specs=[row, pl.BlockSpec((1, d), lambda i: (0, 0))],
        out_specs=row, out_shape=_sds((t_len, d), BF16),
        compiler_params=_params("parallel"),
    )(x, g)


def rms_bwd(x, g, dh, dres, name):
    t_len, d = x.shape
    tm = _row_tile(t_len, 512)

    def body(x_ref, g_ref, dh_ref, dr_ref, dx_ref, dxb_ref, dg_ref):
        i = pl.program_id(0)
        xv = x_ref[...]
        r = lax.rsqrt(jnp.mean(xv * xv, axis=-1, keepdims=True) + EPS)
        xh = xv * r
        dhv = dh_ref[...]

        @pl.when(i == 0)
        def _():
            dg_ref[...] = jnp.zeros_like(dg_ref)

        dg_ref[...] += jnp.sum(dhv * xh, axis=0, keepdims=True)
        dxh = dhv * g_ref[...]
        dx = dr_ref[...] + r * (dxh - xh * jnp.mean(dxh * xh, axis=-1, keepdims=True))
        dx_ref[...] = dx
        dxb_ref[...] = dx.astype(BF16)

    row = pl.BlockSpec((tm, d), lambda i: (i, 0))
    vec = pl.BlockSpec((1, d), lambda i: (0, 0))
    return pl.pallas_call(
        body, name=name, grid=(t_len // tm,),
        in_specs=[row, vec, row, row],
        out_specs=[row, row, vec],
        out_shape=[_sds((t_len, d), F32), _sds((t_len, d), BF16), _sds((1, d), F32)],
        compiler_params=_params("arbitrary"),
    )(x, g, dh, dres)


def loss_head(y, target, name):
    t_len, d = y.shape
    tm = _row_tile(t_len, 512)

    def body(y_ref, t_ref, l_ref, dy_ref, dyb_ref):
        i = pl.program_id(0)
        err = y_ref[...] - t_ref[...]

        @pl.when(i == 0)
        def _():
            l_ref[...] = jnp.zeros_like(l_ref)

        rows = jnp.sum(err * err, axis=-1, keepdims=True) * (1.0 / d)
        l_ref[...] += 0.5 * jnp.sum(rows, axis=0, keepdims=True)
        dy = err * (1.0 / d)
        dy_ref[...] = dy
        dyb_ref[...] = dy.astype(BF16)

    row = pl.BlockSpec((tm, d), lambda i: (i, 0))
    return pl.pallas_call(
        body, name=name, grid=(t_len // tm,),
        in_specs=[row, row],
        out_specs=[pl.BlockSpec((8, LANES), lambda i: (0, 0)), row, row],
        out_shape=[_sds((8, LANES), F32), _sds((t_len, d), F32), _sds((t_len, d), BF16)],
        compiler_params=_params("arbitrary"),
    )(y, target)


def _conv_specs(tm, ch):
    per = tm // HALO
    cur = lambda cb: pl.BlockSpec((tm, ch), lambda i: (i, cb))
    prev = lambda cb: pl.BlockSpec((HALO, ch), lambda i: (jnp.maximum(i * per - 1, 0), cb))
    return [cur(0), cur(1), prev(0), prev(1)]


def _fill_glu(ext, a_ref, gt_ref, ap_ref, gp_ref, i, tm):
    vp = ap_ref[...] * _sigmoid(gp_ref[...])
    ext[0:HALO, :] = jnp.where(i > 0, vp, 0.0)
    ext[HALO:HALO + tm, :] = a_ref[...] * _sigmoid(gt_ref[...])


def _conv_rows(ext, w_ref, b_ref, r0):
    acc = jnp.broadcast_to(b_ref[...], (ROW_CHUNK, b_ref.shape[1]))
    for k in range(CONV_WIDTH):
        acc = acc + w_ref[k:k + 1, :] * ext[pl.ds(r0 + HALO - (CONV_WIDTH - 1) + k, ROW_CHUNK), :]
    return acc


def _layer_norm(yv):
    mu = jnp.mean(yv, axis=-1, keepdims=True)
    cen = yv - mu
    var = jnp.mean(cen * cen, axis=-1, keepdims=True)
    rstd = lax.rsqrt(var + EPS)
    return cen * rstd, rstd


def conv_fwd(z, w, b, lg, lb, name):
    t_len = z.shape[0]
    ch = w.shape[1]
    tm = _row_tile(t_len, 256)

    def body(a_ref, gt_ref, ap_ref, gp_ref, w_ref, b_ref, lg_ref, lb_ref, y_ref, ext):
        i = pl.program_id(0)
        _fill_glu(ext, a_ref, gt_ref, ap_ref, gp_ref, i, tm)
        for r0 in range(0, tm, ROW_CHUNK):
            xh, _ = _layer_norm(_conv_rows(ext, w_ref, b_ref, r0))
            u = xh * lg_ref[...] + lb_ref[...]
            y_ref[r0:r0 + ROW_CHUNK, :] = (u * _sigmoid(u)).astype(BF16)

    vec = pl.BlockSpec((1, ch), lambda i: (0, 0))
    return pl.pallas_call(
        body, name=name, grid=(t_len // tm,),
        in_specs=_conv_specs(tm, ch) + [pl.BlockSpec((32, ch), lambda i: (0, 0)), vec, vec, vec],
        out_specs=pl.BlockSpec((tm, ch), lambda i: (i, 0)),
        out_shape=_sds((t_len, ch), BF16),
        scratch_shapes=[pltpu.VMEM((HALO + tm, ch), F32)],
        compiler_params=_params("parallel"),
    )(z, z, z, z, w, b, lg, lb)


def conv_bwd_norm(z, dy_cat, w, b, lg, lb, name):
    t_len = z.shape[0]
    ch = w.shape[1]
    tm = _row_tile(t_len, 256)

    def body(a_ref, gt_ref, ap_ref, gp_ref, dy_ref, w_ref, b_ref, lg_ref, lb_ref,
             dc_ref, dlg_ref, dlb_ref, db_ref, ext):
        i = pl.program_id(0)
        _fill_glu(ext, a_ref, gt_ref, ap_ref, gp_ref, i, tm)

        @pl.when(i == 0)
        def _():
            dlg_ref[...] = jnp.zeros_like(dlg_ref)
            dlb_ref[...] = jnp.zeros_like(dlb_ref)
            db_ref[...] = jnp.zeros_like(db_ref)

        for r0 in range(0, tm, ROW_CHUNK):
            xh, rstd = _layer_norm(_conv_rows(ext, w_ref, b_ref, r0))
            u = xh * lg_ref[...] + lb_ref[...]
            sg = _sigmoid(u)
            du = dy_ref[r0:r0 + ROW_CHUNK, :] * (sg * (1.0 + u * (1.0 - sg)))
            dlg_ref[...] += jnp.sum(du * xh, axis=0, keepdims=True)
            dlb_ref[...] += jnp.sum(du, axis=0, keepdims=True)
            dxh = du * lg_ref[...]
            dc = rstd * (dxh - jnp.mean(dxh, axis=-1, keepdims=True)
                         - xh * jnp.mean(dxh * xh, axis=-1, keepdims=True))
            db_ref[...] += jnp.sum(dc, axis=0, keepdims=True)
            dc_ref[r0:r0 + ROW_CHUNK, :] = dc

    vec = pl.BlockSpec((1, ch), lambda i: (0, 0))
    row = pl.BlockSpec((tm, ch), lambda i: (i, 0))
    return pl.pallas_call(
        body, name=name, grid=(t_len // tm,),
        in_specs=_conv_specs(tm, ch) + [row, pl.BlockSpec((32, ch), lambda i: (0, 0)), vec, vec, vec],
        out_specs=[row, vec, vec, vec],
        out_shape=[_sds((t_len, ch), F32)] + [_sds((1, ch), F32)] * 3,
        scratch_shapes=[pltpu.VMEM((HALO + tm, ch), F32)],
        compiler_params=_params("arbitrary"),
    )(z, z, z, z, dy_cat, w, b, lg, lb)


def conv_bwd_taps(z, dc, w, name):
    t_len = z.shape[0]
    ch = w.shape[1]
    tm = _row_tile(t_len, 256)
    per = tm // HALO
    n_tiles = t_len // tm
    last_halo = t_len // HALO - 1

    def body(a_ref, gt_ref, ap_ref, gp_ref, dc_ref, dn_ref, w_ref, dz_a_ref, dz_g_ref, dw_ref, ext, dext):
        i = pl.program_id(0)
        _fill_glu(ext, a_ref, gt_ref, ap_ref, gp_ref, i, tm)
        dext[0:tm, :] = dc_ref[...]
        dext[tm:tm + HALO, :] = jnp.where(i < n_tiles - 1, dn_ref[...], 0.0)

        @pl.when(i == 0)
        def _():
            dw_ref[...] = jnp.zeros_like(dw_ref)

        for r0 in range(0, tm, ROW_CHUNK):
            dcv = dext[r0:r0 + ROW_CHUNK, :]
            dv = jnp.zeros((ROW_CHUNK, ch), F32)
            for k in range(CONV_WIDTH):
                dv = dv + w_ref[k:k + 1, :] * dext[pl.ds(r0 + (CONV_WIDTH - 1) - k, ROW_CHUNK), :]
                prod = dcv * ext[pl.ds(r0 + HALO - (CONV_WIDTH - 1) + k, ROW_CHUNK), :]
                fold = prod[0:8]
                for s in range(8, ROW_CHUNK, 8):
                    fold = fold + prod[s:s + 8]
                dw_ref[k] += fold
            av = a_ref[r0:r0 + ROW_CHUNK, :]
            sg = _sigmoid(gt_ref[r0:r0 + ROW_CHUNK, :])
            dz_a_ref[r0:r0 + ROW_CHUNK, :] = (dv * sg).astype(BF16)
            dz_g_ref[r0:r0 + ROW_CHUNK, :] = (dv * av * sg * (1.0 - sg)).astype(BF16)

    row = pl.BlockSpec((tm, ch), lambda i: (i, 0))
    nxt = pl.BlockSpec((HALO, ch), lambda i: (jnp.minimum((i + 1) * per, last_halo), 0))
    return pl.pallas_call(
        body, name=name, grid=(n_tiles,),
        in_specs=_conv_specs(tm, ch) + [row, nxt, pl.BlockSpec((32, ch), lambda i: (0, 0))],
        out_specs=[row, row, pl.BlockSpec((32, 8, ch), lambda i: (0, 0, 0))],
        out_shape=[_sds((t_len, ch), BF16), _sds((t_len, ch), BF16), _sds((32, 8, ch), F32)],
        scratch_shapes=[pltpu.VMEM((HALO + tm, ch), F32), pltpu.VMEM((tm + HALO, ch), F32)],
        compiler_params=_params("arbitrary"),
    )(z, z, z, z, dc, dc, w)


def _head_masks(rows):
    lane = lax.broadcasted_iota(jnp.int32, (rows, LANES), 1)
    low = lane < HEAD_DIM
    return low, jnp.logical_not(low)


def _per_head_mean(val, low):
    s_low = jnp.sum(jnp.where(low, val, 0.0), axis=-1, keepdims=True)
    s_high = jnp.sum(jnp.where(low, 0.0, val), axis=-1, keepdims=True)
    return jnp.where(low, s_low, s_high) * (1.0 / HEAD_DIM)


def qk_norm_fwd(z, g2, ch, name):
    t_len = z.shape[0]
    tm = _row_tile(t_len, 1024)
    n_col = 2 * ch // LANES
    z_off = 2 * ch // LANES

    def body(z_ref, g_ref, o_ref):
        low, _ = _head_masks(tm)
        xv = z_ref[...]
        r = lax.rsqrt(_per_head_mean(xv * xv, low) + EPS)
        o_ref[...] = xv * r * g_ref[...]

    return pl.pallas_call(
        body, name=name, grid=(t_len // tm, n_col),
        in_specs=[pl.BlockSpec((tm, LANES), lambda i, cb: (i, z_off + cb)),
                  pl.BlockSpec((1, LANES), lambda i, cb: (0, cb))],
        out_specs=pl.BlockSpec((tm, LANES), lambda i, cb: (i, cb)),
        out_shape=_sds((t_len, 2 * ch), F32),
        compiler_params=_params("parallel", "parallel"),
    )(z, g2)


def qk_norm_bwd(z, g, d_list, z_off, ch, name):
    t_len = z.shape[0]
    tm = _row_tile(t_len, 1024)
    n_col = ch // LANES
    nd = len(d_list)

    def body(*refs):
        z_ref, g_ref, d_refs = refs[0], refs[1], refs[2:2 + nd]
        dz_ref, dg_ref = refs[2 + nd], refs[3 + nd]
        first = jnp.logical_and(pl.program_id(0) == 0, pl.program_id(1) == 0)
        low, _ = _head_masks(tm)
        xv = z_ref[...]
        r = lax.rsqrt(_per_head_mean(xv * xv, low) + EPS)
        xh = xv * r
        dy = d_refs[0][...]
        for dr in d_refs[1:]:
            dy = dy + dr[...]

        @pl.when(first)
        def _():
            dg_ref[...] = jnp.zeros_like(dg_ref)

        dg_ref[...] += jnp.sum(dy * xh, axis=0, keepdims=True)
        dxh = dy * g_ref[...]
        dz_ref[...] = (r * (dxh - xh * _per_head_mean(dxh * xh, low))).astype(BF16)

    blk = pl.BlockSpec((tm, LANES), lambda i, cb: (i, cb))
    return pl.pallas_call(
        body, name=name, grid=(t_len // tm, n_col),
        in_specs=[pl.BlockSpec((tm, LANES), lambda i, cb: (i, z_off + cb)),
                  pl.BlockSpec((1, LANES), lambda i, cb: (0, 0))] + [blk] * nd,
        out_specs=[blk, pl.BlockSpec((1, LANES), lambda i, cb: (0, 0))],
        out_shape=[_sds((t_len, ch), BF16), _sds((1, LANES), F32)],
        compiler_params=_params("arbitrary", "arbitrary"),
    )(z, g, *d_list)


def _alibi_bias(n_heads, dilation):
    slopes = 2.0 ** (-ALIBI_MAX_BIAS * jnp.arange(1, n_heads + 1, dtype=F32) / n_heads)
    qi = jnp.arange(ATT_BLOCK)[:, None]
    kj = jnp.arange(ATT_BLOCK)[None, :]
    dist_cur = (qi - kj).astype(F32)
    dist_prev = (ATT_BLOCK + qi - kj).astype(F32)
    cur = jnp.where((qi >= kj)[None], -slopes[:, None, None] * (dilation * dist_cur)[None], MASKED)
    prev = jnp.where((kj >= qi)[None], -slopes[:, None, None] * (dilation * dist_prev)[None], MASKED)
    return prev.astype(F32), cur.astype(F32)


def _scores(qm, kb, bias, scale):
    return lax.dot_general(qm, kb, _NT, preferred_element_type=F32) * scale + bias


def attn_fwd(qk, z, dilation, ch, name):
    t_len = qk.shape[0]
    n_sub = t_len // dilation
    nb = n_sub // ATT_BLOCK
    pairs = ch // LANES
    qk_w, z_w = 2 * pairs, 5 * pairs
    scale = 1.0 / math.sqrt(HEAD_DIM)
    bias_prev, bias_cur = _alibi_bias(2 * pairs, dilation)
    qkv = qk.reshape(n_sub, dilation * 2 * ch)
    zv = z.reshape(n_sub, dilation * 5 * ch)

    def body(q_ref, kc_ref, kp_ref, vc_ref, vp_ref, bp_ref, bc_ref, o_ref, l_ref):
        n = pl.program_id(2)
        low, high = _head_masks(ATT_BLOCK)
        qv = q_ref[...]
        kc, kp = kc_ref[...].astype(BF16), kp_ref[...].astype(BF16)
        vc, vp = vc_ref[...].astype(BF16), vp_ref[...].astype(BF16)
        outs, lses = [], []
        for hh, mask in enumerate((low, high)):
            qm = jnp.where(mask, qv, 0.0).astype(BF16)
            s_c = _scores(qm, kc, bc_ref[hh], scale)
            s_p = jnp.where(n > 0, _scores(qm, kp, bp_ref[hh], scale), MASKED)
            mx = jnp.maximum(jnp.max(s_c, axis=-1, keepdims=True), jnp.max(s_p, axis=-1, keepdims=True))
            e_c, e_p = jnp.exp(s_c - mx), jnp.exp(s_p - mx)
            den = jnp.sum(e_c, axis=-1, keepdims=True) + jnp.sum(e_p, axis=-1, keepdims=True)
            acc = (lax.dot_general(e_c.astype(BF16), vc, _NN, preferred_element_type=F32)
                   + lax.dot_general(e_p.astype(BF16), vp, _NN, preferred_element_type=F32))
            outs.append(acc / den)
            lses.append(mx + jnp.log(den))
        o_ref[...] = jnp.where(low, outs[0], outs[1])
        l_ref[...] = jnp.where(low, lses[0], lses[1])

    blk = (ATT_BLOCK, LANES)
    prev = lambda n: jnp.maximum(n - 1, 0)
    bias_spec = pl.BlockSpec((2, ATT_BLOCK, ATT_BLOCK), lambda r, p, n: (p, 0, 0))
    out_spec = pl.BlockSpec(blk, lambda r, p, n: (n, r * pairs + p))
    o, lse = pl.pallas_call(
        body, name=name, grid=(dilation, pairs, nb),
        in_specs=[pl.BlockSpec(blk, lambda r, p, n: (n, r * qk_w + p)),
                  pl.BlockSpec(blk, lambda r, p, n: (n, r * qk_w + pairs + p)),
                  pl.BlockSpec(blk, lambda r, p, n: (prev(n), r * qk_w + pairs + p)),
                  pl.BlockSpec(blk, lambda r, p, n: (n, r * z_w + 4 * pairs + p)),
                  pl.BlockSpec(blk, lambda r, p, n: (prev(n), r * z_w + 4 * pairs + p)),
                  bias_spec, bias_spec],
        out_specs=[out_spec, out_spec],
        out_shape=[_sds((n_sub, dilation * ch), F32)] * 2,
        compiler_params=_params("parallel", "parallel", "arbitrary"),
    )(qkv, qkv, qkv, zv, zv, bias_prev, bias_cur)
    return o.reshape(t_len, ch), lse.reshape(t_len, ch)


def attn_combine(outs, lses, name):
    t_len, ch = outs[0].shape
    tm = _row_tile(t_len, 512)

    def body(o1, o2, o3, l1, l2, l3, out_ref, outb_ref, lg_ref):
        a, b, c = l1[...], l2[...], l3[...]
        mx = jnp.maximum(jnp.maximum(a, b), c)
        tot = mx + jnp.log(jnp.exp(a - mx) + jnp.exp(b - mx) + jnp.exp(c - mx))
        val = jnp.exp(a - tot) * o1[...] + jnp.exp(b - tot) * o2[...] + jnp.exp(c - tot) * o3[...]
        out_ref[...] = val
        outb_ref[...] = val.astype(BF16)
        lg_ref[...] = tot

    row = pl.BlockSpec((tm, ch), lambda i: (i, 0))
    return pl.pallas_call(
        body, name=name, grid=(t_len // tm,),
        in_specs=[row] * 6, out_specs=[row] * 3,
        out_shape=[_sds((t_len, ch), F32), _sds((t_len, ch), BF16), _sds((t_len, ch), F32)],
        compiler_params=_params("parallel"),
    )(*outs, *lses)


def attn_bwd(qk, z, dy_cat, out, lg, dilation, ch, name):
    t_len = qk.shape[0]
    n_sub = t_len // dilation
    nb = n_sub // ATT_BLOCK
    pairs = ch // LANES
    qk_w, z_w = 2 * pairs, 5 * pairs
    scale = 1.0 / math.sqrt(HEAD_DIM)
    bias_prev, bias_cur = _alibi_bias(2 * pairs, dilation)
    qkv = qk.reshape(n_sub, dilation * 2 * ch)
    zv = z.reshape(n_sub, dilation * 5 * ch)
    dyv = dy_cat.reshape(n_sub, dilation * 2 * ch)
    outv = out.reshape(n_sub, dilation * ch)
    lgv = lg.reshape(n_sub, dilation * ch)

    def body(q_ref, qn_ref, kc_ref, kp_ref, vc_ref, vp_ref, do_ref, don_ref, o_ref, on_ref,
             l_ref, ln_ref, bp_ref, bc_ref, dq_ref, dk_ref, dv_ref):
        n = pl.program_id(2)
        low, high = _head_masks(ATT_BLOCK)
        has_prev = n > 0
        has_next = n < nb - 1
        qv, qnv = q_ref[...], qn_ref[...]
        kc, kp = kc_ref[...].astype(BF16), kp_ref[...].astype(BF16)
        vc, vp = vc_ref[...].astype(BF16), vp_ref[...].astype(BF16)
        dov, donv = do_ref[...], don_ref[...]
        prod, prodn = dov * o_ref[...], donv * on_ref[...]
        lv, lnv = l_ref[...], ln_ref[...]
        dqs, dks, dvs = [], [], []
        for hh, mask in enumerate((low, high)):
            def head_rows(val):
                return jnp.max(jnp.where(mask, val, MASKED), axis=-1, keepdims=True)

            def head_sum(val):
                return jnp.sum(jnp.where(mask, val, 0.0), axis=-1, keepdims=True)

            qm = jnp.where(mask, qv, 0.0).astype(BF16)
            qnm = jnp.where(mask, qnv, 0.0).astype(BF16)
            dom = jnp.where(mask, dov, 0.0).astype(BF16)
            donm = jnp.where(mask, donv, 0.0).astype(BF16)
            lse, lse_n = head_rows(lv), head_rows(lnv)
            delta, delta_n = head_sum(prod), head_sum(prodn)

            p_cc = jnp.exp(_scores(qm, kc, bc_ref[hh], scale) - lse)
            dp = lax.dot_general(dom, vc, _NT, preferred_element_type=F32)
            ds_cc = (p_cc * (dp - delta)).astype(BF16)
            p_cp = jnp.where(has_prev, jnp.exp(_scores(qm, kp, bp_ref[hh], scale) - lse), 0.0)
            dp = lax.dot_general(dom, vp, _NT, preferred_element_type=F32)
            ds_cp = (p_cp * (dp - delta)).astype(BF16)
            p_nc = jnp.where(has_next, jnp.exp(_scores(qnm, kc, bp_ref[hh], scale) - lse_n), 0.0)
            dp = lax.dot_general(donm, vc, _NT, preferred_element_type=F32)
            ds_nc = (p_nc * (dp - delta_n)).astype(BF16)

            dqs.append(scale * (lax.dot_general(ds_cc, kc, _NN, preferred_element_type=F32)
                                + lax.dot_general(ds_cp, kp, _NN, preferred_element_type=F32)))
            dks.append(scale * (lax.dot_general(ds_cc, qm, _TN, preferred_element_type=F32)
                                + lax.dot_general(ds_nc, qnm, _TN, preferred_element_type=F32)))
            dvs.append(lax.dot_general(p_cc.astype(BF16), dom, _TN, preferred_element_type=F32)
                       + lax.dot_general(p_nc.astype(BF16), donm, _TN, preferred_element_type=F32))
        dq_ref[...] = jnp.where(low, dqs[0], dqs[1])
        dk_ref[...] = jnp.where(low, dks[0], dks[1])
        dv_ref[...] = jnp.where(low, dvs[0], dvs[1])

    blk = (ATT_BLOCK, LANES)
    prev = lambda n: jnp.maximum(n - 1, 0)
    nxt = lambda n: jnp.minimum(n + 1, nb - 1)
    spec = lambda row_fn, width, off: pl.BlockSpec(blk, lambda r, p, n: (row_fn(n), r * width + off + p))
    same = lambda n: n
    bias_spec = pl.BlockSpec((2, ATT_BLOCK, ATT_BLOCK), lambda r, p, n: (p, 0, 0))
    out_spec = spec(same, pairs, 0)
    dq, dk, dv = pl.pallas_call(
        body, name=name, grid=(dilation, pairs, nb),
        in_specs=[spec(same, qk_w, 0), spec(nxt, qk_w, 0),
                  spec(same, qk_w, pairs), spec(prev, qk_w, pairs),
                  spec(same, z_w, 4 * pairs), spec(prev, z_w, 4 * pairs),
                  spec(same, qk_w, pairs), spec(nxt, qk_w, pairs),
                  spec(same, pairs, 0), spec(nxt, pairs, 0),
                  spec(same, pairs, 0), spec(nxt, pairs, 0),
                  bias_spec, bias_spec],
        out_specs=[out_spec] * 3,
        out_shape=[_sds((n_sub, dilation * ch), F32)] * 3,
        compiler_params=_params("parallel", "parallel", "arbitrary"),
    )(qkv, qkv, qkv, qkv, zv, zv, dyv, dyv, outv, outv, lgv, lgv, bias_prev, bias_cur)
    return dq.reshape(t_len, ch), dk.reshape(t_len, ch), dv.reshape(t_len, ch)


def sum3_bf16(a, b, c, name):
    t_len, ch = a.shape
    tm = _row_tile(t_len, 512)

    def body(a_ref, b_ref, c_ref, o_ref):
        o_ref[...] = (a_ref[...] + b_ref[...] + c_ref[...]).astype(BF16)

    row = pl.BlockSpec((tm, ch), lambda i: (i, 0))
    return pl.pallas_call(
        body, name=name, grid=(t_len // tm,), in_specs=[row] * 3, out_specs=row,
        out_shape=_sds((t_len, ch), BF16), compiler_params=_params("parallel"),
    )(a, b, c)


def _blk3(rows, cols):
    return pl.BlockSpec((None, rows, cols), lambda j, t: (j, 0, 0))


def ffn_up(h, wgu, name, riders):
    t_len, d = h.shape
    n_blk, _, _, fj = wgu.shape

    def epilogue(accs, e_refs, o_refs):
        gate, up = accs
        o_refs[0][...] = gate.astype(BF16)
        o_refs[1][...] = up.astype(BF16)
        o_refs[2][...] = (gate * _sigmoid(gate) * up).astype(BF16)

    w_spec = lambda i: pl.BlockSpec((None, None, d, fj), lambda j, t: (j, i, 0, 0))
    act = lambda tm: pl.BlockSpec((None, tm, fj), lambda j, t: (j, t, 0))
    return mm_cols(name, h, [wgu, wgu], [w_spec(0), w_spec(1)], False, [], lambda tm: [],
                   [_sds((n_blk, t_len, fj), BF16)] * 3, lambda tm: [act(tm)] * 3, epilogue, n_blk, riders)


def ffn_down(act, wd, res, name, riders):
    n_blk, t_len, fj = act.shape
    d = wd.shape[2]
    return mm_reduce(name, [act], lambda tm: [pl.BlockSpec((None, tm, fj), lambda t, j: (j, t, 0))],
                     [wd], [pl.BlockSpec((None, fj, d), lambda t, j: (j, 0, 0))], False, res, 0.5,
                     t_len, d, n_blk, riders)


def ffn_bwd(h, gate, up, act, wgu, wd, dyb, name):
    t_len, d = h.shape
    n_blk, _, fj = act.shape

    def epilogue(accs, e_refs, o_refs):
        d_act = 0.5 * accs[0]
        gv, uv = e_refs[0][...].astype(F32), e_refs[1][...].astype(F32)
        sg = _sigmoid(gv)
        o_refs[0][...] = (d_act * uv * (sg * (1.0 + gv * (1.0 - sg)))).astype(BF16)
        o_refs[1][...] = (d_act * gv * sg).astype(BF16)

    act_jt = lambda tm: pl.BlockSpec((None, tm, fj), lambda j, t: (j, t, 0))
    (d_gate, d_up), _ = mm_cols(name + "_dact", dyb, [wd], [_blk3(fj, d)], True, [gate, up],
                                lambda tm: [act_jt(tm)] * 2, [_sds((n_blk, t_len, fj), BF16)] * 2,
                                lambda tm: [act_jt(tm)] * 2, epilogue, n_blk)

    (d_wd,), _ = mm_tn(name + "_dwd", act, act_jt, [dyb],
                       lambda tt: [pl.BlockSpec((tt, d), lambda j, t: (t, 0))],
                       [_sds((n_blk, fj, d), BF16)], [_blk3(fj, d)], 0.5, t_len, n_blk)

    (d_wg, d_wu), (recv_wd,) = mm_tn(
        name + "_dwgu", h, lambda tt: pl.BlockSpec((tt, d), lambda j, t: (t, 0)),
        [d_gate, d_up], lambda tt: [act_jt(tt)] * 2,
        [_sds((n_blk, d, fj), BF16)] * 2, [_blk3(d, fj)] * 2, 1.0, t_len, n_blk,
        [Rider("scatter", d_wd)])

    act_tj = lambda tm: pl.BlockSpec((None, tm, fj), lambda t, j: (j, t, 0))
    w_in = lambda i: pl.BlockSpec((None, None, d, fj), lambda t, j: (j, i, 0, 0))
    dh, (recv_wg, recv_wu) = mm_reduce(
        name + "_dh", [d_gate, d_up], lambda tm: [act_tj(tm)] * 2,
        [wgu, wgu], [w_in(0), w_in(1)], True, None, 1.0, t_len, d, n_blk,
        [Rider("scatter", d_wg), Rider("scatter", d_wu)])
    return dh, recv_wg, recv_wu, recv_wd


def local_step(x, target, g1, wgu1, wd1_s, gmix, win_s, conv_w, conv_b, ln_g, ln_b, gq, gk, wout_s, g3,
               wgu2_s, wd2_s):
    t_len, d = x.shape
    ch = d // 2
    ij = win_s.shape[1]
    oj = wout_s.shape[0]
    n_blk = N_DEV

    h1 = rms_fwd(x, g1, "rms1")
    (gate1, up1, act1), (wd1,) = ffn_up(h1, wgu1, "ffn1_up", [Rider("gather", wd1_s)])
    x1, (win,) = ffn_down(act1, wd1, x, "ffn1_down", [Rider("gather", win_s)])

    h2 = rms_fwd(x1, gmix, "rms_mix")

    def store_f32(accs, e_refs, o_refs):
        o_refs[0][...] = accs[0]

    (z,), (wout, wgu2) = mm_cols(
        "w_in", h2, [win], [_blk3(d, ij)], False, [], lambda tm: [],
        [_sds((t_len, n_blk * ij), F32)],
        lambda tm: [pl.BlockSpec((tm, ij), lambda j, t: (t, j))], store_f32, n_blk,
        [Rider("gather", wout_s), Rider("gather", wgu2_s)])

    conv_w32 = jnp.pad(conv_w, ((0, 32 - CONV_WIDTH), (0, 0)))
    y_conv = conv_fwd(z, conv_w32, conv_b, ln_g, ln_b, "conv_fwd")

    g2 = jnp.concatenate([jnp.tile(gq, (1, ch // HEAD_DIM)), jnp.tile(gk, (1, ch // HEAD_DIM))], axis=1)
    qk = qk_norm_fwd(z, g2, ch, "qk_norm")
    branch = [attn_fwd(qk, z, dil, ch, "attn_fwd_d%d" % dil) for dil in DILATIONS]
    att, att_b, lg = attn_combine([o for o, _ in branch], [l for _, l in branch], "attn_combine")

    y_cat = jnp.concatenate([y_conv, att_b], axis=1)
    x2, (wd2,) = mm_reduce(
        "w_out", [y_cat], lambda tm: [pl.BlockSpec((tm, oj), lambda t, j: (t, j))],
        [wout], [pl.BlockSpec((None, oj, d), lambda t, j: (j, 0, 0))], False, x1, 1.0,
        t_len, d, n_blk, [Rider("gather", wd2_s)])

    h3 = rms_fwd(x2, g3, "rms3")
    (gate2, up2, act2), _ = ffn_up(h3, wgu2, "ffn2_up", [])
    y, _ = ffn_down(act2, wd2, x2, "ffn2_down", [])

    loss_tile, dy, dyb = loss_head(y, target, "loss")

    dh3, recv_wg2, recv_wu2, recv_wd2 = ffn_bwd(h3, gate2, up2, act2, wgu2, wd2, dyb, "ffn2")
    dx2, dx2b, d_g3 = rms_bwd(x2, g3, dh3, dy, "rms3_bwd")

    (dy_cat,), _ = mm_cols("w_out_dy", dx2b, [wout], [_blk3(oj, d)], True, [], lambda tm: [],
                           [_sds((t_len, n_blk * oj), F32)],
                           lambda tm: [pl.BlockSpec((tm, oj), lambda j, t: (t, j))], store_f32, n_blk)
    (d_wout,), _ = mm_tn("w_out_dw", y_cat, lambda tt: pl.BlockSpec((tt, oj), lambda j, t: (t, j)),
                         [dx2b], lambda tt: [pl.BlockSpec((tt, d), lambda j, t: (t, 0))],
                         [_sds((n_blk, oj, d), BF16)], [_blk3(oj, d)], 1.0, t_len, n_blk)

    dc, d_lg, d_lb, d_cb = conv_bwd_norm(z, dy_cat, conv_w32, conv_b, ln_g, ln_b, "conv_bwd_norm")
    dz_a, dz_g, d_cw8 = conv_bwd_taps(z, dc, conv_w32, "conv_bwd_taps")
    d_cw = jnp.sum(d_cw8, axis=1)[:CONV_WIDTH]

    grads = [attn_bwd(qk, z, dy_cat, att, lg, dil, ch, "attn_bwd_d%d" % dil) for dil in DILATIONS]
    gq_t = jnp.tile(gq, (1, LANES // HEAD_DIM))
    gk_t = jnp.tile(gk, (1, LANES // HEAD_DIM))
    dz_q, d_gq2 = qk_norm_bwd(z, gq_t, [g[0] for g in grads], 2 * ch // LANES, ch, "q_norm_bwd")
    dz_k, d_gk2 = qk_norm_bwd(z, gk_t, [g[1] for g in grads], 3 * ch // LANES, ch, "k_norm_bwd")
    d_gq = d_gq2[:, :HEAD_DIM] + d_gq2[:, HEAD_DIM:]
    d_gk = d_gk2[:, :HEAD_DIM] + d_gk2[:, HEAD_DIM:]
    dz_v = sum3_bf16(grads[0][2], grads[1][2], grads[2][2], "dv_sum")
    dzb = jnp.concatenate([dz_a, dz_g, dz_q, dz_k, dz_v], axis=1)

    (d_win,), (recv_wout,) = mm_tn(
        "w_in_dw", h2, lambda tt: pl.BlockSpec((tt, d), lambda j, t: (t, 0)),
        [dzb], lambda tt: [pl.BlockSpec((tt, ij), lambda j, t: (t, j))],
        [_sds((n_blk, d, ij), BF16)], [_blk3(d, ij)], 1.0, t_len, n_blk, [Rider("scatter", d_wout)])
    dh2, (recv_win,) = mm_reduce(
        "w_in_dh", [dzb], lambda tm: [pl.BlockSpec((tm, ij), lambda t, j: (t, j))],
        [win], [pl.BlockSpec((None, d, ij), lambda t, j: (j, 0, 0))], True, None, 1.0,
        t_len, d, n_blk, [Rider("scatter", d_win)])
    dx1, dx1b, d_gmix = rms_bwd(x1, gmix, dh2, dx2, "rms_mix_bwd")

    dh1, recv_wg1, recv_wu1, recv_wd1 = ffn_bwd(h1, gate1, up1, act1, wgu1, wd1, dx1b, "ffn1")
    grad_x, _, d_g1 = rms_bwd(x, g1, dh1, dx1, "rms1_bwd")

    big = dict(ffn1_w_gate=recv_wg1, ffn1_w_up=recv_wu1, ffn1_w_down=recv_wd1, w_in=recv_win, w_out=recv_wout,
               ffn2_w_gate=recv_wg2, ffn2_w_up=recv_wu2, ffn2_w_down=recv_wd2)
    small = dict(g1=d_g1, gmix=d_gmix, g3=d_g3, conv_b=d_cb, ln_g=d_lg, ln_b=d_lb, gq=d_gq, gk=d_gk, conv_w=d_cw)
    return loss_tile[0, 0], grad_x, big, small


SMALL_ROWS = 48


def _pack_small(ch, g1, gmix, g3, conv_b, ln_g, ln_b, gq, gk, conv_w):
    pad_head = lambda v: jnp.pad(v, ((0, 0), (0, ch - v.shape[1])))
    rows = [g1.reshape(2, ch), gmix.reshape(2, ch), g3.reshape(2, ch), conv_b, ln_g, ln_b,
            pad_head(gq), pad_head(gk), conv_w]
    packed = jnp.concatenate(rows, axis=0)
    return jnp.pad(packed, ((0, SMALL_ROWS - packed.shape[0]), (0, 0)))


def _unpack_small(packed, d):
    return dict(g1=packed[0:2].reshape(1, d), gmix=packed[2:4].reshape(1, d), g3=packed[4:6].reshape(1, d),
                conv_b=packed[6:7], ln_g=packed[7:8], ln_b=packed[8:9],
                gq=packed[9:10, :HEAD_DIM], gk=packed[10:11, :HEAD_DIM])


def kernel(x, ffn1_norm_g, ffn1_w_gate, ffn1_w_up, ffn1_w_down, mix_norm_g, w_in, conv_w_dw, conv_b_dw, conv_ln_g, conv_ln_b, q_norm_g, k_norm_g, w_out, ffn2_norm_g, ffn2_w_gate, ffn2_w_up, ffn2_w_down, loss_target, m_ffn1_norm_g, m_ffn1_w_gate, m_ffn1_w_up, m_ffn1_w_down, m_mix_norm_g, m_w_in, m_conv_w_dw, m_conv_b_dw, m_conv_ln_g, m_conv_ln_b, m_q_norm_g, m_k_norm_g, m_w_out, m_ffn2_norm_g, m_ffn2_w_gate, m_ffn2_w_up, m_ffn2_w_down, v_ffn1_norm_g, v_ffn1_w_gate, v_ffn1_w_up, v_ffn1_w_down, v_mix_norm_g, v_w_in, v_conv_w_dw, v_conv_b_dw, v_conv_ln_g, v_conv_ln_b, v_q_norm_g, v_k_norm_g, v_w_out, v_ffn2_norm_g, v_ffn2_w_gate, v_ffn2_w_up, v_ffn2_w_down):
    d = x.shape[-1]
    ch = d // 2
    me = 4 * lax.axis_index("x") + 2 * lax.axis_index("y") + lax.axis_index("c")

    gu = lambda wg, wu: jnp.stack([wg[0], wu[0]]).astype(BF16)
    wgu1 = all_gather(gu(ffn1_w_gate, ffn1_w_up), "ag_wgu1")
    cw_all = all_gather(conv_w_dw[0], "ag_convw")
    conv_w = jnp.transpose(cw_all, (1, 0, 2)).reshape(CONV_WIDTH, ch)

    loss_part, grad_x, big, small = local_step(
        x[0], loss_target[0], ffn1_norm_g, wgu1, ffn1_w_down[0].astype(BF16), mix_norm_g, w_in[0].astype(BF16),
        conv_w, conv_b_dw, conv_ln_g, conv_ln_b, q_norm_g, k_norm_g, w_out[0].astype(BF16), ffn2_norm_g,
        gu(ffn2_w_gate, ffn2_w_up), ffn2_w_down[0].astype(BF16))
    loss = lax.psum(loss_part, MESH_AXES)

    state = dict(
        ffn1_w_gate=(ffn1_w_gate, m_ffn1_w_gate, v_ffn1_w_gate), ffn1_w_up=(ffn1_w_up, m_ffn1_w_up, v_ffn1_w_up),
        ffn1_w_down=(ffn1_w_down, m_ffn1_w_down, v_ffn1_w_down), w_in=(w_in, m_w_in, v_w_in),
        w_out=(w_out, m_w_out, v_w_out),
        ffn2_w_gate=(ffn2_w_gate, m_ffn2_w_gate, v_ffn2_w_gate), ffn2_w_up=(ffn2_w_up, m_ffn2_w_up, v_ffn2_w_up),
        ffn2_w_down=(ffn2_w_down, m_ffn2_w_down, v_ffn2_w_down))
    out = {}
    for pname, (w, m, v) in state.items():
        res = adamw(w[0], m[0], v[0], big[pname], "adamw_" + pname)
        out[pname] = [r[None] for r in res]

    zero_taps = jnp.zeros((CONV_WIDTH, ch), F32)
    pack = lambda g1, gm, g3, cb, lg, lb, gq, gk: _pack_small(ch, g1, gm, g3, cb, lg, lb, gq, gk, zero_taps)
    small_parts = all_gather(_pack_small(ch, **small), "ag_small_grads")
    s_res = adamw(
        pack(ffn1_norm_g, mix_norm_g, ffn2_norm_g, conv_b_dw, conv_ln_g, conv_ln_b, q_norm_g, k_norm_g),
        pack(m_ffn1_norm_g, m_mix_norm_g, m_ffn2_norm_g, m_conv_b_dw, m_conv_ln_g, m_conv_ln_b, m_q_norm_g, m_k_norm_g),
        pack(v_ffn1_norm_g, v_mix_norm_g, v_ffn2_norm_g, v_conv_b_dw, v_conv_ln_g, v_conv_ln_b, v_q_norm_g, v_k_norm_g),
        small_parts, "adamw_small")
    s_out = [_unpack_small(r, d) for r in s_res]
    names = dict(g1="ffn1_norm_g", gmix="mix_norm_g", g3="ffn2_norm_g", conv_b="conv_b_dw", ln_g="conv_ln_g",
                 ln_b="conv_ln_b", gq="q_norm_g", gk="k_norm_g")
    for key, full in names.items():
        out[full] = [r[key] for r in s_out]

    cshard = ch // N_DEV
    taps_sum = s_res[0][11:11 + CONV_WIDTH]
    taps_mine = lax.dynamic_slice(taps_sum, (0, me * cshard), (CONV_WIDTH, cshard))
    pad_taps = lambda a: jnp.pad(a, ((0, 32 - CONV_WIDTH), (0, 0)))
    c_res = adamw(pad_taps(conv_w_dw[0]), pad_taps(m_conv_w_dw[0]), pad_taps(v_conv_w_dw[0]),
                  pad_taps(taps_mine)[None], "adamw_convw")
    out["conv_w_dw"] = [r[:CONV_WIDTH][None] for r in c_res]

    order = ["ffn1_norm_g", "ffn1_w_gate", "ffn1_w_up", "ffn1_w_down", "mix_norm_g", "w_in", "conv_w_dw",
             "conv_b_dw", "conv_ln_g", "conv_ln_b", "q_norm_g", "k_norm_g", "w_out", "ffn2_norm_g",
             "ffn2_w_gate", "ffn2_w_up", "ffn2_w_down"]
    result = [loss, grad_x[None]]
    for kind in range(4):
        result += [out[n][kind] for n in order]
    return tuple(result)
```

```python
import math
from typing import NamedTuple

import jax
import jax.numpy as jnp
from jax import lax
from jax.experimental import pallas as pl
from jax.experimental.pallas import tpu as pltpu

F32 = jnp.float32
BF16 = jnp.bfloat16

N_DEV = 8
EPS = 1e-6
HEAD_DIM = 64
LANES = 128
CONV_WIDTH = 31
HALO = 32
ROW_CHUNK = 32
ATT_BLOCK = 128
DILATIONS = (1, 4, 16)
ATT_UNITS = 16
ATT_CHUNK = ATT_UNITS * ATT_BLOCK
ALIBI_MAX_BIAS = 8.0
MASKED = -1e30
VMEM_LIMIT = 56 * 1024 * 1024

ADAM_LR = 0.001
ADAM_B1 = 0.9
ADAM_B2 = 0.999
ADAM_EPS = 1e-08
ADAM_WD = 0.01
ADAM_STEP = 10

MESH_AXES = ("x", "y", "c")
ANY = pl.BlockSpec(memory_space=pl.ANY)


def _sds(shape, dtype):
    return jax.ShapeDtypeStruct(tuple(shape), dtype)


def _params(*sem):
    return pltpu.CompilerParams(dimension_semantics=sem, vmem_limit_bytes=VMEM_LIMIT)


def _sigmoid(v):
    return 1.0 / (1.0 + jnp.exp(-v))


def _row_tile(t, want):
    for cand in range(min(want, t) // 8 * 8, 0, -8):
        if t % cand == 0:
            return cand
    return t


def _mesh_pos():
    return lax.axis_index("x"), lax.axis_index("y"), lax.axis_index("c")


def _comm_sems():
    return [pltpu.SemaphoreType.DMA((7,)), pltpu.SemaphoreType.DMA((7,)), pltpu.SemaphoreType.DMA(())]


def _gather_phases(x_ref, out_ref, send_sems, recv_sems, local_sem):
    x, y, c = _mesh_pos()
    me, sibling = (x, y, c), (x, y, 1 - c)
    chips = [(1 - x, y), (x, 1 - y), (1 - x, 1 - y)]

    def slot(px, py, pc):
        return out_ref.at[4 * px + 2 * py + pc]

    def copy(k, block, to, src=None):
        return pltpu.make_async_remote_copy(
            src_ref=slot(*block) if src is None else src, dst_ref=slot(*block),
            send_sem=send_sems.at[k], recv_sem=recv_sems.at[k],
            device_id=to, device_id_type=pl.DeviceIdType.MESH)

    mine = pltpu.make_async_copy(x_ref, slot(*me), local_sem)
    first = [copy(0, me, sibling, src=x_ref)]
    first += [copy(1 + j, me, (*chip, c), src=x_ref) for j, chip in enumerate(chips)]
    passed = [copy(4 + j, (*chip, c), sibling) for j, chip in enumerate(chips)]

    def start():
        mine.start()
        for cp in first:
            cp.start()

    def forward():
        for j, chip in enumerate(chips):
            copy(1 + j, (*chip, c), me).wait_recv()
            passed[j].start()

    def finish():
        copy(0, sibling, me).wait_recv()
        for j, chip in enumerate(chips):
            copy(4 + j, (*chip, 1 - c), me).wait_recv()
        for cp in first + passed:
            cp.wait_send()
        mine.wait()

    return start, forward, finish


def _scatter_phases(p_ref, out_ref, send_sems, recv_sems, local_sem):
    x, y, c = _mesh_pos()
    me = 4 * x + 2 * y + c
    flips = [(fx, fy, fc) for fx in (0, 1) for fy in (0, 1) for fc in (0, 1)][1:]

    def copy(k, flip, receiving):
        px, py, pc = (1 - x if flip[0] else x, 1 - y if flip[1] else y, 1 - c if flip[2] else c)
        them = 4 * px + 2 * py + pc
        return pltpu.make_async_remote_copy(
            src_ref=p_ref.at[them], dst_ref=out_ref.at[them if receiving else me],
            send_sem=send_sems.at[k], recv_sem=recv_sems.at[k],
            device_id=(px, py, pc), device_id_type=pl.DeviceIdType.MESH)

    mine = pltpu.make_async_copy(p_ref.at[me], out_ref.at[me], local_sem)

    def start():
        mine.start()
        for k, flip in enumerate(flips):
            copy(k, flip, False).start()

    def finish():
        for k, flip in enumerate(flips):
            copy(k, flip, True).wait_recv()
            copy(k, flip, False).wait_send()
        mine.wait()

    return start, None, finish


class Rider(NamedTuple):
    kind: str
    src: jax.Array

    def out_shape(self):
        shape = (N_DEV,) + self.src.shape if self.kind == "gather" else self.src.shape
        return _sds(shape, self.src.dtype)


def _rider_hooks(riders, in_refs, out_refs, sem_refs, step, n_steps):
    phases = [(_gather_phases if r.kind == "gather" else _scatter_phases)(
                  in_refs[i], out_refs[i], *sem_refs[3 * i:3 * i + 3]) for i, r in enumerate(riders)]

    def begin():
        for start, forward, _ in phases:
            pl.when(step == 0)(start)
            if forward is not None:
                pl.when(step == n_steps // 2)(forward)

    def end():
        for _, _, finish in phases:
            pl.when(step == n_steps - 1)(finish)

    return begin, end


def _split_refs(refs, n_in, n_out, n_scratch, n_riders):
    pos, parts = 0, []
    for n in (n_in, n_riders, n_out, n_riders, n_scratch, 3 * n_riders):
        parts.append(refs[pos:pos + n])
        pos += n
    return parts


def all_gather(shard, name):
    def body(x_ref, out_ref, send_sems, recv_sems, local_sem):
        start, forward, finish = _gather_phases(x_ref, out_ref, send_sems, recv_sems, local_sem)
        start()
        forward()
        finish()

    return pl.pallas_call(
        body, name=name, out_shape=_sds((N_DEV,) + shard.shape, shard.dtype),
        in_specs=[ANY], out_specs=ANY, scratch_shapes=_comm_sems(),
    )(shard)


def adamw(w, m, v, parts, name):
    n_parts, rows, cols = parts.shape
    tr = _row_tile(rows, 128)
    c1 = 1.0 - ADAM_B1 ** ADAM_STEP
    c2 = 1.0 - ADAM_B2 ** ADAM_STEP

    def body(w_ref, m_ref, v_ref, p_ref, g_ref, d_ref, nm_ref, nv_ref):
        g = p_ref[0].astype(F32)
        for s in range(1, n_parts):
            g = g + p_ref[s].astype(F32)
        nm = ADAM_B1 * m_ref[...] + (1.0 - ADAM_B1) * g
        nv = ADAM_B2 * v_ref[...] + (1.0 - ADAM_B2) * (g * g)
        delta = -ADAM_LR * ((nm / c1) / (jnp.sqrt(nv / c2) + ADAM_EPS) + ADAM_WD * w_ref[...])
        g_ref[...] = g
        d_ref[...] = delta
        nm_ref[...] = nm
        nv_ref[...] = nv

    mat = pl.BlockSpec((tr, cols), lambda i: (i, 0))
    return pl.pallas_call(
        body, name=name, grid=(rows // tr,),
        in_specs=[mat, mat, mat, pl.BlockSpec((n_parts, tr, cols), lambda i: (0, i, 0))],
        out_specs=[mat, mat, mat, mat],
        out_shape=[_sds((rows, cols), F32)] * 4,
        compiler_params=_params("parallel"),
    )(w, m, v, parts)


_NN = (((1,), (0,)), ((), ()))
_NT = (((1,), (1,)), ((), ()))
_TN = (((0,), (0,)), ((), ()))


def mm_cols(name, a, b_list, b_specs, nt, extras, extra_specs, out_shapes, out_specs, epilogue, n_blk, riders=()):
    t_len, k_len = a.shape
    tm = _row_tile(t_len, 512)
    nb, ne, n_out, nr = len(b_list), len(extras), len(out_shapes), len(riders)
    t_steps = t_len // tm

    def body(*refs):
        ins, r_in, outs, r_out, _, r_sem = _split_refs(refs, 1 + nb + ne, n_out, 0, nr)
        step = pl.program_id(0) * t_steps + pl.program_id(1)
        begin, end = _rider_hooks(riders, r_in, r_out, r_sem, step, n_blk * t_steps)
        begin()
        av = ins[0][...]
        accs = [lax.dot_general(av, br[...], _NT if nt else _NN, preferred_element_type=F32)
                for br in ins[1:1 + nb]]
        epilogue(accs, ins[1 + nb:], outs)
        end()

    res = pl.pallas_call(
        body, name=name, grid=(n_blk, t_steps),
        in_specs=([pl.BlockSpec((tm, k_len), lambda j, t: (t, 0))] + list(b_specs) + list(extra_specs(tm))
                  + [ANY] * nr),
        out_specs=list(out_specs(tm)) + [ANY] * nr,
        out_shape=list(out_shapes) + [r.out_shape() for r in riders],
        scratch_shapes=_comm_sems() * nr,
        compiler_params=_params("arbitrary", "arbitrary"),
    )(a, *b_list, *extras, *[r.src for r in riders])
    return res[:n_out], res[n_out:]


def mm_reduce(name, a_list, a_specs, b_list, b_specs, nt, res, scale, t_len, n_len, n_blk, riders=()):
    tm = _row_tile(t_len, 512)
    na, nr = len(a_list), len(riders)
    has_res = res is not None
    t_steps = t_len // tm

    def body(*refs):
        ins, r_in, outs, r_out, scr, r_sem = _split_refs(refs, 2 * na + has_res, 1, 1, nr)
        o_ref, acc = outs[0], scr[0]
        j = pl.program_id(1)
        step = pl.program_id(0) * n_blk + j
        begin, end = _rider_hooks(riders, r_in, r_out, r_sem, step, t_steps * n_blk)
        begin()

        @pl.when(j == 0)
        def _():
            acc[...] = jnp.zeros_like(acc)

        part = None
        for ar, br in zip(ins[:na], ins[na:2 * na]):
            d = lax.dot_general(ar[...], br[...], _NT if nt else _NN, preferred_element_type=F32)
            part = d if part is None else part + d
        acc[...] += part

        @pl.when(j == n_blk - 1)
        def _():
            val = acc[...] * scale if scale != 1.0 else acc[...]
            o_ref[...] = ins[2 * na][...] + val if has_res else val

        end()

    row = pl.BlockSpec((tm, n_len), lambda t, j: (t, 0))
    out = pl.pallas_call(
        body, name=name, grid=(t_steps, n_blk),
        in_specs=list(a_specs(tm)) + list(b_specs) + ([row] if has_res else []) + [ANY] * nr,
        out_specs=[row] + [ANY] * nr,
        out_shape=[_sds((t_len, n_len), F32)] + [r.out_shape() for r in riders],
        scratch_shapes=[pltpu.VMEM((tm, n_len), F32)] + _comm_sems() * nr,
        compiler_params=_params("arbitrary", "arbitrary"),
    )(*a_list, *b_list, *([res] if has_res else []), *[r.src for r in riders])
    return out[0], out[1:]


def mm_tn(name, x, x_spec, dy_list, dy_specs, out_shapes, out_specs, scale, t_len, n_blk, riders=()):
    tt = _row_tile(t_len, 512)
    nd, nr = len(dy_list), len(riders)
    t_steps = t_len // tt
    acc_shapes = [pltpu.VMEM(spec.block_shape[-2:], F32) for spec in out_specs]

    def body(*refs):
        ins, r_in, outs, r_out, accs, r_sem = _split_refs(refs, 1 + nd, nd, nd, nr)
        t = pl.program_id(1)
        step = pl.program_id(0) * t_steps + t
        begin, end = _rider_hooks(riders, r_in, r_out, r_sem, step, n_blk * t_steps)
        begin()
        xv = ins[0][...]
        for dr, acc in zip(ins[1:], accs):
            d = lax.dot_general(xv, dr[...], _TN, preferred_element_type=F32)

            @pl.when(t == 0)
            def _():
                acc[...] = d

            @pl.when(t > 0)
            def _():
                acc[...] += d

        @pl.when(t == t_steps - 1)
        def _():
            for acc, orf in zip(accs, outs):
                val = acc[...] * scale if scale != 1.0 else acc[...]
                orf[...] = val.astype(orf.dtype)

        end()

    res = pl.pallas_call(
        body, name=name, grid=(n_blk, t_steps),
        in_specs=[x_spec(tt)] + list(dy_specs(tt)) + [ANY] * nr,
        out_specs=list(out_specs) + [ANY] * nr,
        out_shape=list(out_shapes) + [r.out_shape() for r in riders],
        scratch_shapes=acc_shapes + _comm_sems() * nr,
        compiler_params=_params("arbitrary", "arbitrary"),
    )(x, *dy_list, *[r.src for r in riders])
    return res[:nd], res[nd:]


def rms_fwd(x, g, name):
    t_len, d = x.shape
    tm = _row_tile(t_len, 512)

    def body(x_ref, g_ref, h_ref):
        xv = x_ref[...]
        r = lax.rsqrt(jnp.mean(xv * xv, axis=-1, keepdims=True) + EPS)
        h_ref[...] = (xv * r * g_ref[...]).astype(BF16)

    row = pl.BlockSpec((tm, d), lambda i: (i, 0))
    return pl.pallas_call(
        body, name=name, grid=(t_len // tm,),
        in_specs=[row, pl.BlockSpec((1, d), lambda i: (0, 0))],
        out_specs=row, out_shape=_sds((t_len, d), BF16),
        compiler_params=_params("parallel"),
    )(x, g)


def rms_bwd(x, g, dh, dres, name):
    t_len, d = x.shape
    tm = _row_tile(t_len, 512)

    def body(x_ref, g_ref, dh_ref, dr_ref, dx_ref, dxb_ref, dg_ref):
        i = pl.program_id(0)
        xv = x_ref[...]
        r = lax.rsqrt(jnp.mean(xv * xv, axis=-1, keepdims=True) + EPS)
        xh = xv * r
        dhv = dh_ref[...]

        @pl.when(i == 0)
        def _():
            dg_ref[...] = jnp.zeros_like(dg_ref)

        dg_ref[...] += jnp.sum(dhv * xh, axis=0, keepdims=True)
        dxh = dhv * g_ref[...]
        dx = dr_ref[...] + r * (dxh - xh * jnp.mean(dxh * xh, axis=-1, keepdims=True))
        dx_ref[...] = dx
        dxb_ref[...] = dx.astype(BF16)

    row = pl.BlockSpec((tm, d), lambda i: (i, 0))
    vec = pl.BlockSpec((1, d), lambda i: (0, 0))
    return pl.pallas_call(
        body, name=name, grid=(t_len // tm,),
        in_specs=[row, vec, row, row],
        out_specs=[row, row, vec],
        out_shape=[_sds((t_len, d), F32), _sds((t_len, d), BF16), _sds((1, d), F32)],
        compiler_params=_params("arbitrary"),
    )(x, g, dh, dres)


def loss_head(y, target, name):
    t_len, d = y.shape
    tm = _row_tile(t_len, 512)

    def body(y_ref, t_ref, l_ref, dy_ref, dyb_ref):
        i = pl.program_id(0)
        err = y_ref[...] - t_ref[...]

        @pl.when(i == 0)
        def _():
            l_ref[...] = jnp.zeros_like(l_ref)

        rows = jnp.sum(err * err, axis=-1, keepdims=True) * (1.0 / d)
        l_ref[...] += 0.5 * jnp.sum(rows, axis=0, keepdims=True)
        dy = err * (1.0 / d)
        dy_ref[...] = dy
        dyb_ref[...] = dy.astype(BF16)

    row = pl.BlockSpec((tm, d), lambda i: (i, 0))
    return pl.pallas_call(
        body, name=name, grid=(t_len // tm,),
        in_specs=[row, row],
        out_specs=[pl.BlockSpec((8, LANES), lambda i: (0, 0)), row, row],
        out_shape=[_sds((8, LANES), F32), _sds((t_len, d), F32), _sds((t_len, d), BF16)],
        compiler_params=_params("arbitrary"),
    )(y, target)


def _conv_specs(tm, ch):
    per = tm // HALO
    cur = lambda cb: pl.BlockSpec((tm, ch), lambda i: (i, cb))
    prev = lambda cb: pl.BlockSpec((HALO, ch), lambda i: (jnp.maximum(i * per - 1, 0), cb))
    return [cur(0), cur(1), prev(0), prev(1)]


def _fill_glu(ext, a_ref, gt_ref, ap_ref, gp_ref, i, tm):
    vp = ap_ref[...] * _sigmoid(gp_ref[...])
    ext[0:HALO, :] = jnp.where(i > 0, vp, 0.0)
    ext[HALO:HALO + tm, :] = a_ref[...] * _sigmoid(gt_ref[...])


def _conv_rows(ext, w_ref, b_ref, r0):
    acc = jnp.broadcast_to(b_ref[...], (ROW_CHUNK, b_ref.shape[1]))
    for k in range(CONV_WIDTH):
        acc = acc + w_ref[k:k + 1, :] * ext[pl.ds(r0 + HALO - (CONV_WIDTH - 1) + k, ROW_CHUNK), :]
    return acc


def _layer_norm(yv):
    mu = jnp.mean(yv, axis=-1, keepdims=True)
    cen = yv - mu
    var = jnp.mean(cen * cen, axis=-1, keepdims=True)
    rstd = lax.rsqrt(var + EPS)
    return cen * rstd, rstd


def conv_fwd(z, w, b, lg, lb, name):
    t_len = z.shape[0]
    ch = w.shape[1]
    tm = _row_tile(t_len, 256)

    def body(a_ref, gt_ref, ap_ref, gp_ref, w_ref, b_ref, lg_ref, lb_ref, y_ref, ext):
        i = pl.program_id(0)
        _fill_glu(ext, a_ref, gt_ref, ap_ref, gp_ref, i, tm)
        for r0 in range(0, tm, ROW_CHUNK):
            xh, _ = _layer_norm(_conv_rows(ext, w_ref, b_ref, r0))
            u = xh * lg_ref[...] + lb_ref[...]
            y_ref[r0:r0 + ROW_CHUNK, :] = (u * _sigmoid(u)).astype(BF16)

    vec = pl.BlockSpec((1, ch), lambda i: (0, 0))
    return pl.pallas_call(
        body, name=name, grid=(t_len // tm,),
        in_specs=_conv_specs(tm, ch) + [pl.BlockSpec((32, ch), lambda i: (0, 0)), vec, vec, vec],
        out_specs=pl.BlockSpec((tm, ch), lambda i: (i, 0)),
        out_shape=_sds((t_len, ch), BF16),
        scratch_shapes=[pltpu.VMEM((HALO + tm, ch), F32)],
        compiler_params=_params("parallel"),
    )(z, z, z, z, w, b, lg, lb)


def conv_bwd_norm(z, dy_cat, w, b, lg, lb, name):
    t_len = z.shape[0]
    ch = w.shape[1]
    tm = _row_tile(t_len, 256)

    def body(a_ref, gt_ref, ap_ref, gp_ref, dy_ref, w_ref, b_ref, lg_ref, lb_ref,
             dc_ref, dlg_ref, dlb_ref, db_ref, ext):
        i = pl.program_id(0)
        _fill_glu(ext, a_ref, gt_ref, ap_ref, gp_ref, i, tm)

        @pl.when(i == 0)
        def _():
            dlg_ref[...] = jnp.zeros_like(dlg_ref)
            dlb_ref[...] = jnp.zeros_like(dlb_ref)
            db_ref[...] = jnp.zeros_like(db_ref)

        for r0 in range(0, tm, ROW_CHUNK):
            xh, rstd = _layer_norm(_conv_rows(ext, w_ref, b_ref, r0))
            u = xh * lg_ref[...] + lb_ref[...]
            sg = _sigmoid(u)
            du = dy_ref[r0:r0 + ROW_CHUNK, :] * (sg * (1.0 + u * (1.0 - sg)))
            dlg_ref[...] += jnp.sum(du * xh, axis=0, keepdims=True)
            dlb_ref[...] += jnp.sum(du, axis=0, keepdims=True)
            dxh = du * lg_ref[...]
            dc = rstd * (dxh - jnp.mean(dxh, axis=-1, keepdims=True)
                         - xh * jnp.mean(dxh * xh, axis=-1, keepdims=True))
            db_ref[...] += jnp.sum(dc, axis=0, keepdims=True)
            dc_ref[r0:r0 + ROW_CHUNK, :] = dc

    vec = pl.BlockSpec((1, ch), lambda i: (0, 0))
    row = pl.BlockSpec((tm, ch), lambda i: (i, 0))
    return pl.pallas_call(
        body, name=name, grid=(t_len // tm,),
        in_specs=_conv_specs(tm, ch) + [row, pl.BlockSpec((32, ch), lambda i: (0, 0)), vec, vec, vec],
        out_specs=[row, vec, vec, vec],
        out_shape=[_sds((t_len, ch), F32)] + [_sds((1, ch), F32)] * 3,
        scratch_shapes=[pltpu.VMEM((HALO + tm, ch), F32)],
        compiler_params=_params("arbitrary"),
    )(z, z, z, z, dy_cat, w, b, lg, lb)


def conv_bwd_taps(z, dc, w, name):
    t_len = z.shape[0]
    ch = w.shape[1]
    tm = _row_tile(t_len, 256)
    per = tm // HALO
    n_tiles = t_len // tm
    last_halo = t_len // HALO - 1

    def body(a_ref, gt_ref, ap_ref, gp_ref, dc_ref, dn_ref, w_ref, dz_a_ref, dz_g_ref, dw_ref, ext, dext):
        i = pl.program_id(0)
        _fill_glu(ext, a_ref, gt_ref, ap_ref, gp_ref, i, tm)
        dext[0:tm, :] = dc_ref[...]
        dext[tm:tm + HALO, :] = jnp.where(i < n_tiles - 1, dn_ref[...], 0.0)

        @pl.when(i == 0)
        def _():
            dw_ref[...] = jnp.zeros_like(dw_ref)

        for r0 in range(0, tm, ROW_CHUNK):
            dcv = dext[r0:r0 + ROW_CHUNK, :]
            dv = jnp.zeros((ROW_CHUNK, ch), F32)
            for k in range(CONV_WIDTH):
                dv = dv + w_ref[k:k + 1, :] * dext[pl.ds(r0 + (CONV_WIDTH - 1) - k, ROW_CHUNK), :]
                prod = dcv * ext[pl.ds(r0 + HALO - (CONV_WIDTH - 1) + k, ROW_CHUNK), :]
                fold = prod[0:8]
                for s in range(8, ROW_CHUNK, 8):
                    fold = fold + prod[s:s + 8]
                dw_ref[k] += fold
            av = a_ref[r0:r0 + ROW_CHUNK, :]
            sg = _sigmoid(gt_ref[r0:r0 + ROW_CHUNK, :])
            dz_a_ref[r0:r0 + ROW_CHUNK, :] = (dv * sg).astype(BF16)
            dz_g_ref[r0:r0 + ROW_CHUNK, :] = (dv * av * sg * (1.0 - sg)).astype(BF16)

    row = pl.BlockSpec((tm, ch), lambda i: (i, 0))
    nxt = pl.BlockSpec((HALO, ch), lambda i: (jnp.minimum((i + 1) * per, last_halo), 0))
    return pl.pallas_call(
        body, name=name, grid=(n_tiles,),
        in_specs=_conv_specs(tm, ch) + [row, nxt, pl.BlockSpec((32, ch), lambda i: (0, 0))],
        out_specs=[row, row, pl.BlockSpec((32, 8, ch), lambda i: (0, 0, 0))],
        out_shape=[_sds((t_len, ch), BF16), _sds((t_len, ch), BF16), _sds((32, 8, ch), F32)],
        scratch_shapes=[pltpu.VMEM((HALO + tm, ch), F32), pltpu.VMEM((tm + HALO, ch), F32)],
        compiler_params=_params("arbitrary"),
    )(z, z, z, z, dc, dc, w)


def _head_masks(rows):
    lane = lax.broadcasted_iota(jnp.int32, (rows, LANES), 1)
    low = lane < HEAD_DIM
    return low, jnp.logical_not(low)


def _per_head_mean(val, low):
    s_low = jnp.sum(jnp.where(low, val, 0.0), axis=-1, keepdims=True)
    s_high = jnp.sum(jnp.where(low, 0.0, val), axis=-1, keepdims=True)
    return jnp.where(low, s_low, s_high) * (1.0 / HEAD_DIM)


def qk_norm_fwd(z, g2, ch, name):
    t_len = z.shape[0]
    tm = _row_tile(t_len, 1024)
    n_col = 2 * ch // LANES
    z_off = 2 * ch // LANES

    def body(z_ref, g_ref, o_ref):
        low, _ = _head_masks(tm)
        xv = z_ref[...]
        r = lax.rsqrt(_per_head_mean(xv * xv, low) + EPS)
        o_ref[...] = xv * r * g_ref[...]

    return pl.pallas_call(
        body, name=name, grid=(t_len // tm, n_col),
        in_specs=[pl.BlockSpec((tm, LANES), lambda i, cb: (i, z_off + cb)),
                  pl.BlockSpec((1, LANES), lambda i, cb: (0, cb))],
        out_specs=pl.BlockSpec((tm, LANES), lambda i, cb: (i, cb)),
        out_shape=_sds((t_len, 2 * ch), F32),
        compiler_params=_params("parallel", "parallel"),
    )(z, g2)


def qk_norm_bwd(z, g, d_list, z_off, ch, name):
    t_len = z.shape[0]
    tm = _row_tile(t_len, 1024)
    n_col = ch // LANES
    nd = len(d_list)

    def body(*refs):
        z_ref, g_ref, d_refs = refs[0], refs[1], refs[2:2 + nd]
        dz_ref, dg_ref = refs[2 + nd], refs[3 + nd]
        first = jnp.logical_and(pl.program_id(0) == 0, pl.program_id(1) == 0)
        low, _ = _head_masks(tm)
        xv = z_ref[...]
        r = lax.rsqrt(_per_head_mean(xv * xv, low) + EPS)
        xh = xv * r
        dy = d_refs[0][...]
        for dr in d_refs[1:]:
            dy = dy + dr[...]

        @pl.when(first)
        def _():
            dg_ref[...] = jnp.zeros_like(dg_ref)

        dg_ref[...] += jnp.sum(dy * xh, axis=0, keepdims=True)
        dxh = dy * g_ref[...]
        dz_ref[...] = (r * (dxh - xh * _per_head_mean(dxh * xh, low))).astype(BF16)

    blk = pl.BlockSpec((tm, LANES), lambda i, cb: (i, cb))
    return pl.pallas_call(
        body, name=name, grid=(t_len // tm, n_col),
        in_specs=[pl.BlockSpec((tm, LANES), lambda i, cb: (i, z_off + cb)),
                  pl.BlockSpec((1, LANES), lambda i, cb: (0, 0))] + [blk] * nd,
        out_specs=[blk, pl.BlockSpec((1, LANES), lambda i, cb: (0, 0))],
        out_shape=[_sds((t_len, ch), BF16), _sds((1, LANES), F32)],
        compiler_params=_params("arbitrary", "arbitrary"),
    )(z, g, *d_list)


def _alibi_bias(n_heads, dilation):
    slopes = 2.0 ** (-ALIBI_MAX_BIAS * jnp.arange(1, n_heads + 1, dtype=F32) / n_heads)
    qi = jnp.arange(ATT_BLOCK)[:, None]
    kj = jnp.arange(ATT_BLOCK)[None, :]
    dist_cur = (qi - kj).astype(F32)
    dist_prev = (ATT_BLOCK + qi - kj).astype(F32)
    cur = jnp.where((qi >= kj)[None], -slopes[:, None, None] * (dilation * dist_cur)[None], MASKED)
    prev = jnp.where((kj >= qi)[None], -slopes[:, None, None] * (dilation * dist_prev)[None], MASKED)
    return jnp.concatenate([prev, cur], axis=-1).astype(F32)


def _stack_heads(val, low, high):
    return jnp.concatenate([jnp.where(low, val, 0.0), jnp.where(high, val, 0.0)], axis=0).astype(BF16)


def _head_rows(val, low, high):
    return jnp.concatenate([jnp.max(jnp.where(low, val, MASKED), axis=-1, keepdims=True),
                            jnp.max(jnp.where(high, val, MASKED), axis=-1, keepdims=True)], axis=0)


def _unit_scores(q2, k2, b_ref, has_prev, scale):
    s = lax.dot_general(q2, k2, _NT, preferred_element_type=F32) * scale
    s = s + b_ref[...].reshape(2 * ATT_BLOCK, 2 * ATT_BLOCK)
    col = lax.broadcasted_iota(jnp.int32, s.shape, 1)
    return jnp.where(jnp.logical_or(has_prev, col >= ATT_BLOCK), s, MASKED)


def _strided_rows(r, dilation):
    per = ATT_CHUNK // dilation
    return pl.ds(r, per, stride=dilation) if dilation > 1 else pl.ds(0, per)


def _deinterleave(dst, src_ref, dilation, base=None, dtype=None):
    per = ATT_CHUNK // dilation
    for r in range(dilation):
        val = src_ref[_strided_rows(r, dilation), :]
        val = val if dtype is None else val.astype(dtype)
        if base is None:
            dst[r * per:(r + 1) * per, :] = val
        else:
            dst[pl.ds(pl.multiple_of(base + r * per, ATT_BLOCK), per), :] = val


def _unit_rows(u, c, nb, base, pbase):
    in_chunk = lax.rem(u, jnp.int32(nb)) > 0
    has_prev = jnp.logical_or(in_chunk, c > 0)
    urow = pl.multiple_of(u * ATT_BLOCK, ATT_BLOCK)
    crow = pl.multiple_of(base + u * ATT_BLOCK, ATT_BLOCK)
    prow = pl.multiple_of(jnp.where(in_chunk, base + (u - 1) * ATT_BLOCK,
                                    pbase + (u + nb - 1) * ATT_BLOCK), ATT_BLOCK)
    return in_chunk, has_prev, urow, crow, prow


def _interleave(dst_ref, src, dilation, base=None):
    per = ATT_CHUNK // dilation
    for r in range(dilation):
        if base is None:
            val = src[r * per:(r + 1) * per, :]
        else:
            val = src[pl.ds(pl.multiple_of(base + r * per, ATT_BLOCK), per), :]
        dst_ref[_strided_rows(r, dilation), :] = val


def attn_fwd(qk, z, dilation, ch, name):
    t_len = qk.shape[0]
    pairs = ch // LANES
    nc = t_len // ATT_CHUNK
    nb = ATT_UNITS // dilation
    scale = 1.0 / math.sqrt(HEAD_DIM)
    bias = _alibi_bias(2 * pairs, dilation)

    def body(q_ref, k_ref, v_ref, b_ref, o_ref, l_ref, qd, kx, vx, od, ld):
        c = pl.program_id(1)
        slot = lax.rem(c, jnp.int32(2))
        base, pbase = slot * ATT_CHUNK, (1 - slot) * ATT_CHUNK

        @pl.when(c == 0)
        def _():
            kx[...] = jnp.zeros_like(kx)
            vx[...] = jnp.zeros_like(vx)

        _deinterleave(qd, q_ref, dilation)
        _deinterleave(kx, k_ref, dilation, base, BF16)
        _deinterleave(vx, v_ref, dilation, base, BF16)
        low, high = _head_masks(ATT_BLOCK)

        def unit(u, carry):
            _, has_prev, urow, crow, prow = _unit_rows(u, c, nb, base, pbase)
            q2 = _stack_heads(qd[pl.ds(urow, ATT_BLOCK), :], low, high)
            k2 = jnp.concatenate([kx[pl.ds(prow, ATT_BLOCK), :], kx[pl.ds(crow, ATT_BLOCK), :]], axis=0)
            v2 = jnp.concatenate([vx[pl.ds(prow, ATT_BLOCK), :], vx[pl.ds(crow, ATT_BLOCK), :]], axis=0)
            s = _unit_scores(q2, k2, b_ref, has_prev, scale)
            mx = jnp.max(s, axis=-1, keepdims=True)
            e = jnp.exp(s - mx)
            den = jnp.sum(e, axis=-1, keepdims=True)
            acc = lax.dot_general(e.astype(BF16), v2, _NN, preferred_element_type=F32) / den
            lse = jnp.broadcast_to(mx + jnp.log(den), acc.shape)
            od[pl.ds(urow, ATT_BLOCK), :] = jnp.where(low, acc[:ATT_BLOCK], acc[ATT_BLOCK:])
            ld[pl.ds(urow, ATT_BLOCK), :] = jnp.where(low, lse[:ATT_BLOCK], lse[ATT_BLOCK:])
            return carry

        lax.fori_loop(0, ATT_UNITS, unit, 0, unroll=2)
        _interleave(o_ref, od, dilation)
        _interleave(l_ref, ld, dilation)

    blk = (ATT_CHUNK, LANES)
    bias_spec = pl.BlockSpec((2, ATT_BLOCK, 2 * ATT_BLOCK), lambda p, c: (p, 0, 0))
    out_spec = pl.BlockSpec(blk, lambda p, c: (c, p))
    return pl.pallas_call(
        body, name=name, grid=(pairs, nc),
        in_specs=[pl.BlockSpec(blk, lambda p, c: (c, p)),
                  pl.BlockSpec(blk, lambda p, c: (c, pairs + p)),
                  pl.BlockSpec(blk, lambda p, c: (c, 4 * pairs + p)),
                  bias_spec],
        out_specs=[out_spec, out_spec],
        out_shape=[_sds((t_len, ch), F32)] * 2,
        scratch_shapes=[pltpu.VMEM((ATT_CHUNK, LANES), F32),
                        pltpu.VMEM((2 * ATT_CHUNK, LANES), BF16), pltpu.VMEM((2 * ATT_CHUNK, LANES), BF16),
                        pltpu.VMEM((ATT_CHUNK, LANES), F32), pltpu.VMEM((ATT_CHUNK, LANES), F32)],
        compiler_params=_params("arbitrary", "arbitrary"),
    )(qk, qk, z, bias)


def attn_combine(outs, lses, name):
    t_len, ch = outs[0].shape
    tm = _row_tile(t_len, 512)

    def body(o1, o2, o3, l1, l2, l3, out_ref, outb_ref, lg_ref):
        a, b, c = l1[...], l2[...], l3[...]
        mx = jnp.maximum(jnp.maximum(a, b), c)
        tot = mx + jnp.log(jnp.exp(a - mx) + jnp.exp(b - mx) + jnp.exp(c - mx))
        val = jnp.exp(a - tot) * o1[...] + jnp.exp(b - tot) * o2[...] + jnp.exp(c - tot) * o3[...]
        out_ref[...] = val
        outb_ref[...] = val.astype(BF16)
        lg_ref[...] = tot

    row = pl.BlockSpec((tm, ch), lambda i: (i, 0))
    return pl.pallas_call(
        body, name=name, grid=(t_len // tm,),
        in_specs=[row] * 6, out_specs=[row] * 3,
        out_shape=[_sds((t_len, ch), F32), _sds((t_len, ch), BF16), _sds((t_len, ch), F32)],
        compiler_params=_params("parallel"),
    )(*outs, *lses)


def attn_bwd(qk, z, dy_cat, out, lg, dilation, ch, name):
    t_len = qk.shape[0]
    pairs = ch // LANES
    nc = t_len // ATT_CHUNK
    nb = ATT_UNITS // dilation
    scale = 1.0 / math.sqrt(HEAD_DIM)
    bias = _alibi_bias(2 * pairs, dilation)

    def body(q_ref, k_ref, v_ref, do_ref, out_ref, lg_ref, b_ref, dq_ref, dk_ref, dv_ref,
             qd, dod, lgd, dld, dl_nat, kx, vx, dkx, dvx, dqd):
        c = pl.program_id(1)
        slot = lax.rem(c, jnp.int32(2))
        base, pbase = slot * ATT_CHUNK, (1 - slot) * ATT_CHUNK

        @pl.when(c == 0)
        def _():
            for ref in (kx, vx, dkx, dvx):
                ref[...] = jnp.zeros_like(ref)

        @pl.when(c < nc)
        def _():
            low_all, _ = _head_masks(ATT_CHUNK)
            dl_nat[...] = _per_head_mean(do_ref[...] * out_ref[...], low_all) * float(HEAD_DIM)
            _deinterleave(qd, q_ref, dilation)
            _deinterleave(dod, do_ref, dilation)
            _deinterleave(lgd, lg_ref, dilation)
            _deinterleave(dld, dl_nat, dilation)
            _deinterleave(kx, k_ref, dilation, base, BF16)
            _deinterleave(vx, v_ref, dilation, base, BF16)
            cur = pl.ds(pl.multiple_of(base, ATT_CHUNK), ATT_CHUNK)
            dkx[cur, :] = jnp.zeros((ATT_CHUNK, LANES), F32)
            dvx[cur, :] = jnp.zeros((ATT_CHUNK, LANES), F32)
            low, high = _head_masks(ATT_BLOCK)

            def unit(u, carry):
                _, has_prev, urow, crow, prow = _unit_rows(u, c, nb, base, pbase)
                rows = pl.ds(urow, ATT_BLOCK)
                q2 = _stack_heads(qd[rows, :], low, high)
                do2 = _stack_heads(dod[rows, :], low, high)
                lse = _head_rows(lgd[rows, :], low, high)
                delta = _head_rows(dld[rows, :], low, high)
                k2 = jnp.concatenate([kx[pl.ds(prow, ATT_BLOCK), :], kx[pl.ds(crow, ATT_BLOCK), :]], axis=0)
                v2 = jnp.concatenate([vx[pl.ds(prow, ATT_BLOCK), :], vx[pl.ds(crow, ATT_BLOCK), :]], axis=0)
                prob = jnp.exp(_unit_scores(q2, k2, b_ref, has_prev, scale) - lse)
                dp = lax.dot_general(do2, v2, _NT, preferred_element_type=F32)
                ds = (prob * (dp - delta)).astype(BF16)
                dq2 = lax.dot_general(ds, k2, _NN, preferred_element_type=F32)
                dk2 = lax.dot_general(ds, q2, _TN, preferred_element_type=F32)
                dv2 = lax.dot_general(prob.astype(BF16), do2, _TN, preferred_element_type=F32)
                dqd[rows, :] = scale * jnp.where(low, dq2[:ATT_BLOCK], dq2[ATT_BLOCK:])
                dkx[pl.ds(prow, ATT_BLOCK), :] += scale * dk2[:ATT_BLOCK]
                dkx[pl.ds(crow, ATT_BLOCK), :] += scale * dk2[ATT_BLOCK:]
                dvx[pl.ds(prow, ATT_BLOCK), :] += dv2[:ATT_BLOCK]
                dvx[pl.ds(crow, ATT_BLOCK), :] += dv2[ATT_BLOCK:]
                return carry

            lax.fori_loop(0, ATT_UNITS, unit, 0, unroll=2)
            _interleave(dq_ref, dqd, dilation)

        @pl.when(c > 0)
        def _():
            _interleave(dk_ref, dkx, dilation, pbase)
            _interleave(dv_ref, dvx, dilation, pbase)

    blk = (ATT_CHUNK, LANES)
    here = lambda c: jnp.minimum(c, nc - 1)
    spec = lambda off: pl.BlockSpec(blk, lambda p, c: (here(c), off + p))
    late = pl.BlockSpec(blk, lambda p, c: (jnp.maximum(c - 1, 0), p))
    bias_spec = pl.BlockSpec((2, ATT_BLOCK, 2 * ATT_BLOCK), lambda p, c: (p, 0, 0))
    f32_chunk = pltpu.VMEM((ATT_CHUNK, LANES), F32)
    return pl.pallas_call(
        body, name=name, grid=(pairs, nc + 1),
        in_specs=[spec(0), spec(pairs), spec(4 * pairs), spec(pairs), spec(0), spec(0), bias_spec],
        out_specs=[spec(0), late, late],
        out_shape=[_sds((t_len, ch), F32)] * 3,
        scratch_shapes=[f32_chunk] * 5
                       + [pltpu.VMEM((2 * ATT_CHUNK, LANES), BF16)] * 2
                       + [pltpu.VMEM((2 * ATT_CHUNK, LANES), F32)] * 2 + [f32_chunk],
        compiler_params=_params("arbitrary", "arbitrary"),
    )(qk, qk, z, dy_cat, out, lg, bias)


def sum3_bf16(a, b, c, name):
    t_len, ch = a.shape
    tm = _row_tile(t_len, 512)

    def body(a_ref, b_ref, c_ref, o_ref):
        o_ref[...] = (a_ref[...] + b_ref[...] + c_ref[...]).astype(BF16)

    row = pl.BlockSpec((tm, ch), lambda i: (i, 0))
    return pl.pallas_call(
        body, name=name, grid=(t_len // tm,), in_specs=[row] * 3, out_specs=row,
        out_shape=_sds((t_len, ch), BF16), compiler_params=_params("parallel"),
    )(a, b, c)


def _blk3(rows, cols):
    return pl.BlockSpec((None, rows, cols), lambda j, t: (j, 0, 0))


def ffn_up(h, wgu, name, riders):
    t_len, d = h.shape
    n_blk, _, _, fj = wgu.shape

    def epilogue(accs, e_refs, o_refs):
        gate, up = accs
        o_refs[0][...] = gate.astype(BF16)
        o_refs[1][...] = up.astype(BF16)
        o_refs[2][...] = (gate * _sigmoid(gate) * up).astype(BF16)

    w_spec = lambda i: pl.BlockSpec((None, None, d, fj), lambda j, t: (j, i, 0, 0))
    act = lambda tm: pl.BlockSpec((None, tm, fj), lambda j, t: (j, t, 0))
    return mm_cols(name, h, [wgu, wgu], [w_spec(0), w_spec(1)], False, [], lambda tm: [],
                   [_sds((n_blk, t_len, fj), BF16)] * 3, lambda tm: [act(tm)] * 3, epilogue, n_blk, riders)


def ffn_down(act, wd, res, name, riders):
    n_blk, t_len, fj = act.shape
    d = wd.shape[2]
    return mm_reduce(name, [act], lambda tm: [pl.BlockSpec((None, tm, fj), lambda t, j: (j, t, 0))],
                     [wd], [pl.BlockSpec((None, fj, d), lambda t, j: (j, 0, 0))], False, res, 0.5,
                     t_len, d, n_blk, riders)


def ffn_bwd(h, gate, up, act, wgu, wd, dyb, name):
    t_len, d = h.shape
    n_blk, _, fj = act.shape

    def epilogue(accs, e_refs, o_refs):
        d_act = 0.5 * accs[0]
        gv, uv = e_refs[0][...].astype(F32), e_refs[1][...].astype(F32)
        sg = _sigmoid(gv)
        o_refs[0][...] = (d_act * uv * (sg * (1.0 + gv * (1.0 - sg)))).astype(BF16)
        o_refs[1][...] = (d_act * gv * sg).astype(BF16)

    act_jt = lambda tm: pl.BlockSpec((None, tm, fj), lambda j, t: (j, t, 0))
    (d_gate, d_up), _ = mm_cols(name + "_dact", dyb, [wd], [_blk3(fj, d)], True, [gate, up],
                                lambda tm: [act_jt(tm)] * 2, [_sds((n_blk, t_len, fj), BF16)] * 2,
                                lambda tm: [act_jt(tm)] * 2, epilogue, n_blk)

    (d_wd,), _ = mm_tn(name + "_dwd", act, act_jt, [dyb],
                       lambda tt: [pl.BlockSpec((tt, d), lambda j, t: (t, 0))],
                       [_sds((n_blk, fj, d), BF16)], [_blk3(fj, d)], 0.5, t_len, n_blk)

    (d_wg, d_wu), (recv_wd,) = mm_tn(
        name + "_dwgu", h, lambda tt: pl.BlockSpec((tt, d), lambda j, t: (t, 0)),
        [d_gate, d_up], lambda tt: [act_jt(tt)] * 2,
        [_sds((n_blk, d, fj), BF16)] * 2, [_blk3(d, fj)] * 2, 1.0, t_len, n_blk,
        [Rider("scatter", d_wd)])

    act_tj = lambda tm: pl.BlockSpec((None, tm, fj), lambda t, j: (j, t, 0))
    w_in = lambda i: pl.BlockSpec((None, None, d, fj), lambda t, j: (j, i, 0, 0))
    dh, (recv_wg, recv_wu) = mm_reduce(
        name + "_dh", [d_gate, d_up], lambda tm: [act_tj(tm)] * 2,
        [wgu, wgu], [w_in(0), w_in(1)], True, None, 1.0, t_len, d, n_blk,
        [Rider("scatter", d_wg), Rider("scatter", d_wu)])
    return dh, recv_wg, recv_wu, recv_wd


def local_step(x, target, g1, wgu1, wd1_s, gmix, win_s, conv_w, conv_b, ln_g, ln_b, gq, gk, wout_s, g3,
               wgu2_s, wd2_s):
    t_len, d = x.shape
    ch = d // 2
    ij = win_s.shape[1]
    oj = wout_s.shape[0]
    n_blk = N_DEV

    h1 = rms_fwd(x, g1, "rms1")
    (gate1, up1, act1), (wd1,) = ffn_up(h1, wgu1, "ffn1_up", [Rider("gather", wd1_s)])
    x1, (win,) = ffn_down(act1, wd1, x, "ffn1_down", [Rider("gather", win_s)])

    h2 = rms_fwd(x1, gmix, "rms_mix")

    def store_f32(accs, e_refs, o_refs):
        o_refs[0][...] = accs[0]

    (z,), (wout, wgu2) = mm_cols(
        "w_in", h2, [win], [_blk3(d, ij)], False, [], lambda tm: [],
        [_sds((t_len, n_blk * ij), F32)],
        lambda tm: [pl.BlockSpec((tm, ij), lambda j, t: (t, j))], store_f32, n_blk,
        [Rider("gather", wout_s), Rider("gather", wgu2_s)])

    conv_w32 = jnp.pad(conv_w, ((0, 32 - CONV_WIDTH), (0, 0)))
    y_conv = conv_fwd(z, conv_w32, conv_b, ln_g, ln_b, "conv_fwd")

    g2 = jnp.concatenate([jnp.tile(gq, (1, ch // HEAD_DIM)), jnp.tile(gk, (1, ch // HEAD_DIM))], axis=1)
    qk = qk_norm_fwd(z, g2, ch, "qk_norm")
    branch = [attn_fwd(qk, z, dil, ch, "attn_fwd_d%d" % dil) for dil in DILATIONS]
    att, att_b, lg = attn_combine([o for o, _ in branch], [l for _, l in branch], "attn_combine")

    y_cat = jnp.concatenate([y_conv, att_b], axis=1)
    x2, (wd2,) = mm_reduce(
        "w_out", [y_cat], lambda tm: [pl.BlockSpec((tm, oj), lambda t, j: (t, j))],
        [wout], [pl.BlockSpec((None, oj, d), lambda t, j: (j, 0, 0))], False, x1, 1.0,
        t_len, d, n_blk, [Rider("gather", wd2_s)])

    h3 = rms_fwd(x2, g3, "rms3")
    (gate2, up2, act2), _ = ffn_up(h3, wgu2, "ffn2_up", [])
    y, _ = ffn_down(act2, wd2, x2, "ffn2_down", [])

    loss_tile, dy, dyb = loss_head(y, target, "loss")

    dh3, recv_wg2, recv_wu2, recv_wd2 = ffn_bwd(h3, gate2, up2, act2, wgu2, wd2, dyb, "ffn2")
    dx2, dx2b, d_g3 = rms_bwd(x2, g3, dh3, dy, "rms3_bwd")

    (dy_cat,), _ = mm_cols("w_out_dy", dx2b, [wout], [_blk3(oj, d)], True, [], lambda tm: [],
                           [_sds((t_len, n_blk * oj), F32)],
                           lambda tm: [pl.BlockSpec((tm, oj), lambda j, t: (t, j))], store_f32, n_blk)
    (d_wout,), _ = mm_tn("w_out_dw", y_cat, lambda tt: pl.BlockSpec((tt, oj), lambda j, t: (t, j)),
                         [dx2b], lambda tt: [pl.BlockSpec((tt, d), lambda j, t: (t, 0))],
                         [_sds((n_blk, oj, d), BF16)], [_blk3(oj, d)], 1.0, t_len, n_blk)

    dc, d_lg, d_lb, d_cb = conv_bwd_norm(z, dy_cat, conv_w32, conv_b, ln_g, ln_b, "conv_bwd_norm")
    dz_a, dz_g, d_cw8 = conv_bwd_taps(z, dc, conv_w32, "conv_bwd_taps")
    d_cw = jnp.sum(d_cw8, axis=1)[:CONV_WIDTH]

    grads = [attn_bwd(qk, z, dy_cat, att, lg, dil, ch, "attn_bwd_d%d" % dil) for dil in DILATIONS]
    gq_t = jnp.tile(gq, (1, LANES // HEAD_DIM))
    gk_t = jnp.tile(gk, (1, LANES // HEAD_DIM))
    dz_q, d_gq2 = qk_norm_bwd(z, gq_t, [g[0] for g in grads], 2 * ch // LANES, ch, "q_norm_bwd")
    dz_k, d_gk2 = qk_norm_bwd(z, gk_t, [g[1] for g in grads], 3 * ch // LANES, ch, "k_norm_bwd")
    d_gq = d_gq2[:, :HEAD_DIM] + d_gq2[:, HEAD_DIM:]
    d_gk = d_gk2[:, :HEAD_DIM] + d_gk2[:, HEAD_DIM:]
    dz_v = sum3_bf16(grads[0][2], grads[1][2], grads[2][2], "dv_sum")
    dzb = jnp.concatenate([dz_a, dz_g, dz_q, dz_k, dz_v], axis=1)

    (d_win,), (recv_wout,) = mm_tn(
        "w_in_dw", h2, lambda tt: pl.BlockSpec((tt, d), lambda j, t: (t, 0)),
        [dzb], lambda tt: [pl.BlockSpec((tt, ij), lambda j, t: (t, j))],
        [_sds((n_blk, d, ij), BF16)], [_blk3(d, ij)], 1.0, t_len, n_blk, [Rider("scatter", d_wout)])
    dh2, (recv_win,) = mm_reduce(
        "w_in_dh", [dzb], lambda tm: [pl.BlockSpec((tm, ij), lambda t, j: (t, j))],
        [win], [pl.BlockSpec((None, d, ij), lambda t, j: (j, 0, 0))], True, None, 1.0,
        t_len, d, n_blk, [Rider("scatter", d_win)])
    dx1, dx1b, d_gmix = rms_bwd(x1, gmix, dh2, dx2, "rms_mix_bwd")

    dh1, recv_wg1, recv_wu1, recv_wd1 = ffn_bwd(h1, gate1, up1, act1, wgu1, wd1, dx1b, "ffn1")
    grad_x, _, d_g1 = rms_bwd(x, g1, dh1, dx1, "rms1_bwd")

    big = dict(ffn1_w_gate=recv_wg1, ffn1_w_up=recv_wu1, ffn1_w_down=recv_wd1, w_in=recv_win, w_out=recv_wout,
               ffn2_w_gate=recv_wg2, ffn2_w_up=recv_wu2, ffn2_w_down=recv_wd2)
    small = dict(g1=d_g1, gmix=d_gmix, g3=d_g3, conv_b=d_cb, ln_g=d_lg, ln_b=d_lb, gq=d_gq, gk=d_gk, conv_w=d_cw)
    return loss_tile[0, 0], grad_x, big, small


SMALL_ROWS = 48


def _pack_small(ch, g1, gmix, g3, conv_b, ln_g, ln_b, gq, gk, conv_w):
    pad_head = lambda v: jnp.pad(v, ((0, 0), (0, ch - v.shape[1])))
    rows = [g1.reshape(2, ch), gmix.reshape(2, ch), g3.reshape(2, ch), conv_b, ln_g, ln_b,
            pad_head(gq), pad_head(gk), conv_w]
    packed = jnp.concatenate(rows, axis=0)
    return jnp.pad(packed, ((0, SMALL_ROWS - packed.shape[0]), (0, 0)))


def _unpack_small(packed, d):
    return dict(g1=packed[0:2].reshape(1, d), gmix=packed[2:4].reshape(1, d), g3=packed[4:6].reshape(1, d),
                conv_b=packed[6:7], ln_g=packed[7:8], ln_b=packed[8:9],
                gq=packed[9:10, :HEAD_DIM], gk=packed[10:11, :HEAD_DIM])


def kernel(x, ffn1_norm_g, ffn1_w_gate, ffn1_w_up, ffn1_w_down, mix_norm_g, w_in, conv_w_dw, conv_b_dw, conv_ln_g, conv_ln_b, q_norm_g, k_norm_g, w_out, ffn2_norm_g, ffn2_w_gate, ffn2_w_up, ffn2_w_down, loss_target, m_ffn1_norm_g, m_ffn1_w_gate, m_ffn1_w_up, m_ffn1_w_down, m_mix_norm_g, m_w_in, m_conv_w_dw, m_conv_b_dw, m_conv_ln_g, m_conv_ln_b, m_q_norm_g, m_k_norm_g, m_w_out, m_ffn2_norm_g, m_ffn2_w_gate, m_ffn2_w_up, m_ffn2_w_down, v_ffn1_norm_g, v_ffn1_w_gate, v_ffn1_w_up, v_ffn1_w_down, v_mix_norm_g, v_w_in, v_conv_w_dw, v_conv_b_dw, v_conv_ln_g, v_conv_ln_b, v_q_norm_g, v_k_norm_g, v_w_out, v_ffn2_norm_g, v_ffn2_w_gate, v_ffn2_w_up, v_ffn2_w_down):
    d = x.shape[-1]
    ch = d // 2
    me = 4 * lax.axis_index("x") + 2 * lax.axis_index("y") + lax.axis_index("c")

    gu = lambda wg, wu: jnp.stack([wg[0], wu[0]]).astype(BF16)
    wgu1 = all_gather(gu(ffn1_w_gate, ffn1_w_up), "ag_wgu1")
    cw_all = all_gather(conv_w_dw[0], "ag_convw")
    conv_w = jnp.transpose(cw_all, (1, 0, 2)).reshape(CONV_WIDTH, ch)

    loss_part, grad_x, big, small = local_step(
        x[0], loss_target[0], ffn1_norm_g, wgu1, ffn1_w_down[0].astype(BF16), mix_norm_g, w_in[0].astype(BF16),
        conv_w, conv_b_dw, conv_ln_g, conv_ln_b, q_norm_g, k_norm_g, w_out[0].astype(BF16), ffn2_norm_g,
        gu(ffn2_w_gate, ffn2_w_up), ffn2_w_down[0].astype(BF16))
    loss = lax.psum(loss_part, MESH_AXES)

    state = dict(
        ffn1_w_gate=(ffn1_w_gate, m_ffn1_w_gate, v_ffn1_w_gate), ffn1_w_up=(ffn1_w_up, m_ffn1_w_up, v_ffn1_w_up),
        ffn1_w_down=(ffn1_w_down, m_ffn1_w_down, v_ffn1_w_down), w_in=(w_in, m_w_in, v_w_in),
        w_out=(w_out, m_w_out, v_w_out),
        ffn2_w_gate=(ffn2_w_gate, m_ffn2_w_gate, v_ffn2_w_gate), ffn2_w_up=(ffn2_w_up, m_ffn2_w_up, v_ffn2_w_up),
        ffn2_w_down=(ffn2_w_down, m_ffn2_w_down, v_ffn2_w_down))
    out = {}
    for pname, (w, m, v) in state.items():
        res = adamw(w[0], m[0], v[0], big[pname], "adamw_" + pname)
        out[pname] = [r[None] for r in res]

    zero_taps = jnp.zeros((CONV_WIDTH, ch), F32)
    pack = lambda g1, gm, g3, cb, lg, lb, gq, gk: _pack_small(ch, g1, gm, g3, cb, lg, lb, gq, gk, zero_taps)
    small_parts = all_gather(_pack_small(ch, **small), "ag_small_grads")
    s_res = adamw(
        pack(ffn1_norm_g, mix_norm_g, ffn2_norm_g, conv_b_dw, conv_ln_g, conv_ln_b, q_norm_g, k_norm_g),
        pack(m_ffn1_norm_g, m_mix_norm_g, m_ffn2_norm_g, m_conv_b_dw, m_conv_ln_g, m_conv_ln_b, m_q_norm_g, m_k_norm_g),
        pack(v_ffn1_norm_g, v_mix_norm_g, v_ffn2_norm_g, v_conv_b_dw, v_conv_ln_g, v_conv_ln_b, v_q_norm_g, v_k_norm_g),
        small_parts, "adamw_small")
    s_out = [_unpack_small(r, d) for r in s_res]
    names = dict(g1="ffn1_norm_g", gmix="mix_norm_g", g3="ffn2_norm_g", conv_b="conv_b_dw", ln_g="conv_ln_g",
                 ln_b="conv_ln_b", gq="q_norm_g", gk="k_norm_g")
    for key, full in names.items():
        out[full] = [r[key] for r in s_out]

    cshard = ch // N_DEV
    taps_sum = s_res[0][11:11 + CONV_WIDTH]
    taps_mine = lax.dynamic_slice(taps_sum, (0, me * cshard), (CONV_WIDTH, cshard))
    pad_taps = lambda a: jnp.pad(a, ((0, 32 - CONV_WIDTH), (0, 0)))
    c_res = adamw(pad_taps(conv_w_dw[0]), pad_taps(m_conv_w_dw[0]), pad_taps(v_conv_w_dw[0]),
                  pad_taps(taps_mine)[None], "adamw_convw")
    out["conv_w_dw"] = [r[:CONV_WIDTH][None] for r in c_res]

    order = ["ffn1_norm_g", "ffn1_w_gate", "ffn1_w_up", "ffn1_w_down", "mix_norm_g", "w_in", "conv_w_dw",
             "conv_b_dw", "conv_ln_g", "conv_ln_b", "q_norm_g", "k_norm_g", "w_out", "ffn2_norm_g",
             "ffn2_w_gate", "ffn2_w_up", "ffn2_w_down"]
    result = [loss, grad_x[None]]
    for kind in range(4):
        result += [out[n][kind] for n in order]
    return tuple(result)
```

```python
import math
from typing import NamedTuple

import jax
import jax.numpy as jnp
from jax import lax
from jax.experimental import pallas as pl
from jax.experimental.pallas import tpu as pltpu

F32 = jnp.float32
BF16 = jnp.bfloat16

N_DEV = 8
EPS = 1e-6
HEAD_DIM = 64
LANES = 128
CONV_WIDTH = 31
HALO = 32
ROW_CHUNK = 32
ATT_BLOCK = 128
DILATIONS = (1, 4, 16)
ATT_UNITS = 16
ATT_CHUNK = ATT_UNITS * ATT_BLOCK
ALIBI_MAX_BIAS = 8.0
MASKED = -1e30
VMEM_LIMIT = 56 * 1024 * 1024

ADAM_LR = 0.001
ADAM_B1 = 0.9
ADAM_B2 = 0.999
ADAM_EPS = 1e-08
ADAM_WD = 0.01
ADAM_STEP = 10

MESH_AXES = ("x", "y", "c")
ANY = pl.BlockSpec(memory_space=pl.ANY)


def _sds(shape, dtype):
    return jax.ShapeDtypeStruct(tuple(shape), dtype)


def _params(*sem):
    return pltpu.CompilerParams(dimension_semantics=sem, vmem_limit_bytes=VMEM_LIMIT)


def _sigmoid(v):
    return 1.0 / (1.0 + jnp.exp(-v))


def _row_tile(t, want):
    for cand in range(min(want, t) // 8 * 8, 0, -8):
        if t % cand == 0:
            return cand
    return t


def _mesh_pos():
    return lax.axis_index("x"), lax.axis_index("y"), lax.axis_index("c")


def _comm_sems():
    return [pltpu.SemaphoreType.DMA((7,)), pltpu.SemaphoreType.DMA((7,)), pltpu.SemaphoreType.DMA(())]


def _gather_phases(x_ref, out_ref, send_sems, recv_sems, local_sem):
    x, y, c = _mesh_pos()
    me, sibling = (x, y, c), (x, y, 1 - c)
    chips = [(1 - x, y), (x, 1 - y), (1 - x, 1 - y)]

    def slot(px, py, pc):
        return out_ref.at[4 * px + 2 * py + pc]

    def copy(k, block, to, src=None):
        return pltpu.make_async_remote_copy(
            src_ref=slot(*block) if src is None else src, dst_ref=slot(*block),
            send_sem=send_sems.at[k], recv_sem=recv_sems.at[k],
            device_id=to, device_id_type=pl.DeviceIdType.MESH)

    mine = pltpu.make_async_copy(x_ref, slot(*me), local_sem)
    first = [copy(0, me, sibling, src=x_ref)]
    first += [copy(1 + j, me, (*chip, c), src=x_ref) for j, chip in enumerate(chips)]
    passed = [copy(4 + j, (*chip, c), sibling) for j, chip in enumerate(chips)]

    def start():
        mine.start()
        for cp in first:
            cp.start()

    def forward():
        for j, chip in enumerate(chips):
            copy(1 + j, (*chip, c), me).wait_recv()
            passed[j].start()

    def finish():
        copy(0, sibling, me).wait_recv()
        for j, chip in enumerate(chips):
            copy(4 + j, (*chip, 1 - c), me).wait_recv()
        for cp in first + passed:
            cp.wait_send()
        mine.wait()

    return start, forward, finish


def _scatter_phases(p_ref, out_ref, send_sems, recv_sems, local_sem):
    x, y, c = _mesh_pos()
    me = 4 * x + 2 * y + c
    flips = [(fx, fy, fc) for fx in (0, 1) for fy in (0, 1) for fc in (0, 1)][1:]

    def copy(k, flip, receiving):
        px, py, pc = (1 - x if flip[0] else x, 1 - y if flip[1] else y, 1 - c if flip[2] else c)
        them = 4 * px + 2 * py + pc
        return pltpu.make_async_remote_copy(
            src_ref=p_ref.at[them], dst_ref=out_ref.at[them if receiving else me],
            send_sem=send_sems.at[k], recv_sem=recv_sems.at[k],
            device_id=(px, py, pc), device_id_type=pl.DeviceIdType.MESH)

    mine = pltpu.make_async_copy(p_ref.at[me], out_ref.at[me], local_sem)

    def start():
        mine.start()
        for k, flip in enumerate(flips):
            copy(k, flip, False).start()

    def finish():
        for k, flip in enumerate(flips):
            copy(k, flip, True).wait_recv()
            copy(k, flip, False).wait_send()
        mine.wait()

    return start, None, finish


class Rider(NamedTuple):
    kind: str
    src: jax.Array

    def out_shape(self):
        shape = (N_DEV,) + self.src.shape if self.kind == "gather" else self.src.shape
        return _sds(shape, self.src.dtype)


def _rider_hooks(riders, in_refs, out_refs, sem_refs, step, n_steps):
    phases = [(_gather_phases if r.kind == "gather" else _scatter_phases)(
                  in_refs[i], out_refs[i], *sem_refs[3 * i:3 * i + 3]) for i, r in enumerate(riders)]

    def begin():
        for start, forward, _ in phases:
            pl.when(step == 0)(start)
            if forward is not None:
                pl.when(step == n_steps // 2)(forward)

    def end():
        for _, _, finish in phases:
            pl.when(step == n_steps - 1)(finish)

    return begin, end


def _split_refs(refs, n_in, n_out, n_scratch, n_riders):
    pos, parts = 0, []
    for n in (n_in, n_riders, n_out, n_riders, n_scratch, 3 * n_riders):
        parts.append(refs[pos:pos + n])
        pos += n
    return parts


def all_gather(shard, name):
    def body(x_ref, out_ref, send_sems, recv_sems, local_sem):
        start, forward, finish = _gather_phases(x_ref, out_ref, send_sems, recv_sems, local_sem)
        start()
        forward()
        finish()

    return pl.pallas_call(
        body, name=name, out_shape=_sds((N_DEV,) + shard.shape, shard.dtype),
        in_specs=[ANY], out_specs=ANY, scratch_shapes=_comm_sems(),
    )(shard)


def adamw(w, m, v, parts, name):
    n_parts, rows, cols = parts.shape
    tr = _row_tile(rows, 128)
    c1 = 1.0 - ADAM_B1 ** ADAM_STEP
    c2 = 1.0 - ADAM_B2 ** ADAM_STEP

    def body(w_ref, m_ref, v_ref, p_ref, g_ref, d_ref, nm_ref, nv_ref):
        g = p_ref[0].astype(F32)
        for s in range(1, n_parts):
            g = g + p_ref[s].astype(F32)
        nm = ADAM_B1 * m_ref[...] + (1.0 - ADAM_B1) * g
        nv = ADAM_B2 * v_ref[...] + (1.0 - ADAM_B2) * (g * g)
        delta = -ADAM_LR * ((nm / c1) / (jnp.sqrt(nv / c2) + ADAM_EPS) + ADAM_WD * w_ref[...])
        g_ref[...] = g
        d_ref[...] = delta
        nm_ref[...] = nm
        nv_ref[...] = nv

    mat = pl.BlockSpec((tr, cols), lambda i: (i, 0))
    return pl.pallas_call(
        body, name=name, grid=(rows // tr,),
        in_specs=[mat, mat, mat, pl.BlockSpec((n_parts, tr, cols), lambda i: (0, i, 0))],
        out_specs=[mat, mat, mat, mat],
        out_shape=[_sds((rows, cols), F32)] * 4,
        compiler_params=_params("parallel"),
    )(w, m, v, parts)


_NN = (((1,), (0,)), ((), ()))
_NT = (((1,), (1,)), ((), ()))
_TN = (((0,), (0,)), ((), ()))


def mm_cols(name, a, b_list, b_specs, nt, extras, extra_specs, out_shapes, out_specs, epilogue, n_blk, riders=()):
    t_len, k_len = a.shape
    tm = _row_tile(t_len, 512)
    nb, ne, n_out, nr = len(b_list), len(extras), len(out_shapes), len(riders)
    t_steps = t_len // tm

    def body(*refs):
        ins, r_in, outs, r_out, _, r_sem = _split_refs(refs, 1 + nb + ne, n_out, 0, nr)
        step = pl.program_id(0) * t_steps + pl.program_id(1)
        begin, end = _rider_hooks(riders, r_in, r_out, r_sem, step, n_blk * t_steps)
        begin()
        av = ins[0][...]
        accs = [lax.dot_general(av, br[...], _NT if nt else _NN, preferred_element_type=F32)
                for br in ins[1:1 + nb]]
        epilogue(accs, ins[1 + nb:], outs)
        end()

    res = pl.pallas_call(
        body, name=name, grid=(n_blk, t_steps),
        in_specs=([pl.BlockSpec((tm, k_len), lambda j, t: (t, 0))] + list(b_specs) + list(extra_specs(tm))
                  + [ANY] * nr),
        out_specs=list(out_specs(tm)) + [ANY] * nr,
        out_shape=list(out_shapes) + [r.out_shape() for r in riders],
        scratch_shapes=_comm_sems() * nr,
        compiler_params=_params("arbitrary", "arbitrary"),
    )(a, *b_list, *extras, *[r.src for r in riders])
    return res[:n_out], res[n_out:]


def mm_reduce(name, a_list, a_specs, b_list, b_specs, nt, res, scale, t_len, n_len, n_blk, riders=()):
    tm = _row_tile(t_len, 512)
    na, nr = len(a_list), len(riders)
    has_res = res is not None
    t_steps = t_len // tm

    def body(*refs):
        ins, r_in, outs, r_out, scr, r_sem = _split_refs(refs, 2 * na + has_res, 1, 1, nr)
        o_ref, acc = outs[0], scr[0]
        j = pl.program_id(1)
        step = pl.program_id(0) * n_blk + j
        begin, end = _rider_hooks(riders, r_in, r_out, r_sem, step, t_steps * n_blk)
        begin()

        @pl.when(j == 0)
        def _():
            acc[...] = jnp.zeros_like(acc)

        part = None
        for ar, br in zip(ins[:na], ins[na:2 * na]):
            d = lax.dot_general(ar[...], br[...], _NT if nt else _NN, preferred_element_type=F32)
            part = d if part is None else part + d
        acc[...] += part

        @pl.when(j == n_blk - 1)
        def _():
            val = acc[...] * scale if scale != 1.0 else acc[...]
            o_ref[...] = ins[2 * na][...] + val if has_res else val

        end()

    row = pl.BlockSpec((tm, n_len), lambda t, j: (t, 0))
    out = pl.pallas_call(
        body, name=name, grid=(t_steps, n_blk),
        in_specs=list(a_specs(tm)) + list(b_specs) + ([row] if has_res else []) + [ANY] * nr,
        out_specs=[row] + [ANY] * nr,
        out_shape=[_sds((t_len, n_len), F32)] + [r.out_shape() for r in riders],
        scratch_shapes=[pltpu.VMEM((tm, n_len), F32)] + _comm_sems() * nr,
        compiler_params=_params("arbitrary", "arbitrary"),
    )(*a_list, *b_list, *([res] if has_res else []), *[r.src for r in riders])
    return out[0], out[1:]


def mm_tn(name, x, x_spec, dy_list, dy_specs, out_shapes, out_specs, scale, t_len, n_blk, riders=(),
          x_transposed=False):
    tt = _row_tile(t_len, 512)
    nd, nr = len(dy_list), len(riders)
    t_steps = t_len // tt
    acc_shapes = [pltpu.VMEM(spec.block_shape[-2:], F32) for spec in out_specs]

    def body(*refs):
        ins, r_in, outs, r_out, accs, r_sem = _split_refs(refs, 1 + nd, nd, nd, nr)
        t = pl.program_id(1)
        step = pl.program_id(0) * t_steps + t
        begin, end = _rider_hooks(riders, r_in, r_out, r_sem, step, n_blk * t_steps)
        begin()
        xv = ins[0][...]
        for dr, acc in zip(ins[1:], accs):
            d = lax.dot_general(xv, dr[...], _NN if x_transposed else _TN, preferred_element_type=F32)

            @pl.when(t == 0)
            def _():
                acc[...] = d

            @pl.when(t > 0)
            def _():
                acc[...] += d

        @pl.when(t == t_steps - 1)
        def _():
            for acc, orf in zip(accs, outs):
                val = acc[...] * scale if scale != 1.0 else acc[...]
                orf[...] = val.astype(orf.dtype)

        end()

    res = pl.pallas_call(
        body, name=name, grid=(n_blk, t_steps),
        in_specs=[x_spec(tt)] + list(dy_specs(tt)) + [ANY] * nr,
        out_specs=list(out_specs) + [ANY] * nr,
        out_shape=list(out_shapes) + [r.out_shape() for r in riders],
        scratch_shapes=acc_shapes + _comm_sems() * nr,
        compiler_params=_params("arbitrary", "arbitrary"),
    )(x, *dy_list, *[r.src for r in riders])
    return res[:nd], res[nd:]


def rms_fwd(x, g, name):
    t_len, d = x.shape
    tm = _row_tile(t_len, 512)

    def body(x_ref, g_ref, h_ref, ht_ref):
        xv = x_ref[...]
        r = lax.rsqrt(jnp.mean(xv * xv, axis=-1, keepdims=True) + EPS)
        hv = xv * r * g_ref[...]
        h_ref[...] = hv.astype(BF16)
        ht_ref[...] = hv.T.astype(BF16)

    row = pl.BlockSpec((tm, d), lambda i: (i, 0))
    return pl.pallas_call(
        body, name=name, grid=(t_len // tm,),
        in_specs=[row, pl.BlockSpec((1, d), lambda i: (0, 0))],
        out_specs=[row, pl.BlockSpec((d, tm), lambda i: (0, i))],
        out_shape=[_sds((t_len, d), BF16), _sds((d, t_len), BF16)],
        compiler_params=_params("parallel"),
    )(x, g)


def rms_bwd(x, g, dh, dres, name):
    t_len, d = x.shape
    tm = _row_tile(t_len, 512)

    def body(x_ref, g_ref, dh_ref, dr_ref, dx_ref, dxb_ref, dg_ref):
        i = pl.program_id(0)
        xv = x_ref[...]
        r = lax.rsqrt(jnp.mean(xv * xv, axis=-1, keepdims=True) + EPS)
        xh = xv * r
        dhv = dh_ref[...]

        @pl.when(i == 0)
        def _():
            dg_ref[...] = jnp.zeros_like(dg_ref)

        dg_ref[...] += jnp.sum(dhv * xh, axis=0, keepdims=True)
        dxh = dhv * g_ref[...]
        dx = dr_ref[...] + r * (dxh - xh * jnp.mean(dxh * xh, axis=-1, keepdims=True))
        dx_ref[...] = dx
        dxb_ref[...] = dx.astype(BF16)

    row = pl.BlockSpec((tm, d), lambda i: (i, 0))
    vec = pl.BlockSpec((1, d), lambda i: (0, 0))
    return pl.pallas_call(
        body, name=name, grid=(t_len // tm,),
        in_specs=[row, vec, row, row],
        out_specs=[row, row, vec],
        out_shape=[_sds((t_len, d), F32), _sds((t_len, d), BF16), _sds((1, d), F32)],
        compiler_params=_params("arbitrary"),
    )(x, g, dh, dres)


def loss_head(y, target, name):
    t_len, d = y.shape
    tm = _row_tile(t_len, 512)

    def body(y_ref, t_ref, l_ref, dy_ref, dyb_ref):
        i = pl.program_id(0)
        err = y_ref[...] - t_ref[...]

        @pl.when(i == 0)
        def _():
            l_ref[...] = jnp.zeros_like(l_ref)

        rows = jnp.sum(err * err, axis=-1, keepdims=True) * (1.0 / d)
        l_ref[...] += 0.5 * jnp.sum(rows, axis=0, keepdims=True)
        dy = err * (1.0 / d)
        dy_ref[...] = dy
        dyb_ref[...] = dy.astype(BF16)

    row = pl.BlockSpec((tm, d), lambda i: (i, 0))
    return pl.pallas_call(
        body, name=name, grid=(t_len // tm,),
        in_specs=[row, row],
        out_specs=[pl.BlockSpec((8, LANES), lambda i: (0, 0)), row, row],
        out_shape=[_sds((8, LANES), F32), _sds((t_len, d), F32), _sds((t_len, d), BF16)],
        compiler_params=_params("arbitrary"),
    )(y, target)


def _conv_specs(tm, ch):
    per = tm // HALO
    cur = lambda cb: pl.BlockSpec((tm, ch), lambda i: (i, cb))
    prev = lambda cb: pl.BlockSpec((HALO, ch), lambda i: (jnp.maximum(i * per - 1, 0), cb))
    return [cur(0), cur(1), prev(0), prev(1)]


def _fill_glu(ext, a_ref, gt_ref, ap_ref, gp_ref, i, tm):
    vp = ap_ref[...] * _sigmoid(gp_ref[...])
    ext[0:HALO, :] = jnp.where(i > 0, vp, 0.0)
    ext[HALO:HALO + tm, :] = a_ref[...] * _sigmoid(gt_ref[...])


def _conv_rows(ext, w_ref, b_ref, r0):
    acc = jnp.broadcast_to(b_ref[...], (ROW_CHUNK, b_ref.shape[1]))
    for k in range(CONV_WIDTH):
        acc = acc + w_ref[k:k + 1, :] * ext[pl.ds(r0 + HALO - (CONV_WIDTH - 1) + k, ROW_CHUNK), :]
    return acc


def _layer_norm(yv):
    mu = jnp.mean(yv, axis=-1, keepdims=True)
    cen = yv - mu
    var = jnp.mean(cen * cen, axis=-1, keepdims=True)
    rstd = lax.rsqrt(var + EPS)
    return cen * rstd, rstd


def conv_fwd(z, w, b, lg, lb, name):
    t_len = z.shape[0]
    ch = w.shape[1]
    tm = _row_tile(t_len, 256)

    def body(a_ref, gt_ref, ap_ref, gp_ref, w_ref, b_ref, lg_ref, lb_ref, y_ref, pre_ref, ext):
        i = pl.program_id(0)
        _fill_glu(ext, a_ref, gt_ref, ap_ref, gp_ref, i, tm)
        for r0 in range(0, tm, ROW_CHUNK):
            pre = _conv_rows(ext, w_ref, b_ref, r0)
            pre_ref[r0:r0 + ROW_CHUNK, :] = pre
            xh, _ = _layer_norm(pre)
            u = xh * lg_ref[...] + lb_ref[...]
            y_ref[r0:r0 + ROW_CHUNK, :] = (u * _sigmoid(u)).astype(BF16)

    vec = pl.BlockSpec((1, ch), lambda i: (0, 0))
    row = pl.BlockSpec((tm, ch), lambda i: (i, 0))
    return pl.pallas_call(
        body, name=name, grid=(t_len // tm,),
        in_specs=_conv_specs(tm, ch) + [pl.BlockSpec((32, ch), lambda i: (0, 0)), vec, vec, vec],
        out_specs=[row, row],
        out_shape=[_sds((t_len, ch), BF16), _sds((t_len, ch), F32)],
        scratch_shapes=[pltpu.VMEM((HALO + tm, ch), F32)],
        compiler_params=_params("parallel"),
    )(z, z, z, z, w, b, lg, lb)


def conv_bwd_norm(pre, dy_cat, lg, lb, name):
    t_len, ch = pre.shape
    tm = _row_tile(t_len, 256)

    def body(pre_ref, dy_ref, lg_ref, lb_ref, dc_ref, dlg_ref, dlb_ref, db_ref):
        i = pl.program_id(0)

        @pl.when(i == 0)
        def _():
            dlg_ref[...] = jnp.zeros_like(dlg_ref)
            dlb_ref[...] = jnp.zeros_like(dlb_ref)
            db_ref[...] = jnp.zeros_like(db_ref)

        for r0 in range(0, tm, ROW_CHUNK):
            xh, rstd = _layer_norm(pre_ref[r0:r0 + ROW_CHUNK, :])
            u = xh * lg_ref[...] + lb_ref[...]
            sg = _sigmoid(u)
            du = dy_ref[r0:r0 + ROW_CHUNK, :] * (sg * (1.0 + u * (1.0 - sg)))
            dlg_ref[...] += jnp.sum(du * xh, axis=0, keepdims=True)
            dlb_ref[...] += jnp.sum(du, axis=0, keepdims=True)
            dxh = du * lg_ref[...]
            dc = rstd * (dxh - jnp.mean(dxh, axis=-1, keepdims=True)
                         - xh * jnp.mean(dxh * xh, axis=-1, keepdims=True))
            db_ref[...] += jnp.sum(dc, axis=0, keepdims=True)
            dc_ref[r0:r0 + ROW_CHUNK, :] = dc

    vec = pl.BlockSpec((1, ch), lambda i: (0, 0))
    row = pl.BlockSpec((tm, ch), lambda i: (i, 0))
    return pl.pallas_call(
        body, name=name, grid=(t_len // tm,),
        in_specs=[row, row, vec, vec],
        out_specs=[row, vec, vec, vec],
        out_shape=[_sds((t_len, ch), F32)] + [_sds((1, ch), F32)] * 3,
        compiler_params=_params("arbitrary"),
    )(pre, dy_cat, lg, lb)


def conv_bwd_taps(z, dc, w, name):
    t_len = z.shape[0]
    ch = w.shape[1]
    tm = _row_tile(t_len, 256)
    per = tm // HALO
    n_tiles = t_len // tm
    last_halo = t_len // HALO - 1

    def body(a_ref, gt_ref, ap_ref, gp_ref, dc_ref, dn_ref, w_ref, dz_a_ref, dz_g_ref, dw_ref, ext, dext):
        i = pl.program_id(0)
        _fill_glu(ext, a_ref, gt_ref, ap_ref, gp_ref, i, tm)
        dext[0:tm, :] = dc_ref[...]
        dext[tm:tm + HALO, :] = jnp.where(i < n_tiles - 1, dn_ref[...], 0.0)

        @pl.when(i == 0)
        def _():
            dw_ref[...] = jnp.zeros_like(dw_ref)

        for r0 in range(0, tm, ROW_CHUNK):
            dcv = dext[r0:r0 + ROW_CHUNK, :]
            dv = jnp.zeros((ROW_CHUNK, ch), F32)
            for k in range(CONV_WIDTH):
                dv = dv + w_ref[k:k + 1, :] * dext[pl.ds(r0 + (CONV_WIDTH - 1) - k, ROW_CHUNK), :]
                prod = dcv * ext[pl.ds(r0 + HALO - (CONV_WIDTH - 1) + k, ROW_CHUNK), :]
                fold = prod[0:8]
                for s in range(8, ROW_CHUNK, 8):
                    fold = fold + prod[s:s + 8]
                dw_ref[k] += fold
            av = a_ref[r0:r0 + ROW_CHUNK, :]
            sg = _sigmoid(gt_ref[r0:r0 + ROW_CHUNK, :])
            dz_a_ref[r0:r0 + ROW_CHUNK, :] = (dv * sg).astype(BF16)
            dz_g_ref[r0:r0 + ROW_CHUNK, :] = (dv * av * sg * (1.0 - sg)).astype(BF16)

    row = pl.BlockSpec((tm, ch), lambda i: (i, 0))
    nxt = pl.BlockSpec((HALO, ch), lambda i: (jnp.minimum((i + 1) * per, last_halo), 0))
    return pl.pallas_call(
        body, name=name, grid=(n_tiles,),
        in_specs=_conv_specs(tm, ch) + [row, nxt, pl.BlockSpec((32, ch), lambda i: (0, 0))],
        out_specs=[row, row, pl.BlockSpec((32, 8, ch), lambda i: (0, 0, 0))],
        out_shape=[_sds((t_len, ch), BF16), _sds((t_len, ch), BF16), _sds((32, 8, ch), F32)],
        scratch_shapes=[pltpu.VMEM((HALO + tm, ch), F32), pltpu.VMEM((tm + HALO, ch), F32)],
        compiler_params=_params("arbitrary"),
    )(z, z, z, z, dc, dc, w)


def _head_masks(rows):
    lane = lax.broadcasted_iota(jnp.int32, (rows, LANES), 1)
    low = lane < HEAD_DIM
    return low, jnp.logical_not(low)


def _per_head_mean(val, low):
    s_low = jnp.sum(jnp.where(low, val, 0.0), axis=-1, keepdims=True)
    s_high = jnp.sum(jnp.where(low, 0.0, val), axis=-1, keepdims=True)
    return jnp.where(low, s_low, s_high) * (1.0 / HEAD_DIM)


def qk_norm_fwd(z, g2, ch, name):
    t_len = z.shape[0]
    tm = _row_tile(t_len, 1024)
    n_col = 2 * ch // LANES
    z_off = 2 * ch // LANES

    def body(z_ref, g_ref, o_ref):
        low, _ = _head_masks(tm)
        xv = z_ref[...]
        r = lax.rsqrt(_per_head_mean(xv * xv, low) + EPS)
        o_ref[...] = xv * r * g_ref[...]

    return pl.pallas_call(
        body, name=name, grid=(t_len // tm, n_col),
        in_specs=[pl.BlockSpec((tm, LANES), lambda i, cb: (i, z_off + cb)),
                  pl.BlockSpec((1, LANES), lambda i, cb: (0, cb))],
        out_specs=pl.BlockSpec((tm, LANES), lambda i, cb: (i, cb)),
        out_shape=_sds((t_len, 2 * ch), F32),
        compiler_params=_params("parallel", "parallel"),
    )(z, g2)


def qk_norm_bwd(z, g, d_list, z_off, ch, name):
    t_len = z.shape[0]
    tm = _row_tile(t_len, 1024)
    n_col = ch // LANES
    nd = len(d_list)

    def body(*refs):
        z_ref, g_ref, d_refs = refs[0], refs[1], refs[2:2 + nd]
        dz_ref, dg_ref = refs[2 + nd], refs[3 + nd]
        first = jnp.logical_and(pl.program_id(0) == 0, pl.program_id(1) == 0)
        low, _ = _head_masks(tm)
        xv = z_ref[...]
        r = lax.rsqrt(_per_head_mean(xv * xv, low) + EPS)
        xh = xv * r
        dy = d_refs[0][...]
        for dr in d_refs[1:]:
            dy = dy + dr[...]

        @pl.when(first)
        def _():
            dg_ref[...] = jnp.zeros_like(dg_ref)

        dg_ref[...] += jnp.sum(dy * xh, axis=0, keepdims=True)
        dxh = dy * g_ref[...]
        dz_ref[...] = (r * (dxh - xh * _per_head_mean(dxh * xh, low))).astype(BF16)

    blk = pl.BlockSpec((tm, LANES), lambda i, cb: (i, cb))
    return pl.pallas_call(
        body, name=name, grid=(t_len // tm, n_col),
        in_specs=[pl.BlockSpec((tm, LANES), lambda i, cb: (i, z_off + cb)),
                  pl.BlockSpec((1, LANES), lambda i, cb: (0, 0))] + [blk] * nd,
        out_specs=[blk, pl.BlockSpec((1, LANES), lambda i, cb: (0, 0))],
        out_shape=[_sds((t_len, ch), BF16), _sds((1, LANES), F32)],
        compiler_params=_params("arbitrary", "arbitrary"),
    )(z, g, *d_list)


def _alibi_bias(n_heads, dilation):
    slopes = 2.0 ** (-ALIBI_MAX_BIAS * jnp.arange(1, n_heads + 1, dtype=F32) / n_heads)
    qi = jnp.arange(ATT_BLOCK)[:, None]
    kj = jnp.arange(ATT_BLOCK)[None, :]
    dist_cur = (qi - kj).astype(F32)
    dist_prev = (ATT_BLOCK + qi - kj).astype(F32)
    cur = jnp.where((qi >= kj)[None], -slopes[:, None, None] * (dilation * dist_cur)[None], MASKED)
    prev = jnp.where((kj >= qi)[None], -slopes[:, None, None] * (dilation * dist_prev)[None], MASKED)
    return jnp.concatenate([prev, cur], axis=-1).astype(F32)


def _stack_heads(val, low, high):
    return jnp.concatenate([jnp.where(low, val, 0.0), jnp.where(high, val, 0.0)], axis=0).astype(BF16)


def _head_rows(val, low, high):
    return jnp.concatenate([jnp.max(jnp.where(low, val, MASKED), axis=-1, keepdims=True),
                            jnp.max(jnp.where(high, val, MASKED), axis=-1, keepdims=True)], axis=0)


def _unit_scores(q2, k2, b_ref, has_prev, scale):
    s = lax.dot_general(q2, k2, _NT, preferred_element_type=F32) * scale
    s = s + b_ref[...].reshape(2 * ATT_BLOCK, 2 * ATT_BLOCK)
    col = lax.broadcasted_iota(jnp.int32, s.shape, 1)
    return jnp.where(jnp.logical_or(has_prev, col >= ATT_BLOCK), s, MASKED)


def _strided_rows(r, dilation):
    per = ATT_CHUNK // dilation
    return pl.ds(r, per, stride=dilation) if dilation > 1 else pl.ds(0, per)


def _deinterleave(dst, src_ref, dilation, base=None, dtype=None):
    per = ATT_CHUNK // dilation
    for r in range(dilation):
        val = src_ref[_strided_rows(r, dilation), :]
        val = val if dtype is None else val.astype(dtype)
        if base is None:
            dst[r * per:(r + 1) * per, :] = val
        else:
            dst[pl.ds(pl.multiple_of(base + r * per, ATT_BLOCK), per), :] = val


def _unit_rows(u, c, nb, base, pbase):
    in_chunk = lax.rem(u, jnp.int32(nb)) > 0
    has_prev = jnp.logical_or(in_chunk, c > 0)
    urow = pl.multiple_of(u * ATT_BLOCK, ATT_BLOCK)
    crow = pl.multiple_of(base + u * ATT_BLOCK, ATT_BLOCK)
    prow = pl.multiple_of(jnp.where(in_chunk, base + (u - 1) * ATT_BLOCK,
                                    pbase + (u + nb - 1) * ATT_BLOCK), ATT_BLOCK)
    return in_chunk, has_prev, urow, crow, prow


def _interleave(dst_ref, src, dilation, base=None):
    per = ATT_CHUNK // dilation
    for r in range(dilation):
        if base is None:
            val = src[r * per:(r + 1) * per, :]
        else:
            val = src[pl.ds(pl.multiple_of(base + r * per, ATT_BLOCK), per), :]
        dst_ref[_strided_rows(r, dilation), :] = val


def attn_fwd(qk, z, dilation, ch, name):
    t_len = qk.shape[0]
    pairs = ch // LANES
    nc = t_len // ATT_CHUNK
    nb = ATT_UNITS // dilation
    scale = 1.0 / math.sqrt(HEAD_DIM)
    bias = _alibi_bias(2 * pairs, dilation)

    def body(q_ref, k_ref, v_ref, b_ref, o_ref, l_ref, qd, kx, vx, od, ld):
        c = pl.program_id(1)
        slot = lax.rem(c, jnp.int32(2))
        base, pbase = slot * ATT_CHUNK, (1 - slot) * ATT_CHUNK

        @pl.when(c == 0)
        def _():
            kx[...] = jnp.zeros_like(kx)
            vx[...] = jnp.zeros_like(vx)

        _deinterleave(qd, q_ref, dilation)
        _deinterleave(kx, k_ref, dilation, base, BF16)
        _deinterleave(vx, v_ref, dilation, base, BF16)
        low, high = _head_masks(ATT_BLOCK)

        def unit(u, carry):
            _, has_prev, urow, crow, prow = _unit_rows(u, c, nb, base, pbase)
            q2 = _stack_heads(qd[pl.ds(urow, ATT_BLOCK), :], low, high)
            k2 = jnp.concatenate([kx[pl.ds(prow, ATT_BLOCK), :], kx[pl.ds(crow, ATT_BLOCK), :]], axis=0)
            v2 = jnp.concatenate([vx[pl.ds(prow, ATT_BLOCK), :], vx[pl.ds(crow, ATT_BLOCK), :]], axis=0)
            s = _unit_scores(q2, k2, b_ref, has_prev, scale)
            mx = jnp.max(s, axis=-1, keepdims=True)
            e = jnp.exp(s - mx)
            den = jnp.sum(e, axis=-1, keepdims=True)
            acc = lax.dot_general(e.astype(BF16), v2, _NN, preferred_element_type=F32) / den
            lse = jnp.broadcast_to(mx + jnp.log(den), acc.shape)
            od[pl.ds(urow, ATT_BLOCK), :] = jnp.where(low, acc[:ATT_BLOCK], acc[ATT_BLOCK:])
            ld[pl.ds(urow, ATT_BLOCK), :] = jnp.where(low, lse[:ATT_BLOCK], lse[ATT_BLOCK:])
            return carry

        lax.fori_loop(0, ATT_UNITS, unit, 0, unroll=2)
        _interleave(o_ref, od, dilation)
        _interleave(l_ref, ld, dilation)

    blk = (ATT_CHUNK, LANES)
    bias_spec = pl.BlockSpec((2, ATT_BLOCK, 2 * ATT_BLOCK), lambda p, c: (p, 0, 0))
    out_spec = pl.BlockSpec(blk, lambda p, c: (c, p))
    return pl.pallas_call(
        body, name=name, grid=(pairs, nc),
        in_specs=[pl.BlockSpec(blk, lambda p, c: (c, p)),
                  pl.BlockSpec(blk, lambda p, c: (c, pairs + p)),
                  pl.BlockSpec(blk, lambda p, c: (c, 4 * pairs + p)),
                  bias_spec],
        out_specs=[out_spec, out_spec],
        out_shape=[_sds((t_len, ch), F32)] * 2,
        scratch_shapes=[pltpu.VMEM((ATT_CHUNK, LANES), F32),
                        pltpu.VMEM((2 * ATT_CHUNK, LANES), BF16), pltpu.VMEM((2 * ATT_CHUNK, LANES), BF16),
                        pltpu.VMEM((ATT_CHUNK, LANES), F32), pltpu.VMEM((ATT_CHUNK, LANES), F32)],
        compiler_params=_params("arbitrary", "arbitrary"),
    )(qk, qk, z, bias)


def attn_combine(outs, lses, name):
    t_len, ch = outs[0].shape
    tm = _row_tile(t_len, 512)

    def body(o1, o2, o3, l1, l2, l3, out_ref, outb_ref, lg_ref):
        a, b, c = l1[...], l2[...], l3[...]
        mx = jnp.maximum(jnp.maximum(a, b), c)
        tot = mx + jnp.log(jnp.exp(a - mx) + jnp.exp(b - mx) + jnp.exp(c - mx))
        val = jnp.exp(a - tot) * o1[...] + jnp.exp(b - tot) * o2[...] + jnp.exp(c - tot) * o3[...]
        out_ref[...] = val
        outb_ref[...] = val.astype(BF16)
        lg_ref[...] = tot

    row = pl.BlockSpec((tm, ch), lambda i: (i, 0))
    return pl.pallas_call(
        body, name=name, grid=(t_len // tm,),
        in_specs=[row] * 6, out_specs=[row] * 3,
        out_shape=[_sds((t_len, ch), F32), _sds((t_len, ch), BF16), _sds((t_len, ch), F32)],
        compiler_params=_params("parallel"),
    )(*outs, *lses)


def attn_bwd(qk, z, dy_cat, out, lg, dilation, ch, name):
    t_len = qk.shape[0]
    pairs = ch // LANES
    nc = t_len // ATT_CHUNK
    nb = ATT_UNITS // dilation
    scale = 1.0 / math.sqrt(HEAD_DIM)
    bias = _alibi_bias(2 * pairs, dilation)

    def body(q_ref, k_ref, v_ref, do_ref, out_ref, lg_ref, b_ref, dq_ref, dk_ref, dv_ref,
             qd, dod, lgd, dld, dl_nat, kx, vx, dkx, dvx, dqd):
        c = pl.program_id(1)
        slot = lax.rem(c, jnp.int32(2))
        base, pbase = slot * ATT_CHUNK, (1 - slot) * ATT_CHUNK

        @pl.when(c == 0)
        def _():
            for ref in (kx, vx, dkx, dvx):
                ref[...] = jnp.zeros_like(ref)

        @pl.when(c < nc)
        def _():
            low_all, _ = _head_masks(ATT_CHUNK)
            dl_nat[...] = _per_head_mean(do_ref[...] * out_ref[...], low_all) * float(HEAD_DIM)
            _deinterleave(qd, q_ref, dilation)
            _deinterleave(dod, do_ref, dilation)
            _deinterleave(lgd, lg_ref, dilation)
            _deinterleave(dld, dl_nat, dilation)
            _deinterleave(kx, k_ref, dilation, base, BF16)
            _deinterleave(vx, v_ref, dilation, base, BF16)
            cur = pl.ds(pl.multiple_of(base, ATT_CHUNK), ATT_CHUNK)
            dkx[cur, :] = jnp.zeros((ATT_CHUNK, LANES), F32)
            dvx[cur, :] = jnp.zeros((ATT_CHUNK, LANES), F32)
            low, high = _head_masks(ATT_BLOCK)

            def unit(u, carry):
                _, has_prev, urow, crow, prow = _unit_rows(u, c, nb, base, pbase)
                rows = pl.ds(urow, ATT_BLOCK)
                q2 = _stack_heads(qd[rows, :], low, high)
                do2 = _stack_heads(dod[rows, :], low, high)
                lse = _head_rows(lgd[rows, :], low, high)
                delta = _head_rows(dld[rows, :], low, high)
                k2 = jnp.concatenate([kx[pl.ds(prow, ATT_BLOCK), :], kx[pl.ds(crow, ATT_BLOCK), :]], axis=0)
                v2 = jnp.concatenate([vx[pl.ds(prow, ATT_BLOCK), :], vx[pl.ds(crow, ATT_BLOCK), :]], axis=0)
                prob = jnp.exp(_unit_scores(q2, k2, b_ref, has_prev, scale) - lse)
                dp = lax.dot_general(do2, v2, _NT, preferred_element_type=F32)
                ds = (prob * (dp - delta)).astype(BF16)
                dq2 = lax.dot_general(ds, k2, _NN, preferred_element_type=F32)
                dk2 = lax.dot_general(ds, q2, _TN, preferred_element_type=F32)
                dv2 = lax.dot_general(prob.astype(BF16), do2, _TN, preferred_element_type=F32)
                dqd[rows, :] = scale * jnp.where(low, dq2[:ATT_BLOCK], dq2[ATT_BLOCK:])
                dkx[pl.ds(prow, ATT_BLOCK), :] += scale * dk2[:ATT_BLOCK]
                dkx[pl.ds(crow, ATT_BLOCK), :] += scale * dk2[ATT_BLOCK:]
                dvx[pl.ds(prow, ATT_BLOCK), :] += dv2[:ATT_BLOCK]
                dvx[pl.ds(crow, ATT_BLOCK), :] += dv2[ATT_BLOCK:]
                return carry

            lax.fori_loop(0, ATT_UNITS, unit, 0, unroll=2)
            _interleave(dq_ref, dqd, dilation)

        @pl.when(c > 0)
        def _():
            _interleave(dk_ref, dkx, dilation, pbase)
            _interleave(dv_ref, dvx, dilation, pbase)

    blk = (ATT_CHUNK, LANES)
    here = lambda c: jnp.minimum(c, nc - 1)
    spec = lambda off: pl.BlockSpec(blk, lambda p, c: (here(c), off + p))
    late = pl.BlockSpec(blk, lambda p, c: (jnp.maximum(c - 1, 0), p))
    bias_spec = pl.BlockSpec((2, ATT_BLOCK, 2 * ATT_BLOCK), lambda p, c: (p, 0, 0))
    f32_chunk = pltpu.VMEM((ATT_CHUNK, LANES), F32)
    return pl.pallas_call(
        body, name=name, grid=(pairs, nc + 1),
        in_specs=[spec(0), spec(pairs), spec(4 * pairs), spec(pairs), spec(0), spec(0), bias_spec],
        out_specs=[spec(0), late, late],
        out_shape=[_sds((t_len, ch), F32)] * 3,
        scratch_shapes=[f32_chunk] * 5
                       + [pltpu.VMEM((2 * ATT_CHUNK, LANES), BF16)] * 2
                       + [pltpu.VMEM((2 * ATT_CHUNK, LANES), F32)] * 2 + [f32_chunk],
        compiler_params=_params("arbitrary", "arbitrary"),
    )(qk, qk, z, dy_cat, out, lg, bias)


def sum3_bf16(a, b, c, name):
    t_len, ch = a.shape
    tm = _row_tile(t_len, 512)

    def body(a_ref, b_ref, c_ref, o_ref):
        o_ref[...] = (a_ref[...] + b_ref[...] + c_ref[...]).astype(BF16)

    row = pl.BlockSpec((tm, ch), lambda i: (i, 0))
    return pl.pallas_call(
        body, name=name, grid=(t_len // tm,), in_specs=[row] * 3, out_specs=row,
        out_shape=_sds((t_len, ch), BF16), compiler_params=_params("parallel"),
    )(a, b, c)


def _blk3(rows, cols):
    return pl.BlockSpec((None, rows, cols), lambda j, t: (j, 0, 0))


def ffn_up(h, wgu, name, riders):
    t_len, d = h.shape
    n_blk, _, _, fj = wgu.shape

    def epilogue(accs, e_refs, o_refs):
        gate, up = accs
        o_refs[0][...] = gate.astype(BF16)
        o_refs[1][...] = up.astype(BF16)
        o_refs[2][...] = (gate * _sigmoid(gate) * up).astype(BF16)

    w_spec = lambda i: pl.BlockSpec((None, None, d, fj), lambda j, t: (j, i, 0, 0))
    act = lambda tm: pl.BlockSpec((None, tm, fj), lambda j, t: (j, t, 0))
    return mm_cols(name, h, [wgu, wgu], [w_spec(0), w_spec(1)], False, [], lambda tm: [],
                   [_sds((n_blk, t_len, fj), BF16)] * 3, lambda tm: [act(tm)] * 3, epilogue, n_blk, riders)


def ffn_down(act, wd, res, name, riders):
    n_blk, t_len, fj = act.shape
    d = wd.shape[2]
    return mm_reduce(name, [act], lambda tm: [pl.BlockSpec((None, tm, fj), lambda t, j: (j, t, 0))],
                     [wd], [pl.BlockSpec((None, fj, d), lambda t, j: (j, 0, 0))], False, res, 0.5,
                     t_len, d, n_blk, riders)


def ffn_bwd(ht, gate, up, act, wgu, wd, dyb, name):
    d, t_len = ht.shape
    n_blk, _, fj = act.shape

    def epilogue(accs, e_refs, o_refs):
        d_act = 0.5 * accs[0]
        gv, uv = e_refs[0][...].astype(F32), e_refs[1][...].astype(F32)
        sg = _sigmoid(gv)
        o_refs[0][...] = (d_act * uv * (sg * (1.0 + gv * (1.0 - sg)))).astype(BF16)
        o_refs[1][...] = (d_act * gv * sg).astype(BF16)

    act_jt = lambda tm: pl.BlockSpec((None, tm, fj), lambda j, t: (j, t, 0))
    (d_gate, d_up), _ = mm_cols(name + "_dact", dyb, [wd], [_blk3(fj, d)], True, [gate, up],
                                lambda tm: [act_jt(tm)] * 2, [_sds((n_blk, t_len, fj), BF16)] * 2,
                                lambda tm: [act_jt(tm)] * 2, epilogue, n_blk)

    (d_wg, d_wu), _ = mm_tn(
        name + "_dwgu", ht, lambda tt: pl.BlockSpec((d, tt), lambda j, t: (0, t)),
        [d_gate, d_up], lambda tt: [act_jt(tt)] * 2,
        [_sds((n_blk, d, fj), BF16)] * 2, [_blk3(d, fj)] * 2, 1.0, t_len, n_blk, x_transposed=True)

    (d_wd,), (recv_wg,) = mm_tn(name + "_dwd", act, act_jt, [dyb],
                                lambda tt: [pl.BlockSpec((tt, d), lambda j, t: (t, 0))],
                                [_sds((n_blk, fj, d), BF16)], [_blk3(fj, d)], 0.5, t_len, n_blk,
                                [Rider("scatter", d_wg)])

    act_tj = lambda tm: pl.BlockSpec((None, tm, fj), lambda t, j: (j, t, 0))
    w_in = lambda i: pl.BlockSpec((None, None, d, fj), lambda t, j: (j, i, 0, 0))
    dh, (recv_wu, recv_wd) = mm_reduce(
        name + "_dh", [d_gate, d_up], lambda tm: [act_tj(tm)] * 2,
        [wgu, wgu], [w_in(0), w_in(1)], True, None, 1.0, t_len, d, n_blk,
        [Rider("scatter", d_wu), Rider("scatter", d_wd)])
    return dh, recv_wg, recv_wu, recv_wd


def local_step(x, target, g1, wgu1, wd1_s, gmix, win_s, conv_w, conv_b, ln_g, ln_b, gq, gk, wout_s, g3,
               wgu2_s, wd2_s):
    t_len, d = x.shape
    ch = d // 2
    ij = win_s.shape[1]
    oj = wout_s.shape[0]
    n_blk = N_DEV

    h1, h1t = rms_fwd(x, g1, "rms1")
    (gate1, up1, act1), (wd1, wgu2) = ffn_up(h1, wgu1, "ffn1_up",
                                             [Rider("gather", wd1_s), Rider("gather", wgu2_s)])
    x1, (win, wout) = ffn_down(act1, wd1, x, "ffn1_down", [Rider("gather", win_s), Rider("gather", wout_s)])

    h2, h2t = rms_fwd(x1, gmix, "rms_mix")

    def store_f32(accs, e_refs, o_refs):
        o_refs[0][...] = accs[0]

    (z,), (wd2,) = mm_cols(
        "w_in", h2, [win], [_blk3(d, ij)], False, [], lambda tm: [],
        [_sds((t_len, n_blk * ij), F32)],
        lambda tm: [pl.BlockSpec((tm, ij), lambda j, t: (t, j))], store_f32, n_blk,
        [Rider("gather", wd2_s)])

    conv_w32 = jnp.pad(conv_w, ((0, 32 - CONV_WIDTH), (0, 0)))
    y_conv, conv_pre = conv_fwd(z, conv_w32, conv_b, ln_g, ln_b, "conv_fwd")

    g2 = jnp.concatenate([jnp.tile(gq, (1, ch // HEAD_DIM)), jnp.tile(gk, (1, ch // HEAD_DIM))], axis=1)
    qk = qk_norm_fwd(z, g2, ch, "qk_norm")
    branch = [attn_fwd(qk, z, dil, ch, "attn_fwd_d%d" % dil) for dil in DILATIONS]
    att, att_b, lg = attn_combine([o for o, _ in branch], [l for _, l in branch], "attn_combine")

    y_cat = jnp.concatenate([y_conv, att_b], axis=1)
    x2, _ = mm_reduce(
        "w_out", [y_cat], lambda tm: [pl.BlockSpec((tm, oj), lambda t, j: (t, j))],
        [wout], [pl.BlockSpec((None, oj, d), lambda t, j: (j, 0, 0))], False, x1, 1.0,
        t_len, d, n_blk)

    h3, h3t = rms_fwd(x2, g3, "rms3")
    (gate2, up2, act2), _ = ffn_up(h3, wgu2, "ffn2_up", [])
    y, _ = ffn_down(act2, wd2, x2, "ffn2_down", [])

    loss_tile, dy, dyb = loss_head(y, target, "loss")

    dh3, recv_wg2, recv_wu2, recv_wd2 = ffn_bwd(h3t, gate2, up2, act2, wgu2, wd2, dyb, "ffn2")
    dx2, dx2b, d_g3 = rms_bwd(x2, g3, dh3, dy, "rms3_bwd")

    (dy_cat,), _ = mm_cols("w_out_dy", dx2b, [wout], [_blk3(oj, d)], True, [], lambda tm: [],
                           [_sds((t_len, n_blk * oj), F32)],
                           lambda tm: [pl.BlockSpec((tm, oj), lambda j, t: (t, j))], store_f32, n_blk)
    (d_wout,), _ = mm_tn("w_out_dw", y_cat, lambda tt: pl.BlockSpec((tt, oj), lambda j, t: (t, j)),
                         [dx2b], lambda tt: [pl.BlockSpec((tt, d), lambda j, t: (t, 0))],
                         [_sds((n_blk, oj, d), BF16)], [_blk3(oj, d)], 1.0, t_len, n_blk)

    dc, d_lg, d_lb, d_cb = conv_bwd_norm(conv_pre, dy_cat, ln_g, ln_b, "conv_bwd_norm")
    dz_a, dz_g, d_cw8 = conv_bwd_taps(z, dc, conv_w32, "conv_bwd_taps")
    d_cw = jnp.sum(d_cw8, axis=1)[:CONV_WIDTH]

    grads = [attn_bwd(qk, z, dy_cat, att, lg, dil, ch, "attn_bwd_d%d" % dil) for dil in DILATIONS]
    gq_t = jnp.tile(gq, (1, LANES // HEAD_DIM))
    gk_t = jnp.tile(gk, (1, LANES // HEAD_DIM))
    dz_q, d_gq2 = qk_norm_bwd(z, gq_t, [g[0] for g in grads], 2 * ch // LANES, ch, "q_norm_bwd")
    dz_k, d_gk2 = qk_norm_bwd(z, gk_t, [g[1] for g in grads], 3 * ch // LANES, ch, "k_norm_bwd")
    d_gq = d_gq2[:, :HEAD_DIM] + d_gq2[:, HEAD_DIM:]
    d_gk = d_gk2[:, :HEAD_DIM] + d_gk2[:, HEAD_DIM:]
    dz_v = sum3_bf16(grads[0][2], grads[1][2], grads[2][2], "dv_sum")
    dzb = jnp.concatenate([dz_a, dz_g, dz_q, dz_k, dz_v], axis=1)

    (d_win,), (recv_wout,) = mm_tn(
        "w_in_dw", h2t, lambda tt: pl.BlockSpec((d, tt), lambda j, t: (0, t)),
        [dzb], lambda tt: [pl.BlockSpec((tt, ij), lambda j, t: (t, j))],
        [_sds((n_blk, d, ij), BF16)], [_blk3(d, ij)], 1.0, t_len, n_blk, [Rider("scatter", d_wout)],
        x_transposed=True)
    dh2, (recv_win,) = mm_reduce(
        "w_in_dh", [dzb], lambda tm: [pl.BlockSpec((tm, ij), lambda t, j: (t, j))],
        [win], [pl.BlockSpec((None, d, ij), lambda t, j: (j, 0, 0))], True, None, 1.0,
        t_len, d, n_blk, [Rider("scatter", d_win)])
    dx1, dx1b, d_gmix = rms_bwd(x1, gmix, dh2, dx2, "rms_mix_bwd")

    dh1, recv_wg1, recv_wu1, recv_wd1 = ffn_bwd(h1t, gate1, up1, act1, wgu1, wd1, dx1b, "ffn1")
    grad_x, _, d_g1 = rms_bwd(x, g1, dh1, dx1, "rms1_bwd")

    big = dict(ffn1_w_gate=recv_wg1, ffn1_w_up=recv_wu1, ffn1_w_down=recv_wd1, w_in=recv_win, w_out=recv_wout,
               ffn2_w_gate=recv_wg2, ffn2_w_up=recv_wu2, ffn2_w_down=recv_wd2)
    small = dict(g1=d_g1, gmix=d_gmix, g3=d_g3, conv_b=d_cb, ln_g=d_lg, ln_b=d_lb, gq=d_gq, gk=d_gk, conv_w=d_cw)
    return loss_tile[0, 0], grad_x, big, small


SMALL_ROWS = 48


def _pack_small(ch, g1, gmix, g3, conv_b, ln_g, ln_b, gq, gk, conv_w):
    pad_head = lambda v: jnp.pad(v, ((0, 0), (0, ch - v.shape[1])))
    rows = [g1.reshape(2, ch), gmix.reshape(2, ch), g3.reshape(2, ch), conv_b, ln_g, ln_b,
            pad_head(gq), pad_head(gk), conv_w]
    packed = jnp.concatenate(rows, axis=0)
    return jnp.pad(packed, ((0, SMALL_ROWS - packed.shape[0]), (0, 0)))


def _unpack_small(packed, d):
    return dict(g1=packed[0:2].reshape(1, d), gmix=packed[2:4].reshape(1, d), g3=packed[4:6].reshape(1, d),
                conv_b=packed[6:7], ln_g=packed[7:8], ln_b=packed[8:9],
                gq=packed[9:10, :HEAD_DIM], gk=packed[10:11, :HEAD_DIM])


def kernel(x, ffn1_norm_g, ffn1_w_gate, ffn1_w_up, ffn1_w_down, mix_norm_g, w_in, conv_w_dw, conv_b_dw, conv_ln_g, conv_ln_b, q_norm_g, k_norm_g, w_out, ffn2_norm_g, ffn2_w_gate, ffn2_w_up, ffn2_w_down, loss_target, m_ffn1_norm_g, m_ffn1_w_gate, m_ffn1_w_up, m_ffn1_w_down, m_mix_norm_g, m_w_in, m_conv_w_dw, m_conv_b_dw, m_conv_ln_g, m_conv_ln_b, m_q_norm_g, m_k_norm_g, m_w_out, m_ffn2_norm_g, m_ffn2_w_gate, m_ffn2_w_up, m_ffn2_w_down, v_ffn1_norm_g, v_ffn1_w_gate, v_ffn1_w_up, v_ffn1_w_down, v_mix_norm_g, v_w_in, v_conv_w_dw, v_conv_b_dw, v_conv_ln_g, v_conv_ln_b, v_q_norm_g, v_k_norm_g, v_w_out, v_ffn2_norm_g, v_ffn2_w_gate, v_ffn2_w_up, v_ffn2_w_down):
    d = x.shape[-1]
    ch = d // 2
    me = 4 * lax.axis_index("x") + 2 * lax.axis_index("y") + lax.axis_index("c")

    gu = lambda wg, wu: jnp.stack([wg[0], wu[0]]).astype(BF16)
    wgu1 = all_gather(gu(ffn1_w_gate, ffn1_w_up), "ag_wgu1")
    cw_all = all_gather(conv_w_dw[0], "ag_convw")
    conv_w = jnp.transpose(cw_all, (1, 0, 2)).reshape(CONV_WIDTH, ch)

    loss_part, grad_x, big, small = local_step(
        x[0], loss_target[0], ffn1_norm_g, wgu1, ffn1_w_down[0].astype(BF16), mix_norm_g, w_in[0].astype(BF16),
        conv_w, conv_b_dw, conv_ln_g, conv_ln_b, q_norm_g, k_norm_g, w_out[0].astype(BF16), ffn2_norm_g,
        gu(ffn2_w_gate, ffn2_w_up), ffn2_w_down[0].astype(BF16))
    loss = lax.psum(loss_part, MESH_AXES)

    state = dict(
        ffn1_w_gate=(ffn1_w_gate, m_ffn1_w_gate, v_ffn1_w_gate), ffn1_w_up=(ffn1_w_up, m_ffn1_w_up, v_ffn1_w_up),
        ffn1_w_down=(ffn1_w_down, m_ffn1_w_down, v_ffn1_w_down), w_in=(w_in, m_w_in, v_w_in),
        w_out=(w_out, m_w_out, v_w_out),
        ffn2_w_gate=(ffn2_w_gate, m_ffn2_w_gate, v_ffn2_w_gate), ffn2_w_up=(ffn2_w_up, m_ffn2_w_up, v_ffn2_w_up),
        ffn2_w_down=(ffn2_w_down, m_ffn2_w_down, v_ffn2_w_down))
    out = {}
    for pname, (w, m, v) in state.items():
        res = adamw(w[0], m[0], v[0], big[pname], "adamw_" + pname)
        out[pname] = [r[None] for r in res]

    zero_taps = jnp.zeros((CONV_WIDTH, ch), F32)
    pack = lambda g1, gm, g3, cb, lg, lb, gq, gk: _pack_small(ch, g1, gm, g3, cb, lg, lb, gq, gk, zero_taps)
    small_parts = all_gather(_pack_small(ch, **small), "ag_small_grads")
    s_res = adamw(
        pack(ffn1_norm_g, mix_norm_g, ffn2_norm_g, conv_b_dw, conv_ln_g, conv_ln_b, q_norm_g, k_norm_g),
        pack(m_ffn1_norm_g, m_mix_norm_g, m_ffn2_norm_g, m_conv_b_dw, m_conv_ln_g, m_conv_ln_b, m_q_norm_g, m_k_norm_g),
        pack(v_ffn1_norm_g, v_mix_norm_g, v_ffn2_norm_g, v_conv_b_dw, v_conv_ln_g, v_conv_ln_b, v_q_norm_g, v_k_norm_g),
        small_parts, "adamw_small")
    s_out = [_unpack_small(r, d) for r in s_res]
    names = dict(g1="ffn1_norm_g", gmix="mix_norm_g", g3="ffn2_norm_g", conv_b="conv_b_dw", ln_g="conv_ln_g",
                 ln_b="conv_ln_b", gq="q_norm_g", gk="k_norm_g")
    for key, full in names.items():
        out[full] = [r[key] for r in s_out]

    cshard = ch // N_DEV
    taps_sum = s_res[0][11:11 + CONV_WIDTH]
    taps_mine = lax.dynamic_slice(taps_sum, (0, me * cshard), (CONV_WIDTH, cshard))
    pad_taps = lambda a: jnp.pad(a, ((0, 32 - CONV_WIDTH), (0, 0)))
    c_res = adamw(pad_taps(conv_w_dw[0]), pad_taps(m_conv_w_dw[0]), pad_taps(v_conv_w_dw[0]),
                  pad_taps(taps_mine)[None], "adamw_convw")
    out["conv_w_dw"] = [r[:CONV_WIDTH][None] for r in c_res]

    order = ["ffn1_norm_g", "ffn1_w_gate", "ffn1_w_up", "ffn1_w_down", "mix_norm_g", "w_in", "conv_w_dw",
             "conv_b_dw", "conv_ln_g", "conv_ln_b", "q_norm_g", "k_norm_g", "w_out", "ffn2_norm_g",
             "ffn2_w_gate", "ffn2_w_up", "ffn2_w_down"]
    result = [loss, grad_x[None]]
    for kind in range(4):
        result += [out[n][kind] for n in order]
    return tuple(result)
```

```python
import math
from typing import NamedTuple

import jax
import jax.numpy as jnp
from jax import lax
from jax.experimental import pallas as pl
from jax.experimental.pallas import tpu as pltpu

F32 = jnp.float32
BF16 = jnp.bfloat16

N_DEV = 8
EPS = 1e-6
HEAD_DIM = 64
LANES = 128
CONV_WIDTH = 31
HALO = 32
ROW_CHUNK = 32
ATT_BLOCK = 128
DILATIONS = (1, 4, 16)
ATT_UNITS = 16
ATT_CHUNK = ATT_UNITS * ATT_BLOCK
ALIBI_MAX_BIAS = 8.0
MASKED = -1e30
VMEM_LIMIT = 56 * 1024 * 1024

ADAM_LR = 0.001
ADAM_B1 = 0.9
ADAM_B2 = 0.999
ADAM_EPS = 1e-08
ADAM_WD = 0.01
ADAM_STEP = 10

MESH_AXES = ("x", "y", "c")
ANY = pl.BlockSpec(memory_space=pl.ANY)


def _sds(shape, dtype):
    return jax.ShapeDtypeStruct(tuple(shape), dtype)


def _params(*sem):
    return pltpu.CompilerParams(dimension_semantics=sem, vmem_limit_bytes=VMEM_LIMIT)


def _sigmoid(v):
    return 1.0 / (1.0 + jnp.exp(-v))


def _row_tile(t, want):
    for cand in range(min(want, t) // 8 * 8, 0, -8):
        if t % cand == 0:
            return cand
    return t


def _mesh_pos():
    return lax.axis_index("x"), lax.axis_index("y"), lax.axis_index("c")


def _comm_sems():
    return [pltpu.SemaphoreType.DMA((7,)), pltpu.SemaphoreType.DMA((7,)), pltpu.SemaphoreType.DMA(())]


def _gather_phases(x_ref, out_ref, send_sems, recv_sems, local_sem):
    x, y, c = _mesh_pos()
    me, sibling = (x, y, c), (x, y, 1 - c)
    chips = [(1 - x, y), (x, 1 - y), (1 - x, 1 - y)]

    def slot(px, py, pc):
        return out_ref.at[4 * px + 2 * py + pc]

    def copy(k, block, to, src=None):
        return pltpu.make_async_remote_copy(
            src_ref=slot(*block) if src is None else src, dst_ref=slot(*block),
            send_sem=send_sems.at[k], recv_sem=recv_sems.at[k],
            device_id=to, device_id_type=pl.DeviceIdType.MESH)

    mine = pltpu.make_async_copy(x_ref, slot(*me), local_sem)
    first = [copy(0, me, sibling, src=x_ref)]
    first += [copy(1 + j, me, (*chip, c), src=x_ref) for j, chip in enumerate(chips)]
    passed = [copy(4 + j, (*chip, c), sibling) for j, chip in enumerate(chips)]

    def start():
        mine.start()
        for cp in first:
            cp.start()

    def forward():
        for j, chip in enumerate(chips):
            copy(1 + j, (*chip, c), me).wait_recv()
            passed[j].start()

    def finish():
        copy(0, sibling, me).wait_recv()
        for j, chip in enumerate(chips):
            copy(4 + j, (*chip, 1 - c), me).wait_recv()
        for cp in first + passed:
            cp.wait_send()
        mine.wait()

    return start, forward, finish


def _scatter_phases(p_ref, out_ref, send_sems, recv_sems, local_sem):
    x, y, c = _mesh_pos()
    me = 4 * x + 2 * y + c
    flips = [(fx, fy, fc) for fx in (0, 1) for fy in (0, 1) for fc in (0, 1)][1:]

    def copy(k, flip, receiving):
        px, py, pc = (1 - x if flip[0] else x, 1 - y if flip[1] else y, 1 - c if flip[2] else c)
        them = 4 * px + 2 * py + pc
        return pltpu.make_async_remote_copy(
            src_ref=p_ref.at[them], dst_ref=out_ref.at[them if receiving else me],
            send_sem=send_sems.at[k], recv_sem=recv_sems.at[k],
            device_id=(px, py, pc), device_id_type=pl.DeviceIdType.MESH)

    mine = pltpu.make_async_copy(p_ref.at[me], out_ref.at[me], local_sem)

    def start():
        mine.start()
        for k, flip in enumerate(flips):
            copy(k, flip, False).start()

    def finish():
        for k, flip in enumerate(flips):
            copy(k, flip, True).wait_recv()
            copy(k, flip, False).wait_send()
        mine.wait()

    return start, None, finish


class Rider(NamedTuple):
    kind: str
    src: jax.Array

    def out_shape(self):
        shape = (N_DEV,) + self.src.shape if self.kind == "gather" else self.src.shape
        return _sds(shape, self.src.dtype)


def _rider_hooks(riders, in_refs, out_refs, sem_refs, step, n_steps):
    phases = [(_gather_phases if r.kind == "gather" else _scatter_phases)(
                  in_refs[i], out_refs[i], *sem_refs[3 * i:3 * i + 3]) for i, r in enumerate(riders)]

    def begin():
        for start, forward, _ in phases:
            pl.when(step == 0)(start)
            if forward is not None:
                pl.when(step == (7 * n_steps) // 8)(forward)

    def end():
        for _, _, finish in phases:
            pl.when(step == n_steps - 1)(finish)

    return begin, end


def _split_refs(refs, n_in, n_out, n_scratch, n_riders):
    pos, parts = 0, []
    for n in (n_in, n_riders, n_out, n_riders, n_scratch, 3 * n_riders):
        parts.append(refs[pos:pos + n])
        pos += n
    return parts


def all_gather(shard, name):
    def body(x_ref, out_ref, send_sems, recv_sems, local_sem):
        start, forward, finish = _gather_phases(x_ref, out_ref, send_sems, recv_sems, local_sem)
        start()
        forward()
        finish()

    return pl.pallas_call(
        body, name=name, out_shape=_sds((N_DEV,) + shard.shape, shard.dtype),
        in_specs=[ANY], out_specs=ANY, scratch_shapes=_comm_sems(),
    )(shard)


def adamw(w, m, v, parts, name):
    n_parts, rows, cols = parts.shape
    tr = _row_tile(rows, 128)
    c1 = 1.0 - ADAM_B1 ** ADAM_STEP
    c2 = 1.0 - ADAM_B2 ** ADAM_STEP

    def body(w_ref, m_ref, v_ref, p_ref, g_ref, d_ref, nm_ref, nv_ref):
        g = p_ref[0].astype(F32)
        for s in range(1, n_parts):
            g = g + p_ref[s].astype(F32)
        nm = ADAM_B1 * m_ref[...] + (1.0 - ADAM_B1) * g
        nv = ADAM_B2 * v_ref[...] + (1.0 - ADAM_B2) * (g * g)
        delta = -ADAM_LR * ((nm / c1) / (jnp.sqrt(nv / c2) + ADAM_EPS) + ADAM_WD * w_ref[...])
        g_ref[...] = g
        d_ref[...] = delta
        nm_ref[...] = nm
        nv_ref[...] = nv

    mat = pl.BlockSpec((tr, cols), lambda i: (i, 0))
    return pl.pallas_call(
        body, name=name, grid=(rows // tr,),
        in_specs=[mat, mat, mat, pl.BlockSpec((n_parts, tr, cols), lambda i: (0, i, 0))],
        out_specs=[mat, mat, mat, mat],
        out_shape=[_sds((rows, cols), F32)] * 4,
        compiler_params=_params("parallel"),
    )(w, m, v, parts)


_NN = (((1,), (0,)), ((), ()))
_NT = (((1,), (1,)), ((), ()))
_TN = (((0,), (0,)), ((), ()))


def mm_cols(name, a, b_list, b_specs, nt, extras, extra_specs, out_shapes, out_specs, epilogue, n_blk, riders=()):
    t_len, k_len = a.shape
    tm = _row_tile(t_len, 512)
    nb, ne, n_out, nr = len(b_list), len(extras), len(out_shapes), len(riders)
    t_steps = t_len // tm

    def body(*refs):
        ins, r_in, outs, r_out, _, r_sem = _split_refs(refs, 1 + nb + ne, n_out, 0, nr)
        step = pl.program_id(0) * t_steps + pl.program_id(1)
        begin, end = _rider_hooks(riders, r_in, r_out, r_sem, step, n_blk * t_steps)
        begin()
        av = ins[0][...]
        accs = [lax.dot_general(av, br[...], _NT if nt else _NN, preferred_element_type=F32)
                for br in ins[1:1 + nb]]
        epilogue(accs, ins[1 + nb:], outs)
        end()

    res = pl.pallas_call(
        body, name=name, grid=(n_blk, t_steps),
        in_specs=([pl.BlockSpec((tm, k_len), lambda j, t: (t, 0))] + list(b_specs) + list(extra_specs(tm))
                  + [ANY] * nr),
        out_specs=list(out_specs(tm)) + [ANY] * nr,
        out_shape=list(out_shapes) + [r.out_shape() for r in riders],
        scratch_shapes=_comm_sems() * nr,
        compiler_params=_params("arbitrary", "arbitrary"),
    )(a, *b_list, *extras, *[r.src for r in riders])
    return res[:n_out], res[n_out:]


def mm_reduce(name, a_list, a_specs, b_list, b_specs, nt, res, scale, t_len, n_len, n_blk, riders=()):
    tm = _row_tile(t_len, 512)
    na, nr = len(a_list), len(riders)
    has_res = res is not None
    t_steps = t_len // tm

    def body(*refs):
        ins, r_in, outs, r_out, scr, r_sem = _split_refs(refs, 2 * na + has_res, 1, 1, nr)
        o_ref, acc = outs[0], scr[0]
        j = pl.program_id(1)
        step = pl.program_id(0) * n_blk + j
        begin, end = _rider_hooks(riders, r_in, r_out, r_sem, step, t_steps * n_blk)
        begin()

        @pl.when(j == 0)
        def _():
            acc[...] = jnp.zeros_like(acc)

        part = None
        for ar, br in zip(ins[:na], ins[na:2 * na]):
            d = lax.dot_general(ar[...], br[...], _NT if nt else _NN, preferred_element_type=F32)
            part = d if part is None else part + d
        acc[...] += part

        @pl.when(j == n_blk - 1)
        def _():
            val = acc[...] * scale if scale != 1.0 else acc[...]
            o_ref[...] = ins[2 * na][...] + val if has_res else val

        end()

    row = pl.BlockSpec((tm, n_len), lambda t, j: (t, 0))
    out = pl.pallas_call(
        body, name=name, grid=(t_steps, n_blk),
        in_specs=list(a_specs(tm)) + list(b_specs) + ([row] if has_res else []) + [ANY] * nr,
        out_specs=[row] + [ANY] * nr,
        out_shape=[_sds((t_len, n_len), F32)] + [r.out_shape() for r in riders],
        scratch_shapes=[pltpu.VMEM((tm, n_len), F32)] + _comm_sems() * nr,
        compiler_params=_params("arbitrary", "arbitrary"),
    )(*a_list, *b_list, *([res] if has_res else []), *[r.src for r in riders])
    return out[0], out[1:]


def mm_tn(name, x, x_spec, dy_list, dy_specs, out_shapes, out_specs, scale, t_len, n_blk, riders=(),
          x_transposed=False):
    tt = _row_tile(t_len, 2048)
    nd, nr = len(dy_list), len(riders)
    t_steps = t_len // tt
    acc_shapes = [pltpu.VMEM(spec.block_shape[-2:], F32) for spec in out_specs]

    def body(*refs):
        ins, r_in, outs, r_out, accs, r_sem = _split_refs(refs, 1 + nd, nd, nd, nr)
        t = pl.program_id(1)
        step = pl.program_id(0) * t_steps + t
        begin, end = _rider_hooks(riders, r_in, r_out, r_sem, step, n_blk * t_steps)
        begin()
        xv = ins[0][...]
        for dr, acc in zip(ins[1:], accs):
            d = lax.dot_general(xv, dr[...], _NN if x_transposed else _TN, preferred_element_type=F32)

            @pl.when(t == 0)
            def _():
                acc[...] = d

            @pl.when(t > 0)
            def _():
                acc[...] += d

        @pl.when(t == t_steps - 1)
        def _():
            for acc, orf in zip(accs, outs):
                val = acc[...] * scale if scale != 1.0 else acc[...]
                orf[...] = val.astype(orf.dtype)

        end()

    res = pl.pallas_call(
        body, name=name, grid=(n_blk, t_steps),
        in_specs=[x_spec(tt)] + list(dy_specs(tt)) + [ANY] * nr,
        out_specs=list(out_specs) + [ANY] * nr,
        out_shape=list(out_shapes) + [r.out_shape() for r in riders],
        scratch_shapes=acc_shapes + _comm_sems() * nr,
        compiler_params=_params("arbitrary", "arbitrary"),
    )(x, *dy_list, *[r.src for r in riders])
    return res[:nd], res[nd:]


def rms_fwd(x, g, name):
    t_len, d = x.shape
    tm = _row_tile(t_len, 512)

    def body(x_ref, g_ref, h_ref, ht_ref):
        xv = x_ref[...]
        r = lax.rsqrt(jnp.mean(xv * xv, axis=-1, keepdims=True) + EPS)
        hv = xv * r * g_ref[...]
        h_ref[...] = hv.astype(BF16)
        ht_ref[...] = hv.T.astype(BF16)

    row = pl.BlockSpec((tm, d), lambda i: (i, 0))
    return pl.pallas_call(
        body, name=name, grid=(t_len // tm,),
        in_specs=[row, pl.BlockSpec((1, d), lambda i: (0, 0))],
        out_specs=[row, pl.BlockSpec((d, tm), lambda i: (0, i))],
        out_shape=[_sds((t_len, d), BF16), _sds((d, t_len), BF16)],
        compiler_params=_params("parallel"),
    )(x, g)


def rms_bwd(x, g, dh, dres, name):
    t_len, d = x.shape
    tm = _row_tile(t_len, 512)

    def body(x_ref, g_ref, dh_ref, dr_ref, dx_ref, dxb_ref, dg_ref):
        i = pl.program_id(0)
        xv = x_ref[...]
        r = lax.rsqrt(jnp.mean(xv * xv, axis=-1, keepdims=True) + EPS)
        xh = xv * r
        dhv = dh_ref[...]

        @pl.when(i == 0)
        def _():
            dg_ref[...] = jnp.zeros_like(dg_ref)

        dg_ref[...] += jnp.sum(dhv * xh, axis=0, keepdims=True)
        dxh = dhv * g_ref[...]
        dx = dr_ref[...] + r * (dxh - xh * jnp.mean(dxh * xh, axis=-1, keepdims=True))
        dx_ref[...] = dx
        dxb_ref[...] = dx.astype(BF16)

    row = pl.BlockSpec((tm, d), lambda i: (i, 0))
    vec = pl.BlockSpec((1, d), lambda i: (0, 0))
    return pl.pallas_call(
        body, name=name, grid=(t_len // tm,),
        in_specs=[row, vec, row, row],
        out_specs=[row, row, vec],
        out_shape=[_sds((t_len, d), F32), _sds((t_len, d), BF16), _sds((1, d), F32)],
        compiler_params=_params("arbitrary"),
    )(x, g, dh, dres)


def loss_head(y, target, name):
    t_len, d = y.shape
    tm = _row_tile(t_len, 512)

    def body(y_ref, t_ref, l_ref, dy_ref, dyb_ref):
        i = pl.program_id(0)
        err = y_ref[...] - t_ref[...]

        @pl.when(i == 0)
        def _():
            l_ref[...] = jnp.zeros_like(l_ref)

        rows = jnp.sum(err * err, axis=-1, keepdims=True) * (1.0 / d)
        l_ref[...] += 0.5 * jnp.sum(rows, axis=0, keepdims=True)
        dy = err * (1.0 / d)
        dy_ref[...] = dy
        dyb_ref[...] = dy.astype(BF16)

    row = pl.BlockSpec((tm, d), lambda i: (i, 0))
    return pl.pallas_call(
        body, name=name, grid=(t_len // tm,),
        in_specs=[row, row],
        out_specs=[pl.BlockSpec((8, LANES), lambda i: (0, 0)), row, row],
        out_shape=[_sds((8, LANES), F32), _sds((t_len, d), F32), _sds((t_len, d), BF16)],
        compiler_params=_params("arbitrary"),
    )(y, target)


def _conv_specs(tm, ch):
    per = tm // HALO
    cur = lambda cb: pl.BlockSpec((tm, ch), lambda i: (i, cb))
    prev = lambda cb: pl.BlockSpec((HALO, ch), lambda i: (jnp.maximum(i * per - 1, 0), cb))
    return [cur(0), cur(1), prev(0), prev(1)]


def _fill_glu(ext, a_ref, gt_ref, ap_ref, gp_ref, i, tm):
    vp = ap_ref[...] * _sigmoid(gp_ref[...])
    ext[0:HALO, :] = jnp.where(i > 0, vp, 0.0)
    ext[HALO:HALO + tm, :] = a_ref[...] * _sigmoid(gt_ref[...])


def _conv_rows(ext, w_ref, b_ref, r0):
    acc = jnp.broadcast_to(b_ref[...], (ROW_CHUNK, b_ref.shape[1]))
    for k in range(CONV_WIDTH):
        acc = acc + w_ref[k:k + 1, :] * ext[pl.ds(r0 + HALO - (CONV_WIDTH - 1) + k, ROW_CHUNK), :]
    return acc


def _layer_norm(yv):
    mu = jnp.mean(yv, axis=-1, keepdims=True)
    cen = yv - mu
    var = jnp.mean(cen * cen, axis=-1, keepdims=True)
    rstd = lax.rsqrt(var + EPS)
    return cen * rstd, rstd


def conv_fwd(z, w, b, lg, lb, name):
    t_len = z.shape[0]
    ch = w.shape[1]
    tm = _row_tile(t_len, 256)

    def body(a_ref, gt_ref, ap_ref, gp_ref, w_ref, b_ref, lg_ref, lb_ref, y_ref, pre_ref, ext):
        i = pl.program_id(0)
        _fill_glu(ext, a_ref, gt_ref, ap_ref, gp_ref, i, tm)
        for r0 in range(0, tm, ROW_CHUNK):
            pre = _conv_rows(ext, w_ref, b_ref, r0)
            pre_ref[r0:r0 + ROW_CHUNK, :] = pre
            xh, _ = _layer_norm(pre)
            u = xh * lg_ref[...] + lb_ref[...]
            y_ref[r0:r0 + ROW_CHUNK, :] = (u * _sigmoid(u)).astype(BF16)

    vec = pl.BlockSpec((1, ch), lambda i: (0, 0))
    row = pl.BlockSpec((tm, ch), lambda i: (i, 0))
    return pl.pallas_call(
        body, name=name, grid=(t_len // tm,),
        in_specs=_conv_specs(tm, ch) + [pl.BlockSpec((32, ch), lambda i: (0, 0)), vec, vec, vec],
        out_specs=[row, row],
        out_shape=[_sds((t_len, ch), BF16), _sds((t_len, ch), F32)],
        scratch_shapes=[pltpu.VMEM((HALO + tm, ch), F32)],
        compiler_params=_params("parallel"),
    )(z, z, z, z, w, b, lg, lb)


def conv_bwd_norm(pre, dy_cat, lg, lb, name):
    t_len, ch = pre.shape
    tm = _row_tile(t_len, 256)

    def body(pre_ref, dy_ref, lg_ref, lb_ref, dc_ref, dlg_ref, dlb_ref, db_ref):
        i = pl.program_id(0)

        @pl.when(i == 0)
        def _():
            dlg_ref[...] = jnp.zeros_like(dlg_ref)
            dlb_ref[...] = jnp.zeros_like(dlb_ref)
            db_ref[...] = jnp.zeros_like(db_ref)

        for r0 in range(0, tm, ROW_CHUNK):
            xh, rstd = _layer_norm(pre_ref[r0:r0 + ROW_CHUNK, :])
            u = xh * lg_ref[...] + lb_ref[...]
            sg = _sigmoid(u)
            du = dy_ref[r0:r0 + ROW_CHUNK, :] * (sg * (1.0 + u * (1.0 - sg)))
            dlg_ref[...] += jnp.sum(du * xh, axis=0, keepdims=True)
            dlb_ref[...] += jnp.sum(du, axis=0, keepdims=True)
            dxh = du * lg_ref[...]
            dc = rstd * (dxh - jnp.mean(dxh, axis=-1, keepdims=True)
                         - xh * jnp.mean(dxh * xh, axis=-1, keepdims=True))
            db_ref[...] += jnp.sum(dc, axis=0, keepdims=True)
            dc_ref[r0:r0 + ROW_CHUNK, :] = dc

    vec = pl.BlockSpec((1, ch), lambda i: (0, 0))
    row = pl.BlockSpec((tm, ch), lambda i: (i, 0))
    return pl.pallas_call(
        body, name=name, grid=(t_len // tm,),
        in_specs=[row, row, vec, vec],
        out_specs=[row, vec, vec, vec],
        out_shape=[_sds((t_len, ch), F32)] + [_sds((1, ch), F32)] * 3,
        compiler_params=_params("arbitrary"),
    )(pre, dy_cat, lg, lb)


def conv_bwd_taps(z, dc, w, name):
    t_len = z.shape[0]
    ch = w.shape[1]
    tm = _row_tile(t_len, 256)
    per = tm // HALO
    n_tiles = t_len // tm
    last_halo = t_len // HALO - 1

    def body(a_ref, gt_ref, ap_ref, gp_ref, dc_ref, dn_ref, w_ref, dz_a_ref, dz_g_ref, dw_ref, ext, dext):
        i = pl.program_id(0)
        _fill_glu(ext, a_ref, gt_ref, ap_ref, gp_ref, i, tm)
        dext[0:tm, :] = dc_ref[...]
        dext[tm:tm + HALO, :] = jnp.where(i < n_tiles - 1, dn_ref[...], 0.0)

        @pl.when(i == 0)
        def _():
            dw_ref[...] = jnp.zeros_like(dw_ref)

        for r0 in range(0, tm, ROW_CHUNK):
            dcv = dext[r0:r0 + ROW_CHUNK, :]
            dv = jnp.zeros((ROW_CHUNK, ch), F32)
            for k in range(CONV_WIDTH):
                dv = dv + w_ref[k:k + 1, :] * dext[pl.ds(r0 + (CONV_WIDTH - 1) - k, ROW_CHUNK), :]
                prod = dcv * ext[pl.ds(r0 + HALO - (CONV_WIDTH - 1) + k, ROW_CHUNK), :]
                fold = prod[0:8]
                for s in range(8, ROW_CHUNK, 8):
                    fold = fold + prod[s:s + 8]
                dw_ref[k] += fold
            av = a_ref[r0:r0 + ROW_CHUNK, :]
            sg = _sigmoid(gt_ref[r0:r0 + ROW_CHUNK, :])
            dz_a_ref[r0:r0 + ROW_CHUNK, :] = (dv * sg).astype(BF16)
            dz_g_ref[r0:r0 + ROW_CHUNK, :] = (dv * av * sg * (1.0 - sg)).astype(BF16)

    row = pl.BlockSpec((tm, ch), lambda i: (i, 0))
    nxt = pl.BlockSpec((HALO, ch), lambda i: (jnp.minimum((i + 1) * per, last_halo), 0))
    return pl.pallas_call(
        body, name=name, grid=(n_tiles,),
        in_specs=_conv_specs(tm, ch) + [row, nxt, pl.BlockSpec((32, ch), lambda i: (0, 0))],
        out_specs=[row, row, pl.BlockSpec((32, 8, ch), lambda i: (0, 0, 0))],
        out_shape=[_sds((t_len, ch), BF16), _sds((t_len, ch), BF16), _sds((32, 8, ch), F32)],
        scratch_shapes=[pltpu.VMEM((HALO + tm, ch), F32), pltpu.VMEM((tm + HALO, ch), F32)],
        compiler_params=_params("arbitrary"),
    )(z, z, z, z, dc, dc, w)


def _head_masks(rows):
    lane = lax.broadcasted_iota(jnp.int32, (rows, LANES), 1)
    low = lane < HEAD_DIM
    return low, jnp.logical_not(low)


def _per_head_mean(val, low):
    s_low = jnp.sum(jnp.where(low, val, 0.0), axis=-1, keepdims=True)
    s_high = jnp.sum(jnp.where(low, 0.0, val), axis=-1, keepdims=True)
    return jnp.where(low, s_low, s_high) * (1.0 / HEAD_DIM)


def qk_norm_fwd(z, g2, ch, name):
    t_len = z.shape[0]
    tm = _row_tile(t_len, 1024)
    n_col = 2 * ch // LANES
    z_off = 2 * ch // LANES

    def body(z_ref, g_ref, o_ref):
        low, _ = _head_masks(tm)
        xv = z_ref[...]
        r = lax.rsqrt(_per_head_mean(xv * xv, low) + EPS)
        o_ref[...] = xv * r * g_ref[...]

    return pl.pallas_call(
        body, name=name, grid=(t_len // tm, n_col),
        in_specs=[pl.BlockSpec((tm, LANES), lambda i, cb: (i, z_off + cb)),
                  pl.BlockSpec((1, LANES), lambda i, cb: (0, cb))],
        out_specs=pl.BlockSpec((tm, LANES), lambda i, cb: (i, cb)),
        out_shape=_sds((t_len, 2 * ch), F32),
        compiler_params=_params("parallel", "parallel"),
    )(z, g2)


def qk_norm_bwd(z, g, d_list, z_off, ch, name):
    t_len = z.shape[0]
    tm = _row_tile(t_len, 1024)
    n_col = ch // LANES
    nd = len(d_list)

    def body(*refs):
        z_ref, g_ref, d_refs = refs[0], refs[1], refs[2:2 + nd]
        dz_ref, dg_ref = refs[2 + nd], refs[3 + nd]
        first = jnp.logical_and(pl.program_id(0) == 0, pl.program_id(1) == 0)
        low, _ = _head_masks(tm)
        xv = z_ref[...]
        r = lax.rsqrt(_per_head_mean(xv * xv, low) + EPS)
        xh = xv * r
        dy = d_refs[0][...]
        for dr in d_refs[1:]:
            dy = dy + dr[...]

        @pl.when(first)
        def _():
            dg_ref[...] = jnp.zeros_like(dg_ref)

        dg_ref[...] += jnp.sum(dy * xh, axis=0, keepdims=True)
        dxh = dy * g_ref[...]
        dz_ref[...] = (r * (dxh - xh * _per_head_mean(dxh * xh, low))).astype(BF16)

    blk = pl.BlockSpec((tm, LANES), lambda i, cb: (i, cb))
    return pl.pallas_call(
        body, name=name, grid=(t_len // tm, n_col),
        in_specs=[pl.BlockSpec((tm, LANES), lambda i, cb: (i, z_off + cb)),
                  pl.BlockSpec((1, LANES), lambda i, cb: (0, 0))] + [blk] * nd,
        out_specs=[blk, pl.BlockSpec((1, LANES), lambda i, cb: (0, 0))],
        out_shape=[_sds((t_len, ch), BF16), _sds((1, LANES), F32)],
        compiler_params=_params("arbitrary", "arbitrary"),
    )(z, g, *d_list)


def _alibi_bias(n_heads, dilation):
    slopes = 2.0 ** (-ALIBI_MAX_BIAS * jnp.arange(1, n_heads + 1, dtype=F32) / n_heads)
    qi = jnp.arange(ATT_BLOCK)[:, None]
    kj = jnp.arange(ATT_BLOCK)[None, :]
    dist_cur = (qi - kj).astype(F32)
    dist_prev = (ATT_BLOCK + qi - kj).astype(F32)
    cur = jnp.where((qi >= kj)[None], -slopes[:, None, None] * (dilation * dist_cur)[None], MASKED)
    prev = jnp.where((kj >= qi)[None], -slopes[:, None, None] * (dilation * dist_prev)[None], MASKED)
    return jnp.concatenate([prev, cur], axis=-1).astype(F32)


def _stack_heads(val, low, high):
    return jnp.concatenate([jnp.where(low, val, 0.0), jnp.where(high, val, 0.0)], axis=0).astype(BF16)


def _head_rows(val, low, high):
    return jnp.concatenate([jnp.max(jnp.where(low, val, MASKED), axis=-1, keepdims=True),
                            jnp.max(jnp.where(high, val, MASKED), axis=-1, keepdims=True)], axis=0)


def _unit_scores(q2, k2, b_ref, has_prev, scale):
    s = lax.dot_general(q2, k2, _NT, preferred_element_type=F32) * scale
    s = s + b_ref[...].reshape(2 * ATT_BLOCK, 2 * ATT_BLOCK)
    col = lax.broadcasted_iota(jnp.int32, s.shape, 1)
    return jnp.where(jnp.logical_or(has_prev, col >= ATT_BLOCK), s, MASKED)


def _strided_rows(r, dilation):
    per = ATT_CHUNK // dilation
    return pl.ds(r, per, stride=dilation) if dilation > 1 else pl.ds(0, per)


def _deinterleave(dst, src_ref, dilation, base=None, dtype=None):
    per = ATT_CHUNK // dilation
    for r in range(dilation):
        val = src_ref[_strided_rows(r, dilation), :]
        val = val if dtype is None else val.astype(dtype)
        if base is None:
            dst[r * per:(r + 1) * per, :] = val
        else:
            dst[pl.ds(pl.multiple_of(base + r * per, ATT_BLOCK), per), :] = val


def _unit_rows(u, c, nb, base, pbase):
    in_chunk = lax.rem(u, jnp.int32(nb)) > 0
    has_prev = jnp.logical_or(in_chunk, c > 0)
    urow = pl.multiple_of(u * ATT_BLOCK, ATT_BLOCK)
    crow = pl.multiple_of(base + u * ATT_BLOCK, ATT_BLOCK)
    prow = pl.multiple_of(jnp.where(in_chunk, base + (u - 1) * ATT_BLOCK,
                                    pbase + (u + nb - 1) * ATT_BLOCK), ATT_BLOCK)
    return in_chunk, has_prev, urow, crow, prow


def _interleave(dst_ref, src, dilation, base=None):
    per = ATT_CHUNK // dilation
    for r in range(dilation):
        if base is None:
            val = src[r * per:(r + 1) * per, :]
        else:
            val = src[pl.ds(pl.multiple_of(base + r * per, ATT_BLOCK), per), :]
        dst_ref[_strided_rows(r, dilation), :] = val


def attn_fwd(qk, z, dilation, ch, name):
    t_len = qk.shape[0]
    pairs = ch // LANES
    nc = t_len // ATT_CHUNK
    nb = ATT_UNITS // dilation
    scale = 1.0 / math.sqrt(HEAD_DIM)
    bias = _alibi_bias(2 * pairs, dilation)

    def body(q_ref, k_ref, v_ref, b_ref, o_ref, l_ref, qd, kx, vx, od, ld):
        c = pl.program_id(1)
        slot = lax.rem(c, jnp.int32(2))
        base, pbase = slot * ATT_CHUNK, (1 - slot) * ATT_CHUNK

        @pl.when(c == 0)
        def _():
            kx[...] = jnp.zeros_like(kx)
            vx[...] = jnp.zeros_like(vx)

        _deinterleave(qd, q_ref, dilation)
        _deinterleave(kx, k_ref, dilation, base, BF16)
        _deinterleave(vx, v_ref, dilation, base, BF16)
        low, high = _head_masks(ATT_BLOCK)

        def unit(u, carry):
            _, has_prev, urow, crow, prow = _unit_rows(u, c, nb, base, pbase)
            q2 = _stack_heads(qd[pl.ds(urow, ATT_BLOCK), :], low, high)
            k2 = jnp.concatenate([kx[pl.ds(prow, ATT_BLOCK), :], kx[pl.ds(crow, ATT_BLOCK), :]], axis=0)
            v2 = jnp.concatenate([vx[pl.ds(prow, ATT_BLOCK), :], vx[pl.ds(crow, ATT_BLOCK), :]], axis=0)
            s = _unit_scores(q2, k2, b_ref, has_prev, scale)
            mx = jnp.max(s, axis=-1, keepdims=True)
            e = jnp.exp(s - mx)
            den = jnp.sum(e, axis=-1, keepdims=True)
            acc = lax.dot_general(e.astype(BF16), v2, _NN, preferred_element_type=F32) / den
            lse = jnp.broadcast_to(mx + jnp.log(den), acc.shape)
            od[pl.ds(urow, ATT_BLOCK), :] = jnp.where(low, acc[:ATT_BLOCK], acc[ATT_BLOCK:])
            ld[pl.ds(urow, ATT_BLOCK), :] = jnp.where(low, lse[:ATT_BLOCK], lse[ATT_BLOCK:])
            return carry

        lax.fori_loop(0, ATT_UNITS, unit, 0, unroll=2)
        _interleave(o_ref, od, dilation)
        _interleave(l_ref, ld, dilation)

    blk = (ATT_CHUNK, LANES)
    bias_spec = pl.BlockSpec((2, ATT_BLOCK, 2 * ATT_BLOCK), lambda p, c: (p, 0, 0))
    out_spec = pl.BlockSpec(blk, lambda p, c: (c, p))
    return pl.pallas_call(
        body, name=name, grid=(pairs, nc),
        in_specs=[pl.BlockSpec(blk, lambda p, c: (c, p)),
                  pl.BlockSpec(blk, lambda p, c: (c, pairs + p)),
                  pl.BlockSpec(blk, lambda p, c: (c, 4 * pairs + p)),
                  bias_spec],
        out_specs=[out_spec, out_spec],
        out_shape=[_sds((t_len, ch), F32)] * 2,
        scratch_shapes=[pltpu.VMEM((ATT_CHUNK, LANES), F32),
                        pltpu.VMEM((2 * ATT_CHUNK, LANES), BF16), pltpu.VMEM((2 * ATT_CHUNK, LANES), BF16),
                        pltpu.VMEM((ATT_CHUNK, LANES), F32), pltpu.VMEM((ATT_CHUNK, LANES), F32)],
        compiler_params=_params("arbitrary", "arbitrary"),
    )(qk, qk, z, bias)


def attn_combine(outs, lses, name):
    t_len, ch = outs[0].shape
    tm = _row_tile(t_len, 512)

    def body(o1, o2, o3, l1, l2, l3, out_ref, outb_ref, lg_ref):
        a, b, c = l1[...], l2[...], l3[...]
        mx = jnp.maximum(jnp.maximum(a, b), c)
        tot = mx + jnp.log(jnp.exp(a - mx) + jnp.exp(b - mx) + jnp.exp(c - mx))
        val = jnp.exp(a - tot) * o1[...] + jnp.exp(b - tot) * o2[...] + jnp.exp(c - tot) * o3[...]
        out_ref[...] = val
        outb_ref[...] = val.astype(BF16)
        lg_ref[...] = tot

    row = pl.BlockSpec((tm, ch), lambda i: (i, 0))
    return pl.pallas_call(
        body, name=name, grid=(t_len // tm,),
        in_specs=[row] * 6, out_specs=[row] * 3,
        out_shape=[_sds((t_len, ch), F32), _sds((t_len, ch), BF16), _sds((t_len, ch), F32)],
        compiler_params=_params("parallel"),
    )(*outs, *lses)


def attn_bwd(qk, z, dy_cat, out, lg, dilation, ch, name):
    t_len = qk.shape[0]
    pairs = ch // LANES
    nc = t_len // ATT_CHUNK
    nb = ATT_UNITS // dilation
    scale = 1.0 / math.sqrt(HEAD_DIM)
    bias = _alibi_bias(2 * pairs, dilation)

    def body(q_ref, k_ref, v_ref, do_ref, out_ref, lg_ref, b_ref, dq_ref, dk_ref, dv_ref,
             qd, dod, lgd, dld, dl_nat, kx, vx, dkx, dvx, dqd):
        c = pl.program_id(1)
        slot = lax.rem(c, jnp.int32(2))
        base, pbase = slot * ATT_CHUNK, (1 - slot) * ATT_CHUNK

        @pl.when(c == 0)
        def _():
            for ref in (kx, vx, dkx, dvx):
                ref[...] = jnp.zeros_like(ref)

        @pl.when(c < nc)
        def _():
            low_all, _ = _head_masks(ATT_CHUNK)
            dl_nat[...] = _per_head_mean(do_ref[...] * out_ref[...], low_all) * float(HEAD_DIM)
            _deinterleave(qd, q_ref, dilation)
            _deinterleave(dod, do_ref, dilation)
            _deinterleave(lgd, lg_ref, dilation)
            _deinterleave(dld, dl_nat, dilation)
            _deinterleave(kx, k_ref, dilation, base, BF16)
            _deinterleave(vx, v_ref, dilation, base, BF16)
            cur = pl.ds(pl.multiple_of(base, ATT_CHUNK), ATT_CHUNK)
            dkx[cur, :] = jnp.zeros((ATT_CHUNK, LANES), F32)
            dvx[cur, :] = jnp.zeros((ATT_CHUNK, LANES), F32)
            low, high = _head_masks(ATT_BLOCK)

            def unit(u, carry):
                _, has_prev, urow, crow, prow = _unit_rows(u, c, nb, base, pbase)
                rows = pl.ds(urow, ATT_BLOCK)
                q2 = _stack_heads(qd[rows, :], low, high)
                do2 = _stack_heads(dod[rows, :], low, high)
                lse = _head_rows(lgd[rows, :], low, high)
                delta = _head_rows(dld[rows, :], low, high)
                k2 = jnp.concatenate([kx[pl.ds(prow, ATT_BLOCK), :], kx[pl.ds(crow, ATT_BLOCK), :]], axis=0)
                v2 = jnp.concatenate([vx[pl.ds(prow, ATT_BLOCK), :], vx[pl.ds(crow, ATT_BLOCK), :]], axis=0)
                prob = jnp.exp(_unit_scores(q2, k2, b_ref, has_prev, scale) - lse)
                dp = lax.dot_general(do2, v2, _NT, preferred_element_type=F32)
                ds = (prob * (dp - delta)).astype(BF16)
                dq2 = lax.dot_general(ds, k2, _NN, preferred_element_type=F32)
                dk2 = lax.dot_general(ds, q2, _TN, preferred_element_type=F32)
                dv2 = lax.dot_general(prob.astype(BF16), do2, _TN, preferred_element_type=F32)
                dqd[rows, :] = scale * jnp.where(low, dq2[:ATT_BLOCK], dq2[ATT_BLOCK:])
                dkx[pl.ds(prow, ATT_BLOCK), :] += scale * dk2[:ATT_BLOCK]
                dkx[pl.ds(crow, ATT_BLOCK), :] += scale * dk2[ATT_BLOCK:]
                dvx[pl.ds(prow, ATT_BLOCK), :] += dv2[:ATT_BLOCK]
                dvx[pl.ds(crow, ATT_BLOCK), :] += dv2[ATT_BLOCK:]
                return carry

            lax.fori_loop(0, ATT_UNITS, unit, 0, unroll=2)
            _interleave(dq_ref, dqd, dilation)

        @pl.when(c > 0)
        def _():
            _interleave(dk_ref, dkx, dilation, pbase)
            _interleave(dv_ref, dvx, dilation, pbase)

    blk = (ATT_CHUNK, LANES)
    here = lambda c: jnp.minimum(c, nc - 1)
    spec = lambda off: pl.BlockSpec(blk, lambda p, c: (here(c), off + p))
    late = pl.BlockSpec(blk, lambda p, c: (jnp.maximum(c - 1, 0), p))
    bias_spec = pl.BlockSpec((2, ATT_BLOCK, 2 * ATT_BLOCK), lambda p, c: (p, 0, 0))
    f32_chunk = pltpu.VMEM((ATT_CHUNK, LANES), F32)
    return pl.pallas_call(
        body, name=name, grid=(pairs, nc + 1),
        in_specs=[spec(0), spec(pairs), spec(4 * pairs), spec(pairs), spec(0), spec(0), bias_spec],
        out_specs=[spec(0), late, late],
        out_shape=[_sds((t_len, ch), F32)] * 3,
        scratch_shapes=[f32_chunk] * 5
                       + [pltpu.VMEM((2 * ATT_CHUNK, LANES), BF16)] * 2
                       + [pltpu.VMEM((2 * ATT_CHUNK, LANES), F32)] * 2 + [f32_chunk],
        compiler_params=_params("arbitrary", "arbitrary"),
    )(qk, qk, z, dy_cat, out, lg, bias)


def sum3_bf16(a, b, c, name):
    t_len, ch = a.shape
    tm = _row_tile(t_len, 512)

    def body(a_ref, b_ref, c_ref, o_ref):
        o_ref[...] = (a_ref[...] + b_ref[...] + c_ref[...]).astype(BF16)

    row = pl.BlockSpec((tm, ch), lambda i: (i, 0))
    return pl.pallas_call(
        body, name=name, grid=(t_len // tm,), in_specs=[row] * 3, out_specs=row,
        out_shape=_sds((t_len, ch), BF16), compiler_params=_params("parallel"),
    )(a, b, c)


def _blk3(rows, cols):
    return pl.BlockSpec((None, rows, cols), lambda j, t: (j, 0, 0))


def ffn_up(h, wgu, name, riders):
    t_len, d = h.shape
    n_blk, _, _, fj = wgu.shape

    def epilogue(accs, e_refs, o_refs):
        gate, up = accs
        o_refs[0][...] = gate.astype(BF16)
        o_refs[1][...] = up.astype(BF16)
        o_refs[2][...] = (gate * _sigmoid(gate) * up).astype(BF16)

    w_spec = lambda i: pl.BlockSpec((None, None, d, fj), lambda j, t: (j, i, 0, 0))
    act = lambda tm: pl.BlockSpec((None, tm, fj), lambda j, t: (j, t, 0))
    return mm_cols(name, h, [wgu, wgu], [w_spec(0), w_spec(1)], False, [], lambda tm: [],
                   [_sds((n_blk, t_len, fj), BF16)] * 3, lambda tm: [act(tm)] * 3, epilogue, n_blk, riders)


def ffn_down(act, wd, res, name, riders):
    n_blk, t_len, fj = act.shape
    d = wd.shape[2]
    return mm_reduce(name, [act], lambda tm: [pl.BlockSpec((None, tm, fj), lambda t, j: (j, t, 0))],
                     [wd], [pl.BlockSpec((None, fj, d), lambda t, j: (j, 0, 0))], False, res, 0.5,
                     t_len, d, n_blk, riders)


def ffn_bwd(ht, gate, up, act, wgu, wd, dyb, name):
    d, t_len = ht.shape
    n_blk, _, fj = act.shape

    def epilogue(accs, e_refs, o_refs):
        d_act = 0.5 * accs[0]
        gv, uv = e_refs[0][...].astype(F32), e_refs[1][...].astype(F32)
        sg = _sigmoid(gv)
        o_refs[0][...] = (d_act * uv * (sg * (1.0 + gv * (1.0 - sg)))).astype(BF16)
        o_refs[1][...] = (d_act * gv * sg).astype(BF16)

    act_jt = lambda tm: pl.BlockSpec((None, tm, fj), lambda j, t: (j, t, 0))
    (d_gate, d_up), _ = mm_cols(name + "_dact", dyb, [wd], [_blk3(fj, d)], True, [gate, up],
                                lambda tm: [act_jt(tm)] * 2, [_sds((n_blk, t_len, fj), BF16)] * 2,
                                lambda tm: [act_jt(tm)] * 2, epilogue, n_blk)

    ht_spec = lambda tt: pl.BlockSpec((d, tt), lambda j, t: (0, t))
    (d_wg,), _ = mm_tn(name + "_dwg", ht, ht_spec, [d_gate], lambda tt: [act_jt(tt)],
                       [_sds((n_blk, d, fj), BF16)], [_blk3(d, fj)], 1.0, t_len, n_blk, x_transposed=True)
    (d_wu,), (recv_wg,) = mm_tn(name + "_dwu", ht, ht_spec, [d_up], lambda tt: [act_jt(tt)],
                                [_sds((n_blk, d, fj), BF16)], [_blk3(d, fj)], 1.0, t_len, n_blk,
                                [Rider("scatter", d_wg)], x_transposed=True)

    (d_wd,), (recv_wu,) = mm_tn(name + "_dwd", act, act_jt, [dyb],
                                lambda tt: [pl.BlockSpec((tt, d), lambda j, t: (t, 0))],
                                [_sds((n_blk, fj, d), BF16)], [_blk3(fj, d)], 0.5, t_len, n_blk,
                                [Rider("scatter", d_wu)])

    act_tj = lambda tm: pl.BlockSpec((None, tm, fj), lambda t, j: (j, t, 0))
    w_in = lambda i: pl.BlockSpec((None, None, d, fj), lambda t, j: (j, i, 0, 0))
    dh, (recv_wd,) = mm_reduce(
        name + "_dh", [d_gate, d_up], lambda tm: [act_tj(tm)] * 2,
        [wgu, wgu], [w_in(0), w_in(1)], True, None, 1.0, t_len, d, n_blk,
        [Rider("scatter", d_wd)])
    return dh, recv_wg, recv_wu, recv_wd


def local_step(x, target, g1, wgu1, wd1_s, gmix, win_s, conv_w, conv_b, ln_g, ln_b, gq, gk, wout_s, g3,
               wgu2_s, wd2_s):
    t_len, d = x.shape
    ch = d // 2
    ij = win_s.shape[1]
    oj = wout_s.shape[0]
    n_blk = N_DEV

    h1, h1t = rms_fwd(x, g1, "rms1")
    (gate1, up1, act1), (wd1, wgu2) = ffn_up(h1, wgu1, "ffn1_up",
                                             [Rider("gather", wd1_s), Rider("gather", wgu2_s)])
    x1, (win, wout) = ffn_down(act1, wd1, x, "ffn1_down", [Rider("gather", win_s), Rider("gather", wout_s)])

    h2, h2t = rms_fwd(x1, gmix, "rms_mix")

    def store_f32(accs, e_refs, o_refs):
        o_refs[0][...] = accs[0]

    (z,), (wd2,) = mm_cols(
        "w_in", h2, [win], [_blk3(d, ij)], False, [], lambda tm: [],
        [_sds((t_len, n_blk * ij), F32)],
        lambda tm: [pl.BlockSpec((tm, ij), lambda j, t: (t, j))], store_f32, n_blk,
        [Rider("gather", wd2_s)])

    conv_w32 = jnp.pad(conv_w, ((0, 32 - CONV_WIDTH), (0, 0)))
    y_conv, conv_pre = conv_fwd(z, conv_w32, conv_b, ln_g, ln_b, "conv_fwd")

    g2 = jnp.concatenate([jnp.tile(gq, (1, ch // HEAD_DIM)), jnp.tile(gk, (1, ch // HEAD_DIM))], axis=1)
    qk = qk_norm_fwd(z, g2, ch, "qk_norm")
    branch = [attn_fwd(qk, z, dil, ch, "attn_fwd_d%d" % dil) for dil in DILATIONS]
    att, att_b, lg = attn_combine([o for o, _ in branch], [l for _, l in branch], "attn_combine")

    y_cat = jnp.concatenate([y_conv, att_b], axis=1)
    wout_full = wout.reshape(1, n_blk * oj, d)
    x2, _ = mm_reduce(
        "w_out", [y_cat], lambda tm: [pl.BlockSpec((tm, n_blk * oj), lambda t, j: (t, 0))],
        [wout_full], [pl.BlockSpec((None, n_blk * oj, d), lambda t, j: (0, 0, 0))], False, x1, 1.0,
        t_len, d, 1)

    h3, h3t = rms_fwd(x2, g3, "rms3")
    (gate2, up2, act2), _ = ffn_up(h3, wgu2, "ffn2_up", [])
    y, _ = ffn_down(act2, wd2, x2, "ffn2_down", [])

    loss_tile, dy, dyb = loss_head(y, target, "loss")

    dh3, recv_wg2, recv_wu2, recv_wd2 = ffn_bwd(h3t, gate2, up2, act2, wgu2, wd2, dyb, "ffn2")
    dx2, dx2b, d_g3 = rms_bwd(x2, g3, dh3, dy, "rms3_bwd")

    (dy_cat,), _ = mm_cols("w_out_dy", dx2b, [wout_full], [_blk3(n_blk * oj, d)], True, [], lambda tm: [],
                           [_sds((t_len, n_blk * oj), F32)],
                           lambda tm: [pl.BlockSpec((tm, n_blk * oj), lambda j, t: (t, 0))], store_f32, 1)
    (d_wout,), _ = mm_tn("w_out_dw", y_cat, lambda tt: pl.BlockSpec((tt, oj), lambda j, t: (t, j)),
                         [dx2b], lambda tt: [pl.BlockSpec((tt, d), lambda j, t: (t, 0))],
                         [_sds((n_blk, oj, d), BF16)], [_blk3(oj, d)], 1.0, t_len, n_blk)

    dc, d_lg, d_lb, d_cb = conv_bwd_norm(conv_pre, dy_cat, ln_g, ln_b, "conv_bwd_norm")
    dz_a, dz_g, d_cw8 = conv_bwd_taps(z, dc, conv_w32, "conv_bwd_taps")
    d_cw = jnp.sum(d_cw8, axis=1)[:CONV_WIDTH]

    grads = [attn_bwd(qk, z, dy_cat, att, lg, dil, ch, "attn_bwd_d%d" % dil) for dil in DILATIONS]
    gq_t = jnp.tile(gq, (1, LANES // HEAD_DIM))
    gk_t = jnp.tile(gk, (1, LANES // HEAD_DIM))
    dz_q, d_gq2 = qk_norm_bwd(z, gq_t, [g[0] for g in grads], 2 * ch // LANES, ch, "q_norm_bwd")
    dz_k, d_gk2 = qk_norm_bwd(z, gk_t, [g[1] for g in grads], 3 * ch // LANES, ch, "k_norm_bwd")
    d_gq = d_gq2[:, :HEAD_DIM] + d_gq2[:, HEAD_DIM:]
    d_gk = d_gk2[:, :HEAD_DIM] + d_gk2[:, HEAD_DIM:]
    dz_v = sum3_bf16(grads[0][2], grads[1][2], grads[2][2], "dv_sum")
    dzb = jnp.concatenate([dz_a, dz_g, dz_q, dz_k, dz_v], axis=1)

    (d_win,), (recv_wout,) = mm_tn(
        "w_in_dw", h2t, lambda tt: pl.BlockSpec((d, tt), lambda j, t: (0, t)),
        [dzb], lambda tt: [pl.BlockSpec((tt, ij), lambda j, t: (t, j))],
        [_sds((n_blk, d, ij), BF16)], [_blk3(d, ij)], 1.0, t_len, n_blk, [Rider("scatter", d_wout)],
        x_transposed=True)
    dh2, (recv_win,) = mm_reduce(
        "w_in_dh", [dzb], lambda tm: [pl.BlockSpec((tm, ij), lambda t, j: (t, j))],
        [win], [pl.BlockSpec((None, d, ij), lambda t, j: (j, 0, 0))], True, None, 1.0,
        t_len, d, n_blk, [Rider("scatter", d_win)])
    dx1, dx1b, d_gmix = rms_bwd(x1, gmix, dh2, dx2, "rms_mix_bwd")

    dh1, recv_wg1, recv_wu1, recv_wd1 = ffn_bwd(h1t, gate1, up1, act1, wgu1, wd1, dx1b, "ffn1")
    grad_x, _, d_g1 = rms_bwd(x, g1, dh1, dx1, "rms1_bwd")

    big = dict(ffn1_w_gate=recv_wg1, ffn1_w_up=recv_wu1, ffn1_w_down=recv_wd1, w_in=recv_win, w_out=recv_wout,
               ffn2_w_gate=recv_wg2, ffn2_w_up=recv_wu2, ffn2_w_down=recv_wd2)
    small = dict(g1=d_g1, gmix=d_gmix, g3=d_g3, conv_b=d_cb, ln_g=d_lg, ln_b=d_lb, gq=d_gq, gk=d_gk, conv_w=d_cw)
    return loss_tile[0, 0], grad_x, big, small


SMALL_ROWS = 48


def _pack_small(ch, g1, gmix, g3, conv_b, ln_g, ln_b, gq, gk, conv_w):
    pad_head = lambda v: jnp.pad(v, ((0, 0), (0, ch - v.shape[1])))
    rows = [g1.reshape(2, ch), gmix.reshape(2, ch), g3.reshape(2, ch), conv_b, ln_g, ln_b,
            pad_head(gq), pad_head(gk), conv_w]
    packed = jnp.concatenate(rows, axis=0)
    return jnp.pad(packed, ((0, SMALL_ROWS - packed.shape[0]), (0, 0)))


def _unpack_small(packed, d):
    return dict(g1=packed[0:2].reshape(1, d), gmix=packed[2:4].reshape(1, d), g3=packed[4:6].reshape(1, d),
                conv_b=packed[6:7], ln_g=packed[7:8], ln_b=packed[8:9],
                gq=packed[9:10, :HEAD_DIM], gk=packed[10:11, :HEAD_DIM])


def kernel(x, ffn1_norm_g, ffn1_w_gate, ffn1_w_up, ffn1_w_down, mix_norm_g, w_in, conv_w_dw, conv_b_dw, conv_ln_g, conv_ln_b, q_norm_g, k_norm_g, w_out, ffn2_norm_g, ffn2_w_gate, ffn2_w_up, ffn2_w_down, loss_target, m_ffn1_norm_g, m_ffn1_w_gate, m_ffn1_w_up, m_ffn1_w_down, m_mix_norm_g, m_w_in, m_conv_w_dw, m_conv_b_dw, m_conv_ln_g, m_conv_ln_b, m_q_norm_g, m_k_norm_g, m_w_out, m_ffn2_norm_g, m_ffn2_w_gate, m_ffn2_w_up, m_ffn2_w_down, v_ffn1_norm_g, v_ffn1_w_gate, v_ffn1_w_up, v_ffn1_w_down, v_mix_norm_g, v_w_in, v_conv_w_dw, v_conv_b_dw, v_conv_ln_g, v_conv_ln_b, v_q_norm_g, v_k_norm_g, v_w_out, v_ffn2_norm_g, v_ffn2_w_gate, v_ffn2_w_up, v_ffn2_w_down):
    d = x.shape[-1]
    ch = d // 2
    me = 4 * lax.axis_index("x") + 2 * lax.axis_index("y") + lax.axis_index("c")

    gu = lambda wg, wu: jnp.stack([wg[0], wu[0]]).astype(BF16)
    wgu1 = all_gather(gu(ffn1_w_gate, ffn1_w_up), "ag_wgu1")
    cw_all = all_gather(conv_w_dw[0], "ag_convw")
    conv_w = jnp.transpose(cw_all, (1, 0, 2)).reshape(CONV_WIDTH, ch)

    loss_part, grad_x, big, small = local_step(
        x[0], loss_target[0], ffn1_norm_g, wgu1, ffn1_w_down[0].astype(BF16), mix_norm_g, w_in[0].astype(BF16),
        conv_w, conv_b_dw, conv_ln_g, conv_ln_b, q_norm_g, k_norm_g, w_out[0].astype(BF16), ffn2_norm_g,
        gu(ffn2_w_gate, ffn2_w_up), ffn2_w_down[0].astype(BF16))
    loss = lax.psum(loss_part, MESH_AXES)

    state = dict(
        ffn1_w_gate=(ffn1_w_gate, m_ffn1_w_gate, v_ffn1_w_gate), ffn1_w_up=(ffn1_w_up, m_ffn1_w_up, v_ffn1_w_up),
        ffn1_w_down=(ffn1_w_down, m_ffn1_w_down, v_ffn1_w_down), w_in=(w_in, m_w_in, v_w_in),
        w_out=(w_out, m_w_out, v_w_out),
        ffn2_w_gate=(ffn2_w_gate, m_ffn2_w_gate, v_ffn2_w_gate), ffn2_w_up=(ffn2_w_up, m_ffn2_w_up, v_ffn2_w_up),
        ffn2_w_down=(ffn2_w_down, m_ffn2_w_down, v_ffn2_w_down))
    out = {}
    for pname, (w, m, v) in state.items():
        res = adamw(w[0], m[0], v[0], big[pname], "adamw_" + pname)
        out[pname] = [r[None] for r in res]

    zero_taps = jnp.zeros((CONV_WIDTH, ch), F32)
    pack = lambda g1, gm, g3, cb, lg, lb, gq, gk: _pack_small(ch, g1, gm, g3, cb, lg, lb, gq, gk, zero_taps)
    small_parts = all_gather(_pack_small(ch, **small), "ag_small_grads")
    s_res = adamw(
        pack(ffn1_norm_g, mix_norm_g, ffn2_norm_g, conv_b_dw, conv_ln_g, conv_ln_b, q_norm_g, k_norm_g),
        pack(m_ffn1_norm_g, m_mix_norm_g, m_ffn2_norm_g, m_conv_b_dw, m_conv_ln_g, m_conv_ln_b, m_q_norm_g, m_k_norm_g),
        pack(v_ffn1_norm_g, v_mix_norm_g, v_ffn2_norm_g, v_conv_b_dw, v_conv_ln_g, v_conv_ln_b, v_q_norm_g, v_k_norm_g),
        small_parts, "adamw_small")
    s_out = [_unpack_small(r, d) for r in s_res]
    names = dict(g1="ffn1_norm_g", gmix="mix_norm_g", g3="ffn2_norm_g", conv_b="conv_b_dw", ln_g="conv_ln_g",
                 ln_b="conv_ln_b", gq="q_norm_g", gk="k_norm_g")
    for key, full in names.items():
        out[full] = [r[key] for r in s_out]

    cshard = ch // N_DEV
    taps_sum = s_res[0][11:11 + CONV_WIDTH]
    taps_mine = lax.dynamic_slice(taps_sum, (0, me * cshard), (CONV_WIDTH, cshard))
    pad_taps = lambda a: jnp.pad(a, ((0, 32 - CONV_WIDTH), (0, 0)))
    c_res = adamw(pad_taps(conv_w_dw[0]), pad_taps(m_conv_w_dw[0]), pad_taps(v_conv_w_dw[0]),
                  pad_taps(taps_mine)[None], "adamw_convw")
    out["conv_w_dw"] = [r[:CONV_WIDTH][None] for r in c_res]

    order = ["ffn1_norm_g", "ffn1_w_gate", "ffn1_w_up", "ffn1_w_down", "mix_norm_g", "w_in", "conv_w_dw",
             "conv_b_dw", "conv_ln_g", "conv_ln_b", "q_norm_g", "k_norm_g", "w_out", "ffn2_norm_g",
             "ffn2_w_gate", "ffn2_w_up", "ffn2_w_down"]
    result = [loss, grad_x[None]]
    for kind in range(4):
        result += [out[n][kind] for n in order]
    return tuple(result)
```

```python
import math
from typing import NamedTuple

import jax
import jax.numpy as jnp
from jax import lax
from jax.experimental import pallas as pl
from jax.experimental.pallas import tpu as pltpu

F32 = jnp.float32
BF16 = jnp.bfloat16

N_DEV = 8
EPS = 1e-6
HEAD_DIM = 64
LANES = 128
MXU_WIDTH = 256
CONV_WIDTH = 31
HALO = 32
ROW_CHUNK = 32
ATT_BLOCK = 128
DILATIONS = (1, 4, 16)
ATT_UNITS = 16
ATT_CHUNK = ATT_UNITS * ATT_BLOCK
ALIBI_MAX_BIAS = 8.0
MASKED = -1e30
VMEM_LIMIT = 56 * 1024 * 1024

ADAM_LR = 0.001
ADAM_B1 = 0.9
ADAM_B2 = 0.999
ADAM_EPS = 1e-08
ADAM_WD = 0.01
ADAM_STEP = 10

MESH_AXES = ("x", "y", "c")
ANY = pl.BlockSpec(memory_space=pl.ANY)


def _sds(shape, dtype):
    return jax.ShapeDtypeStruct(tuple(shape), dtype)


def _params(*sem):
    return pltpu.CompilerParams(dimension_semantics=sem, vmem_limit_bytes=VMEM_LIMIT)


def _sigmoid(v):
    return 1.0 / (1.0 + jnp.exp(-v))


def _row_tile(t, want):
    for cand in range(min(want, t) // 8 * 8, 0, -8):
        if t % cand == 0:
            return cand
    return t


def _mesh_pos():
    return lax.axis_index("x"), lax.axis_index("y"), lax.axis_index("c")


def _comm_sems():
    return [pltpu.SemaphoreType.DMA((7,)), pltpu.SemaphoreType.DMA((7,)), pltpu.SemaphoreType.DMA(())]


def _gather_phases(x_ref, out_ref, send_sems, recv_sems, local_sem):
    x, y, c = _mesh_pos()
    me, sibling = (x, y, c), (x, y, 1 - c)
    chips = [(1 - x, y), (x, 1 - y), (1 - x, 1 - y)]

    def slot(px, py, pc):
        return out_ref.at[4 * px + 2 * py + pc]

    def copy(k, block, to, src=None):
        return pltpu.make_async_remote_copy(
            src_ref=slot(*block) if src is None else src, dst_ref=slot(*block),
            send_sem=send_sems.at[k], recv_sem=recv_sems.at[k],
            device_id=to, device_id_type=pl.DeviceIdType.MESH)

    mine = pltpu.make_async_copy(x_ref, slot(*me), local_sem)
    first = [copy(0, me, sibling, src=x_ref)]
    first += [copy(1 + j, me, (*chip, c), src=x_ref) for j, chip in enumerate(chips)]
    passed = [copy(4 + j, (*chip, c), sibling) for j, chip in enumerate(chips)]

    def start():
        mine.start()
        for cp in first:
            cp.start()

    def forward():
        for j, chip in enumerate(chips):
            copy(1 + j, (*chip, c), me).wait_recv()
            passed[j].start()

    def finish():
        copy(0, sibling, me).wait_recv()
        for j, chip in enumerate(chips):
            copy(4 + j, (*chip, 1 - c), me).wait_recv()
        for cp in first + passed:
            cp.wait_send()
        mine.wait()

    return start, forward, finish


def _scatter_phases(p_ref, out_ref, send_sems, recv_sems, local_sem):
    x, y, c = _mesh_pos()
    me = 4 * x + 2 * y + c
    flips = [(fx, fy, fc) for fx in (0, 1) for fy in (0, 1) for fc in (0, 1)][1:]

    def copy(k, flip, receiving):
        px, py, pc = (1 - x if flip[0] else x, 1 - y if flip[1] else y, 1 - c if flip[2] else c)
        them = 4 * px + 2 * py + pc
        return pltpu.make_async_remote_copy(
            src_ref=p_ref.at[them], dst_ref=out_ref.at[them if receiving else me],
            send_sem=send_sems.at[k], recv_sem=recv_sems.at[k],
            device_id=(px, py, pc), device_id_type=pl.DeviceIdType.MESH)

    mine = pltpu.make_async_copy(p_ref.at[me], out_ref.at[me], local_sem)

    def start():
        mine.start()
        for k, flip in enumerate(flips):
            copy(k, flip, False).start()

    def finish():
        for k, flip in enumerate(flips):
            copy(k, flip, True).wait_recv()
            copy(k, flip, False).wait_send()
        mine.wait()

    return start, None, finish


class Rider(NamedTuple):
    kind: str
    src: jax.Array

    def out_shape(self):
        shape = (N_DEV,) + self.src.shape if self.kind == "gather" else self.src.shape
        return _sds(shape, self.src.dtype)


def _rider_hooks(riders, in_refs, out_refs, sem_refs, step, n_steps):
    phases = [(_gather_phases if r.kind == "gather" else _scatter_phases)(
                  in_refs[i], out_refs[i], *sem_refs[3 * i:3 * i + 3]) for i, r in enumerate(riders)]

    def begin():
        for start, forward, _ in phases:
            pl.when(step == 0)(start)
            if forward is not None:
                pl.when(step == (7 * n_steps) // 8)(forward)

    def end():
        for _, _, finish in phases:
            pl.when(step == n_steps - 1)(finish)

    return begin, end


def _split_refs(refs, n_in, n_out, n_scratch, n_riders):
    pos, parts = 0, []
    for n in (n_in, n_riders, n_out, n_riders, n_scratch, 3 * n_riders):
        parts.append(refs[pos:pos + n])
        pos += n
    return parts


def all_gather(shard, name):
    def body(x_ref, out_ref, send_sems, recv_sems, local_sem):
        start, forward, finish = _gather_phases(x_ref, out_ref, send_sems, recv_sems, local_sem)
        start()
        forward()
        finish()

    return pl.pallas_call(
        body, name=name, out_shape=_sds((N_DEV,) + shard.shape, shard.dtype),
        in_specs=[ANY], out_specs=ANY, scratch_shapes=_comm_sems(),
    )(shard)


def adamw(w, m, v, parts, name):
    n_parts, rows, cols = parts.shape
    tr = _row_tile(rows, 128)
    c1 = 1.0 - ADAM_B1 ** ADAM_STEP
    c2 = 1.0 - ADAM_B2 ** ADAM_STEP

    def body(w_ref, m_ref, v_ref, p_ref, g_ref, d_ref, nm_ref, nv_ref):
        g = p_ref[0].astype(F32)
        for s in range(1, n_parts):
            g = g + p_ref[s].astype(F32)
        nm = ADAM_B1 * m_ref[...] + (1.0 - ADAM_B1) * g
        nv = ADAM_B2 * v_ref[...] + (1.0 - ADAM_B2) * (g * g)
        delta = -ADAM_LR * ((nm / c1) / (jnp.sqrt(nv / c2) + ADAM_EPS) + ADAM_WD * w_ref[...])
        g_ref[...] = g
        d_ref[...] = delta
        nm_ref[...] = nm
        nv_ref[...] = nv

    mat = pl.BlockSpec((tr, cols), lambda i: (i, 0))
    return pl.pallas_call(
        body, name=name, grid=(rows // tr,),
        in_specs=[mat, mat, mat, pl.BlockSpec((n_parts, tr, cols), lambda i: (0, i, 0))],
        out_specs=[mat, mat, mat, mat],
        out_shape=[_sds((rows, cols), F32)] * 4,
        compiler_params=_params("parallel"),
    )(w, m, v, parts)


_NN = (((1,), (0,)), ((), ()))
_NT = (((1,), (1,)), ((), ()))
_TN = (((0,), (0,)), ((), ()))


def mm_cols(name, a, b_list, b_specs, nt, extras, extra_specs, out_shapes, out_specs, epilogue, n_blk, riders=()):
    t_len, k_len = a.shape
    tm = _row_tile(t_len, 512)
    nb, ne, n_out, nr = len(b_list), len(extras), len(out_shapes), len(riders)
    t_steps = t_len // tm
    n_cols = b_specs[0].block_shape[-2 if nt else -1]

    def body(*refs):
        ins, r_in, outs, r_out, _, r_sem = _split_refs(refs, 1 + nb + ne, n_out, 0, nr)
        step = pl.program_id(0) * t_steps + pl.program_id(1)
        begin, end = _rider_hooks(riders, r_in, r_out, r_sem, step, n_blk * t_steps)
        begin()
        av = ins[0][...]
        for c0 in range(0, n_cols, MXU_WIDTH):
            cols = slice(c0, min(c0 + MXU_WIDTH, n_cols))
            accs = [lax.dot_general(av, br[cols, :] if nt else br[:, cols], _NT if nt else _NN,
                                    preferred_element_type=F32) for br in ins[1:1 + nb]]
            epilogue(accs, ins[1 + nb:], outs, cols)
        end()

    res = pl.pallas_call(
        body, name=name, grid=(n_blk, t_steps),
        in_specs=([pl.BlockSpec((tm, k_len), lambda j, t: (t, 0))] + list(b_specs) + list(extra_specs(tm))
                  + [ANY] * nr),
        out_specs=list(out_specs(tm)) + [ANY] * nr,
        out_shape=list(out_shapes) + [r.out_shape() for r in riders],
        scratch_shapes=_comm_sems() * nr,
        compiler_params=_params("arbitrary", "arbitrary"),
    )(a, *b_list, *extras, *[r.src for r in riders])
    return res[:n_out], res[n_out:]


def mm_reduce(name, a_list, a_specs, b_list, b_specs, nt, res, scale, t_len, n_len, n_blk, riders=(), tm_want=512):
    tm = _row_tile(t_len, tm_want)
    na, nr = len(a_list), len(riders)
    has_res = res is not None
    t_steps = t_len // tm

    def body(*refs):
        ins, r_in, outs, r_out, _, r_sem = _split_refs(refs, 2 * na + has_res, 1, 0, nr)
        o_ref = outs[0]
        j = pl.program_id(1)
        step = pl.program_id(0) * n_blk + j
        begin, end = _rider_hooks(riders, r_in, r_out, r_sem, step, t_steps * n_blk)
        begin()

        part = None
        for ar, br in zip(ins[:na], ins[na:2 * na]):
            d = lax.dot_general(ar[...], br[...], _NT if nt else _NN, preferred_element_type=F32)
            part = d if part is None else part + d

        @pl.when(j == 0)
        def _():
            o_ref[...] = part

        @pl.when(j > 0)
        def _():
            o_ref[...] += part

        if has_res or scale != 1.0:
            @pl.when(j == n_blk - 1)
            def _():
                val = o_ref[...] * scale if scale != 1.0 else o_ref[...]
                o_ref[...] = ins[2 * na][...] + val if has_res else val

        end()

    row = pl.BlockSpec((tm, n_len), lambda t, j: (t, 0))
    out = pl.pallas_call(
        body, name=name, grid=(t_steps, n_blk),
        in_specs=list(a_specs(tm)) + list(b_specs) + ([row] if has_res else []) + [ANY] * nr,
        out_specs=[row] + [ANY] * nr,
        out_shape=[_sds((t_len, n_len), F32)] + [r.out_shape() for r in riders],
        scratch_shapes=_comm_sems() * nr,
        compiler_params=_params("arbitrary", "arbitrary"),
    )(*a_list, *b_list, *([res] if has_res else []), *[r.src for r in riders])
    return out[0], out[1:]


def mm_tn(name, x, x_spec, dy_list, dy_specs, out_shapes, out_specs, scale, t_len, n_blk, riders=(),
          x_transposed=False):
    tt = _row_tile(t_len, 2048)
    nd, nr = len(dy_list), len(riders)
    t_steps = t_len // tt
    acc_shapes = [pltpu.VMEM(spec.block_shape[-2:], F32) for spec in out_specs]

    def body(*refs):
        ins, r_in, outs, r_out, accs, r_sem = _split_refs(refs, 1 + nd, nd, nd, nr)
        t = pl.program_id(1)
        step = pl.program_id(0) * t_steps + t
        begin, end = _rider_hooks(riders, r_in, r_out, r_sem, step, n_blk * t_steps)
        begin()
        xv = ins[0][...]
        for dr, acc in zip(ins[1:], accs):
            d = lax.dot_general(xv, dr[...], _NN if x_transposed else _TN, preferred_element_type=F32)

            @pl.when(t == 0)
            def _():
                acc[...] = d

            @pl.when(t > 0)
            def _():
                acc[...] += d

        @pl.when(t == t_steps - 1)
        def _():
            for acc, orf in zip(accs, outs):
                val = acc[...] * scale if scale != 1.0 else acc[...]
                orf[...] = val.astype(orf.dtype)

        end()

    res = pl.pallas_call(
        body, name=name, grid=(n_blk, t_steps),
        in_specs=[x_spec(tt)] + list(dy_specs(tt)) + [ANY] * nr,
        out_specs=list(out_specs) + [ANY] * nr,
        out_shape=list(out_shapes) + [r.out_shape() for r in riders],
        scratch_shapes=acc_shapes + _comm_sems() * nr,
        compiler_params=_params("arbitrary", "arbitrary"),
    )(x, *dy_list, *[r.src for r in riders])
    return res[:nd], res[nd:]


def rms_fwd(x, g, name):
    t_len, d = x.shape
    tm = _row_tile(t_len, 512)

    def body(x_ref, g_ref, h_ref, ht_ref):
        xv = x_ref[...]
        r = lax.rsqrt(jnp.mean(xv * xv, axis=-1, keepdims=True) + EPS)
        hv = xv * r * g_ref[...]
        h_ref[...] = hv.astype(BF16)
        ht_ref[...] = hv.T.astype(BF16)

    row = pl.BlockSpec((tm, d), lambda i: (i, 0))
    return pl.pallas_call(
        body, name=name, grid=(t_len // tm,),
        in_specs=[row, pl.BlockSpec((1, d), lambda i: (0, 0))],
        out_specs=[row, pl.BlockSpec((d, tm), lambda i: (0, i))],
        out_shape=[_sds((t_len, d), BF16), _sds((d, t_len), BF16)],
        compiler_params=_params("parallel"),
    )(x, g)


def rms_bwd(x, g, dh, dres, name):
    t_len, d = x.shape
    tm = _row_tile(t_len, 512)

    def body(x_ref, g_ref, dh_ref, dr_ref, dx_ref, dxb_ref, dg_ref):
        i = pl.program_id(0)
        xv = x_ref[...]
        r = lax.rsqrt(jnp.mean(xv * xv, axis=-1, keepdims=True) + EPS)
        xh = xv * r
        dhv = dh_ref[...]

        @pl.when(i == 0)
        def _():
            dg_ref[...] = jnp.zeros_like(dg_ref)

        dg_ref[...] += jnp.sum(dhv * xh, axis=0, keepdims=True)
        dxh = dhv * g_ref[...]
        dx = dr_ref[...] + r * (dxh - xh * jnp.mean(dxh * xh, axis=-1, keepdims=True))
        dx_ref[...] = dx
        dxb_ref[...] = dx.astype(BF16)

    row = pl.BlockSpec((tm, d), lambda i: (i, 0))
    vec = pl.BlockSpec((1, d), lambda i: (0, 0))
    return pl.pallas_call(
        body, name=name, grid=(t_len // tm,),
        in_specs=[row, vec, row, row],
        out_specs=[row, row, vec],
        out_shape=[_sds((t_len, d), F32), _sds((t_len, d), BF16), _sds((1, d), F32)],
        compiler_params=_params("arbitrary"),
    )(x, g, dh, dres)


def loss_head(y, target, name):
    t_len, d = y.shape
    tm = _row_tile(t_len, 512)

    def body(y_ref, t_ref, l_ref, dy_ref, dyb_ref):
        i = pl.program_id(0)
        err = y_ref[...] - t_ref[...]

        @pl.when(i == 0)
        def _():
            l_ref[...] = jnp.zeros_like(l_ref)

        rows = jnp.sum(err * err, axis=-1, keepdims=True) * (1.0 / d)
        l_ref[...] += 0.5 * jnp.sum(rows, axis=0, keepdims=True)
        dy = err * (1.0 / d)
        dy_ref[...] = dy
        dyb_ref[...] = dy.astype(BF16)

    row = pl.BlockSpec((tm, d), lambda i: (i, 0))
    return pl.pallas_call(
        body, name=name, grid=(t_len // tm,),
        in_specs=[row, row],
        out_specs=[pl.BlockSpec((8, LANES), lambda i: (0, 0)), row, row],
        out_shape=[_sds((8, LANES), F32), _sds((t_len, d), F32), _sds((t_len, d), BF16)],
        compiler_params=_params("arbitrary"),
    )(y, target)


def _conv_specs(tm, ch):
    per = tm // HALO
    cur = lambda cb: pl.BlockSpec((tm, ch), lambda i: (i, cb))
    prev = lambda cb: pl.BlockSpec((HALO, ch), lambda i: (jnp.maximum(i * per - 1, 0), cb))
    return [cur(0), cur(1), prev(0), prev(1)]


def _fill_glu(ext, a_ref, gt_ref, ap_ref, gp_ref, i, tm):
    vp = ap_ref[...] * _sigmoid(gp_ref[...])
    ext[0:HALO, :] = jnp.where(i > 0, vp, 0.0)
    ext[HALO:HALO + tm, :] = a_ref[...] * _sigmoid(gt_ref[...])


def _conv_rows(ext, w_ref, b_ref, r0):
    acc = jnp.broadcast_to(b_ref[...], (ROW_CHUNK, b_ref.shape[1]))
    for k in range(CONV_WIDTH):
        acc = acc + w_ref[k:k + 1, :] * ext[pl.ds(r0 + HALO - (CONV_WIDTH - 1) + k, ROW_CHUNK), :]
    return acc


def _layer_norm(yv):
    mu = jnp.mean(yv, axis=-1, keepdims=True)
    cen = yv - mu
    var = jnp.mean(cen * cen, axis=-1, keepdims=True)
    rstd = lax.rsqrt(var + EPS)
    return cen * rstd, rstd


def conv_fwd(z, w, b, lg, lb, name):
    t_len = z.shape[0]
    ch = w.shape[1]
    tm = _row_tile(t_len, 256)

    def body(a_ref, gt_ref, ap_ref, gp_ref, w_ref, b_ref, lg_ref, lb_ref, y_ref, pre_ref, ext):
        i = pl.program_id(0)
        _fill_glu(ext, a_ref, gt_ref, ap_ref, gp_ref, i, tm)
        for r0 in range(0, tm, ROW_CHUNK):
            pre = _conv_rows(ext, w_ref, b_ref, r0)
            pre_ref[r0:r0 + ROW_CHUNK, :] = pre
            xh, _ = _layer_norm(pre)
            u = xh * lg_ref[...] + lb_ref[...]
            y_ref[r0:r0 + ROW_CHUNK, :] = (u * _sigmoid(u)).astype(BF16)

    vec = pl.BlockSpec((1, ch), lambda i: (0, 0))
    row = pl.BlockSpec((tm, ch), lambda i: (i, 0))
    return pl.pallas_call(
        body, name=name, grid=(t_len // tm,),
        in_specs=_conv_specs(tm, ch) + [pl.BlockSpec((32, ch), lambda i: (0, 0)), vec, vec, vec],
        out_specs=[row, row],
        out_shape=[_sds((t_len, ch), BF16), _sds((t_len, ch), F32)],
        scratch_shapes=[pltpu.VMEM((HALO + tm, ch), F32)],
        compiler_params=_params("parallel"),
    )(z, z, z, z, w, b, lg, lb)


def conv_bwd_norm(pre, dy_cat, lg, lb, name):
    t_len, ch = pre.shape
    tm = _row_tile(t_len, 256)

    def body(pre_ref, dy_ref, lg_ref, lb_ref, dc_ref, dlg_ref, dlb_ref, db_ref):
        i = pl.program_id(0)

        @pl.when(i == 0)
        def _():
            dlg_ref[...] = jnp.zeros_like(dlg_ref)
            dlb_ref[...] = jnp.zeros_like(dlb_ref)
            db_ref[...] = jnp.zeros_like(db_ref)

        for r0 in range(0, tm, ROW_CHUNK):
            xh, rstd = _layer_norm(pre_ref[r0:r0 + ROW_CHUNK, :])
            u = xh * lg_ref[...] + lb_ref[...]
            sg = _sigmoid(u)
            du = dy_ref[r0:r0 + ROW_CHUNK, :] * (sg * (1.0 + u * (1.0 - sg)))
            dlg_ref[...] += jnp.sum(du * xh, axis=0, keepdims=True)
            dlb_ref[...] += jnp.sum(du, axis=0, keepdims=True)
            dxh = du * lg_ref[...]
            dc = rstd * (dxh - jnp.mean(dxh, axis=-1, keepdims=True)
                         - xh * jnp.mean(dxh * xh, axis=-1, keepdims=True))
            db_ref[...] += jnp.sum(dc, axis=0, keepdims=True)
            dc_ref[r0:r0 + ROW_CHUNK, :] = dc

    vec = pl.BlockSpec((1, ch), lambda i: (0, 0))
    row = pl.BlockSpec((tm, ch), lambda i: (i, 0))
    return pl.pallas_call(
        body, name=name, grid=(t_len // tm,),
        in_specs=[row, row, vec, vec],
        out_specs=[row, vec, vec, vec],
        out_shape=[_sds((t_len, ch), F32)] + [_sds((1, ch), F32)] * 3,
        compiler_params=_params("arbitrary"),
    )(pre, dy_cat, lg, lb)


def conv_bwd_taps(z, dc, w, name):
    t_len = z.shape[0]
    ch = w.shape[1]
    tm = _row_tile(t_len, 256)
    per = tm // HALO
    n_tiles = t_len // tm
    last_halo = t_len // HALO - 1

    def body(a_ref, gt_ref, ap_ref, gp_ref, dc_ref, dn_ref, w_ref, dz_a_ref, dz_g_ref, dw_ref, ext, dext):
        i = pl.program_id(0)
        _fill_glu(ext, a_ref, gt_ref, ap_ref, gp_ref, i, tm)
        dext[0:tm, :] = dc_ref[...]
        dext[tm:tm + HALO, :] = jnp.where(i < n_tiles - 1, dn_ref[...], 0.0)

        @pl.when(i == 0)
        def _():
            dw_ref[...] = jnp.zeros_like(dw_ref)

        for r0 in range(0, tm, ROW_CHUNK):
            dcv = dext[r0:r0 + ROW_CHUNK, :]
            dv = jnp.zeros((ROW_CHUNK, ch), F32)
            for k in range(CONV_WIDTH):
                dv = dv + w_ref[k:k + 1, :] * dext[pl.ds(r0 + (CONV_WIDTH - 1) - k, ROW_CHUNK), :]
                prod = dcv * ext[pl.ds(r0 + HALO - (CONV_WIDTH - 1) + k, ROW_CHUNK), :]
                fold = prod[0:8]
                for s in range(8, ROW_CHUNK, 8):
                    fold = fold + prod[s:s + 8]
                dw_ref[k] += fold
            av = a_ref[r0:r0 + ROW_CHUNK, :]
            sg = _sigmoid(gt_ref[r0:r0 + ROW_CHUNK, :])
            dz_a_ref[r0:r0 + ROW_CHUNK, :] = (dv * sg).astype(BF16)
            dz_g_ref[r0:r0 + ROW_CHUNK, :] = (dv * av * sg * (1.0 - sg)).astype(BF16)

    row = pl.BlockSpec((tm, ch), lambda i: (i, 0))
    nxt = pl.BlockSpec((HALO, ch), lambda i: (jnp.minimum((i + 1) * per, last_halo), 0))
    return pl.pallas_call(
        body, name=name, grid=(n_tiles,),
        in_specs=_conv_specs(tm, ch) + [row, nxt, pl.BlockSpec((32, ch), lambda i: (0, 0))],
        out_specs=[row, row, pl.BlockSpec((32, 8, ch), lambda i: (0, 0, 0))],
        out_shape=[_sds((t_len, ch), BF16), _sds((t_len, ch), BF16), _sds((32, 8, ch), F32)],
        scratch_shapes=[pltpu.VMEM((HALO + tm, ch), F32), pltpu.VMEM((tm + HALO, ch), F32)],
        compiler_params=_params("arbitrary"),
    )(z, z, z, z, dc, dc, w)


def _head_masks(rows):
    lane = lax.broadcasted_iota(jnp.int32, (rows, LANES), 1)
    low = lane < HEAD_DIM
    return low, jnp.logical_not(low)


def _per_head_mean(val, low):
    s_low = jnp.sum(jnp.where(low, val, 0.0), axis=-1, keepdims=True)
    s_high = jnp.sum(jnp.where(low, 0.0, val), axis=-1, keepdims=True)
    return jnp.where(low, s_low, s_high) * (1.0 / HEAD_DIM)


def qk_norm_fwd(z, g2, ch, name):
    t_len = z.shape[0]
    tm = _row_tile(t_len, 1024)
    n_col = 2 * ch // LANES
    z_off = 2 * ch // LANES

    def body(z_ref, g_ref, o_ref):
        low, _ = _head_masks(tm)
        xv = z_ref[...]
        r = lax.rsqrt(_per_head_mean(xv * xv, low) + EPS)
        o_ref[...] = xv * r * g_ref[...]

    return pl.pallas_call(
        body, name=name, grid=(t_len // tm, n_col),
        in_specs=[pl.BlockSpec((tm, LANES), lambda i, cb: (i, z_off + cb)),
                  pl.BlockSpec((1, LANES), lambda i, cb: (0, cb))],
        out_specs=pl.BlockSpec((tm, LANES), lambda i, cb: (i, cb)),
        out_shape=_sds((t_len, 2 * ch), F32),
        compiler_params=_params("parallel", "parallel"),
    )(z, g2)


def qk_norm_bwd(z, g, d_list, z_off, ch, name):
    t_len = z.shape[0]
    tm = _row_tile(t_len, 1024)
    n_col = ch // LANES
    nd = len(d_list)

    def body(*refs):
        z_ref, g_ref, d_refs = refs[0], refs[1], refs[2:2 + nd]
        dz_ref, dg_ref = refs[2 + nd], refs[3 + nd]
        first = jnp.logical_and(pl.program_id(0) == 0, pl.program_id(1) == 0)
        low, _ = _head_masks(tm)
        xv = z_ref[...]
        r = lax.rsqrt(_per_head_mean(xv * xv, low) + EPS)
        xh = xv * r
        dy = d_refs[0][...]
        for dr in d_refs[1:]:
            dy = dy + dr[...]

        @pl.when(first)
        def _():
            dg_ref[...] = jnp.zeros_like(dg_ref)

        dg_ref[...] += jnp.sum(dy * xh, axis=0, keepdims=True)
        dxh = dy * g_ref[...]
        dz_ref[...] = (r * (dxh - xh * _per_head_mean(dxh * xh, low))).astype(BF16)

    blk = pl.BlockSpec((tm, LANES), lambda i, cb: (i, cb))
    return pl.pallas_call(
        body, name=name, grid=(t_len // tm, n_col),
        in_specs=[pl.BlockSpec((tm, LANES), lambda i, cb: (i, z_off + cb)),
                  pl.BlockSpec((1, LANES), lambda i, cb: (0, 0))] + [blk] * nd,
        out_specs=[blk, pl.BlockSpec((1, LANES), lambda i, cb: (0, 0))],
        out_shape=[_sds((t_len, ch), BF16), _sds((1, LANES), F32)],
        compiler_params=_params("arbitrary", "arbitrary"),
    )(z, g, *d_list)


def _alibi_bias(n_heads, dilation):
    slopes = 2.0 ** (-ALIBI_MAX_BIAS * jnp.arange(1, n_heads + 1, dtype=F32) / n_heads)
    qi = jnp.arange(ATT_BLOCK)[:, None]
    kj = jnp.arange(ATT_BLOCK)[None, :]
    dist_cur = (qi - kj).astype(F32)
    dist_prev = (ATT_BLOCK + qi - kj).astype(F32)
    cur = jnp.where((qi >= kj)[None], -slopes[:, None, None] * (dilation * dist_cur)[None], MASKED)
    prev = jnp.where((kj >= qi)[None], -slopes[:, None, None] * (dilation * dist_prev)[None], MASKED)
    return jnp.concatenate([prev, cur], axis=-1).astype(F32)


def _stack_heads(val, low, high):
    return jnp.concatenate([jnp.where(low, val, 0.0), jnp.where(high, val, 0.0)], axis=0).astype(BF16)


def _head_rows(val, low, high):
    return jnp.concatenate([jnp.max(jnp.where(low, val, MASKED), axis=-1, keepdims=True),
                            jnp.max(jnp.where(high, val, MASKED), axis=-1, keepdims=True)], axis=0)


def _unit_scores(q2, k2, b_ref, has_prev, scale):
    s = lax.dot_general(q2, k2, _NT, preferred_element_type=F32) * scale
    s = s + b_ref[...].reshape(2 * ATT_BLOCK, 2 * ATT_BLOCK)
    col = lax.broadcasted_iota(jnp.int32, s.shape, 1)
    return jnp.where(jnp.logical_or(has_prev, col >= ATT_BLOCK), s, MASKED)


def _strided_rows(r, dilation):
    per = ATT_CHUNK // dilation
    return pl.ds(r, per, stride=dilation) if dilation > 1 else pl.ds(0, per)


def _deinterleave(dst, src_ref, dilation, base=None, dtype=None):
    per = ATT_CHUNK // dilation
    for r in range(dilation):
        val = src_ref[_strided_rows(r, dilation), :]
        val = val if dtype is None else val.astype(dtype)
        if base is None:
            dst[r * per:(r + 1) * per, :] = val
        else:
            dst[pl.ds(pl.multiple_of(base + r * per, ATT_BLOCK), per), :] = val


def _unit_rows(u, c, nb, base, pbase):
    in_chunk = lax.rem(u, jnp.int32(nb)) > 0
    has_prev = jnp.logical_or(in_chunk, c > 0)
    urow = pl.multiple_of(u * ATT_BLOCK, ATT_BLOCK)
    crow = pl.multiple_of(base + u * ATT_BLOCK, ATT_BLOCK)
    prow = pl.multiple_of(jnp.where(in_chunk, base + (u - 1) * ATT_BLOCK,
                                    pbase + (u + nb - 1) * ATT_BLOCK), ATT_BLOCK)
    return in_chunk, has_prev, urow, crow, prow


def _interleave(dst_ref, src, dilation, base=None):
    per = ATT_CHUNK // dilation
    for r in range(dilation):
        if base is None:
            val = src[r * per:(r + 1) * per, :]
        else:
            val = src[pl.ds(pl.multiple_of(base + r * per, ATT_BLOCK), per), :]
        dst_ref[_strided_rows(r, dilation), :] = val


def attn_fwd(qk, z, dilation, ch, name):
    t_len = qk.shape[0]
    pairs = ch // LANES
    nc = t_len // ATT_CHUNK
    nb = ATT_UNITS // dilation
    scale = 1.0 / math.sqrt(HEAD_DIM)
    bias = _alibi_bias(2 * pairs, dilation)

    def body(q_ref, k_ref, v_ref, b_ref, o_ref, l_ref, qd, kx, vx, od, ld):
        c = pl.program_id(1)
        slot = lax.rem(c, jnp.int32(2))
        base, pbase = slot * ATT_CHUNK, (1 - slot) * ATT_CHUNK

        @pl.when(c == 0)
        def _():
            kx[...] = jnp.zeros_like(kx)
            vx[...] = jnp.zeros_like(vx)

        _deinterleave(qd, q_ref, dilation)
        _deinterleave(kx, k_ref, dilation, base, BF16)
        _deinterleave(vx, v_ref, dilation, base, BF16)
        low, high = _head_masks(ATT_BLOCK)

        def unit(u, carry):
            _, has_prev, urow, crow, prow = _unit_rows(u, c, nb, base, pbase)
            q2 = _stack_heads(qd[pl.ds(urow, ATT_BLOCK), :], low, high)
            k2 = jnp.concatenate([kx[pl.ds(prow, ATT_BLOCK), :], kx[pl.ds(crow, ATT_BLOCK), :]], axis=0)
            v2 = jnp.concatenate([vx[pl.ds(prow, ATT_BLOCK), :], vx[pl.ds(crow, ATT_BLOCK), :]], axis=0)
            s = _unit_scores(q2, k2, b_ref, has_prev, scale)
            mx = jnp.max(s, axis=-1, keepdims=True)
            e = jnp.exp(s - mx)
            den = jnp.sum(e, axis=-1, keepdims=True)
            acc = lax.dot_general(e.astype(BF16), v2, _NN, preferred_element_type=F32) / den
            lse = jnp.broadcast_to(mx + jnp.log(den), acc.shape)
            od[pl.ds(urow, ATT_BLOCK), :] = jnp.where(low, acc[:ATT_BLOCK], acc[ATT_BLOCK:])
            ld[pl.ds(urow, ATT_BLOCK), :] = jnp.where(low, lse[:ATT_BLOCK], lse[ATT_BLOCK:])
            return carry

        lax.fori_loop(0, ATT_UNITS, unit, 0, unroll=2)
        _interleave(o_ref, od, dilation)
        _interleave(l_ref, ld, dilation)

    blk = (ATT_CHUNK, LANES)
    bias_spec = pl.BlockSpec((2, ATT_BLOCK, 2 * ATT_BLOCK), lambda p, c: (p, 0, 0))
    out_spec = pl.BlockSpec(blk, lambda p, c: (c, p))
    return pl.pallas_call(
        body, name=name, grid=(pairs, nc),
        in_specs=[pl.BlockSpec(blk, lambda p, c: (c, p)),
                  pl.BlockSpec(blk, lambda p, c: (c, pairs + p)),
                  pl.BlockSpec(blk, lambda p, c: (c, 4 * pairs + p)),
                  bias_spec],
        out_specs=[out_spec, out_spec],
        out_shape=[_sds((t_len, ch), F32)] * 2,
        scratch_shapes=[pltpu.VMEM((ATT_CHUNK, LANES), F32),
                        pltpu.VMEM((2 * ATT_CHUNK, LANES), BF16), pltpu.VMEM((2 * ATT_CHUNK, LANES), BF16),
                        pltpu.VMEM((ATT_CHUNK, LANES), F32), pltpu.VMEM((ATT_CHUNK, LANES), F32)],
        compiler_params=_params("arbitrary", "arbitrary"),
    )(qk, qk, z, bias)


def attn_combine(outs, lses, name):
    t_len, ch = outs[0].shape
    tm = _row_tile(t_len, 512)

    def body(o1, o2, o3, l1, l2, l3, out_ref, outb_ref, lg_ref):
        a, b, c = l1[...], l2[...], l3[...]
        mx = jnp.maximum(jnp.maximum(a, b), c)
        tot = mx + jnp.log(jnp.exp(a - mx) + jnp.exp(b - mx) + jnp.exp(c - mx))
        val = jnp.exp(a - tot) * o1[...] + jnp.exp(b - tot) * o2[...] + jnp.exp(c - tot) * o3[...]
        out_ref[...] = val
        outb_ref[...] = val.astype(BF16)
        lg_ref[...] = tot

    row = pl.BlockSpec((tm, ch), lambda i: (i, 0))
    return pl.pallas_call(
        body, name=name, grid=(t_len // tm,),
        in_specs=[row] * 6, out_specs=[row] * 3,
        out_shape=[_sds((t_len, ch), F32), _sds((t_len, ch), BF16), _sds((t_len, ch), F32)],
        compiler_params=_params("parallel"),
    )(*outs, *lses)


def attn_bwd(qk, z, dy_cat, out, lg, dilation, ch, name):
    t_len = qk.shape[0]
    pairs = ch // LANES
    nc = t_len // ATT_CHUNK
    nb = ATT_UNITS // dilation
    scale = 1.0 / math.sqrt(HEAD_DIM)
    bias = _alibi_bias(2 * pairs, dilation)

    def body(q_ref, k_ref, v_ref, do_ref, out_ref, lg_ref, b_ref, dq_ref, dk_ref, dv_ref,
             qd, dod, lgd, dld, dl_nat, kx, vx, dkx, dvx, dqd):
        c = pl.program_id(1)
        slot = lax.rem(c, jnp.int32(2))
        base, pbase = slot * ATT_CHUNK, (1 - slot) * ATT_CHUNK

        @pl.when(c == 0)
        def _():
            for ref in (kx, vx, dkx, dvx):
                ref[...] = jnp.zeros_like(ref)

        @pl.when(c < nc)
        def _():
            low_all, _ = _head_masks(ATT_CHUNK)
            dl_nat[...] = _per_head_mean(do_ref[...] * out_ref[...], low_all) * float(HEAD_DIM)
            _deinterleave(qd, q_ref, dilation)
            _deinterleave(dod, do_ref, dilation)
            _deinterleave(lgd, lg_ref, dilation)
            _deinterleave(dld, dl_nat, dilation)
            _deinterleave(kx, k_ref, dilation, base, BF16)
            _deinterleave(vx, v_ref, dilation, base, BF16)
            cur = pl.ds(pl.multiple_of(base, ATT_CHUNK), ATT_CHUNK)
            dkx[cur, :] = jnp.zeros((ATT_CHUNK, LANES), F32)
            dvx[cur, :] = jnp.zeros((ATT_CHUNK, LANES), F32)
            low, high = _head_masks(ATT_BLOCK)

            def unit(u, carry):
                _, has_prev, urow, crow, prow = _unit_rows(u, c, nb, base, pbase)
                rows = pl.ds(urow, ATT_BLOCK)
                q2 = _stack_heads(qd[rows, :], low, high)
                do2 = _stack_heads(dod[rows, :], low, high)
                lse = _head_rows(lgd[rows, :], low, high)
                delta = _head_rows(dld[rows, :], low, high)
                k2 = jnp.concatenate([kx[pl.ds(prow, ATT_BLOCK), :], kx[pl.ds(crow, ATT_BLOCK), :]], axis=0)
                v2 = jnp.concatenate([vx[pl.ds(prow, ATT_BLOCK), :], vx[pl.ds(crow, ATT_BLOCK), :]], axis=0)
                prob = jnp.exp(_unit_scores(q2, k2, b_ref, has_prev, scale) - lse)
                dp = lax.dot_general(do2, v2, _NT, preferred_element_type=F32)
                ds = (prob * (dp - delta)).astype(BF16)
                dq2 = lax.dot_general(ds, k2, _NN, preferred_element_type=F32)
                dk2 = lax.dot_general(ds, q2, _TN, preferred_element_type=F32)
                dv2 = lax.dot_general(prob.astype(BF16), do2, _TN, preferred_element_type=F32)
                dqd[rows, :] = scale * jnp.where(low, dq2[:ATT_BLOCK], dq2[ATT_BLOCK:])
                dkx[pl.ds(prow, ATT_BLOCK), :] += scale * dk2[:ATT_BLOCK]
                dkx[pl.ds(crow, ATT_BLOCK), :] += scale * dk2[ATT_BLOCK:]
                dvx[pl.ds(prow, ATT_BLOCK), :] += dv2[:ATT_BLOCK]
                dvx[pl.ds(crow, ATT_BLOCK), :] += dv2[ATT_BLOCK:]
                return carry

            lax.fori_loop(0, ATT_UNITS, unit, 0, unroll=2)
            _interleave(dq_ref, dqd, dilation)

        @pl.when(c > 0)
        def _():
            _interleave(dk_ref, dkx, dilation, pbase)
            _interleave(dv_ref, dvx, dilation, pbase)

    blk = (ATT_CHUNK, LANES)
    here = lambda c: jnp.minimum(c, nc - 1)
    spec = lambda off: pl.BlockSpec(blk, lambda p, c: (here(c), off + p))
    late = pl.BlockSpec(blk, lambda p, c: (jnp.maximum(c - 1, 0), p))
    bias_spec = pl.BlockSpec((2, ATT_BLOCK, 2 * ATT_BLOCK), lambda p, c: (p, 0, 0))
    f32_chunk = pltpu.VMEM((ATT_CHUNK, LANES), F32)
    return pl.pallas_call(
        body, name=name, grid=(pairs, nc + 1),
        in_specs=[spec(0), spec(pairs), spec(4 * pairs), spec(pairs), spec(0), spec(0), bias_spec],
        out_specs=[spec(0), late, late],
        out_shape=[_sds((t_len, ch), F32)] * 3,
        scratch_shapes=[f32_chunk] * 5
                       + [pltpu.VMEM((2 * ATT_CHUNK, LANES), BF16)] * 2
                       + [pltpu.VMEM((2 * ATT_CHUNK, LANES), F32)] * 2 + [f32_chunk],
        compiler_params=_params("arbitrary", "arbitrary"),
    )(qk, qk, z, dy_cat, out, lg, bias)


def sum3_bf16(a, b, c, name):
    t_len, ch = a.shape
    tm = _row_tile(t_len, 512)

    def body(a_ref, b_ref, c_ref, o_ref):
        o_ref[...] = (a_ref[...] + b_ref[...] + c_ref[...]).astype(BF16)

    row = pl.BlockSpec((tm, ch), lambda i: (i, 0))
    return pl.pallas_call(
        body, name=name, grid=(t_len // tm,), in_specs=[row] * 3, out_specs=row,
        out_shape=_sds((t_len, ch), BF16), compiler_params=_params("parallel"),
    )(a, b, c)


def _blk3(rows, cols):
    return pl.BlockSpec((None, rows, cols), lambda j, t: (j, 0, 0))


def ffn_up(h, wgu, name, riders):
    t_len, d = h.shape
    n_blk, _, _, fj = wgu.shape

    def epilogue(accs, e_refs, o_refs, cols):
        gate, up = accs
        o_refs[0][:, cols] = gate.astype(BF16)
        o_refs[1][:, cols] = up.astype(BF16)
        o_refs[2][:, cols] = (gate * _sigmoid(gate) * up).astype(BF16)

    w_spec = lambda i: pl.BlockSpec((None, None, d, fj), lambda j, t: (j, i, 0, 0))
    act = lambda tm: pl.BlockSpec((None, tm, fj), lambda j, t: (j, t, 0))
    return mm_cols(name, h, [wgu, wgu], [w_spec(0), w_spec(1)], False, [], lambda tm: [],
                   [_sds((n_blk, t_len, fj), BF16)] * 3, lambda tm: [act(tm)] * 3, epilogue, n_blk, riders)


def ffn_down(act, wd, res, name, riders):
    n_blk, t_len, fj = act.shape
    d = wd.shape[2]
    return mm_reduce(name, [act], lambda tm: [pl.BlockSpec((None, tm, fj), lambda t, j: (j, t, 0))],
                     [wd], [pl.BlockSpec((None, fj, d), lambda t, j: (j, 0, 0))], False, res, 0.5,
                     t_len, d, n_blk, riders, tm_want=1024)


def ffn_bwd(ht, gate, up, act, wgu, wd, dyb, name):
    d, t_len = ht.shape
    n_blk, _, fj = act.shape

    def epilogue(accs, e_refs, o_refs, cols):
        d_act = 0.5 * accs[0]
        gv, uv = e_refs[0][:, cols].astype(F32), e_refs[1][:, cols].astype(F32)
        sg = _sigmoid(gv)
        o_refs[0][:, cols] = (d_act * uv * (sg * (1.0 + gv * (1.0 - sg)))).astype(BF16)
        o_refs[1][:, cols] = (d_act * gv * sg).astype(BF16)

    act_jt = lambda tm: pl.BlockSpec((None, tm, fj), lambda j, t: (j, t, 0))
    (d_gate, d_up), _ = mm_cols(name + "_dact", dyb, [wd], [_blk3(fj, d)], True, [gate, up],
                                lambda tm: [act_jt(tm)] * 2, [_sds((n_blk, t_len, fj), BF16)] * 2,
                                lambda tm: [act_jt(tm)] * 2, epilogue, n_blk)

    ht_spec = lambda tt: pl.BlockSpec((d, tt), lambda j, t: (0, t))
    (d_wg,), _ = mm_tn(name + "_dwg", ht, ht_spec, [d_gate], lambda tt: [act_jt(tt)],
                       [_sds((n_blk, d, fj), BF16)], [_blk3(d, fj)], 1.0, t_len, n_blk, x_transposed=True)
    (d_wu,), (recv_wg,) = mm_tn(name + "_dwu", ht, ht_spec, [d_up], lambda tt: [act_jt(tt)],
                                [_sds((n_blk, d, fj), BF16)], [_blk3(d, fj)], 1.0, t_len, n_blk,
                                [Rider("scatter", d_wg)], x_transposed=True)

    (d_wd,), (recv_wu,) = mm_tn(name + "_dwd", act, act_jt, [dyb],
                                lambda tt: [pl.BlockSpec((tt, d), lambda j, t: (t, 0))],
                                [_sds((n_blk, fj, d), BF16)], [_blk3(fj, d)], 0.5, t_len, n_blk,
                                [Rider("scatter", d_wu)])

    act_tj = lambda tm: pl.BlockSpec((None, tm, fj), lambda t, j: (j, t, 0))
    w_in = lambda i: pl.BlockSpec((None, None, d, fj), lambda t, j: (j, i, 0, 0))
    dh, (recv_wd,) = mm_reduce(
        name + "_dh", [d_gate, d_up], lambda tm: [act_tj(tm)] * 2,
        [wgu, wgu], [w_in(0), w_in(1)], True, None, 1.0, t_len, d, n_blk,
        [Rider("scatter", d_wd)], tm_want=1024)
    return dh, recv_wg, recv_wu, recv_wd


def local_step(x, target, g1, wgu1, wd1_s, gmix, win_s, conv_w, conv_b, ln_g, ln_b, gq, gk, wout_s, g3,
               wgu2_s, wd2_s):
    t_len, d = x.shape
    ch = d // 2
    ij = win_s.shape[1]
    oj = wout_s.shape[0]
    n_blk = N_DEV

    h1, h1t = rms_fwd(x, g1, "rms1")
    (gate1, up1, act1), (wd1, wgu2) = ffn_up(h1, wgu1, "ffn1_up",
                                             [Rider("gather", wd1_s), Rider("gather", wgu2_s)])
    x1, (win, wout) = ffn_down(act1, wd1, x, "ffn1_down", [Rider("gather", win_s), Rider("gather", wout_s)])

    h2, h2t = rms_fwd(x1, gmix, "rms_mix")

    def store_f32(accs, e_refs, o_refs, cols):
        o_refs[0][:, cols] = accs[0]

    (z,), (wd2,) = mm_cols(
        "w_in", h2, [win], [_blk3(d, ij)], False, [], lambda tm: [],
        [_sds((t_len, n_blk * ij), F32)],
        lambda tm: [pl.BlockSpec((tm, ij), lambda j, t: (t, j))], store_f32, n_blk,
        [Rider("gather", wd2_s)])

    conv_w32 = jnp.pad(conv_w, ((0, 32 - CONV_WIDTH), (0, 0)))
    y_conv, conv_pre = conv_fwd(z, conv_w32, conv_b, ln_g, ln_b, "conv_fwd")

    g2 = jnp.concatenate([jnp.tile(gq, (1, ch // HEAD_DIM)), jnp.tile(gk, (1, ch // HEAD_DIM))], axis=1)
    qk = qk_norm_fwd(z, g2, ch, "qk_norm")
    branch = [attn_fwd(qk, z, dil, ch, "attn_fwd_d%d" % dil) for dil in DILATIONS]
    att, att_b, lg = attn_combine([o for o, _ in branch], [l for _, l in branch], "attn_combine")

    y_cat = jnp.concatenate([y_conv, att_b], axis=1)
    wout_full = wout.reshape(1, n_blk * oj, d)
    x2, _ = mm_reduce(
        "w_out", [y_cat], lambda tm: [pl.BlockSpec((tm, n_blk * oj), lambda t, j: (t, 0))],
        [wout_full], [pl.BlockSpec((None, n_blk * oj, d), lambda t, j: (0, 0, 0))], False, x1, 1.0,
        t_len, d, 1)

    h3, h3t = rms_fwd(x2, g3, "rms3")
    (gate2, up2, act2), _ = ffn_up(h3, wgu2, "ffn2_up", [])
    y, _ = ffn_down(act2, wd2, x2, "ffn2_down", [])

    loss_tile, dy, dyb = loss_head(y, target, "loss")

    dh3, recv_wg2, recv_wu2, recv_wd2 = ffn_bwd(h3t, gate2, up2, act2, wgu2, wd2, dyb, "ffn2")
    dx2, dx2b, d_g3 = rms_bwd(x2, g3, dh3, dy, "rms3_bwd")

    (dy_cat,), _ = mm_cols("w_out_dy", dx2b, [wout_full], [_blk3(n_blk * oj, d)], True, [], lambda tm: [],
                           [_sds((t_len, n_blk * oj), F32)],
                           lambda tm: [pl.BlockSpec((tm, n_blk * oj), lambda j, t: (t, 0))], store_f32, 1)
    (d_wout,), _ = mm_tn("w_out_dw", y_cat, lambda tt: pl.BlockSpec((tt, oj), lambda j, t: (t, j)),
                         [dx2b], lambda tt: [pl.BlockSpec((tt, d), lambda j, t: (t, 0))],
                         [_sds((n_blk, oj, d), BF16)], [_blk3(oj, d)], 1.0, t_len, n_blk)

    dc, d_lg, d_lb, d_cb = conv_bwd_norm(conv_pre, dy_cat, ln_g, ln_b, "conv_bwd_norm")
    dz_a, dz_g, d_cw8 = conv_bwd_taps(z, dc, conv_w32, "conv_bwd_taps")
    d_cw = jnp.sum(d_cw8, axis=1)[:CONV_WIDTH]

    grads = [attn_bwd(qk, z, dy_cat, att, lg, dil, ch, "attn_bwd_d%d" % dil) for dil in DILATIONS]
    gq_t = jnp.tile(gq, (1, LANES // HEAD_DIM))
    gk_t = jnp.tile(gk, (1, LANES // HEAD_DIM))
    dz_q, d_gq2 = qk_norm_bwd(z, gq_t, [g[0] for g in grads], 2 * ch // LANES, ch, "q_norm_bwd")
    dz_k, d_gk2 = qk_norm_bwd(z, gk_t, [g[1] for g in grads], 3 * ch // LANES, ch, "k_norm_bwd")
    d_gq = d_gq2[:, :HEAD_DIM] + d_gq2[:, HEAD_DIM:]
    d_gk = d_gk2[:, :HEAD_DIM] + d_gk2[:, HEAD_DIM:]
    dz_v = sum3_bf16(grads[0][2], grads[1][2], grads[2][2], "dv_sum")
    dzb = jnp.concatenate([dz_a, dz_g, dz_q, dz_k, dz_v], axis=1)

    (d_win,), (recv_wout,) = mm_tn(
        "w_in_dw", h2t, lambda tt: pl.BlockSpec((d, tt), lambda j, t: (0, t)),
        [dzb], lambda tt: [pl.BlockSpec((tt, ij), lambda j, t: (t, j))],
        [_sds((n_blk, d, ij), BF16)], [_blk3(d, ij)], 1.0, t_len, n_blk, [Rider("scatter", d_wout)],
        x_transposed=True)
    dh2, (recv_win,) = mm_reduce(
        "w_in_dh", [dzb], lambda tm: [pl.BlockSpec((tm, ij), lambda t, j: (t, j))],
        [win], [pl.BlockSpec((None, d, ij), lambda t, j: (j, 0, 0))], True, None, 1.0,
        t_len, d, n_blk, [Rider("scatter", d_win)], tm_want=1024)
    dx1, dx1b, d_gmix = rms_bwd(x1, gmix, dh2, dx2, "rms_mix_bwd")

    dh1, recv_wg1, recv_wu1, recv_wd1 = ffn_bwd(h1t, gate1, up1, act1, wgu1, wd1, dx1b, "ffn1")
    grad_x, _, d_g1 = rms_bwd(x, g1, dh1, dx1, "rms1_bwd")

    big = dict(ffn1_w_gate=recv_wg1, ffn1_w_up=recv_wu1, ffn1_w_down=recv_wd1, w_in=recv_win, w_out=recv_wout,
               ffn2_w_gate=recv_wg2, ffn2_w_up=recv_wu2, ffn2_w_down=recv_wd2)
    small = dict(g1=d_g1, gmix=d_gmix, g3=d_g3, conv_b=d_cb, ln_g=d_lg, ln_b=d_lb, gq=d_gq, gk=d_gk, conv_w=d_cw)
    return loss_tile[0, 0], grad_x, big, small


SMALL_ROWS = 48


def _pack_small(ch, g1, gmix, g3, conv_b, ln_g, ln_b, gq, gk, conv_w):
    pad_head = lambda v: jnp.pad(v, ((0, 0), (0, ch - v.shape[1])))
    rows = [g1.reshape(2, ch), gmix.reshape(2, ch), g3.reshape(2, ch), conv_b, ln_g, ln_b,
            pad_head(gq), pad_head(gk), conv_w]
    packed = jnp.concatenate(rows, axis=0)
    return jnp.pad(packed, ((0, SMALL_ROWS - packed.shape[0]), (0, 0)))


def _unpack_small(packed, d):
    return dict(g1=packed[0:2].reshape(1, d), gmix=packed[2:4].reshape(1, d), g3=packed[4:6].reshape(1, d),
                conv_b=packed[6:7], ln_g=packed[7:8], ln_b=packed[8:9],
                gq=packed[9:10, :HEAD_DIM], gk=packed[10:11, :HEAD_DIM])


def kernel(x, ffn1_norm_g, ffn1_w_gate, ffn1_w_up, ffn1_w_down, mix_norm_g, w_in, conv_w_dw, conv_b_dw, conv_ln_g, conv_ln_b, q_norm_g, k_norm_g, w_out, ffn2_norm_g, ffn2_w_gate, ffn2_w_up, ffn2_w_down, loss_target, m_ffn1_norm_g, m_ffn1_w_gate, m_ffn1_w_up, m_ffn1_w_down, m_mix_norm_g, m_w_in, m_conv_w_dw, m_conv_b_dw, m_conv_ln_g, m_conv_ln_b, m_q_norm_g, m_k_norm_g, m_w_out, m_ffn2_norm_g, m_ffn2_w_gate, m_ffn2_w_up, m_ffn2_w_down, v_ffn1_norm_g, v_ffn1_w_gate, v_ffn1_w_up, v_ffn1_w_down, v_mix_norm_g, v_w_in, v_conv_w_dw, v_conv_b_dw, v_conv_ln_g, v_conv_ln_b, v_q_norm_g, v_k_norm_g, v_w_out, v_ffn2_norm_g, v_ffn2_w_gate, v_ffn2_w_up, v_ffn2_w_down):
    d = x.shape[-1]
    ch = d // 2
    me = 4 * lax.axis_index("x") + 2 * lax.axis_index("y") + lax.axis_index("c")

    gu = lambda wg, wu: jnp.stack([wg[0], wu[0]]).astype(BF16)
    wgu1 = all_gather(gu(ffn1_w_gate, ffn1_w_up), "ag_wgu1")
    cw_all = all_gather(conv_w_dw[0], "ag_convw")
    conv_w = jnp.transpose(cw_all, (1, 0, 2)).reshape(CONV_WIDTH, ch)

    loss_part, grad_x, big, small = local_step(
        x[0], loss_target[0], ffn1_norm_g, wgu1, ffn1_w_down[0].astype(BF16), mix_norm_g, w_in[0].astype(BF16),
        conv_w, conv_b_dw, conv_ln_g, conv_ln_b, q_norm_g, k_norm_g, w_out[0].astype(BF16), ffn2_norm_g,
        gu(ffn2_w_gate, ffn2_w_up), ffn2_w_down[0].astype(BF16))
    loss = lax.psum(loss_part, MESH_AXES)

    state = dict(
        ffn1_w_gate=(ffn1_w_gate, m_ffn1_w_gate, v_ffn1_w_gate), ffn1_w_up=(ffn1_w_up, m_ffn1_w_up, v_ffn1_w_up),
        ffn1_w_down=(ffn1_w_down, m_ffn1_w_down, v_ffn1_w_down), w_in=(w_in, m_w_in, v_w_in),
        w_out=(w_out, m_w_out, v_w_out),
        ffn2_w_gate=(ffn2_w_gate, m_ffn2_w_gate, v_ffn2_w_gate), ffn2_w_up=(ffn2_w_up, m_ffn2_w_up, v_ffn2_w_up),
        ffn2_w_down=(ffn2_w_down, m_ffn2_w_down, v_ffn2_w_down))
    out = {}
    for pname, (w, m, v) in state.items():
        res = adamw(w[0], m[0], v[0], big[pname], "adamw_" + pname)
        out[pname] = [r[None] for r in res]

    zero_taps = jnp.zeros((CONV_WIDTH, ch), F32)
    pack = lambda g1, gm, g3, cb, lg, lb, gq, gk: _pack_small(ch, g1, gm, g3, cb, lg, lb, gq, gk, zero_taps)
    small_parts = all_gather(_pack_small(ch, **small), "ag_small_grads")
    s_res = adamw(
        pack(ffn1_norm_g, mix_norm_g, ffn2_norm_g, conv_b_dw, conv_ln_g, conv_ln_b, q_norm_g, k_norm_g),
        pack(m_ffn1_norm_g, m_mix_norm_g, m_ffn2_norm_g, m_conv_b_dw, m_conv_ln_g, m_conv_ln_b, m_q_norm_g, m_k_norm_g),
        pack(v_ffn1_norm_g, v_mix_norm_g, v_ffn2_norm_g, v_conv_b_dw, v_conv_ln_g, v_conv_ln_b, v_q_norm_g, v_k_norm_g),
        small_parts, "adamw_small")
    s_out = [_unpack_small(r, d) for r in s_res]
    names = dict(g1="ffn1_norm_g", gmix="mix_norm_g", g3="ffn2_norm_g", conv_b="conv_b_dw", ln_g="conv_ln_g",
                 ln_b="conv_ln_b", gq="q_norm_g", gk="k_norm_g")
    for key, full in names.items():
        out[full] = [r[key] for r in s_out]

    cshard = ch // N_DEV
    taps_sum = s_res[0][11:11 + CONV_WIDTH]
    taps_mine = lax.dynamic_slice(taps_sum, (0, me * cshard), (CONV_WIDTH, cshard))
    pad_taps = lambda a: jnp.pad(a, ((0, 32 - CONV_WIDTH), (0, 0)))
    c_res = adamw(pad_taps(conv_w_dw[0]), pad_taps(m_conv_w_dw[0]), pad_taps(v_conv_w_dw[0]),
                  pad_taps(taps_mine)[None], "adamw_convw")
    out["conv_w_dw"] = [r[:CONV_WIDTH][None] for r in c_res]

    order = ["ffn1_norm_g", "ffn1_w_gate", "ffn1_w_up", "ffn1_w_down", "mix_norm_g", "w_in", "conv_w_dw",
             "conv_b_dw", "conv_ln_g", "conv_ln_b", "q_norm_g", "k_norm_g", "w_out", "ffn2_norm_g",
             "ffn2_w_gate", "ffn2_w_up", "ffn2_w_down"]
    result = [loss, grad_x[None]]
    for kind in range(4):
        result += [out[n][kind] for n in order]
    return tuple(result)
```

```python
import math
from typing import NamedTuple

import jax
import jax.numpy as jnp
from jax import lax
from jax.experimental import pallas as pl
from jax.experimental.pallas import tpu as pltpu

F32 = jnp.float32
BF16 = jnp.bfloat16

N_DEV = 8
EPS = 1e-6
HEAD_DIM = 64
LANES = 128
MXU_WIDTH = 256
CONV_WIDTH = 31
SUBLANES = 8
HALO = 32
ROW_CHUNK = 32
ATT_BLOCK = 128
DILATIONS = (1, 4, 16)
ATT_UNITS = 16
ATT_CHUNK = ATT_UNITS * ATT_BLOCK
ALIBI_MAX_BIAS = 8.0
MASKED = -1e30
VMEM_LIMIT = 56 * 1024 * 1024

ADAM_LR = 0.001
ADAM_B1 = 0.9
ADAM_B2 = 0.999
ADAM_EPS = 1e-08
ADAM_WD = 0.01
ADAM_STEP = 10

MESH_AXES = ("x", "y", "c")
ANY = pl.BlockSpec(memory_space=pl.ANY)


def _sds(shape, dtype):
    return jax.ShapeDtypeStruct(tuple(shape), dtype)


def _params(*sem):
    return pltpu.CompilerParams(dimension_semantics=sem, vmem_limit_bytes=VMEM_LIMIT)


def _sigmoid(v):
    return 1.0 / (1.0 + jnp.exp(-v))


def _row_tile(t, want):
    for cand in range(min(want, t) // 8 * 8, 0, -8):
        if t % cand == 0:
            return cand
    return t


def _mesh_pos():
    return lax.axis_index("x"), lax.axis_index("y"), lax.axis_index("c")


def _comm_sems():
    return [pltpu.SemaphoreType.DMA((7,)), pltpu.SemaphoreType.DMA((7,)), pltpu.SemaphoreType.DMA(())]


def _gather_phases(x_ref, out_ref, send_sems, recv_sems, local_sem):
    x, y, c = _mesh_pos()
    me, sibling = (x, y, c), (x, y, 1 - c)
    chips = [(1 - x, y), (x, 1 - y), (1 - x, 1 - y)]

    def slot(px, py, pc):
        return out_ref.at[4 * px + 2 * py + pc]

    def copy(k, block, to, src=None):
        return pltpu.make_async_remote_copy(
            src_ref=slot(*block) if src is None else src, dst_ref=slot(*block),
            send_sem=send_sems.at[k], recv_sem=recv_sems.at[k],
            device_id=to, device_id_type=pl.DeviceIdType.MESH)

    mine = pltpu.make_async_copy(x_ref, slot(*me), local_sem)
    first = [copy(0, me, sibling, src=x_ref)]
    first += [copy(1 + j, me, (*chip, c), src=x_ref) for j, chip in enumerate(chips)]
    passed = [copy(4 + j, (*chip, c), sibling) for j, chip in enumerate(chips)]

    def start():
        mine.start()
        for cp in first:
            cp.start()

    def forward():
        for j, chip in enumerate(chips):
            copy(1 + j, (*chip, c), me).wait_recv()
            passed[j].start()

    def finish():
        copy(0, sibling, me).wait_recv()
        for j, chip in enumerate(chips):
            copy(4 + j, (*chip, 1 - c), me).wait_recv()
        for cp in first + passed:
            cp.wait_send()
        mine.wait()

    return start, forward, finish


def _scatter_phases(p_ref, out_ref, send_sems, recv_sems, local_sem):
    x, y, c = _mesh_pos()
    me = 4 * x + 2 * y + c
    flips = [(fx, fy, fc) for fx in (0, 1) for fy in (0, 1) for fc in (0, 1)][1:]

    def copy(k, flip, receiving):
        px, py, pc = (1 - x if flip[0] else x, 1 - y if flip[1] else y, 1 - c if flip[2] else c)
        them = 4 * px + 2 * py + pc
        return pltpu.make_async_remote_copy(
            src_ref=p_ref.at[them], dst_ref=out_ref.at[them if receiving else me],
            send_sem=send_sems.at[k], recv_sem=recv_sems.at[k],
            device_id=(px, py, pc), device_id_type=pl.DeviceIdType.MESH)

    mine = pltpu.make_async_copy(p_ref.at[me], out_ref.at[me], local_sem)

    def start():
        mine.start()
        for k, flip in enumerate(flips):
            copy(k, flip, False).start()

    def finish():
        for k, flip in enumerate(flips):
            copy(k, flip, True).wait_recv()
            copy(k, flip, False).wait_send()
        mine.wait()

    return start, None, finish


class Rider(NamedTuple):
    kind: str
    src: jax.Array

    def out_shape(self):
        shape = (N_DEV,) + self.src.shape if self.kind == "gather" else self.src.shape
        return _sds(shape, self.src.dtype)


def _rider_hooks(riders, in_refs, out_refs, sem_refs, step, n_steps):
    phases = [(_gather_phases if r.kind == "gather" else _scatter_phases)(
                  in_refs[i], out_refs[i], *sem_refs[3 * i:3 * i + 3]) for i, r in enumerate(riders)]

    def begin():
        for start, forward, _ in phases:
            pl.when(step == 0)(start)
            if forward is not None:
                pl.when(step == (7 * n_steps) // 8)(forward)

    def end():
        for _, _, finish in phases:
            pl.when(step == n_steps - 1)(finish)

    return begin, end


def _split_refs(refs, n_in, n_out, n_scratch, n_riders):
    pos, parts = 0, []
    for n in (n_in, n_riders, n_out, n_riders, n_scratch, 3 * n_riders):
        parts.append(refs[pos:pos + n])
        pos += n
    return parts


def all_gather(shard, name):
    def body(x_ref, out_ref, send_sems, recv_sems, local_sem):
        start, forward, finish = _gather_phases(x_ref, out_ref, send_sems, recv_sems, local_sem)
        start()
        forward()
        finish()

    return pl.pallas_call(
        body, name=name, out_shape=_sds((N_DEV,) + shard.shape, shard.dtype),
        in_specs=[ANY], out_specs=ANY, scratch_shapes=_comm_sems(),
    )(shard)


def adamw(w, m, v, parts, name):
    n_parts, rows, cols = parts.shape
    tr = _row_tile(rows, 128)
    c1 = 1.0 - ADAM_B1 ** ADAM_STEP
    c2 = 1.0 - ADAM_B2 ** ADAM_STEP

    def body(w_ref, m_ref, v_ref, p_ref, g_ref, d_ref, nm_ref, nv_ref):
        g = p_ref[0].astype(F32)
        for s in range(1, n_parts):
            g = g + p_ref[s].astype(F32)
        nm = ADAM_B1 * m_ref[...] + (1.0 - ADAM_B1) * g
        nv = ADAM_B2 * v_ref[...] + (1.0 - ADAM_B2) * (g * g)
        delta = -ADAM_LR * ((nm / c1) / (jnp.sqrt(nv / c2) + ADAM_EPS) + ADAM_WD * w_ref[...])
        g_ref[...] = g
        d_ref[...] = delta
        nm_ref[...] = nm
        nv_ref[...] = nv

    mat = pl.BlockSpec((tr, cols), lambda i: (i, 0))
    return pl.pallas_call(
        body, name=name, grid=(rows // tr,),
        in_specs=[mat, mat, mat, pl.BlockSpec((n_parts, tr, cols), lambda i: (0, i, 0))],
        out_specs=[mat, mat, mat, mat],
        out_shape=[_sds((rows, cols), F32)] * 4,
        compiler_params=_params("parallel"),
    )(w, m, v, parts)


_NN = (((1,), (0,)), ((), ()))
_NT = (((1,), (1,)), ((), ()))
_TN = (((0,), (0,)), ((), ()))


def mm_cols(name, a, b_list, b_specs, nt, extras, extra_specs, out_shapes, out_specs, epilogue, n_blk, riders=()):
    t_len, k_len = a.shape
    tm = _row_tile(t_len, 512)
    nb, ne, n_out, nr = len(b_list), len(extras), len(out_shapes), len(riders)
    t_steps = t_len // tm
    n_cols = b_specs[0].block_shape[-2 if nt else -1]

    def body(*refs):
        ins, r_in, outs, r_out, _, r_sem = _split_refs(refs, 1 + nb + ne, n_out, 0, nr)
        step = pl.program_id(0) * t_steps + pl.program_id(1)
        begin, end = _rider_hooks(riders, r_in, r_out, r_sem, step, n_blk * t_steps)
        begin()
        av = ins[0][...]
        for c0 in range(0, n_cols, MXU_WIDTH):
            cols = slice(c0, min(c0 + MXU_WIDTH, n_cols))
            accs = [lax.dot_general(av, br[cols, :] if nt else br[:, cols], _NT if nt else _NN,
                                    preferred_element_type=F32) for br in ins[1:1 + nb]]
            epilogue(accs, ins[1 + nb:], outs, cols)
        end()

    res = pl.pallas_call(
        body, name=name, grid=(n_blk, t_steps),
        in_specs=([pl.BlockSpec((tm, k_len), lambda j, t: (t, 0))] + list(b_specs) + list(extra_specs(tm))
                  + [ANY] * nr),
        out_specs=list(out_specs(tm)) + [ANY] * nr,
        out_shape=list(out_shapes) + [r.out_shape() for r in riders],
        scratch_shapes=_comm_sems() * nr,
        compiler_params=_params("arbitrary", "arbitrary"),
    )(a, *b_list, *extras, *[r.src for r in riders])
    return res[:n_out], res[n_out:]


def mm_reduce(name, a_list, a_specs, b_list, b_specs, nt, res, scale, t_len, n_len, n_blk, riders=(), tm_want=512):
    tm = _row_tile(t_len, tm_want)
    na, nr = len(a_list), len(riders)
    has_res = res is not None
    t_steps = t_len // tm

    def body(*refs):
        ins, r_in, outs, r_out, _, r_sem = _split_refs(refs, 2 * na + has_res, 1, 0, nr)
        o_ref = outs[0]
        j = pl.program_id(1)
        step = pl.program_id(0) * n_blk + j
        begin, end = _rider_hooks(riders, r_in, r_out, r_sem, step, t_steps * n_blk)
        begin()

        part = None
        for ar, br in zip(ins[:na], ins[na:2 * na]):
            d = lax.dot_general(ar[...], br[...], _NT if nt else _NN, preferred_element_type=F32)
            part = d if part is None else part + d

        @pl.when(j == 0)
        def _():
            o_ref[...] = part

        @pl.when(j > 0)
        def _():
            o_ref[...] += part

        if has_res or scale != 1.0:
            @pl.when(j == n_blk - 1)
            def _():
                val = o_ref[...] * scale if scale != 1.0 else o_ref[...]
                o_ref[...] = ins[2 * na][...] + val if has_res else val

        end()

    row = pl.BlockSpec((tm, n_len), lambda t, j: (t, 0))
    out = pl.pallas_call(
        body, name=name, grid=(t_steps, n_blk),
        in_specs=list(a_specs(tm)) + list(b_specs) + ([row] if has_res else []) + [ANY] * nr,
        out_specs=[row] + [ANY] * nr,
        out_shape=[_sds((t_len, n_len), F32)] + [r.out_shape() for r in riders],
        scratch_shapes=_comm_sems() * nr,
        compiler_params=_params("arbitrary", "arbitrary"),
    )(*a_list, *b_list, *([res] if has_res else []), *[r.src for r in riders])
    return out[0], out[1:]


def mm_tn(name, x, x_spec, dy_list, dy_specs, out_shapes, out_specs, scale, t_len, n_blk, riders=(),
          x_transposed=False):
    tt = _row_tile(t_len, 2048)
    nd, nr = len(dy_list), len(riders)
    t_steps = t_len // tt
    acc_shapes = [pltpu.VMEM(spec.block_shape[-2:], F32) for spec in out_specs]

    def body(*refs):
        ins, r_in, outs, r_out, accs, r_sem = _split_refs(refs, 1 + nd, nd, nd, nr)
        t = pl.program_id(1)
        step = pl.program_id(0) * t_steps + t
        begin, end = _rider_hooks(riders, r_in, r_out, r_sem, step, n_blk * t_steps)
        begin()
        xv = ins[0][...]
        for dr, acc in zip(ins[1:], accs):
            d = lax.dot_general(xv, dr[...], _NN if x_transposed else _TN, preferred_element_type=F32)

            @pl.when(t == 0)
            def _():
                acc[...] = d

            @pl.when(t > 0)
            def _():
                acc[...] += d

        @pl.when(t == t_steps - 1)
        def _():
            for acc, orf in zip(accs, outs):
                val = acc[...] * scale if scale != 1.0 else acc[...]
                orf[...] = val.astype(orf.dtype)

        end()

    res = pl.pallas_call(
        body, name=name, grid=(n_blk, t_steps),
        in_specs=[x_spec(tt)] + list(dy_specs(tt)) + [ANY] * nr,
        out_specs=list(out_specs) + [ANY] * nr,
        out_shape=list(out_shapes) + [r.out_shape() for r in riders],
        scratch_shapes=acc_shapes + _comm_sems() * nr,
        compiler_params=_params("arbitrary", "arbitrary"),
    )(x, *dy_list, *[r.src for r in riders])
    return res[:nd], res[nd:]


def rms_fwd(x, g, name):
    t_len, d = x.shape
    tm = _row_tile(t_len, 512)

    def body(x_ref, g_ref, h_ref, ht_ref):
        xv = x_ref[...]
        r = lax.rsqrt(jnp.mean(xv * xv, axis=-1, keepdims=True) + EPS)
        hv = xv * r * g_ref[...]
        h_ref[...] = hv.astype(BF16)
        ht_ref[...] = hv.T.astype(BF16)

    row = pl.BlockSpec((tm, d), lambda i: (i, 0))
    return pl.pallas_call(
        body, name=name, grid=(t_len // tm,),
        in_specs=[row, pl.BlockSpec((1, d), lambda i: (0, 0))],
        out_specs=[row, pl.BlockSpec((d, tm), lambda i: (0, i))],
        out_shape=[_sds((t_len, d), BF16), _sds((d, t_len), BF16)],
        compiler_params=_params("parallel"),
    )(x, g)


def rms_bwd(x, g, dh, dres, name):
    t_len, d = x.shape
    tm = _row_tile(t_len, 512)

    def body(x_ref, g_ref, dh_ref, dr_ref, dx_ref, dxb_ref, dg_ref):
        i = pl.program_id(0)
        xv = x_ref[...]
        r = lax.rsqrt(jnp.mean(xv * xv, axis=-1, keepdims=True) + EPS)
        xh = xv * r
        dhv = dh_ref[...]

        @pl.when(i == 0)
        def _():
            dg_ref[...] = jnp.zeros_like(dg_ref)

        dg_ref[...] += jnp.sum(dhv * xh, axis=0, keepdims=True)
        dxh = dhv * g_ref[...]
        dx = dr_ref[...] + r * (dxh - xh * jnp.mean(dxh * xh, axis=-1, keepdims=True))
        dx_ref[...] = dx
        dxb_ref[...] = dx.astype(BF16)

    row = pl.BlockSpec((tm, d), lambda i: (i, 0))
    vec = pl.BlockSpec((1, d), lambda i: (0, 0))
    return pl.pallas_call(
        body, name=name, grid=(t_len // tm,),
        in_specs=[row, vec, row, row],
        out_specs=[row, row, vec],
        out_shape=[_sds((t_len, d), F32), _sds((t_len, d), BF16), _sds((1, d), F32)],
        compiler_params=_params("arbitrary"),
    )(x, g, dh, dres)


def loss_head(y, target, name):
    t_len, d = y.shape
    tm = _row_tile(t_len, 512)

    def body(y_ref, t_ref, l_ref, dy_ref, dyb_ref):
        i = pl.program_id(0)
        err = y_ref[...] - t_ref[...]

        @pl.when(i == 0)
        def _():
            l_ref[...] = jnp.zeros_like(l_ref)

        rows = jnp.sum(err * err, axis=-1, keepdims=True) * (1.0 / d)
        l_ref[...] += 0.5 * jnp.sum(rows, axis=0, keepdims=True)
        dy = err * (1.0 / d)
        dy_ref[...] = dy
        dyb_ref[...] = dy.astype(BF16)

    row = pl.BlockSpec((tm, d), lambda i: (i, 0))
    return pl.pallas_call(
        body, name=name, grid=(t_len // tm,),
        in_specs=[row, row],
        out_specs=[pl.BlockSpec((8, LANES), lambda i: (0, 0)), row, row],
        out_shape=[_sds((8, LANES), F32), _sds((t_len, d), F32), _sds((t_len, d), BF16)],
        compiler_params=_params("arbitrary"),
    )(y, target)


def _conv_specs(tm, ch):
    per = tm // HALO
    cur = lambda cb: pl.BlockSpec((tm, ch), lambda i: (i, cb))
    prev = lambda cb: pl.BlockSpec((HALO, ch), lambda i: (jnp.maximum(i * per - 1, 0), cb))
    return [cur(0), cur(1), prev(0), prev(1)]


def _tap_scratch(rows, ch):
    return pltpu.VMEM((SUBLANES, rows + SUBLANES, ch), F32)


def _shifted_copies(buf, rows):
    buf[0, rows:rows + SUBLANES, :] = jnp.zeros((SUBLANES, buf.shape[2]), F32)
    for s in range(1, SUBLANES):
        buf[s, 0:rows, :] = buf[0, pl.ds(s, rows), :]


def _tap_rows(buf, off):
    shift = off % SUBLANES
    return buf[shift, off - shift:off - shift + ROW_CHUNK, :]


def _fill_glu(ext, a_ref, gt_ref, ap_ref, gp_ref, i, tm):
    vp = ap_ref[...] * _sigmoid(gp_ref[...])
    ext[0, 0:HALO, :] = jnp.where(i > 0, vp, 0.0)
    ext[0, HALO:HALO + tm, :] = a_ref[...] * _sigmoid(gt_ref[...])
    _shifted_copies(ext, HALO + tm)


def _conv_rows(ext, w_ref, b_ref, r0):
    acc = jnp.broadcast_to(b_ref[...], (ROW_CHUNK, b_ref.shape[1]))
    for k in range(CONV_WIDTH):
        acc = acc + w_ref[k:k + 1, :] * _tap_rows(ext, r0 + HALO - (CONV_WIDTH - 1) + k)
    return acc


def _layer_norm(yv):
    mu = jnp.mean(yv, axis=-1, keepdims=True)
    cen = yv - mu
    var = jnp.mean(cen * cen, axis=-1, keepdims=True)
    rstd = lax.rsqrt(var + EPS)
    return cen * rstd, rstd


def conv_fwd(z, w, b, lg, lb, name):
    t_len = z.shape[0]
    ch = w.shape[1]
    tm = _row_tile(t_len, 256)

    def body(a_ref, gt_ref, ap_ref, gp_ref, w_ref, b_ref, lg_ref, lb_ref, y_ref, pre_ref, ext):
        i = pl.program_id(0)
        _fill_glu(ext, a_ref, gt_ref, ap_ref, gp_ref, i, tm)
        for r0 in range(0, tm, ROW_CHUNK):
            pre = _conv_rows(ext, w_ref, b_ref, r0)
            pre_ref[r0:r0 + ROW_CHUNK, :] = pre
            xh, _ = _layer_norm(pre)
            u = xh * lg_ref[...] + lb_ref[...]
            y_ref[r0:r0 + ROW_CHUNK, :] = (u * _sigmoid(u)).astype(BF16)

    vec = pl.BlockSpec((1, ch), lambda i: (0, 0))
    row = pl.BlockSpec((tm, ch), lambda i: (i, 0))
    return pl.pallas_call(
        body, name=name, grid=(t_len // tm,),
        in_specs=_conv_specs(tm, ch) + [pl.BlockSpec((32, ch), lambda i: (0, 0)), vec, vec, vec],
        out_specs=[row, row],
        out_shape=[_sds((t_len, ch), BF16), _sds((t_len, ch), F32)],
        scratch_shapes=[_tap_scratch(HALO + tm, ch)],
        compiler_params=_params("parallel"),
    )(z, z, z, z, w, b, lg, lb)


def conv_bwd_norm(pre, dy_cat, lg, lb, name):
    t_len, ch = pre.shape
    tm = _row_tile(t_len, 256)

    def body(pre_ref, dy_ref, lg_ref, lb_ref, dc_ref, dlg_ref, dlb_ref, db_ref):
        i = pl.program_id(0)

        @pl.when(i == 0)
        def _():
            dlg_ref[...] = jnp.zeros_like(dlg_ref)
            dlb_ref[...] = jnp.zeros_like(dlb_ref)
            db_ref[...] = jnp.zeros_like(db_ref)

        for r0 in range(0, tm, ROW_CHUNK):
            xh, rstd = _layer_norm(pre_ref[r0:r0 + ROW_CHUNK, :])
            u = xh * lg_ref[...] + lb_ref[...]
            sg = _sigmoid(u)
            du = dy_ref[r0:r0 + ROW_CHUNK, :] * (sg * (1.0 + u * (1.0 - sg)))
            dlg_ref[...] += jnp.sum(du * xh, axis=0, keepdims=True)
            dlb_ref[...] += jnp.sum(du, axis=0, keepdims=True)
            dxh = du * lg_ref[...]
            dc = rstd * (dxh - jnp.mean(dxh, axis=-1, keepdims=True)
                         - xh * jnp.mean(dxh * xh, axis=-1, keepdims=True))
            db_ref[...] += jnp.sum(dc, axis=0, keepdims=True)
            dc_ref[r0:r0 + ROW_CHUNK, :] = dc

    vec = pl.BlockSpec((1, ch), lambda i: (0, 0))
    row = pl.BlockSpec((tm, ch), lambda i: (i, 0))
    return pl.pallas_call(
        body, name=name, grid=(t_len // tm,),
        in_specs=[row, row, vec, vec],
        out_specs=[row, vec, vec, vec],
        out_shape=[_sds((t_len, ch), F32)] + [_sds((1, ch), F32)] * 3,
        compiler_params=_params("arbitrary"),
    )(pre, dy_cat, lg, lb)


def conv_bwd_taps(z, dc, w, name):
    t_len = z.shape[0]
    ch = w.shape[1]
    tm = _row_tile(t_len, 256)
    per = tm // HALO
    n_tiles = t_len // tm
    last_halo = t_len // HALO - 1

    def body(a_ref, gt_ref, ap_ref, gp_ref, dc_ref, dn_ref, w_ref, dz_a_ref, dz_g_ref, dw_ref, ext, dext):
        i = pl.program_id(0)
        _fill_glu(ext, a_ref, gt_ref, ap_ref, gp_ref, i, tm)
        dext[0, 0:tm, :] = dc_ref[...]
        dext[0, tm:tm + HALO, :] = jnp.where(i < n_tiles - 1, dn_ref[...], 0.0)
        _shifted_copies(dext, tm + HALO)

        @pl.when(i == 0)
        def _():
            dw_ref[...] = jnp.zeros_like(dw_ref)

        for r0 in range(0, tm, ROW_CHUNK):
            dcv = dext[0, r0:r0 + ROW_CHUNK, :]
            dv = jnp.zeros((ROW_CHUNK, ch), F32)
            for k in range(CONV_WIDTH):
                dv = dv + w_ref[k:k + 1, :] * _tap_rows(dext, r0 + (CONV_WIDTH - 1) - k)
                prod = dcv * _tap_rows(ext, r0 + HALO - (CONV_WIDTH - 1) + k)
                fold = prod[0:8]
                for s in range(8, ROW_CHUNK, 8):
                    fold = fold + prod[s:s + 8]
                dw_ref[k] += fold
            av = a_ref[r0:r0 + ROW_CHUNK, :]
            sg = _sigmoid(gt_ref[r0:r0 + ROW_CHUNK, :])
            dz_a_ref[r0:r0 + ROW_CHUNK, :] = (dv * sg).astype(BF16)
            dz_g_ref[r0:r0 + ROW_CHUNK, :] = (dv * av * sg * (1.0 - sg)).astype(BF16)

    row = pl.BlockSpec((tm, ch), lambda i: (i, 0))
    nxt = pl.BlockSpec((HALO, ch), lambda i: (jnp.minimum((i + 1) * per, last_halo), 0))
    return pl.pallas_call(
        body, name=name, grid=(n_tiles,),
        in_specs=_conv_specs(tm, ch) + [row, nxt, pl.BlockSpec((32, ch), lambda i: (0, 0))],
        out_specs=[row, row, pl.BlockSpec((32, 8, ch), lambda i: (0, 0, 0))],
        out_shape=[_sds((t_len, ch), BF16), _sds((t_len, ch), BF16), _sds((32, 8, ch), F32)],
        scratch_shapes=[_tap_scratch(HALO + tm, ch), _tap_scratch(tm + HALO, ch)],
        compiler_params=_params("arbitrary"),
    )(z, z, z, z, dc, dc, w)


def _head_masks(rows):
    lane = lax.broadcasted_iota(jnp.int32, (rows, LANES), 1)
    low = lane < HEAD_DIM
    return low, jnp.logical_not(low)


def _per_head_mean(val, low):
    s_low = jnp.sum(jnp.where(low, val, 0.0), axis=-1, keepdims=True)
    s_high = jnp.sum(jnp.where(low, 0.0, val), axis=-1, keepdims=True)
    return jnp.where(low, s_low, s_high) * (1.0 / HEAD_DIM)


def qk_norm_fwd(z, g2, ch, name):
    t_len = z.shape[0]
    tm = _row_tile(t_len, 1024)
    n_col = 2 * ch // LANES
    z_off = 2 * ch // LANES

    def body(z_ref, g_ref, o_ref):
        low, _ = _head_masks(tm)
        xv = z_ref[...]
        r = lax.rsqrt(_per_head_mean(xv * xv, low) + EPS)
        o_ref[...] = xv * r * g_ref[...]

    return pl.pallas_call(
        body, name=name, grid=(t_len // tm, n_col),
        in_specs=[pl.BlockSpec((tm, LANES), lambda i, cb: (i, z_off + cb)),
                  pl.BlockSpec((1, LANES), lambda i, cb: (0, cb))],
        out_specs=pl.BlockSpec((tm, LANES), lambda i, cb: (i, cb)),
        out_shape=_sds((t_len, 2 * ch), F32),
        compiler_params=_params("parallel", "parallel"),
    )(z, g2)


def qk_norm_bwd(z, g, d_list, z_off, ch, name):
    t_len = z.shape[0]
    tm = _row_tile(t_len, 1024)
    n_col = ch // LANES
    nd = len(d_list)

    def body(*refs):
        z_ref, g_ref, d_refs = refs[0], refs[1], refs[2:2 + nd]
        dz_ref, dg_ref = refs[2 + nd], refs[3 + nd]
        first = jnp.logical_and(pl.program_id(0) == 0, pl.program_id(1) == 0)
        low, _ = _head_masks(tm)
        xv = z_ref[...]
        r = lax.rsqrt(_per_head_mean(xv * xv, low) + EPS)
        xh = xv * r
        dy = d_refs[0][...]
        for dr in d_refs[1:]:
            dy = dy + dr[...]

        @pl.when(first)
        def _():
            dg_ref[...] = jnp.zeros_like(dg_ref)

        dg_ref[...] += jnp.sum(dy * xh, axis=0, keepdims=True)
        dxh = dy * g_ref[...]
        dz_ref[...] = (r * (dxh - xh * _per_head_mean(dxh * xh, low))).astype(BF16)

    blk = pl.BlockSpec((tm, LANES), lambda i, cb: (i, cb))
    return pl.pallas_call(
        body, name=name, grid=(t_len // tm, n_col),
        in_specs=[pl.BlockSpec((tm, LANES), lambda i, cb: (i, z_off + cb)),
                  pl.BlockSpec((1, LANES), lambda i, cb: (0, 0))] + [blk] * nd,
        out_specs=[blk, pl.BlockSpec((1, LANES), lambda i, cb: (0, 0))],
        out_shape=[_sds((t_len, ch), BF16), _sds((1, LANES), F32)],
        compiler_params=_params("arbitrary", "arbitrary"),
    )(z, g, *d_list)


def _alibi_bias(n_heads, dilation):
    slopes = 2.0 ** (-ALIBI_MAX_BIAS * jnp.arange(1, n_heads + 1, dtype=F32) / n_heads)
    qi = jnp.arange(ATT_BLOCK)[:, None]
    kj = jnp.arange(ATT_BLOCK)[None, :]
    dist_cur = (qi - kj).astype(F32)
    dist_prev = (ATT_BLOCK + qi - kj).astype(F32)
    cur = jnp.where((qi >= kj)[None], -slopes[:, None, None] * (dilation * dist_cur)[None], MASKED)
    prev = jnp.where((kj >= qi)[None], -slopes[:, None, None] * (dilation * dist_prev)[None], MASKED)
    return jnp.concatenate([prev, cur], axis=-1).astype(F32)


def _stack_heads(val, low, high):
    return jnp.concatenate([jnp.where(low, val, 0.0), jnp.where(high, val, 0.0)], axis=0).astype(BF16)


def _head_rows(val, low, high):
    return jnp.concatenate([jnp.max(jnp.where(low, val, MASKED), axis=-1, keepdims=True),
                            jnp.max(jnp.where(high, val, MASKED), axis=-1, keepdims=True)], axis=0)


def _unit_scores(q2, k2, b_ref, has_prev):
    s = lax.dot_general(q2, k2, _NT, preferred_element_type=F32)
    s = s + b_ref[...].reshape(2 * ATT_BLOCK, 2 * ATT_BLOCK)
    penalty = jnp.where(has_prev, 0.0, MASKED)
    return jnp.concatenate([s[:, :ATT_BLOCK] + penalty, s[:, ATT_BLOCK:]], axis=1)


def _strided_rows(r, dilation):
    per = ATT_CHUNK // dilation
    return pl.ds(r, per, stride=dilation) if dilation > 1 else pl.ds(0, per)


def _deinterleave(dst, src_ref, dilation, base=None, dtype=None):
    per = ATT_CHUNK // dilation
    for r in range(dilation):
        val = src_ref[_strided_rows(r, dilation), :]
        val = val if dtype is None else val.astype(dtype)
        if base is None:
            dst[r * per:(r + 1) * per, :] = val
        else:
            dst[pl.ds(pl.multiple_of(base + r * per, ATT_BLOCK), per), :] = val


def _unit_rows(u, c, nb, base, pbase):
    in_chunk = lax.rem(u, jnp.int32(nb)) > 0
    has_prev = jnp.logical_or(in_chunk, c > 0)
    urow = pl.multiple_of(u * ATT_BLOCK, ATT_BLOCK)
    crow = pl.multiple_of(base + u * ATT_BLOCK, ATT_BLOCK)
    prow = pl.multiple_of(jnp.where(in_chunk, base + (u - 1) * ATT_BLOCK,
                                    pbase + (u + nb - 1) * ATT_BLOCK), ATT_BLOCK)
    return in_chunk, has_prev, urow, crow, prow


def _interleave(dst_ref, src, dilation, base=None):
    per = ATT_CHUNK // dilation
    for r in range(dilation):
        if base is None:
            val = src[r * per:(r + 1) * per, :]
        else:
            val = src[pl.ds(pl.multiple_of(base + r * per, ATT_BLOCK), per), :]
        dst_ref[_strided_rows(r, dilation), :] = val


def attn_fwd(qk, z, dilation, ch, name):
    t_len = qk.shape[0]
    pairs = ch // LANES
    nc = t_len // ATT_CHUNK
    nb = ATT_UNITS // dilation
    scale = 1.0 / math.sqrt(HEAD_DIM)
    bias = _alibi_bias(2 * pairs, dilation)

    def body(q_ref, k_ref, v_ref, b_ref, o_ref, l_ref, qd, kx, vx, od, ld):
        c = pl.program_id(1)
        slot = lax.rem(c, jnp.int32(2))
        base, pbase = slot * ATT_CHUNK, (1 - slot) * ATT_CHUNK

        @pl.when(c == 0)
        def _():
            kx[...] = jnp.zeros_like(kx)
            vx[...] = jnp.zeros_like(vx)

        _deinterleave(qd, q_ref, dilation)
        _deinterleave(kx, k_ref, dilation, base, BF16)
        _deinterleave(vx, v_ref, dilation, base, BF16)
        low, high = _head_masks(ATT_BLOCK)

        def unit(u, carry):
            _, has_prev, urow, crow, prow = _unit_rows(u, c, nb, base, pbase)
            q2 = _stack_heads(qd[pl.ds(urow, ATT_BLOCK), :] * scale, low, high)
            k2 = jnp.concatenate([kx[pl.ds(prow, ATT_BLOCK), :], kx[pl.ds(crow, ATT_BLOCK), :]], axis=0)
            v2 = jnp.concatenate([vx[pl.ds(prow, ATT_BLOCK), :], vx[pl.ds(crow, ATT_BLOCK), :]], axis=0)
            s = _unit_scores(q2, k2, b_ref, has_prev)
            mx = jnp.max(s, axis=-1, keepdims=True)
            e = jnp.exp(s - mx)
            den = jnp.sum(e, axis=-1, keepdims=True)
            acc = lax.dot_general(e.astype(BF16), v2, _NN, preferred_element_type=F32) / den
            lse = jnp.broadcast_to(mx + jnp.log(den), acc.shape)
            od[pl.ds(urow, ATT_BLOCK), :] = jnp.where(low, acc[:ATT_BLOCK], acc[ATT_BLOCK:])
            ld[pl.ds(urow, ATT_BLOCK), :] = jnp.where(low, lse[:ATT_BLOCK], lse[ATT_BLOCK:])
            return carry

        lax.fori_loop(0, ATT_UNITS, unit, 0, unroll=8)
        _interleave(o_ref, od, dilation)
        _interleave(l_ref, ld, dilation)

    blk = (ATT_CHUNK, LANES)
    bias_spec = pl.BlockSpec((2, ATT_BLOCK, 2 * ATT_BLOCK), lambda p, c: (p, 0, 0))
    out_spec = pl.BlockSpec(blk, lambda p, c: (c, p))
    return pl.pallas_call(
        body, name=name, grid=(pairs, nc),
        in_specs=[pl.BlockSpec(blk, lambda p, c: (c, p)),
                  pl.BlockSpec(blk, lambda p, c: (c, pairs + p)),
                  pl.BlockSpec(blk, lambda p, c: (c, 4 * pairs + p)),
                  bias_spec],
        out_specs=[out_spec, out_spec],
        out_shape=[_sds((t_len, ch), F32)] * 2,
        scratch_shapes=[pltpu.VMEM((ATT_CHUNK, LANES), F32),
                        pltpu.VMEM((2 * ATT_CHUNK, LANES), BF16), pltpu.VMEM((2 * ATT_CHUNK, LANES), BF16),
                        pltpu.VMEM((ATT_CHUNK, LANES), F32), pltpu.VMEM((ATT_CHUNK, LANES), F32)],
        compiler_params=_params("arbitrary", "arbitrary"),
    )(qk, qk, z, bias)


def attn_combine(outs, lses, name):
    t_len, ch = outs[0].shape
    tm = _row_tile(t_len, 512)

    def body(o1, o2, o3, l1, l2, l3, out_ref, outb_ref, lg_ref):
        a, b, c = l1[...], l2[...], l3[...]
        mx = jnp.maximum(jnp.maximum(a, b), c)
        tot = mx + jnp.log(jnp.exp(a - mx) + jnp.exp(b - mx) + jnp.exp(c - mx))
        val = jnp.exp(a - tot) * o1[...] + jnp.exp(b - tot) * o2[...] + jnp.exp(c - tot) * o3[...]
        out_ref[...] = val
        outb_ref[...] = val.astype(BF16)
        lg_ref[...] = tot

    row = pl.BlockSpec((tm, ch), lambda i: (i, 0))
    return pl.pallas_call(
        body, name=name, grid=(t_len // tm,),
        in_specs=[row] * 6, out_specs=[row] * 3,
        out_shape=[_sds((t_len, ch), F32), _sds((t_len, ch), BF16), _sds((t_len, ch), F32)],
        compiler_params=_params("parallel"),
    )(*outs, *lses)


def attn_bwd(qk, z, dy_cat, out, lg, dilation, ch, name):
    t_len = qk.shape[0]
    pairs = ch // LANES
    nc = t_len // ATT_CHUNK
    nb = ATT_UNITS // dilation
    scale = 1.0 / math.sqrt(HEAD_DIM)
    bias = _alibi_bias(2 * pairs, dilation)

    def body(q_ref, k_ref, v_ref, do_ref, out_ref, lg_ref, b_ref, dq_ref, dk_ref, dv_ref,
             qd, dod, lgd, dld, dl_nat, kx, vx, dkx, dvx, dqd):
        c = pl.program_id(1)
        slot = lax.rem(c, jnp.int32(2))
        base, pbase = slot * ATT_CHUNK, (1 - slot) * ATT_CHUNK

        @pl.when(c == 0)
        def _():
            for ref in (kx, vx, dkx, dvx):
                ref[...] = jnp.zeros_like(ref)

        @pl.when(c < nc)
        def _():
            low_all, _ = _head_masks(ATT_CHUNK)
            dl_nat[...] = _per_head_mean(do_ref[...] * out_ref[...], low_all) * float(HEAD_DIM)
            _deinterleave(qd, q_ref, dilation)
            _deinterleave(dod, do_ref, dilation)
            _deinterleave(lgd, lg_ref, dilation)
            _deinterleave(dld, dl_nat, dilation)
            _deinterleave(kx, k_ref, dilation, base, BF16)
            _deinterleave(vx, v_ref, dilation, base, BF16)
            cur = pl.ds(pl.multiple_of(base, ATT_CHUNK), ATT_CHUNK)
            dkx[cur, :] = jnp.zeros((ATT_CHUNK, LANES), F32)
            dvx[cur, :] = jnp.zeros((ATT_CHUNK, LANES), F32)
            low, high = _head_masks(ATT_BLOCK)

            def unit(u, carry):
                _, has_prev, urow, crow, prow = _unit_rows(u, c, nb, base, pbase)
                rows = pl.ds(urow, ATT_BLOCK)
                q2 = _stack_heads(qd[rows, :] * scale, low, high)
                do2 = _stack_heads(dod[rows, :], low, high)
                lse = _head_rows(lgd[rows, :], low, high)
                delta = _head_rows(dld[rows, :], low, high)
                k2 = jnp.concatenate([kx[pl.ds(prow, ATT_BLOCK), :], kx[pl.ds(crow, ATT_BLOCK), :]], axis=0)
                v2 = jnp.concatenate([vx[pl.ds(prow, ATT_BLOCK), :], vx[pl.ds(crow, ATT_BLOCK), :]], axis=0)
                prob = jnp.exp(_unit_scores(q2, k2, b_ref, has_prev) - lse)
                dp = lax.dot_general(do2, v2, _NT, preferred_element_type=F32)
                ds = (prob * (dp - delta)).astype(BF16)
                dq2 = lax.dot_general(ds, k2, _NN, preferred_element_type=F32)
                dk2 = lax.dot_general(ds, q2, _TN, preferred_element_type=F32)
                dv2 = lax.dot_general(prob.astype(BF16), do2, _TN, preferred_element_type=F32)
                dqd[rows, :] = scale * jnp.where(low, dq2[:ATT_BLOCK], dq2[ATT_BLOCK:])
                dkx[pl.ds(prow, ATT_BLOCK), :] += dk2[:ATT_BLOCK]
                dkx[pl.ds(crow, ATT_BLOCK), :] += dk2[ATT_BLOCK:]
                dvx[pl.ds(prow, ATT_BLOCK), :] += dv2[:ATT_BLOCK]
                dvx[pl.ds(crow, ATT_BLOCK), :] += dv2[ATT_BLOCK:]
                return carry

            lax.fori_loop(0, ATT_UNITS, unit, 0, unroll=8)
            _interleave(dq_ref, dqd, dilation)

        @pl.when(c > 0)
        def _():
            _interleave(dk_ref, dkx, dilation, pbase)
            _interleave(dv_ref, dvx, dilation, pbase)

    blk = (ATT_CHUNK, LANES)
    here = lambda c: jnp.minimum(c, nc - 1)
    spec = lambda off: pl.BlockSpec(blk, lambda p, c: (here(c), off + p))
    late = pl.BlockSpec(blk, lambda p, c: (jnp.maximum(c - 1, 0), p))
    bias_spec = pl.BlockSpec((2, ATT_BLOCK, 2 * ATT_BLOCK), lambda p, c: (p, 0, 0))
    f32_chunk = pltpu.VMEM((ATT_CHUNK, LANES), F32)
    return pl.pallas_call(
        body, name=name, grid=(pairs, nc + 1),
        in_specs=[spec(0), spec(pairs), spec(4 * pairs), spec(pairs), spec(0), spec(0), bias_spec],
        out_specs=[spec(0), late, late],
        out_shape=[_sds((t_len, ch), F32)] * 3,
        scratch_shapes=[f32_chunk] * 5
                       + [pltpu.VMEM((2 * ATT_CHUNK, LANES), BF16)] * 2
                       + [pltpu.VMEM((2 * ATT_CHUNK, LANES), F32)] * 2 + [f32_chunk],
        compiler_params=_params("arbitrary", "arbitrary"),
    )(qk, qk, z, dy_cat, out, lg, bias)


def sum3_bf16(a, b, c, name):
    t_len, ch = a.shape
    tm = _row_tile(t_len, 512)

    def body(a_ref, b_ref, c_ref, o_ref):
        o_ref[...] = (a_ref[...] + b_ref[...] + c_ref[...]).astype(BF16)

    row = pl.BlockSpec((tm, ch), lambda i: (i, 0))
    return pl.pallas_call(
        body, name=name, grid=(t_len // tm,), in_specs=[row] * 3, out_specs=row,
        out_shape=_sds((t_len, ch), BF16), compiler_params=_params("parallel"),
    )(a, b, c)


def _blk3(rows, cols):
    return pl.BlockSpec((None, rows, cols), lambda j, t: (j, 0, 0))


def ffn_up(h, wgu, name, riders):
    t_len, d = h.shape
    n_blk, _, _, fj = wgu.shape

    def epilogue(accs, e_refs, o_refs, cols):
        gate, up = accs
        o_refs[0][:, cols] = gate.astype(BF16)
        o_refs[1][:, cols] = up.astype(BF16)
        o_refs[2][:, cols] = (gate * _sigmoid(gate) * up).astype(BF16)

    w_spec = lambda i: pl.BlockSpec((None, None, d, fj), lambda j, t: (j, i, 0, 0))
    act = lambda tm: pl.BlockSpec((None, tm, fj), lambda j, t: (j, t, 0))
    return mm_cols(name, h, [wgu, wgu], [w_spec(0), w_spec(1)], False, [], lambda tm: [],
                   [_sds((n_blk, t_len, fj), BF16)] * 3, lambda tm: [act(tm)] * 3, epilogue, n_blk, riders)


def ffn_down(act, wd, res, name, riders):
    n_blk, t_len, fj = act.shape
    d = wd.shape[2]
    return mm_reduce(name, [act], lambda tm: [pl.BlockSpec((None, tm, fj), lambda t, j: (j, t, 0))],
                     [wd], [pl.BlockSpec((None, fj, d), lambda t, j: (j, 0, 0))], False, res, 0.5,
                     t_len, d, n_blk, riders, tm_want=1024)


def ffn_bwd(ht, gate, up, act, wgu, wd, dyb, name):
    d, t_len = ht.shape
    n_blk, _, fj = act.shape

    def epilogue(accs, e_refs, o_refs, cols):
        d_act = 0.5 * accs[0]
        gv, uv = e_refs[0][:, cols].astype(F32), e_refs[1][:, cols].astype(F32)
        sg = _sigmoid(gv)
        o_refs[0][:, cols] = (d_act * uv * (sg * (1.0 + gv * (1.0 - sg)))).astype(BF16)
        o_refs[1][:, cols] = (d_act * gv * sg).astype(BF16)

    act_jt = lambda tm: pl.BlockSpec((None, tm, fj), lambda j, t: (j, t, 0))
    (d_gate, d_up), _ = mm_cols(name + "_dact", dyb, [wd], [_blk3(fj, d)], True, [gate, up],
                                lambda tm: [act_jt(tm)] * 2, [_sds((n_blk, t_len, fj), BF16)] * 2,
                                lambda tm: [act_jt(tm)] * 2, epilogue, n_blk)

    ht_spec = lambda tt: pl.BlockSpec((d, tt), lambda j, t: (0, t))
    (d_wg,), _ = mm_tn(name + "_dwg", ht, ht_spec, [d_gate], lambda tt: [act_jt(tt)],
                       [_sds((n_blk, d, fj), BF16)], [_blk3(d, fj)], 1.0, t_len, n_blk, x_transposed=True)
    (d_wu,), (recv_wg,) = mm_tn(name + "_dwu", ht, ht_spec, [d_up], lambda tt: [act_jt(tt)],
                                [_sds((n_blk, d, fj), BF16)], [_blk3(d, fj)], 1.0, t_len, n_blk,
                                [Rider("scatter", d_wg)], x_transposed=True)

    (d_wd,), (recv_wu,) = mm_tn(name + "_dwd", act, act_jt, [dyb],
                                lambda tt: [pl.BlockSpec((tt, d), lambda j, t: (t, 0))],
                                [_sds((n_blk, fj, d), BF16)], [_blk3(fj, d)], 0.5, t_len, n_blk,
                                [Rider("scatter", d_wu)])

    act_tj = lambda tm: pl.BlockSpec((None, tm, fj), lambda t, j: (j, t, 0))
    w_in = lambda i: pl.BlockSpec((None, None, d, fj), lambda t, j: (j, i, 0, 0))
    dh, (recv_wd,) = mm_reduce(
        name + "_dh", [d_gate, d_up], lambda tm: [act_tj(tm)] * 2,
        [wgu, wgu], [w_in(0), w_in(1)], True, None, 1.0, t_len, d, n_blk,
        [Rider("scatter", d_wd)], tm_want=1024)
    return dh, recv_wg, recv_wu, recv_wd


def local_step(x, target, g1, wgu1, wd1_s, gmix, win_s, conv_w, conv_b, ln_g, ln_b, gq, gk, wout_s, g3,
               wgu2_s, wd2_s):
    t_len, d = x.shape
    ch = d // 2
    ij = win_s.shape[1]
    oj = wout_s.shape[0]
    n_blk = N_DEV

    h1, h1t = rms_fwd(x, g1, "rms1")
    (gate1, up1, act1), (wd1, wgu2) = ffn_up(h1, wgu1, "ffn1_up",
                                             [Rider("gather", wd1_s), Rider("gather", wgu2_s)])
    x1, (win, wout) = ffn_down(act1, wd1, x, "ffn1_down", [Rider("gather", win_s), Rider("gather", wout_s)])

    h2, h2t = rms_fwd(x1, gmix, "rms_mix")

    def store_f32(accs, e_refs, o_refs, cols):
        o_refs[0][:, cols] = accs[0]

    (z,), (wd2,) = mm_cols(
        "w_in", h2, [win], [_blk3(d, ij)], False, [], lambda tm: [],
        [_sds((t_len, n_blk * ij), F32)],
        lambda tm: [pl.BlockSpec((tm, ij), lambda j, t: (t, j))], store_f32, n_blk,
        [Rider("gather", wd2_s)])

    conv_w32 = jnp.pad(conv_w, ((0, 32 - CONV_WIDTH), (0, 0)))
    y_conv, conv_pre = conv_fwd(z, conv_w32, conv_b, ln_g, ln_b, "conv_fwd")

    g2 = jnp.concatenate([jnp.tile(gq, (1, ch // HEAD_DIM)), jnp.tile(gk, (1, ch // HEAD_DIM))], axis=1)
    qk = qk_norm_fwd(z, g2, ch, "qk_norm")
    branch = [attn_fwd(qk, z, dil, ch, "attn_fwd_d%d" % dil) for dil in DILATIONS]
    att, att_b, lg = attn_combine([o for o, _ in branch], [l for _, l in branch], "attn_combine")

    y_cat = jnp.concatenate([y_conv, att_b], axis=1)
    wout_full = wout.reshape(1, n_blk * oj, d)
    x2, _ = mm_reduce(
        "w_out", [y_cat], lambda tm: [pl.BlockSpec((tm, n_blk * oj), lambda t, j: (t, 0))],
        [wout_full], [pl.BlockSpec((None, n_blk * oj, d), lambda t, j: (0, 0, 0))], False, x1, 1.0,
        t_len, d, 1)

    h3, h3t = rms_fwd(x2, g3, "rms3")
    (gate2, up2, act2), _ = ffn_up(h3, wgu2, "ffn2_up", [])
    y, _ = ffn_down(act2, wd2, x2, "ffn2_down", [])

    loss_tile, dy, dyb = loss_head(y, target, "loss")

    dh3, recv_wg2, recv_wu2, recv_wd2 = ffn_bwd(h3t, gate2, up2, act2, wgu2, wd2, dyb, "ffn2")
    dx2, dx2b, d_g3 = rms_bwd(x2, g3, dh3, dy, "rms3_bwd")

    (dy_cat,), _ = mm_cols("w_out_dy", dx2b, [wout_full], [_blk3(n_blk * oj, d)], True, [], lambda tm: [],
                           [_sds((t_len, n_blk * oj), F32)],
                           lambda tm: [pl.BlockSpec((tm, n_blk * oj), lambda j, t: (t, 0))], store_f32, 1)
    (d_wout,), _ = mm_tn("w_out_dw", y_cat, lambda tt: pl.BlockSpec((tt, oj), lambda j, t: (t, j)),
                         [dx2b], lambda tt: [pl.BlockSpec((tt, d), lambda j, t: (t, 0))],
                         [_sds((n_blk, oj, d), BF16)], [_blk3(oj, d)], 1.0, t_len, n_blk)

    dc, d_lg, d_lb, d_cb = conv_bwd_norm(conv_pre, dy_cat, ln_g, ln_b, "conv_bwd_norm")
    dz_a, dz_g, d_cw8 = conv_bwd_taps(z, dc, conv_w32, "conv_bwd_taps")
    d_cw = jnp.sum(d_cw8, axis=1)[:CONV_WIDTH]

    grads = [attn_bwd(qk, z, dy_cat, att, lg, dil, ch, "attn_bwd_d%d" % dil) for dil in DILATIONS]
    gq_t = jnp.tile(gq, (1, LANES // HEAD_DIM))
    gk_t = jnp.tile(gk, (1, LANES // HEAD_DIM))
    dz_q, d_gq2 = qk_norm_bwd(z, gq_t, [g[0] for g in grads], 2 * ch // LANES, ch, "q_norm_bwd")
    dz_k, d_gk2 = qk_norm_bwd(z, gk_t, [g[1] for g in grads], 3 * ch // LANES, ch, "k_norm_bwd")
    d_gq = d_gq2[:, :HEAD_DIM] + d_gq2[:, HEAD_DIM:]
    d_gk = d_gk2[:, :HEAD_DIM] + d_gk2[:, HEAD_DIM:]
    dz_v = sum3_bf16(grads[0][2], grads[1][2], grads[2][2], "dv_sum")
    dzb = jnp.concatenate([dz_a, dz_g, dz_q, dz_k, dz_v], axis=1)

    (d_win,), (recv_wout,) = mm_tn(
        "w_in_dw", h2t, lambda tt: pl.BlockSpec((d, tt), lambda j, t: (0, t)),
        [dzb], lambda tt: [pl.BlockSpec((tt, ij), lambda j, t: (t, j))],
        [_sds((n_blk, d, ij), BF16)], [_blk3(d, ij)], 1.0, t_len, n_blk, [Rider("scatter", d_wout)],
        x_transposed=True)
    dh2, (recv_win,) = mm_reduce(
        "w_in_dh", [dzb], lambda tm: [pl.BlockSpec((tm, ij), lambda t, j: (t, j))],
        [win], [pl.BlockSpec((None, d, ij), lambda t, j: (j, 0, 0))], True, None, 1.0,
        t_len, d, n_blk, [Rider("scatter", d_win)], tm_want=1024)
    dx1, dx1b, d_gmix = rms_bwd(x1, gmix, dh2, dx2, "rms_mix_bwd")

    dh1, recv_wg1, recv_wu1, recv_wd1 = ffn_bwd(h1t, gate1, up1, act1, wgu1, wd1, dx1b, "ffn1")
    grad_x, _, d_g1 = rms_bwd(x, g1, dh1, dx1, "rms1_bwd")

    big = dict(ffn1_w_gate=recv_wg1, ffn1_w_up=recv_wu1, ffn1_w_down=recv_wd1, w_in=recv_win, w_out=recv_wout,
               ffn2_w_gate=recv_wg2, ffn2_w_up=recv_wu2, ffn2_w_down=recv_wd2)
    small = dict(g1=d_g1, gmix=d_gmix, g3=d_g3, conv_b=d_cb, ln_g=d_lg, ln_b=d_lb, gq=d_gq, gk=d_gk, conv_w=d_cw)
    return loss_tile[0, 0], grad_x, big, small


SMALL_ROWS = 48


def _pack_small(ch, g1, gmix, g3, conv_b, ln_g, ln_b, gq, gk, conv_w):
    pad_head = lambda v: jnp.pad(v, ((0, 0), (0, ch - v.shape[1])))
    rows = [g1.reshape(2, ch), gmix.reshape(2, ch), g3.reshape(2, ch), conv_b, ln_g, ln_b,
            pad_head(gq), pad_head(gk), conv_w]
    packed = jnp.concatenate(rows, axis=0)
    return jnp.pad(packed, ((0, SMALL_ROWS - packed.shape[0]), (0, 0)))


def _unpack_small(packed, d):
    return dict(g1=packed[0:2].reshape(1, d), gmix=packed[2:4].reshape(1, d), g3=packed[4:6].reshape(1, d),
                conv_b=packed[6:7], ln_g=packed[7:8], ln_b=packed[8:9],
                gq=packed[9:10, :HEAD_DIM], gk=packed[10:11, :HEAD_DIM])


def kernel(x, ffn1_norm_g, ffn1_w_gate, ffn1_w_up, ffn1_w_down, mix_norm_g, w_in, conv_w_dw, conv_b_dw, conv_ln_g, conv_ln_b, q_norm_g, k_norm_g, w_out, ffn2_norm_g, ffn2_w_gate, ffn2_w_up, ffn2_w_down, loss_target, m_ffn1_norm_g, m_ffn1_w_gate, m_ffn1_w_up, m_ffn1_w_down, m_mix_norm_g, m_w_in, m_conv_w_dw, m_conv_b_dw, m_conv_ln_g, m_conv_ln_b, m_q_norm_g, m_k_norm_g, m_w_out, m_ffn2_norm_g, m_ffn2_w_gate, m_ffn2_w_up, m_ffn2_w_down, v_ffn1_norm_g, v_ffn1_w_gate, v_ffn1_w_up, v_ffn1_w_down, v_mix_norm_g, v_w_in, v_conv_w_dw, v_conv_b_dw, v_conv_ln_g, v_conv_ln_b, v_q_norm_g, v_k_norm_g, v_w_out, v_ffn2_norm_g, v_ffn2_w_gate, v_ffn2_w_up, v_ffn2_w_down):
    d = x.shape[-1]
    ch = d // 2
    me = 4 * lax.axis_index("x") + 2 * lax.axis_index("y") + lax.axis_index("c")

    gu = lambda wg, wu: jnp.stack([wg[0], wu[0]]).astype(BF16)
    wgu1 = all_gather(gu(ffn1_w_gate, ffn1_w_up), "ag_wgu1")
    cw_all = all_gather(conv_w_dw[0], "ag_convw")
    conv_w = jnp.transpose(cw_all, (1, 0, 2)).reshape(CONV_WIDTH, ch)

    loss_part, grad_x, big, small = local_step(
        x[0], loss_target[0], ffn1_norm_g, wgu1, ffn1_w_down[0].astype(BF16), mix_norm_g, w_in[0].astype(BF16),
        conv_w, conv_b_dw, conv_ln_g, conv_ln_b, q_norm_g, k_norm_g, w_out[0].astype(BF16), ffn2_norm_g,
        gu(ffn2_w_gate, ffn2_w_up), ffn2_w_down[0].astype(BF16))
    loss = lax.psum(loss_part, MESH_AXES)

    state = dict(
        ffn1_w_gate=(ffn1_w_gate, m_ffn1_w_gate, v_ffn1_w_gate), ffn1_w_up=(ffn1_w_up, m_ffn1_w_up, v_ffn1_w_up),
        ffn1_w_down=(ffn1_w_down, m_ffn1_w_down, v_ffn1_w_down), w_in=(w_in, m_w_in, v_w_in),
        w_out=(w_out, m_w_out, v_w_out),
        ffn2_w_gate=(ffn2_w_gate, m_ffn2_w_gate, v_ffn2_w_gate), ffn2_w_up=(ffn2_w_up, m_ffn2_w_up, v_ffn2_w_up),
        ffn2_w_down=(ffn2_w_down, m_ffn2_w_down, v_ffn2_w_down))
    out = {}
    for pname, (w, m, v) in state.items():
        res = adamw(w[0], m[0], v[0], big[pname], "adamw_" + pname)
        out[pname] = [r[None] for r in res]

    zero_taps = jnp.zeros((CONV_WIDTH, ch), F32)
    pack = lambda g1, gm, g3, cb, lg, lb, gq, gk: _pack_small(ch, g1, gm, g3, cb, lg, lb, gq, gk, zero_taps)
    small_parts = all_gather(_pack_small(ch, **small), "ag_small_grads")
    s_res = adamw(
        pack(ffn1_norm_g, mix_norm_g, ffn2_norm_g, conv_b_dw, conv_ln_g, conv_ln_b, q_norm_g, k_norm_g),
        pack(m_ffn1_norm_g, m_mix_norm_g, m_ffn2_norm_g, m_conv_b_dw, m_conv_ln_g, m_conv_ln_b, m_q_norm_g, m_k_norm_g),
        pack(v_ffn1_norm_g, v_mix_norm_g, v_ffn2_norm_g, v_conv_b_dw, v_conv_ln_g, v_conv_ln_b, v_q_norm_g, v_k_norm_g),
        small_parts, "adamw_small")
    s_out = [_unpack_small(r, d) for r in s_res]
    names = dict(g1="ffn1_norm_g", gmix="mix_norm_g", g3="ffn2_norm_g", conv_b="conv_b_dw", ln_g="conv_ln_g",
                 ln_b="conv_ln_b", gq="q_norm_g", gk="k_norm_g")
    for key, full in names.items():
        out[full] = [r[key] for r in s_out]

    cshard = ch // N_DEV
    taps_sum = s_res[0][11:11 + CONV_WIDTH]
    taps_mine = lax.dynamic_slice(taps_sum, (0, me * cshard), (CONV_WIDTH, cshard))
    pad_taps = lambda a: jnp.pad(a, ((0, 32 - CONV_WIDTH), (0, 0)))
    c_res = adamw(pad_taps(conv_w_dw[0]), pad_taps(m_conv_w_dw[0]), pad_taps(v_conv_w_dw[0]),
                  pad_taps(taps_mine)[None], "adamw_convw")
    out["conv_w_dw"] = [r[:CONV_WIDTH][None] for r in c_res]

    order = ["ffn1_norm_g", "ffn1_w_gate", "ffn1_w_up", "ffn1_w_down", "mix_norm_g", "w_in", "conv_w_dw",
             "conv_b_dw", "conv_ln_g", "conv_ln_b", "q_norm_g", "k_norm_g", "w_out", "ffn2_norm_g",
             "ffn2_w_gate", "ffn2_w_up", "ffn2_w_down"]
    result = [loss, grad_x[None]]
    for kind in range(4):
        result += [out[n][kind] for n in order]
    return tuple(result)
```

```python
import math
from typing import NamedTuple

import jax
import jax.numpy as jnp
from jax import lax
from jax.experimental import pallas as pl
from jax.experimental.pallas import tpu as pltpu

F32 = jnp.float32
BF16 = jnp.bfloat16

N_DEV = 8
EPS = 1e-6
HEAD_DIM = 64
LANES = 128
MXU_WIDTH = 256
CONV_WIDTH = 31
SUBLANES = 8
HALO = 32
ROW_CHUNK = 32
ATT_BLOCK = 128
DILATIONS = (1, 4, 16)
ATT_UNITS = 16
ATT_CHUNK = ATT_UNITS * ATT_BLOCK
ALIBI_MAX_BIAS = 8.0
MASKED = -1e30
VMEM_LIMIT = 56 * 1024 * 1024

ADAM_LR = 0.001
ADAM_B1 = 0.9
ADAM_B2 = 0.999
ADAM_EPS = 1e-08
ADAM_WD = 0.01
ADAM_STEP = 10

MESH_AXES = ("x", "y", "c")
ANY = pl.BlockSpec(memory_space=pl.ANY)


def _sds(shape, dtype):
    return jax.ShapeDtypeStruct(tuple(shape), dtype)


def _params(*sem):
    return pltpu.CompilerParams(dimension_semantics=sem, vmem_limit_bytes=VMEM_LIMIT)


def _sigmoid(v):
    return 1.0 / (1.0 + jnp.exp(-v))


def _row_tile(t, want):
    for cand in range(min(want, t) // 8 * 8, 0, -8):
        if t % cand == 0:
            return cand
    return t


def _mesh_pos():
    return lax.axis_index("x"), lax.axis_index("y"), lax.axis_index("c")


def _comm_sems():
    return [pltpu.SemaphoreType.DMA((7,)), pltpu.SemaphoreType.DMA((7,)), pltpu.SemaphoreType.DMA(())]


def _gather_phases(x_ref, out_ref, send_sems, recv_sems, local_sem):
    x, y, c = _mesh_pos()
    me, sibling = (x, y, c), (x, y, 1 - c)
    chips = [(1 - x, y), (x, 1 - y), (1 - x, 1 - y)]

    def slot(px, py, pc):
        return out_ref.at[4 * px + 2 * py + pc]

    def copy(k, block, to, src=None):
        return pltpu.make_async_remote_copy(
            src_ref=slot(*block) if src is None else src, dst_ref=slot(*block),
            send_sem=send_sems.at[k], recv_sem=recv_sems.at[k],
            device_id=to, device_id_type=pl.DeviceIdType.MESH)

    mine = pltpu.make_async_copy(x_ref, slot(*me), local_sem)
    first = [copy(0, me, sibling, src=x_ref)]
    first += [copy(1 + j, me, (*chip, c), src=x_ref) for j, chip in enumerate(chips)]
    passed = [copy(4 + j, (*chip, c), sibling) for j, chip in enumerate(chips)]

    def start():
        mine.start()
        for cp in first:
            cp.start()

    def forward():
        for j, chip in enumerate(chips):
            copy(1 + j, (*chip, c), me).wait_recv()
            passed[j].start()

    def finish():
        copy(0, sibling, me).wait_recv()
        for j, chip in enumerate(chips):
            copy(4 + j, (*chip, 1 - c), me).wait_recv()
        for cp in first + passed:
            cp.wait_send()
        mine.wait()

    return start, forward, finish


def _scatter_phases(p_ref, out_ref, send_sems, recv_sems, local_sem):
    x, y, c = _mesh_pos()
    me = 4 * x + 2 * y + c
    flips = [(fx, fy, fc) for fx in (0, 1) for fy in (0, 1) for fc in (0, 1)][1:]

    def copy(k, flip, receiving):
        px, py, pc = (1 - x if flip[0] else x, 1 - y if flip[1] else y, 1 - c if flip[2] else c)
        them = 4 * px + 2 * py + pc
        return pltpu.make_async_remote_copy(
            src_ref=p_ref.at[them], dst_ref=out_ref.at[them if receiving else me],
            send_sem=send_sems.at[k], recv_sem=recv_sems.at[k],
            device_id=(px, py, pc), device_id_type=pl.DeviceIdType.MESH)

    mine = pltpu.make_async_copy(p_ref.at[me], out_ref.at[me], local_sem)

    def start():
        mine.start()
        for k, flip in enumerate(flips):
            copy(k, flip, False).start()

    def finish():
        for k, flip in enumerate(flips):
            copy(k, flip, True).wait_recv()
            copy(k, flip, False).wait_send()
        mine.wait()

    return start, None, finish


class Rider(NamedTuple):
    kind: str
    src: jax.Array

    def out_shape(self):
        shape = (N_DEV,) + self.src.shape if self.kind == "gather" else self.src.shape
        return _sds(shape, self.src.dtype)


def _rider_hooks(riders, in_refs, out_refs, sem_refs, step, n_steps):
    phases = [(_gather_phases if r.kind == "gather" else _scatter_phases)(
                  in_refs[i], out_refs[i], *sem_refs[3 * i:3 * i + 3]) for i, r in enumerate(riders)]

    def begin():
        for start, forward, _ in phases:
            pl.when(step == 0)(start)
            if forward is not None:
                pl.when(step == (7 * n_steps) // 8)(forward)

    def end():
        for _, _, finish in phases:
            pl.when(step == n_steps - 1)(finish)

    return begin, end


def _split_refs(refs, n_in, n_out, n_scratch, n_riders):
    pos, parts = 0, []
    for n in (n_in, n_riders, n_out, n_riders, n_scratch, 3 * n_riders):
        parts.append(refs[pos:pos + n])
        pos += n
    return parts


def all_gather(shard, name):
    def body(x_ref, out_ref, send_sems, recv_sems, local_sem):
        start, forward, finish = _gather_phases(x_ref, out_ref, send_sems, recv_sems, local_sem)
        start()
        forward()
        finish()

    return pl.pallas_call(
        body, name=name, out_shape=_sds((N_DEV,) + shard.shape, shard.dtype),
        in_specs=[ANY], out_specs=ANY, scratch_shapes=_comm_sems(),
    )(shard)


def adamw(w, m, v, parts, name):
    n_parts, rows, cols = parts.shape
    tr = _row_tile(rows, 128)
    c1 = 1.0 - ADAM_B1 ** ADAM_STEP
    c2 = 1.0 - ADAM_B2 ** ADAM_STEP

    def body(w_ref, m_ref, v_ref, p_ref, g_ref, d_ref, nm_ref, nv_ref):
        g = p_ref[0].astype(F32)
        for s in range(1, n_parts):
            g = g + p_ref[s].astype(F32)
        nm = ADAM_B1 * m_ref[...] + (1.0 - ADAM_B1) * g
        nv = ADAM_B2 * v_ref[...] + (1.0 - ADAM_B2) * (g * g)
        delta = -ADAM_LR * ((nm / c1) / (jnp.sqrt(nv / c2) + ADAM_EPS) + ADAM_WD * w_ref[...])
        g_ref[...] = g
        d_ref[...] = delta
        nm_ref[...] = nm
        nv_ref[...] = nv

    mat = pl.BlockSpec((tr, cols), lambda i: (i, 0))
    return pl.pallas_call(
        body, name=name, grid=(rows // tr,),
        in_specs=[mat, mat, mat, pl.BlockSpec((n_parts, tr, cols), lambda i: (0, i, 0))],
        out_specs=[mat, mat, mat, mat],
        out_shape=[_sds((rows, cols), F32)] * 4,
        compiler_params=_params("parallel"),
    )(w, m, v, parts)


_NN = (((1,), (0,)), ((), ()))
_NT = (((1,), (1,)), ((), ()))
_TN = (((0,), (0,)), ((), ()))


def mm_cols(name, a, b_list, b_specs, nt, extras, extra_specs, out_shapes, out_specs, epilogue, n_blk, riders=(),
            tm_want=512):
    t_len, k_len = a.shape
    tm = _row_tile(t_len, tm_want)
    nb, ne, n_out, nr = len(b_list), len(extras), len(out_shapes), len(riders)
    t_steps = t_len // tm
    n_cols = b_specs[0].block_shape[-2 if nt else -1]

    def body(*refs):
        ins, r_in, outs, r_out, _, r_sem = _split_refs(refs, 1 + nb + ne, n_out, 0, nr)
        step = pl.program_id(0) * t_steps + pl.program_id(1)
        begin, end = _rider_hooks(riders, r_in, r_out, r_sem, step, n_blk * t_steps)
        begin()
        av = ins[0][...]
        for c0 in range(0, n_cols, MXU_WIDTH):
            cols = slice(c0, min(c0 + MXU_WIDTH, n_cols))
            accs = [lax.dot_general(av, br[cols, :] if nt else br[:, cols], _NT if nt else _NN,
                                    preferred_element_type=F32) for br in ins[1:1 + nb]]
            epilogue(accs, ins[1 + nb:], outs, cols)
        end()

    res = pl.pallas_call(
        body, name=name, grid=(n_blk, t_steps),
        in_specs=([pl.BlockSpec((tm, k_len), lambda j, t: (t, 0))] + list(b_specs) + list(extra_specs(tm))
                  + [ANY] * nr),
        out_specs=list(out_specs(tm)) + [ANY] * nr,
        out_shape=list(out_shapes) + [r.out_shape() for r in riders],
        scratch_shapes=_comm_sems() * nr,
        compiler_params=_params("arbitrary", "arbitrary"),
    )(a, *b_list, *extras, *[r.src for r in riders])
    return res[:n_out], res[n_out:]


def mm_reduce(name, a_list, a_specs, b_list, b_specs, nt, res, scale, t_len, n_len, n_blk, riders=(), tm_want=512):
    tm = _row_tile(t_len, tm_want)
    na, nr = len(a_list), len(riders)
    has_res = res is not None
    t_steps = t_len // tm

    def body(*refs):
        ins, r_in, outs, r_out, _, r_sem = _split_refs(refs, 2 * na + has_res, 1, 0, nr)
        o_ref = outs[0]
        j = pl.program_id(1)
        step = pl.program_id(0) * n_blk + j
        begin, end = _rider_hooks(riders, r_in, r_out, r_sem, step, t_steps * n_blk)
        begin()

        part = None
        for ar, br in zip(ins[:na], ins[na:2 * na]):
            d = lax.dot_general(ar[...], br[...], _NT if nt else _NN, preferred_element_type=F32)
            part = d if part is None else part + d

        @pl.when(j == 0)
        def _():
            o_ref[...] = part

        @pl.when(j > 0)
        def _():
            o_ref[...] += part

        if has_res or scale != 1.0:
            @pl.when(j == n_blk - 1)
            def _():
                val = o_ref[...] * scale if scale != 1.0 else o_ref[...]
                o_ref[...] = ins[2 * na][...] + val if has_res else val

        end()

    row = pl.BlockSpec((tm, n_len), lambda t, j: (t, 0))
    out = pl.pallas_call(
        body, name=name, grid=(t_steps, n_blk),
        in_specs=list(a_specs(tm)) + list(b_specs) + ([row] if has_res else []) + [ANY] * nr,
        out_specs=[row] + [ANY] * nr,
        out_shape=[_sds((t_len, n_len), F32)] + [r.out_shape() for r in riders],
        scratch_shapes=_comm_sems() * nr,
        compiler_params=_params("arbitrary", "arbitrary"),
    )(*a_list, *b_list, *([res] if has_res else []), *[r.src for r in riders])
    return out[0], out[1:]


def mm_tn(name, x, x_spec, dy_list, dy_specs, out_shapes, out_specs, scale, t_len, n_blk, riders=(),
          x_transposed=False):
    tt = _row_tile(t_len, 2048)
    nd, nr = len(dy_list), len(riders)
    t_steps = t_len // tt
    acc_shapes = [pltpu.VMEM(spec.block_shape[-2:], F32) for spec in out_specs]

    def body(*refs):
        ins, r_in, outs, r_out, accs, r_sem = _split_refs(refs, 1 + nd, nd, nd, nr)
        t = pl.program_id(1)
        step = pl.program_id(0) * t_steps + t
        begin, end = _rider_hooks(riders, r_in, r_out, r_sem, step, n_blk * t_steps)
        begin()
        xv = ins[0][...]
        for dr, acc in zip(ins[1:], accs):
            d = lax.dot_general(xv, dr[...], _NN if x_transposed else _TN, preferred_element_type=F32)

            @pl.when(t == 0)
            def _():
                acc[...] = d

            @pl.when(t > 0)
            def _():
                acc[...] += d

        @pl.when(t == t_steps - 1)
        def _():
            for acc, orf in zip(accs, outs):
                val = acc[...] * scale if scale != 1.0 else acc[...]
                orf[...] = val.astype(orf.dtype)

        end()

    res = pl.pallas_call(
        body, name=name, grid=(n_blk, t_steps),
        in_specs=[x_spec(tt)] + list(dy_specs(tt)) + [ANY] * nr,
        out_specs=list(out_specs) + [ANY] * nr,
        out_shape=list(out_shapes) + [r.out_shape() for r in riders],
        scratch_shapes=acc_shapes + _comm_sems() * nr,
        compiler_params=_params("arbitrary", "arbitrary"),
    )(x, *dy_list, *[r.src for r in riders])
    return res[:nd], res[nd:]


def rms_fwd(x, g, name):
    t_len, d = x.shape
    tm = _row_tile(t_len, 512)

    def body(x_ref, g_ref, h_ref, ht_ref):
        xv = x_ref[...]
        r = lax.rsqrt(jnp.mean(xv * xv, axis=-1, keepdims=True) + EPS)
        hv = xv * r * g_ref[...]
        h_ref[...] = hv.astype(BF16)
        ht_ref[...] = hv.T.astype(BF16)

    row = pl.BlockSpec((tm, d), lambda i: (i, 0))
    return pl.pallas_call(
        body, name=name, grid=(t_len // tm,),
        in_specs=[row, pl.BlockSpec((1, d), lambda i: (0, 0))],
        out_specs=[row, pl.BlockSpec((d, tm), lambda i: (0, i))],
        out_shape=[_sds((t_len, d), BF16), _sds((d, t_len), BF16)],
        compiler_params=_params("parallel"),
    )(x, g)


def rms_bwd(x, g, dh, dres, name):
    t_len, d = x.shape
    tm = _row_tile(t_len, 512)

    def body(x_ref, g_ref, dh_ref, dr_ref, dx_ref, dxb_ref, dg_ref):
        i = pl.program_id(0)
        xv = x_ref[...]
        r = lax.rsqrt(jnp.mean(xv * xv, axis=-1, keepdims=True) + EPS)
        xh = xv * r
        dhv = dh_ref[...]

        @pl.when(i == 0)
        def _():
            dg_ref[...] = jnp.zeros_like(dg_ref)

        dg_ref[...] += jnp.sum(dhv * xh, axis=0, keepdims=True)
        dxh = dhv * g_ref[...]
        dx = dr_ref[...] + r * (dxh - xh * jnp.mean(dxh * xh, axis=-1, keepdims=True))
        dx_ref[...] = dx
        dxb_ref[...] = dx.astype(BF16)

    row = pl.BlockSpec((tm, d), lambda i: (i, 0))
    vec = pl.BlockSpec((1, d), lambda i: (0, 0))
    return pl.pallas_call(
        body, name=name, grid=(t_len // tm,),
        in_specs=[row, vec, row, row],
        out_specs=[row, row, vec],
        out_shape=[_sds((t_len, d), F32), _sds((t_len, d), BF16), _sds((1, d), F32)],
        compiler_params=_params("arbitrary"),
    )(x, g, dh, dres)


def loss_head(y, target, name):
    t_len, d = y.shape
    tm = _row_tile(t_len, 512)

    def body(y_ref, t_ref, l_ref, dy_ref, dyb_ref):
        i = pl.program_id(0)
        err = y_ref[...] - t_ref[...]

        @pl.when(i == 0)
        def _():
            l_ref[...] = jnp.zeros_like(l_ref)

        rows = jnp.sum(err * err, axis=-1, keepdims=True) * (1.0 / d)
        l_ref[...] += 0.5 * jnp.sum(rows, axis=0, keepdims=True)
        dy = err * (1.0 / d)
        dy_ref[...] = dy
        dyb_ref[...] = dy.astype(BF16)

    row = pl.BlockSpec((tm, d), lambda i: (i, 0))
    return pl.pallas_call(
        body, name=name, grid=(t_len // tm,),
        in_specs=[row, row],
        out_specs=[pl.BlockSpec((8, LANES), lambda i: (0, 0)), row, row],
        out_shape=[_sds((8, LANES), F32), _sds((t_len, d), F32), _sds((t_len, d), BF16)],
        compiler_params=_params("arbitrary"),
    )(y, target)


def _conv_specs(tm, ch):
    per = tm // HALO
    cur = lambda cb: pl.BlockSpec((tm, ch), lambda i: (i, cb))
    prev = lambda cb: pl.BlockSpec((HALO, ch), lambda i: (jnp.maximum(i * per - 1, 0), cb))
    return [cur(0), cur(1), prev(0), prev(1)]


def _tap_scratch(rows, ch):
    return pltpu.VMEM((SUBLANES, rows + SUBLANES, ch), F32)


def _shifted_copies(buf, rows):
    buf[0, rows:rows + SUBLANES, :] = jnp.zeros((SUBLANES, buf.shape[2]), F32)
    for s in range(1, SUBLANES):
        buf[s, 0:rows, :] = buf[0, pl.ds(s, rows), :]


def _tap_rows(buf, off):
    shift = off % SUBLANES
    return buf[shift, off - shift:off - shift + ROW_CHUNK, :]


def _fill_glu(ext, a_ref, gt_ref, ap_ref, gp_ref, i, tm):
    vp = ap_ref[...] * _sigmoid(gp_ref[...])
    ext[0, 0:HALO, :] = jnp.where(i > 0, vp, 0.0)
    ext[0, HALO:HALO + tm, :] = a_ref[...] * _sigmoid(gt_ref[...])
    _shifted_copies(ext, HALO + tm)


def _conv_rows(ext, w_ref, b_ref, r0):
    acc = jnp.broadcast_to(b_ref[...], (ROW_CHUNK, b_ref.shape[1]))
    for k in range(CONV_WIDTH):
        acc = acc + w_ref[k:k + 1, :] * _tap_rows(ext, r0 + HALO - (CONV_WIDTH - 1) + k)
    return acc


def _layer_norm(yv):
    mu = jnp.mean(yv, axis=-1, keepdims=True)
    cen = yv - mu
    var = jnp.mean(cen * cen, axis=-1, keepdims=True)
    rstd = lax.rsqrt(var + EPS)
    return cen * rstd, rstd


def conv_fwd(z, w, b, lg, lb, name):
    t_len = z.shape[0]
    ch = w.shape[1]
    tm = _row_tile(t_len, 256)

    def body(a_ref, gt_ref, ap_ref, gp_ref, w_ref, b_ref, lg_ref, lb_ref, y_ref, pre_ref, ext):
        i = pl.program_id(0)
        _fill_glu(ext, a_ref, gt_ref, ap_ref, gp_ref, i, tm)
        for r0 in range(0, tm, ROW_CHUNK):
            pre = _conv_rows(ext, w_ref, b_ref, r0)
            pre_ref[r0:r0 + ROW_CHUNK, :] = pre
            xh, _ = _layer_norm(pre)
            u = xh * lg_ref[...] + lb_ref[...]
            y_ref[r0:r0 + ROW_CHUNK, :] = (u * _sigmoid(u)).astype(BF16)

    vec = pl.BlockSpec((1, ch), lambda i: (0, 0))
    row = pl.BlockSpec((tm, ch), lambda i: (i, 0))
    return pl.pallas_call(
        body, name=name, grid=(t_len // tm,),
        in_specs=_conv_specs(tm, ch) + [pl.BlockSpec((32, ch), lambda i: (0, 0)), vec, vec, vec],
        out_specs=[row, row],
        out_shape=[_sds((t_len, ch), BF16), _sds((t_len, ch), F32)],
        scratch_shapes=[_tap_scratch(HALO + tm, ch)],
        compiler_params=_params("parallel"),
    )(z, z, z, z, w, b, lg, lb)


def conv_bwd_norm(pre, dy_cat, lg, lb, name):
    t_len, ch = pre.shape
    tm = _row_tile(t_len, 256)

    def body(pre_ref, dy_ref, lg_ref, lb_ref, dc_ref, dlg_ref, dlb_ref, db_ref):
        i = pl.program_id(0)

        @pl.when(i == 0)
        def _():
            dlg_ref[...] = jnp.zeros_like(dlg_ref)
            dlb_ref[...] = jnp.zeros_like(dlb_ref)
            db_ref[...] = jnp.zeros_like(db_ref)

        for r0 in range(0, tm, ROW_CHUNK):
            xh, rstd = _layer_norm(pre_ref[r0:r0 + ROW_CHUNK, :])
            u = xh * lg_ref[...] + lb_ref[...]
            sg = _sigmoid(u)
            du = dy_ref[r0:r0 + ROW_CHUNK, :] * (sg * (1.0 + u * (1.0 - sg)))
            dlg_ref[...] += jnp.sum(du * xh, axis=0, keepdims=True)
            dlb_ref[...] += jnp.sum(du, axis=0, keepdims=True)
            dxh = du * lg_ref[...]
            dc = rstd * (dxh - jnp.mean(dxh, axis=-1, keepdims=True)
                         - xh * jnp.mean(dxh * xh, axis=-1, keepdims=True))
            db_ref[...] += jnp.sum(dc, axis=0, keepdims=True)
            dc_ref[r0:r0 + ROW_CHUNK, :] = dc

    vec = pl.BlockSpec((1, ch), lambda i: (0, 0))
    row = pl.BlockSpec((tm, ch), lambda i: (i, 0))
    return pl.pallas_call(
        body, name=name, grid=(t_len // tm,),
        in_specs=[row, row, vec, vec],
        out_specs=[row, vec, vec, vec],
        out_shape=[_sds((t_len, ch), F32)] + [_sds((1, ch), F32)] * 3,
        compiler_params=_params("arbitrary"),
    )(pre, dy_cat, lg, lb)


def conv_bwd_taps(z, dc, w, name):
    t_len = z.shape[0]
    ch = w.shape[1]
    tm = _row_tile(t_len, 256)
    per = tm // HALO
    n_tiles = t_len // tm
    last_halo = t_len // HALO - 1

    def body(a_ref, gt_ref, ap_ref, gp_ref, dc_ref, dn_ref, w_ref, dz_a_ref, dz_g_ref, dw_ref, ext, dext):
        i = pl.program_id(0)
        _fill_glu(ext, a_ref, gt_ref, ap_ref, gp_ref, i, tm)
        dext[0, 0:tm, :] = dc_ref[...]
        dext[0, tm:tm + HALO, :] = jnp.where(i < n_tiles - 1, dn_ref[...], 0.0)
        _shifted_copies(dext, tm + HALO)

        @pl.when(i == 0)
        def _():
            dw_ref[...] = jnp.zeros_like(dw_ref)

        for r0 in range(0, tm, ROW_CHUNK):
            dcv = dext[0, r0:r0 + ROW_CHUNK, :]
            dv = jnp.zeros((ROW_CHUNK, ch), F32)
            for k in range(CONV_WIDTH):
                dv = dv + w_ref[k:k + 1, :] * _tap_rows(dext, r0 + (CONV_WIDTH - 1) - k)
                prod = dcv * _tap_rows(ext, r0 + HALO - (CONV_WIDTH - 1) + k)
                fold = prod[0:8]
                for s in range(8, ROW_CHUNK, 8):
                    fold = fold + prod[s:s + 8]
                dw_ref[k] += fold
            av = a_ref[r0:r0 + ROW_CHUNK, :]
            sg = _sigmoid(gt_ref[r0:r0 + ROW_CHUNK, :])
            dz_a_ref[r0:r0 + ROW_CHUNK, :] = (dv * sg).astype(BF16)
            dz_g_ref[r0:r0 + ROW_CHUNK, :] = (dv * av * sg * (1.0 - sg)).astype(BF16)

    row = pl.BlockSpec((tm, ch), lambda i: (i, 0))
    nxt = pl.BlockSpec((HALO, ch), lambda i: (jnp.minimum((i + 1) * per, last_halo), 0))
    return pl.pallas_call(
        body, name=name, grid=(n_tiles,),
        in_specs=_conv_specs(tm, ch) + [row, nxt, pl.BlockSpec((32, ch), lambda i: (0, 0))],
        out_specs=[row, row, pl.BlockSpec((32, 8, ch), lambda i: (0, 0, 0))],
        out_shape=[_sds((t_len, ch), BF16), _sds((t_len, ch), BF16), _sds((32, 8, ch), F32)],
        scratch_shapes=[_tap_scratch(HALO + tm, ch), _tap_scratch(tm + HALO, ch)],
        compiler_params=_params("arbitrary"),
    )(z, z, z, z, dc, dc, w)


def _head_masks(rows):
    lane = lax.broadcasted_iota(jnp.int32, (rows, LANES), 1)
    low = lane < HEAD_DIM
    return low, jnp.logical_not(low)


def _per_head_mean(val, low):
    s_low = jnp.sum(jnp.where(low, val, 0.0), axis=-1, keepdims=True)
    s_high = jnp.sum(jnp.where(low, 0.0, val), axis=-1, keepdims=True)
    return jnp.where(low, s_low, s_high) * (1.0 / HEAD_DIM)


def qk_norm_fwd(z, g2, ch, name):
    t_len = z.shape[0]
    tm = _row_tile(t_len, 1024)
    n_col = 2 * ch // LANES
    z_off = 2 * ch // LANES

    def body(z_ref, g_ref, o_ref):
        low, _ = _head_masks(tm)
        xv = z_ref[...]
        r = lax.rsqrt(_per_head_mean(xv * xv, low) + EPS)
        o_ref[...] = xv * r * g_ref[...]

    return pl.pallas_call(
        body, name=name, grid=(t_len // tm, n_col),
        in_specs=[pl.BlockSpec((tm, LANES), lambda i, cb: (i, z_off + cb)),
                  pl.BlockSpec((1, LANES), lambda i, cb: (0, cb))],
        out_specs=pl.BlockSpec((tm, LANES), lambda i, cb: (i, cb)),
        out_shape=_sds((t_len, 2 * ch), F32),
        compiler_params=_params("parallel", "parallel"),
    )(z, g2)


def qk_norm_bwd(z, g, d_list, z_off, ch, name):
    t_len = z.shape[0]
    tm = _row_tile(t_len, 1024)
    n_col = ch // LANES
    nd = len(d_list)

    def body(*refs):
        z_ref, g_ref, d_refs = refs[0], refs[1], refs[2:2 + nd]
        dz_ref, dg_ref = refs[2 + nd], refs[3 + nd]
        first = jnp.logical_and(pl.program_id(0) == 0, pl.program_id(1) == 0)
        low, _ = _head_masks(tm)
        xv = z_ref[...]
        r = lax.rsqrt(_per_head_mean(xv * xv, low) + EPS)
        xh = xv * r
        dy = d_refs[0][...]
        for dr in d_refs[1:]:
            dy = dy + dr[...]

        @pl.when(first)
        def _():
            dg_ref[...] = jnp.zeros_like(dg_ref)

        dg_ref[...] += jnp.sum(dy * xh, axis=0, keepdims=True)
        dxh = dy * g_ref[...]
        dz_ref[...] = (r * (dxh - xh * _per_head_mean(dxh * xh, low))).astype(BF16)

    blk = pl.BlockSpec((tm, LANES), lambda i, cb: (i, cb))
    return pl.pallas_call(
        body, name=name, grid=(t_len // tm, n_col),
        in_specs=[pl.BlockSpec((tm, LANES), lambda i, cb: (i, z_off + cb)),
                  pl.BlockSpec((1, LANES), lambda i, cb: (0, 0))] + [blk] * nd,
        out_specs=[blk, pl.BlockSpec((1, LANES), lambda i, cb: (0, 0))],
        out_shape=[_sds((t_len, ch), BF16), _sds((1, LANES), F32)],
        compiler_params=_params("arbitrary", "arbitrary"),
    )(z, g, *d_list)


def _alibi_bias(n_heads, dilation):
    slopes = 2.0 ** (-ALIBI_MAX_BIAS * jnp.arange(1, n_heads + 1, dtype=F32) / n_heads)
    qi = jnp.arange(ATT_BLOCK)[:, None]
    kj = jnp.arange(ATT_BLOCK)[None, :]
    dist_cur = (qi - kj).astype(F32)
    dist_prev = (ATT_BLOCK + qi - kj).astype(F32)
    cur = jnp.where((qi >= kj)[None], -slopes[:, None, None] * (dilation * dist_cur)[None], MASKED)
    prev = jnp.where((kj >= qi)[None], -slopes[:, None, None] * (dilation * dist_prev)[None], MASKED)
    return jnp.concatenate([prev, cur], axis=-1).astype(F32)


def _stack_heads(val, low, high):
    return jnp.concatenate([jnp.where(low, val, 0.0), jnp.where(high, val, 0.0)], axis=0).astype(BF16)


def _head_rows(val, low, high):
    return jnp.concatenate([jnp.max(jnp.where(low, val, MASKED), axis=-1, keepdims=True),
                            jnp.max(jnp.where(high, val, MASKED), axis=-1, keepdims=True)], axis=0)


def _unit_scores(q2, k2, b_ref, has_prev):
    s = lax.dot_general(q2, k2, _NT, preferred_element_type=F32)
    s = s + b_ref[...].reshape(2 * ATT_BLOCK, 2 * ATT_BLOCK)
    penalty = jnp.where(has_prev, 0.0, MASKED)
    return jnp.concatenate([s[:, :ATT_BLOCK] + penalty, s[:, ATT_BLOCK:]], axis=1)


def _strided_rows(r, dilation):
    per = ATT_CHUNK // dilation
    return pl.ds(r, per, stride=dilation) if dilation > 1 else pl.ds(0, per)


def _deinterleave(dst, src_ref, dilation, base=None, dtype=None):
    per = ATT_CHUNK // dilation
    for r in range(dilation):
        val = src_ref[_strided_rows(r, dilation), :]
        val = val if dtype is None else val.astype(dtype)
        if base is None:
            dst[r * per:(r + 1) * per, :] = val
        else:
            dst[pl.ds(pl.multiple_of(base + r * per, ATT_BLOCK), per), :] = val


def _unit_rows(u, c, nb, base, pbase):
    in_chunk = lax.rem(u, jnp.int32(nb)) > 0
    has_prev = jnp.logical_or(in_chunk, c > 0)
    urow = pl.multiple_of(u * ATT_BLOCK, ATT_BLOCK)
    crow = pl.multiple_of(base + u * ATT_BLOCK, ATT_BLOCK)
    prow = pl.multiple_of(jnp.where(in_chunk, base + (u - 1) * ATT_BLOCK,
                                    pbase + (u + nb - 1) * ATT_BLOCK), ATT_BLOCK)
    return in_chunk, has_prev, urow, crow, prow


def _interleave(dst_ref, src, dilation, base=None):
    per = ATT_CHUNK // dilation
    for r in range(dilation):
        if base is None:
            val = src[r * per:(r + 1) * per, :]
        else:
            val = src[pl.ds(pl.multiple_of(base + r * per, ATT_BLOCK), per), :]
        dst_ref[_strided_rows(r, dilation), :] = val


def attn_fwd(qk, z, dilation, ch, name):
    t_len = qk.shape[0]
    pairs = ch // LANES
    nc = t_len // ATT_CHUNK
    nb = ATT_UNITS // dilation
    scale = 1.0 / math.sqrt(HEAD_DIM)
    bias = _alibi_bias(2 * pairs, dilation)

    def body(q_ref, k_ref, v_ref, b_ref, o_ref, l_ref, qd, kx, vx, od, ld):
        c = pl.program_id(1)
        slot = lax.rem(c, jnp.int32(2))
        base, pbase = slot * ATT_CHUNK, (1 - slot) * ATT_CHUNK

        @pl.when(c == 0)
        def _():
            kx[...] = jnp.zeros_like(kx)
            vx[...] = jnp.zeros_like(vx)

        _deinterleave(qd, q_ref, dilation)
        _deinterleave(kx, k_ref, dilation, base, BF16)
        _deinterleave(vx, v_ref, dilation, base, BF16)
        low, high = _head_masks(ATT_BLOCK)

        def unit(u, carry):
            _, has_prev, urow, crow, prow = _unit_rows(u, c, nb, base, pbase)
            q2 = _stack_heads(qd[pl.ds(urow, ATT_BLOCK), :] * scale, low, high)
            k2 = jnp.concatenate([kx[pl.ds(prow, ATT_BLOCK), :], kx[pl.ds(crow, ATT_BLOCK), :]], axis=0)
            v2 = jnp.concatenate([vx[pl.ds(prow, ATT_BLOCK), :], vx[pl.ds(crow, ATT_BLOCK), :]], axis=0)
            s = _unit_scores(q2, k2, b_ref, has_prev)
            mx = jnp.max(s, axis=-1, keepdims=True)
            e = jnp.exp(s - mx)
            den = jnp.sum(e, axis=-1, keepdims=True)
            acc = lax.dot_general(e.astype(BF16), v2, _NN, preferred_element_type=F32) / den
            lse = jnp.broadcast_to(mx + jnp.log(den), acc.shape)
            od[pl.ds(urow, ATT_BLOCK), :] = jnp.where(low, acc[:ATT_BLOCK], acc[ATT_BLOCK:])
            ld[pl.ds(urow, ATT_BLOCK), :] = jnp.where(low, lse[:ATT_BLOCK], lse[ATT_BLOCK:])
            return carry

        lax.fori_loop(0, ATT_UNITS, unit, 0, unroll=8)
        _interleave(o_ref, od, dilation)
        _interleave(l_ref, ld, dilation)

    blk = (ATT_CHUNK, LANES)
    bias_spec = pl.BlockSpec((2, ATT_BLOCK, 2 * ATT_BLOCK), lambda p, c: (p, 0, 0))
    out_spec = pl.BlockSpec(blk, lambda p, c: (c, p))
    return pl.pallas_call(
        body, name=name, grid=(pairs, nc),
        in_specs=[pl.BlockSpec(blk, lambda p, c: (c, p)),
                  pl.BlockSpec(blk, lambda p, c: (c, pairs + p)),
                  pl.BlockSpec(blk, lambda p, c: (c, 4 * pairs + p)),
                  bias_spec],
        out_specs=[out_spec, out_spec],
        out_shape=[_sds((t_len, ch), F32)] * 2,
        scratch_shapes=[pltpu.VMEM((ATT_CHUNK, LANES), F32),
                        pltpu.VMEM((2 * ATT_CHUNK, LANES), BF16), pltpu.VMEM((2 * ATT_CHUNK, LANES), BF16),
                        pltpu.VMEM((ATT_CHUNK, LANES), F32), pltpu.VMEM((ATT_CHUNK, LANES), F32)],
        compiler_params=_params("arbitrary", "arbitrary"),
    )(qk, qk, z, bias)


def attn_combine(outs, lses, name):
    t_len, ch = outs[0].shape
    tm = _row_tile(t_len, 512)

    def body(o1, o2, o3, l1, l2, l3, out_ref, outb_ref, lg_ref):
        a, b, c = l1[...], l2[...], l3[...]
        mx = jnp.maximum(jnp.maximum(a, b), c)
        tot = mx + jnp.log(jnp.exp(a - mx) + jnp.exp(b - mx) + jnp.exp(c - mx))
        val = jnp.exp(a - tot) * o1[...] + jnp.exp(b - tot) * o2[...] + jnp.exp(c - tot) * o3[...]
        out_ref[...] = val
        outb_ref[...] = val.astype(BF16)
        lg_ref[...] = tot

    row = pl.BlockSpec((tm, ch), lambda i: (i, 0))
    return pl.pallas_call(
        body, name=name, grid=(t_len // tm,),
        in_specs=[row] * 6, out_specs=[row] * 3,
        out_shape=[_sds((t_len, ch), F32), _sds((t_len, ch), BF16), _sds((t_len, ch), F32)],
        compiler_params=_params("parallel"),
    )(*outs, *lses)


def attn_bwd(qk, z, dy_cat, out, lg, dilation, ch, name):
    t_len = qk.shape[0]
    pairs = ch // LANES
    nc = t_len // ATT_CHUNK
    nb = ATT_UNITS // dilation
    scale = 1.0 / math.sqrt(HEAD_DIM)
    bias = _alibi_bias(2 * pairs, dilation)

    def body(q_ref, k_ref, v_ref, do_ref, out_ref, lg_ref, b_ref, dq_ref, dk_ref, dv_ref,
             qd, dod, lgd, dld, dl_nat, kx, vx, dkx, dvx, dqd):
        c = pl.program_id(1)
        slot = lax.rem(c, jnp.int32(2))
        base, pbase = slot * ATT_CHUNK, (1 - slot) * ATT_CHUNK

        @pl.when(c == 0)
        def _():
            for ref in (kx, vx, dkx, dvx):
                ref[...] = jnp.zeros_like(ref)

        @pl.when(c < nc)
        def _():
            low_all, _ = _head_masks(ATT_CHUNK)
            dl_nat[...] = _per_head_mean(do_ref[...] * out_ref[...], low_all) * float(HEAD_DIM)
            _deinterleave(qd, q_ref, dilation)
            _deinterleave(dod, do_ref, dilation)
            _deinterleave(lgd, lg_ref, dilation)
            _deinterleave(dld, dl_nat, dilation)
            _deinterleave(kx, k_ref, dilation, base, BF16)
            _deinterleave(vx, v_ref, dilation, base, BF16)
            cur = pl.ds(pl.multiple_of(base, ATT_CHUNK), ATT_CHUNK)
            dkx[cur, :] = jnp.zeros((ATT_CHUNK, LANES), F32)
            dvx[cur, :] = jnp.zeros((ATT_CHUNK, LANES), F32)
            low, high = _head_masks(ATT_BLOCK)

            def unit(u, carry):
                _, has_prev, urow, crow, prow = _unit_rows(u, c, nb, base, pbase)
                rows = pl.ds(urow, ATT_BLOCK)
                q2 = _stack_heads(qd[rows, :] * scale, low, high)
                do2 = _stack_heads(dod[rows, :], low, high)
                lse = _head_rows(lgd[rows, :], low, high)
                delta = _head_rows(dld[rows, :], low, high)
                k2 = jnp.concatenate([kx[pl.ds(prow, ATT_BLOCK), :], kx[pl.ds(crow, ATT_BLOCK), :]], axis=0)
                v2 = jnp.concatenate([vx[pl.ds(prow, ATT_BLOCK), :], vx[pl.ds(crow, ATT_BLOCK), :]], axis=0)
                prob = jnp.exp(_unit_scores(q2, k2, b_ref, has_prev) - lse)
                dp = lax.dot_general(do2, v2, _NT, preferred_element_type=F32)
                ds = (prob * (dp - delta)).astype(BF16)
                dq2 = lax.dot_general(ds, k2, _NN, preferred_element_type=F32)
                dk2 = lax.dot_general(ds, q2, _TN, preferred_element_type=F32)
                dv2 = lax.dot_general(prob.astype(BF16), do2, _TN, preferred_element_type=F32)
                dqd[rows, :] = scale * jnp.where(low, dq2[:ATT_BLOCK], dq2[ATT_BLOCK:])
                dkx[pl.ds(prow, ATT_BLOCK), :] += dk2[:ATT_BLOCK]
                dkx[pl.ds(crow, ATT_BLOCK), :] += dk2[ATT_BLOCK:]
                dvx[pl.ds(prow, ATT_BLOCK), :] += dv2[:ATT_BLOCK]
                dvx[pl.ds(crow, ATT_BLOCK), :] += dv2[ATT_BLOCK:]
                return carry

            lax.fori_loop(0, ATT_UNITS, unit, 0, unroll=8)
            _interleave(dq_ref, dqd, dilation)

        @pl.when(c > 0)
        def _():
            _interleave(dk_ref, dkx, dilation, pbase)
            _interleave(dv_ref, dvx, dilation, pbase)

    blk = (ATT_CHUNK, LANES)
    here = lambda c: jnp.minimum(c, nc - 1)
    spec = lambda off: pl.BlockSpec(blk, lambda p, c: (here(c), off + p))
    late = pl.BlockSpec(blk, lambda p, c: (jnp.maximum(c - 1, 0), p))
    bias_spec = pl.BlockSpec((2, ATT_BLOCK, 2 * ATT_BLOCK), lambda p, c: (p, 0, 0))
    f32_chunk = pltpu.VMEM((ATT_CHUNK, LANES), F32)
    return pl.pallas_call(
        body, name=name, grid=(pairs, nc + 1),
        in_specs=[spec(0), spec(pairs), spec(4 * pairs), spec(pairs), spec(0), spec(0), bias_spec],
        out_specs=[spec(0), late, late],
        out_shape=[_sds((t_len, ch), F32)] * 3,
        scratch_shapes=[f32_chunk] * 5
                       + [pltpu.VMEM((2 * ATT_CHUNK, LANES), BF16)] * 2
                       + [pltpu.VMEM((2 * ATT_CHUNK, LANES), F32)] * 2 + [f32_chunk],
        compiler_params=_params("arbitrary", "arbitrary"),
    )(qk, qk, z, dy_cat, out, lg, bias)


def sum3_bf16(a, b, c, name):
    t_len, ch = a.shape
    tm = _row_tile(t_len, 512)

    def body(a_ref, b_ref, c_ref, o_ref):
        o_ref[...] = (a_ref[...] + b_ref[...] + c_ref[...]).astype(BF16)

    row = pl.BlockSpec((tm, ch), lambda i: (i, 0))
    return pl.pallas_call(
        body, name=name, grid=(t_len // tm,), in_specs=[row] * 3, out_specs=row,
        out_shape=_sds((t_len, ch), BF16), compiler_params=_params("parallel"),
    )(a, b, c)


def _blk3(rows, cols):
    return pl.BlockSpec((None, rows, cols), lambda j, t: (j, 0, 0))


def ffn_up(h, wg, wu, name, riders):
    t_len, d = h.shape
    n_blk, _, fj = wg.shape

    def epilogue(accs, e_refs, o_refs, cols):
        gate, up = accs
        o_refs[0][:, cols] = gate.astype(BF16)
        o_refs[1][:, cols] = up.astype(BF16)
        o_refs[2][:, cols] = (gate * _sigmoid(gate) * up).astype(BF16)

    act = lambda tm: pl.BlockSpec((None, tm, fj), lambda j, t: (j, t, 0))
    return mm_cols(name, h, [wg, wu], [_blk3(d, fj)] * 2, False, [], lambda tm: [],
                   [_sds((n_blk, t_len, fj), BF16)] * 3, lambda tm: [act(tm)] * 3, epilogue, n_blk, riders,
                   tm_want=1024)


def ffn_gate(h, wg, name, riders):
    t_len, d = h.shape
    n_blk, _, fj = wg.shape

    def epilogue(accs, e_refs, o_refs, cols):
        o_refs[0][:, cols] = accs[0].astype(BF16)

    act = lambda tm: pl.BlockSpec((None, tm, fj), lambda j, t: (j, t, 0))
    return mm_cols(name, h, [wg], [_blk3(d, fj)], False, [], lambda tm: [],
                   [_sds((n_blk, t_len, fj), BF16)], lambda tm: [act(tm)], epilogue, n_blk, riders, tm_want=1024)


def ffn_up_after_gate(h, wu, gate, name, riders):
    t_len, d = h.shape
    n_blk, _, fj = wu.shape

    def epilogue(accs, e_refs, o_refs, cols):
        gv = e_refs[0][:, cols].astype(F32)
        o_refs[0][:, cols] = accs[0].astype(BF16)
        o_refs[1][:, cols] = (gv * _sigmoid(gv) * accs[0]).astype(BF16)

    act = lambda tm: pl.BlockSpec((None, tm, fj), lambda j, t: (j, t, 0))
    return mm_cols(name, h, [wu], [_blk3(d, fj)], False, [gate], lambda tm: [act(tm)],
                   [_sds((n_blk, t_len, fj), BF16)] * 2, lambda tm: [act(tm)] * 2, epilogue, n_blk, riders,
                   tm_want=1024)


def ffn_down(act, wd, res, name, riders):
    n_blk, t_len, fj = act.shape
    d = wd.shape[2]
    return mm_reduce(name, [act], lambda tm: [pl.BlockSpec((None, tm, fj), lambda t, j: (j, t, 0))],
                     [wd], [pl.BlockSpec((None, fj, d), lambda t, j: (j, 0, 0))], False, res, 0.5,
                     t_len, d, n_blk, riders, tm_want=1024)


def ffn_bwd(ht, gate, up, act, wg, wu, wd, dyb, name):
    d, t_len = ht.shape
    n_blk, _, fj = act.shape

    def epilogue(accs, e_refs, o_refs, cols):
        d_act = 0.5 * accs[0]
        gv, uv = e_refs[0][:, cols].astype(F32), e_refs[1][:, cols].astype(F32)
        sg = _sigmoid(gv)
        o_refs[0][:, cols] = (d_act * uv * (sg * (1.0 + gv * (1.0 - sg)))).astype(BF16)
        o_refs[1][:, cols] = (d_act * gv * sg).astype(BF16)

    act_jt = lambda tm: pl.BlockSpec((None, tm, fj), lambda j, t: (j, t, 0))
    (d_gate, d_up), _ = mm_cols(name + "_dact", dyb, [wd], [_blk3(fj, d)], True, [gate, up],
                                lambda tm: [act_jt(tm)] * 2, [_sds((n_blk, t_len, fj), BF16)] * 2,
                                lambda tm: [act_jt(tm)] * 2, epilogue, n_blk, tm_want=1024)

    ht_spec = lambda tt: pl.BlockSpec((d, tt), lambda j, t: (0, t))
    (d_wg,), _ = mm_tn(name + "_dwg", ht, ht_spec, [d_gate], lambda tt: [act_jt(tt)],
                       [_sds((n_blk, d, fj), BF16)], [_blk3(d, fj)], 1.0, t_len, n_blk, x_transposed=True)
    (d_wu,), (recv_wg,) = mm_tn(name + "_dwu", ht, ht_spec, [d_up], lambda tt: [act_jt(tt)],
                                [_sds((n_blk, d, fj), BF16)], [_blk3(d, fj)], 1.0, t_len, n_blk,
                                [Rider("scatter", d_wg)], x_transposed=True)

    (d_wd,), (recv_wu,) = mm_tn(name + "_dwd", act, act_jt, [dyb],
                                lambda tt: [pl.BlockSpec((tt, d), lambda j, t: (t, 0))],
                                [_sds((n_blk, fj, d), BF16)], [_blk3(fj, d)], 0.5, t_len, n_blk,
                                [Rider("scatter", d_wu)])

    act_tj = lambda tm: pl.BlockSpec((None, tm, fj), lambda t, j: (j, t, 0))
    w_tj = pl.BlockSpec((None, d, fj), lambda t, j: (j, 0, 0))
    dh, (recv_wd,) = mm_reduce(
        name + "_dh", [d_gate, d_up], lambda tm: [act_tj(tm)] * 2,
        [wg, wu], [w_tj, w_tj], True, None, 1.0, t_len, d, n_blk,
        [Rider("scatter", d_wd)], tm_want=1024)
    return dh, recv_wg, recv_wu, recv_wd


def local_step(x, target, g1, wg1, wu1_s, wd1_s, gmix, win_s, conv_w, conv_b, ln_g, ln_b, gq, gk, wout_s, g3,
               wg2_s, wu2_s, wd2_s):
    t_len, d = x.shape
    ch = d // 2
    ij = win_s.shape[1]
    oj = wout_s.shape[0]
    n_blk = N_DEV

    h1, h1t = rms_fwd(x, g1, "rms1")
    (gate1,), (wu1,) = ffn_gate(h1, wg1, "ffn1_gate", [Rider("gather", wu1_s)])
    (up1, act1), (wd1, wg2) = ffn_up_after_gate(h1, wu1, gate1, "ffn1_up",
                                                [Rider("gather", wd1_s), Rider("gather", wg2_s)])
    x1, (win, wout, wu2) = ffn_down(act1, wd1, x, "ffn1_down",
                                    [Rider("gather", win_s), Rider("gather", wout_s), Rider("gather", wu2_s)])

    h2, h2t = rms_fwd(x1, gmix, "rms_mix")

    def store_f32(accs, e_refs, o_refs, cols):
        o_refs[0][:, cols] = accs[0]

    (z,), (wd2,) = mm_cols(
        "w_in", h2, [win], [_blk3(d, ij)], False, [], lambda tm: [],
        [_sds((t_len, n_blk * ij), F32)],
        lambda tm: [pl.BlockSpec((tm, ij), lambda j, t: (t, j))], store_f32, n_blk,
        [Rider("gather", wd2_s)], tm_want=2048)

    conv_w32 = jnp.pad(conv_w, ((0, 32 - CONV_WIDTH), (0, 0)))
    y_conv, conv_pre = conv_fwd(z, conv_w32, conv_b, ln_g, ln_b, "conv_fwd")

    g2 = jnp.concatenate([jnp.tile(gq, (1, ch // HEAD_DIM)), jnp.tile(gk, (1, ch // HEAD_DIM))], axis=1)
    qk = qk_norm_fwd(z, g2, ch, "qk_norm")
    branch = [attn_fwd(qk, z, dil, ch, "attn_fwd_d%d" % dil) for dil in DILATIONS]
    att, att_b, lg = attn_combine([o for o, _ in branch], [l for _, l in branch], "attn_combine")

    y_cat = jnp.concatenate([y_conv, att_b], axis=1)
    wout_full = wout.reshape(1, n_blk * oj, d)
    x2, _ = mm_reduce(
        "w_out", [y_cat], lambda tm: [pl.BlockSpec((tm, n_blk * oj), lambda t, j: (t, 0))],
        [wout_full], [pl.BlockSpec((None, n_blk * oj, d), lambda t, j: (0, 0, 0))], False, x1, 1.0,
        t_len, d, 1)

    h3, h3t = rms_fwd(x2, g3, "rms3")
    (gate2, up2, act2), _ = ffn_up(h3, wg2, wu2, "ffn2_up", [])
    y, _ = ffn_down(act2, wd2, x2, "ffn2_down", [])

    loss_tile, dy, dyb = loss_head(y, target, "loss")

    dh3, recv_wg2, recv_wu2, recv_wd2 = ffn_bwd(h3t, gate2, up2, act2, wg2, wu2, wd2, dyb, "ffn2")
    dx2, dx2b, d_g3 = rms_bwd(x2, g3, dh3, dy, "rms3_bwd")

    (dy_cat,), _ = mm_cols("w_out_dy", dx2b, [wout_full], [_blk3(n_blk * oj, d)], True, [], lambda tm: [],
                           [_sds((t_len, n_blk * oj), F32)],
                           lambda tm: [pl.BlockSpec((tm, n_blk * oj), lambda j, t: (t, 0))], store_f32, 1,
                           tm_want=1024)
    (d_wout,), _ = mm_tn("w_out_dw", y_cat, lambda tt: pl.BlockSpec((tt, oj), lambda j, t: (t, j)),
                         [dx2b], lambda tt: [pl.BlockSpec((tt, d), lambda j, t: (t, 0))],
                         [_sds((n_blk, oj, d), BF16)], [_blk3(oj, d)], 1.0, t_len, n_blk)

    dc, d_lg, d_lb, d_cb = conv_bwd_norm(conv_pre, dy_cat, ln_g, ln_b, "conv_bwd_norm")
    dz_a, dz_g, d_cw8 = conv_bwd_taps(z, dc, conv_w32, "conv_bwd_taps")
    d_cw = jnp.sum(d_cw8, axis=1)[:CONV_WIDTH]

    grads = [attn_bwd(qk, z, dy_cat, att, lg, dil, ch, "attn_bwd_d%d" % dil) for dil in DILATIONS]
    gq_t = jnp.tile(gq, (1, LANES // HEAD_DIM))
    gk_t = jnp.tile(gk, (1, LANES // HEAD_DIM))
    dz_q, d_gq2 = qk_norm_bwd(z, gq_t, [g[0] for g in grads], 2 * ch // LANES, ch, "q_norm_bwd")
    dz_k, d_gk2 = qk_norm_bwd(z, gk_t, [g[1] for g in grads], 3 * ch // LANES, ch, "k_norm_bwd")
    d_gq = d_gq2[:, :HEAD_DIM] + d_gq2[:, HEAD_DIM:]
    d_gk = d_gk2[:, :HEAD_DIM] + d_gk2[:, HEAD_DIM:]
    dz_v = sum3_bf16(grads[0][2], grads[1][2], grads[2][2], "dv_sum")
    dzb = jnp.concatenate([dz_a, dz_g, dz_q, dz_k, dz_v], axis=1)

    (d_win,), (recv_wout,) = mm_tn(
        "w_in_dw", h2t, lambda tt: pl.BlockSpec((d, tt), lambda j, t: (0, t)),
        [dzb], lambda tt: [pl.BlockSpec((tt, ij), lambda j, t: (t, j))],
        [_sds((n_blk, d, ij), BF16)], [_blk3(d, ij)], 1.0, t_len, n_blk, [Rider("scatter", d_wout)],
        x_transposed=True)
    dh2, (recv_win,) = mm_reduce(
        "w_in_dh", [dzb], lambda tm: [pl.BlockSpec((tm, ij), lambda t, j: (t, j))],
        [win], [pl.BlockSpec((None, d, ij), lambda t, j: (j, 0, 0))], True, None, 1.0,
        t_len, d, n_blk, [Rider("scatter", d_win)], tm_want=1024)
    dx1, dx1b, d_gmix = rms_bwd(x1, gmix, dh2, dx2, "rms_mix_bwd")

    dh1, recv_wg1, recv_wu1, recv_wd1 = ffn_bwd(h1t, gate1, up1, act1, wg1, wu1, wd1, dx1b, "ffn1")
    grad_x, _, d_g1 = rms_bwd(x, g1, dh1, dx1, "rms1_bwd")

    big = dict(ffn1_w_gate=recv_wg1, ffn1_w_up=recv_wu1, ffn1_w_down=recv_wd1, w_in=recv_win, w_out=recv_wout,
               ffn2_w_gate=recv_wg2, ffn2_w_up=recv_wu2, ffn2_w_down=recv_wd2)
    small = dict(g1=d_g1, gmix=d_gmix, g3=d_g3, conv_b=d_cb, ln_g=d_lg, ln_b=d_lb, gq=d_gq, gk=d_gk, conv_w=d_cw)
    return loss_tile[0, 0], grad_x, big, small


SMALL_ROWS = 48


def _pack_small(ch, g1, gmix, g3, conv_b, ln_g, ln_b, gq, gk, conv_w):
    pad_head = lambda v: jnp.pad(v, ((0, 0), (0, ch - v.shape[1])))
    rows = [g1.reshape(2, ch), gmix.reshape(2, ch), g3.reshape(2, ch), conv_b, ln_g, ln_b,
            pad_head(gq), pad_head(gk), conv_w]
    packed = jnp.concatenate(rows, axis=0)
    return jnp.pad(packed, ((0, SMALL_ROWS - packed.shape[0]), (0, 0)))


def _unpack_small(packed, d):
    return dict(g1=packed[0:2].reshape(1, d), gmix=packed[2:4].reshape(1, d), g3=packed[4:6].reshape(1, d),
                conv_b=packed[6:7], ln_g=packed[7:8], ln_b=packed[8:9],
                gq=packed[9:10, :HEAD_DIM], gk=packed[10:11, :HEAD_DIM])


def kernel(x, ffn1_norm_g, ffn1_w_gate, ffn1_w_up, ffn1_w_down, mix_norm_g, w_in, conv_w_dw, conv_b_dw, conv_ln_g, conv_ln_b, q_norm_g, k_norm_g, w_out, ffn2_norm_g, ffn2_w_gate, ffn2_w_up, ffn2_w_down, loss_target, m_ffn1_norm_g, m_ffn1_w_gate, m_ffn1_w_up, m_ffn1_w_down, m_mix_norm_g, m_w_in, m_conv_w_dw, m_conv_b_dw, m_conv_ln_g, m_conv_ln_b, m_q_norm_g, m_k_norm_g, m_w_out, m_ffn2_norm_g, m_ffn2_w_gate, m_ffn2_w_up, m_ffn2_w_down, v_ffn1_norm_g, v_ffn1_w_gate, v_ffn1_w_up, v_ffn1_w_down, v_mix_norm_g, v_w_in, v_conv_w_dw, v_conv_b_dw, v_conv_ln_g, v_conv_ln_b, v_q_norm_g, v_k_norm_g, v_w_out, v_ffn2_norm_g, v_ffn2_w_gate, v_ffn2_w_up, v_ffn2_w_down):
    d = x.shape[-1]
    ch = d // 2
    me = 4 * lax.axis_index("x") + 2 * lax.axis_index("y") + lax.axis_index("c")

    shard = lambda w: w[0].astype(BF16)
    wg1 = all_gather(shard(ffn1_w_gate), "ag_wg1")
    cw_all = all_gather(conv_w_dw[0], "ag_convw")
    conv_w = jnp.transpose(cw_all, (1, 0, 2)).reshape(CONV_WIDTH, ch)

    loss_part, grad_x, big, small = local_step(
        x[0], loss_target[0], ffn1_norm_g, wg1, shard(ffn1_w_up), shard(ffn1_w_down), mix_norm_g, shard(w_in),
        conv_w, conv_b_dw, conv_ln_g, conv_ln_b, q_norm_g, k_norm_g, shard(w_out), ffn2_norm_g,
        shard(ffn2_w_gate), shard(ffn2_w_up), shard(ffn2_w_down))
    loss = lax.psum(loss_part, MESH_AXES)

    state = dict(
        ffn1_w_gate=(ffn1_w_gate, m_ffn1_w_gate, v_ffn1_w_gate), ffn1_w_up=(ffn1_w_up, m_ffn1_w_up, v_ffn1_w_up),
        ffn1_w_down=(ffn1_w_down, m_ffn1_w_down, v_ffn1_w_down), w_in=(w_in, m_w_in, v_w_in),
        w_out=(w_out, m_w_out, v_w_out),
        ffn2_w_gate=(ffn2_w_gate, m_ffn2_w_gate, v_ffn2_w_gate), ffn2_w_up=(ffn2_w_up, m_ffn2_w_up, v_ffn2_w_up),
        ffn2_w_down=(ffn2_w_down, m_ffn2_w_down, v_ffn2_w_down))
    out = {}
    for pname, (w, m, v) in state.items():
        res = adamw(w[0], m[0], v[0], big[pname], "adamw_" + pname)
        out[pname] = [r[None] for r in res]

    zero_taps = jnp.zeros((CONV_WIDTH, ch), F32)
    pack = lambda g1, gm, g3, cb, lg, lb, gq, gk: _pack_small(ch, g1, gm, g3, cb, lg, lb, gq, gk, zero_taps)
    small_parts = all_gather(_pack_small(ch, **small), "ag_small_grads")
    s_res = adamw(
        pack(ffn1_norm_g, mix_norm_g, ffn2_norm_g, conv_b_dw, conv_ln_g, conv_ln_b, q_norm_g, k_norm_g),
        pack(m_ffn1_norm_g, m_mix_norm_g, m_ffn2_norm_g, m_conv_b_dw, m_conv_ln_g, m_conv_ln_b, m_q_norm_g, m_k_norm_g),
        pack(v_ffn1_norm_g, v_mix_norm_g, v_ffn2_norm_g, v_conv_b_dw, v_conv_ln_g, v_conv_ln_b, v_q_norm_g, v_k_norm_g),
        small_parts, "adamw_small")
    s_out = [_unpack_small(r, d) for r in s_res]
    names = dict(g1="ffn1_norm_g", gmix="mix_norm_g", g3="ffn2_norm_g", conv_b="conv_b_dw", ln_g="conv_ln_g",
                 ln_b="conv_ln_b", gq="q_norm_g", gk="k_norm_g")
    for key, full in names.items():
        out[full] = [r[key] for r in s_out]

    cshard = ch // N_DEV
    taps_sum = s_res[0][11:11 + CONV_WIDTH]
    taps_mine = lax.dynamic_slice(taps_sum, (0, me * cshard), (CONV_WIDTH, cshard))
    pad_taps = lambda a: jnp.pad(a, ((0, 32 - CONV_WIDTH), (0, 0)))
    c_res = adamw(pad_taps(conv_w_dw[0]), pad_taps(m_conv_w_dw[0]), pad_taps(v_conv_w_dw[0]),
                  pad_taps(taps_mine)[None], "adamw_convw")
    out["conv_w_dw"] = [r[:CONV_WIDTH][None] for r in c_res]

    order = ["ffn1_norm_g", "ffn1_w_gate", "ffn1_w_up", "ffn1_w_down", "mix_norm_g", "w_in", "conv_w_dw",
             "conv_b_dw", "conv_ln_g", "conv_ln_b", "q_norm_g", "k_norm_g", "w_out", "ffn2_norm_g",
             "ffn2_w_gate", "ffn2_w_up", "ffn2_w_down"]
    result = [loss, grad_x[None]]
    for kind in range(4):
        result += [out[n][kind] for n in order]
    return tuple(result)
```

```python
import math
from typing import NamedTuple

import jax
import jax.numpy as jnp
from jax import lax
from jax.experimental import pallas as pl
from jax.experimental.pallas import tpu as pltpu

F32 = jnp.float32
BF16 = jnp.bfloat16

N_DEV = 8
EPS = 1e-6
HEAD_DIM = 64
LANES = 128
MXU_WIDTH = 256
CONV_WIDTH = 31
SUBLANES = 8
HALO = 32
ROW_CHUNK = 32
ATT_BLOCK = 128
DILATIONS = (1, 4, 16)
ATT_UNITS = 16
ATT_CHUNK = ATT_UNITS * ATT_BLOCK
ALIBI_MAX_BIAS = 8.0
MASKED = -1e30
VMEM_LIMIT = 56 * 1024 * 1024

ADAM_LR = 0.001
ADAM_B1 = 0.9
ADAM_B2 = 0.999
ADAM_EPS = 1e-08
ADAM_WD = 0.01
ADAM_STEP = 10

MESH_AXES = ("x", "y", "c")
ANY = pl.BlockSpec(memory_space=pl.ANY)


def _sds(shape, dtype):
    return jax.ShapeDtypeStruct(tuple(shape), dtype)


def _params(*sem):
    return pltpu.CompilerParams(dimension_semantics=sem, vmem_limit_bytes=VMEM_LIMIT)


def _sigmoid(v):
    return 1.0 / (1.0 + jnp.exp(-v))


def _row_tile(t, want):
    for cand in range(min(want, t) // 8 * 8, 0, -8):
        if t % cand == 0:
            return cand
    return t


def _mesh_pos():
    return lax.axis_index("x"), lax.axis_index("y"), lax.axis_index("c")


def _comm_sems():
    return [pltpu.SemaphoreType.DMA((7,)), pltpu.SemaphoreType.DMA((7,)), pltpu.SemaphoreType.DMA(())]


def _gather_phases(x_ref, out_ref, send_sems, recv_sems, local_sem):
    x, y, c = _mesh_pos()
    me, sibling = (x, y, c), (x, y, 1 - c)
    chips = [(1 - x, y), (x, 1 - y), (1 - x, 1 - y)]

    def slot(px, py, pc):
        return out_ref.at[4 * px + 2 * py + pc]

    def copy(k, block, to, src=None):
        return pltpu.make_async_remote_copy(
            src_ref=slot(*block) if src is None else src, dst_ref=slot(*block),
            send_sem=send_sems.at[k], recv_sem=recv_sems.at[k],
            device_id=to, device_id_type=pl.DeviceIdType.MESH)

    mine = pltpu.make_async_copy(x_ref, slot(*me), local_sem)
    first = [copy(0, me, sibling, src=x_ref)]
    first += [copy(1 + j, me, (*chip, c), src=x_ref) for j, chip in enumerate(chips)]
    passed = [copy(4 + j, (*chip, c), sibling) for j, chip in enumerate(chips)]

    def start():
        mine.start()
        for cp in first:
            cp.start()

    def forward():
        for j, chip in enumerate(chips):
            copy(1 + j, (*chip, c), me).wait_recv()
            passed[j].start()

    def finish():
        copy(0, sibling, me).wait_recv()
        for j, chip in enumerate(chips):
            copy(4 + j, (*chip, 1 - c), me).wait_recv()
        for cp in first + passed:
            cp.wait_send()
        mine.wait()

    return start, forward, finish


def _scatter_phases(p_ref, out_ref, send_sems, recv_sems, local_sem):
    x, y, c = _mesh_pos()
    me = 4 * x + 2 * y + c
    flips = [(fx, fy, fc) for fx in (0, 1) for fy in (0, 1) for fc in (0, 1)][1:]

    def copy(k, flip, receiving):
        px, py, pc = (1 - x if flip[0] else x, 1 - y if flip[1] else y, 1 - c if flip[2] else c)
        them = 4 * px + 2 * py + pc
        return pltpu.make_async_remote_copy(
            src_ref=p_ref.at[them], dst_ref=out_ref.at[them if receiving else me],
            send_sem=send_sems.at[k], recv_sem=recv_sems.at[k],
            device_id=(px, py, pc), device_id_type=pl.DeviceIdType.MESH)

    mine = pltpu.make_async_copy(p_ref.at[me], out_ref.at[me], local_sem)

    def start():
        mine.start()
        for k, flip in enumerate(flips):
            copy(k, flip, False).start()

    def finish():
        for k, flip in enumerate(flips):
            copy(k, flip, True).wait_recv()
            copy(k, flip, False).wait_send()
        mine.wait()

    return start, None, finish


class Rider(NamedTuple):
    kind: str
    src: jax.Array

    def out_shape(self):
        shape = (N_DEV,) + self.src.shape if self.kind == "gather" else self.src.shape
        return _sds(shape, self.src.dtype)


def _rider_hooks(riders, in_refs, out_refs, sem_refs, step, n_steps):
    phases = [(_gather_phases if r.kind == "gather" else _scatter_phases)(
                  in_refs[i], out_refs[i], *sem_refs[3 * i:3 * i + 3]) for i, r in enumerate(riders)]

    def begin():
        for start, forward, _ in phases:
            pl.when(step == 0)(start)
            if forward is not None:
                pl.when(step == (7 * n_steps) // 8)(forward)

    def end():
        for _, _, finish in phases:
            pl.when(step == n_steps - 1)(finish)

    return begin, end


def _split_refs(refs, n_in, n_out, n_scratch, n_riders):
    pos, parts = 0, []
    for n in (n_in, n_riders, n_out, n_riders, n_scratch, 3 * n_riders):
        parts.append(refs[pos:pos + n])
        pos += n
    return parts


def all_gather(shard, name):
    def body(x_ref, out_ref, send_sems, recv_sems, local_sem):
        start, forward, finish = _gather_phases(x_ref, out_ref, send_sems, recv_sems, local_sem)
        start()
        forward()
        finish()

    return pl.pallas_call(
        body, name=name, out_shape=_sds((N_DEV,) + shard.shape, shard.dtype),
        in_specs=[ANY], out_specs=ANY, scratch_shapes=_comm_sems(),
    )(shard)


def adamw(w, m, v, parts, name):
    n_parts, rows, cols = parts.shape
    tr = _row_tile(rows, 128)
    c1 = 1.0 - ADAM_B1 ** ADAM_STEP
    c2 = 1.0 - ADAM_B2 ** ADAM_STEP

    def body(w_ref, m_ref, v_ref, p_ref, g_ref, d_ref, nm_ref, nv_ref):
        g = p_ref[0].astype(F32)
        for s in range(1, n_parts):
            g = g + p_ref[s].astype(F32)
        nm = ADAM_B1 * m_ref[...] + (1.0 - ADAM_B1) * g
        nv = ADAM_B2 * v_ref[...] + (1.0 - ADAM_B2) * (g * g)
        delta = -ADAM_LR * ((nm / c1) / (jnp.sqrt(nv / c2) + ADAM_EPS) + ADAM_WD * w_ref[...])
        g_ref[...] = g
        d_ref[...] = delta
        nm_ref[...] = nm
        nv_ref[...] = nv

    mat = pl.BlockSpec((tr, cols), lambda i: (i, 0))
    return pl.pallas_call(
        body, name=name, grid=(rows // tr,),
        in_specs=[mat, mat, mat, pl.BlockSpec((n_parts, tr, cols), lambda i: (0, i, 0))],
        out_specs=[mat, mat, mat, mat],
        out_shape=[_sds((rows, cols), F32)] * 4,
        compiler_params=_params("parallel"),
    )(w, m, v, parts)


_NN = (((1,), (0,)), ((), ()))
_NT = (((1,), (1,)), ((), ()))
_TN = (((0,), (0,)), ((), ()))


def mm_cols(name, a, b_list, b_specs, nt, extras, extra_specs, out_shapes, out_specs, epilogue, n_blk, riders=(),
            tm_want=512):
    t_len, k_len = a.shape
    tm = _row_tile(t_len, tm_want)
    nb, ne, n_out, nr = len(b_list), len(extras), len(out_shapes), len(riders)
    t_steps = t_len // tm
    n_cols = b_specs[0].block_shape[-2 if nt else -1]

    def body(*refs):
        ins, r_in, outs, r_out, _, r_sem = _split_refs(refs, 1 + nb + ne, n_out, 0, nr)
        step = pl.program_id(0) * t_steps + pl.program_id(1)
        begin, end = _rider_hooks(riders, r_in, r_out, r_sem, step, n_blk * t_steps)
        begin()
        av = ins[0][...]
        for c0 in range(0, n_cols, MXU_WIDTH):
            cols = slice(c0, min(c0 + MXU_WIDTH, n_cols))
            accs = [lax.dot_general(av, br[cols, :] if nt else br[:, cols], _NT if nt else _NN,
                                    preferred_element_type=F32) for br in ins[1:1 + nb]]
            epilogue(accs, ins[1 + nb:], outs, cols)
        end()

    res = pl.pallas_call(
        body, name=name, grid=(n_blk, t_steps),
        in_specs=([pl.BlockSpec((tm, k_len), lambda j, t: (t, 0))] + list(b_specs) + list(extra_specs(tm))
                  + [ANY] * nr),
        out_specs=list(out_specs(tm)) + [ANY] * nr,
        out_shape=list(out_shapes) + [r.out_shape() for r in riders],
        scratch_shapes=_comm_sems() * nr,
        compiler_params=_params("arbitrary", "arbitrary"),
    )(a, *b_list, *extras, *[r.src for r in riders])
    return res[:n_out], res[n_out:]


def mm_reduce(name, a_list, a_specs, b_list, b_specs, nt, res, scale, t_len, n_len, n_blk, riders=(), tm_want=512):
    tm = _row_tile(t_len, tm_want)
    na, nr = len(a_list), len(riders)
    has_res = res is not None
    t_steps = t_len // tm

    def body(*refs):
        ins, r_in, outs, r_out, _, r_sem = _split_refs(refs, 2 * na + has_res, 1, 0, nr)
        o_ref = outs[0]
        j = pl.program_id(1)
        step = pl.program_id(0) * n_blk + j
        begin, end = _rider_hooks(riders, r_in, r_out, r_sem, step, t_steps * n_blk)
        begin()

        part = None
        for ar, br in zip(ins[:na], ins[na:2 * na]):
            d = lax.dot_general(ar[...], br[...], _NT if nt else _NN, preferred_element_type=F32)
            part = d if part is None else part + d

        @pl.when(j == 0)
        def _():
            o_ref[...] = part

        @pl.when(j > 0)
        def _():
            o_ref[...] += part

        if has_res or scale != 1.0:
            @pl.when(j == n_blk - 1)
            def _():
                val = o_ref[...] * scale if scale != 1.0 else o_ref[...]
                o_ref[...] = ins[2 * na][...] + val if has_res else val

        end()

    row = pl.BlockSpec((tm, n_len), lambda t, j: (t, 0))
    out = pl.pallas_call(
        body, name=name, grid=(t_steps, n_blk),
        in_specs=list(a_specs(tm)) + list(b_specs) + ([row] if has_res else []) + [ANY] * nr,
        out_specs=[row] + [ANY] * nr,
        out_shape=[_sds((t_len, n_len), F32)] + [r.out_shape() for r in riders],
        scratch_shapes=_comm_sems() * nr,
        compiler_params=_params("arbitrary", "arbitrary"),
    )(*a_list, *b_list, *([res] if has_res else []), *[r.src for r in riders])
    return out[0], out[1:]


def mm_tn(name, x, x_spec, dy_list, dy_specs, out_shapes, out_specs, scale, t_len, n_blk, riders=(),
          x_transposed=False):
    tt = _row_tile(t_len, 2048)
    nd, nr = len(dy_list), len(riders)
    t_steps = t_len // tt
    acc_shapes = [pltpu.VMEM(spec.block_shape[-2:], F32) for spec in out_specs]

    def body(*refs):
        ins, r_in, outs, r_out, accs, r_sem = _split_refs(refs, 1 + nd, nd, nd, nr)
        t = pl.program_id(1)
        step = pl.program_id(0) * t_steps + t
        begin, end = _rider_hooks(riders, r_in, r_out, r_sem, step, n_blk * t_steps)
        begin()
        xv = ins[0][...]
        for dr, acc in zip(ins[1:], accs):
            d = lax.dot_general(xv, dr[...], _NN if x_transposed else _TN, preferred_element_type=F32)

            @pl.when(t == 0)
            def _():
                acc[...] = d

            @pl.when(t > 0)
            def _():
                acc[...] += d

        @pl.when(t == t_steps - 1)
        def _():
            for acc, orf in zip(accs, outs):
                val = acc[...] * scale if scale != 1.0 else acc[...]
                orf[...] = val.astype(orf.dtype)

        end()

    res = pl.pallas_call(
        body, name=name, grid=(n_blk, t_steps),
        in_specs=[x_spec(tt)] + list(dy_specs(tt)) + [ANY] * nr,
        out_specs=list(out_specs) + [ANY] * nr,
        out_shape=list(out_shapes) + [r.out_shape() for r in riders],
        scratch_shapes=acc_shapes + _comm_sems() * nr,
        compiler_params=_params("arbitrary", "arbitrary"),
    )(x, *dy_list, *[r.src for r in riders])
    return res[:nd], res[nd:]


def rms_fwd(x, g, name):
    t_len, d = x.shape
    tm = _row_tile(t_len, 512)

    def body(x_ref, g_ref, h_ref, ht_ref):
        xv = x_ref[...]
        r = lax.rsqrt(jnp.mean(xv * xv, axis=-1, keepdims=True) + EPS)
        hv = xv * r * g_ref[...]
        h_ref[...] = hv.astype(BF16)
        ht_ref[...] = hv.T.astype(BF16)

    row = pl.BlockSpec((tm, d), lambda i: (i, 0))
    return pl.pallas_call(
        body, name=name, grid=(t_len // tm,),
        in_specs=[row, pl.BlockSpec((1, d), lambda i: (0, 0))],
        out_specs=[row, pl.BlockSpec((d, tm), lambda i: (0, i))],
        out_shape=[_sds((t_len, d), BF16), _sds((d, t_len), BF16)],
        compiler_params=_params("parallel"),
    )(x, g)


def rms_bwd(x, g, dh, dres, name):
    t_len, d = x.shape
    tm = _row_tile(t_len, 512)

    def body(x_ref, g_ref, dh_ref, dr_ref, dx_ref, dxb_ref, dg_ref):
        i = pl.program_id(0)
        xv = x_ref[...]
        r = lax.rsqrt(jnp.mean(xv * xv, axis=-1, keepdims=True) + EPS)
        xh = xv * r
        dhv = dh_ref[...]

        @pl.when(i == 0)
        def _():
            dg_ref[...] = jnp.zeros_like(dg_ref)

        dg_ref[...] += jnp.sum(dhv * xh, axis=0, keepdims=True)
        dxh = dhv * g_ref[...]
        dx = dr_ref[...] + r * (dxh - xh * jnp.mean(dxh * xh, axis=-1, keepdims=True))
        dx_ref[...] = dx
        dxb_ref[...] = dx.astype(BF16)

    row = pl.BlockSpec((tm, d), lambda i: (i, 0))
    vec = pl.BlockSpec((1, d), lambda i: (0, 0))
    return pl.pallas_call(
        body, name=name, grid=(t_len // tm,),
        in_specs=[row, vec, row, row],
        out_specs=[row, row, vec],
        out_shape=[_sds((t_len, d), F32), _sds((t_len, d), BF16), _sds((1, d), F32)],
        compiler_params=_params("arbitrary"),
    )(x, g, dh, dres)


def loss_head(y, target, name):
    t_len, d = y.shape
    tm = _row_tile(t_len, 512)

    def body(y_ref, t_ref, l_ref, dy_ref, dyb_ref):
        i = pl.program_id(0)
        err = y_ref[...] - t_ref[...]

        @pl.when(i == 0)
        def _():
            l_ref[...] = jnp.zeros_like(l_ref)

        rows = jnp.sum(err * err, axis=-1, keepdims=True) * (1.0 / d)
        l_ref[...] += 0.5 * jnp.sum(rows, axis=0, keepdims=True)
        dy = err * (1.0 / d)
        dy_ref[...] = dy
        dyb_ref[...] = dy.astype(BF16)

    row = pl.BlockSpec((tm, d), lambda i: (i, 0))
    return pl.pallas_call(
        body, name=name, grid=(t_len // tm,),
        in_specs=[row, row],
        out_specs=[pl.BlockSpec((8, LANES), lambda i: (0, 0)), row, row],
        out_shape=[_sds((8, LANES), F32), _sds((t_len, d), F32), _sds((t_len, d), BF16)],
        compiler_params=_params("arbitrary"),
    )(y, target)


def _conv_specs(tm, ch):
    per = tm // HALO
    cur = lambda cb: pl.BlockSpec((tm, ch), lambda i: (i, cb))
    prev = lambda cb: pl.BlockSpec((HALO, ch), lambda i: (jnp.maximum(i * per - 1, 0), cb))
    return [cur(0), cur(1), prev(0), prev(1)]


def _tap_scratch(rows, ch):
    return pltpu.VMEM((SUBLANES, rows + SUBLANES, ch), F32)


def _shifted_copies(buf, rows):
    buf[0, rows:rows + SUBLANES, :] = jnp.zeros((SUBLANES, buf.shape[2]), F32)
    for s in range(1, SUBLANES):
        buf[s, 0:rows, :] = buf[0, pl.ds(s, rows), :]


def _tap_rows(buf, off):
    shift = off % SUBLANES
    return buf[shift, off - shift:off - shift + ROW_CHUNK, :]


def _fill_glu(ext, a_ref, gt_ref, ap_ref, gp_ref, i, tm):
    vp = ap_ref[...] * _sigmoid(gp_ref[...])
    ext[0, 0:HALO, :] = jnp.where(i > 0, vp, 0.0)
    ext[0, HALO:HALO + tm, :] = a_ref[...] * _sigmoid(gt_ref[...])
    _shifted_copies(ext, HALO + tm)


def _conv_rows(ext, w_ref, b_ref, r0):
    acc = jnp.broadcast_to(b_ref[...], (ROW_CHUNK, b_ref.shape[1]))
    for k in range(CONV_WIDTH):
        acc = acc + w_ref[k:k + 1, :] * _tap_rows(ext, r0 + HALO - (CONV_WIDTH - 1) + k)
    return acc


def _layer_norm(yv):
    mu = jnp.mean(yv, axis=-1, keepdims=True)
    cen = yv - mu
    var = jnp.mean(cen * cen, axis=-1, keepdims=True)
    rstd = lax.rsqrt(var + EPS)
    return cen * rstd, rstd


def conv_fwd(z, w, b, lg, lb, name):
    t_len = z.shape[0]
    ch = w.shape[1]
    tm = _row_tile(t_len, 256)

    def body(a_ref, gt_ref, ap_ref, gp_ref, w_ref, b_ref, lg_ref, lb_ref, y_ref, pre_ref, ext):
        i = pl.program_id(0)
        _fill_glu(ext, a_ref, gt_ref, ap_ref, gp_ref, i, tm)
        for r0 in range(0, tm, ROW_CHUNK):
            pre = _conv_rows(ext, w_ref, b_ref, r0)
            pre_ref[r0:r0 + ROW_CHUNK, :] = pre
            xh, _ = _layer_norm(pre)
            u = xh * lg_ref[...] + lb_ref[...]
            y_ref[r0:r0 + ROW_CHUNK, :] = (u * _sigmoid(u)).astype(BF16)

    vec = pl.BlockSpec((1, ch), lambda i: (0, 0))
    row = pl.BlockSpec((tm, ch), lambda i: (i, 0))
    return pl.pallas_call(
        body, name=name, grid=(t_len // tm,),
        in_specs=_conv_specs(tm, ch) + [pl.BlockSpec((32, ch), lambda i: (0, 0)), vec, vec, vec],
        out_specs=[row, row],
        out_shape=[_sds((t_len, ch), BF16), _sds((t_len, ch), F32)],
        scratch_shapes=[_tap_scratch(HALO + tm, ch)],
        compiler_params=_params("parallel"),
    )(z, z, z, z, w, b, lg, lb)


def conv_bwd_norm(pre, dy_cat, lg, lb, name):
    t_len, ch = pre.shape
    tm = _row_tile(t_len, 256)

    def body(pre_ref, dy_ref, lg_ref, lb_ref, dc_ref, dlg_ref, dlb_ref, db_ref):
        i = pl.program_id(0)

        @pl.when(i == 0)
        def _():
            dlg_ref[...] = jnp.zeros_like(dlg_ref)
            dlb_ref[...] = jnp.zeros_like(dlb_ref)
            db_ref[...] = jnp.zeros_like(db_ref)

        for r0 in range(0, tm, ROW_CHUNK):
            xh, rstd = _layer_norm(pre_ref[r0:r0 + ROW_CHUNK, :])
            u = xh * lg_ref[...] + lb_ref[...]
            sg = _sigmoid(u)
            du = dy_ref[r0:r0 + ROW_CHUNK, :] * (sg * (1.0 + u * (1.0 - sg)))
            dlg_ref[...] += jnp.sum(du * xh, axis=0, keepdims=True)
            dlb_ref[...] += jnp.sum(du, axis=0, keepdims=True)
            dxh = du * lg_ref[...]
            dc = rstd * (dxh - jnp.mean(dxh, axis=-1, keepdims=True)
                         - xh * jnp.mean(dxh * xh, axis=-1, keepdims=True))
            db_ref[...] += jnp.sum(dc, axis=0, keepdims=True)
            dc_ref[r0:r0 + ROW_CHUNK, :] = dc

    vec = pl.BlockSpec((1, ch), lambda i: (0, 0))
    row = pl.BlockSpec((tm, ch), lambda i: (i, 0))
    return pl.pallas_call(
        body, name=name, grid=(t_len // tm,),
        in_specs=[row, row, vec, vec],
        out_specs=[row, vec, vec, vec],
        out_shape=[_sds((t_len, ch), F32)] + [_sds((1, ch), F32)] * 3,
        compiler_params=_params("arbitrary"),
    )(pre, dy_cat, lg, lb)


def conv_bwd_taps(z, dc, w, name):
    t_len = z.shape[0]
    ch = w.shape[1]
    tm = _row_tile(t_len, 256)
    per = tm // HALO
    n_tiles = t_len // tm
    last_halo = t_len // HALO - 1

    def body(a_ref, gt_ref, ap_ref, gp_ref, dc_ref, dn_ref, w_ref, dz_a_ref, dz_g_ref, dw_ref, ext, dext):
        i = pl.program_id(0)
        _fill_glu(ext, a_ref, gt_ref, ap_ref, gp_ref, i, tm)
        dext[0, 0:tm, :] = dc_ref[...]
        dext[0, tm:tm + HALO, :] = jnp.where(i < n_tiles - 1, dn_ref[...], 0.0)
        _shifted_copies(dext, tm + HALO)

        @pl.when(i == 0)
        def _():
            dw_ref[...] = jnp.zeros_like(dw_ref)

        for r0 in range(0, tm, ROW_CHUNK):
            dcv = dext[0, r0:r0 + ROW_CHUNK, :]
            dv = jnp.zeros((ROW_CHUNK, ch), F32)
            for k in range(CONV_WIDTH):
                dv = dv + w_ref[k:k + 1, :] * _tap_rows(dext, r0 + (CONV_WIDTH - 1) - k)
                prod = dcv * _tap_rows(ext, r0 + HALO - (CONV_WIDTH - 1) + k)
                fold = prod[0:8]
                for s in range(8, ROW_CHUNK, 8):
                    fold = fold + prod[s:s + 8]
                dw_ref[k] += fold
            av = a_ref[r0:r0 + ROW_CHUNK, :]
            sg = _sigmoid(gt_ref[r0:r0 + ROW_CHUNK, :])
            dz_a_ref[r0:r0 + ROW_CHUNK, :] = (dv * sg).astype(BF16)
            dz_g_ref[r0:r0 + ROW_CHUNK, :] = (dv * av * sg * (1.0 - sg)).astype(BF16)

    row = pl.BlockSpec((tm, ch), lambda i: (i, 0))
    nxt = pl.BlockSpec((HALO, ch), lambda i: (jnp.minimum((i + 1) * per, last_halo), 0))
    return pl.pallas_call(
        body, name=name, grid=(n_tiles,),
        in_specs=_conv_specs(tm, ch) + [row, nxt, pl.BlockSpec((32, ch), lambda i: (0, 0))],
        out_specs=[row, row, pl.BlockSpec((32, 8, ch), lambda i: (0, 0, 0))],
        out_shape=[_sds((t_len, ch), BF16), _sds((t_len, ch), BF16), _sds((32, 8, ch), F32)],
        scratch_shapes=[_tap_scratch(HALO + tm, ch), _tap_scratch(tm + HALO, ch)],
        compiler_params=_params("arbitrary"),
    )(z, z, z, z, dc, dc, w)


def _head_masks(rows):
    lane = lax.broadcasted_iota(jnp.int32, (rows, LANES), 1)
    low = lane < HEAD_DIM
    return low, jnp.logical_not(low)


def _per_head_mean(val, low):
    s_low = jnp.sum(jnp.where(low, val, 0.0), axis=-1, keepdims=True)
    s_high = jnp.sum(jnp.where(low, 0.0, val), axis=-1, keepdims=True)
    return jnp.where(low, s_low, s_high) * (1.0 / HEAD_DIM)


def qk_norm_fwd(z, g2, ch, name):
    t_len = z.shape[0]
    tm = _row_tile(t_len, 512)

    def body(z_ref, g_ref, o_ref):
        low, _ = _head_masks(tm)
        for c0 in range(0, ch, LANES):
            cols = slice(c0, c0 + LANES)
            xv = z_ref[:, cols]
            r = lax.rsqrt(_per_head_mean(xv * xv, low) + EPS)
            o_ref[:, cols] = xv * r * g_ref[:, cols]

    return pl.pallas_call(
        body, name=name, grid=(t_len // tm, 2),
        in_specs=[pl.BlockSpec((tm, ch), lambda i, w: (i, 2 + w)),
                  pl.BlockSpec((1, ch), lambda i, w: (0, w))],
        out_specs=pl.BlockSpec((tm, ch), lambda i, w: (i, w)),
        out_shape=_sds((t_len, 2 * ch), F32),
        compiler_params=_params("parallel", "parallel"),
    )(z, g2)


def qk_norm_bwd(z, g, d_list, z_off, ch, name):
    t_len = z.shape[0]
    tm = _row_tile(t_len, 512)
    nd = len(d_list)

    def body(*refs):
        z_ref, g_ref, d_refs = refs[0], refs[1], refs[2:2 + nd]
        dz_ref, dg_ref = refs[2 + nd], refs[3 + nd]
        low, _ = _head_masks(tm)

        @pl.when(pl.program_id(0) == 0)
        def _():
            dg_ref[...] = jnp.zeros_like(dg_ref)

        for c0 in range(0, ch, LANES):
            cols = slice(c0, c0 + LANES)
            xv = z_ref[:, cols]
            r = lax.rsqrt(_per_head_mean(xv * xv, low) + EPS)
            xh = xv * r
            dy = d_refs[0][:, cols]
            for dr in d_refs[1:]:
                dy = dy + dr[:, cols]
            dg_ref[...] += jnp.sum(dy * xh, axis=0, keepdims=True)
            dxh = dy * g_ref[...]
            dz_ref[:, cols] = (r * (dxh - xh * _per_head_mean(dxh * xh, low))).astype(BF16)

    blk = pl.BlockSpec((tm, ch), lambda i: (i, 0))
    return pl.pallas_call(
        body, name=name, grid=(t_len // tm,),
        in_specs=[pl.BlockSpec((tm, ch), lambda i: (i, z_off)),
                  pl.BlockSpec((1, LANES), lambda i: (0, 0))] + [blk] * nd,
        out_specs=[blk, pl.BlockSpec((1, LANES), lambda i: (0, 0))],
        out_shape=[_sds((t_len, ch), BF16), _sds((1, LANES), F32)],
        compiler_params=_params("arbitrary"),
    )(z, g, *d_list)


def _alibi_bias(n_heads, dilation):
    slopes = 2.0 ** (-ALIBI_MAX_BIAS * jnp.arange(1, n_heads + 1, dtype=F32) / n_heads)
    qi = jnp.arange(ATT_BLOCK)[:, None]
    kj = jnp.arange(ATT_BLOCK)[None, :]
    dist_cur = (qi - kj).astype(F32)
    dist_prev = (ATT_BLOCK + qi - kj).astype(F32)
    cur = jnp.where((qi >= kj)[None], -slopes[:, None, None] * (dilation * dist_cur)[None], MASKED)
    prev = jnp.where((kj >= qi)[None], -slopes[:, None, None] * (dilation * dist_prev)[None], MASKED)
    return jnp.concatenate([prev, cur], axis=-1).astype(F32)


def _stack_heads(val, low, high):
    return jnp.concatenate([jnp.where(low, val, 0.0), jnp.where(high, val, 0.0)], axis=0).astype(BF16)


def _head_rows(val, low, high):
    other = pltpu.roll(val, HEAD_DIM, axis=1)
    rows = jnp.concatenate([jnp.where(low, val, other), jnp.where(high, val, other)], axis=0)
    return jnp.concatenate([rows, rows], axis=1)


def _unit_scores(q2, k2, b_ref, has_prev):
    s = lax.dot_general(q2, k2, _NT, preferred_element_type=F32)
    s = s + b_ref[...].reshape(2 * ATT_BLOCK, 2 * ATT_BLOCK)
    penalty = jnp.where(has_prev, 0.0, MASKED)
    return jnp.concatenate([s[:, :ATT_BLOCK] + penalty, s[:, ATT_BLOCK:]], axis=1)


def _strided_rows(r, dilation):
    per = ATT_CHUNK // dilation
    return pl.ds(r, per, stride=dilation) if dilation > 1 else pl.ds(0, per)


def _deinterleave(dst, src_ref, dilation, base=None, dtype=None):
    per = ATT_CHUNK // dilation
    for r in range(dilation):
        val = src_ref[_strided_rows(r, dilation), :]
        val = val if dtype is None else val.astype(dtype)
        if base is None:
            dst[r * per:(r + 1) * per, :] = val
        else:
            dst[pl.ds(pl.multiple_of(base + r * per, ATT_BLOCK), per), :] = val


def _unit_rows(u, c, nb, base, pbase):
    in_chunk = lax.rem(u, jnp.int32(nb)) > 0
    has_prev = jnp.logical_or(in_chunk, c > 0)
    urow = pl.multiple_of(u * ATT_BLOCK, ATT_BLOCK)
    crow = pl.multiple_of(base + u * ATT_BLOCK, ATT_BLOCK)
    prow = pl.multiple_of(jnp.where(in_chunk, base + (u - 1) * ATT_BLOCK,
                                    pbase + (u + nb - 1) * ATT_BLOCK), ATT_BLOCK)
    return in_chunk, has_prev, urow, crow, prow


def _interleave(dst_ref, src, dilation, base=None):
    per = ATT_CHUNK // dilation
    for r in range(dilation):
        if base is None:
            val = src[r * per:(r + 1) * per, :]
        else:
            val = src[pl.ds(pl.multiple_of(base + r * per, ATT_BLOCK), per), :]
        dst_ref[_strided_rows(r, dilation), :] = val


def attn_fwd(qk, z, dilation, ch, name):
    t_len = qk.shape[0]
    pairs = ch // LANES
    nc = t_len // ATT_CHUNK
    nb = ATT_UNITS // dilation
    scale = 1.0 / math.sqrt(HEAD_DIM)
    bias = _alibi_bias(2 * pairs, dilation)

    def body(q_ref, k_ref, v_ref, b_ref, o_ref, l_ref, qd, kx, vx, od, ld):
        c = pl.program_id(1)
        slot = lax.rem(c, jnp.int32(2))
        base, pbase = slot * ATT_CHUNK, (1 - slot) * ATT_CHUNK

        @pl.when(c == 0)
        def _():
            kx[...] = jnp.zeros_like(kx)
            vx[...] = jnp.zeros_like(vx)

        _deinterleave(qd, q_ref, dilation)
        _deinterleave(kx, k_ref, dilation, base, BF16)
        _deinterleave(vx, v_ref, dilation, base, BF16)
        low, high = _head_masks(ATT_BLOCK)

        def unit(u, carry):
            _, has_prev, urow, crow, prow = _unit_rows(u, c, nb, base, pbase)
            q2 = _stack_heads(qd[pl.ds(urow, ATT_BLOCK), :] * scale, low, high)
            k2 = jnp.concatenate([kx[pl.ds(prow, ATT_BLOCK), :], kx[pl.ds(crow, ATT_BLOCK), :]], axis=0)
            v2 = jnp.concatenate([vx[pl.ds(prow, ATT_BLOCK), :], vx[pl.ds(crow, ATT_BLOCK), :]], axis=0)
            s = _unit_scores(q2, k2, b_ref, has_prev)
            mx = jnp.max(s, axis=-1, keepdims=True)
            e = jnp.exp(s - mx)
            den = jnp.sum(e, axis=-1, keepdims=True)
            acc = lax.dot_general(e.astype(BF16), v2, _NN, preferred_element_type=F32) / den
            lse = jnp.broadcast_to(mx + jnp.log(den), acc.shape)
            od[pl.ds(urow, ATT_BLOCK), :] = jnp.where(low, acc[:ATT_BLOCK], acc[ATT_BLOCK:])
            ld[pl.ds(urow, ATT_BLOCK), :] = jnp.where(low, lse[:ATT_BLOCK], lse[ATT_BLOCK:])
            return carry

        lax.fori_loop(0, ATT_UNITS, unit, 0, unroll=8)
        _interleave(o_ref, od, dilation)
        _interleave(l_ref, ld, dilation)

    blk = (ATT_CHUNK, LANES)
    bias_spec = pl.BlockSpec((2, ATT_BLOCK, 2 * ATT_BLOCK), lambda p, c: (p, 0, 0))
    out_spec = pl.BlockSpec(blk, lambda p, c: (c, p))
    return pl.pallas_call(
        body, name=name, grid=(pairs, nc),
        in_specs=[pl.BlockSpec(blk, lambda p, c: (c, p)),
                  pl.BlockSpec(blk, lambda p, c: (c, pairs + p)),
                  pl.BlockSpec(blk, lambda p, c: (c, 4 * pairs + p)),
                  bias_spec],
        out_specs=[out_spec, out_spec],
        out_shape=[_sds((t_len, ch), F32)] * 2,
        scratch_shapes=[pltpu.VMEM((ATT_CHUNK, LANES), F32),
                        pltpu.VMEM((2 * ATT_CHUNK, LANES), BF16), pltpu.VMEM((2 * ATT_CHUNK, LANES), BF16),
                        pltpu.VMEM((ATT_CHUNK, LANES), F32), pltpu.VMEM((ATT_CHUNK, LANES), F32)],
        compiler_params=_params("arbitrary", "arbitrary"),
    )(qk, qk, z, bias)


def attn_combine(outs, lses, name):
    t_len, ch = outs[0].shape
    tm = _row_tile(t_len, 512)

    def body(o1, o2, o3, l1, l2, l3, out_ref, outb_ref, lg_ref):
        a, b, c = l1[...], l2[...], l3[...]
        mx = jnp.maximum(jnp.maximum(a, b), c)
        tot = mx + jnp.log(jnp.exp(a - mx) + jnp.exp(b - mx) + jnp.exp(c - mx))
        val = jnp.exp(a - tot) * o1[...] + jnp.exp(b - tot) * o2[...] + jnp.exp(c - tot) * o3[...]
        out_ref[...] = val
        outb_ref[...] = val.astype(BF16)
        lg_ref[...] = tot

    row = pl.BlockSpec((tm, ch), lambda i: (i, 0))
    return pl.pallas_call(
        body, name=name, grid=(t_len // tm,),
        in_specs=[row] * 6, out_specs=[row] * 3,
        out_shape=[_sds((t_len, ch), F32), _sds((t_len, ch), BF16), _sds((t_len, ch), F32)],
        compiler_params=_params("parallel"),
    )(*outs, *lses)


def attn_bwd(qk, z, dy_cat, out, lg, dilation, ch, name):
    t_len = qk.shape[0]
    pairs = ch // LANES
    nc = t_len // ATT_CHUNK
    nb = ATT_UNITS // dilation
    scale = 1.0 / math.sqrt(HEAD_DIM)
    bias = _alibi_bias(2 * pairs, dilation)

    def body(q_ref, k_ref, v_ref, do_ref, out_ref, lg_ref, b_ref, dq_ref, dk_ref, dv_ref,
             qd, dod, lgd, dld, dl_nat, kx, vx, dkx, dvx, dqd):
        c = pl.program_id(1)
        slot = lax.rem(c, jnp.int32(2))
        base, pbase = slot * ATT_CHUNK, (1 - slot) * ATT_CHUNK

        @pl.when(c == 0)
        def _():
            for ref in (kx, vx, dkx, dvx):
                ref[...] = jnp.zeros_like(ref)

        @pl.when(c < nc)
        def _():
            low_all, _ = _head_masks(ATT_CHUNK)
            dl_nat[...] = _per_head_mean(do_ref[...] * out_ref[...], low_all) * float(HEAD_DIM)
            _deinterleave(qd, q_ref, dilation)
            _deinterleave(dod, do_ref, dilation)
            _deinterleave(lgd, lg_ref, dilation)
            _deinterleave(dld, dl_nat, dilation)
            _deinterleave(kx, k_ref, dilation, base, BF16)
            _deinterleave(vx, v_ref, dilation, base, BF16)
            cur = pl.ds(pl.multiple_of(base, ATT_CHUNK), ATT_CHUNK)
            dkx[cur, :] = jnp.zeros((ATT_CHUNK, LANES), F32)
            dvx[cur, :] = jnp.zeros((ATT_CHUNK, LANES), F32)
            low, high = _head_masks(ATT_BLOCK)

            def unit(u, carry):
                _, has_prev, urow, crow, prow = _unit_rows(u, c, nb, base, pbase)
                rows = pl.ds(urow, ATT_BLOCK)
                q2 = _stack_heads(qd[rows, :] * scale, low, high)
                do2 = _stack_heads(dod[rows, :], low, high)
                lse = _head_rows(lgd[rows, :], low, high)
                delta = _head_rows(dld[rows, :], low, high)
                k2 = jnp.concatenate([kx[pl.ds(prow, ATT_BLOCK), :], kx[pl.ds(crow, ATT_BLOCK), :]], axis=0)
                v2 = jnp.concatenate([vx[pl.ds(prow, ATT_BLOCK), :], vx[pl.ds(crow, ATT_BLOCK), :]], axis=0)
                prob = jnp.exp(_unit_scores(q2, k2, b_ref, has_prev) - lse)
                dp = lax.dot_general(do2, v2, _NT, preferred_element_type=F32)
                ds = (prob * (dp - delta)).astype(BF16)
                dq2 = lax.dot_general(ds, k2, _NN, preferred_element_type=F32)
                dk2 = lax.dot_general(ds, q2, _TN, preferred_element_type=F32)
                dv2 = lax.dot_general(prob.astype(BF16), do2, _TN, preferred_element_type=F32)
                dqd[rows, :] = scale * jnp.where(low, dq2[:ATT_BLOCK], dq2[ATT_BLOCK:])
                dkx[pl.ds(prow, ATT_BLOCK), :] += dk2[:ATT_BLOCK]
                dkx[pl.ds(crow, ATT_BLOCK), :] += dk2[ATT_BLOCK:]
                dvx[pl.ds(prow, ATT_BLOCK), :] += dv2[:ATT_BLOCK]
                dvx[pl.ds(crow, ATT_BLOCK), :] += dv2[ATT_BLOCK:]
                return carry

            lax.fori_loop(0, ATT_UNITS, unit, 0, unroll=8)
            _interleave(dq_ref, dqd, dilation)

        @pl.when(c > 0)
        def _():
            _interleave(dk_ref, dkx, dilation, pbase)
            _interleave(dv_ref, dvx, dilation, pbase)

    blk = (ATT_CHUNK, LANES)
    here = lambda c: jnp.minimum(c, nc - 1)
    spec = lambda off: pl.BlockSpec(blk, lambda p, c: (here(c), off + p))
    late = pl.BlockSpec(blk, lambda p, c: (jnp.maximum(c - 1, 0), p))
    bias_spec = pl.BlockSpec((2, ATT_BLOCK, 2 * ATT_BLOCK), lambda p, c: (p, 0, 0))
    f32_chunk = pltpu.VMEM((ATT_CHUNK, LANES), F32)
    return pl.pallas_call(
        body, name=name, grid=(pairs, nc + 1),
        in_specs=[spec(0), spec(pairs), spec(4 * pairs), spec(pairs), spec(0), spec(0), bias_spec],
        out_specs=[spec(0), late, late],
        out_shape=[_sds((t_len, ch), F32)] * 3,
        scratch_shapes=[f32_chunk] * 5
                       + [pltpu.VMEM((2 * ATT_CHUNK, LANES), BF16)] * 2
                       + [pltpu.VMEM((2 * ATT_CHUNK, LANES), F32)] * 2 + [f32_chunk],
        compiler_params=_params("arbitrary", "arbitrary"),
    )(qk, qk, z, dy_cat, out, lg, bias)


def sum3_bf16(a, b, c, name):
    t_len, ch = a.shape
    tm = _row_tile(t_len, 512)

    def body(a_ref, b_ref, c_ref, o_ref):
        o_ref[...] = (a_ref[...] + b_ref[...] + c_ref[...]).astype(BF16)

    row = pl.BlockSpec((tm, ch), lambda i: (i, 0))
    return pl.pallas_call(
        body, name=name, grid=(t_len // tm,), in_specs=[row] * 3, out_specs=row,
        out_shape=_sds((t_len, ch), BF16), compiler_params=_params("parallel"),
    )(a, b, c)


def _blk3(rows, cols):
    return pl.BlockSpec((None, rows, cols), lambda j, t: (j, 0, 0))


def ffn_up(h, wg, wu, name, riders):
    t_len, d = h.shape
    n_blk, _, fj = wg.shape

    def epilogue(accs, e_refs, o_refs, cols):
        gate, up = accs
        o_refs[0][:, cols] = gate.astype(BF16)
        o_refs[1][:, cols] = up.astype(BF16)
        o_refs[2][:, cols] = (gate * _sigmoid(gate) * up).astype(BF16)

    act = lambda tm: pl.BlockSpec((None, tm, fj), lambda j, t: (j, t, 0))
    return mm_cols(name, h, [wg, wu], [_blk3(d, fj)] * 2, False, [], lambda tm: [],
                   [_sds((n_blk, t_len, fj), BF16)] * 3, lambda tm: [act(tm)] * 3, epilogue, n_blk, riders,
                   tm_want=1024)


def ffn_gate(h, wg, name, riders):
    t_len, d = h.shape
    n_blk, _, fj = wg.shape

    def epilogue(accs, e_refs, o_refs, cols):
        o_refs[0][:, cols] = accs[0].astype(BF16)

    act = lambda tm: pl.BlockSpec((None, tm, fj), lambda j, t: (j, t, 0))
    return mm_cols(name, h, [wg], [_blk3(d, fj)], False, [], lambda tm: [],
                   [_sds((n_blk, t_len, fj), BF16)], lambda tm: [act(tm)], epilogue, n_blk, riders, tm_want=1024)


def ffn_up_after_gate(h, wu, gate, name, riders):
    t_len, d = h.shape
    n_blk, _, fj = wu.shape

    def epilogue(accs, e_refs, o_refs, cols):
        gv = e_refs[0][:, cols].astype(F32)
        o_refs[0][:, cols] = accs[0].astype(BF16)
        o_refs[1][:, cols] = (gv * _sigmoid(gv) * accs[0]).astype(BF16)

    act = lambda tm: pl.BlockSpec((None, tm, fj), lambda j, t: (j, t, 0))
    return mm_cols(name, h, [wu], [_blk3(d, fj)], False, [gate], lambda tm: [act(tm)],
                   [_sds((n_blk, t_len, fj), BF16)] * 2, lambda tm: [act(tm)] * 2, epilogue, n_blk, riders,
                   tm_want=1024)


def ffn_down(act, wd, res, name, riders):
    n_blk, t_len, fj = act.shape
    d = wd.shape[2]
    return mm_reduce(name, [act], lambda tm: [pl.BlockSpec((None, tm, fj), lambda t, j: (j, t, 0))],
                     [wd], [pl.BlockSpec((None, fj, d), lambda t, j: (j, 0, 0))], False, res, 0.5,
                     t_len, d, n_blk, riders, tm_want=1024)


def ffn_bwd(ht, gate, up, act, wg, wu, wd, dyb, name):
    d, t_len = ht.shape
    n_blk, _, fj = act.shape

    def epilogue(accs, e_refs, o_refs, cols):
        d_act = 0.5 * accs[0]
        gv, uv = e_refs[0][:, cols].astype(F32), e_refs[1][:, cols].astype(F32)
        sg = _sigmoid(gv)
        o_refs[0][:, cols] = (d_act * uv * (sg * (1.0 + gv * (1.0 - sg)))).astype(BF16)
        o_refs[1][:, cols] = (d_act * gv * sg).astype(BF16)

    act_jt = lambda tm: pl.BlockSpec((None, tm, fj), lambda j, t: (j, t, 0))
    (d_gate, d_up), _ = mm_cols(name + "_dact", dyb, [wd], [_blk3(fj, d)], True, [gate, up],
                                lambda tm: [act_jt(tm)] * 2, [_sds((n_blk, t_len, fj), BF16)] * 2,
                                lambda tm: [act_jt(tm)] * 2, epilogue, n_blk, tm_want=1024)

    ht_spec = lambda tt: pl.BlockSpec((d, tt), lambda j, t: (0, t))
    (d_wg,), _ = mm_tn(name + "_dwg", ht, ht_spec, [d_gate], lambda tt: [act_jt(tt)],
                       [_sds((n_blk, d, fj), BF16)], [_blk3(d, fj)], 1.0, t_len, n_blk, x_transposed=True)
    (d_wu,), (recv_wg,) = mm_tn(name + "_dwu", ht, ht_spec, [d_up], lambda tt: [act_jt(tt)],
                                [_sds((n_blk, d, fj), BF16)], [_blk3(d, fj)], 1.0, t_len, n_blk,
                                [Rider("scatter", d_wg)], x_transposed=True)

    (d_wd,), (recv_wu,) = mm_tn(name + "_dwd", act, act_jt, [dyb],
                                lambda tt: [pl.BlockSpec((tt, d), lambda j, t: (t, 0))],
                                [_sds((n_blk, fj, d), BF16)], [_blk3(fj, d)], 0.5, t_len, n_blk,
                                [Rider("scatter", d_wu)])

    act_tj = lambda tm: pl.BlockSpec((None, tm, fj), lambda t, j: (j, t, 0))
    w_tj = pl.BlockSpec((None, d, fj), lambda t, j: (j, 0, 0))
    dh, (recv_wd,) = mm_reduce(
        name + "_dh", [d_gate, d_up], lambda tm: [act_tj(tm)] * 2,
        [wg, wu], [w_tj, w_tj], True, None, 1.0, t_len, d, n_blk,
        [Rider("scatter", d_wd)], tm_want=1024)
    return dh, recv_wg, recv_wu, recv_wd


def local_step(x, target, g1, wg1, wu1_s, wd1_s, gmix, win_s, conv_w, conv_b, ln_g, ln_b, gq, gk, wout_s, g3,
               wg2_s, wu2_s, wd2_s):
    t_len, d = x.shape
    ch = d // 2
    ij = win_s.shape[1]
    oj = wout_s.shape[0]
    n_blk = N_DEV

    h1, h1t = rms_fwd(x, g1, "rms1")
    (gate1,), (wu1,) = ffn_gate(h1, wg1, "ffn1_gate", [Rider("gather", wu1_s)])
    (up1, act1), (wd1, wg2) = ffn_up_after_gate(h1, wu1, gate1, "ffn1_up",
                                                [Rider("gather", wd1_s), Rider("gather", wg2_s)])
    x1, (win, wu2) = ffn_down(act1, wd1, x, "ffn1_down", [Rider("gather", win_s), Rider("gather", wu2_s)])

    h2, h2t = rms_fwd(x1, gmix, "rms_mix")

    def store_f32(accs, e_refs, o_refs, cols):
        o_refs[0][:, cols] = accs[0]

    (z,), (wd2, wout) = mm_cols(
        "w_in", h2, [win], [_blk3(d, ij)], False, [], lambda tm: [],
        [_sds((t_len, n_blk * ij), F32)],
        lambda tm: [pl.BlockSpec((tm, ij), lambda j, t: (t, j))], store_f32, n_blk,
        [Rider("gather", wd2_s), Rider("gather", wout_s)], tm_want=2048)

    conv_w32 = jnp.pad(conv_w, ((0, 32 - CONV_WIDTH), (0, 0)))
    y_conv, conv_pre = conv_fwd(z, conv_w32, conv_b, ln_g, ln_b, "conv_fwd")

    g2 = jnp.concatenate([jnp.tile(gq, (1, ch // HEAD_DIM)), jnp.tile(gk, (1, ch // HEAD_DIM))], axis=1)
    qk = qk_norm_fwd(z, g2, ch, "qk_norm")
    branch = [attn_fwd(qk, z, dil, ch, "attn_fwd_d%d" % dil) for dil in DILATIONS]
    att, att_b, lg = attn_combine([o for o, _ in branch], [l for _, l in branch], "attn_combine")

    y_cat = jnp.concatenate([y_conv, att_b], axis=1)
    wout_full = wout.reshape(1, n_blk * oj, d)
    x2, _ = mm_reduce(
        "w_out", [y_cat], lambda tm: [pl.BlockSpec((tm, n_blk * oj), lambda t, j: (t, 0))],
        [wout_full], [pl.BlockSpec((None, n_blk * oj, d), lambda t, j: (0, 0, 0))], False, x1, 1.0,
        t_len, d, 1)

    h3, h3t = rms_fwd(x2, g3, "rms3")
    (gate2, up2, act2), _ = ffn_up(h3, wg2, wu2, "ffn2_up", [])
    y, _ = ffn_down(act2, wd2, x2, "ffn2_down", [])

    loss_tile, dy, dyb = loss_head(y, target, "loss")

    dh3, recv_wg2, recv_wu2, recv_wd2 = ffn_bwd(h3t, gate2, up2, act2, wg2, wu2, wd2, dyb, "ffn2")
    dx2, dx2b, d_g3 = rms_bwd(x2, g3, dh3, dy, "rms3_bwd")

    (dy_cat,), _ = mm_cols("w_out_dy", dx2b, [wout_full], [_blk3(n_blk * oj, d)], True, [], lambda tm: [],
                           [_sds((t_len, n_blk * oj), F32)],
                           lambda tm: [pl.BlockSpec((tm, n_blk * oj), lambda j, t: (t, 0))], store_f32, 1,
                           tm_want=1024)
    (d_wout,), _ = mm_tn("w_out_dw", y_cat, lambda tt: pl.BlockSpec((tt, oj), lambda j, t: (t, j)),
                         [dx2b], lambda tt: [pl.BlockSpec((tt, d), lambda j, t: (t, 0))],
                         [_sds((n_blk, oj, d), BF16)], [_blk3(oj, d)], 1.0, t_len, n_blk)

    dc, d_lg, d_lb, d_cb = conv_bwd_norm(conv_pre, dy_cat, ln_g, ln_b, "conv_bwd_norm")
    dz_a, dz_g, d_cw8 = conv_bwd_taps(z, dc, conv_w32, "conv_bwd_taps")
    d_cw = jnp.sum(d_cw8, axis=1)[:CONV_WIDTH]

    grads = [attn_bwd(qk, z, dy_cat, att, lg, dil, ch, "attn_bwd_d%d" % dil) for dil in DILATIONS]
    gq_t = jnp.tile(gq, (1, LANES // HEAD_DIM))
    gk_t = jnp.tile(gk, (1, LANES // HEAD_DIM))
    dz_q, d_gq2 = qk_norm_bwd(z, gq_t, [g[0] for g in grads], 2, ch, "q_norm_bwd")
    dz_k, d_gk2 = qk_norm_bwd(z, gk_t, [g[1] for g in grads], 3, ch, "k_norm_bwd")
    d_gq = d_gq2[:, :HEAD_DIM] + d_gq2[:, HEAD_DIM:]
    d_gk = d_gk2[:, :HEAD_DIM] + d_gk2[:, HEAD_DIM:]
    dz_v = sum3_bf16(grads[0][2], grads[1][2], grads[2][2], "dv_sum")
    dzb = jnp.concatenate([dz_a, dz_g, dz_q, dz_k, dz_v], axis=1)

    (d_win,), (recv_wout,) = mm_tn(
        "w_in_dw", h2t, lambda tt: pl.BlockSpec((d, tt), lambda j, t: (0, t)),
        [dzb], lambda tt: [pl.BlockSpec((tt, ij), lambda j, t: (t, j))],
        [_sds((n_blk, d, ij), BF16)], [_blk3(d, ij)], 1.0, t_len, n_blk, [Rider("scatter", d_wout)],
        x_transposed=True)
    dh2, (recv_win,) = mm_reduce(
        "w_in_dh", [dzb], lambda tm: [pl.BlockSpec((tm, ij), lambda t, j: (t, j))],
        [win], [pl.BlockSpec((None, d, ij), lambda t, j: (j, 0, 0))], True, None, 1.0,
        t_len, d, n_blk, [Rider("scatter", d_win)], tm_want=1024)
    dx1, dx1b, d_gmix = rms_bwd(x1, gmix, dh2, dx2, "rms_mix_bwd")

    dh1, recv_wg1, recv_wu1, recv_wd1 = ffn_bwd(h1t, gate1, up1, act1, wg1, wu1, wd1, dx1b, "ffn1")
    grad_x, _, d_g1 = rms_bwd(x, g1, dh1, dx1, "rms1_bwd")

    big = dict(ffn1_w_gate=recv_wg1, ffn1_w_up=recv_wu1, ffn1_w_down=recv_wd1, w_in=recv_win, w_out=recv_wout,
               ffn2_w_gate=recv_wg2, ffn2_w_up=recv_wu2, ffn2_w_down=recv_wd2)
    small = dict(g1=d_g1, gmix=d_gmix, g3=d_g3, conv_b=d_cb, ln_g=d_lg, ln_b=d_lb, gq=d_gq, gk=d_gk, conv_w=d_cw)
    return loss_tile[0, 0], grad_x, big, small


SMALL_ROWS = 48


def _pack_small(ch, g1, gmix, g3, conv_b, ln_g, ln_b, gq, gk, conv_w):
    pad_head = lambda v: jnp.pad(v, ((0, 0), (0, ch - v.shape[1])))
    rows = [g1.reshape(2, ch), gmix.reshape(2, ch), g3.reshape(2, ch), conv_b, ln_g, ln_b,
            pad_head(gq), pad_head(gk), conv_w]
    packed = jnp.concatenate(rows, axis=0)
    return jnp.pad(packed, ((0, SMALL_ROWS - packed.shape[0]), (0, 0)))


def _unpack_small(packed, d):
    return dict(g1=packed[0:2].reshape(1, d), gmix=packed[2:4].reshape(1, d), g3=packed[4:6].reshape(1, d),
                conv_b=packed[6:7], ln_g=packed[7:8], ln_b=packed[8:9],
                gq=packed[9:10, :HEAD_DIM], gk=packed[10:11, :HEAD_DIM])


def kernel(x, ffn1_norm_g, ffn1_w_gate, ffn1_w_up, ffn1_w_down, mix_norm_g, w_in, conv_w_dw, conv_b_dw, conv_ln_g, conv_ln_b, q_norm_g, k_norm_g, w_out, ffn2_norm_g, ffn2_w_gate, ffn2_w_up, ffn2_w_down, loss_target, m_ffn1_norm_g, m_ffn1_w_gate, m_ffn1_w_up, m_ffn1_w_down, m_mix_norm_g, m_w_in, m_conv_w_dw, m_conv_b_dw, m_conv_ln_g, m_conv_ln_b, m_q_norm_g, m_k_norm_g, m_w_out, m_ffn2_norm_g, m_ffn2_w_gate, m_ffn2_w_up, m_ffn2_w_down, v_ffn1_norm_g, v_ffn1_w_gate, v_ffn1_w_up, v_ffn1_w_down, v_mix_norm_g, v_w_in, v_conv_w_dw, v_conv_b_dw, v_conv_ln_g, v_conv_ln_b, v_q_norm_g, v_k_norm_g, v_w_out, v_ffn2_norm_g, v_ffn2_w_gate, v_ffn2_w_up, v_ffn2_w_down):
    d = x.shape[-1]
    ch = d // 2
    me = 4 * lax.axis_index("x") + 2 * lax.axis_index("y") + lax.axis_index("c")

    shard = lambda w: w[0].astype(BF16)
    wg1 = all_gather(shard(ffn1_w_gate), "ag_wg1")
    cw_all = all_gather(conv_w_dw[0], "ag_convw")
    conv_w = jnp.transpose(cw_all, (1, 0, 2)).reshape(CONV_WIDTH, ch)

    loss_part, grad_x, big, small = local_step(
        x[0], loss_target[0], ffn1_norm_g, wg1, shard(ffn1_w_up), shard(ffn1_w_down), mix_norm_g, shard(w_in),
        conv_w, conv_b_dw, conv_ln_g, conv_ln_b, q_norm_g, k_norm_g, shard(w_out), ffn2_norm_g,
        shard(ffn2_w_gate), shard(ffn2_w_up), shard(ffn2_w_down))
    loss = lax.psum(loss_part, MESH_AXES)

    state = dict(
        ffn1_w_gate=(ffn1_w_gate, m_ffn1_w_gate, v_ffn1_w_gate), ffn1_w_up=(ffn1_w_up, m_ffn1_w_up, v_ffn1_w_up),
        ffn1_w_down=(ffn1_w_down, m_ffn1_w_down, v_ffn1_w_down), w_in=(w_in, m_w_in, v_w_in),
        w_out=(w_out, m_w_out, v_w_out),
        ffn2_w_gate=(ffn2_w_gate, m_ffn2_w_gate, v_ffn2_w_gate), ffn2_w_up=(ffn2_w_up, m_ffn2_w_up, v_ffn2_w_up),
        ffn2_w_down=(ffn2_w_down, m_ffn2_w_down, v_ffn2_w_down))
    out = {}
    for pname, (w, m, v) in state.items():
        res = adamw(w[0], m[0], v[0], big[pname], "adamw_" + pname)
        out[pname] = [r[None] for r in res]

    zero_taps = jnp.zeros((CONV_WIDTH, ch), F32)
    pack = lambda g1, gm, g3, cb, lg, lb, gq, gk: _pack_small(ch, g1, gm, g3, cb, lg, lb, gq, gk, zero_taps)
    small_parts = all_gather(_pack_small(ch, **small), "ag_small_grads")
    s_res = adamw(
        pack(ffn1_norm_g, mix_norm_g, ffn2_norm_g, conv_b_dw, conv_ln_g, conv_ln_b, q_norm_g, k_norm_g),
        pack(m_ffn1_norm_g, m_mix_norm_g, m_ffn2_norm_g, m_conv_b_dw, m_conv_ln_g, m_conv_ln_b, m_q_norm_g, m_k_norm_g),
        pack(v_ffn1_norm_g, v_mix_norm_g, v_ffn2_norm_g, v_conv_b_dw, v_conv_ln_g, v_conv_ln_b, v_q_norm_g, v_k_norm_g),
        small_parts, "adamw_small")
    s_out = [_unpack_small(r, d) for r in s_res]
    names = dict(g1="ffn1_norm_g", gmix="mix_norm_g", g3="ffn2_norm_g", conv_b="conv_b_dw", ln_g="conv_ln_g",
                 ln_b="conv_ln_b", gq="q_norm_g", gk="k_norm_g")
    for key, full in names.items():
        out[full] = [r[key] for r in s_out]

    cshard = ch // N_DEV
    taps_sum = s_res[0][11:11 + CONV_WIDTH]
    taps_mine = lax.dynamic_slice(taps_sum, (0, me * cshard), (CONV_WIDTH, cshard))
    pad_taps = lambda a: jnp.pad(a, ((0, 32 - CONV_WIDTH), (0, 0)))
    c_res = adamw(pad_taps(conv_w_dw[0]), pad_taps(m_conv_w_dw[0]), pad_taps(v_conv_w_dw[0]),
                  pad_taps(taps_mine)[None], "adamw_convw")
    out["conv_w_dw"] = [r[:CONV_WIDTH][None] for r in c_res]

    order = ["ffn1_norm_g", "ffn1_w_gate", "ffn1_w_up", "ffn1_w_down", "mix_norm_g", "w_in", "conv_w_dw",
             "conv_b_dw", "conv_ln_g", "conv_ln_b", "q_norm_g", "k_norm_g", "w_out", "ffn2_norm_g",
             "ffn2_w_gate", "ffn2_w_up", "ffn2_w_down"]
    result = [loss, grad_x[None]]
    for kind in range(4):
        result += [out[n][kind] for n in order]
    return tuple(result)
```

```python
import math
from typing import NamedTuple

import jax
import jax.numpy as jnp
from jax import lax
from jax.experimental import pallas as pl
from jax.experimental.pallas import tpu as pltpu

F32 = jnp.float32
BF16 = jnp.bfloat16

N_DEV = 8
EPS = 1e-6
HEAD_DIM = 64
LANES = 128
MXU_WIDTH = 256
W_IN_SPLIT = 4
CONV_WIDTH = 31
SUBLANES = 8
HALO = 32
ROW_CHUNK = 32
ATT_BLOCK = 128
DILATIONS = (1, 4, 16)
ATT_UNITS = 16
ATT_CHUNK = ATT_UNITS * ATT_BLOCK
ALIBI_MAX_BIAS = 8.0
MASKED = -1e30
VMEM_LIMIT = 56 * 1024 * 1024

ADAM_LR = 0.001
ADAM_B1 = 0.9
ADAM_B2 = 0.999
ADAM_EPS = 1e-08
ADAM_WD = 0.01
ADAM_STEP = 10

MESH_AXES = ("x", "y", "c")
ANY = pl.BlockSpec(memory_space=pl.ANY)


def _sds(shape, dtype):
    return jax.ShapeDtypeStruct(tuple(shape), dtype)


def _params(*sem):
    return pltpu.CompilerParams(dimension_semantics=sem, vmem_limit_bytes=VMEM_LIMIT)


def _sigmoid(v):
    return 1.0 / (1.0 + jnp.exp(-v))


def _row_tile(t, want):
    for cand in range(min(want, t) // 8 * 8, 0, -8):
        if t % cand == 0:
            return cand
    return t


def _mesh_pos():
    return lax.axis_index("x"), lax.axis_index("y"), lax.axis_index("c")


def _comm_sems():
    return [pltpu.SemaphoreType.DMA((7,)), pltpu.SemaphoreType.DMA((7,)), pltpu.SemaphoreType.DMA(())]


def _gather_phases(x_ref, out_ref, send_sems, recv_sems, local_sem):
    x, y, c = _mesh_pos()
    me, sibling = (x, y, c), (x, y, 1 - c)
    chips = [(1 - x, y), (x, 1 - y), (1 - x, 1 - y)]

    def slot(px, py, pc):
        return out_ref.at[4 * px + 2 * py + pc]

    def copy(k, block, to, src=None):
        return pltpu.make_async_remote_copy(
            src_ref=slot(*block) if src is None else src, dst_ref=slot(*block),
            send_sem=send_sems.at[k], recv_sem=recv_sems.at[k],
            device_id=to, device_id_type=pl.DeviceIdType.MESH)

    mine = pltpu.make_async_copy(x_ref, slot(*me), local_sem)
    first = [copy(0, me, sibling, src=x_ref)]
    first += [copy(1 + j, me, (*chip, c), src=x_ref) for j, chip in enumerate(chips)]
    passed = [copy(4 + j, (*chip, c), sibling) for j, chip in enumerate(chips)]

    def start():
        mine.start()
        for cp in first:
            cp.start()

    def forward():
        for j, chip in enumerate(chips):
            copy(1 + j, (*chip, c), me).wait_recv()
            passed[j].start()

    def finish():
        copy(0, sibling, me).wait_recv()
        for j, chip in enumerate(chips):
            copy(4 + j, (*chip, 1 - c), me).wait_recv()
        for cp in first + passed:
            cp.wait_send()
        mine.wait()

    return start, forward, finish


def _scatter_phases(p_ref, out_ref, send_sems, recv_sems, local_sem):
    x, y, c = _mesh_pos()
    me = 4 * x + 2 * y + c
    flips = [(fx, fy, fc) for fx in (0, 1) for fy in (0, 1) for fc in (0, 1)][1:]

    def copy(k, flip, receiving):
        px, py, pc = (1 - x if flip[0] else x, 1 - y if flip[1] else y, 1 - c if flip[2] else c)
        them = 4 * px + 2 * py + pc
        return pltpu.make_async_remote_copy(
            src_ref=p_ref.at[them], dst_ref=out_ref.at[them if receiving else me],
            send_sem=send_sems.at[k], recv_sem=recv_sems.at[k],
            device_id=(px, py, pc), device_id_type=pl.DeviceIdType.MESH)

    mine = pltpu.make_async_copy(p_ref.at[me], out_ref.at[me], local_sem)

    def start():
        mine.start()
        for k, flip in enumerate(flips):
            copy(k, flip, False).start()

    def finish():
        for k, flip in enumerate(flips):
            copy(k, flip, True).wait_recv()
            copy(k, flip, False).wait_send()
        mine.wait()

    return start, None, finish


class Rider(NamedTuple):
    kind: str
    src: jax.Array

    def out_shape(self):
        shape = (N_DEV,) + self.src.shape if self.kind == "gather" else self.src.shape
        return _sds(shape, self.src.dtype)


def _rider_hooks(riders, in_refs, out_refs, sem_refs, step, n_steps):
    phases = [(_gather_phases if r.kind == "gather" else _scatter_phases)(
                  in_refs[i], out_refs[i], *sem_refs[3 * i:3 * i + 3]) for i, r in enumerate(riders)]

    def begin():
        for start, forward, _ in phases:
            pl.when(step == 0)(start)
            if forward is not None:
                pl.when(step == (7 * n_steps) // 8)(forward)

    def end():
        for _, _, finish in phases:
            pl.when(step == n_steps - 1)(finish)

    return begin, end


def _split_refs(refs, n_in, n_out, n_scratch, n_riders):
    pos, parts = 0, []
    for n in (n_in, n_riders, n_out, n_riders, n_scratch, 3 * n_riders):
        parts.append(refs[pos:pos + n])
        pos += n
    return parts


def all_gather(shard, name):
    def body(x_ref, out_ref, send_sems, recv_sems, local_sem):
        start, forward, finish = _gather_phases(x_ref, out_ref, send_sems, recv_sems, local_sem)
        start()
        forward()
        finish()

    return pl.pallas_call(
        body, name=name, out_shape=_sds((N_DEV,) + shard.shape, shard.dtype),
        in_specs=[ANY], out_specs=ANY, scratch_shapes=_comm_sems(),
    )(shard)


def adamw(w, m, v, parts, name):
    n_parts, rows, cols = parts.shape
    tr = _row_tile(rows, 128)
    c1 = 1.0 - ADAM_B1 ** ADAM_STEP
    c2 = 1.0 - ADAM_B2 ** ADAM_STEP

    def body(w_ref, m_ref, v_ref, p_ref, g_ref, d_ref, nm_ref, nv_ref):
        g = p_ref[0].astype(F32)
        for s in range(1, n_parts):
            g = g + p_ref[s].astype(F32)
        nm = ADAM_B1 * m_ref[...] + (1.0 - ADAM_B1) * g
        nv = ADAM_B2 * v_ref[...] + (1.0 - ADAM_B2) * (g * g)
        delta = -ADAM_LR * ((nm / c1) / (jnp.sqrt(nv / c2) + ADAM_EPS) + ADAM_WD * w_ref[...])
        g_ref[...] = g
        d_ref[...] = delta
        nm_ref[...] = nm
        nv_ref[...] = nv

    mat = pl.BlockSpec((None, tr, cols), lambda i: (0, i, 0))
    return pl.pallas_call(
        body, name=name, grid=(rows // tr,),
        in_specs=[mat, mat, mat, pl.BlockSpec((n_parts, tr, cols), lambda i: (0, i, 0))],
        out_specs=[mat, mat, mat, mat],
        out_shape=[_sds((1, rows, cols), F32)] * 4,
        compiler_params=_params("parallel"),
    )(w, m, v, parts)


_NN = (((1,), (0,)), ((), ()))
_NT = (((1,), (1,)), ((), ()))
_TN = (((0,), (0,)), ((), ()))


def mm_cols(name, a, b_list, b_specs, nt, extras, extra_specs, out_shapes, out_specs, epilogue, n_blk, riders=(),
            tm_want=512):
    t_len, k_len = a.shape
    tm = _row_tile(t_len, tm_want)
    nb, ne, n_out, nr = len(b_list), len(extras), len(out_shapes), len(riders)
    t_steps = t_len // tm
    n_cols = b_specs[0].block_shape[-2 if nt else -1]

    def body(*refs):
        ins, r_in, outs, r_out, _, r_sem = _split_refs(refs, 1 + nb + ne, n_out, 0, nr)
        step = pl.program_id(0) * t_steps + pl.program_id(1)
        begin, end = _rider_hooks(riders, r_in, r_out, r_sem, step, n_blk * t_steps)
        begin()
        av = ins[0][...]
        for c0 in range(0, n_cols, MXU_WIDTH):
            cols = slice(c0, min(c0 + MXU_WIDTH, n_cols))
            accs = [lax.dot_general(av, br[cols, :] if nt else br[:, cols], _NT if nt else _NN,
                                    preferred_element_type=F32) for br in ins[1:1 + nb]]
            epilogue(accs, ins[1 + nb:], outs, cols)
        end()

    res = pl.pallas_call(
        body, name=name, grid=(n_blk, t_steps),
        in_specs=([pl.BlockSpec((tm, k_len), lambda j, t: (t, 0))] + list(b_specs) + list(extra_specs(tm))
                  + [ANY] * nr),
        out_specs=list(out_specs(tm)) + [ANY] * nr,
        out_shape=list(out_shapes) + [r.out_shape() for r in riders],
        scratch_shapes=_comm_sems() * nr,
        compiler_params=_params("arbitrary", "arbitrary"),
    )(a, *b_list, *extras, *[r.src for r in riders])
    return res[:n_out], res[n_out:]


def mm_reduce(name, a_list, a_specs, b_list, b_specs, nt, res, scale, t_len, n_len, n_blk, riders=(), tm_want=512):
    tm = _row_tile(t_len, tm_want)
    na, nr = len(a_list), len(riders)
    has_res = res is not None
    t_steps = t_len // tm

    def body(*refs):
        ins, r_in, outs, r_out, _, r_sem = _split_refs(refs, 2 * na + has_res, 1, 0, nr)
        o_ref = outs[0]
        j = pl.program_id(1)
        step = pl.program_id(0) * n_blk + j
        begin, end = _rider_hooks(riders, r_in, r_out, r_sem, step, t_steps * n_blk)
        begin()

        part = None
        for ar, br in zip(ins[:na], ins[na:2 * na]):
            d = lax.dot_general(ar[...], br[...], _NT if nt else _NN, preferred_element_type=F32)
            part = d if part is None else part + d

        @pl.when(j == 0)
        def _():
            o_ref[...] = part

        @pl.when(j > 0)
        def _():
            o_ref[...] += part

        if has_res or scale != 1.0:
            @pl.when(j == n_blk - 1)
            def _():
                val = o_ref[...] * scale if scale != 1.0 else o_ref[...]
                o_ref[...] = ins[2 * na][...] + val if has_res else val

        end()

    row = pl.BlockSpec((tm, n_len), lambda t, j: (t, 0))
    out = pl.pallas_call(
        body, name=name, grid=(t_steps, n_blk),
        in_specs=list(a_specs(tm)) + list(b_specs) + ([row] if has_res else []) + [ANY] * nr,
        out_specs=[row] + [ANY] * nr,
        out_shape=[_sds((t_len, n_len), F32)] + [r.out_shape() for r in riders],
        scratch_shapes=_comm_sems() * nr,
        compiler_params=_params("arbitrary", "arbitrary"),
    )(*a_list, *b_list, *([res] if has_res else []), *[r.src for r in riders])
    return out[0], out[1:]


def mm_tn(name, x, x_spec, dy_list, dy_specs, out_shapes, out_specs, scale, t_len, n_blk, riders=(),
          x_transposed=False):
    tt = _row_tile(t_len, 2048)
    nd, nr = len(dy_list), len(riders)
    t_steps = t_len // tt
    acc_shapes = [pltpu.VMEM(spec.block_shape[-2:], F32) for spec in out_specs]

    def body(*refs):
        ins, r_in, outs, r_out, accs, r_sem = _split_refs(refs, 1 + nd, nd, nd, nr)
        t = pl.program_id(1)
        step = pl.program_id(0) * t_steps + t
        begin, end = _rider_hooks(riders, r_in, r_out, r_sem, step, n_blk * t_steps)
        begin()
        xv = ins[0][...]
        for dr, acc in zip(ins[1:], accs):
            d = lax.dot_general(xv, dr[...], _NN if x_transposed else _TN, preferred_element_type=F32)

            @pl.when(t == 0)
            def _():
                acc[...] = d

            @pl.when(t > 0)
            def _():
                acc[...] += d

        @pl.when(t == t_steps - 1)
        def _():
            for acc, orf in zip(accs, outs):
                val = acc[...] * scale if scale != 1.0 else acc[...]
                orf[...] = val.astype(orf.dtype)

        end()

    res = pl.pallas_call(
        body, name=name, grid=(n_blk, t_steps),
        in_specs=[x_spec(tt)] + list(dy_specs(tt)) + [ANY] * nr,
        out_specs=list(out_specs) + [ANY] * nr,
        out_shape=list(out_shapes) + [r.out_shape() for r in riders],
        scratch_shapes=acc_shapes + _comm_sems() * nr,
        compiler_params=_params("arbitrary", "arbitrary"),
    )(x, *dy_list, *[r.src for r in riders])
    return res[:nd], res[nd:]


def rms_fwd(x, g, name):
    t_len, d = x.shape
    tm = _row_tile(t_len, 512)

    def body(x_ref, g_ref, h_ref, ht_ref):
        xv = x_ref[...]
        r = lax.rsqrt(jnp.mean(xv * xv, axis=-1, keepdims=True) + EPS)
        hv = xv * r * g_ref[...]
        h_ref[...] = hv.astype(BF16)
        ht_ref[...] = hv.T.astype(BF16)

    row = pl.BlockSpec((tm, d), lambda i: (i, 0))
    return pl.pallas_call(
        body, name=name, grid=(t_len // tm,),
        in_specs=[row, pl.BlockSpec((1, d), lambda i: (0, 0))],
        out_specs=[row, pl.BlockSpec((d, tm), lambda i: (0, i))],
        out_shape=[_sds((t_len, d), BF16), _sds((d, t_len), BF16)],
        compiler_params=_params("parallel"),
    )(x, g)


def rms_bwd(x, g, dh, dres, name):
    t_len, d = x.shape
    tm = _row_tile(t_len, 512)

    def body(x_ref, g_ref, dh_ref, dr_ref, dx_ref, dxb_ref, dg_ref):
        i = pl.program_id(0)
        xv = x_ref[...]
        r = lax.rsqrt(jnp.mean(xv * xv, axis=-1, keepdims=True) + EPS)
        xh = xv * r
        dhv = dh_ref[...]

        @pl.when(i == 0)
        def _():
            dg_ref[...] = jnp.zeros_like(dg_ref)

        dg_ref[...] += jnp.sum(dhv * xh, axis=0, keepdims=True)
        dxh = dhv * g_ref[...]
        dx = dr_ref[...] + r * (dxh - xh * jnp.mean(dxh * xh, axis=-1, keepdims=True))
        dx_ref[...] = dx
        dxb_ref[...] = dx.astype(BF16)

    row = pl.BlockSpec((tm, d), lambda i: (i, 0))
    vec = pl.BlockSpec((1, d), lambda i: (0, 0))
    return pl.pallas_call(
        body, name=name, grid=(t_len // tm,),
        in_specs=[row, vec, row, row],
        out_specs=[row, row, vec],
        out_shape=[_sds((t_len, d), F32), _sds((t_len, d), BF16), _sds((1, d), F32)],
        compiler_params=_params("arbitrary"),
    )(x, g, dh, dres)


def loss_head(y, target, name):
    t_len, d = y.shape
    tm = _row_tile(t_len, 512)

    def body(y_ref, t_ref, l_ref, dy_ref, dyb_ref):
        i = pl.program_id(0)
        err = y_ref[...] - t_ref[...]

        @pl.when(i == 0)
        def _():
            l_ref[...] = jnp.zeros_like(l_ref)

        rows = jnp.sum(err * err, axis=-1, keepdims=True) * (1.0 / d)
        l_ref[...] += 0.5 * jnp.sum(rows, axis=0, keepdims=True)
        dy = err * (1.0 / d)
        dy_ref[...] = dy
        dyb_ref[...] = dy.astype(BF16)

    row = pl.BlockSpec((tm, d), lambda i: (i, 0))
    return pl.pallas_call(
        body, name=name, grid=(t_len // tm,),
        in_specs=[row, row],
        out_specs=[pl.BlockSpec((8, LANES), lambda i: (0, 0)), row, row],
        out_shape=[_sds((8, LANES), F32), _sds((t_len, d), F32), _sds((t_len, d), BF16)],
        compiler_params=_params("arbitrary"),
    )(y, target)


def _conv_specs(tm, ch):
    per = tm // HALO
    cur = lambda cb: pl.BlockSpec((tm, ch), lambda i: (i, cb))
    prev = lambda cb: pl.BlockSpec((HALO, ch), lambda i: (jnp.maximum(i * per - 1, 0), cb))
    return [cur(0), cur(1), prev(0), prev(1)]


def _tap_scratch(rows, ch):
    return pltpu.VMEM((SUBLANES, rows + SUBLANES, ch), F32)


def _shifted_copies(buf, rows):
    buf[0, rows:rows + SUBLANES, :] = jnp.zeros((SUBLANES, buf.shape[2]), F32)
    for s in range(1, SUBLANES):
        buf[s, 0:rows, :] = buf[0, pl.ds(s, rows), :]


def _tap_rows(buf, off):
    shift = off % SUBLANES
    return buf[shift, off - shift:off - shift + ROW_CHUNK, :]


def _fill_glu(ext, a_ref, gt_ref, ap_ref, gp_ref, i, tm):
    vp = ap_ref[...] * _sigmoid(gp_ref[...])
    ext[0, 0:HALO, :] = jnp.where(i > 0, vp, 0.0)
    ext[0, HALO:HALO + tm, :] = a_ref[...] * _sigmoid(gt_ref[...])
    _shifted_copies(ext, HALO + tm)


def _conv_rows(ext, w_ref, b_ref, r0):
    acc = jnp.broadcast_to(b_ref[...], (ROW_CHUNK, b_ref.shape[1]))
    for k in range(CONV_WIDTH):
        acc = acc + w_ref[k:k + 1, :] * _tap_rows(ext, r0 + HALO - (CONV_WIDTH - 1) + k)
    return acc


def _layer_norm(yv):
    mu = jnp.mean(yv, axis=-1, keepdims=True)
    cen = yv - mu
    var = jnp.mean(cen * cen, axis=-1, keepdims=True)
    rstd = lax.rsqrt(var + EPS)
    return cen * rstd, rstd


def conv_fwd(z, w, b, lg, lb, name):
    t_len = z.shape[0]
    ch = w.shape[1]
    tm = _row_tile(t_len, 256)

    def body(a_ref, gt_ref, ap_ref, gp_ref, w_ref, b_ref, lg_ref, lb_ref, y_ref, pre_ref, ext):
        i = pl.program_id(0)
        _fill_glu(ext, a_ref, gt_ref, ap_ref, gp_ref, i, tm)
        for r0 in range(0, tm, ROW_CHUNK):
            pre = _conv_rows(ext, w_ref, b_ref, r0)
            pre_ref[r0:r0 + ROW_CHUNK, :] = pre
            xh, _ = _layer_norm(pre)
            u = xh * lg_ref[...] + lb_ref[...]
            y_ref[r0:r0 + ROW_CHUNK, :] = (u * _sigmoid(u)).astype(BF16)

    vec = pl.BlockSpec((1, ch), lambda i: (0, 0))
    row = pl.BlockSpec((tm, ch), lambda i: (i, 0))
    return pl.pallas_call(
        body, name=name, grid=(t_len // tm,),
        in_specs=_conv_specs(tm, ch) + [pl.BlockSpec((32, ch), lambda i: (0, 0)), vec, vec, vec],
        out_specs=[row, row],
        out_shape=[_sds((t_len, ch), BF16), _sds((t_len, ch), F32)],
        scratch_shapes=[_tap_scratch(HALO + tm, ch)],
        compiler_params=_params("parallel"),
    )(z, z, z, z, w, b, lg, lb)


def conv_bwd_norm(pre, dy_cat, lg, lb, name):
    t_len, ch = pre.shape
    tm = _row_tile(t_len, 256)

    def body(pre_ref, dy_ref, lg_ref, lb_ref, dc_ref, dlg_ref, dlb_ref, db_ref):
        i = pl.program_id(0)

        @pl.when(i == 0)
        def _():
            dlg_ref[...] = jnp.zeros_like(dlg_ref)
            dlb_ref[...] = jnp.zeros_like(dlb_ref)
            db_ref[...] = jnp.zeros_like(db_ref)

        for r0 in range(0, tm, ROW_CHUNK):
            xh, rstd = _layer_norm(pre_ref[r0:r0 + ROW_CHUNK, :])
            u = xh * lg_ref[...] + lb_ref[...]
            sg = _sigmoid(u)
            du = dy_ref[r0:r0 + ROW_CHUNK, :] * (sg * (1.0 + u * (1.0 - sg)))
            dlg_ref[...] += jnp.sum(du * xh, axis=0, keepdims=True)
            dlb_ref[...] += jnp.sum(du, axis=0, keepdims=True)
            dxh = du * lg_ref[...]
            dc = rstd * (dxh - jnp.mean(dxh, axis=-1, keepdims=True)
                         - xh * jnp.mean(dxh * xh, axis=-1, keepdims=True))
            db_ref[...] += jnp.sum(dc, axis=0, keepdims=True)
            dc_ref[r0:r0 + ROW_CHUNK, :] = dc

    vec = pl.BlockSpec((1, ch), lambda i: (0, 0))
    row = pl.BlockSpec((tm, ch), lambda i: (i, 0))
    return pl.pallas_call(
        body, name=name, grid=(t_len // tm,),
        in_specs=[row, row, vec, vec],
        out_specs=[row, vec, vec, vec],
        out_shape=[_sds((t_len, ch), F32)] + [_sds((1, ch), F32)] * 3,
        compiler_params=_params("arbitrary"),
    )(pre, dy_cat, lg, lb)


def conv_bwd_taps(z, dc, w, name):
    t_len = z.shape[0]
    ch = w.shape[1]
    tm = _row_tile(t_len, 256)
    per = tm // HALO
    n_tiles = t_len // tm
    last_halo = t_len // HALO - 1

    def body(a_ref, gt_ref, ap_ref, gp_ref, dc_ref, dn_ref, w_ref, dz_a_ref, dz_g_ref, dw_ref, ext, dext):
        i = pl.program_id(0)
        _fill_glu(ext, a_ref, gt_ref, ap_ref, gp_ref, i, tm)
        dext[0, 0:tm, :] = dc_ref[...]
        dext[0, tm:tm + HALO, :] = jnp.where(i < n_tiles - 1, dn_ref[...], 0.0)
        _shifted_copies(dext, tm + HALO)

        @pl.when(i == 0)
        def _():
            dw_ref[...] = jnp.zeros_like(dw_ref)

        for r0 in range(0, tm, ROW_CHUNK):
            dcv = dext[0, r0:r0 + ROW_CHUNK, :]
            dv = jnp.zeros((ROW_CHUNK, ch), F32)
            for k in range(CONV_WIDTH):
                dv = dv + w_ref[k:k + 1, :] * _tap_rows(dext, r0 + (CONV_WIDTH - 1) - k)
                prod = dcv * _tap_rows(ext, r0 + HALO - (CONV_WIDTH - 1) + k)
                fold = prod[0:8]
                for s in range(8, ROW_CHUNK, 8):
                    fold = fold + prod[s:s + 8]
                dw_ref[k] += fold
            av = a_ref[r0:r0 + ROW_CHUNK, :]
            sg = _sigmoid(gt_ref[r0:r0 + ROW_CHUNK, :])
            dz_a_ref[r0:r0 + ROW_CHUNK, :] = (dv * sg).astype(BF16)
            dz_g_ref[r0:r0 + ROW_CHUNK, :] = (dv * av * sg * (1.0 - sg)).astype(BF16)

    row = pl.BlockSpec((tm, ch), lambda i: (i, 0))
    nxt = pl.BlockSpec((HALO, ch), lambda i: (jnp.minimum((i + 1) * per, last_halo), 0))
    return pl.pallas_call(
        body, name=name, grid=(n_tiles,),
        in_specs=_conv_specs(tm, ch) + [row, nxt, pl.BlockSpec((32, ch), lambda i: (0, 0))],
        out_specs=[row, row, pl.BlockSpec((32, 8, ch), lambda i: (0, 0, 0))],
        out_shape=[_sds((t_len, ch), BF16), _sds((t_len, ch), BF16), _sds((32, 8, ch), F32)],
        scratch_shapes=[_tap_scratch(HALO + tm, ch), _tap_scratch(tm + HALO, ch)],
        compiler_params=_params("arbitrary"),
    )(z, z, z, z, dc, dc, w)


def _head_masks(rows):
    lane = lax.broadcasted_iota(jnp.int32, (rows, LANES), 1)
    low = lane < HEAD_DIM
    return low, jnp.logical_not(low)


def _per_head_mean(val, low):
    s_low = jnp.sum(jnp.where(low, val, 0.0), axis=-1, keepdims=True)
    s_high = jnp.sum(jnp.where(low, 0.0, val), axis=-1, keepdims=True)
    return jnp.where(low, s_low, s_high) * (1.0 / HEAD_DIM)


def qk_norm_fwd(z, g2, ch, name):
    t_len = z.shape[0]
    tm = _row_tile(t_len, 512)

    def body(z_ref, g_ref, o_ref):
        low, _ = _head_masks(tm)
        for c0 in range(0, ch, LANES):
            cols = slice(c0, c0 + LANES)
            xv = z_ref[:, cols]
            r = lax.rsqrt(_per_head_mean(xv * xv, low) + EPS)
            o_ref[:, cols] = xv * r * g_ref[:, cols]

    return pl.pallas_call(
        body, name=name, grid=(t_len // tm, 2),
        in_specs=[pl.BlockSpec((tm, ch), lambda i, w: (i, 2 + w)),
                  pl.BlockSpec((1, ch), lambda i, w: (0, w))],
        out_specs=pl.BlockSpec((tm, ch), lambda i, w: (i, w)),
        out_shape=_sds((t_len, 2 * ch), F32),
        compiler_params=_params("parallel", "parallel"),
    )(z, g2)


def qk_norm_bwd(z, g, d_list, z_off, ch, name):
    t_len = z.shape[0]
    tm = _row_tile(t_len, 512)
    nd = len(d_list)

    def body(*refs):
        z_ref, g_ref, d_refs = refs[0], refs[1], refs[2:2 + nd]
        dz_ref, dg_ref = refs[2 + nd], refs[3 + nd]
        low, _ = _head_masks(tm)

        @pl.when(pl.program_id(0) == 0)
        def _():
            dg_ref[...] = jnp.zeros_like(dg_ref)

        for c0 in range(0, ch, LANES):
            cols = slice(c0, c0 + LANES)
            xv = z_ref[:, cols]
            r = lax.rsqrt(_per_head_mean(xv * xv, low) + EPS)
            xh = xv * r
            dy = d_refs[0][:, cols]
            for dr in d_refs[1:]:
                dy = dy + dr[:, cols]
            dg_ref[...] += jnp.sum(dy * xh, axis=0, keepdims=True)
            dxh = dy * g_ref[...]
            dz_ref[:, cols] = (r * (dxh - xh * _per_head_mean(dxh * xh, low))).astype(BF16)

    blk = pl.BlockSpec((tm, ch), lambda i: (i, 0))
    return pl.pallas_call(
        body, name=name, grid=(t_len // tm,),
        in_specs=[pl.BlockSpec((tm, ch), lambda i: (i, z_off)),
                  pl.BlockSpec((1, LANES), lambda i: (0, 0))] + [blk] * nd,
        out_specs=[blk, pl.BlockSpec((1, LANES), lambda i: (0, 0))],
        out_shape=[_sds((t_len, ch), BF16), _sds((1, LANES), F32)],
        compiler_params=_params("arbitrary"),
    )(z, g, *d_list)


def _alibi_bias(n_heads, dilation):
    slopes = 2.0 ** (-ALIBI_MAX_BIAS * jnp.arange(1, n_heads + 1, dtype=F32) / n_heads)
    qi = jnp.arange(ATT_BLOCK)[:, None]
    kj = jnp.arange(ATT_BLOCK)[None, :]
    dist_cur = (qi - kj).astype(F32)
    dist_prev = (ATT_BLOCK + qi - kj).astype(F32)
    cur = jnp.where((qi >= kj)[None], -slopes[:, None, None] * (dilation * dist_cur)[None], MASKED)
    prev = jnp.where((kj >= qi)[None], -slopes[:, None, None] * (dilation * dist_prev)[None], MASKED)
    return jnp.concatenate([prev, cur], axis=-1).astype(F32)


def _stack_heads(val, low, high):
    return jnp.concatenate([jnp.where(low, val, 0.0), jnp.where(high, val, 0.0)], axis=0).astype(BF16)


def _head_rows(val, low, high):
    other = pltpu.roll(val, HEAD_DIM, axis=1)
    rows = jnp.concatenate([jnp.where(low, val, other), jnp.where(high, val, other)], axis=0)
    return jnp.concatenate([rows, rows], axis=1)


def _unit_scores(q2, k2, b_ref, has_prev):
    s = lax.dot_general(q2, k2, _NT, preferred_element_type=F32)
    s = s + b_ref[...].reshape(2 * ATT_BLOCK, 2 * ATT_BLOCK)
    penalty = jnp.where(has_prev, 0.0, MASKED)
    return jnp.concatenate([s[:, :ATT_BLOCK] + penalty, s[:, ATT_BLOCK:]], axis=1)


def _strided_rows(r, dilation):
    per = ATT_CHUNK // dilation
    return pl.ds(r, per, stride=dilation) if dilation > 1 else pl.ds(0, per)


def _deinterleave(dst, src_ref, dilation, base=None, dtype=None):
    per = ATT_CHUNK // dilation
    for r in range(dilation):
        val = src_ref[_strided_rows(r, dilation), :]
        val = val if dtype is None else val.astype(dtype)
        if base is None:
            dst[r * per:(r + 1) * per, :] = val
        else:
            dst[pl.ds(pl.multiple_of(base + r * per, ATT_BLOCK), per), :] = val


def _unit_rows(u, c, nb, base, pbase):
    in_chunk = lax.rem(u, jnp.int32(nb)) > 0
    has_prev = jnp.logical_or(in_chunk, c > 0)
    urow = pl.multiple_of(u * ATT_BLOCK, ATT_BLOCK)
    crow = pl.multiple_of(base + u * ATT_BLOCK, ATT_BLOCK)
    prow = pl.multiple_of(jnp.where(in_chunk, base + (u - 1) * ATT_BLOCK,
                                    pbase + (u + nb - 1) * ATT_BLOCK), ATT_BLOCK)
    return in_chunk, has_prev, urow, crow, prow


def _interleave(dst_ref, src, dilation, base=None):
    per = ATT_CHUNK // dilation
    for r in range(dilation):
        if base is None:
            val = src[r * per:(r + 1) * per, :]
        else:
            val = src[pl.ds(pl.multiple_of(base + r * per, ATT_BLOCK), per), :]
        dst_ref[_strided_rows(r, dilation), :] = val


def attn_fwd(qk, z, dilation, ch, name):
    t_len = qk.shape[0]
    pairs = ch // LANES
    nc = t_len // ATT_CHUNK
    nb = ATT_UNITS // dilation
    scale = 1.0 / math.sqrt(HEAD_DIM)
    bias = _alibi_bias(2 * pairs, dilation)

    def body(q_ref, k_ref, v_ref, b_ref, o_ref, l_ref, qd, kx, vx, od, ld):
        c = pl.program_id(1)
        slot = lax.rem(c, jnp.int32(2))
        base, pbase = slot * ATT_CHUNK, (1 - slot) * ATT_CHUNK

        @pl.when(c == 0)
        def _():
            kx[...] = jnp.zeros_like(kx)
            vx[...] = jnp.zeros_like(vx)

        _deinterleave(qd, q_ref, dilation)
        _deinterleave(kx, k_ref, dilation, base, BF16)
        _deinterleave(vx, v_ref, dilation, base, BF16)
        low, high = _head_masks(ATT_BLOCK)

        def unit(u, carry):
            _, has_prev, urow, crow, prow = _unit_rows(u, c, nb, base, pbase)
            q2 = _stack_heads(qd[pl.ds(urow, ATT_BLOCK), :] * scale, low, high)
            k2 = jnp.concatenate([kx[pl.ds(prow, ATT_BLOCK), :], kx[pl.ds(crow, ATT_BLOCK), :]], axis=0)
            v2 = jnp.concatenate([vx[pl.ds(prow, ATT_BLOCK), :], vx[pl.ds(crow, ATT_BLOCK), :]], axis=0)
            s = _unit_scores(q2, k2, b_ref, has_prev)
            mx = jnp.max(s, axis=-1, keepdims=True)
            e = jnp.exp(s - mx)
            den = jnp.sum(e, axis=-1, keepdims=True)
            acc = lax.dot_general(e.astype(BF16), v2, _NN, preferred_element_type=F32) / den
            lse = jnp.broadcast_to(mx + jnp.log(den), acc.shape)
            od[pl.ds(urow, ATT_BLOCK), :] = jnp.where(low, acc[:ATT_BLOCK], acc[ATT_BLOCK:])
            ld[pl.ds(urow, ATT_BLOCK), :] = jnp.where(low, lse[:ATT_BLOCK], lse[ATT_BLOCK:])
            return carry

        lax.fori_loop(0, ATT_UNITS, unit, 0, unroll=8)
        _interleave(o_ref, od, dilation)
        _interleave(l_ref, ld, dilation)

    blk = (ATT_CHUNK, LANES)
    bias_spec = pl.BlockSpec((2, ATT_BLOCK, 2 * ATT_BLOCK), lambda p, c: (p, 0, 0))
    out_spec = pl.BlockSpec(blk, lambda p, c: (c, p))
    return pl.pallas_call(
        body, name=name, grid=(pairs, nc),
        in_specs=[pl.BlockSpec(blk, lambda p, c: (c, p)),
                  pl.BlockSpec(blk, lambda p, c: (c, pairs + p)),
                  pl.BlockSpec(blk, lambda p, c: (c, 4 * pairs + p)),
                  bias_spec],
        out_specs=[out_spec, out_spec],
        out_shape=[_sds((t_len, ch), F32)] * 2,
        scratch_shapes=[pltpu.VMEM((ATT_CHUNK, LANES), F32),
                        pltpu.VMEM((2 * ATT_CHUNK, LANES), BF16), pltpu.VMEM((2 * ATT_CHUNK, LANES), BF16),
                        pltpu.VMEM((ATT_CHUNK, LANES), F32), pltpu.VMEM((ATT_CHUNK, LANES), F32)],
        compiler_params=_params("arbitrary", "arbitrary"),
    )(qk, qk, z, bias)


def attn_combine(outs, lses, name):
    t_len, ch = outs[0].shape
    tm = _row_tile(t_len, 512)

    def body(o1, o2, o3, l1, l2, l3, out_ref, outb_ref, lg_ref):
        a, b, c = l1[...], l2[...], l3[...]
        mx = jnp.maximum(jnp.maximum(a, b), c)
        tot = mx + jnp.log(jnp.exp(a - mx) + jnp.exp(b - mx) + jnp.exp(c - mx))
        val = jnp.exp(a - tot) * o1[...] + jnp.exp(b - tot) * o2[...] + jnp.exp(c - tot) * o3[...]
        out_ref[...] = val
        outb_ref[...] = val.astype(BF16)
        lg_ref[...] = tot

    row = pl.BlockSpec((tm, ch), lambda i: (i, 0))
    return pl.pallas_call(
        body, name=name, grid=(t_len // tm,),
        in_specs=[row] * 6, out_specs=[row] * 3,
        out_shape=[_sds((t_len, ch), F32), _sds((t_len, ch), BF16), _sds((t_len, ch), F32)],
        compiler_params=_params("parallel"),
    )(*outs, *lses)


def attn_bwd(qk, z, dy_cat, out, lg, dilation, ch, name):
    t_len = qk.shape[0]
    pairs = ch // LANES
    nc = t_len // ATT_CHUNK
    nb = ATT_UNITS // dilation
    scale = 1.0 / math.sqrt(HEAD_DIM)
    bias = _alibi_bias(2 * pairs, dilation)

    def body(q_ref, k_ref, v_ref, do_ref, out_ref, lg_ref, b_ref, dq_ref, dk_ref, dv_ref,
             qd, dod, lgd, dld, dl_nat, kx, vx, dkx, dvx, dqd):
        c = pl.program_id(1)
        slot = lax.rem(c, jnp.int32(2))
        base, pbase = slot * ATT_CHUNK, (1 - slot) * ATT_CHUNK

        @pl.when(c == 0)
        def _():
            for ref in (kx, vx, dkx, dvx):
                ref[...] = jnp.zeros_like(ref)

        @pl.when(c < nc)
        def _():
            low_all, _ = _head_masks(ATT_CHUNK)
            dl_nat[...] = _per_head_mean(do_ref[...] * out_ref[...], low_all) * float(HEAD_DIM)
            _deinterleave(qd, q_ref, dilation)
            _deinterleave(dod, do_ref, dilation)
            _deinterleave(lgd, lg_ref, dilation)
            _deinterleave(dld, dl_nat, dilation)
            _deinterleave(kx, k_ref, dilation, base, BF16)
            _deinterleave(vx, v_ref, dilation, base, BF16)
            cur = pl.ds(pl.multiple_of(base, ATT_CHUNK), ATT_CHUNK)
            dkx[cur, :] = jnp.zeros((ATT_CHUNK, LANES), F32)
            dvx[cur, :] = jnp.zeros((ATT_CHUNK, LANES), F32)
            low, high = _head_masks(ATT_BLOCK)

            def unit(u, carry):
                _, has_prev, urow, crow, prow = _unit_rows(u, c, nb, base, pbase)
                rows = pl.ds(urow, ATT_BLOCK)
                q2 = _stack_heads(qd[rows, :] * scale, low, high)
                do2 = _stack_heads(dod[rows, :], low, high)
                lse = _head_rows(lgd[rows, :], low, high)
                delta = _head_rows(dld[rows, :], low, high)
                k2 = jnp.concatenate([kx[pl.ds(prow, ATT_BLOCK), :], kx[pl.ds(crow, ATT_BLOCK), :]], axis=0)
                v2 = jnp.concatenate([vx[pl.ds(prow, ATT_BLOCK), :], vx[pl.ds(crow, ATT_BLOCK), :]], axis=0)
                prob = jnp.exp(_unit_scores(q2, k2, b_ref, has_prev) - lse)
                dp = lax.dot_general(do2, v2, _NT, preferred_element_type=F32)
                ds = (prob * (dp - delta)).astype(BF16)
                dq2 = lax.dot_general(ds, k2, _NN, preferred_element_type=F32)
                dk2 = lax.dot_general(ds, q2, _TN, preferred_element_type=F32)
                dv2 = lax.dot_general(prob.astype(BF16), do2, _TN, preferred_element_type=F32)
                dqd[rows, :] = scale * jnp.where(low, dq2[:ATT_BLOCK], dq2[ATT_BLOCK:])
                dkx[pl.ds(prow, ATT_BLOCK), :] += dk2[:ATT_BLOCK]
                dkx[pl.ds(crow, ATT_BLOCK), :] += dk2[ATT_BLOCK:]
                dvx[pl.ds(prow, ATT_BLOCK), :] += dv2[:ATT_BLOCK]
                dvx[pl.ds(crow, ATT_BLOCK), :] += dv2[ATT_BLOCK:]
                return carry

            lax.fori_loop(0, ATT_UNITS, unit, 0, unroll=8)
            _interleave(dq_ref, dqd, dilation)

        @pl.when(c > 0)
        def _():
            _interleave(dk_ref, dkx, dilation, pbase)
            _interleave(dv_ref, dvx, dilation, pbase)

    blk = (ATT_CHUNK, LANES)
    here = lambda c: jnp.minimum(c, nc - 1)
    spec = lambda off: pl.BlockSpec(blk, lambda p, c: (here(c), off + p))
    late = pl.BlockSpec(blk, lambda p, c: (jnp.maximum(c - 1, 0), p))
    bias_spec = pl.BlockSpec((2, ATT_BLOCK, 2 * ATT_BLOCK), lambda p, c: (p, 0, 0))
    f32_chunk = pltpu.VMEM((ATT_CHUNK, LANES), F32)
    return pl.pallas_call(
        body, name=name, grid=(pairs, nc + 1),
        in_specs=[spec(0), spec(pairs), spec(4 * pairs), spec(pairs), spec(0), spec(0), bias_spec],
        out_specs=[spec(0), late, late],
        out_shape=[_sds((t_len, ch), F32)] * 3,
        scratch_shapes=[f32_chunk] * 5
                       + [pltpu.VMEM((2 * ATT_CHUNK, LANES), BF16)] * 2
                       + [pltpu.VMEM((2 * ATT_CHUNK, LANES), F32)] * 2 + [f32_chunk],
        compiler_params=_params("arbitrary", "arbitrary"),
    )(qk, qk, z, dy_cat, out, lg, bias)


def sum3_bf16(a, b, c, name):
    t_len, ch = a.shape
    tm = _row_tile(t_len, 512)

    def body(a_ref, b_ref, c_ref, o_ref):
        o_ref[...] = (a_ref[...] + b_ref[...] + c_ref[...]).astype(BF16)

    row = pl.BlockSpec((tm, ch), lambda i: (i, 0))
    return pl.pallas_call(
        body, name=name, grid=(t_len // tm,), in_specs=[row] * 3, out_specs=row,
        out_shape=_sds((t_len, ch), BF16), compiler_params=_params("parallel"),
    )(a, b, c)


def _blk3(rows, cols):
    return pl.BlockSpec((None, rows, cols), lambda j, t: (j, 0, 0))


def ffn_up(h, wg, wu, name, riders):
    t_len, d = h.shape
    n_blk, _, fj = wg.shape

    def epilogue(accs, e_refs, o_refs, cols):
        gate, up = accs
        o_refs[0][:, cols] = gate.astype(BF16)
        o_refs[1][:, cols] = up.astype(BF16)
        o_refs[2][:, cols] = (gate * _sigmoid(gate) * up).astype(BF16)

    act = lambda tm: pl.BlockSpec((None, tm, fj), lambda j, t: (j, t, 0))
    return mm_cols(name, h, [wg, wu], [_blk3(d, fj)] * 2, False, [], lambda tm: [],
                   [_sds((n_blk, t_len, fj), BF16)] * 3, lambda tm: [act(tm)] * 3, epilogue, n_blk, riders,
                   tm_want=1024)


def ffn_gate(h, wg, name, riders):
    t_len, d = h.shape
    n_blk, _, fj = wg.shape

    def epilogue(accs, e_refs, o_refs, cols):
        o_refs[0][:, cols] = accs[0].astype(BF16)

    act = lambda tm: pl.BlockSpec((None, tm, fj), lambda j, t: (j, t, 0))
    return mm_cols(name, h, [wg], [_blk3(d, fj)], False, [], lambda tm: [],
                   [_sds((n_blk, t_len, fj), BF16)], lambda tm: [act(tm)], epilogue, n_blk, riders, tm_want=1024)


def ffn_up_after_gate(h, wu, gate, name, riders):
    t_len, d = h.shape
    n_blk, _, fj = wu.shape

    def epilogue(accs, e_refs, o_refs, cols):
        gv = e_refs[0][:, cols].astype(F32)
        o_refs[0][:, cols] = accs[0].astype(BF16)
        o_refs[1][:, cols] = (gv * _sigmoid(gv) * accs[0]).astype(BF16)

    act = lambda tm: pl.BlockSpec((None, tm, fj), lambda j, t: (j, t, 0))
    return mm_cols(name, h, [wu], [_blk3(d, fj)], False, [gate], lambda tm: [act(tm)],
                   [_sds((n_blk, t_len, fj), BF16)] * 2, lambda tm: [act(tm)] * 2, epilogue, n_blk, riders,
                   tm_want=1024)


def ffn_down(act, wd, res, name, riders):
    n_blk, t_len, fj = act.shape
    d = wd.shape[2]
    return mm_reduce(name, [act], lambda tm: [pl.BlockSpec((None, tm, fj), lambda t, j: (j, t, 0))],
                     [wd], [pl.BlockSpec((None, fj, d), lambda t, j: (j, 0, 0))], False, res, 0.5,
                     t_len, d, n_blk, riders, tm_want=1024)


def ffn_bwd(ht, gate, up, act, wg, wu, wd, dyb, name):
    d, t_len = ht.shape
    n_blk, _, fj = act.shape

    def epilogue(accs, e_refs, o_refs, cols):
        d_act = 0.5 * accs[0]
        gv, uv = e_refs[0][:, cols].astype(F32), e_refs[1][:, cols].astype(F32)
        sg = _sigmoid(gv)
        o_refs[0][:, cols] = (d_act * uv * (sg * (1.0 + gv * (1.0 - sg)))).astype(BF16)
        o_refs[1][:, cols] = (d_act * gv * sg).astype(BF16)

    act_jt = lambda tm: pl.BlockSpec((None, tm, fj), lambda j, t: (j, t, 0))
    (d_gate, d_up), _ = mm_cols(name + "_dact", dyb, [wd], [_blk3(fj, d)], True, [gate, up],
                                lambda tm: [act_jt(tm)] * 2, [_sds((n_blk, t_len, fj), BF16)] * 2,
                                lambda tm: [act_jt(tm)] * 2, epilogue, n_blk, tm_want=1024)

    ht_spec = lambda tt: pl.BlockSpec((d, tt), lambda j, t: (0, t))
    (d_wg,), _ = mm_tn(name + "_dwg", ht, ht_spec, [d_gate], lambda tt: [act_jt(tt)],
                       [_sds((n_blk, d, fj), BF16)], [_blk3(d, fj)], 1.0, t_len, n_blk, x_transposed=True)
    (d_wu,), (recv_wg,) = mm_tn(name + "_dwu", ht, ht_spec, [d_up], lambda tt: [act_jt(tt)],
                                [_sds((n_blk, d, fj), BF16)], [_blk3(d, fj)], 1.0, t_len, n_blk,
                                [Rider("scatter", d_wg)], x_transposed=True)

    (d_wd,), (recv_wu,) = mm_tn(name + "_dwd", act, act_jt, [dyb],
                                lambda tt: [pl.BlockSpec((tt, d), lambda j, t: (t, 0))],
                                [_sds((n_blk, fj, d), BF16)], [_blk3(fj, d)], 0.5, t_len, n_blk,
                                [Rider("scatter", d_wu)])

    act_tj = lambda tm: pl.BlockSpec((None, tm, fj), lambda t, j: (j, t, 0))
    w_tj = pl.BlockSpec((None, d, fj), lambda t, j: (j, 0, 0))
    dh, (recv_wd,) = mm_reduce(
        name + "_dh", [d_gate, d_up], lambda tm: [act_tj(tm)] * 2,
        [wg, wu], [w_tj, w_tj], True, None, 1.0, t_len, d, n_blk,
        [Rider("scatter", d_wd)], tm_want=1024)
    return dh, recv_wg, recv_wu, recv_wd


def local_step(x, target, g1, wg1, wu1_s, wd1_s, gmix, win_s, conv_w, conv_b, ln_g, ln_b, gq, gk, wout_s, g3,
               wg2_s, wu2_s, wd2_s):
    t_len, d = x.shape
    ch = d // 2
    ij = win_s.shape[1]
    oj = wout_s.shape[0]
    n_blk = N_DEV

    h1, h1t = rms_fwd(x, g1, "rms1")
    (gate1,), (wu1,) = ffn_gate(h1, wg1, "ffn1_gate", [Rider("gather", wu1_s)])
    (up1, act1), (wd1, wg2) = ffn_up_after_gate(h1, wu1, gate1, "ffn1_up",
                                                [Rider("gather", wd1_s), Rider("gather", wg2_s)])
    x1, (win, wu2) = ffn_down(act1, wd1, x, "ffn1_down", [Rider("gather", win_s), Rider("gather", wu2_s)])

    h2, h2t = rms_fwd(x1, gmix, "rms_mix")

    def store_f32(accs, e_refs, o_refs, cols):
        o_refs[0][:, cols] = accs[0]

    d_in = n_blk * ij
    win_full = jnp.transpose(win, (1, 0, 2)).reshape(1, d, d_in)
    wide = d_in // W_IN_SPLIT
    (z,), (wd2, wout) = mm_cols(
        "w_in", h2, [win_full], [pl.BlockSpec((None, d, wide), lambda j, t: (0, 0, j))], False, [], lambda tm: [],
        [_sds((t_len, d_in), F32)],
        lambda tm: [pl.BlockSpec((tm, wide), lambda j, t: (t, j))], store_f32, W_IN_SPLIT,
        [Rider("gather", wd2_s), Rider("gather", wout_s)], tm_want=1024)

    conv_w32 = jnp.pad(conv_w, ((0, 32 - CONV_WIDTH), (0, 0)))
    y_conv, conv_pre = conv_fwd(z, conv_w32, conv_b, ln_g, ln_b, "conv_fwd")

    g2 = jnp.concatenate([jnp.tile(gq, (1, ch // HEAD_DIM)), jnp.tile(gk, (1, ch // HEAD_DIM))], axis=1)
    qk = qk_norm_fwd(z, g2, ch, "qk_norm")
    branch = [attn_fwd(qk, z, dil, ch, "attn_fwd_d%d" % dil) for dil in DILATIONS]
    att, att_b, lg = attn_combine([o for o, _ in branch], [l for _, l in branch], "attn_combine")

    y_cat = jnp.concatenate([y_conv, att_b], axis=1)
    wout_full = wout.reshape(1, n_blk * oj, d)
    x2, _ = mm_reduce(
        "w_out", [y_cat], lambda tm: [pl.BlockSpec((tm, n_blk * oj), lambda t, j: (t, 0))],
        [wout_full], [pl.BlockSpec((None, n_blk * oj, d), lambda t, j: (0, 0, 0))], False, x1, 1.0,
        t_len, d, 1)

    h3, h3t = rms_fwd(x2, g3, "rms3")
    (gate2, up2, act2), _ = ffn_up(h3, wg2, wu2, "ffn2_up", [])
    y, _ = ffn_down(act2, wd2, x2, "ffn2_down", [])

    loss_tile, dy, dyb = loss_head(y, target, "loss")

    dh3, recv_wg2, recv_wu2, recv_wd2 = ffn_bwd(h3t, gate2, up2, act2, wg2, wu2, wd2, dyb, "ffn2")
    dx2, dx2b, d_g3 = rms_bwd(x2, g3, dh3, dy, "rms3_bwd")

    (dy_cat,), _ = mm_cols("w_out_dy", dx2b, [wout_full], [_blk3(n_blk * oj, d)], True, [], lambda tm: [],
                           [_sds((t_len, n_blk * oj), F32)],
                           lambda tm: [pl.BlockSpec((tm, n_blk * oj), lambda j, t: (t, 0))], store_f32, 1,
                           tm_want=1024)
    (d_wout,), _ = mm_tn("w_out_dw", y_cat, lambda tt: pl.BlockSpec((tt, oj), lambda j, t: (t, j)),
                         [dx2b], lambda tt: [pl.BlockSpec((tt, d), lambda j, t: (t, 0))],
                         [_sds((n_blk, oj, d), BF16)], [_blk3(oj, d)], 1.0, t_len, n_blk)

    dc, d_lg, d_lb, d_cb = conv_bwd_norm(conv_pre, dy_cat, ln_g, ln_b, "conv_bwd_norm")
    dz_a, dz_g, d_cw8 = conv_bwd_taps(z, dc, conv_w32, "conv_bwd_taps")
    d_cw = jnp.sum(d_cw8, axis=1)[:CONV_WIDTH]

    grads = [attn_bwd(qk, z, dy_cat, att, lg, dil, ch, "attn_bwd_d%d" % dil) for dil in DILATIONS]
    gq_t = jnp.tile(gq, (1, LANES // HEAD_DIM))
    gk_t = jnp.tile(gk, (1, LANES // HEAD_DIM))
    dz_q, d_gq2 = qk_norm_bwd(z, gq_t, [g[0] for g in grads], 2, ch, "q_norm_bwd")
    dz_k, d_gk2 = qk_norm_bwd(z, gk_t, [g[1] for g in grads], 3, ch, "k_norm_bwd")
    d_gq = d_gq2[:, :HEAD_DIM] + d_gq2[:, HEAD_DIM:]
    d_gk = d_gk2[:, :HEAD_DIM] + d_gk2[:, HEAD_DIM:]
    dz_v = sum3_bf16(grads[0][2], grads[1][2], grads[2][2], "dv_sum")
    dzb = jnp.concatenate([dz_a, dz_g, dz_q, dz_k, dz_v], axis=1)

    (d_win,), (recv_wout,) = mm_tn(
        "w_in_dw", h2t, lambda tt: pl.BlockSpec((d, tt), lambda j, t: (0, t)),
        [dzb], lambda tt: [pl.BlockSpec((tt, ij), lambda j, t: (t, j))],
        [_sds((n_blk, d, ij), BF16)], [_blk3(d, ij)], 1.0, t_len, n_blk, [Rider("scatter", d_wout)],
        x_transposed=True)
    tall = d // W_IN_SPLIT
    (dh2,), (recv_win,) = mm_cols(
        "w_in_dh", dzb, [win_full], [pl.BlockSpec((None, tall, d_in), lambda j, t: (0, j, 0))], True, [],
        lambda tm: [], [_sds((t_len, d), F32)],
        lambda tm: [pl.BlockSpec((tm, tall), lambda j, t: (t, j))], store_f32, W_IN_SPLIT,
        [Rider("scatter", d_win)], tm_want=1024)
    dx1, dx1b, d_gmix = rms_bwd(x1, gmix, dh2, dx2, "rms_mix_bwd")

    dh1, recv_wg1, recv_wu1, recv_wd1 = ffn_bwd(h1t, gate1, up1, act1, wg1, wu1, wd1, dx1b, "ffn1")
    grad_x, _, d_g1 = rms_bwd(x, g1, dh1, dx1, "rms1_bwd")

    big = dict(ffn1_w_gate=recv_wg1, ffn1_w_up=recv_wu1, ffn1_w_down=recv_wd1, w_in=recv_win, w_out=recv_wout,
               ffn2_w_gate=recv_wg2, ffn2_w_up=recv_wu2, ffn2_w_down=recv_wd2)
    small = dict(g1=d_g1, gmix=d_gmix, g3=d_g3, conv_b=d_cb, ln_g=d_lg, ln_b=d_lb, gq=d_gq, gk=d_gk, conv_w=d_cw)
    return loss_tile[0, 0], grad_x, big, small


SMALL_ROWS = 48


def _pack_small(ch, g1, gmix, g3, conv_b, ln_g, ln_b, gq, gk, conv_w):
    pad_head = lambda v: jnp.pad(v, ((0, 0), (0, ch - v.shape[1])))
    rows = [g1.reshape(2, ch), gmix.reshape(2, ch), g3.reshape(2, ch), conv_b, ln_g, ln_b,
            pad_head(gq), pad_head(gk), conv_w]
    packed = jnp.concatenate(rows, axis=0)
    return jnp.pad(packed, ((0, SMALL_ROWS - packed.shape[0]), (0, 0)))


def _unpack_small(packed, d):
    return dict(g1=packed[0:2].reshape(1, d), gmix=packed[2:4].reshape(1, d), g3=packed[4:6].reshape(1, d),
                conv_b=packed[6:7], ln_g=packed[7:8], ln_b=packed[8:9],
                gq=packed[9:10, :HEAD_DIM], gk=packed[10:11, :HEAD_DIM])


def kernel(x, ffn1_norm_g, ffn1_w_gate, ffn1_w_up, ffn1_w_down, mix_norm_g, w_in, conv_w_dw, conv_b_dw, conv_ln_g, conv_ln_b, q_norm_g, k_norm_g, w_out, ffn2_norm_g, ffn2_w_gate, ffn2_w_up, ffn2_w_down, loss_target, m_ffn1_norm_g, m_ffn1_w_gate, m_ffn1_w_up, m_ffn1_w_down, m_mix_norm_g, m_w_in, m_conv_w_dw, m_conv_b_dw, m_conv_ln_g, m_conv_ln_b, m_q_norm_g, m_k_norm_g, m_w_out, m_ffn2_norm_g, m_ffn2_w_gate, m_ffn2_w_up, m_ffn2_w_down, v_ffn1_norm_g, v_ffn1_w_gate, v_ffn1_w_up, v_ffn1_w_down, v_mix_norm_g, v_w_in, v_conv_w_dw, v_conv_b_dw, v_conv_ln_g, v_conv_ln_b, v_q_norm_g, v_k_norm_g, v_w_out, v_ffn2_norm_g, v_ffn2_w_gate, v_ffn2_w_up, v_ffn2_w_down):
    d = x.shape[-1]
    ch = d // 2
    me = 4 * lax.axis_index("x") + 2 * lax.axis_index("y") + lax.axis_index("c")

    shard = lambda w: w[0].astype(BF16)
    wg1 = all_gather(shard(ffn1_w_gate), "ag_wg1")
    cw_all = all_gather(conv_w_dw[0], "ag_convw")
    conv_w = jnp.transpose(cw_all, (1, 0, 2)).reshape(CONV_WIDTH, ch)

    loss_part, grad_x, big, small = local_step(
        x[0], loss_target[0], ffn1_norm_g, wg1, shard(ffn1_w_up), shard(ffn1_w_down), mix_norm_g, shard(w_in),
        conv_w, conv_b_dw, conv_ln_g, conv_ln_b, q_norm_g, k_norm_g, shard(w_out), ffn2_norm_g,
        shard(ffn2_w_gate), shard(ffn2_w_up), shard(ffn2_w_down))
    loss = lax.psum(loss_part, MESH_AXES)

    state = dict(
        ffn1_w_gate=(ffn1_w_gate, m_ffn1_w_gate, v_ffn1_w_gate), ffn1_w_up=(ffn1_w_up, m_ffn1_w_up, v_ffn1_w_up),
        ffn1_w_down=(ffn1_w_down, m_ffn1_w_down, v_ffn1_w_down), w_in=(w_in, m_w_in, v_w_in),
        w_out=(w_out, m_w_out, v_w_out),
        ffn2_w_gate=(ffn2_w_gate, m_ffn2_w_gate, v_ffn2_w_gate), ffn2_w_up=(ffn2_w_up, m_ffn2_w_up, v_ffn2_w_up),
        ffn2_w_down=(ffn2_w_down, m_ffn2_w_down, v_ffn2_w_down))
    out = {}
    for pname, (w, m, v) in state.items():
        out[pname] = adamw(w, m, v, big[pname], "adamw_" + pname)

    zero_taps = jnp.zeros((CONV_WIDTH, ch), F32)
    pack = lambda g1, gm, g3, cb, lg, lb, gq, gk: _pack_small(ch, g1, gm, g3, cb, lg, lb, gq, gk, zero_taps)[None]
    small_parts = all_gather(_pack_small(ch, **small), "ag_small_grads")
    s_res = adamw(
        pack(ffn1_norm_g, mix_norm_g, ffn2_norm_g, conv_b_dw, conv_ln_g, conv_ln_b, q_norm_g, k_norm_g),
        pack(m_ffn1_norm_g, m_mix_norm_g, m_ffn2_norm_g, m_conv_b_dw, m_conv_ln_g, m_conv_ln_b, m_q_norm_g, m_k_norm_g),
        pack(v_ffn1_norm_g, v_mix_norm_g, v_ffn2_norm_g, v_conv_b_dw, v_conv_ln_g, v_conv_ln_b, v_q_norm_g, v_k_norm_g),
        small_parts, "adamw_small")
    s_res = [r[0] for r in s_res]
    s_out = [_unpack_small(r, d) for r in s_res]
    names = dict(g1="ffn1_norm_g", gmix="mix_norm_g", g3="ffn2_norm_g", conv_b="conv_b_dw", ln_g="conv_ln_g",
                 ln_b="conv_ln_b", gq="q_norm_g", gk="k_norm_g")
    for key, full in names.items():
        out[full] = [r[key] for r in s_out]

    cshard = ch // N_DEV
    taps_sum = s_res[0][11:11 + CONV_WIDTH]
    taps_mine = lax.dynamic_slice(taps_sum, (0, me * cshard), (CONV_WIDTH, cshard))
    pad_taps = lambda a: jnp.pad(a, ((0, 0), (0, 32 - CONV_WIDTH), (0, 0)))
    c_res = adamw(pad_taps(conv_w_dw), pad_taps(m_conv_w_dw), pad_taps(v_conv_w_dw),
                  pad_taps(taps_mine[None]), "adamw_convw")
    out["conv_w_dw"] = [r[:, :CONV_WIDTH] for r in c_res]

    order = ["ffn1_norm_g", "ffn1_w_gate", "ffn1_w_up", "ffn1_w_down", "mix_norm_g", "w_in", "conv_w_dw",
             "conv_b_dw", "conv_ln_g", "conv_ln_b", "q_norm_g", "k_norm_g", "w_out", "ffn2_norm_g",
             "ffn2_w_gate", "ffn2_w_up", "ffn2_w_down"]
    result = [loss, grad_x[None]]
    for kind in range(4):
        result += [out[n][kind] for n in order]
    return tuple(result)
```

```python
import math
from typing import NamedTuple

import jax
import jax.numpy as jnp
from jax import lax
from jax.experimental import pallas as pl
from jax.experimental.pallas import tpu as pltpu

F32 = jnp.float32
BF16 = jnp.bfloat16

N_DEV = 8
EPS = 1e-6
HEAD_DIM = 64
LANES = 128
MXU_WIDTH = 256
W_IN_SPLIT = 4
CONV_WIDTH = 31
SUBLANES = 8
HALO = 32
ROW_CHUNK = 32
ATT_BLOCK = 128
DILATIONS = (1, 4, 16)
ATT_UNITS = 16
ATT_CHUNK = ATT_UNITS * ATT_BLOCK
ALIBI_MAX_BIAS = 8.0
MASKED = -1e30
VMEM_LIMIT = 56 * 1024 * 1024

ADAM_LR = 0.001
ADAM_B1 = 0.9
ADAM_B2 = 0.999
ADAM_EPS = 1e-08
ADAM_WD = 0.01
ADAM_STEP = 10

MESH_AXES = ("x", "y", "c")
ANY = pl.BlockSpec(memory_space=pl.ANY)


def _sds(shape, dtype):
    return jax.ShapeDtypeStruct(tuple(shape), dtype)


def _params(*sem):
    return pltpu.CompilerParams(dimension_semantics=sem, vmem_limit_bytes=VMEM_LIMIT)


def _sigmoid(v):
    return 1.0 / (1.0 + jnp.exp(-v))


def _row_tile(t, want):
    for cand in range(min(want, t) // 8 * 8, 0, -8):
        if t % cand == 0:
            return cand
    return t


def _mesh_pos():
    return lax.axis_index("x"), lax.axis_index("y"), lax.axis_index("c")


def _comm_sems():
    return [pltpu.SemaphoreType.DMA((7,)), pltpu.SemaphoreType.DMA((7,)), pltpu.SemaphoreType.DMA(())]


def _gather_phases(x_ref, out_ref, send_sems, recv_sems, local_sem):
    x, y, c = _mesh_pos()
    me, sibling = (x, y, c), (x, y, 1 - c)
    chips = [(1 - x, y), (x, 1 - y), (1 - x, 1 - y)]

    def slot(px, py, pc):
        return out_ref.at[4 * px + 2 * py + pc]

    def copy(k, block, to, src=None):
        return pltpu.make_async_remote_copy(
            src_ref=slot(*block) if src is None else src, dst_ref=slot(*block),
            send_sem=send_sems.at[k], recv_sem=recv_sems.at[k],
            device_id=to, device_id_type=pl.DeviceIdType.MESH)

    mine = pltpu.make_async_copy(x_ref, slot(*me), local_sem)
    first = [copy(0, me, sibling, src=x_ref)]
    first += [copy(1 + j, me, (*chip, c), src=x_ref) for j, chip in enumerate(chips)]
    passed = [copy(4 + j, (*chip, c), sibling) for j, chip in enumerate(chips)]

    def start():
        mine.start()
        for cp in first:
            cp.start()

    def forward():
        for j, chip in enumerate(chips):
            copy(1 + j, (*chip, c), me).wait_recv()
            passed[j].start()

    def finish():
        copy(0, sibling, me).wait_recv()
        for j, chip in enumerate(chips):
            copy(4 + j, (*chip, 1 - c), me).wait_recv()
        for cp in first + passed:
            cp.wait_send()
        mine.wait()

    return start, forward, finish


def _scatter_phases(p_ref, out_ref, send_sems, recv_sems, local_sem):
    x, y, c = _mesh_pos()
    me = 4 * x + 2 * y + c
    flips = [(fx, fy, fc) for fx in (0, 1) for fy in (0, 1) for fc in (0, 1)][1:]

    def copy(k, flip, receiving):
        px, py, pc = (1 - x if flip[0] else x, 1 - y if flip[1] else y, 1 - c if flip[2] else c)
        them = 4 * px + 2 * py + pc
        return pltpu.make_async_remote_copy(
            src_ref=p_ref.at[them], dst_ref=out_ref.at[them if receiving else me],
            send_sem=send_sems.at[k], recv_sem=recv_sems.at[k],
            device_id=(px, py, pc), device_id_type=pl.DeviceIdType.MESH)

    mine = pltpu.make_async_copy(p_ref.at[me], out_ref.at[me], local_sem)

    def start():
        mine.start()
        for k, flip in enumerate(flips):
            copy(k, flip, False).start()

    def finish():
        for k, flip in enumerate(flips):
            copy(k, flip, True).wait_recv()
            copy(k, flip, False).wait_send()
        mine.wait()

    return start, None, finish


class Rider(NamedTuple):
    kind: str
    src: jax.Array

    def out_shape(self):
        shape = (N_DEV,) + self.src.shape if self.kind == "gather" else self.src.shape
        return _sds(shape, self.src.dtype)


def _rider_hooks(riders, in_refs, out_refs, sem_refs, step, n_steps):
    phases = [(_gather_phases if r.kind == "gather" else _scatter_phases)(
                  in_refs[i], out_refs[i], *sem_refs[3 * i:3 * i + 3]) for i, r in enumerate(riders)]

    def begin():
        for start, forward, _ in phases:
            pl.when(step == 0)(start)
            if forward is not None:
                pl.when(step == (7 * n_steps) // 8)(forward)

    def end():
        for _, _, finish in phases:
            pl.when(step == n_steps - 1)(finish)

    return begin, end


def _split_refs(refs, n_in, n_out, n_scratch, n_riders):
    pos, parts = 0, []
    for n in (n_in, n_riders, n_out, n_riders, n_scratch, 3 * n_riders):
        parts.append(refs[pos:pos + n])
        pos += n
    return parts


def all_gather(shard, name):
    def body(x_ref, out_ref, send_sems, recv_sems, local_sem):
        start, forward, finish = _gather_phases(x_ref, out_ref, send_sems, recv_sems, local_sem)
        start()
        forward()
        finish()

    return pl.pallas_call(
        body, name=name, out_shape=_sds((N_DEV,) + shard.shape, shard.dtype),
        in_specs=[ANY], out_specs=ANY, scratch_shapes=_comm_sems(),
    )(shard)


def adamw(w, m, v, parts, name):
    n_parts, rows, cols = parts.shape
    tr = _row_tile(rows, 128)
    c1 = 1.0 - ADAM_B1 ** ADAM_STEP
    c2 = 1.0 - ADAM_B2 ** ADAM_STEP

    def body(w_ref, m_ref, v_ref, p_ref, g_ref, d_ref, nm_ref, nv_ref):
        g = p_ref[0].astype(F32)
        for s in range(1, n_parts):
            g = g + p_ref[s].astype(F32)
        nm = ADAM_B1 * m_ref[...] + (1.0 - ADAM_B1) * g
        nv = ADAM_B2 * v_ref[...] + (1.0 - ADAM_B2) * (g * g)
        delta = -ADAM_LR * ((nm / c1) / (jnp.sqrt(nv / c2) + ADAM_EPS) + ADAM_WD * w_ref[...])
        g_ref[...] = g
        d_ref[...] = delta
        nm_ref[...] = nm
        nv_ref[...] = nv

    mat = pl.BlockSpec((tr, cols), lambda i: (i, 0))
    return pl.pallas_call(
        body, name=name, grid=(rows // tr,),
        in_specs=[mat, mat, mat, pl.BlockSpec((n_parts, tr, cols), lambda i: (0, i, 0))],
        out_specs=[mat, mat, mat, mat],
        out_shape=[_sds((rows, cols), F32)] * 4,
        compiler_params=_params("parallel"),
    )(w, m, v, parts)


_NN = (((1,), (0,)), ((), ()))
_NT = (((1,), (1,)), ((), ()))
_TN = (((0,), (0,)), ((), ()))


def mm_cols(name, a, b_list, b_specs, nt, extras, extra_specs, out_shapes, out_specs, epilogue, n_blk, riders=(),
            tm_want=512):
    t_len, k_len = a.shape
    tm = _row_tile(t_len, tm_want)
    nb, ne, n_out, nr = len(b_list), len(extras), len(out_shapes), len(riders)
    t_steps = t_len // tm
    n_cols = b_specs[0].block_shape[-2 if nt else -1]

    def body(*refs):
        ins, r_in, outs, r_out, _, r_sem = _split_refs(refs, 1 + nb + ne, n_out, 0, nr)
        step = pl.program_id(0) * t_steps + pl.program_id(1)
        begin, end = _rider_hooks(riders, r_in, r_out, r_sem, step, n_blk * t_steps)
        begin()
        av = ins[0][...]
        for c0 in range(0, n_cols, MXU_WIDTH):
            cols = slice(c0, min(c0 + MXU_WIDTH, n_cols))
            accs = [lax.dot_general(av, br[cols, :] if nt else br[:, cols], _NT if nt else _NN,
                                    preferred_element_type=F32) for br in ins[1:1 + nb]]
            epilogue(accs, ins[1 + nb:], outs, cols)
        end()

    res = pl.pallas_call(
        body, name=name, grid=(n_blk, t_steps),
        in_specs=([pl.BlockSpec((tm, k_len), lambda j, t: (t, 0))] + list(b_specs) + list(extra_specs(tm))
                  + [ANY] * nr),
        out_specs=list(out_specs(tm)) + [ANY] * nr,
        out_shape=list(out_shapes) + [r.out_shape() for r in riders],
        scratch_shapes=_comm_sems() * nr,
        compiler_params=_params("arbitrary", "arbitrary"),
    )(a, *b_list, *extras, *[r.src for r in riders])
    return res[:n_out], res[n_out:]


def mm_reduce(name, a_list, a_specs, b_list, b_specs, nt, res, scale, t_len, n_len, n_blk, riders=(), tm_want=512):
    tm = _row_tile(t_len, tm_want)
    na, nr = len(a_list), len(riders)
    has_res = res is not None
    t_steps = t_len // tm

    def body(*refs):
        ins, r_in, outs, r_out, _, r_sem = _split_refs(refs, 2 * na + has_res, 1, 0, nr)
        o_ref = outs[0]
        j = pl.program_id(1)
        step = pl.program_id(0) * n_blk + j
        begin, end = _rider_hooks(riders, r_in, r_out, r_sem, step, t_steps * n_blk)
        begin()

        part = None
        for ar, br in zip(ins[:na], ins[na:2 * na]):
            d = lax.dot_general(ar[...], br[...], _NT if nt else _NN, preferred_element_type=F32)
            part = d if part is None else part + d

        @pl.when(j == 0)
        def _():
            o_ref[...] = part

        @pl.when(j > 0)
        def _():
            o_ref[...] += part

        if has_res or scale != 1.0:
            @pl.when(j == n_blk - 1)
            def _():
                val = o_ref[...] * scale if scale != 1.0 else o_ref[...]
                o_ref[...] = ins[2 * na][...] + val if has_res else val

        end()

    row = pl.BlockSpec((tm, n_len), lambda t, j: (t, 0))
    out = pl.pallas_call(
        body, name=name, grid=(t_steps, n_blk),
        in_specs=list(a_specs(tm)) + list(b_specs) + ([row] if has_res else []) + [ANY] * nr,
        out_specs=[row] + [ANY] * nr,
        out_shape=[_sds((t_len, n_len), F32)] + [r.out_shape() for r in riders],
        scratch_shapes=_comm_sems() * nr,
        compiler_params=_params("arbitrary", "arbitrary"),
    )(*a_list, *b_list, *([res] if has_res else []), *[r.src for r in riders])
    return out[0], out[1:]


def mm_jsum(name, a_list, b_list, nt, res, scale, n_len, riders=(), tm_want=512, tn=512):
    n_j, t_len, k_j = a_list[0].shape
    tm = _row_tile(t_len, tm_want)
    tn = min(tn, n_len)
    na, nr = len(a_list), len(riders)
    has_res = res is not None
    t_steps, n_steps = t_len // tm, n_len // tn

    def body(*refs):
        ins, r_in, outs, r_out, _, r_sem = _split_refs(refs, 2 * na + has_res, 1, 0, nr)
        o_ref = outs[0]
        step = pl.program_id(0) * t_steps + pl.program_id(1)
        begin, end = _rider_hooks(riders, r_in, r_out, r_sem, step, n_steps * t_steps)
        begin()
        for c0 in range(0, tn, MXU_WIDTH):
            cols = slice(c0, min(c0 + MXU_WIDTH, tn))
            acc = None
            for ar, br in zip(ins[:na], ins[na:2 * na]):
                for j in range(n_j):
                    bj = br[j, cols, :] if nt else br[j, :, cols]
                    d = lax.dot_general(ar[j], bj, _NT if nt else _NN, preferred_element_type=F32)
                    acc = d if acc is None else acc + d
            val = acc * scale if scale != 1.0 else acc
            o_ref[:, cols] = ins[2 * na][:, cols] + val if has_res else val
        end()

    a_spec = pl.BlockSpec((n_j, tm, k_j), lambda n, t: (0, t, 0))
    b_spec = (pl.BlockSpec((n_j, tn, k_j), lambda n, t: (0, n, 0)) if nt
              else pl.BlockSpec((n_j, k_j, tn), lambda n, t: (0, 0, n)))
    tile = pl.BlockSpec((tm, tn), lambda n, t: (t, n))
    out = pl.pallas_call(
        body, name=name, grid=(n_steps, t_steps),
        in_specs=[a_spec] * na + [b_spec] * na + ([tile] if has_res else []) + [ANY] * nr,
        out_specs=[tile] + [ANY] * nr,
        out_shape=[_sds((t_len, n_len), F32)] + [r.out_shape() for r in riders],
        scratch_shapes=_comm_sems() * nr,
        compiler_params=_params("arbitrary", "arbitrary"),
    )(*a_list, *b_list, *([res] if has_res else []), *[r.src for r in riders])
    return out[0], out[1:]


def mm_tn(name, x, x_spec, dy_list, dy_specs, out_shapes, out_specs, scale, t_len, n_blk, riders=(),
          x_transposed=False):
    tt = _row_tile(t_len, 2048)
    nd, nr = len(dy_list), len(riders)
    t_steps = t_len // tt
    acc_shapes = [pltpu.VMEM(spec.block_shape[-2:], F32) for spec in out_specs]

    def body(*refs):
        ins, r_in, outs, r_out, accs, r_sem = _split_refs(refs, 1 + nd, nd, nd, nr)
        t = pl.program_id(1)
        step = pl.program_id(0) * t_steps + t
        begin, end = _rider_hooks(riders, r_in, r_out, r_sem, step, n_blk * t_steps)
        begin()
        xv = ins[0][...]
        for dr, acc in zip(ins[1:], accs):
            d = lax.dot_general(xv, dr[...], _NN if x_transposed else _TN, preferred_element_type=F32)

            @pl.when(t == 0)
            def _():
                acc[...] = d

            @pl.when(t > 0)
            def _():
                acc[...] += d

        @pl.when(t == t_steps - 1)
        def _():
            for acc, orf in zip(accs, outs):
                val = acc[...] * scale if scale != 1.0 else acc[...]
                orf[...] = val.astype(orf.dtype)

        end()

    res = pl.pallas_call(
        body, name=name, grid=(n_blk, t_steps),
        in_specs=[x_spec(tt)] + list(dy_specs(tt)) + [ANY] * nr,
        out_specs=list(out_specs) + [ANY] * nr,
        out_shape=list(out_shapes) + [r.out_shape() for r in riders],
        scratch_shapes=acc_shapes + _comm_sems() * nr,
        compiler_params=_params("arbitrary", "arbitrary"),
    )(x, *dy_list, *[r.src for r in riders])
    return res[:nd], res[nd:]


def rms_fwd(x, g, name):
    t_len, d = x.shape
    tm = _row_tile(t_len, 512)

    def body(x_ref, g_ref, h_ref, ht_ref):
        xv = x_ref[...]
        r = lax.rsqrt(jnp.mean(xv * xv, axis=-1, keepdims=True) + EPS)
        hv = xv * r * g_ref[...]
        h_ref[...] = hv.astype(BF16)
        ht_ref[...] = hv.T.astype(BF16)

    row = pl.BlockSpec((tm, d), lambda i: (i, 0))
    return pl.pallas_call(
        body, name=name, grid=(t_len // tm,),
        in_specs=[row, pl.BlockSpec((1, d), lambda i: (0, 0))],
        out_specs=[row, pl.BlockSpec((d, tm), lambda i: (0, i))],
        out_shape=[_sds((t_len, d), BF16), _sds((d, t_len), BF16)],
        compiler_params=_params("parallel"),
    )(x, g)


def rms_bwd(x, g, dh, dres, name):
    t_len, d = x.shape
    tm = _row_tile(t_len, 512)

    def body(x_ref, g_ref, dh_ref, dr_ref, dx_ref, dxb_ref, dg_ref):
        i = pl.program_id(0)
        xv = x_ref[...]
        r = lax.rsqrt(jnp.mean(xv * xv, axis=-1, keepdims=True) + EPS)
        xh = xv * r
        dhv = dh_ref[...]

        @pl.when(i == 0)
        def _():
            dg_ref[...] = jnp.zeros_like(dg_ref)

        dg_ref[...] += jnp.sum(dhv * xh, axis=0, keepdims=True)
        dxh = dhv * g_ref[...]
        dx = dr_ref[...] + r * (dxh - xh * jnp.mean(dxh * xh, axis=-1, keepdims=True))
        dx_ref[...] = dx
        dxb_ref[...] = dx.astype(BF16)

    row = pl.BlockSpec((tm, d), lambda i: (i, 0))
    vec = pl.BlockSpec((1, d), lambda i: (0, 0))
    return pl.pallas_call(
        body, name=name, grid=(t_len // tm,),
        in_specs=[row, vec, row, row],
        out_specs=[row, row, vec],
        out_shape=[_sds((t_len, d), F32), _sds((t_len, d), BF16), _sds((1, d), F32)],
        compiler_params=_params("arbitrary"),
    )(x, g, dh, dres)


def loss_head(y, target, name):
    t_len, d = y.shape
    tm = _row_tile(t_len, 512)

    def body(y_ref, t_ref, l_ref, dy_ref, dyb_ref):
        i = pl.program_id(0)
        err = y_ref[...] - t_ref[...]

        @pl.when(i == 0)
        def _():
            l_ref[...] = jnp.zeros_like(l_ref)

        rows = jnp.sum(err * err, axis=-1, keepdims=True) * (1.0 / d)
        l_ref[...] += 0.5 * jnp.sum(rows, axis=0, keepdims=True)
        dy = err * (1.0 / d)
        dy_ref[...] = dy
        dyb_ref[...] = dy.astype(BF16)

    row = pl.BlockSpec((tm, d), lambda i: (i, 0))
    return pl.pallas_call(
        body, name=name, grid=(t_len // tm,),
        in_specs=[row, row],
        out_specs=[pl.BlockSpec((8, LANES), lambda i: (0, 0)), row, row],
        out_shape=[_sds((8, LANES), F32), _sds((t_len, d), F32), _sds((t_len, d), BF16)],
        compiler_params=_params("arbitrary"),
    )(y, target)


def _conv_specs(tm, ch):
    per = tm // HALO
    cur = lambda cb: pl.BlockSpec((tm, ch), lambda i: (i, cb))
    prev = lambda cb: pl.BlockSpec((HALO, ch), lambda i: (jnp.maximum(i * per - 1, 0), cb))
    return [cur(0), cur(1), prev(0), prev(1)]


def _tap_scratch(rows, ch):
    return pltpu.VMEM((SUBLANES, rows + SUBLANES, ch), F32)


def _shifted_copies(buf, rows):
    buf[0, rows:rows + SUBLANES, :] = jnp.zeros((SUBLANES, buf.shape[2]), F32)
    for s in range(1, SUBLANES):
        buf[s, 0:rows, :] = buf[0, pl.ds(s, rows), :]


def _tap_rows(buf, off):
    shift = off % SUBLANES
    return buf[shift, off - shift:off - shift + ROW_CHUNK, :]


def _fill_glu(ext, a_ref, gt_ref, ap_ref, gp_ref, i, tm):
    vp = ap_ref[...] * _sigmoid(gp_ref[...])
    ext[0, 0:HALO, :] = jnp.where(i > 0, vp, 0.0)
    ext[0, HALO:HALO + tm, :] = a_ref[...] * _sigmoid(gt_ref[...])
    _shifted_copies(ext, HALO + tm)


def _conv_rows(ext, w_ref, b_ref, r0):
    acc = jnp.broadcast_to(b_ref[...], (ROW_CHUNK, b_ref.shape[1]))
    for k in range(CONV_WIDTH):
        acc = acc + w_ref[k:k + 1, :] * _tap_rows(ext, r0 + HALO - (CONV_WIDTH - 1) + k)
    return acc


def _layer_norm(yv):
    mu = jnp.mean(yv, axis=-1, keepdims=True)
    cen = yv - mu
    var = jnp.mean(cen * cen, axis=-1, keepdims=True)
    rstd = lax.rsqrt(var + EPS)
    return cen * rstd, rstd


def conv_fwd(z, w, b, lg, lb, name):
    t_len = z.shape[0]
    ch = w.shape[1]
    tm = _row_tile(t_len, 256)

    def body(a_ref, gt_ref, ap_ref, gp_ref, w_ref, b_ref, lg_ref, lb_ref, y_ref, pre_ref, ext):
        i = pl.program_id(0)
        _fill_glu(ext, a_ref, gt_ref, ap_ref, gp_ref, i, tm)
        for r0 in range(0, tm, ROW_CHUNK):
            pre = _conv_rows(ext, w_ref, b_ref, r0)
            pre_ref[r0:r0 + ROW_CHUNK, :] = pre
            xh, _ = _layer_norm(pre)
            u = xh * lg_ref[...] + lb_ref[...]
            y_ref[r0:r0 + ROW_CHUNK, :] = (u * _sigmoid(u)).astype(BF16)

    vec = pl.BlockSpec((1, ch), lambda i: (0, 0))
    row = pl.BlockSpec((tm, ch), lambda i: (i, 0))
    return pl.pallas_call(
        body, name=name, grid=(t_len // tm,),
        in_specs=_conv_specs(tm, ch) + [pl.BlockSpec((32, ch), lambda i: (0, 0)), vec, vec, vec],
        out_specs=[row, row],
        out_shape=[_sds((t_len, ch), BF16), _sds((t_len, ch), F32)],
        scratch_shapes=[_tap_scratch(HALO + tm, ch)],
        compiler_params=_params("parallel"),
    )(z, z, z, z, w, b, lg, lb)


def conv_bwd_norm(pre, dy_cat, lg, lb, name):
    t_len, ch = pre.shape
    tm = _row_tile(t_len, 256)

    def body(pre_ref, dy_ref, lg_ref, lb_ref, dc_ref, dlg_ref, dlb_ref, db_ref):
        i = pl.program_id(0)

        @pl.when(i == 0)
        def _():
            dlg_ref[...] = jnp.zeros_like(dlg_ref)
            dlb_ref[...] = jnp.zeros_like(dlb_ref)
            db_ref[...] = jnp.zeros_like(db_ref)

        for r0 in range(0, tm, ROW_CHUNK):
            xh, rstd = _layer_norm(pre_ref[r0:r0 + ROW_CHUNK, :])
            u = xh * lg_ref[...] + lb_ref[...]
            sg = _sigmoid(u)
            du = dy_ref[r0:r0 + ROW_CHUNK, :] * (sg * (1.0 + u * (1.0 - sg)))
            dlg_ref[...] += jnp.sum(du * xh, axis=0, keepdims=True)
            dlb_ref[...] += jnp.sum(du, axis=0, keepdims=True)
            dxh = du * lg_ref[...]
            dc = rstd * (dxh - jnp.mean(dxh, axis=-1, keepdims=True)
                         - xh * jnp.mean(dxh * xh, axis=-1, keepdims=True))
            db_ref[...] += jnp.sum(dc, axis=0, keepdims=True)
            dc_ref[r0:r0 + ROW_CHUNK, :] = dc

    vec = pl.BlockSpec((1, ch), lambda i: (0, 0))
    row = pl.BlockSpec((tm, ch), lambda i: (i, 0))
    return pl.pallas_call(
        body, name=name, grid=(t_len // tm,),
        in_specs=[row, row, vec, vec],
        out_specs=[row, vec, vec, vec],
        out_shape=[_sds((t_len, ch), F32)] + [_sds((1, ch), F32)] * 3,
        compiler_params=_params("arbitrary"),
    )(pre, dy_cat, lg, lb)


def conv_bwd_taps(z, dc, w, name):
    t_len = z.shape[0]
    ch = w.shape[1]
    tm = _row_tile(t_len, 256)
    per = tm // HALO
    n_tiles = t_len // tm
    last_halo = t_len // HALO - 1

    def body(a_ref, gt_ref, ap_ref, gp_ref, dc_ref, dn_ref, w_ref, dz_a_ref, dz_g_ref, dw_ref, ext, dext):
        i = pl.program_id(0)
        _fill_glu(ext, a_ref, gt_ref, ap_ref, gp_ref, i, tm)
        dext[0, 0:tm, :] = dc_ref[...]
        dext[0, tm:tm + HALO, :] = jnp.where(i < n_tiles - 1, dn_ref[...], 0.0)
        _shifted_copies(dext, tm + HALO)

        @pl.when(i == 0)
        def _():
            dw_ref[...] = jnp.zeros_like(dw_ref)

        for r0 in range(0, tm, ROW_CHUNK):
            dcv = dext[0, r0:r0 + ROW_CHUNK, :]
            dv = jnp.zeros((ROW_CHUNK, ch), F32)
            for k in range(CONV_WIDTH):
                dv = dv + w_ref[k:k + 1, :] * _tap_rows(dext, r0 + (CONV_WIDTH - 1) - k)
                prod = dcv * _tap_rows(ext, r0 + HALO - (CONV_WIDTH - 1) + k)
                fold = prod[0:8]
                for s in range(8, ROW_CHUNK, 8):
                    fold = fold + prod[s:s + 8]
                dw_ref[k] += fold
            av = a_ref[r0:r0 + ROW_CHUNK, :]
            sg = _sigmoid(gt_ref[r0:r0 + ROW_CHUNK, :])
            dz_a_ref[r0:r0 + ROW_CHUNK, :] = (dv * sg).astype(BF16)
            dz_g_ref[r0:r0 + ROW_CHUNK, :] = (dv * av * sg * (1.0 - sg)).astype(BF16)

    row = pl.BlockSpec((tm, ch), lambda i: (i, 0))
    nxt = pl.BlockSpec((HALO, ch), lambda i: (jnp.minimum((i + 1) * per, last_halo), 0))
    return pl.pallas_call(
        body, name=name, grid=(n_tiles,),
        in_specs=_conv_specs(tm, ch) + [row, nxt, pl.BlockSpec((32, ch), lambda i: (0, 0))],
        out_specs=[row, row, pl.BlockSpec((32, 8, ch), lambda i: (0, 0, 0))],
        out_shape=[_sds((t_len, ch), BF16), _sds((t_len, ch), BF16), _sds((32, 8, ch), F32)],
        scratch_shapes=[_tap_scratch(HALO + tm, ch), _tap_scratch(tm + HALO, ch)],
        compiler_params=_params("arbitrary"),
    )(z, z, z, z, dc, dc, w)


def _head_masks(rows):
    lane = lax.broadcasted_iota(jnp.int32, (rows, LANES), 1)
    low = lane < HEAD_DIM
    return low, jnp.logical_not(low)


def _per_head_mean(val, low):
    s_low = jnp.sum(jnp.where(low, val, 0.0), axis=-1, keepdims=True)
    s_high = jnp.sum(jnp.where(low, 0.0, val), axis=-1, keepdims=True)
    return jnp.where(low, s_low, s_high) * (1.0 / HEAD_DIM)


def qk_norm_fwd(z, g2, ch, name):
    t_len = z.shape[0]
    tm = _row_tile(t_len, 512)

    def body(z_ref, g_ref, o_ref):
        low, _ = _head_masks(tm)
        for c0 in range(0, ch, LANES):
            cols = slice(c0, c0 + LANES)
            xv = z_ref[:, cols]
            r = lax.rsqrt(_per_head_mean(xv * xv, low) + EPS)
            o_ref[:, cols] = xv * r * g_ref[:, cols]

    return pl.pallas_call(
        body, name=name, grid=(t_len // tm, 2),
        in_specs=[pl.BlockSpec((tm, ch), lambda i, w: (i, 2 + w)),
                  pl.BlockSpec((1, ch), lambda i, w: (0, w))],
        out_specs=pl.BlockSpec((tm, ch), lambda i, w: (i, w)),
        out_shape=_sds((t_len, 2 * ch), F32),
        compiler_params=_params("parallel", "parallel"),
    )(z, g2)


def qk_norm_bwd(z, g, d_list, z_off, ch, name):
    t_len = z.shape[0]
    tm = _row_tile(t_len, 512)
    nd = len(d_list)

    def body(*refs):
        z_ref, g_ref, d_refs = refs[0], refs[1], refs[2:2 + nd]
        dz_ref, dg_ref = refs[2 + nd], refs[3 + nd]
        low, _ = _head_masks(tm)

        @pl.when(pl.program_id(0) == 0)
        def _():
            dg_ref[...] = jnp.zeros_like(dg_ref)

        for c0 in range(0, ch, LANES):
            cols = slice(c0, c0 + LANES)
            xv = z_ref[:, cols]
            r = lax.rsqrt(_per_head_mean(xv * xv, low) + EPS)
            xh = xv * r
            dy = d_refs[0][:, cols]
            for dr in d_refs[1:]:
                dy = dy + dr[:, cols]
            dg_ref[...] += jnp.sum(dy * xh, axis=0, keepdims=True)
            dxh = dy * g_ref[...]
            dz_ref[:, cols] = (r * (dxh - xh * _per_head_mean(dxh * xh, low))).astype(BF16)

    blk = pl.BlockSpec((tm, ch), lambda i: (i, 0))
    return pl.pallas_call(
        body, name=name, grid=(t_len // tm,),
        in_specs=[pl.BlockSpec((tm, ch), lambda i: (i, z_off)),
                  pl.BlockSpec((1, LANES), lambda i: (0, 0))] + [blk] * nd,
        out_specs=[blk, pl.BlockSpec((1, LANES), lambda i: (0, 0))],
        out_shape=[_sds((t_len, ch), BF16), _sds((1, LANES), F32)],
        compiler_params=_params("arbitrary"),
    )(z, g, *d_list)


def _alibi_bias(n_heads, dilation):
    slopes = 2.0 ** (-ALIBI_MAX_BIAS * jnp.arange(1, n_heads + 1, dtype=F32) / n_heads)
    qi = jnp.arange(ATT_BLOCK)[:, None]
    kj = jnp.arange(ATT_BLOCK)[None, :]
    dist_cur = (qi - kj).astype(F32)
    dist_prev = (ATT_BLOCK + qi - kj).astype(F32)
    cur = jnp.where((qi >= kj)[None], -slopes[:, None, None] * (dilation * dist_cur)[None], MASKED)
    prev = jnp.where((kj >= qi)[None], -slopes[:, None, None] * (dilation * dist_prev)[None], MASKED)
    return jnp.concatenate([prev, cur], axis=-1).astype(F32)


def _stack_heads(val, low, high):
    return jnp.concatenate([jnp.where(low, val, 0.0), jnp.where(high, val, 0.0)], axis=0).astype(BF16)


def _head_rows(val, low, high):
    other = pltpu.roll(val, HEAD_DIM, axis=1)
    rows = jnp.concatenate([jnp.where(low, val, other), jnp.where(high, val, other)], axis=0)
    return jnp.concatenate([rows, rows], axis=1)


def _unit_scores(q2, k2, b_ref, has_prev):
    s = lax.dot_general(q2, k2, _NT, preferred_element_type=F32)
    s = s + b_ref[...].reshape(2 * ATT_BLOCK, 2 * ATT_BLOCK)
    penalty = jnp.where(has_prev, 0.0, MASKED)
    return jnp.concatenate([s[:, :ATT_BLOCK] + penalty, s[:, ATT_BLOCK:]], axis=1)


def _strided_rows(r, dilation):
    per = ATT_CHUNK // dilation
    return pl.ds(r, per, stride=dilation) if dilation > 1 else pl.ds(0, per)


def _deinterleave(dst, src_ref, dilation, base=None, dtype=None):
    per = ATT_CHUNK // dilation
    for r in range(dilation):
        val = src_ref[_strided_rows(r, dilation), :]
        val = val if dtype is None else val.astype(dtype)
        if base is None:
            dst[r * per:(r + 1) * per, :] = val
        else:
            dst[pl.ds(pl.multiple_of(base + r * per, ATT_BLOCK), per), :] = val


def _unit_rows(u, c, nb, base, pbase):
    in_chunk = lax.rem(u, jnp.int32(nb)) > 0
    has_prev = jnp.logical_or(in_chunk, c > 0)
    urow = pl.multiple_of(u * ATT_BLOCK, ATT_BLOCK)
    crow = pl.multiple_of(base + u * ATT_BLOCK, ATT_BLOCK)
    prow = pl.multiple_of(jnp.where(in_chunk, base + (u - 1) * ATT_BLOCK,
                                    pbase + (u + nb - 1) * ATT_BLOCK), ATT_BLOCK)
    return in_chunk, has_prev, urow, crow, prow


def _interleave(dst_ref, src, dilation, base=None):
    per = ATT_CHUNK // dilation
    for r in range(dilation):
        if base is None:
            val = src[r * per:(r + 1) * per, :]
        else:
            val = src[pl.ds(pl.multiple_of(base + r * per, ATT_BLOCK), per), :]
        dst_ref[_strided_rows(r, dilation), :] = val


def attn_fwd(qk, z, dilation, ch, name):
    t_len = qk.shape[0]
    pairs = ch // LANES
    nc = t_len // ATT_CHUNK
    nb = ATT_UNITS // dilation
    scale = 1.0 / math.sqrt(HEAD_DIM)
    bias = _alibi_bias(2 * pairs, dilation)

    def body(q_ref, k_ref, v_ref, b_ref, o_ref, l_ref, qd, kx, vx, od, ld):
        c = pl.program_id(1)
        slot = lax.rem(c, jnp.int32(2))
        base, pbase = slot * ATT_CHUNK, (1 - slot) * ATT_CHUNK

        @pl.when(c == 0)
        def _():
            kx[...] = jnp.zeros_like(kx)
            vx[...] = jnp.zeros_like(vx)

        _deinterleave(qd, q_ref, dilation)
        _deinterleave(kx, k_ref, dilation, base, BF16)
        _deinterleave(vx, v_ref, dilation, base, BF16)
        low, high = _head_masks(ATT_BLOCK)

        def unit(u, carry):
            _, has_prev, urow, crow, prow = _unit_rows(u, c, nb, base, pbase)
            q2 = _stack_heads(qd[pl.ds(urow, ATT_BLOCK), :] * scale, low, high)
            k2 = jnp.concatenate([kx[pl.ds(prow, ATT_BLOCK), :], kx[pl.ds(crow, ATT_BLOCK), :]], axis=0)
            v2 = jnp.concatenate([vx[pl.ds(prow, ATT_BLOCK), :], vx[pl.ds(crow, ATT_BLOCK), :]], axis=0)
            s = _unit_scores(q2, k2, b_ref, has_prev)
            mx = jnp.max(s, axis=-1, keepdims=True)
            e = jnp.exp(s - mx)
            den = jnp.sum(e, axis=-1, keepdims=True)
            acc = lax.dot_general(e.astype(BF16), v2, _NN, preferred_element_type=F32) / den
            lse = jnp.broadcast_to(mx + jnp.log(den), acc.shape)
            od[pl.ds(urow, ATT_BLOCK), :] = jnp.where(low, acc[:ATT_BLOCK], acc[ATT_BLOCK:])
            ld[pl.ds(urow, ATT_BLOCK), :] = jnp.where(low, lse[:ATT_BLOCK], lse[ATT_BLOCK:])
            return carry

        lax.fori_loop(0, ATT_UNITS, unit, 0, unroll=8)
        _interleave(o_ref, od, dilation)
        _interleave(l_ref, ld, dilation)

    blk = (ATT_CHUNK, LANES)
    bias_spec = pl.BlockSpec((2, ATT_BLOCK, 2 * ATT_BLOCK), lambda p, c: (p, 0, 0))
    out_spec = pl.BlockSpec(blk, lambda p, c: (c, p))
    return pl.pallas_call(
        body, name=name, grid=(pairs, nc),
        in_specs=[pl.BlockSpec(blk, lambda p, c: (c, p)),
                  pl.BlockSpec(blk, lambda p, c: (c, pairs + p)),
                  pl.BlockSpec(blk, lambda p, c: (c, 4 * pairs + p)),
                  bias_spec],
        out_specs=[out_spec, out_spec],
        out_shape=[_sds((t_len, ch), F32)] * 2,
        scratch_shapes=[pltpu.VMEM((ATT_CHUNK, LANES), F32),
                        pltpu.VMEM((2 * ATT_CHUNK, LANES), BF16), pltpu.VMEM((2 * ATT_CHUNK, LANES), BF16),
                        pltpu.VMEM((ATT_CHUNK, LANES), F32), pltpu.VMEM((ATT_CHUNK, LANES), F32)],
        compiler_params=_params("arbitrary", "arbitrary"),
    )(qk, qk, z, bias)


def attn_combine(outs, lses, name):
    t_len, ch = outs[0].shape
    tm = _row_tile(t_len, 512)

    def body(o1, o2, o3, l1, l2, l3, out_ref, outb_ref, lg_ref):
        a, b, c = l1[...], l2[...], l3[...]
        mx = jnp.maximum(jnp.maximum(a, b), c)
        tot = mx + jnp.log(jnp.exp(a - mx) + jnp.exp(b - mx) + jnp.exp(c - mx))
        val = jnp.exp(a - tot) * o1[...] + jnp.exp(b - tot) * o2[...] + jnp.exp(c - tot) * o3[...]
        out_ref[...] = val
        outb_ref[...] = val.astype(BF16)
        lg_ref[...] = tot

    row = pl.BlockSpec((tm, ch), lambda i: (i, 0))
    return pl.pallas_call(
        body, name=name, grid=(t_len // tm,),
        in_specs=[row] * 6, out_specs=[row] * 3,
        out_shape=[_sds((t_len, ch), F32), _sds((t_len, ch), BF16), _sds((t_len, ch), F32)],
        compiler_params=_params("parallel"),
    )(*outs, *lses)


def attn_bwd(qk, z, dy_cat, out, lg, dilation, ch, name):
    t_len = qk.shape[0]
    pairs = ch // LANES
    nc = t_len // ATT_CHUNK
    nb = ATT_UNITS // dilation
    scale = 1.0 / math.sqrt(HEAD_DIM)
    bias = _alibi_bias(2 * pairs, dilation)

    def body(q_ref, k_ref, v_ref, do_ref, out_ref, lg_ref, b_ref, dq_ref, dk_ref, dv_ref,
             qd, dod, lgd, dld, dl_nat, kx, vx, dkx, dvx, dqd):
        c = pl.program_id(1)
        slot = lax.rem(c, jnp.int32(2))
        base, pbase = slot * ATT_CHUNK, (1 - slot) * ATT_CHUNK

        @pl.when(c == 0)
        def _():
            for ref in (kx, vx, dkx, dvx):
                ref[...] = jnp.zeros_like(ref)

        @pl.when(c < nc)
        def _():
            low_all, _ = _head_masks(ATT_CHUNK)
            dl_nat[...] = _per_head_mean(do_ref[...] * out_ref[...], low_all) * float(HEAD_DIM)
            _deinterleave(qd, q_ref, dilation)
            _deinterleave(dod, do_ref, dilation)
            _deinterleave(lgd, lg_ref, dilation)
            _deinterleave(dld, dl_nat, dilation)
            _deinterleave(kx, k_ref, dilation, base, BF16)
            _deinterleave(vx, v_ref, dilation, base, BF16)
            cur = pl.ds(pl.multiple_of(base, ATT_CHUNK), ATT_CHUNK)
            dkx[cur, :] = jnp.zeros((ATT_CHUNK, LANES), F32)
            dvx[cur, :] = jnp.zeros((ATT_CHUNK, LANES), F32)
            low, high = _head_masks(ATT_BLOCK)

            def unit(u, carry):
                _, has_prev, urow, crow, prow = _unit_rows(u, c, nb, base, pbase)
                rows = pl.ds(urow, ATT_BLOCK)
                q2 = _stack_heads(qd[rows, :] * scale, low, high)
                do2 = _stack_heads(dod[rows, :], low, high)
                lse = _head_rows(lgd[rows, :], low, high)
                delta = _head_rows(dld[rows, :], low, high)
                k2 = jnp.concatenate([kx[pl.ds(prow, ATT_BLOCK), :], kx[pl.ds(crow, ATT_BLOCK), :]], axis=0)
                v2 = jnp.concatenate([vx[pl.ds(prow, ATT_BLOCK), :], vx[pl.ds(crow, ATT_BLOCK), :]], axis=0)
                prob = jnp.exp(_unit_scores(q2, k2, b_ref, has_prev) - lse)
                dp = lax.dot_general(do2, v2, _NT, preferred_element_type=F32)
                ds = (prob * (dp - delta)).astype(BF16)
                dq2 = lax.dot_general(ds, k2, _NN, preferred_element_type=F32)
                dk2 = lax.dot_general(ds, q2, _TN, preferred_element_type=F32)
                dv2 = lax.dot_general(prob.astype(BF16), do2, _TN, preferred_element_type=F32)
                dqd[rows, :] = scale * jnp.where(low, dq2[:ATT_BLOCK], dq2[ATT_BLOCK:])
                dkx[pl.ds(prow, ATT_BLOCK), :] += dk2[:ATT_BLOCK]
                dkx[pl.ds(crow, ATT_BLOCK), :] += dk2[ATT_BLOCK:]
                dvx[pl.ds(prow, ATT_BLOCK), :] += dv2[:ATT_BLOCK]
                dvx[pl.ds(crow, ATT_BLOCK), :] += dv2[ATT_BLOCK:]
                return carry

            lax.fori_loop(0, ATT_UNITS, unit, 0, unroll=8)
            _interleave(dq_ref, dqd, dilation)

        @pl.when(c > 0)
        def _():
            _interleave(dk_ref, dkx, dilation, pbase)
            _interleave(dv_ref, dvx, dilation, pbase)

    blk = (ATT_CHUNK, LANES)
    here = lambda c: jnp.minimum(c, nc - 1)
    spec = lambda off: pl.BlockSpec(blk, lambda p, c: (here(c), off + p))
    late = pl.BlockSpec(blk, lambda p, c: (jnp.maximum(c - 1, 0), p))
    bias_spec = pl.BlockSpec((2, ATT_BLOCK, 2 * ATT_BLOCK), lambda p, c: (p, 0, 0))
    f32_chunk = pltpu.VMEM((ATT_CHUNK, LANES), F32)
    return pl.pallas_call(
        body, name=name, grid=(pairs, nc + 1),
        in_specs=[spec(0), spec(pairs), spec(4 * pairs), spec(pairs), spec(0), spec(0), bias_spec],
        out_specs=[spec(0), late, late],
        out_shape=[_sds((t_len, ch), F32)] * 3,
        scratch_shapes=[f32_chunk] * 5
                       + [pltpu.VMEM((2 * ATT_CHUNK, LANES), BF16)] * 2
                       + [pltpu.VMEM((2 * ATT_CHUNK, LANES), F32)] * 2 + [f32_chunk],
        compiler_params=_params("arbitrary", "arbitrary"),
    )(qk, qk, z, dy_cat, out, lg, bias)


def sum3_bf16(a, b, c, name):
    t_len, ch = a.shape
    tm = _row_tile(t_len, 512)

    def body(a_ref, b_ref, c_ref, o_ref):
        o_ref[...] = (a_ref[...] + b_ref[...] + c_ref[...]).astype(BF16)

    row = pl.BlockSpec((tm, ch), lambda i: (i, 0))
    return pl.pallas_call(
        body, name=name, grid=(t_len // tm,), in_specs=[row] * 3, out_specs=row,
        out_shape=_sds((t_len, ch), BF16), compiler_params=_params("parallel"),
    )(a, b, c)


def _blk3(rows, cols):
    return pl.BlockSpec((None, rows, cols), lambda j, t: (j, 0, 0))


def ffn_up(h, wg, wu, name, riders):
    t_len, d = h.shape
    n_blk, _, fj = wg.shape

    def epilogue(accs, e_refs, o_refs, cols):
        gate, up = accs
        o_refs[0][:, cols] = gate.astype(BF16)
        o_refs[1][:, cols] = up.astype(BF16)
        o_refs[2][:, cols] = (gate * _sigmoid(gate) * up).astype(BF16)

    act = lambda tm: pl.BlockSpec((None, tm, fj), lambda j, t: (j, t, 0))
    return mm_cols(name, h, [wg, wu], [_blk3(d, fj)] * 2, False, [], lambda tm: [],
                   [_sds((n_blk, t_len, fj), BF16)] * 3, lambda tm: [act(tm)] * 3, epilogue, n_blk, riders,
                   tm_want=1024)


def ffn_gate(h, wg, name, riders):
    t_len, d = h.shape
    n_blk, _, fj = wg.shape

    def epilogue(accs, e_refs, o_refs, cols):
        o_refs[0][:, cols] = accs[0].astype(BF16)

    act = lambda tm: pl.BlockSpec((None, tm, fj), lambda j, t: (j, t, 0))
    return mm_cols(name, h, [wg], [_blk3(d, fj)], False, [], lambda tm: [],
                   [_sds((n_blk, t_len, fj), BF16)], lambda tm: [act(tm)], epilogue, n_blk, riders, tm_want=1024)


def ffn_up_after_gate(h, wu, gate, name, riders):
    t_len, d = h.shape
    n_blk, _, fj = wu.shape

    def epilogue(accs, e_refs, o_refs, cols):
        gv = e_refs[0][:, cols].astype(F32)
        o_refs[0][:, cols] = accs[0].astype(BF16)
        o_refs[1][:, cols] = (gv * _sigmoid(gv) * accs[0]).astype(BF16)

    act = lambda tm: pl.BlockSpec((None, tm, fj), lambda j, t: (j, t, 0))
    return mm_cols(name, h, [wu], [_blk3(d, fj)], False, [gate], lambda tm: [act(tm)],
                   [_sds((n_blk, t_len, fj), BF16)] * 2, lambda tm: [act(tm)] * 2, epilogue, n_blk, riders,
                   tm_want=1024)


def ffn_down(act, wd, res, name, riders):
    n_blk, t_len, fj = act.shape
    d = wd.shape[2]
    return mm_jsum(name, [act], [wd], False, res, 0.5, d, riders)


def ffn_bwd(ht, gate, up, act, wg, wu, wd, dyb, name):
    d, t_len = ht.shape
    n_blk, _, fj = act.shape

    def epilogue(accs, e_refs, o_refs, cols):
        d_act = 0.5 * accs[0]
        gv, uv = e_refs[0][:, cols].astype(F32), e_refs[1][:, cols].astype(F32)
        sg = _sigmoid(gv)
        o_refs[0][:, cols] = (d_act * uv * (sg * (1.0 + gv * (1.0 - sg)))).astype(BF16)
        o_refs[1][:, cols] = (d_act * gv * sg).astype(BF16)

    act_jt = lambda tm: pl.BlockSpec((None, tm, fj), lambda j, t: (j, t, 0))
    (d_gate, d_up), _ = mm_cols(name + "_dact", dyb, [wd], [_blk3(fj, d)], True, [gate, up],
                                lambda tm: [act_jt(tm)] * 2, [_sds((n_blk, t_len, fj), BF16)] * 2,
                                lambda tm: [act_jt(tm)] * 2, epilogue, n_blk, tm_want=1024)

    ht_spec = lambda tt: pl.BlockSpec((d, tt), lambda j, t: (0, t))
    (d_wg,), _ = mm_tn(name + "_dwg", ht, ht_spec, [d_gate], lambda tt: [act_jt(tt)],
                       [_sds((n_blk, d, fj), BF16)], [_blk3(d, fj)], 1.0, t_len, n_blk, x_transposed=True)
    (d_wu,), (recv_wg,) = mm_tn(name + "_dwu", ht, ht_spec, [d_up], lambda tt: [act_jt(tt)],
                                [_sds((n_blk, d, fj), BF16)], [_blk3(d, fj)], 1.0, t_len, n_blk,
                                [Rider("scatter", d_wg)], x_transposed=True)

    (d_wd,), (recv_wu,) = mm_tn(name + "_dwd", act, act_jt, [dyb],
                                lambda tt: [pl.BlockSpec((tt, d), lambda j, t: (t, 0))],
                                [_sds((n_blk, fj, d), BF16)], [_blk3(fj, d)], 0.5, t_len, n_blk,
                                [Rider("scatter", d_wu)])

    dh, (recv_wd,) = mm_jsum(name + "_dh", [d_gate, d_up], [wg, wu], True, None, 1.0, d,
                             [Rider("scatter", d_wd)])
    return dh, recv_wg, recv_wu, recv_wd


def local_step(x, target, g1, wg1, wu1_s, wd1_s, gmix, win_s, conv_w, conv_b, ln_g, ln_b, gq, gk, wout_s, g3,
               wg2_s, wu2_s, wd2_s):
    t_len, d = x.shape
    ch = d // 2
    ij = win_s.shape[1]
    oj = wout_s.shape[0]
    n_blk = N_DEV

    h1, h1t = rms_fwd(x, g1, "rms1")
    (gate1,), (wu1,) = ffn_gate(h1, wg1, "ffn1_gate", [Rider("gather", wu1_s)])
    (up1, act1), (wd1, wg2) = ffn_up_after_gate(h1, wu1, gate1, "ffn1_up",
                                                [Rider("gather", wd1_s), Rider("gather", wg2_s)])
    x1, (win, wu2) = ffn_down(act1, wd1, x, "ffn1_down", [Rider("gather", win_s), Rider("gather", wu2_s)])

    h2, h2t = rms_fwd(x1, gmix, "rms_mix")

    def store_f32(accs, e_refs, o_refs, cols):
        o_refs[0][:, cols] = accs[0]

    d_in = n_blk * ij
    win_full = jnp.transpose(win, (1, 0, 2)).reshape(1, d, d_in)
    wide = d_in // W_IN_SPLIT
    (z,), (wd2, wout) = mm_cols(
        "w_in", h2, [win_full], [pl.BlockSpec((None, d, wide), lambda j, t: (0, 0, j))], False, [], lambda tm: [],
        [_sds((t_len, d_in), F32)],
        lambda tm: [pl.BlockSpec((tm, wide), lambda j, t: (t, j))], store_f32, W_IN_SPLIT,
        [Rider("gather", wd2_s), Rider("gather", wout_s)], tm_want=1024)

    conv_w32 = jnp.pad(conv_w, ((0, 32 - CONV_WIDTH), (0, 0)))
    y_conv, conv_pre = conv_fwd(z, conv_w32, conv_b, ln_g, ln_b, "conv_fwd")

    g2 = jnp.concatenate([jnp.tile(gq, (1, ch // HEAD_DIM)), jnp.tile(gk, (1, ch // HEAD_DIM))], axis=1)
    qk = qk_norm_fwd(z, g2, ch, "qk_norm")
    branch = [attn_fwd(qk, z, dil, ch, "attn_fwd_d%d" % dil) for dil in DILATIONS]
    att, att_b, lg = attn_combine([o for o, _ in branch], [l for _, l in branch], "attn_combine")

    y_cat = jnp.concatenate([y_conv, att_b], axis=1)
    wout_full = wout.reshape(1, n_blk * oj, d)
    x2, _ = mm_reduce(
        "w_out", [y_cat], lambda tm: [pl.BlockSpec((tm, n_blk * oj), lambda t, j: (t, 0))],
        [wout_full], [pl.BlockSpec((None, n_blk * oj, d), lambda t, j: (0, 0, 0))], False, x1, 1.0,
        t_len, d, 1)

    h3, h3t = rms_fwd(x2, g3, "rms3")
    (gate2, up2, act2), _ = ffn_up(h3, wg2, wu2, "ffn2_up", [])
    y, _ = ffn_down(act2, wd2, x2, "ffn2_down", [])

    loss_tile, dy, dyb = loss_head(y, target, "loss")

    dh3, recv_wg2, recv_wu2, recv_wd2 = ffn_bwd(h3t, gate2, up2, act2, wg2, wu2, wd2, dyb, "ffn2")
    dx2, dx2b, d_g3 = rms_bwd(x2, g3, dh3, dy, "rms3_bwd")

    (dy_cat,), _ = mm_cols("w_out_dy", dx2b, [wout_full], [_blk3(n_blk * oj, d)], True, [], lambda tm: [],
                           [_sds((t_len, n_blk * oj), F32)],
                           lambda tm: [pl.BlockSpec((tm, n_blk * oj), lambda j, t: (t, 0))], store_f32, 1,
                           tm_want=1024)
    (d_wout,), _ = mm_tn("w_out_dw", y_cat, lambda tt: pl.BlockSpec((tt, oj), lambda j, t: (t, j)),
                         [dx2b], lambda tt: [pl.BlockSpec((tt, d), lambda j, t: (t, 0))],
                         [_sds((n_blk, oj, d), BF16)], [_blk3(oj, d)], 1.0, t_len, n_blk)

    dc, d_lg, d_lb, d_cb = conv_bwd_norm(conv_pre, dy_cat, ln_g, ln_b, "conv_bwd_norm")
    dz_a, dz_g, d_cw8 = conv_bwd_taps(z, dc, conv_w32, "conv_bwd_taps")
    d_cw = jnp.sum(d_cw8, axis=1)[:CONV_WIDTH]

    grads = [attn_bwd(qk, z, dy_cat, att, lg, dil, ch, "attn_bwd_d%d" % dil) for dil in DILATIONS]
    gq_t = jnp.tile(gq, (1, LANES // HEAD_DIM))
    gk_t = jnp.tile(gk, (1, LANES // HEAD_DIM))
    dz_q, d_gq2 = qk_norm_bwd(z, gq_t, [g[0] for g in grads], 2, ch, "q_norm_bwd")
    dz_k, d_gk2 = qk_norm_bwd(z, gk_t, [g[1] for g in grads], 3, ch, "k_norm_bwd")
    d_gq = d_gq2[:, :HEAD_DIM] + d_gq2[:, HEAD_DIM:]
    d_gk = d_gk2[:, :HEAD_DIM] + d_gk2[:, HEAD_DIM:]
    dz_v = sum3_bf16(grads[0][2], grads[1][2], grads[2][2], "dv_sum")
    dzb = jnp.concatenate([dz_a, dz_g, dz_q, dz_k, dz_v], axis=1)

    (d_win,), (recv_wout,) = mm_tn(
        "w_in_dw", h2t, lambda tt: pl.BlockSpec((d, tt), lambda j, t: (0, t)),
        [dzb], lambda tt: [pl.BlockSpec((tt, ij), lambda j, t: (t, j))],
        [_sds((n_blk, d, ij), BF16)], [_blk3(d, ij)], 1.0, t_len, n_blk, [Rider("scatter", d_wout)],
        x_transposed=True)
    tall = d // W_IN_SPLIT
    (dh2,), (recv_win,) = mm_cols(
        "w_in_dh", dzb, [win_full], [pl.BlockSpec((None, tall, d_in), lambda j, t: (0, j, 0))], True, [],
        lambda tm: [], [_sds((t_len, d), F32)],
        lambda tm: [pl.BlockSpec((tm, tall), lambda j, t: (t, j))], store_f32, W_IN_SPLIT,
        [Rider("scatter", d_win)], tm_want=1024)
    dx1, dx1b, d_gmix = rms_bwd(x1, gmix, dh2, dx2, "rms_mix_bwd")

    dh1, recv_wg1, recv_wu1, recv_wd1 = ffn_bwd(h1t, gate1, up1, act1, wg1, wu1, wd1, dx1b, "ffn1")
    grad_x, _, d_g1 = rms_bwd(x, g1, dh1, dx1, "rms1_bwd")

    big = dict(ffn1_w_gate=recv_wg1, ffn1_w_up=recv_wu1, ffn1_w_down=recv_wd1, w_in=recv_win, w_out=recv_wout,
               ffn2_w_gate=recv_wg2, ffn2_w_up=recv_wu2, ffn2_w_down=recv_wd2)
    small = dict(g1=d_g1, gmix=d_gmix, g3=d_g3, conv_b=d_cb, ln_g=d_lg, ln_b=d_lb, gq=d_gq, gk=d_gk, conv_w=d_cw)
    return loss_tile[0, 0], grad_x, big, small


SMALL_ROWS = 48


def _pack_small(ch, g1, gmix, g3, conv_b, ln_g, ln_b, gq, gk, conv_w):
    pad_head = lambda v: jnp.pad(v, ((0, 0), (0, ch - v.shape[1])))
    rows = [g1.reshape(2, ch), gmix.reshape(2, ch), g3.reshape(2, ch), conv_b, ln_g, ln_b,
            pad_head(gq), pad_head(gk), conv_w]
    packed = jnp.concatenate(rows, axis=0)
    return jnp.pad(packed, ((0, SMALL_ROWS - packed.shape[0]), (0, 0)))


def _unpack_small(packed, d):
    return dict(g1=packed[0:2].reshape(1, d), gmix=packed[2:4].reshape(1, d), g3=packed[4:6].reshape(1, d),
                conv_b=packed[6:7], ln_g=packed[7:8], ln_b=packed[8:9],
                gq=packed[9:10, :HEAD_DIM], gk=packed[10:11, :HEAD_DIM])


def kernel(x, ffn1_norm_g, ffn1_w_gate, ffn1_w_up, ffn1_w_down, mix_norm_g, w_in, conv_w_dw, conv_b_dw, conv_ln_g, conv_ln_b, q_norm_g, k_norm_g, w_out, ffn2_norm_g, ffn2_w_gate, ffn2_w_up, ffn2_w_down, loss_target, m_ffn1_norm_g, m_ffn1_w_gate, m_ffn1_w_up, m_ffn1_w_down, m_mix_norm_g, m_w_in, m_conv_w_dw, m_conv_b_dw, m_conv_ln_g, m_conv_ln_b, m_q_norm_g, m_k_norm_g, m_w_out, m_ffn2_norm_g, m_ffn2_w_gate, m_ffn2_w_up, m_ffn2_w_down, v_ffn1_norm_g, v_ffn1_w_gate, v_ffn1_w_up, v_ffn1_w_down, v_mix_norm_g, v_w_in, v_conv_w_dw, v_conv_b_dw, v_conv_ln_g, v_conv_ln_b, v_q_norm_g, v_k_norm_g, v_w_out, v_ffn2_norm_g, v_ffn2_w_gate, v_ffn2_w_up, v_ffn2_w_down):
    d = x.shape[-1]
    ch = d // 2
    me = 4 * lax.axis_index("x") + 2 * lax.axis_index("y") + lax.axis_index("c")

    shard = lambda w: w[0].astype(BF16)
    wg1 = all_gather(shard(ffn1_w_gate), "ag_wg1")
    cw_all = all_gather(conv_w_dw[0], "ag_convw")
    conv_w = jnp.transpose(cw_all, (1, 0, 2)).reshape(CONV_WIDTH, ch)

    loss_part, grad_x, big, small = local_step(
        x[0], loss_target[0], ffn1_norm_g, wg1, shard(ffn1_w_up), shard(ffn1_w_down), mix_norm_g, shard(w_in),
        conv_w, conv_b_dw, conv_ln_g, conv_ln_b, q_norm_g, k_norm_g, shard(w_out), ffn2_norm_g,
        shard(ffn2_w_gate), shard(ffn2_w_up), shard(ffn2_w_down))
    loss = lax.psum(loss_part, MESH_AXES)

    state = dict(
        ffn1_w_gate=(ffn1_w_gate, m_ffn1_w_gate, v_ffn1_w_gate), ffn1_w_up=(ffn1_w_up, m_ffn1_w_up, v_ffn1_w_up),
        ffn1_w_down=(ffn1_w_down, m_ffn1_w_down, v_ffn1_w_down), w_in=(w_in, m_w_in, v_w_in),
        w_out=(w_out, m_w_out, v_w_out),
        ffn2_w_gate=(ffn2_w_gate, m_ffn2_w_gate, v_ffn2_w_gate), ffn2_w_up=(ffn2_w_up, m_ffn2_w_up, v_ffn2_w_up),
        ffn2_w_down=(ffn2_w_down, m_ffn2_w_down, v_ffn2_w_down))
    out = {}
    for pname, (w, m, v) in state.items():
        out[pname] = [r[None] for r in adamw(w[0], m[0], v[0], big[pname], "adamw_" + pname)]

    zero_taps = jnp.zeros((CONV_WIDTH, ch), F32)
    pack = lambda g1, gm, g3, cb, lg, lb, gq, gk: _pack_small(ch, g1, gm, g3, cb, lg, lb, gq, gk, zero_taps)
    small_parts = all_gather(_pack_small(ch, **small), "ag_small_grads")
    s_res = adamw(
        pack(ffn1_norm_g, mix_norm_g, ffn2_norm_g, conv_b_dw, conv_ln_g, conv_ln_b, q_norm_g, k_norm_g),
        pack(m_ffn1_norm_g, m_mix_norm_g, m_ffn2_norm_g, m_conv_b_dw, m_conv_ln_g, m_conv_ln_b, m_q_norm_g, m_k_norm_g),
        pack(v_ffn1_norm_g, v_mix_norm_g, v_ffn2_norm_g, v_conv_b_dw, v_conv_ln_g, v_conv_ln_b, v_q_norm_g, v_k_norm_g),
        small_parts, "adamw_small")
    s_out = [_unpack_small(r, d) for r in s_res]
    names = dict(g1="ffn1_norm_g", gmix="mix_norm_g", g3="ffn2_norm_g", conv_b="conv_b_dw", ln_g="conv_ln_g",
                 ln_b="conv_ln_b", gq="q_norm_g", gk="k_norm_g")
    for key, full in names.items():
        out[full] = [r[key] for r in s_out]

    cshard = ch // N_DEV
    taps_sum = s_res[0][11:11 + CONV_WIDTH]
    taps_mine = lax.dynamic_slice(taps_sum, (0, me * cshard), (CONV_WIDTH, cshard))
    pad_taps = lambda a: jnp.pad(a, ((0, 32 - CONV_WIDTH), (0, 0)))
    c_res = adamw(pad_taps(conv_w_dw[0]), pad_taps(m_conv_w_dw[0]), pad_taps(v_conv_w_dw[0]),
                  pad_taps(taps_mine)[None], "adamw_convw")
    out["conv_w_dw"] = [r[:CONV_WIDTH][None] for r in c_res]

    order = ["ffn1_norm_g", "ffn1_w_gate", "ffn1_w_up", "ffn1_w_down", "mix_norm_g", "w_in", "conv_w_dw",
             "conv_b_dw", "conv_ln_g", "conv_ln_b", "q_norm_g", "k_norm_g", "w_out", "ffn2_norm_g",
             "ffn2_w_gate", "ffn2_w_up", "ffn2_w_down"]
    result = [loss, grad_x[None]]
    for kind in range(4):
        result += [out[n][kind] for n in order]
    return tuple(result)
```

```python
import math
from typing import NamedTuple

import jax
import jax.numpy as jnp
from jax import lax
from jax.experimental import pallas as pl
from jax.experimental.pallas import tpu as pltpu

F32 = jnp.float32
BF16 = jnp.bfloat16

N_DEV = 8
EPS = 1e-6
HEAD_DIM = 64
LANES = 128
MXU_WIDTH = 256
W_IN_SPLIT = 4
CONV_WIDTH = 31
SUBLANES = 8
HALO = 32
ROW_CHUNK = 32
ATT_BLOCK = 128
DILATIONS = (1, 4, 16)
ATT_UNITS = 16
ATT_CHUNK = ATT_UNITS * ATT_BLOCK
ALIBI_MAX_BIAS = 8.0
MASKED = -1e30
VMEM_LIMIT = 56 * 1024 * 1024

ADAM_LR = 0.001
ADAM_B1 = 0.9
ADAM_B2 = 0.999
ADAM_EPS = 1e-08
ADAM_WD = 0.01
ADAM_STEP = 10

MESH_AXES = ("x", "y", "c")
ANY = pl.BlockSpec(memory_space=pl.ANY)


def _sds(shape, dtype):
    return jax.ShapeDtypeStruct(tuple(shape), dtype)


def _params(*sem):
    return pltpu.CompilerParams(dimension_semantics=sem, vmem_limit_bytes=VMEM_LIMIT)


def _sigmoid(v):
    return 1.0 / (1.0 + jnp.exp(-v))


def _row_tile(t, want):
    for cand in range(min(want, t) // 8 * 8, 0, -8):
        if t % cand == 0:
            return cand
    return t


def _mesh_pos():
    return lax.axis_index("x"), lax.axis_index("y"), lax.axis_index("c")


def _comm_sems():
    return [pltpu.SemaphoreType.DMA((7,)), pltpu.SemaphoreType.DMA((7,)), pltpu.SemaphoreType.DMA(())]


def _gather_phases(x_ref, out_ref, send_sems, recv_sems, local_sem):
    x, y, c = _mesh_pos()
    me, sibling = (x, y, c), (x, y, 1 - c)
    chips = [(1 - x, y), (x, 1 - y), (1 - x, 1 - y)]

    def slot(px, py, pc):
        return out_ref.at[4 * px + 2 * py + pc]

    def copy(k, block, to, src=None):
        return pltpu.make_async_remote_copy(
            src_ref=slot(*block) if src is None else src, dst_ref=slot(*block),
            send_sem=send_sems.at[k], recv_sem=recv_sems.at[k],
            device_id=to, device_id_type=pl.DeviceIdType.MESH)

    mine = pltpu.make_async_copy(x_ref, slot(*me), local_sem)
    first = [copy(0, me, sibling, src=x_ref)]
    first += [copy(1 + j, me, (*chip, c), src=x_ref) for j, chip in enumerate(chips)]
    passed = [copy(4 + j, (*chip, c), sibling) for j, chip in enumerate(chips)]

    def start():
        mine.start()
        for cp in first:
            cp.start()

    def forward():
        for j, chip in enumerate(chips):
            copy(1 + j, (*chip, c), me).wait_recv()
            passed[j].start()

    def finish():
        copy(0, sibling, me).wait_recv()
        for j, chip in enumerate(chips):
            copy(4 + j, (*chip, 1 - c), me).wait_recv()
        for cp in first + passed:
            cp.wait_send()
        mine.wait()

    return start, forward, finish


def _scatter_phases(p_ref, out_ref, send_sems, recv_sems, local_sem):
    x, y, c = _mesh_pos()
    me = 4 * x + 2 * y + c
    flips = [(fx, fy, fc) for fx in (0, 1) for fy in (0, 1) for fc in (0, 1)][1:]

    def copy(k, flip, receiving):
        px, py, pc = (1 - x if flip[0] else x, 1 - y if flip[1] else y, 1 - c if flip[2] else c)
        them = 4 * px + 2 * py + pc
        return pltpu.make_async_remote_copy(
            src_ref=p_ref.at[them], dst_ref=out_ref.at[them if receiving else me],
            send_sem=send_sems.at[k], recv_sem=recv_sems.at[k],
            device_id=(px, py, pc), device_id_type=pl.DeviceIdType.MESH)

    mine = pltpu.make_async_copy(p_ref.at[me], out_ref.at[me], local_sem)

    def start():
        mine.start()
        for k, flip in enumerate(flips):
            copy(k, flip, False).start()

    def finish():
        for k, flip in enumerate(flips):
            copy(k, flip, True).wait_recv()
            copy(k, flip, False).wait_send()
        mine.wait()

    return start, None, finish


class Rider(NamedTuple):
    kind: str
    src: jax.Array

    def out_shape(self):
        shape = (N_DEV,) + self.src.shape if self.kind == "gather" else self.src.shape
        return _sds(shape, self.src.dtype)


def _rider_hooks(riders, in_refs, out_refs, sem_refs, step, n_steps):
    phases = [(_gather_phases if r.kind == "gather" else _scatter_phases)(
                  in_refs[i], out_refs[i], *sem_refs[3 * i:3 * i + 3]) for i, r in enumerate(riders)]

    def begin():
        for start, forward, _ in phases:
            pl.when(step == 0)(start)
            if forward is not None:
                pl.when(step == (7 * n_steps) // 8)(forward)

    def end():
        for _, _, finish in phases:
            pl.when(step == n_steps - 1)(finish)

    return begin, end


def _split_refs(refs, n_in, n_out, n_scratch, n_riders):
    pos, parts = 0, []
    for n in (n_in, n_riders, n_out, n_riders, n_scratch, 3 * n_riders):
        parts.append(refs[pos:pos + n])
        pos += n
    return parts


def all_gather(shard, name):
    def body(x_ref, out_ref, send_sems, recv_sems, local_sem):
        start, forward, finish = _gather_phases(x_ref, out_ref, send_sems, recv_sems, local_sem)
        start()
        forward()
        finish()

    return pl.pallas_call(
        body, name=name, out_shape=_sds((N_DEV,) + shard.shape, shard.dtype),
        in_specs=[ANY], out_specs=ANY, scratch_shapes=_comm_sems(),
    )(shard)


def adamw(w, m, v, parts, name):
    n_parts, rows, cols = parts.shape
    tr = _row_tile(rows, 128)
    c1 = 1.0 - ADAM_B1 ** ADAM_STEP
    c2 = 1.0 - ADAM_B2 ** ADAM_STEP

    def body(w_ref, m_ref, v_ref, p_ref, g_ref, d_ref, nm_ref, nv_ref):
        g = p_ref[0].astype(F32)
        for s in range(1, n_parts):
            g = g + p_ref[s].astype(F32)
        nm = ADAM_B1 * m_ref[...] + (1.0 - ADAM_B1) * g
        nv = ADAM_B2 * v_ref[...] + (1.0 - ADAM_B2) * (g * g)
        delta = -ADAM_LR * ((nm / c1) / (jnp.sqrt(nv / c2) + ADAM_EPS) + ADAM_WD * w_ref[...])
        g_ref[...] = g
        d_ref[...] = delta
        nm_ref[...] = nm
        nv_ref[...] = nv

    mat = pl.BlockSpec((tr, cols), lambda i: (i, 0))
    return pl.pallas_call(
        body, name=name, grid=(rows // tr,),
        in_specs=[mat, mat, mat, pl.BlockSpec((n_parts, tr, cols), lambda i: (0, i, 0))],
        out_specs=[mat, mat, mat, mat],
        out_shape=[_sds((rows, cols), F32)] * 4,
        compiler_params=_params("parallel"),
    )(w, m, v, parts)


_NN = (((1,), (0,)), ((), ()))
_NT = (((1,), (1,)), ((), ()))
_TN = (((0,), (0,)), ((), ()))


def mm_cols(name, a, b_list, b_specs, nt, extras, extra_specs, out_shapes, out_specs, epilogue, n_blk, riders=(),
            tm_want=512):
    t_len, k_len = a.shape
    tm = _row_tile(t_len, tm_want)
    nb, ne, n_out, nr = len(b_list), len(extras), len(out_shapes), len(riders)
    t_steps = t_len // tm
    n_cols = b_specs[0].block_shape[-2 if nt else -1]

    def body(*refs):
        ins, r_in, outs, r_out, _, r_sem = _split_refs(refs, 1 + nb + ne, n_out, 0, nr)
        step = pl.program_id(0) * t_steps + pl.program_id(1)
        begin, end = _rider_hooks(riders, r_in, r_out, r_sem, step, n_blk * t_steps)
        begin()
        av = ins[0][...]
        for c0 in range(0, n_cols, MXU_WIDTH):
            cols = slice(c0, min(c0 + MXU_WIDTH, n_cols))
            accs = [lax.dot_general(av, br[cols, :] if nt else br[:, cols], _NT if nt else _NN,
                                    preferred_element_type=F32) for br in ins[1:1 + nb]]
            epilogue(accs, ins[1 + nb:], outs, cols)
        end()

    res = pl.pallas_call(
        body, name=name, grid=(n_blk, t_steps),
        in_specs=([pl.BlockSpec((tm, k_len), lambda j, t: (t, 0))] + list(b_specs) + list(extra_specs(tm))
                  + [ANY] * nr),
        out_specs=list(out_specs(tm)) + [ANY] * nr,
        out_shape=list(out_shapes) + [r.out_shape() for r in riders],
        scratch_shapes=_comm_sems() * nr,
        compiler_params=_params("arbitrary", "arbitrary"),
    )(a, *b_list, *extras, *[r.src for r in riders])
    return res[:n_out], res[n_out:]


def mm_reduce(name, a_list, a_specs, b_list, b_specs, nt, res, scale, t_len, n_len, n_blk, riders=(), tm_want=512):
    tm = _row_tile(t_len, tm_want)
    na, nr = len(a_list), len(riders)
    has_res = res is not None
    t_steps = t_len // tm

    def body(*refs):
        ins, r_in, outs, r_out, _, r_sem = _split_refs(refs, 2 * na + has_res, 1, 0, nr)
        o_ref = outs[0]
        j = pl.program_id(1)
        step = pl.program_id(0) * n_blk + j
        begin, end = _rider_hooks(riders, r_in, r_out, r_sem, step, t_steps * n_blk)
        begin()

        part = None
        for ar, br in zip(ins[:na], ins[na:2 * na]):
            d = lax.dot_general(ar[...], br[...], _NT if nt else _NN, preferred_element_type=F32)
            part = d if part is None else part + d

        @pl.when(j == 0)
        def _():
            o_ref[...] = part

        @pl.when(j > 0)
        def _():
            o_ref[...] += part

        if has_res or scale != 1.0:
            @pl.when(j == n_blk - 1)
            def _():
                val = o_ref[...] * scale if scale != 1.0 else o_ref[...]
                o_ref[...] = ins[2 * na][...] + val if has_res else val

        end()

    row = pl.BlockSpec((tm, n_len), lambda t, j: (t, 0))
    out = pl.pallas_call(
        body, name=name, grid=(t_steps, n_blk),
        in_specs=list(a_specs(tm)) + list(b_specs) + ([row] if has_res else []) + [ANY] * nr,
        out_specs=[row] + [ANY] * nr,
        out_shape=[_sds((t_len, n_len), F32)] + [r.out_shape() for r in riders],
        scratch_shapes=_comm_sems() * nr,
        compiler_params=_params("arbitrary", "arbitrary"),
    )(*a_list, *b_list, *([res] if has_res else []), *[r.src for r in riders])
    return out[0], out[1:]


def mm_jsum(name, a_list, b_list, nt, res, scale, n_len, riders=(), tm_want=512, tn=512):
    n_j, t_len, k_j = a_list[0].shape
    tm = _row_tile(t_len, tm_want)
    tn = min(tn, n_len)
    na, nr = len(a_list), len(riders)
    has_res = res is not None
    t_steps, n_steps = t_len // tm, n_len // tn

    def body(*refs):
        ins, r_in, outs, r_out, _, r_sem = _split_refs(refs, 2 * na + has_res, 1, 0, nr)
        o_ref = outs[0]
        step = pl.program_id(0) * t_steps + pl.program_id(1)
        begin, end = _rider_hooks(riders, r_in, r_out, r_sem, step, n_steps * t_steps)
        begin()
        for c0 in range(0, tn, MXU_WIDTH):
            cols = slice(c0, min(c0 + MXU_WIDTH, tn))
            acc = None
            for ar, br in zip(ins[:na], ins[na:2 * na]):
                for j in range(n_j):
                    bj = br[j, cols, :] if nt else br[j, :, cols]
                    d = lax.dot_general(ar[j], bj, _NT if nt else _NN, preferred_element_type=F32)
                    acc = d if acc is None else acc + d
            val = acc * scale if scale != 1.0 else acc
            o_ref[:, cols] = ins[2 * na][:, cols] + val if has_res else val
        end()

    a_spec = pl.BlockSpec((n_j, tm, k_j), lambda n, t: (0, t, 0))
    b_spec = (pl.BlockSpec((n_j, tn, k_j), lambda n, t: (0, n, 0)) if nt
              else pl.BlockSpec((n_j, k_j, tn), lambda n, t: (0, 0, n)))
    tile = pl.BlockSpec((tm, tn), lambda n, t: (t, n))
    out = pl.pallas_call(
        body, name=name, grid=(n_steps, t_steps),
        in_specs=[a_spec] * na + [b_spec] * na + ([tile] if has_res else []) + [ANY] * nr,
        out_specs=[tile] + [ANY] * nr,
        out_shape=[_sds((t_len, n_len), F32)] + [r.out_shape() for r in riders],
        scratch_shapes=_comm_sems() * nr,
        compiler_params=_params("arbitrary", "arbitrary"),
    )(*a_list, *b_list, *([res] if has_res else []), *[r.src for r in riders])
    return out[0], out[1:]


def mm_xt(name, xt, dy, dy_spec, n_cols, n_blk, riders=(), tm_want=512):
    m_len, t_len = xt.shape
    tm = _row_tile(m_len, tm_want)
    m_steps = m_len // tm
    nr = len(riders)

    def body(*refs):
        ins, r_in, outs, r_out, _, r_sem = _split_refs(refs, 2, 1, 0, nr)
        step = pl.program_id(0) * m_steps + pl.program_id(1)
        begin, end = _rider_hooks(riders, r_in, r_out, r_sem, step, n_blk * m_steps)
        begin()
        outs[0][...] = lax.dot_general(ins[0][...], ins[1][...], _NN, preferred_element_type=F32).astype(BF16)
        end()

    res = pl.pallas_call(
        body, name=name, grid=(n_blk, m_steps),
        in_specs=[pl.BlockSpec((tm, t_len), lambda j, m: (m, 0)), dy_spec] + [ANY] * nr,
        out_specs=[pl.BlockSpec((None, tm, n_cols), lambda j, m: (j, m, 0))] + [ANY] * nr,
        out_shape=[_sds((n_blk, m_len, n_cols), BF16)] + [r.out_shape() for r in riders],
        scratch_shapes=_comm_sems() * nr,
        compiler_params=_params("arbitrary", "arbitrary"),
    )(xt, dy, *[r.src for r in riders])
    return res[0], res[1:]


def mm_tn(name, x, x_spec, dy_list, dy_specs, out_shapes, out_specs, scale, t_len, n_blk, riders=()):
    tt = _row_tile(t_len, 2048)
    nd, nr = len(dy_list), len(riders)
    t_steps = t_len // tt
    acc_shapes = [pltpu.VMEM(spec.block_shape[-2:], F32) for spec in out_specs]

    def body(*refs):
        ins, r_in, outs, r_out, accs, r_sem = _split_refs(refs, 1 + nd, nd, nd, nr)
        t = pl.program_id(1)
        step = pl.program_id(0) * t_steps + t
        begin, end = _rider_hooks(riders, r_in, r_out, r_sem, step, n_blk * t_steps)
        begin()
        xv = ins[0][...]
        for dr, acc in zip(ins[1:], accs):
            d = lax.dot_general(xv, dr[...], _TN, preferred_element_type=F32)

            @pl.when(t == 0)
            def _():
                acc[...] = d

            @pl.when(t > 0)
            def _():
                acc[...] += d

        @pl.when(t == t_steps - 1)
        def _():
            for acc, orf in zip(accs, outs):
                val = acc[...] * scale if scale != 1.0 else acc[...]
                orf[...] = val.astype(orf.dtype)

        end()

    res = pl.pallas_call(
        body, name=name, grid=(n_blk, t_steps),
        in_specs=[x_spec(tt)] + list(dy_specs(tt)) + [ANY] * nr,
        out_specs=list(out_specs) + [ANY] * nr,
        out_shape=list(out_shapes) + [r.out_shape() for r in riders],
        scratch_shapes=acc_shapes + _comm_sems() * nr,
        compiler_params=_params("arbitrary", "arbitrary"),
    )(x, *dy_list, *[r.src for r in riders])
    return res[:nd], res[nd:]


def rms_fwd(x, g, name):
    t_len, d = x.shape
    tm = _row_tile(t_len, 512)

    def body(x_ref, g_ref, h_ref, ht_ref):
        xv = x_ref[...]
        r = lax.rsqrt(jnp.mean(xv * xv, axis=-1, keepdims=True) + EPS)
        hv = xv * r * g_ref[...]
        h_ref[...] = hv.astype(BF16)
        ht_ref[...] = hv.T.astype(BF16)

    row = pl.BlockSpec((tm, d), lambda i: (i, 0))
    return pl.pallas_call(
        body, name=name, grid=(t_len // tm,),
        in_specs=[row, pl.BlockSpec((1, d), lambda i: (0, 0))],
        out_specs=[row, pl.BlockSpec((d, tm), lambda i: (0, i))],
        out_shape=[_sds((t_len, d), BF16), _sds((d, t_len), BF16)],
        compiler_params=_params("parallel"),
    )(x, g)


def rms_bwd(x, g, dh, dres, name):
    t_len, d = x.shape
    tm = _row_tile(t_len, 512)

    def body(x_ref, g_ref, dh_ref, dr_ref, dx_ref, dxb_ref, dg_ref):
        i = pl.program_id(0)
        xv = x_ref[...]
        r = lax.rsqrt(jnp.mean(xv * xv, axis=-1, keepdims=True) + EPS)
        xh = xv * r
        dhv = dh_ref[...]

        @pl.when(i == 0)
        def _():
            dg_ref[...] = jnp.zeros_like(dg_ref)

        dg_ref[...] += jnp.sum(dhv * xh, axis=0, keepdims=True)
        dxh = dhv * g_ref[...]
        dx = dr_ref[...] + r * (dxh - xh * jnp.mean(dxh * xh, axis=-1, keepdims=True))
        dx_ref[...] = dx
        dxb_ref[...] = dx.astype(BF16)

    row = pl.BlockSpec((tm, d), lambda i: (i, 0))
    vec = pl.BlockSpec((1, d), lambda i: (0, 0))
    return pl.pallas_call(
        body, name=name, grid=(t_len // tm,),
        in_specs=[row, vec, row, row],
        out_specs=[row, row, vec],
        out_shape=[_sds((t_len, d), F32), _sds((t_len, d), BF16), _sds((1, d), F32)],
        compiler_params=_params("arbitrary"),
    )(x, g, dh, dres)


def loss_head(y, target, name):
    t_len, d = y.shape
    tm = _row_tile(t_len, 512)

    def body(y_ref, t_ref, l_ref, dy_ref, dyb_ref):
        i = pl.program_id(0)
        err = y_ref[...] - t_ref[...]

        @pl.when(i == 0)
        def _():
            l_ref[...] = jnp.zeros_like(l_ref)

        rows = jnp.sum(err * err, axis=-1, keepdims=True) * (1.0 / d)
        l_ref[...] += 0.5 * jnp.sum(rows, axis=0, keepdims=True)
        dy = err * (1.0 / d)
        dy_ref[...] = dy
        dyb_ref[...] = dy.astype(BF16)

    row = pl.BlockSpec((tm, d), lambda i: (i, 0))
    return pl.pallas_call(
        body, name=name, grid=(t_len // tm,),
        in_specs=[row, row],
        out_specs=[pl.BlockSpec((8, LANES), lambda i: (0, 0)), row, row],
        out_shape=[_sds((8, LANES), F32), _sds((t_len, d), F32), _sds((t_len, d), BF16)],
        compiler_params=_params("arbitrary"),
    )(y, target)


def _conv_specs(tm, ch):
    per = tm // HALO
    cur = lambda cb: pl.BlockSpec((tm, ch), lambda i: (i, cb))
    prev = lambda cb: pl.BlockSpec((HALO, ch), lambda i: (jnp.maximum(i * per - 1, 0), cb))
    return [cur(0), cur(1), prev(0), prev(1)]


def _tap_scratch(rows, ch):
    return pltpu.VMEM((SUBLANES, rows + SUBLANES, ch), F32)


def _shifted_copies(buf, rows):
    buf[0, rows:rows + SUBLANES, :] = jnp.zeros((SUBLANES, buf.shape[2]), F32)
    for s in range(1, SUBLANES):
        buf[s, 0:rows, :] = buf[0, pl.ds(s, rows), :]


def _tap_rows(buf, off):
    shift = off % SUBLANES
    return buf[shift, off - shift:off - shift + ROW_CHUNK, :]


def _fill_glu(ext, a_ref, gt_ref, ap_ref, gp_ref, i, tm):
    vp = ap_ref[...] * _sigmoid(gp_ref[...])
    ext[0, 0:HALO, :] = jnp.where(i > 0, vp, 0.0)
    ext[0, HALO:HALO + tm, :] = a_ref[...] * _sigmoid(gt_ref[...])
    _shifted_copies(ext, HALO + tm)


def _conv_rows(ext, w_ref, b_ref, r0):
    acc = jnp.broadcast_to(b_ref[...], (ROW_CHUNK, b_ref.shape[1]))
    for k in range(CONV_WIDTH):
        acc = acc + w_ref[k:k + 1, :] * _tap_rows(ext, r0 + HALO - (CONV_WIDTH - 1) + k)
    return acc


def _layer_norm(yv):
    mu = jnp.mean(yv, axis=-1, keepdims=True)
    cen = yv - mu
    var = jnp.mean(cen * cen, axis=-1, keepdims=True)
    rstd = lax.rsqrt(var + EPS)
    return cen * rstd, rstd


def conv_fwd(z, w, b, lg, lb, name):
    t_len = z.shape[0]
    ch = w.shape[1]
    tm = _row_tile(t_len, 256)

    def body(a_ref, gt_ref, ap_ref, gp_ref, w_ref, b_ref, lg_ref, lb_ref, y_ref, pre_ref, ext):
        i = pl.program_id(0)
        _fill_glu(ext, a_ref, gt_ref, ap_ref, gp_ref, i, tm)
        for r0 in range(0, tm, ROW_CHUNK):
            pre = _conv_rows(ext, w_ref, b_ref, r0)
            pre_ref[r0:r0 + ROW_CHUNK, :] = pre
            xh, _ = _layer_norm(pre)
            u = xh * lg_ref[...] + lb_ref[...]
            y_ref[r0:r0 + ROW_CHUNK, :] = (u * _sigmoid(u)).astype(BF16)

    vec = pl.BlockSpec((1, ch), lambda i: (0, 0))
    row = pl.BlockSpec((tm, ch), lambda i: (i, 0))
    return pl.pallas_call(
        body, name=name, grid=(t_len // tm,),
        in_specs=_conv_specs(tm, ch) + [pl.BlockSpec((32, ch), lambda i: (0, 0)), vec, vec, vec],
        out_specs=[row, row],
        out_shape=[_sds((t_len, ch), BF16), _sds((t_len, ch), F32)],
        scratch_shapes=[_tap_scratch(HALO + tm, ch)],
        compiler_params=_params("parallel"),
    )(z, z, z, z, w, b, lg, lb)


def conv_bwd_norm(pre, dy_cat, lg, lb, name):
    t_len, ch = pre.shape
    tm = _row_tile(t_len, 256)

    def body(pre_ref, dy_ref, lg_ref, lb_ref, dc_ref, dlg_ref, dlb_ref, db_ref):
        i = pl.program_id(0)

        @pl.when(i == 0)
        def _():
            dlg_ref[...] = jnp.zeros_like(dlg_ref)
            dlb_ref[...] = jnp.zeros_like(dlb_ref)
            db_ref[...] = jnp.zeros_like(db_ref)

        for r0 in range(0, tm, ROW_CHUNK):
            xh, rstd = _layer_norm(pre_ref[r0:r0 + ROW_CHUNK, :])
            u = xh * lg_ref[...] + lb_ref[...]
            sg = _sigmoid(u)
            du = dy_ref[r0:r0 + ROW_CHUNK, :] * (sg * (1.0 + u * (1.0 - sg)))
            dlg_ref[...] += jnp.sum(du * xh, axis=0, keepdims=True)
            dlb_ref[...] += jnp.sum(du, axis=0, keepdims=True)
            dxh = du * lg_ref[...]
            dc = rstd * (dxh - jnp.mean(dxh, axis=-1, keepdims=True)
                         - xh * jnp.mean(dxh * xh, axis=-1, keepdims=True))
            db_ref[...] += jnp.sum(dc, axis=0, keepdims=True)
            dc_ref[r0:r0 + ROW_CHUNK, :] = dc

    vec = pl.BlockSpec((1, ch), lambda i: (0, 0))
    row = pl.BlockSpec((tm, ch), lambda i: (i, 0))
    return pl.pallas_call(
        body, name=name, grid=(t_len // tm,),
        in_specs=[row, row, vec, vec],
        out_specs=[row, vec, vec, vec],
        out_shape=[_sds((t_len, ch), F32)] + [_sds((1, ch), F32)] * 3,
        compiler_params=_params("arbitrary"),
    )(pre, dy_cat, lg, lb)


def conv_bwd_taps(z, dc, w, name):
    t_len = z.shape[0]
    ch = w.shape[1]
    tm = _row_tile(t_len, 256)
    per = tm // HALO
    n_tiles = t_len // tm
    last_halo = t_len // HALO - 1

    def body(a_ref, gt_ref, ap_ref, gp_ref, dc_ref, dn_ref, w_ref, dz_a_ref, dz_g_ref, dw_ref, ext, dext):
        i = pl.program_id(0)
        _fill_glu(ext, a_ref, gt_ref, ap_ref, gp_ref, i, tm)
        dext[0, 0:tm, :] = dc_ref[...]
        dext[0, tm:tm + HALO, :] = jnp.where(i < n_tiles - 1, dn_ref[...], 0.0)
        _shifted_copies(dext, tm + HALO)

        @pl.when(i == 0)
        def _():
            dw_ref[...] = jnp.zeros_like(dw_ref)

        for r0 in range(0, tm, ROW_CHUNK):
            dcv = dext[0, r0:r0 + ROW_CHUNK, :]
            dv = jnp.zeros((ROW_CHUNK, ch), F32)
            for k in range(CONV_WIDTH):
                dv = dv + w_ref[k:k + 1, :] * _tap_rows(dext, r0 + (CONV_WIDTH - 1) - k)
                prod = dcv * _tap_rows(ext, r0 + HALO - (CONV_WIDTH - 1) + k)
                fold = prod[0:8]
                for s in range(8, ROW_CHUNK, 8):
                    fold = fold + prod[s:s + 8]
                dw_ref[k] += fold
            av = a_ref[r0:r0 + ROW_CHUNK, :]
            sg = _sigmoid(gt_ref[r0:r0 + ROW_CHUNK, :])
            dz_a_ref[r0:r0 + ROW_CHUNK, :] = (dv * sg).astype(BF16)
            dz_g_ref[r0:r0 + ROW_CHUNK, :] = (dv * av * sg * (1.0 - sg)).astype(BF16)

    row = pl.BlockSpec((tm, ch), lambda i: (i, 0))
    nxt = pl.BlockSpec((HALO, ch), lambda i: (jnp.minimum((i + 1) * per, last_halo), 0))
    return pl.pallas_call(
        body, name=name, grid=(n_tiles,),
        in_specs=_conv_specs(tm, ch) + [row, nxt, pl.BlockSpec((32, ch), lambda i: (0, 0))],
        out_specs=[row, row, pl.BlockSpec((32, 8, ch), lambda i: (0, 0, 0))],
        out_shape=[_sds((t_len, ch), BF16), _sds((t_len, ch), BF16), _sds((32, 8, ch), F32)],
        scratch_shapes=[_tap_scratch(HALO + tm, ch), _tap_scratch(tm + HALO, ch)],
        compiler_params=_params("arbitrary"),
    )(z, z, z, z, dc, dc, w)


def _head_masks(rows):
    lane = lax.broadcasted_iota(jnp.int32, (rows, LANES), 1)
    low = lane < HEAD_DIM
    return low, jnp.logical_not(low)


def _per_head_mean(val, low):
    s_low = jnp.sum(jnp.where(low, val, 0.0), axis=-1, keepdims=True)
    s_high = jnp.sum(jnp.where(low, 0.0, val), axis=-1, keepdims=True)
    return jnp.where(low, s_low, s_high) * (1.0 / HEAD_DIM)


def qk_norm_fwd(z, g2, ch, name):
    t_len = z.shape[0]
    tm = _row_tile(t_len, 512)

    def body(z_ref, g_ref, o_ref):
        low, _ = _head_masks(tm)
        for c0 in range(0, ch, LANES):
            cols = slice(c0, c0 + LANES)
            xv = z_ref[:, cols]
            r = lax.rsqrt(_per_head_mean(xv * xv, low) + EPS)
            o_ref[:, cols] = xv * r * g_ref[:, cols]

    return pl.pallas_call(
        body, name=name, grid=(t_len // tm, 2),
        in_specs=[pl.BlockSpec((tm, ch), lambda i, w: (i, 2 + w)),
                  pl.BlockSpec((1, ch), lambda i, w: (0, w))],
        out_specs=pl.BlockSpec((tm, ch), lambda i, w: (i, w)),
        out_shape=_sds((t_len, 2 * ch), F32),
        compiler_params=_params("parallel", "parallel"),
    )(z, g2)


def qk_norm_bwd(z, g, d_list, z_off, ch, name):
    t_len = z.shape[0]
    tm = _row_tile(t_len, 512)
    nd = len(d_list)

    def body(*refs):
        z_ref, g_ref, d_refs = refs[0], refs[1], refs[2:2 + nd]
        dz_ref, dg_ref = refs[2 + nd], refs[3 + nd]
        low, _ = _head_masks(tm)

        @pl.when(pl.program_id(0) == 0)
        def _():
            dg_ref[...] = jnp.zeros_like(dg_ref)

        for c0 in range(0, ch, LANES):
            cols = slice(c0, c0 + LANES)
            xv = z_ref[:, cols]
            r = lax.rsqrt(_per_head_mean(xv * xv, low) + EPS)
            xh = xv * r
            dy = d_refs[0][:, cols]
            for dr in d_refs[1:]:
                dy = dy + dr[:, cols]
            dg_ref[...] += jnp.sum(dy * xh, axis=0, keepdims=True)
            dxh = dy * g_ref[...]
            dz_ref[:, cols] = (r * (dxh - xh * _per_head_mean(dxh * xh, low))).astype(BF16)

    blk = pl.BlockSpec((tm, ch), lambda i: (i, 0))
    return pl.pallas_call(
        body, name=name, grid=(t_len // tm,),
        in_specs=[pl.BlockSpec((tm, ch), lambda i: (i, z_off)),
                  pl.BlockSpec((1, LANES), lambda i: (0, 0))] + [blk] * nd,
        out_specs=[blk, pl.BlockSpec((1, LANES), lambda i: (0, 0))],
        out_shape=[_sds((t_len, ch), BF16), _sds((1, LANES), F32)],
        compiler_params=_params("arbitrary"),
    )(z, g, *d_list)


def _alibi_bias(n_heads, dilation):
    slopes = 2.0 ** (-ALIBI_MAX_BIAS * jnp.arange(1, n_heads + 1, dtype=F32) / n_heads)
    qi = jnp.arange(ATT_BLOCK)[:, None]
    kj = jnp.arange(ATT_BLOCK)[None, :]
    dist_cur = (qi - kj).astype(F32)
    dist_prev = (ATT_BLOCK + qi - kj).astype(F32)
    cur = jnp.where((qi >= kj)[None], -slopes[:, None, None] * (dilation * dist_cur)[None], MASKED)
    prev = jnp.where((kj >= qi)[None], -slopes[:, None, None] * (dilation * dist_prev)[None], MASKED)
    return jnp.concatenate([prev, cur], axis=-1).astype(F32)


def _stack_heads(val, low, high):
    return jnp.concatenate([jnp.where(low, val, 0.0), jnp.where(high, val, 0.0)], axis=0).astype(BF16)


def _head_rows(val, low, high):
    other = pltpu.roll(val, HEAD_DIM, axis=1)
    rows = jnp.concatenate([jnp.where(low, val, other), jnp.where(high, val, other)], axis=0)
    return jnp.concatenate([rows, rows], axis=1)


def _unit_scores(q2, k2, b_ref, has_prev):
    s = lax.dot_general(q2, k2, _NT, preferred_element_type=F32)
    s = s + b_ref[...].reshape(2 * ATT_BLOCK, 2 * ATT_BLOCK)
    penalty = jnp.where(has_prev, 0.0, MASKED)
    return jnp.concatenate([s[:, :ATT_BLOCK] + penalty, s[:, ATT_BLOCK:]], axis=1)


def _strided_rows(r, dilation):
    per = ATT_CHUNK // dilation
    return pl.ds(r, per, stride=dilation) if dilation > 1 else pl.ds(0, per)


def _deinterleave(dst, src_ref, dilation, base=None, dtype=None):
    per = ATT_CHUNK // dilation
    for r in range(dilation):
        val = src_ref[_strided_rows(r, dilation), :]
        val = val if dtype is None else val.astype(dtype)
        if base is None:
            dst[r * per:(r + 1) * per, :] = val
        else:
            dst[pl.ds(pl.multiple_of(base + r * per, ATT_BLOCK), per), :] = val


def _unit_rows(u, c, nb, base, pbase):
    in_chunk = lax.rem(u, jnp.int32(nb)) > 0
    has_prev = jnp.logical_or(in_chunk, c > 0)
    urow = pl.multiple_of(u * ATT_BLOCK, ATT_BLOCK)
    crow = pl.multiple_of(base + u * ATT_BLOCK, ATT_BLOCK)
    prow = pl.multiple_of(jnp.where(in_chunk, base + (u - 1) * ATT_BLOCK,
                                    pbase + (u + nb - 1) * ATT_BLOCK), ATT_BLOCK)
    return in_chunk, has_prev, urow, crow, prow


def _interleave(dst_ref, src, dilation, base=None):
    per = ATT_CHUNK // dilation
    for r in range(dilation):
        if base is None:
            val = src[r * per:(r + 1) * per, :]
        else:
            val = src[pl.ds(pl.multiple_of(base + r * per, ATT_BLOCK), per), :]
        dst_ref[_strided_rows(r, dilation), :] = val


def attn_fwd(qk, z, dilation, ch, name):
    t_len = qk.shape[0]
    pairs = ch // LANES
    nc = t_len // ATT_CHUNK
    nb = ATT_UNITS // dilation
    scale = 1.0 / math.sqrt(HEAD_DIM)
    bias = _alibi_bias(2 * pairs, dilation)

    def body(q_ref, k_ref, v_ref, b_ref, o_ref, l_ref, qd, kx, vx, od, ld):
        c = pl.program_id(1)
        slot = lax.rem(c, jnp.int32(2))
        base, pbase = slot * ATT_CHUNK, (1 - slot) * ATT_CHUNK

        @pl.when(c == 0)
        def _():
            kx[...] = jnp.zeros_like(kx)
            vx[...] = jnp.zeros_like(vx)

        _deinterleave(qd, q_ref, dilation)
        _deinterleave(kx, k_ref, dilation, base, BF16)
        _deinterleave(vx, v_ref, dilation, base, BF16)
        low, high = _head_masks(ATT_BLOCK)

        def unit(u, carry):
            _, has_prev, urow, crow, prow = _unit_rows(u, c, nb, base, pbase)
            q2 = _stack_heads(qd[pl.ds(urow, ATT_BLOCK), :] * scale, low, high)
            k2 = jnp.concatenate([kx[pl.ds(prow, ATT_BLOCK), :], kx[pl.ds(crow, ATT_BLOCK), :]], axis=0)
            v2 = jnp.concatenate([vx[pl.ds(prow, ATT_BLOCK), :], vx[pl.ds(crow, ATT_BLOCK), :]], axis=0)
            s = _unit_scores(q2, k2, b_ref, has_prev)
            mx = jnp.max(s, axis=-1, keepdims=True)
            e = jnp.exp(s - mx)
            den = jnp.sum(e, axis=-1, keepdims=True)
            acc = lax.dot_general(e.astype(BF16), v2, _NN, preferred_element_type=F32) / den
            lse = jnp.broadcast_to(mx + jnp.log(den), acc.shape)
            od[pl.ds(urow, ATT_BLOCK), :] = jnp.where(low, acc[:ATT_BLOCK], acc[ATT_BLOCK:])
            ld[pl.ds(urow, ATT_BLOCK), :] = jnp.where(low, lse[:ATT_BLOCK], lse[ATT_BLOCK:])
            return carry

        lax.fori_loop(0, ATT_UNITS, unit, 0, unroll=8)
        _interleave(o_ref, od, dilation)
        _interleave(l_ref, ld, dilation)

    blk = (ATT_CHUNK, LANES)
    bias_spec = pl.BlockSpec((2, ATT_BLOCK, 2 * ATT_BLOCK), lambda p, c: (p, 0, 0))
    out_spec = pl.BlockSpec(blk, lambda p, c: (c, p))
    return pl.pallas_call(
        body, name=name, grid=(pairs, nc),
        in_specs=[pl.BlockSpec(blk, lambda p, c: (c, p)),
                  pl.BlockSpec(blk, lambda p, c: (c, pairs + p)),
                  pl.BlockSpec(blk, lambda p, c: (c, 4 * pairs + p)),
                  bias_spec],
        out_specs=[out_spec, out_spec],
        out_shape=[_sds((t_len, ch), F32)] * 2,
        scratch_shapes=[pltpu.VMEM((ATT_CHUNK, LANES), F32),
                        pltpu.VMEM((2 * ATT_CHUNK, LANES), BF16), pltpu.VMEM((2 * ATT_CHUNK, LANES), BF16),
                        pltpu.VMEM((ATT_CHUNK, LANES), F32), pltpu.VMEM((ATT_CHUNK, LANES), F32)],
        compiler_params=_params("arbitrary", "arbitrary"),
    )(qk, qk, z, bias)


def attn_combine(outs, lses, name):
    t_len, ch = outs[0].shape
    tm = _row_tile(t_len, 512)

    def body(o1, o2, o3, l1, l2, l3, out_ref, outb_ref, lg_ref):
        a, b, c = l1[...], l2[...], l3[...]
        mx = jnp.maximum(jnp.maximum(a, b), c)
        tot = mx + jnp.log(jnp.exp(a - mx) + jnp.exp(b - mx) + jnp.exp(c - mx))
        val = jnp.exp(a - tot) * o1[...] + jnp.exp(b - tot) * o2[...] + jnp.exp(c - tot) * o3[...]
        out_ref[...] = val
        outb_ref[...] = val.astype(BF16)
        lg_ref[...] = tot

    row = pl.BlockSpec((tm, ch), lambda i: (i, 0))
    return pl.pallas_call(
        body, name=name, grid=(t_len // tm,),
        in_specs=[row] * 6, out_specs=[row] * 3,
        out_shape=[_sds((t_len, ch), F32), _sds((t_len, ch), BF16), _sds((t_len, ch), F32)],
        compiler_params=_params("parallel"),
    )(*outs, *lses)


def attn_bwd(qk, z, dy_cat, out, lg, dilation, ch, name):
    t_len = qk.shape[0]
    pairs = ch // LANES
    nc = t_len // ATT_CHUNK
    nb = ATT_UNITS // dilation
    scale = 1.0 / math.sqrt(HEAD_DIM)
    bias = _alibi_bias(2 * pairs, dilation)

    def body(q_ref, k_ref, v_ref, do_ref, out_ref, lg_ref, b_ref, dq_ref, dk_ref, dv_ref,
             qd, dod, lgd, dld, dl_nat, kx, vx, dkx, dvx, dqd):
        c = pl.program_id(1)
        slot = lax.rem(c, jnp.int32(2))
        base, pbase = slot * ATT_CHUNK, (1 - slot) * ATT_CHUNK

        @pl.when(c == 0)
        def _():
            for ref in (kx, vx, dkx, dvx):
                ref[...] = jnp.zeros_like(ref)

        @pl.when(c < nc)
        def _():
            low_all, _ = _head_masks(ATT_CHUNK)
            dl_nat[...] = _per_head_mean(do_ref[...] * out_ref[...], low_all) * float(HEAD_DIM)
            _deinterleave(qd, q_ref, dilation)
            _deinterleave(dod, do_ref, dilation)
            _deinterleave(lgd, lg_ref, dilation)
            _deinterleave(dld, dl_nat, dilation)
            _deinterleave(kx, k_ref, dilation, base, BF16)
            _deinterleave(vx, v_ref, dilation, base, BF16)
            low, high = _head_masks(ATT_BLOCK)

            def unit(u, carry):
                _, has_prev, urow, crow, prow = _unit_rows(u, c, nb, base, pbase)
                rows = pl.ds(urow, ATT_BLOCK)
                q2 = _stack_heads(qd[rows, :] * scale, low, high)
                do2 = _stack_heads(dod[rows, :], low, high)
                lse = _head_rows(lgd[rows, :], low, high)
                delta = _head_rows(dld[rows, :], low, high)
                k2 = jnp.concatenate([kx[pl.ds(prow, ATT_BLOCK), :], kx[pl.ds(crow, ATT_BLOCK), :]], axis=0)
                v2 = jnp.concatenate([vx[pl.ds(prow, ATT_BLOCK), :], vx[pl.ds(crow, ATT_BLOCK), :]], axis=0)
                prob = jnp.exp(_unit_scores(q2, k2, b_ref, has_prev) - lse)
                dp = lax.dot_general(do2, v2, _NT, preferred_element_type=F32)
                ds = (prob * (dp - delta)).astype(BF16)
                dq2 = lax.dot_general(ds, k2, _NN, preferred_element_type=F32)
                dk2 = lax.dot_general(ds, q2, _TN, preferred_element_type=F32)
                dv2 = lax.dot_general(prob.astype(BF16), do2, _TN, preferred_element_type=F32)
                dqd[rows, :] = scale * jnp.where(low, dq2[:ATT_BLOCK], dq2[ATT_BLOCK:])
                dkx[pl.ds(prow, ATT_BLOCK), :] += dk2[:ATT_BLOCK]
                dkx[pl.ds(crow, ATT_BLOCK), :] = dk2[ATT_BLOCK:]
                dvx[pl.ds(prow, ATT_BLOCK), :] += dv2[:ATT_BLOCK]
                dvx[pl.ds(crow, ATT_BLOCK), :] = dv2[ATT_BLOCK:]
                return carry

            lax.fori_loop(0, ATT_UNITS, unit, 0, unroll=8)
            _interleave(dq_ref, dqd, dilation)

        @pl.when(c > 0)
        def _():
            _interleave(dk_ref, dkx, dilation, pbase)
            _interleave(dv_ref, dvx, dilation, pbase)

    blk = (ATT_CHUNK, LANES)
    here = lambda c: jnp.minimum(c, nc - 1)
    spec = lambda off: pl.BlockSpec(blk, lambda p, c: (here(c), off + p))
    late = pl.BlockSpec(blk, lambda p, c: (jnp.maximum(c - 1, 0), p))
    bias_spec = pl.BlockSpec((2, ATT_BLOCK, 2 * ATT_BLOCK), lambda p, c: (p, 0, 0))
    f32_chunk = pltpu.VMEM((ATT_CHUNK, LANES), F32)
    return pl.pallas_call(
        body, name=name, grid=(pairs, nc + 1),
        in_specs=[spec(0), spec(pairs), spec(4 * pairs), spec(pairs), spec(0), spec(0), bias_spec],
        out_specs=[spec(0), late, late],
        out_shape=[_sds((t_len, ch), F32)] * 3,
        scratch_shapes=[f32_chunk] * 5
                       + [pltpu.VMEM((2 * ATT_CHUNK, LANES), BF16)] * 2
                       + [pltpu.VMEM((2 * ATT_CHUNK, LANES), F32)] * 2 + [f32_chunk],
        compiler_params=_params("arbitrary", "arbitrary"),
    )(qk, qk, z, dy_cat, out, lg, bias)


def sum3_bf16(a, b, c, name):
    t_len, ch = a.shape
    tm = _row_tile(t_len, 512)

    def body(a_ref, b_ref, c_ref, o_ref):
        o_ref[...] = (a_ref[...] + b_ref[...] + c_ref[...]).astype(BF16)

    row = pl.BlockSpec((tm, ch), lambda i: (i, 0))
    return pl.pallas_call(
        body, name=name, grid=(t_len // tm,), in_specs=[row] * 3, out_specs=row,
        out_shape=_sds((t_len, ch), BF16), compiler_params=_params("parallel"),
    )(a, b, c)


def _blk3(rows, cols):
    return pl.BlockSpec((None, rows, cols), lambda j, t: (j, 0, 0))


def ffn_up(h, wg, wu, name, riders):
    t_len, d = h.shape
    n_blk, _, fj = wg.shape

    def epilogue(accs, e_refs, o_refs, cols):
        gate, up = accs
        o_refs[0][:, cols] = gate.astype(BF16)
        o_refs[1][:, cols] = up.astype(BF16)
        o_refs[2][:, cols] = (gate * _sigmoid(gate) * up).astype(BF16)

    act = lambda tm: pl.BlockSpec((None, tm, fj), lambda j, t: (j, t, 0))
    return mm_cols(name, h, [wg, wu], [_blk3(d, fj)] * 2, False, [], lambda tm: [],
                   [_sds((n_blk, t_len, fj), BF16)] * 3, lambda tm: [act(tm)] * 3, epilogue, n_blk, riders,
                   tm_want=1024)


def ffn_gate(h, wg, name, riders):
    t_len, d = h.shape
    n_blk, _, fj = wg.shape

    def epilogue(accs, e_refs, o_refs, cols):
        o_refs[0][:, cols] = accs[0].astype(BF16)

    act = lambda tm: pl.BlockSpec((None, tm, fj), lambda j, t: (j, t, 0))
    return mm_cols(name, h, [wg], [_blk3(d, fj)], False, [], lambda tm: [],
                   [_sds((n_blk, t_len, fj), BF16)], lambda tm: [act(tm)], epilogue, n_blk, riders, tm_want=1024)


def ffn_up_after_gate(h, wu, gate, name, riders):
    t_len, d = h.shape
    n_blk, _, fj = wu.shape

    def epilogue(accs, e_refs, o_refs, cols):
        gv = e_refs[0][:, cols].astype(F32)
        o_refs[0][:, cols] = accs[0].astype(BF16)
        o_refs[1][:, cols] = (gv * _sigmoid(gv) * accs[0]).astype(BF16)

    act = lambda tm: pl.BlockSpec((None, tm, fj), lambda j, t: (j, t, 0))
    return mm_cols(name, h, [wu], [_blk3(d, fj)], False, [gate], lambda tm: [act(tm)],
                   [_sds((n_blk, t_len, fj), BF16)] * 2, lambda tm: [act(tm)] * 2, epilogue, n_blk, riders,
                   tm_want=1024)


def ffn_down(act, wd, res, name, riders):
    n_blk, t_len, fj = act.shape
    d = wd.shape[2]
    return mm_jsum(name, [act], [wd], False, res, 0.5, d, riders)


def ffn_bwd(ht, gate, up, act, wg, wu, wd, dyb, name):
    d, t_len = ht.shape
    n_blk, _, fj = act.shape

    def epilogue(accs, e_refs, o_refs, cols):
        d_act = 0.5 * accs[0]
        gv, uv = e_refs[0][:, cols].astype(F32), e_refs[1][:, cols].astype(F32)
        sg = _sigmoid(gv)
        o_refs[0][:, cols] = (d_act * uv * (sg * (1.0 + gv * (1.0 - sg)))).astype(BF16)
        o_refs[1][:, cols] = (d_act * gv * sg).astype(BF16)

    act_jt = lambda tm: pl.BlockSpec((None, tm, fj), lambda j, t: (j, t, 0))
    (d_gate, d_up), _ = mm_cols(name + "_dact", dyb, [wd], [_blk3(fj, d)], True, [gate, up],
                                lambda tm: [act_jt(tm)] * 2, [_sds((n_blk, t_len, fj), BF16)] * 2,
                                lambda tm: [act_jt(tm)] * 2, epilogue, n_blk, tm_want=1024)

    whole_t = pl.BlockSpec((None, t_len, fj), lambda j, m: (j, 0, 0))
    d_wg, _ = mm_xt(name + "_dwg", ht, d_gate, whole_t, fj, n_blk)
    d_wu, (recv_wg,) = mm_xt(name + "_dwu", ht, d_up, whole_t, fj, n_blk, [Rider("scatter", d_wg)])

    (d_wd,), (recv_wu,) = mm_tn(name + "_dwd", act, act_jt, [dyb],
                                lambda tt: [pl.BlockSpec((tt, d), lambda j, t: (t, 0))],
                                [_sds((n_blk, fj, d), BF16)], [_blk3(fj, d)], 0.5, t_len, n_blk,
                                [Rider("scatter", d_wu)])

    dh, (recv_wd,) = mm_jsum(name + "_dh", [d_gate, d_up], [wg, wu], True, None, 1.0, d,
                             [Rider("scatter", d_wd)])
    return dh, recv_wg, recv_wu, recv_wd


def local_step(x, target, g1, wg1, wu1_s, wd1_s, gmix, win_s, conv_w, conv_b, ln_g, ln_b, gq, gk, wout_s, g3,
               wg2_s, wu2_s, wd2_s):
    t_len, d = x.shape
    ch = d // 2
    ij = win_s.shape[1]
    oj = wout_s.shape[0]
    n_blk = N_DEV

    h1, h1t = rms_fwd(x, g1, "rms1")
    (gate1,), (wu1,) = ffn_gate(h1, wg1, "ffn1_gate", [Rider("gather", wu1_s)])
    (up1, act1), (wd1, wg2) = ffn_up_after_gate(h1, wu1, gate1, "ffn1_up",
                                                [Rider("gather", wd1_s), Rider("gather", wg2_s)])
    x1, (win, wu2) = ffn_down(act1, wd1, x, "ffn1_down", [Rider("gather", win_s), Rider("gather", wu2_s)])

    h2, h2t = rms_fwd(x1, gmix, "rms_mix")

    def store_f32(accs, e_refs, o_refs, cols):
        o_refs[0][:, cols] = accs[0]

    d_in = n_blk * ij
    win_full = jnp.transpose(win, (1, 0, 2)).reshape(1, d, d_in)
    wide = d_in // W_IN_SPLIT
    (z,), (wd2, wout) = mm_cols(
        "w_in", h2, [win_full], [pl.BlockSpec((None, d, wide), lambda j, t: (0, 0, j))], False, [], lambda tm: [],
        [_sds((t_len, d_in), F32)],
        lambda tm: [pl.BlockSpec((tm, wide), lambda j, t: (t, j))], store_f32, W_IN_SPLIT,
        [Rider("gather", wd2_s), Rider("gather", wout_s)], tm_want=1024)

    conv_w32 = jnp.pad(conv_w, ((0, 32 - CONV_WIDTH), (0, 0)))
    y_conv, conv_pre = conv_fwd(z, conv_w32, conv_b, ln_g, ln_b, "conv_fwd")

    g2 = jnp.concatenate([jnp.tile(gq, (1, ch // HEAD_DIM)), jnp.tile(gk, (1, ch // HEAD_DIM))], axis=1)
    qk = qk_norm_fwd(z, g2, ch, "qk_norm")
    branch = [attn_fwd(qk, z, dil, ch, "attn_fwd_d%d" % dil) for dil in DILATIONS]
    att, att_b, lg = attn_combine([o for o, _ in branch], [l for _, l in branch], "attn_combine")

    y_cat = jnp.concatenate([y_conv, att_b], axis=1)
    wout_full = wout.reshape(1, n_blk * oj, d)
    x2, _ = mm_reduce(
        "w_out", [y_cat], lambda tm: [pl.BlockSpec((tm, n_blk * oj), lambda t, j: (t, 0))],
        [wout_full], [pl.BlockSpec((None, n_blk * oj, d), lambda t, j: (0, 0, 0))], False, x1, 1.0,
        t_len, d, 1)

    h3, h3t = rms_fwd(x2, g3, "rms3")
    (gate2, up2, act2), _ = ffn_up(h3, wg2, wu2, "ffn2_up", [])
    y, _ = ffn_down(act2, wd2, x2, "ffn2_down", [])

    loss_tile, dy, dyb = loss_head(y, target, "loss")

    dh3, recv_wg2, recv_wu2, recv_wd2 = ffn_bwd(h3t, gate2, up2, act2, wg2, wu2, wd2, dyb, "ffn2")
    dx2, dx2b, d_g3 = rms_bwd(x2, g3, dh3, dy, "rms3_bwd")

    (dy_cat,), _ = mm_cols("w_out_dy", dx2b, [wout_full], [_blk3(n_blk * oj, d)], True, [], lambda tm: [],
                           [_sds((t_len, n_blk * oj), F32)],
                           lambda tm: [pl.BlockSpec((tm, n_blk * oj), lambda j, t: (t, 0))], store_f32, 1,
                           tm_want=1024)
    (d_wout,), _ = mm_tn("w_out_dw", y_cat, lambda tt: pl.BlockSpec((tt, oj), lambda j, t: (t, j)),
                         [dx2b], lambda tt: [pl.BlockSpec((tt, d), lambda j, t: (t, 0))],
                         [_sds((n_blk, oj, d), BF16)], [_blk3(oj, d)], 1.0, t_len, n_blk)

    dc, d_lg, d_lb, d_cb = conv_bwd_norm(conv_pre, dy_cat, ln_g, ln_b, "conv_bwd_norm")
    dz_a, dz_g, d_cw8 = conv_bwd_taps(z, dc, conv_w32, "conv_bwd_taps")
    d_cw = jnp.sum(d_cw8, axis=1)[:CONV_WIDTH]

    grads = [attn_bwd(qk, z, dy_cat, att, lg, dil, ch, "attn_bwd_d%d" % dil) for dil in DILATIONS]
    gq_t = jnp.tile(gq, (1, LANES // HEAD_DIM))
    gk_t = jnp.tile(gk, (1, LANES // HEAD_DIM))
    dz_q, d_gq2 = qk_norm_bwd(z, gq_t, [g[0] for g in grads], 2, ch, "q_norm_bwd")
    dz_k, d_gk2 = qk_norm_bwd(z, gk_t, [g[1] for g in grads], 3, ch, "k_norm_bwd")
    d_gq = d_gq2[:, :HEAD_DIM] + d_gq2[:, HEAD_DIM:]
    d_gk = d_gk2[:, :HEAD_DIM] + d_gk2[:, HEAD_DIM:]
    dz_v = sum3_bf16(grads[0][2], grads[1][2], grads[2][2], "dv_sum")
    dzb = jnp.concatenate([dz_a, dz_g, dz_q, dz_k, dz_v], axis=1)

    d_win, (recv_wout,) = mm_xt("w_in_dw", h2t, dzb, pl.BlockSpec((t_len, ij), lambda j, m: (0, j)), ij, n_blk,
                                [Rider("scatter", d_wout)])
    tall = d // W_IN_SPLIT
    (dh2,), (recv_win,) = mm_cols(
        "w_in_dh", dzb, [win_full], [pl.BlockSpec((None, tall, d_in), lambda j, t: (0, j, 0))], True, [],
        lambda tm: [], [_sds((t_len, d), F32)],
        lambda tm: [pl.BlockSpec((tm, tall), lambda j, t: (t, j))], store_f32, W_IN_SPLIT,
        [Rider("scatter", d_win)], tm_want=1024)
    dx1, dx1b, d_gmix = rms_bwd(x1, gmix, dh2, dx2, "rms_mix_bwd")

    dh1, recv_wg1, recv_wu1, recv_wd1 = ffn_bwd(h1t, gate1, up1, act1, wg1, wu1, wd1, dx1b, "ffn1")
    grad_x, _, d_g1 = rms_bwd(x, g1, dh1, dx1, "rms1_bwd")

    big = dict(ffn1_w_gate=recv_wg1, ffn1_w_up=recv_wu1, ffn1_w_down=recv_wd1, w_in=recv_win, w_out=recv_wout,
               ffn2_w_gate=recv_wg2, ffn2_w_up=recv_wu2, ffn2_w_down=recv_wd2)
    small = dict(g1=d_g1, gmix=d_gmix, g3=d_g3, conv_b=d_cb, ln_g=d_lg, ln_b=d_lb, gq=d_gq, gk=d_gk, conv_w=d_cw)
    return loss_tile[0, 0], grad_x, big, small


SMALL_ROWS = 48


def _pack_small(ch, g1, gmix, g3, conv_b, ln_g, ln_b, gq, gk, conv_w):
    pad_head = lambda v: jnp.pad(v, ((0, 0), (0, ch - v.shape[1])))
    rows = [g1.reshape(2, ch), gmix.reshape(2, ch), g3.reshape(2, ch), conv_b, ln_g, ln_b,
            pad_head(gq), pad_head(gk), conv_w]
    packed = jnp.concatenate(rows, axis=0)
    return jnp.pad(packed, ((0, SMALL_ROWS - packed.shape[0]), (0, 0)))


def _unpack_small(packed, d):
    return dict(g1=packed[0:2].reshape(1, d), gmix=packed[2:4].reshape(1, d), g3=packed[4:6].reshape(1, d),
                conv_b=packed[6:7], ln_g=packed[7:8], ln_b=packed[8:9],
                gq=packed[9:10, :HEAD_DIM], gk=packed[10:11, :HEAD_DIM])


def kernel(x, ffn1_norm_g, ffn1_w_gate, ffn1_w_up, ffn1_w_down, mix_norm_g, w_in, conv_w_dw, conv_b_dw, conv_ln_g, conv_ln_b, q_norm_g, k_norm_g, w_out, ffn2_norm_g, ffn2_w_gate, ffn2_w_up, ffn2_w_down, loss_target, m_ffn1_norm_g, m_ffn1_w_gate, m_ffn1_w_up, m_ffn1_w_down, m_mix_norm_g, m_w_in, m_conv_w_dw, m_conv_b_dw, m_conv_ln_g, m_conv_ln_b, m_q_norm_g, m_k_norm_g, m_w_out, m_ffn2_norm_g, m_ffn2_w_gate, m_ffn2_w_up, m_ffn2_w_down, v_ffn1_norm_g, v_ffn1_w_gate, v_ffn1_w_up, v_ffn1_w_down, v_mix_norm_g, v_w_in, v_conv_w_dw, v_conv_b_dw, v_conv_ln_g, v_conv_ln_b, v_q_norm_g, v_k_norm_g, v_w_out, v_ffn2_norm_g, v_ffn2_w_gate, v_ffn2_w_up, v_ffn2_w_down):
    d = x.shape[-1]
    ch = d // 2
    me = 4 * lax.axis_index("x") + 2 * lax.axis_index("y") + lax.axis_index("c")

    shard = lambda w: w[0].astype(BF16)
    wg1 = all_gather(shard(ffn1_w_gate), "ag_wg1")
    cw_all = all_gather(conv_w_dw[0], "ag_convw")
    conv_w = jnp.transpose(cw_all, (1, 0, 2)).reshape(CONV_WIDTH, ch)

    loss_part, grad_x, big, small = local_step(
        x[0], loss_target[0], ffn1_norm_g, wg1, shard(ffn1_w_up), shard(ffn1_w_down), mix_norm_g, shard(w_in),
        conv_w, conv_b_dw, conv_ln_g, conv_ln_b, q_norm_g, k_norm_g, shard(w_out), ffn2_norm_g,
        shard(ffn2_w_gate), shard(ffn2_w_up), shard(ffn2_w_down))
    loss = lax.psum(loss_part, MESH_AXES)

    state = dict(
        ffn1_w_gate=(ffn1_w_gate, m_ffn1_w_gate, v_ffn1_w_gate), ffn1_w_up=(ffn1_w_up, m_ffn1_w_up, v_ffn1_w_up),
        ffn1_w_down=(ffn1_w_down, m_ffn1_w_down, v_ffn1_w_down), w_in=(w_in, m_w_in, v_w_in),
        w_out=(w_out, m_w_out, v_w_out),
        ffn2_w_gate=(ffn2_w_gate, m_ffn2_w_gate, v_ffn2_w_gate), ffn2_w_up=(ffn2_w_up, m_ffn2_w_up, v_ffn2_w_up),
        ffn2_w_down=(ffn2_w_down, m_ffn2_w_down, v_ffn2_w_down))
    out = {}
    for pname, (w, m, v) in state.items():
        out[pname] = [r[None] for r in adamw(w[0], m[0], v[0], big[pname], "adamw_" + pname)]

    zero_taps = jnp.zeros((CONV_WIDTH, ch), F32)
    pack = lambda g1, gm, g3, cb, lg, lb, gq, gk: _pack_small(ch, g1, gm, g3, cb, lg, lb, gq, gk, zero_taps)
    small_parts = all_gather(_pack_small(ch, **small), "ag_small_grads")
    s_res = adamw(
        pack(ffn1_norm_g, mix_norm_g, ffn2_norm_g, conv_b_dw, conv_ln_g, conv_ln_b, q_norm_g, k_norm_g),
        pack(m_ffn1_norm_g, m_mix_norm_g, m_ffn2_norm_g, m_conv_b_dw, m_conv_ln_g, m_conv_ln_b, m_q_norm_g, m_k_norm_g),
        pack(v_ffn1_norm_g, v_mix_norm_g, v_ffn2_norm_g, v_conv_b_dw, v_conv_ln_g, v_conv_ln_b, v_q_norm_g, v_k_norm_g),
        small_parts, "adamw_small")
    s_out = [_unpack_small(r, d) for r in s_res]
    names = dict(g1="ffn1_norm_g", gmix="mix_norm_g", g3="ffn2_norm_g", conv_b="conv_b_dw", ln_g="conv_ln_g",
                 ln_b="conv_ln_b", gq="q_norm_g", gk="k_norm_g")
    for key, full in names.items():
        out[full] = [r[key] for r in s_out]

    cshard = ch // N_DEV
    taps_sum = s_res[0][11:11 + CONV_WIDTH]
    taps_mine = lax.dynamic_slice(taps_sum, (0, me * cshard), (CONV_WIDTH, cshard))
    pad_taps = lambda a: jnp.pad(a, ((0, 32 - CONV_WIDTH), (0, 0)))
    c_res = adamw(pad_taps(conv_w_dw[0]), pad_taps(m_conv_w_dw[0]), pad_taps(v_conv_w_dw[0]),
                  pad_taps(taps_mine)[None], "adamw_convw")
    out["conv_w_dw"] = [r[:CONV_WIDTH][None] for r in c_res]

    order = ["ffn1_norm_g", "ffn1_w_gate", "ffn1_w_up", "ffn1_w_down", "mix_norm_g", "w_in", "conv_w_dw",
             "conv_b_dw", "conv_ln_g", "conv_ln_b", "q_norm_g", "k_norm_g", "w_out", "ffn2_norm_g",
             "ffn2_w_gate", "ffn2_w_up", "ffn2_w_down"]
    result = [loss, grad_x[None]]
    for kind in range(4):
        result += [out[n][kind] for n in order]
    return tuple(result)
```

```python
import math
from typing import NamedTuple

import jax
import jax.numpy as jnp
from jax import lax
from jax.experimental import pallas as pl
from jax.experimental.pallas import tpu as pltpu

F32 = jnp.float32
BF16 = jnp.bfloat16

N_DEV = 8
EPS = 1e-6
HEAD_DIM = 64
LANES = 128
MXU_WIDTH = 256
W_IN_SPLIT = 4
CONV_WIDTH = 31
SUBLANES = 8
HALO = 32
ROW_CHUNK = 32
ATT_BLOCK = 128
DILATIONS = (1, 4, 16)
ATT_UNITS = 16
ATT_CHUNK = ATT_UNITS * ATT_BLOCK
ALIBI_MAX_BIAS = 8.0
MASKED = -1e30
VMEM_LIMIT = 56 * 1024 * 1024

ADAM_LR = 0.001
ADAM_B1 = 0.9
ADAM_B2 = 0.999
ADAM_EPS = 1e-08
ADAM_WD = 0.01
ADAM_STEP = 10

MESH_AXES = ("x", "y", "c")
ANY = pl.BlockSpec(memory_space=pl.ANY)


def _sds(shape, dtype):
    return jax.ShapeDtypeStruct(tuple(shape), dtype)


def _params(*sem):
    return pltpu.CompilerParams(dimension_semantics=sem, vmem_limit_bytes=VMEM_LIMIT)


def _sigmoid(v):
    return 1.0 / (1.0 + jnp.exp(-v))


def _row_tile(t, want):
    for cand in range(min(want, t) // 8 * 8, 0, -8):
        if t % cand == 0:
            return cand
    return t


def _mesh_pos():
    return lax.axis_index("x"), lax.axis_index("y"), lax.axis_index("c")


def _comm_sems():
    return [pltpu.SemaphoreType.DMA((7,)), pltpu.SemaphoreType.DMA((7,)), pltpu.SemaphoreType.DMA(())]


def _gather_phases(x_ref, out_ref, send_sems, recv_sems, local_sem, by_cols=False):
    x, y, c = _mesh_pos()
    me, sibling = (x, y, c), (x, y, 1 - c)
    chips = [(1 - x, y), (x, 1 - y), (1 - x, 1 - y)]

    def slot(px, py, pc):
        idx = 4 * px + 2 * py + pc
        if not by_cols:
            return out_ref.at[idx]
        width = x_ref.shape[1]
        return out_ref.at[:, pl.ds(pl.multiple_of(idx * width, LANES), width)]

    def copy(k, block, to, src=None):
        return pltpu.make_async_remote_copy(
            src_ref=slot(*block) if src is None else src, dst_ref=slot(*block),
            send_sem=send_sems.at[k], recv_sem=recv_sems.at[k],
            device_id=to, device_id_type=pl.DeviceIdType.MESH)

    mine = pltpu.make_async_copy(x_ref, slot(*me), local_sem)
    first = [copy(0, me, sibling, src=x_ref)]
    first += [copy(1 + j, me, (*chip, c), src=x_ref) for j, chip in enumerate(chips)]
    passed = [copy(4 + j, (*chip, c), sibling) for j, chip in enumerate(chips)]

    def start():
        mine.start()
        for cp in first:
            cp.start()

    def forward():
        for j, chip in enumerate(chips):
            copy(1 + j, (*chip, c), me).wait_recv()
            passed[j].start()

    def finish():
        copy(0, sibling, me).wait_recv()
        for j, chip in enumerate(chips):
            copy(4 + j, (*chip, 1 - c), me).wait_recv()
        for cp in first + passed:
            cp.wait_send()
        mine.wait()

    return start, forward, finish


def _scatter_phases(p_ref, out_ref, send_sems, recv_sems, local_sem):
    x, y, c = _mesh_pos()
    me = 4 * x + 2 * y + c
    flips = [(fx, fy, fc) for fx in (0, 1) for fy in (0, 1) for fc in (0, 1)][1:]

    def copy(k, flip, receiving):
        px, py, pc = (1 - x if flip[0] else x, 1 - y if flip[1] else y, 1 - c if flip[2] else c)
        them = 4 * px + 2 * py + pc
        return pltpu.make_async_remote_copy(
            src_ref=p_ref.at[them], dst_ref=out_ref.at[them if receiving else me],
            send_sem=send_sems.at[k], recv_sem=recv_sems.at[k],
            device_id=(px, py, pc), device_id_type=pl.DeviceIdType.MESH)

    mine = pltpu.make_async_copy(p_ref.at[me], out_ref.at[me], local_sem)

    def start():
        mine.start()
        for k, flip in enumerate(flips):
            copy(k, flip, False).start()

    def finish():
        for k, flip in enumerate(flips):
            copy(k, flip, True).wait_recv()
            copy(k, flip, False).wait_send()
        mine.wait()

    return start, None, finish


class Rider(NamedTuple):
    kind: str
    src: jax.Array

    def out_shape(self):
        if self.kind == "gather":
            shape = (N_DEV,) + self.src.shape
        elif self.kind == "gather_cols":
            shape = (self.src.shape[0], N_DEV * self.src.shape[1])
        else:
            shape = self.src.shape
        return _sds(shape, self.src.dtype)

    def phases(self, in_ref, out_ref, sems):
        if self.kind == "scatter":
            return _scatter_phases(in_ref, out_ref, *sems)
        return _gather_phases(in_ref, out_ref, *sems, by_cols=self.kind == "gather_cols")


def _rider_hooks(riders, in_refs, out_refs, sem_refs, step, n_steps):
    phases = [r.phases(in_refs[i], out_refs[i], sem_refs[3 * i:3 * i + 3]) for i, r in enumerate(riders)]

    def begin():
        for start, forward, _ in phases:
            pl.when(step == 0)(start)
            if forward is not None:
                pl.when(step == (7 * n_steps) // 8)(forward)

    def end():
        for _, _, finish in phases:
            pl.when(step == n_steps - 1)(finish)

    return begin, end


def _split_refs(refs, n_in, n_out, n_scratch, n_riders):
    pos, parts = 0, []
    for n in (n_in, n_riders, n_out, n_riders, n_scratch, 3 * n_riders):
        parts.append(refs[pos:pos + n])
        pos += n
    return parts


def all_gather(shard, name):
    def body(x_ref, out_ref, send_sems, recv_sems, local_sem):
        start, forward, finish = _gather_phases(x_ref, out_ref, send_sems, recv_sems, local_sem)
        start()
        forward()
        finish()

    return pl.pallas_call(
        body, name=name, out_shape=_sds((N_DEV,) + shard.shape, shard.dtype),
        in_specs=[ANY], out_specs=ANY, scratch_shapes=_comm_sems(),
    )(shard)


def adamw(w, m, v, parts, name):
    n_parts, rows, cols = parts.shape
    tr = _row_tile(rows, 128)
    c1 = 1.0 - ADAM_B1 ** ADAM_STEP
    c2 = 1.0 - ADAM_B2 ** ADAM_STEP

    def body(w_ref, m_ref, v_ref, p_ref, g_ref, d_ref, nm_ref, nv_ref):
        g = p_ref[0].astype(F32)
        for s in range(1, n_parts):
            g = g + p_ref[s].astype(F32)
        nm = ADAM_B1 * m_ref[...] + (1.0 - ADAM_B1) * g
        nv = ADAM_B2 * v_ref[...] + (1.0 - ADAM_B2) * (g * g)
        delta = -ADAM_LR * ((nm / c1) / (jnp.sqrt(nv / c2) + ADAM_EPS) + ADAM_WD * w_ref[...])
        g_ref[...] = g
        d_ref[...] = delta
        nm_ref[...] = nm
        nv_ref[...] = nv

    mat = pl.BlockSpec((tr, cols), lambda i: (i, 0))
    return pl.pallas_call(
        body, name=name, grid=(rows // tr,),
        in_specs=[mat, mat, mat, pl.BlockSpec((n_parts, tr, cols), lambda i: (0, i, 0))],
        out_specs=[mat, mat, mat, mat],
        out_shape=[_sds((rows, cols), F32)] * 4,
        compiler_params=_params("parallel"),
    )(w, m, v, parts)


_NN = (((1,), (0,)), ((), ()))
_NT = (((1,), (1,)), ((), ()))
_TN = (((0,), (0,)), ((), ()))


def mm_cols(name, a, b_list, b_specs, nt, extras, extra_specs, out_shapes, out_specs, epilogue, n_blk, riders=(),
            tm_want=512):
    t_len, k_len = a.shape
    tm = _row_tile(t_len, tm_want)
    nb, ne, n_out, nr = len(b_list), len(extras), len(out_shapes), len(riders)
    t_steps = t_len // tm
    n_cols = b_specs[0].block_shape[-2 if nt else -1]

    def body(*refs):
        ins, r_in, outs, r_out, _, r_sem = _split_refs(refs, 1 + nb + ne, n_out, 0, nr)
        step = pl.program_id(0) * t_steps + pl.program_id(1)
        begin, end = _rider_hooks(riders, r_in, r_out, r_sem, step, n_blk * t_steps)
        begin()
        av = ins[0][...]
        for c0 in range(0, n_cols, MXU_WIDTH):
            cols = slice(c0, min(c0 + MXU_WIDTH, n_cols))
            accs = [lax.dot_general(av, br[cols, :] if nt else br[:, cols], _NT if nt else _NN,
                                    preferred_element_type=F32) for br in ins[1:1 + nb]]
            epilogue(accs, ins[1 + nb:], outs, cols)
        end()

    res = pl.pallas_call(
        body, name=name, grid=(n_blk, t_steps),
        in_specs=([pl.BlockSpec((tm, k_len), lambda j, t: (t, 0))] + list(b_specs) + list(extra_specs(tm))
                  + [ANY] * nr),
        out_specs=list(out_specs(tm)) + [ANY] * nr,
        out_shape=list(out_shapes) + [r.out_shape() for r in riders],
        scratch_shapes=_comm_sems() * nr,
        compiler_params=_params("arbitrary", "arbitrary"),
    )(a, *b_list, *extras, *[r.src for r in riders])
    return res[:n_out], res[n_out:]


def mm_reduce(name, a_list, a_specs, b_list, b_specs, nt, res, scale, t_len, n_len, n_blk, riders=(), tm_want=512):
    tm = _row_tile(t_len, tm_want)
    na, nr = len(a_list), len(riders)
    has_res = res is not None
    t_steps = t_len // tm

    def body(*refs):
        ins, r_in, outs, r_out, _, r_sem = _split_refs(refs, 2 * na + has_res, 1, 0, nr)
        o_ref = outs[0]
        j = pl.program_id(1)
        step = pl.program_id(0) * n_blk + j
        begin, end = _rider_hooks(riders, r_in, r_out, r_sem, step, t_steps * n_blk)
        begin()

        part = None
        for ar, br in zip(ins[:na], ins[na:2 * na]):
            d = lax.dot_general(ar[...], br[...], _NT if nt else _NN, preferred_element_type=F32)
            part = d if part is None else part + d

        @pl.when(j == 0)
        def _():
            o_ref[...] = part

        @pl.when(j > 0)
        def _():
            o_ref[...] += part

        if has_res or scale != 1.0:
            @pl.when(j == n_blk - 1)
            def _():
                val = o_ref[...] * scale if scale != 1.0 else o_ref[...]
                o_ref[...] = ins[2 * na][...] + val if has_res else val

        end()

    row = pl.BlockSpec((tm, n_len), lambda t, j: (t, 0))
    out = pl.pallas_call(
        body, name=name, grid=(t_steps, n_blk),
        in_specs=list(a_specs(tm)) + list(b_specs) + ([row] if has_res else []) + [ANY] * nr,
        out_specs=[row] + [ANY] * nr,
        out_shape=[_sds((t_len, n_len), F32)] + [r.out_shape() for r in riders],
        scratch_shapes=_comm_sems() * nr,
        compiler_params=_params("arbitrary", "arbitrary"),
    )(*a_list, *b_list, *([res] if has_res else []), *[r.src for r in riders])
    return out[0], out[1:]


def mm_jsum(name, a_list, b_list, nt, res, scale, n_len, riders=(), tm_want=512, tn=512):
    n_j, t_len, k_j = a_list[0].shape
    tm = _row_tile(t_len, tm_want)
    tn = min(tn, n_len)
    na, nr = len(a_list), len(riders)
    has_res = res is not None
    t_steps, n_steps = t_len // tm, n_len // tn

    def body(*refs):
        ins, r_in, outs, r_out, _, r_sem = _split_refs(refs, 2 * na + has_res, 1, 0, nr)
        o_ref = outs[0]
        step = pl.program_id(0) * t_steps + pl.program_id(1)
        begin, end = _rider_hooks(riders, r_in, r_out, r_sem, step, n_steps * t_steps)
        begin()
        for c0 in range(0, tn, MXU_WIDTH):
            cols = slice(c0, min(c0 + MXU_WIDTH, tn))
            acc = None
            for ar, br in zip(ins[:na], ins[na:2 * na]):
                for j in range(n_j):
                    bj = br[j, cols, :] if nt else br[j, :, cols]
                    d = lax.dot_general(ar[j], bj, _NT if nt else _NN, preferred_element_type=F32)
                    acc = d if acc is None else acc + d
            val = acc * scale if scale != 1.0 else acc
            o_ref[:, cols] = ins[2 * na][:, cols] + val if has_res else val
        end()

    a_spec = pl.BlockSpec((n_j, tm, k_j), lambda n, t: (0, t, 0))
    b_spec = (pl.BlockSpec((n_j, tn, k_j), lambda n, t: (0, n, 0)) if nt
              else pl.BlockSpec((n_j, k_j, tn), lambda n, t: (0, 0, n)))
    tile = pl.BlockSpec((tm, tn), lambda n, t: (t, n))
    out = pl.pallas_call(
        body, name=name, grid=(n_steps, t_steps),
        in_specs=[a_spec] * na + [b_spec] * na + ([tile] if has_res else []) + [ANY] * nr,
        out_specs=[tile] + [ANY] * nr,
        out_shape=[_sds((t_len, n_len), F32)] + [r.out_shape() for r in riders],
        scratch_shapes=_comm_sems() * nr,
        compiler_params=_params("arbitrary", "arbitrary"),
    )(*a_list, *b_list, *([res] if has_res else []), *[r.src for r in riders])
    return out[0], out[1:]


def mm_xt(name, xt, dy, dy_spec, n_cols, n_blk, riders=(), tm_want=512):
    m_len, t_len = xt.shape
    tm = _row_tile(m_len, tm_want)
    m_steps = m_len // tm
    nr = len(riders)

    def body(*refs):
        ins, r_in, outs, r_out, _, r_sem = _split_refs(refs, 2, 1, 0, nr)
        step = pl.program_id(0) * m_steps + pl.program_id(1)
        begin, end = _rider_hooks(riders, r_in, r_out, r_sem, step, n_blk * m_steps)
        begin()
        outs[0][...] = lax.dot_general(ins[0][...], ins[1][...], _NN, preferred_element_type=F32).astype(BF16)
        end()

    res = pl.pallas_call(
        body, name=name, grid=(n_blk, m_steps),
        in_specs=[pl.BlockSpec((tm, t_len), lambda j, m: (m, 0)), dy_spec] + [ANY] * nr,
        out_specs=[pl.BlockSpec((None, tm, n_cols), lambda j, m: (j, m, 0))] + [ANY] * nr,
        out_shape=[_sds((n_blk, m_len, n_cols), BF16)] + [r.out_shape() for r in riders],
        scratch_shapes=_comm_sems() * nr,
        compiler_params=_params("arbitrary", "arbitrary"),
    )(xt, dy, *[r.src for r in riders])
    return res[0], res[1:]


def mm_tn(name, x, x_spec, dy_list, dy_specs, out_shapes, out_specs, scale, t_len, n_blk, riders=()):
    tt = _row_tile(t_len, 2048)
    nd, nr = len(dy_list), len(riders)
    t_steps = t_len // tt
    acc_shapes = [pltpu.VMEM(spec.block_shape[-2:], F32) for spec in out_specs]

    def body(*refs):
        ins, r_in, outs, r_out, accs, r_sem = _split_refs(refs, 1 + nd, nd, nd, nr)
        t = pl.program_id(1)
        step = pl.program_id(0) * t_steps + t
        begin, end = _rider_hooks(riders, r_in, r_out, r_sem, step, n_blk * t_steps)
        begin()
        xv = ins[0][...]
        for dr, acc in zip(ins[1:], accs):
            d = lax.dot_general(xv, dr[...], _TN, preferred_element_type=F32)

            @pl.when(t == 0)
            def _():
                acc[...] = d

            @pl.when(t > 0)
            def _():
                acc[...] += d

        @pl.when(t == t_steps - 1)
        def _():
            for acc, orf in zip(accs, outs):
                val = acc[...] * scale if scale != 1.0 else acc[...]
                orf[...] = val.astype(orf.dtype)

        end()

    res = pl.pallas_call(
        body, name=name, grid=(n_blk, t_steps),
        in_specs=[x_spec(tt)] + list(dy_specs(tt)) + [ANY] * nr,
        out_specs=list(out_specs) + [ANY] * nr,
        out_shape=list(out_shapes) + [r.out_shape() for r in riders],
        scratch_shapes=acc_shapes + _comm_sems() * nr,
        compiler_params=_params("arbitrary", "arbitrary"),
    )(x, *dy_list, *[r.src for r in riders])
    return res[:nd], res[nd:]


def rms_fwd(x, g, name):
    t_len, d = x.shape
    tm = _row_tile(t_len, 512)

    def body(x_ref, g_ref, h_ref, ht_ref):
        xv = x_ref[...]
        r = lax.rsqrt(jnp.mean(xv * xv, axis=-1, keepdims=True) + EPS)
        hv = xv * r * g_ref[...]
        h_ref[...] = hv.astype(BF16)
        ht_ref[...] = hv.T.astype(BF16)

    row = pl.BlockSpec((tm, d), lambda i: (i, 0))
    return pl.pallas_call(
        body, name=name, grid=(t_len // tm,),
        in_specs=[row, pl.BlockSpec((1, d), lambda i: (0, 0))],
        out_specs=[row, pl.BlockSpec((d, tm), lambda i: (0, i))],
        out_shape=[_sds((t_len, d), BF16), _sds((d, t_len), BF16)],
        compiler_params=_params("parallel"),
    )(x, g)


def rms_bwd(x, g, dh, dres, name):
    t_len, d = x.shape
    tm = _row_tile(t_len, 512)

    def body(x_ref, g_ref, dh_ref, dr_ref, dx_ref, dxb_ref, dg_ref):
        i = pl.program_id(0)
        xv = x_ref[...]
        r = lax.rsqrt(jnp.mean(xv * xv, axis=-1, keepdims=True) + EPS)
        xh = xv * r
        dhv = dh_ref[...]

        @pl.when(i == 0)
        def _():
            dg_ref[...] = jnp.zeros_like(dg_ref)

        dg_ref[...] += jnp.sum(dhv * xh, axis=0, keepdims=True)
        dxh = dhv * g_ref[...]
        dx = dr_ref[...] + r * (dxh - xh * jnp.mean(dxh * xh, axis=-1, keepdims=True))
        dx_ref[...] = dx
        dxb_ref[...] = dx.astype(BF16)

    row = pl.BlockSpec((tm, d), lambda i: (i, 0))
    vec = pl.BlockSpec((1, d), lambda i: (0, 0))
    return pl.pallas_call(
        body, name=name, grid=(t_len // tm,),
        in_specs=[row, vec, row, row],
        out_specs=[row, row, vec],
        out_shape=[_sds((t_len, d), F32), _sds((t_len, d), BF16), _sds((1, d), F32)],
        compiler_params=_params("arbitrary"),
    )(x, g, dh, dres)


def loss_head(y, target, name):
    t_len, d = y.shape
    tm = _row_tile(t_len, 512)

    def body(y_ref, t_ref, l_ref, dy_ref, dyb_ref):
        i = pl.program_id(0)
        err = y_ref[...] - t_ref[...]

        @pl.when(i == 0)
        def _():
            l_ref[...] = jnp.zeros_like(l_ref)

        rows = jnp.sum(err * err, axis=-1, keepdims=True) * (1.0 / d)
        l_ref[...] += 0.5 * jnp.sum(rows, axis=0, keepdims=True)
        dy = err * (1.0 / d)
        dy_ref[...] = dy
        dyb_ref[...] = dy.astype(BF16)

    row = pl.BlockSpec((tm, d), lambda i: (i, 0))
    return pl.pallas_call(
        body, name=name, grid=(t_len // tm,),
        in_specs=[row, row],
        out_specs=[pl.BlockSpec((8, LANES), lambda i: (0, 0)), row, row],
        out_shape=[_sds((8, LANES), F32), _sds((t_len, d), F32), _sds((t_len, d), BF16)],
        compiler_params=_params("arbitrary"),
    )(y, target)


def _conv_specs(tm, ch):
    per = tm // HALO
    cur = lambda cb: pl.BlockSpec((tm, ch), lambda i: (i, cb))
    prev = lambda cb: pl.BlockSpec((HALO, ch), lambda i: (jnp.maximum(i * per - 1, 0), cb))
    return [cur(0), cur(1), prev(0), prev(1)]


def _tap_scratch(rows, ch):
    return pltpu.VMEM((SUBLANES, rows + SUBLANES, ch), F32)


def _shifted_copies(buf, rows):
    buf[0, rows:rows + SUBLANES, :] = jnp.zeros((SUBLANES, buf.shape[2]), F32)
    for s in range(1, SUBLANES):
        buf[s, 0:rows, :] = buf[0, pl.ds(s, rows), :]


def _tap_rows(buf, off):
    shift = off % SUBLANES
    return buf[shift, off - shift:off - shift + ROW_CHUNK, :]


def _fill_glu(ext, a_ref, gt_ref, ap_ref, gp_ref, i, tm):
    vp = ap_ref[...] * _sigmoid(gp_ref[...])
    ext[0, 0:HALO, :] = jnp.where(i > 0, vp, 0.0)
    ext[0, HALO:HALO + tm, :] = a_ref[...] * _sigmoid(gt_ref[...])
    _shifted_copies(ext, HALO + tm)


def _conv_rows(ext, w_ref, b_ref, r0):
    acc = jnp.broadcast_to(b_ref[...], (ROW_CHUNK, b_ref.shape[1]))
    for k in range(CONV_WIDTH):
        acc = acc + w_ref[k:k + 1, :] * _tap_rows(ext, r0 + HALO - (CONV_WIDTH - 1) + k)
    return acc


def _layer_norm(yv):
    mu = jnp.mean(yv, axis=-1, keepdims=True)
    cen = yv - mu
    var = jnp.mean(cen * cen, axis=-1, keepdims=True)
    rstd = lax.rsqrt(var + EPS)
    return cen * rstd, rstd


def conv_fwd(z, w, b, lg, lb, name):
    t_len = z.shape[0]
    ch = w.shape[1]
    tm = _row_tile(t_len, 256)

    def body(a_ref, gt_ref, ap_ref, gp_ref, w_ref, b_ref, lg_ref, lb_ref, y_ref, pre_ref, ext):
        i = pl.program_id(0)
        _fill_glu(ext, a_ref, gt_ref, ap_ref, gp_ref, i, tm)
        for r0 in range(0, tm, ROW_CHUNK):
            pre = _conv_rows(ext, w_ref, b_ref, r0)
            pre_ref[r0:r0 + ROW_CHUNK, :] = pre
            xh, _ = _layer_norm(pre)
            u = xh * lg_ref[...] + lb_ref[...]
            y_ref[r0:r0 + ROW_CHUNK, :] = (u * _sigmoid(u)).astype(BF16)

    vec = pl.BlockSpec((1, ch), lambda i: (0, 0))
    row = pl.BlockSpec((tm, ch), lambda i: (i, 0))
    return pl.pallas_call(
        body, name=name, grid=(t_len // tm,),
        in_specs=_conv_specs(tm, ch) + [pl.BlockSpec((32, ch), lambda i: (0, 0)), vec, vec, vec],
        out_specs=[row, row],
        out_shape=[_sds((t_len, ch), BF16), _sds((t_len, ch), F32)],
        scratch_shapes=[_tap_scratch(HALO + tm, ch)],
        compiler_params=_params("parallel"),
    )(z, z, z, z, w, b, lg, lb)


def conv_bwd_norm(pre, dy_cat, lg, lb, name):
    t_len, ch = pre.shape
    tm = _row_tile(t_len, 256)

    def body(pre_ref, dy_ref, lg_ref, lb_ref, dc_ref, dlg_ref, dlb_ref, db_ref):
        i = pl.program_id(0)

        @pl.when(i == 0)
        def _():
            dlg_ref[...] = jnp.zeros_like(dlg_ref)
            dlb_ref[...] = jnp.zeros_like(dlb_ref)
            db_ref[...] = jnp.zeros_like(db_ref)

        for r0 in range(0, tm, ROW_CHUNK):
            xh, rstd = _layer_norm(pre_ref[r0:r0 + ROW_CHUNK, :])
            u = xh * lg_ref[...] + lb_ref[...]
            sg = _sigmoid(u)
            du = dy_ref[r0:r0 + ROW_CHUNK, :] * (sg * (1.0 + u * (1.0 - sg)))
            dlg_ref[...] += jnp.sum(du * xh, axis=0, keepdims=True)
            dlb_ref[...] += jnp.sum(du, axis=0, keepdims=True)
            dxh = du * lg_ref[...]
            dc = rstd * (dxh - jnp.mean(dxh, axis=-1, keepdims=True)
                         - xh * jnp.mean(dxh * xh, axis=-1, keepdims=True))
            db_ref[...] += jnp.sum(dc, axis=0, keepdims=True)
            dc_ref[r0:r0 + ROW_CHUNK, :] = dc

    vec = pl.BlockSpec((1, ch), lambda i: (0, 0))
    row = pl.BlockSpec((tm, ch), lambda i: (i, 0))
    return pl.pallas_call(
        body, name=name, grid=(t_len // tm,),
        in_specs=[row, row, vec, vec],
        out_specs=[row, vec, vec, vec],
        out_shape=[_sds((t_len, ch), F32)] + [_sds((1, ch), F32)] * 3,
        compiler_params=_params("arbitrary"),
    )(pre, dy_cat, lg, lb)


def conv_bwd_taps(z, dc, w, name):
    t_len = z.shape[0]
    ch = w.shape[1]
    tm = _row_tile(t_len, 256)
    per = tm // HALO
    n_tiles = t_len // tm
    last_halo = t_len // HALO - 1

    def body(a_ref, gt_ref, ap_ref, gp_ref, dc_ref, dn_ref, w_ref, dz_a_ref, dz_g_ref, dw_ref, ext, dext):
        i = pl.program_id(0)
        _fill_glu(ext, a_ref, gt_ref, ap_ref, gp_ref, i, tm)
        dext[0, 0:tm, :] = dc_ref[...]
        dext[0, tm:tm + HALO, :] = jnp.where(i < n_tiles - 1, dn_ref[...], 0.0)
        _shifted_copies(dext, tm + HALO)

        @pl.when(i == 0)
        def _():
            dw_ref[...] = jnp.zeros_like(dw_ref)

        for r0 in range(0, tm, ROW_CHUNK):
            dcv = dext[0, r0:r0 + ROW_CHUNK, :]
            dv = jnp.zeros((ROW_CHUNK, ch), F32)
            for k in range(CONV_WIDTH):
                dv = dv + w_ref[k:k + 1, :] * _tap_rows(dext, r0 + (CONV_WIDTH - 1) - k)
                prod = dcv * _tap_rows(ext, r0 + HALO - (CONV_WIDTH - 1) + k)
                fold = prod[0:8]
                for s in range(8, ROW_CHUNK, 8):
                    fold = fold + prod[s:s + 8]
                dw_ref[k] += fold
            av = a_ref[r0:r0 + ROW_CHUNK, :]
            sg = _sigmoid(gt_ref[r0:r0 + ROW_CHUNK, :])
            dz_a_ref[r0:r0 + ROW_CHUNK, :] = (dv * sg).astype(BF16)
            dz_g_ref[r0:r0 + ROW_CHUNK, :] = (dv * av * sg * (1.0 - sg)).astype(BF16)

    row = pl.BlockSpec((tm, ch), lambda i: (i, 0))
    nxt = pl.BlockSpec((HALO, ch), lambda i: (jnp.minimum((i + 1) * per, last_halo), 0))
    return pl.pallas_call(
        body, name=name, grid=(n_tiles,),
        in_specs=_conv_specs(tm, ch) + [row, nxt, pl.BlockSpec((32, ch), lambda i: (0, 0))],
        out_specs=[row, row, pl.BlockSpec((32, 8, ch), lambda i: (0, 0, 0))],
        out_shape=[_sds((t_len, ch), BF16), _sds((t_len, ch), BF16), _sds((32, 8, ch), F32)],
        scratch_shapes=[_tap_scratch(HALO + tm, ch), _tap_scratch(tm + HALO, ch)],
        compiler_params=_params("arbitrary"),
    )(z, z, z, z, dc, dc, w)


def _head_masks(rows):
    lane = lax.broadcasted_iota(jnp.int32, (rows, LANES), 1)
    low = lane < HEAD_DIM
    return low, jnp.logical_not(low)


def _per_head_mean(val, low):
    s_low = jnp.sum(jnp.where(low, val, 0.0), axis=-1, keepdims=True)
    s_high = jnp.sum(jnp.where(low, 0.0, val), axis=-1, keepdims=True)
    return jnp.where(low, s_low, s_high) * (1.0 / HEAD_DIM)


def qk_norm_fwd(z, g2, ch, name):
    t_len = z.shape[0]
    tm = _row_tile(t_len, 512)

    def body(z_ref, g_ref, o_ref):
        low, _ = _head_masks(tm)
        for c0 in range(0, ch, LANES):
            cols = slice(c0, c0 + LANES)
            xv = z_ref[:, cols]
            r = lax.rsqrt(_per_head_mean(xv * xv, low) + EPS)
            o_ref[:, cols] = xv * r * g_ref[:, cols]

    return pl.pallas_call(
        body, name=name, grid=(t_len // tm, 2),
        in_specs=[pl.BlockSpec((tm, ch), lambda i, w: (i, 2 + w)),
                  pl.BlockSpec((1, ch), lambda i, w: (0, w))],
        out_specs=pl.BlockSpec((tm, ch), lambda i, w: (i, w)),
        out_shape=_sds((t_len, 2 * ch), F32),
        compiler_params=_params("parallel", "parallel"),
    )(z, g2)


def qk_norm_bwd(z, g, d_list, z_off, ch, name):
    t_len = z.shape[0]
    tm = _row_tile(t_len, 512)
    nd = len(d_list)

    def body(*refs):
        z_ref, g_ref, d_refs = refs[0], refs[1], refs[2:2 + nd]
        dz_ref, dg_ref = refs[2 + nd], refs[3 + nd]
        low, _ = _head_masks(tm)

        @pl.when(pl.program_id(0) == 0)
        def _():
            dg_ref[...] = jnp.zeros_like(dg_ref)

        for c0 in range(0, ch, LANES):
            cols = slice(c0, c0 + LANES)
            xv = z_ref[:, cols]
            r = lax.rsqrt(_per_head_mean(xv * xv, low) + EPS)
            xh = xv * r
            dy = d_refs[0][:, cols]
            for dr in d_refs[1:]:
                dy = dy + dr[:, cols]
            dg_ref[...] += jnp.sum(dy * xh, axis=0, keepdims=True)
            dxh = dy * g_ref[...]
            dz_ref[:, cols] = (r * (dxh - xh * _per_head_mean(dxh * xh, low))).astype(BF16)

    blk = pl.BlockSpec((tm, ch), lambda i: (i, 0))
    return pl.pallas_call(
        body, name=name, grid=(t_len // tm,),
        in_specs=[pl.BlockSpec((tm, ch), lambda i: (i, z_off)),
                  pl.BlockSpec((1, LANES), lambda i: (0, 0))] + [blk] * nd,
        out_specs=[blk, pl.BlockSpec((1, LANES), lambda i: (0, 0))],
        out_shape=[_sds((t_len, ch), BF16), _sds((1, LANES), F32)],
        compiler_params=_params("arbitrary"),
    )(z, g, *d_list)


def _alibi_bias(n_heads, dilation):
    slopes = 2.0 ** (-ALIBI_MAX_BIAS * jnp.arange(1, n_heads + 1, dtype=F32) / n_heads)
    qi = jnp.arange(ATT_BLOCK)[:, None]
    kj = jnp.arange(ATT_BLOCK)[None, :]
    dist_cur = (qi - kj).astype(F32)
    dist_prev = (ATT_BLOCK + qi - kj).astype(F32)
    cur = jnp.where((qi >= kj)[None], -slopes[:, None, None] * (dilation * dist_cur)[None], MASKED)
    prev = jnp.where((kj >= qi)[None], -slopes[:, None, None] * (dilation * dist_prev)[None], MASKED)
    return jnp.concatenate([prev, cur], axis=-1).astype(F32)


def _stack_heads(val, low, high):
    return jnp.concatenate([jnp.where(low, val, 0.0), jnp.where(high, val, 0.0)], axis=0).astype(BF16)


def _head_rows(val, low, high):
    other = pltpu.roll(val, HEAD_DIM, axis=1)
    rows = jnp.concatenate([jnp.where(low, val, other), jnp.where(high, val, other)], axis=0)
    return jnp.concatenate([rows, rows], axis=1)


def _unit_scores(q2, k2, b_ref, has_prev):
    s = lax.dot_general(q2, k2, _NT, preferred_element_type=F32)
    s = s + b_ref[...].reshape(2 * ATT_BLOCK, 2 * ATT_BLOCK)
    penalty = jnp.where(has_prev, 0.0, MASKED)
    return jnp.concatenate([s[:, :ATT_BLOCK] + penalty, s[:, ATT_BLOCK:]], axis=1)


def _strided_rows(r, dilation):
    per = ATT_CHUNK // dilation
    return pl.ds(r, per, stride=dilation) if dilation > 1 else pl.ds(0, per)


def _deinterleave(dst, src_ref, dilation, base=None, dtype=None):
    per = ATT_CHUNK // dilation
    for r in range(dilation):
        val = src_ref[_strided_rows(r, dilation), :]
        val = val if dtype is None else val.astype(dtype)
        if base is None:
            dst[r * per:(r + 1) * per, :] = val
        else:
            dst[pl.ds(pl.multiple_of(base + r * per, ATT_BLOCK), per), :] = val


def _unit_rows(u, c, nb, base, pbase):
    in_chunk = lax.rem(u, jnp.int32(nb)) > 0
    has_prev = jnp.logical_or(in_chunk, c > 0)
    urow = pl.multiple_of(u * ATT_BLOCK, ATT_BLOCK)
    crow = pl.multiple_of(base + u * ATT_BLOCK, ATT_BLOCK)
    prow = pl.multiple_of(jnp.where(in_chunk, base + (u - 1) * ATT_BLOCK,
                                    pbase + (u + nb - 1) * ATT_BLOCK), ATT_BLOCK)
    return in_chunk, has_prev, urow, crow, prow


def _interleave(dst_ref, src, dilation, base=None):
    per = ATT_CHUNK // dilation
    for r in range(dilation):
        if base is None:
            val = src[r * per:(r + 1) * per, :]
        else:
            val = src[pl.ds(pl.multiple_of(base + r * per, ATT_BLOCK), per), :]
        dst_ref[_strided_rows(r, dilation), :] = val


def attn_fwd(qk, z, dilation, ch, name):
    t_len = qk.shape[0]
    pairs = ch // LANES
    nc = t_len // ATT_CHUNK
    nb = ATT_UNITS // dilation
    scale = 1.0 / math.sqrt(HEAD_DIM)
    bias = _alibi_bias(2 * pairs, dilation)

    def body(q_ref, k_ref, v_ref, b_ref, o_ref, l_ref, qd, kx, vx, od, ld):
        c = pl.program_id(1)
        slot = lax.rem(c, jnp.int32(2))
        base, pbase = slot * ATT_CHUNK, (1 - slot) * ATT_CHUNK

        @pl.when(c == 0)
        def _():
            kx[...] = jnp.zeros_like(kx)
            vx[...] = jnp.zeros_like(vx)

        _deinterleave(qd, q_ref, dilation)
        _deinterleave(kx, k_ref, dilation, base, BF16)
        _deinterleave(vx, v_ref, dilation, base, BF16)
        low, high = _head_masks(ATT_BLOCK)

        def unit(u, carry):
            _, has_prev, urow, crow, prow = _unit_rows(u, c, nb, base, pbase)
            q2 = _stack_heads(qd[pl.ds(urow, ATT_BLOCK), :] * scale, low, high)
            k2 = jnp.concatenate([kx[pl.ds(prow, ATT_BLOCK), :], kx[pl.ds(crow, ATT_BLOCK), :]], axis=0)
            v2 = jnp.concatenate([vx[pl.ds(prow, ATT_BLOCK), :], vx[pl.ds(crow, ATT_BLOCK), :]], axis=0)
            s = _unit_scores(q2, k2, b_ref, has_prev)
            mx = jnp.max(s, axis=-1, keepdims=True)
            e = jnp.exp(s - mx)
            den = jnp.sum(e, axis=-1, keepdims=True)
            acc = lax.dot_general(e.astype(BF16), v2, _NN, preferred_element_type=F32) / den
            lse = jnp.broadcast_to(mx + jnp.log(den), acc.shape)
            od[pl.ds(urow, ATT_BLOCK), :] = jnp.where(low, acc[:ATT_BLOCK], acc[ATT_BLOCK:])
            ld[pl.ds(urow, ATT_BLOCK), :] = jnp.where(low, lse[:ATT_BLOCK], lse[ATT_BLOCK:])
            return carry

        lax.fori_loop(0, ATT_UNITS, unit, 0, unroll=8)
        _interleave(o_ref, od, dilation)
        _interleave(l_ref, ld, dilation)

    blk = (ATT_CHUNK, LANES)
    bias_spec = pl.BlockSpec((2, ATT_BLOCK, 2 * ATT_BLOCK), lambda p, c: (p, 0, 0))
    out_spec = pl.BlockSpec(blk, lambda p, c: (c, p))
    return pl.pallas_call(
        body, name=name, grid=(pairs, nc),
        in_specs=[pl.BlockSpec(blk, lambda p, c: (c, p)),
                  pl.BlockSpec(blk, lambda p, c: (c, pairs + p)),
                  pl.BlockSpec(blk, lambda p, c: (c, 4 * pairs + p)),
                  bias_spec],
        out_specs=[out_spec, out_spec],
        out_shape=[_sds((t_len, ch), F32)] * 2,
        scratch_shapes=[pltpu.VMEM((ATT_CHUNK, LANES), F32),
                        pltpu.VMEM((2 * ATT_CHUNK, LANES), BF16), pltpu.VMEM((2 * ATT_CHUNK, LANES), BF16),
                        pltpu.VMEM((ATT_CHUNK, LANES), F32), pltpu.VMEM((ATT_CHUNK, LANES), F32)],
        compiler_params=_params("arbitrary", "arbitrary"),
    )(qk, qk, z, bias)


def attn_combine(outs, lses, name):
    t_len, ch = outs[0].shape
    tm = _row_tile(t_len, 512)

    def body(o1, o2, o3, l1, l2, l3, out_ref, outb_ref, lg_ref):
        a, b, c = l1[...], l2[...], l3[...]
        mx = jnp.maximum(jnp.maximum(a, b), c)
        tot = mx + jnp.log(jnp.exp(a - mx) + jnp.exp(b - mx) + jnp.exp(c - mx))
        val = jnp.exp(a - tot) * o1[...] + jnp.exp(b - tot) * o2[...] + jnp.exp(c - tot) * o3[...]
        out_ref[...] = val
        outb_ref[...] = val.astype(BF16)
        lg_ref[...] = tot

    row = pl.BlockSpec((tm, ch), lambda i: (i, 0))
    return pl.pallas_call(
        body, name=name, grid=(t_len // tm,),
        in_specs=[row] * 6, out_specs=[row] * 3,
        out_shape=[_sds((t_len, ch), F32), _sds((t_len, ch), BF16), _sds((t_len, ch), F32)],
        compiler_params=_params("parallel"),
    )(*outs, *lses)


def attn_bwd(qk, z, dy_cat, out, lg, dilation, ch, name):
    t_len = qk.shape[0]
    pairs = ch // LANES
    nc = t_len // ATT_CHUNK
    nb = ATT_UNITS // dilation
    scale = 1.0 / math.sqrt(HEAD_DIM)
    bias = _alibi_bias(2 * pairs, dilation)

    def body(q_ref, k_ref, v_ref, do_ref, out_ref, lg_ref, b_ref, dq_ref, dk_ref, dv_ref,
             qd, dod, lgd, dld, dl_nat, kx, vx, dkx, dvx, dqd):
        c = pl.program_id(1)
        slot = lax.rem(c, jnp.int32(2))
        base, pbase = slot * ATT_CHUNK, (1 - slot) * ATT_CHUNK

        @pl.when(c == 0)
        def _():
            for ref in (kx, vx, dkx, dvx):
                ref[...] = jnp.zeros_like(ref)

        @pl.when(c < nc)
        def _():
            low_all, _ = _head_masks(ATT_CHUNK)
            dl_nat[...] = _per_head_mean(do_ref[...] * out_ref[...], low_all) * float(HEAD_DIM)
            _deinterleave(qd, q_ref, dilation)
            _deinterleave(dod, do_ref, dilation)
            _deinterleave(lgd, lg_ref, dilation)
            _deinterleave(dld, dl_nat, dilation)
            _deinterleave(kx, k_ref, dilation, base, BF16)
            _deinterleave(vx, v_ref, dilation, base, BF16)
            low, high = _head_masks(ATT_BLOCK)

            def unit(u, carry):
                _, has_prev, urow, crow, prow = _unit_rows(u, c, nb, base, pbase)
                rows = pl.ds(urow, ATT_BLOCK)
                q2 = _stack_heads(qd[rows, :] * scale, low, high)
                do2 = _stack_heads(dod[rows, :], low, high)
                lse = _head_rows(lgd[rows, :], low, high)
                delta = _head_rows(dld[rows, :], low, high)
                k2 = jnp.concatenate([kx[pl.ds(prow, ATT_BLOCK), :], kx[pl.ds(crow, ATT_BLOCK), :]], axis=0)
                v2 = jnp.concatenate([vx[pl.ds(prow, ATT_BLOCK), :], vx[pl.ds(crow, ATT_BLOCK), :]], axis=0)
                prob = jnp.exp(_unit_scores(q2, k2, b_ref, has_prev) - lse)
                dp = lax.dot_general(do2, v2, _NT, preferred_element_type=F32)
                ds = (prob * (dp - delta)).astype(BF16)
                dq2 = lax.dot_general(ds, k2, _NN, preferred_element_type=F32)
                dk2 = lax.dot_general(ds, q2, _TN, preferred_element_type=F32)
                dv2 = lax.dot_general(prob.astype(BF16), do2, _TN, preferred_element_type=F32)
                dqd[rows, :] = scale * jnp.where(low, dq2[:ATT_BLOCK], dq2[ATT_BLOCK:])
                dkx[pl.ds(prow, ATT_BLOCK), :] += dk2[:ATT_BLOCK]
                dkx[pl.ds(crow, ATT_BLOCK), :] = dk2[ATT_BLOCK:]
                dvx[pl.ds(prow, ATT_BLOCK), :] += dv2[:ATT_BLOCK]
                dvx[pl.ds(crow, ATT_BLOCK), :] = dv2[ATT_BLOCK:]
                return carry

            lax.fori_loop(0, ATT_UNITS, unit, 0, unroll=8)
            _interleave(dq_ref, dqd, dilation)

        @pl.when(c > 0)
        def _():
            _interleave(dk_ref, dkx, dilation, pbase)
            _interleave(dv_ref, dvx, dilation, pbase)

    blk = (ATT_CHUNK, LANES)
    here = lambda c: jnp.minimum(c, nc - 1)
    spec = lambda off: pl.BlockSpec(blk, lambda p, c: (here(c), off + p))
    late = pl.BlockSpec(blk, lambda p, c: (jnp.maximum(c - 1, 0), p))
    bias_spec = pl.BlockSpec((2, ATT_BLOCK, 2 * ATT_BLOCK), lambda p, c: (p, 0, 0))
    f32_chunk = pltpu.VMEM((ATT_CHUNK, LANES), F32)
    return pl.pallas_call(
        body, name=name, grid=(pairs, nc + 1),
        in_specs=[spec(0), spec(pairs), spec(4 * pairs), spec(pairs), spec(0), spec(0), bias_spec],
        out_specs=[spec(0), late, late],
        out_shape=[_sds((t_len, ch), F32)] * 3,
        scratch_shapes=[f32_chunk] * 5
                       + [pltpu.VMEM((2 * ATT_CHUNK, LANES), BF16)] * 2
                       + [pltpu.VMEM((2 * ATT_CHUNK, LANES), F32)] * 2 + [f32_chunk],
        compiler_params=_params("arbitrary", "arbitrary"),
    )(qk, qk, z, dy_cat, out, lg, bias)


def sum3_bf16(a, b, c, name):
    t_len, ch = a.shape
    tm = _row_tile(t_len, 512)

    def body(a_ref, b_ref, c_ref, o_ref):
        o_ref[...] = (a_ref[...] + b_ref[...] + c_ref[...]).astype(BF16)

    row = pl.BlockSpec((tm, ch), lambda i: (i, 0))
    return pl.pallas_call(
        body, name=name, grid=(t_len // tm,), in_specs=[row] * 3, out_specs=row,
        out_shape=_sds((t_len, ch), BF16), compiler_params=_params("parallel"),
    )(a, b, c)


def _blk3(rows, cols):
    return pl.BlockSpec((None, rows, cols), lambda j, t: (j, 0, 0))


def ffn_up(h, wg, wu, name, riders):
    t_len, d = h.shape
    n_blk, _, fj = wg.shape

    def epilogue(accs, e_refs, o_refs, cols):
        gate, up = accs
        o_refs[0][:, cols] = gate.astype(BF16)
        o_refs[1][:, cols] = up.astype(BF16)
        o_refs[2][:, cols] = (gate * _sigmoid(gate) * up).astype(BF16)

    act = lambda tm: pl.BlockSpec((None, tm, fj), lambda j, t: (j, t, 0))
    return mm_cols(name, h, [wg, wu], [_blk3(d, fj)] * 2, False, [], lambda tm: [],
                   [_sds((n_blk, t_len, fj), BF16)] * 3, lambda tm: [act(tm)] * 3, epilogue, n_blk, riders,
                   tm_want=1024)


def ffn_gate(h, wg, name, riders):
    t_len, d = h.shape
    n_blk, _, fj = wg.shape

    def epilogue(accs, e_refs, o_refs, cols):
        o_refs[0][:, cols] = accs[0].astype(BF16)

    act = lambda tm: pl.BlockSpec((None, tm, fj), lambda j, t: (j, t, 0))
    return mm_cols(name, h, [wg], [_blk3(d, fj)], False, [], lambda tm: [],
                   [_sds((n_blk, t_len, fj), BF16)], lambda tm: [act(tm)], epilogue, n_blk, riders, tm_want=1024)


def ffn_up_after_gate(h, wu, gate, name, riders):
    t_len, d = h.shape
    n_blk, _, fj = wu.shape

    def epilogue(accs, e_refs, o_refs, cols):
        gv = e_refs[0][:, cols].astype(F32)
        o_refs[0][:, cols] = accs[0].astype(BF16)
        o_refs[1][:, cols] = (gv * _sigmoid(gv) * accs[0]).astype(BF16)

    act = lambda tm: pl.BlockSpec((None, tm, fj), lambda j, t: (j, t, 0))
    return mm_cols(name, h, [wu], [_blk3(d, fj)], False, [gate], lambda tm: [act(tm)],
                   [_sds((n_blk, t_len, fj), BF16)] * 2, lambda tm: [act(tm)] * 2, epilogue, n_blk, riders,
                   tm_want=1024)


def ffn_down(act, wd, res, name, riders):
    n_blk, t_len, fj = act.shape
    d = wd.shape[2]
    return mm_jsum(name, [act], [wd], False, res, 0.5, d, riders)


def ffn_bwd(ht, gate, up, act, wg, wu, wd, dyb, name):
    d, t_len = ht.shape
    n_blk, _, fj = act.shape

    def epilogue(accs, e_refs, o_refs, cols):
        d_act = 0.5 * accs[0]
        gv, uv = e_refs[0][:, cols].astype(F32), e_refs[1][:, cols].astype(F32)
        sg = _sigmoid(gv)
        o_refs[0][:, cols] = (d_act * uv * (sg * (1.0 + gv * (1.0 - sg)))).astype(BF16)
        o_refs[1][:, cols] = (d_act * gv * sg).astype(BF16)

    act_jt = lambda tm: pl.BlockSpec((None, tm, fj), lambda j, t: (j, t, 0))
    (d_gate, d_up), _ = mm_cols(name + "_dact", dyb, [wd], [_blk3(fj, d)], True, [gate, up],
                                lambda tm: [act_jt(tm)] * 2, [_sds((n_blk, t_len, fj), BF16)] * 2,
                                lambda tm: [act_jt(tm)] * 2, epilogue, n_blk, tm_want=1024)

    (d_wd,), _ = mm_tn(name + "_dwd", act, act_jt, [dyb],
                       lambda tt: [pl.BlockSpec((tt, d), lambda j, t: (t, 0))],
                       [_sds((n_blk, fj, d), BF16)], [_blk3(fj, d)], 0.5, t_len, n_blk)

    whole_t = pl.BlockSpec((None, t_len, fj), lambda j, m: (j, 0, 0))
    d_wg, (recv_wd,) = mm_xt(name + "_dwg", ht, d_gate, whole_t, fj, n_blk, [Rider("scatter", d_wd)])
    d_wu, (recv_wg,) = mm_xt(name + "_dwu", ht, d_up, whole_t, fj, n_blk, [Rider("scatter", d_wg)])

    dh, (recv_wu,) = mm_jsum(name + "_dh", [d_gate, d_up], [wg, wu], True, None, 1.0, d,
                             [Rider("scatter", d_wu)])
    return dh, recv_wg, recv_wu, recv_wd


def local_step(x, target, g1, wg1, wu1_s, wd1_s, gmix, win_s, conv_w, conv_b, ln_g, ln_b, gq, gk, wout_s, g3,
               wg2_s, wu2_s, wd2_s):
    t_len, d = x.shape
    ch = d // 2
    ij = win_s.shape[1]
    oj = wout_s.shape[0]
    n_blk = N_DEV

    h1, h1t = rms_fwd(x, g1, "rms1")
    (gate1,), (wu1, wout) = ffn_gate(h1, wg1, "ffn1_gate", [Rider("gather", wu1_s), Rider("gather", wout_s)])
    (up1, act1), (wd1, wg2) = ffn_up_after_gate(h1, wu1, gate1, "ffn1_up",
                                                [Rider("gather", wd1_s), Rider("gather", wg2_s)])
    x1, (win, wd2) = ffn_down(act1, wd1, x, "ffn1_down",
                              [Rider("gather_cols", win_s), Rider("gather", wd2_s)])

    h2, h2t = rms_fwd(x1, gmix, "rms_mix")

    def store_f32(accs, e_refs, o_refs, cols):
        o_refs[0][:, cols] = accs[0]

    d_in = n_blk * ij
    win_full = win.reshape(1, d, d_in)
    wide = d_in // W_IN_SPLIT
    (z,), (wu2,) = mm_cols(
        "w_in", h2, [win_full], [pl.BlockSpec((None, d, wide), lambda j, t: (0, 0, j))], False, [], lambda tm: [],
        [_sds((t_len, d_in), F32)],
        lambda tm: [pl.BlockSpec((tm, wide), lambda j, t: (t, j))], store_f32, W_IN_SPLIT,
        [Rider("gather", wu2_s)], tm_want=1024)

    conv_w32 = jnp.pad(conv_w, ((0, 32 - CONV_WIDTH), (0, 0)))
    y_conv, conv_pre = conv_fwd(z, conv_w32, conv_b, ln_g, ln_b, "conv_fwd")

    g2 = jnp.concatenate([jnp.tile(gq, (1, ch // HEAD_DIM)), jnp.tile(gk, (1, ch // HEAD_DIM))], axis=1)
    qk = qk_norm_fwd(z, g2, ch, "qk_norm")
    branch = [attn_fwd(qk, z, dil, ch, "attn_fwd_d%d" % dil) for dil in DILATIONS]
    att, att_b, lg = attn_combine([o for o, _ in branch], [l for _, l in branch], "attn_combine")

    y_cat = jnp.concatenate([y_conv, att_b], axis=1)
    wout_full = wout.reshape(1, n_blk * oj, d)
    x2, _ = mm_reduce(
        "w_out", [y_cat], lambda tm: [pl.BlockSpec((tm, n_blk * oj), lambda t, j: (t, 0))],
        [wout_full], [pl.BlockSpec((None, n_blk * oj, d), lambda t, j: (0, 0, 0))], False, x1, 1.0,
        t_len, d, 1)

    h3, h3t = rms_fwd(x2, g3, "rms3")
    (gate2, up2, act2), _ = ffn_up(h3, wg2, wu2, "ffn2_up", [])
    y, _ = ffn_down(act2, wd2, x2, "ffn2_down", [])

    loss_tile, dy, dyb = loss_head(y, target, "loss")

    dh3, recv_wg2, recv_wu2, recv_wd2 = ffn_bwd(h3t, gate2, up2, act2, wg2, wu2, wd2, dyb, "ffn2")
    dx2, dx2b, d_g3 = rms_bwd(x2, g3, dh3, dy, "rms3_bwd")

    (dy_cat,), _ = mm_cols("w_out_dy", dx2b, [wout_full], [_blk3(n_blk * oj, d)], True, [], lambda tm: [],
                           [_sds((t_len, n_blk * oj), F32)],
                           lambda tm: [pl.BlockSpec((tm, n_blk * oj), lambda j, t: (t, 0))], store_f32, 1,
                           tm_want=1024)
    (d_wout,), _ = mm_tn("w_out_dw", y_cat, lambda tt: pl.BlockSpec((tt, oj), lambda j, t: (t, j)),
                         [dx2b], lambda tt: [pl.BlockSpec((tt, d), lambda j, t: (t, 0))],
                         [_sds((n_blk, oj, d), BF16)], [_blk3(oj, d)], 1.0, t_len, n_blk)

    dc, d_lg, d_lb, d_cb = conv_bwd_norm(conv_pre, dy_cat, ln_g, ln_b, "conv_bwd_norm")
    dz_a, dz_g, d_cw8 = conv_bwd_taps(z, dc, conv_w32, "conv_bwd_taps")
    d_cw = jnp.sum(d_cw8, axis=1)[:CONV_WIDTH]

    grads = [attn_bwd(qk, z, dy_cat, att, lg, dil, ch, "attn_bwd_d%d" % dil) for dil in DILATIONS]
    gq_t = jnp.tile(gq, (1, LANES // HEAD_DIM))
    gk_t = jnp.tile(gk, (1, LANES // HEAD_DIM))
    dz_q, d_gq2 = qk_norm_bwd(z, gq_t, [g[0] for g in grads], 2, ch, "q_norm_bwd")
    dz_k, d_gk2 = qk_norm_bwd(z, gk_t, [g[1] for g in grads], 3, ch, "k_norm_bwd")
    d_gq = d_gq2[:, :HEAD_DIM] + d_gq2[:, HEAD_DIM:]
    d_gk = d_gk2[:, :HEAD_DIM] + d_gk2[:, HEAD_DIM:]
    dz_v = sum3_bf16(grads[0][2], grads[1][2], grads[2][2], "dv_sum")
    dzb = jnp.concatenate([dz_a, dz_g, dz_q, dz_k, dz_v], axis=1)

    d_win, (recv_wout,) = mm_xt("w_in_dw", h2t, dzb, pl.BlockSpec((t_len, ij), lambda j, m: (0, j)), ij, n_blk,
                                [Rider("scatter", d_wout)])
    tall = d // W_IN_SPLIT
    (dh2,), (recv_win,) = mm_cols(
        "w_in_dh", dzb, [win_full], [pl.BlockSpec((None, tall, d_in), lambda j, t: (0, j, 0))], True, [],
        lambda tm: [], [_sds((t_len, d), F32)],
        lambda tm: [pl.BlockSpec((tm, tall), lambda j, t: (t, j))], store_f32, W_IN_SPLIT,
        [Rider("scatter", d_win)], tm_want=1024)
    dx1, dx1b, d_gmix = rms_bwd(x1, gmix, dh2, dx2, "rms_mix_bwd")

    dh1, recv_wg1, recv_wu1, recv_wd1 = ffn_bwd(h1t, gate1, up1, act1, wg1, wu1, wd1, dx1b, "ffn1")
    grad_x, _, d_g1 = rms_bwd(x, g1, dh1, dx1, "rms1_bwd")

    big = dict(ffn1_w_gate=recv_wg1, ffn1_w_up=recv_wu1, ffn1_w_down=recv_wd1, w_in=recv_win, w_out=recv_wout,
               ffn2_w_gate=recv_wg2, ffn2_w_up=recv_wu2, ffn2_w_down=recv_wd2)
    small = dict(g1=d_g1, gmix=d_gmix, g3=d_g3, conv_b=d_cb, ln_g=d_lg, ln_b=d_lb, gq=d_gq, gk=d_gk, conv_w=d_cw)
    return loss_tile[0, 0], grad_x, big, small


SMALL_ROWS = 48


def _pack_small(ch, g1, gmix, g3, conv_b, ln_g, ln_b, gq, gk, conv_w):
    pad_head = lambda v: jnp.pad(v, ((0, 0), (0, ch - v.shape[1])))
    rows = [g1.reshape(2, ch), gmix.reshape(2, ch), g3.reshape(2, ch), conv_b, ln_g, ln_b,
            pad_head(gq), pad_head(gk), conv_w]
    packed = jnp.concatenate(rows, axis=0)
    return jnp.pad(packed, ((0, SMALL_ROWS - packed.shape[0]), (0, 0)))


def _unpack_small(packed, d):
    return dict(g1=packed[0:2].reshape(1, d), gmix=packed[2:4].reshape(1, d), g3=packed[4:6].reshape(1, d),
                conv_b=packed[6:7], ln_g=packed[7:8], ln_b=packed[8:9],
                gq=packed[9:10, :HEAD_DIM], gk=packed[10:11, :HEAD_DIM])


def kernel(x, ffn1_norm_g, ffn1_w_gate, ffn1_w_up, ffn1_w_down, mix_norm_g, w_in, conv_w_dw, conv_b_dw, conv_ln_g, conv_ln_b, q_norm_g, k_norm_g, w_out, ffn2_norm_g, ffn2_w_gate, ffn2_w_up, ffn2_w_down, loss_target, m_ffn1_norm_g, m_ffn1_w_gate, m_ffn1_w_up, m_ffn1_w_down, m_mix_norm_g, m_w_in, m_conv_w_dw, m_conv_b_dw, m_conv_ln_g, m_conv_ln_b, m_q_norm_g, m_k_norm_g, m_w_out, m_ffn2_norm_g, m_ffn2_w_gate, m_ffn2_w_up, m_ffn2_w_down, v_ffn1_norm_g, v_ffn1_w_gate, v_ffn1_w_up, v_ffn1_w_down, v_mix_norm_g, v_w_in, v_conv_w_dw, v_conv_b_dw, v_conv_ln_g, v_conv_ln_b, v_q_norm_g, v_k_norm_g, v_w_out, v_ffn2_norm_g, v_ffn2_w_gate, v_ffn2_w_up, v_ffn2_w_down):
    d = x.shape[-1]
    ch = d // 2
    me = 4 * lax.axis_index("x") + 2 * lax.axis_index("y") + lax.axis_index("c")

    shard = lambda w: w[0].astype(BF16)
    wg1 = all_gather(shard(ffn1_w_gate), "ag_wg1")
    cw_all = all_gather(conv_w_dw[0], "ag_convw")
    conv_w = jnp.transpose(cw_all, (1, 0, 2)).reshape(CONV_WIDTH, ch)

    loss_part, grad_x, big, small = local_step(
        x[0], loss_target[0], ffn1_norm_g, wg1, shard(ffn1_w_up), shard(ffn1_w_down), mix_norm_g, shard(w_in),
        conv_w, conv_b_dw, conv_ln_g, conv_ln_b, q_norm_g, k_norm_g, shard(w_out), ffn2_norm_g,
        shard(ffn2_w_gate), shard(ffn2_w_up), shard(ffn2_w_down))
    loss = lax.psum(loss_part, MESH_AXES)

    state = dict(
        ffn1_w_gate=(ffn1_w_gate, m_ffn1_w_gate, v_ffn1_w_gate), ffn1_w_up=(ffn1_w_up, m_ffn1_w_up, v_ffn1_w_up),
        ffn1_w_down=(ffn1_w_down, m_ffn1_w_down, v_ffn1_w_down), w_in=(w_in, m_w_in, v_w_in),
        w_out=(w_out, m_w_out, v_w_out),
        ffn2_w_gate=(ffn2_w_gate, m_ffn2_w_gate, v_ffn2_w_gate), ffn2_w_up=(ffn2_w_up, m_ffn2_w_up, v_ffn2_w_up),
        ffn2_w_down=(ffn2_w_down, m_ffn2_w_down, v_ffn2_w_down))
    out = {}
    for pname, (w, m, v) in state.items():
        out[pname] = [r[None] for r in adamw(w[0], m[0], v[0], big[pname], "adamw_" + pname)]

    zero_taps = jnp.zeros((CONV_WIDTH, ch), F32)
    pack = lambda g1, gm, g3, cb, lg, lb, gq, gk: _pack_small(ch, g1, gm, g3, cb, lg, lb, gq, gk, zero_taps)
    small_parts = all_gather(_pack_small(ch, **small), "ag_small_grads")
    s_res = adamw(
        pack(ffn1_norm_g, mix_norm_g, ffn2_norm_g, conv_b_dw, conv_ln_g, conv_ln_b, q_norm_g, k_norm_g),
        pack(m_ffn1_norm_g, m_mix_norm_g, m_ffn2_norm_g, m_conv_b_dw, m_conv_ln_g, m_conv_ln_b, m_q_norm_g, m_k_norm_g),
        pack(v_ffn1_norm_g, v_mix_norm_g, v_ffn2_norm_g, v_conv_b_dw, v_conv_ln_g, v_conv_ln_b, v_q_norm_g, v_k_norm_g),
        small_parts, "adamw_small")
    s_out = [_unpack_small(r, d) for r in s_res]
    names = dict(g1="ffn1_norm_g", gmix="mix_norm_g", g3="ffn2_norm_g", conv_b="conv_b_dw", ln_g="conv_ln_g",
                 ln_b="conv_ln_b", gq="q_norm_g", gk="k_norm_g")
    for key, full in names.items():
        out[full] = [r[key] for r in s_out]

    cshard = ch // N_DEV
    taps_sum = s_res[0][11:11 + CONV_WIDTH]
    taps_mine = lax.dynamic_slice(taps_sum, (0, me * cshard), (CONV_WIDTH, cshard))
    pad_taps = lambda a: jnp.pad(a, ((0, 32 - CONV_WIDTH), (0, 0)))
    c_res = adamw(pad_taps(conv_w_dw[0]), pad_taps(m_conv_w_dw[0]), pad_taps(v_conv_w_dw[0]),
                  pad_taps(taps_mine)[None], "adamw_convw")
    out["conv_w_dw"] = [r[:CONV_WIDTH][None] for r in c_res]

    order = ["ffn1_norm_g", "ffn1_w_gate", "ffn1_w_up", "ffn1_w_down", "mix_norm_g", "w_in", "conv_w_dw",
             "conv_b_dw", "conv_ln_g", "conv_ln_b", "q_norm_g", "k_norm_g", "w_out", "ffn2_norm_g",
             "ffn2_w_gate", "ffn2_w_up", "ffn2_w_down"]
    result = [loss, grad_x[None]]
    for kind in range(4):
        result += [out[n][kind] for n in order]
    return tuple(result)
```

```python
import math
from typing import NamedTuple

import jax
import jax.numpy as jnp
from jax import lax
from jax.experimental import pallas as pl
from jax.experimental.pallas import tpu as pltpu

F32 = jnp.float32
BF16 = jnp.bfloat16

N_DEV = 8
EPS = 1e-6
HEAD_DIM = 64
LANES = 128
MXU_WIDTH = 256
W_IN_SPLIT = 4
CONV_WIDTH = 31
SUBLANES = 8
HALO = 32
ROW_CHUNK = 32
ATT_BLOCK = 128
DILATIONS = (1, 4, 16)
ATT_UNITS = 16
ATT_CHUNK = ATT_UNITS * ATT_BLOCK
ALIBI_MAX_BIAS = 8.0
MASKED = -1e30
VMEM_LIMIT = 56 * 1024 * 1024

ADAM_LR = 0.001
ADAM_B1 = 0.9
ADAM_B2 = 0.999
ADAM_EPS = 1e-08
ADAM_WD = 0.01
ADAM_STEP = 10

MESH_AXES = ("x", "y", "c")
ANY = pl.BlockSpec(memory_space=pl.ANY)


def _sds(shape, dtype):
    return jax.ShapeDtypeStruct(tuple(shape), dtype)


def _params(*sem):
    return pltpu.CompilerParams(dimension_semantics=sem, vmem_limit_bytes=VMEM_LIMIT)


def _sigmoid(v):
    return 1.0 / (1.0 + jnp.exp(-v))


def _row_tile(t, want):
    for cand in range(min(want, t) // 8 * 8, 0, -8):
        if t % cand == 0:
            return cand
    return t


def _mesh_pos():
    return lax.axis_index("x"), lax.axis_index("y"), lax.axis_index("c")


def _comm_sems():
    return [pltpu.SemaphoreType.DMA((7,)), pltpu.SemaphoreType.DMA((7,)), pltpu.SemaphoreType.DMA(())]


def _gather_phases(x_ref, out_ref, send_sems, recv_sems, local_sem, by_cols=False):
    x, y, c = _mesh_pos()
    me, sibling = (x, y, c), (x, y, 1 - c)
    chips = [(1 - x, y), (x, 1 - y), (1 - x, 1 - y)]

    def slot(px, py, pc):
        idx = 4 * px + 2 * py + pc
        if not by_cols:
            return out_ref.at[idx]
        width = x_ref.shape[1]
        return out_ref.at[:, pl.ds(pl.multiple_of(idx * width, LANES), width)]

    def copy(k, block, to, src=None):
        return pltpu.make_async_remote_copy(
            src_ref=slot(*block) if src is None else src, dst_ref=slot(*block),
            send_sem=send_sems.at[k], recv_sem=recv_sems.at[k],
            device_id=to, device_id_type=pl.DeviceIdType.MESH)

    mine = pltpu.make_async_copy(x_ref, slot(*me), local_sem)
    first = [copy(0, me, sibling, src=x_ref)]
    first += [copy(1 + j, me, (*chip, c), src=x_ref) for j, chip in enumerate(chips)]
    passed = [copy(4 + j, (*chip, c), sibling) for j, chip in enumerate(chips)]

    def start():
        mine.start()
        for cp in first:
            cp.start()

    def forward():
        for j, chip in enumerate(chips):
            copy(1 + j, (*chip, c), me).wait_recv()
            passed[j].start()

    def finish():
        copy(0, sibling, me).wait_recv()
        for j, chip in enumerate(chips):
            copy(4 + j, (*chip, 1 - c), me).wait_recv()
        for cp in first + passed:
            cp.wait_send()
        mine.wait()

    return start, forward, finish


def _scatter_phases(p_ref, out_ref, send_sems, recv_sems, local_sem):
    x, y, c = _mesh_pos()
    me = 4 * x + 2 * y + c
    flips = [(fx, fy, fc) for fx in (0, 1) for fy in (0, 1) for fc in (0, 1)][1:]

    def copy(k, flip, receiving):
        px, py, pc = (1 - x if flip[0] else x, 1 - y if flip[1] else y, 1 - c if flip[2] else c)
        them = 4 * px + 2 * py + pc
        return pltpu.make_async_remote_copy(
            src_ref=p_ref.at[them], dst_ref=out_ref.at[them if receiving else me],
            send_sem=send_sems.at[k], recv_sem=recv_sems.at[k],
            device_id=(px, py, pc), device_id_type=pl.DeviceIdType.MESH)

    mine = pltpu.make_async_copy(p_ref.at[me], out_ref.at[me], local_sem)

    def start():
        mine.start()
        for k, flip in enumerate(flips):
            copy(k, flip, False).start()

    def finish():
        for k, flip in enumerate(flips):
            copy(k, flip, True).wait_recv()
            copy(k, flip, False).wait_send()
        mine.wait()

    return start, None, finish


class Rider(NamedTuple):
    kind: str
    src: jax.Array

    def out_shape(self):
        if self.kind == "gather":
            shape = (N_DEV,) + self.src.shape
        elif self.kind == "gather_cols":
            shape = (self.src.shape[0], N_DEV * self.src.shape[1])
        else:
            shape = self.src.shape
        return _sds(shape, self.src.dtype)

    def phases(self, in_ref, out_ref, sems):
        if self.kind == "scatter":
            return _scatter_phases(in_ref, out_ref, *sems)
        return _gather_phases(in_ref, out_ref, *sems, by_cols=self.kind == "gather_cols")


def _rider_hooks(riders, in_refs, out_refs, sem_refs, step, n_steps):
    phases = [r.phases(in_refs[i], out_refs[i], sem_refs[3 * i:3 * i + 3]) for i, r in enumerate(riders)]

    def begin():
        for start, forward, _ in phases:
            pl.when(step == 0)(start)
            if forward is not None:
                pl.when(step == (7 * n_steps) // 8)(forward)

    def end():
        for _, _, finish in phases:
            pl.when(step == n_steps - 1)(finish)

    return begin, end


def _split_refs(refs, n_in, n_out, n_scratch, n_riders):
    pos, parts = 0, []
    for n in (n_in, n_riders, n_out, n_riders, n_scratch, 3 * n_riders):
        parts.append(refs[pos:pos + n])
        pos += n
    return parts


def all_gather(shard, name):
    def body(x_ref, out_ref, send_sems, recv_sems, local_sem):
        start, forward, finish = _gather_phases(x_ref, out_ref, send_sems, recv_sems, local_sem)
        start()
        forward()
        finish()

    return pl.pallas_call(
        body, name=name, out_shape=_sds((N_DEV,) + shard.shape, shard.dtype),
        in_specs=[ANY], out_specs=ANY, scratch_shapes=_comm_sems(),
    )(shard)


def adamw(w, m, v, parts, name):
    n_parts, rows, cols = parts.shape
    tr = _row_tile(rows, 128)
    c1 = 1.0 - ADAM_B1 ** ADAM_STEP
    c2 = 1.0 - ADAM_B2 ** ADAM_STEP

    def body(w_ref, m_ref, v_ref, p_ref, g_ref, d_ref, nm_ref, nv_ref):
        g = p_ref[0].astype(F32)
        for s in range(1, n_parts):
            g = g + p_ref[s].astype(F32)
        nm = ADAM_B1 * m_ref[...] + (1.0 - ADAM_B1) * g
        nv = ADAM_B2 * v_ref[...] + (1.0 - ADAM_B2) * (g * g)
        delta = -ADAM_LR * ((nm / c1) / (jnp.sqrt(nv / c2) + ADAM_EPS) + ADAM_WD * w_ref[...])
        g_ref[...] = g
        d_ref[...] = delta
        nm_ref[...] = nm
        nv_ref[...] = nv

    mat = pl.BlockSpec((tr, cols), lambda i: (i, 0))
    return pl.pallas_call(
        body, name=name, grid=(rows // tr,),
        in_specs=[mat, mat, mat, pl.BlockSpec((n_parts, tr, cols), lambda i: (0, i, 0))],
        out_specs=[mat, mat, mat, mat],
        out_shape=[_sds((rows, cols), F32)] * 4,
        compiler_params=_params("parallel"),
    )(w, m, v, parts)


_NN = (((1,), (0,)), ((), ()))
_NT = (((1,), (1,)), ((), ()))
_TN = (((0,), (0,)), ((), ()))


def mm_cols(name, a, b_list, b_specs, nt, extras, extra_specs, out_shapes, out_specs, epilogue, n_blk, riders=(),
            tm_want=512):
    t_len, k_len = a.shape
    tm = _row_tile(t_len, tm_want)
    nb, ne, n_out, nr = len(b_list), len(extras), len(out_shapes), len(riders)
    t_steps = t_len // tm
    n_cols = b_specs[0].block_shape[-2 if nt else -1]

    def body(*refs):
        ins, r_in, outs, r_out, _, r_sem = _split_refs(refs, 1 + nb + ne, n_out, 0, nr)
        step = pl.program_id(0) * t_steps + pl.program_id(1)
        begin, end = _rider_hooks(riders, r_in, r_out, r_sem, step, n_blk * t_steps)
        begin()
        av = ins[0][...]
        for c0 in range(0, n_cols, MXU_WIDTH):
            cols = slice(c0, min(c0 + MXU_WIDTH, n_cols))
            accs = [lax.dot_general(av, br[cols, :] if nt else br[:, cols], _NT if nt else _NN,
                                    preferred_element_type=F32) for br in ins[1:1 + nb]]
            epilogue(accs, ins[1 + nb:], outs, cols)
        end()

    res = pl.pallas_call(
        body, name=name, grid=(n_blk, t_steps),
        in_specs=([pl.BlockSpec((tm, k_len), lambda j, t: (t, 0))] + list(b_specs) + list(extra_specs(tm))
                  + [ANY] * nr),
        out_specs=list(out_specs(tm)) + [ANY] * nr,
        out_shape=list(out_shapes) + [r.out_shape() for r in riders],
        scratch_shapes=_comm_sems() * nr,
        compiler_params=_params("arbitrary", "arbitrary"),
    )(a, *b_list, *extras, *[r.src for r in riders])
    return res[:n_out], res[n_out:]


def mm_reduce(name, a_list, a_specs, b_list, b_specs, nt, res, scale, t_len, n_len, n_blk, riders=(), tm_want=512):
    tm = _row_tile(t_len, tm_want)
    na, nr = len(a_list), len(riders)
    has_res = res is not None
    t_steps = t_len // tm

    def body(*refs):
        ins, r_in, outs, r_out, _, r_sem = _split_refs(refs, 2 * na + has_res, 1, 0, nr)
        o_ref = outs[0]
        j = pl.program_id(1)
        step = pl.program_id(0) * n_blk + j
        begin, end = _rider_hooks(riders, r_in, r_out, r_sem, step, t_steps * n_blk)
        begin()

        part = None
        for ar, br in zip(ins[:na], ins[na:2 * na]):
            d = lax.dot_general(ar[...], br[...], _NT if nt else _NN, preferred_element_type=F32)
            part = d if part is None else part + d

        @pl.when(j == 0)
        def _():
            o_ref[...] = part

        @pl.when(j > 0)
        def _():
            o_ref[...] += part

        if has_res or scale != 1.0:
            @pl.when(j == n_blk - 1)
            def _():
                val = o_ref[...] * scale if scale != 1.0 else o_ref[...]
                o_ref[...] = ins[2 * na][...] + val if has_res else val

        end()

    row = pl.BlockSpec((tm, n_len), lambda t, j: (t, 0))
    out = pl.pallas_call(
        body, name=name, grid=(t_steps, n_blk),
        in_specs=list(a_specs(tm)) + list(b_specs) + ([row] if has_res else []) + [ANY] * nr,
        out_specs=[row] + [ANY] * nr,
        out_shape=[_sds((t_len, n_len), F32)] + [r.out_shape() for r in riders],
        scratch_shapes=_comm_sems() * nr,
        compiler_params=_params("arbitrary", "arbitrary"),
    )(*a_list, *b_list, *([res] if has_res else []), *[r.src for r in riders])
    return out[0], out[1:]


def mm_jsum(name, a_list, b_list, nt, res, scale, n_len, riders=(), tm_want=512, tn=512):
    n_j, t_len, k_j = a_list[0].shape
    tm = _row_tile(t_len, tm_want)
    tn = min(tn, n_len)
    na, nr = len(a_list), len(riders)
    has_res = res is not None
    t_steps, n_steps = t_len // tm, n_len // tn

    def body(*refs):
        ins, r_in, outs, r_out, _, r_sem = _split_refs(refs, 2 * na + has_res, 1, 0, nr)
        o_ref = outs[0]
        step = pl.program_id(0) * t_steps + pl.program_id(1)
        begin, end = _rider_hooks(riders, r_in, r_out, r_sem, step, n_steps * t_steps)
        begin()
        for c0 in range(0, tn, MXU_WIDTH):
            cols = slice(c0, min(c0 + MXU_WIDTH, tn))
            acc = None
            for ar, br in zip(ins[:na], ins[na:2 * na]):
                for j in range(n_j):
                    bj = br[j, cols, :] if nt else br[j, :, cols]
                    d = lax.dot_general(ar[j], bj, _NT if nt else _NN, preferred_element_type=F32)
                    acc = d if acc is None else acc + d
            val = acc * scale if scale != 1.0 else acc
            o_ref[:, cols] = ins[2 * na][:, cols] + val if has_res else val
        end()

    a_spec = pl.BlockSpec((n_j, tm, k_j), lambda n, t: (0, t, 0))
    b_spec = (pl.BlockSpec((n_j, tn, k_j), lambda n, t: (0, n, 0)) if nt
              else pl.BlockSpec((n_j, k_j, tn), lambda n, t: (0, 0, n)))
    tile = pl.BlockSpec((tm, tn), lambda n, t: (t, n))
    out = pl.pallas_call(
        body, name=name, grid=(n_steps, t_steps),
        in_specs=[a_spec] * na + [b_spec] * na + ([tile] if has_res else []) + [ANY] * nr,
        out_specs=[tile] + [ANY] * nr,
        out_shape=[_sds((t_len, n_len), F32)] + [r.out_shape() for r in riders],
        scratch_shapes=_comm_sems() * nr,
        compiler_params=_params("arbitrary", "arbitrary"),
    )(*a_list, *b_list, *([res] if has_res else []), *[r.src for r in riders])
    return out[0], out[1:]


def mm_xt(name, xt, dy, dy_spec, n_cols, n_blk, riders=(), tm_want=512):
    m_len, t_len = xt.shape
    tm = _row_tile(m_len, tm_want)
    m_steps = m_len // tm
    nr = len(riders)

    def body(*refs):
        ins, r_in, outs, r_out, _, r_sem = _split_refs(refs, 2, 1, 0, nr)
        step = pl.program_id(0) * m_steps + pl.program_id(1)
        begin, end = _rider_hooks(riders, r_in, r_out, r_sem, step, n_blk * m_steps)
        begin()
        outs[0][...] = lax.dot_general(ins[0][...], ins[1][...], _NN, preferred_element_type=F32).astype(BF16)
        end()

    res = pl.pallas_call(
        body, name=name, grid=(n_blk, m_steps),
        in_specs=[pl.BlockSpec((tm, t_len), lambda j, m: (m, 0)), dy_spec] + [ANY] * nr,
        out_specs=[pl.BlockSpec((None, tm, n_cols), lambda j, m: (j, m, 0))] + [ANY] * nr,
        out_shape=[_sds((n_blk, m_len, n_cols), BF16)] + [r.out_shape() for r in riders],
        scratch_shapes=_comm_sems() * nr,
        compiler_params=_params("arbitrary", "arbitrary"),
    )(xt, dy, *[r.src for r in riders])
    return res[0], res[1:]


def mm_tn(name, x, x_spec, dy_list, dy_specs, out_shapes, out_specs, scale, t_len, n_blk, riders=()):
    tt = _row_tile(t_len, 2048)
    nd, nr = len(dy_list), len(riders)
    t_steps = t_len // tt
    acc_shapes = [pltpu.VMEM(spec.block_shape[-2:], F32) for spec in out_specs]

    def body(*refs):
        ins, r_in, outs, r_out, accs, r_sem = _split_refs(refs, 1 + nd, nd, nd, nr)
        t = pl.program_id(1)
        step = pl.program_id(0) * t_steps + t
        begin, end = _rider_hooks(riders, r_in, r_out, r_sem, step, n_blk * t_steps)
        begin()
        xv = ins[0][...]
        for dr, acc in zip(ins[1:], accs):
            d = lax.dot_general(xv, dr[...], _TN, preferred_element_type=F32)

            @pl.when(t == 0)
            def _():
                acc[...] = d

            @pl.when(t > 0)
            def _():
                acc[...] += d

        @pl.when(t == t_steps - 1)
        def _():
            for acc, orf in zip(accs, outs):
                val = acc[...] * scale if scale != 1.0 else acc[...]
                orf[...] = val.astype(orf.dtype)

        end()

    res = pl.pallas_call(
        body, name=name, grid=(n_blk, t_steps),
        in_specs=[x_spec(tt)] + list(dy_specs(tt)) + [ANY] * nr,
        out_specs=list(out_specs) + [ANY] * nr,
        out_shape=list(out_shapes) + [r.out_shape() for r in riders],
        scratch_shapes=acc_shapes + _comm_sems() * nr,
        compiler_params=_params("arbitrary", "arbitrary"),
    )(x, *dy_list, *[r.src for r in riders])
    return res[:nd], res[nd:]


def rms_fwd(x, g, name):
    t_len, d = x.shape
    tm = _row_tile(t_len, 512)

    def body(x_ref, g_ref, h_ref, ht_ref):
        xv = x_ref[...]
        r = lax.rsqrt(jnp.mean(xv * xv, axis=-1, keepdims=True) + EPS)
        hv = xv * r * g_ref[...]
        h_ref[...] = hv.astype(BF16)
        ht_ref[...] = hv.T.astype(BF16)

    row = pl.BlockSpec((tm, d), lambda i: (i, 0))
    return pl.pallas_call(
        body, name=name, grid=(t_len // tm,),
        in_specs=[row, pl.BlockSpec((1, d), lambda i: (0, 0))],
        out_specs=[row, pl.BlockSpec((d, tm), lambda i: (0, i))],
        out_shape=[_sds((t_len, d), BF16), _sds((d, t_len), BF16)],
        compiler_params=_params("parallel"),
    )(x, g)


def rms_bwd(x, g, dh, dres, name):
    t_len, d = x.shape
    tm = _row_tile(t_len, 512)

    def body(x_ref, g_ref, dh_ref, dr_ref, dx_ref, dxb_ref, dg_ref):
        i = pl.program_id(0)
        xv = x_ref[...]
        r = lax.rsqrt(jnp.mean(xv * xv, axis=-1, keepdims=True) + EPS)
        xh = xv * r
        dhv = dh_ref[...]

        @pl.when(i == 0)
        def _():
            dg_ref[...] = jnp.zeros_like(dg_ref)

        dg_ref[...] += jnp.sum(dhv * xh, axis=0, keepdims=True)
        dxh = dhv * g_ref[...]
        dx = dr_ref[...] + r * (dxh - xh * jnp.mean(dxh * xh, axis=-1, keepdims=True))
        dx_ref[...] = dx
        dxb_ref[...] = dx.astype(BF16)

    row = pl.BlockSpec((tm, d), lambda i: (i, 0))
    vec = pl.BlockSpec((1, d), lambda i: (0, 0))
    return pl.pallas_call(
        body, name=name, grid=(t_len // tm,),
        in_specs=[row, vec, row, row],
        out_specs=[row, row, vec],
        out_shape=[_sds((t_len, d), F32), _sds((t_len, d), BF16), _sds((1, d), F32)],
        compiler_params=_params("arbitrary"),
    )(x, g, dh, dres)


def loss_head(y, target, name):
    t_len, d = y.shape
    tm = _row_tile(t_len, 512)

    def body(y_ref, t_ref, l_ref, dy_ref, dyb_ref):
        i = pl.program_id(0)
        err = y_ref[...] - t_ref[...]

        @pl.when(i == 0)
        def _():
            l_ref[...] = jnp.zeros_like(l_ref)

        rows = jnp.sum(err * err, axis=-1, keepdims=True) * (1.0 / d)
        l_ref[...] += 0.5 * jnp.sum(rows, axis=0, keepdims=True)
        dy = err * (1.0 / d)
        dy_ref[...] = dy
        dyb_ref[...] = dy.astype(BF16)

    row = pl.BlockSpec((tm, d), lambda i: (i, 0))
    return pl.pallas_call(
        body, name=name, grid=(t_len // tm,),
        in_specs=[row, row],
        out_specs=[pl.BlockSpec((8, LANES), lambda i: (0, 0)), row, row],
        out_shape=[_sds((8, LANES), F32), _sds((t_len, d), F32), _sds((t_len, d), BF16)],
        compiler_params=_params("arbitrary"),
    )(y, target)


def _conv_specs(tm, ch):
    per = tm // HALO
    cur = lambda cb: pl.BlockSpec((tm, ch), lambda i: (i, cb))
    prev = lambda cb: pl.BlockSpec((HALO, ch), lambda i: (jnp.maximum(i * per - 1, 0), cb))
    return [cur(0), cur(1), prev(0), prev(1)]


def _tap_scratch(rows, ch):
    return pltpu.VMEM((SUBLANES, rows + SUBLANES, ch), F32)


def _shifted_copies(buf, rows):
    buf[0, rows:rows + SUBLANES, :] = jnp.zeros((SUBLANES, buf.shape[2]), F32)
    for s in range(1, SUBLANES):
        buf[s, 0:rows, :] = buf[0, pl.ds(s, rows), :]


def _tap_rows(buf, off):
    shift = off % SUBLANES
    return buf[shift, off - shift:off - shift + ROW_CHUNK, :]


def _fill_glu(ext, a_ref, gt_ref, ap_ref, gp_ref, i, tm):
    vp = ap_ref[...] * _sigmoid(gp_ref[...])
    ext[0, 0:HALO, :] = jnp.where(i > 0, vp, 0.0)
    ext[0, HALO:HALO + tm, :] = a_ref[...] * _sigmoid(gt_ref[...])
    _shifted_copies(ext, HALO + tm)


def _conv_rows(ext, w_ref, b_ref, r0):
    acc = jnp.broadcast_to(b_ref[...], (ROW_CHUNK, b_ref.shape[1]))
    for k in range(CONV_WIDTH):
        acc = acc + w_ref[k:k + 1, :] * _tap_rows(ext, r0 + HALO - (CONV_WIDTH - 1) + k)
    return acc


def _layer_norm(yv):
    mu = jnp.mean(yv, axis=-1, keepdims=True)
    cen = yv - mu
    var = jnp.mean(cen * cen, axis=-1, keepdims=True)
    rstd = lax.rsqrt(var + EPS)
    return cen * rstd, rstd


def conv_fwd(z, w, b, lg, lb, name):
    t_len = z.shape[0]
    ch = w.shape[1]
    tm = _row_tile(t_len, 256)

    def body(a_ref, gt_ref, ap_ref, gp_ref, w_ref, b_ref, lg_ref, lb_ref, y_ref, pre_ref, ext):
        i = pl.program_id(0)
        _fill_glu(ext, a_ref, gt_ref, ap_ref, gp_ref, i, tm)
        for r0 in range(0, tm, ROW_CHUNK):
            pre = _conv_rows(ext, w_ref, b_ref, r0)
            pre_ref[r0:r0 + ROW_CHUNK, :] = pre
            xh, _ = _layer_norm(pre)
            u = xh * lg_ref[...] + lb_ref[...]
            y_ref[r0:r0 + ROW_CHUNK, :] = (u * _sigmoid(u)).astype(BF16)

    vec = pl.BlockSpec((1, ch), lambda i: (0, 0))
    row = pl.BlockSpec((tm, ch), lambda i: (i, 0))
    return pl.pallas_call(
        body, name=name, grid=(t_len // tm,),
        in_specs=_conv_specs(tm, ch) + [pl.BlockSpec((32, ch), lambda i: (0, 0)), vec, vec, vec],
        out_specs=[row, row],
        out_shape=[_sds((t_len, ch), BF16), _sds((t_len, ch), F32)],
        scratch_shapes=[_tap_scratch(HALO + tm, ch)],
        compiler_params=_params("parallel"),
    )(z, z, z, z, w, b, lg, lb)


def conv_bwd_norm(pre, dy_cat, lg, lb, name):
    t_len, ch = pre.shape
    tm = _row_tile(t_len, 256)

    def body(pre_ref, dy_ref, lg_ref, lb_ref, dc_ref, dlg_ref, dlb_ref, db_ref):
        i = pl.program_id(0)

        @pl.when(i == 0)
        def _():
            dlg_ref[...] = jnp.zeros_like(dlg_ref)
            dlb_ref[...] = jnp.zeros_like(dlb_ref)
            db_ref[...] = jnp.zeros_like(db_ref)

        for r0 in range(0, tm, ROW_CHUNK):
            xh, rstd = _layer_norm(pre_ref[r0:r0 + ROW_CHUNK, :])
            u = xh * lg_ref[...] + lb_ref[...]
            sg = _sigmoid(u)
            du = dy_ref[r0:r0 + ROW_CHUNK, :] * (sg * (1.0 + u * (1.0 - sg)))
            dlg_ref[...] += jnp.sum(du * xh, axis=0, keepdims=True)
            dlb_ref[...] += jnp.sum(du, axis=0, keepdims=True)
            dxh = du * lg_ref[...]
            dc = rstd * (dxh - jnp.mean(dxh, axis=-1, keepdims=True)
                         - xh * jnp.mean(dxh * xh, axis=-1, keepdims=True))
            db_ref[...] += jnp.sum(dc, axis=0, keepdims=True)
            dc_ref[r0:r0 + ROW_CHUNK, :] = dc

    vec = pl.BlockSpec((1, ch), lambda i: (0, 0))
    row = pl.BlockSpec((tm, ch), lambda i: (i, 0))
    return pl.pallas_call(
        body, name=name, grid=(t_len // tm,),
        in_specs=[row, row, vec, vec],
        out_specs=[row, vec, vec, vec],
        out_shape=[_sds((t_len, ch), F32)] + [_sds((1, ch), F32)] * 3,
        compiler_params=_params("arbitrary"),
    )(pre, dy_cat, lg, lb)


def conv_bwd_taps(z, dc, w, name):
    t_len = z.shape[0]
    ch = w.shape[1]
    tm = _row_tile(t_len, 256)
    per = tm // HALO
    n_tiles = t_len // tm
    last_halo = t_len // HALO - 1

    def body(a_ref, gt_ref, ap_ref, gp_ref, dc_ref, dn_ref, w_ref, dz_a_ref, dz_g_ref, dw_ref, ext, dext):
        i = pl.program_id(0)
        _fill_glu(ext, a_ref, gt_ref, ap_ref, gp_ref, i, tm)
        dext[0, 0:tm, :] = dc_ref[...]
        dext[0, tm:tm + HALO, :] = jnp.where(i < n_tiles - 1, dn_ref[...], 0.0)
        _shifted_copies(dext, tm + HALO)

        @pl.when(i == 0)
        def _():
            dw_ref[...] = jnp.zeros_like(dw_ref)

        for r0 in range(0, tm, ROW_CHUNK):
            dcv = dext[0, r0:r0 + ROW_CHUNK, :]
            dv = jnp.zeros((ROW_CHUNK, ch), F32)
            for k in range(CONV_WIDTH):
                dv = dv + w_ref[k:k + 1, :] * _tap_rows(dext, r0 + (CONV_WIDTH - 1) - k)
                prod = dcv * _tap_rows(ext, r0 + HALO - (CONV_WIDTH - 1) + k)
                fold = prod[0:8]
                for s in range(8, ROW_CHUNK, 8):
                    fold = fold + prod[s:s + 8]
                dw_ref[k] += fold
            av = a_ref[r0:r0 + ROW_CHUNK, :]
            sg = _sigmoid(gt_ref[r0:r0 + ROW_CHUNK, :])
            dz_a_ref[r0:r0 + ROW_CHUNK, :] = (dv * sg).astype(BF16)
            dz_g_ref[r0:r0 + ROW_CHUNK, :] = (dv * av * sg * (1.0 - sg)).astype(BF16)

    row = pl.BlockSpec((tm, ch), lambda i: (i, 0))
    nxt = pl.BlockSpec((HALO, ch), lambda i: (jnp.minimum((i + 1) * per, last_halo), 0))
    return pl.pallas_call(
        body, name=name, grid=(n_tiles,),
        in_specs=_conv_specs(tm, ch) + [row, nxt, pl.BlockSpec((32, ch), lambda i: (0, 0))],
        out_specs=[row, row, pl.BlockSpec((32, 8, ch), lambda i: (0, 0, 0))],
        out_shape=[_sds((t_len, ch), BF16), _sds((t_len, ch), BF16), _sds((32, 8, ch), F32)],
        scratch_shapes=[_tap_scratch(HALO + tm, ch), _tap_scratch(tm + HALO, ch)],
        compiler_params=_params("arbitrary"),
    )(z, z, z, z, dc, dc, w)


def _head_masks(rows):
    lane = lax.broadcasted_iota(jnp.int32, (rows, LANES), 1)
    low = lane < HEAD_DIM
    return low, jnp.logical_not(low)


def _per_head_mean(val, low):
    s_low = jnp.sum(jnp.where(low, val, 0.0), axis=-1, keepdims=True)
    s_high = jnp.sum(jnp.where(low, 0.0, val), axis=-1, keepdims=True)
    return jnp.where(low, s_low, s_high) * (1.0 / HEAD_DIM)


def qk_norm_fwd(z, g2, ch, name):
    t_len = z.shape[0]
    tm = _row_tile(t_len, 512)

    def body(z_ref, g_ref, o_ref):
        low, _ = _head_masks(tm)
        for c0 in range(0, ch, LANES):
            cols = slice(c0, c0 + LANES)
            xv = z_ref[:, cols]
            r = lax.rsqrt(_per_head_mean(xv * xv, low) + EPS)
            o_ref[:, cols] = xv * r * g_ref[:, cols]

    return pl.pallas_call(
        body, name=name, grid=(t_len // tm, 2),
        in_specs=[pl.BlockSpec((tm, ch), lambda i, w: (i, 2 + w)),
                  pl.BlockSpec((1, ch), lambda i, w: (0, w))],
        out_specs=pl.BlockSpec((tm, ch), lambda i, w: (i, w)),
        out_shape=_sds((t_len, 2 * ch), F32),
        compiler_params=_params("parallel", "parallel"),
    )(z, g2)


def qk_norm_bwd(z, g, d_list, z_off, ch, name):
    t_len = z.shape[0]
    tm = _row_tile(t_len, 512)
    nd = len(d_list)

    def body(*refs):
        z_ref, g_ref, d_refs = refs[0], refs[1], refs[2:2 + nd]
        dz_ref, dg_ref = refs[2 + nd], refs[3 + nd]
        low, _ = _head_masks(tm)

        @pl.when(pl.program_id(0) == 0)
        def _():
            dg_ref[...] = jnp.zeros_like(dg_ref)

        for c0 in range(0, ch, LANES):
            cols = slice(c0, c0 + LANES)
            xv = z_ref[:, cols]
            r = lax.rsqrt(_per_head_mean(xv * xv, low) + EPS)
            xh = xv * r
            dy = d_refs[0][:, cols]
            for dr in d_refs[1:]:
                dy = dy + dr[:, cols]
            dg_ref[...] += jnp.sum(dy * xh, axis=0, keepdims=True)
            dxh = dy * g_ref[...]
            dz_ref[:, cols] = (r * (dxh - xh * _per_head_mean(dxh * xh, low))).astype(BF16)

    blk = pl.BlockSpec((tm, ch), lambda i: (i, 0))
    return pl.pallas_call(
        body, name=name, grid=(t_len // tm,),
        in_specs=[pl.BlockSpec((tm, ch), lambda i: (i, z_off)),
                  pl.BlockSpec((1, LANES), lambda i: (0, 0))] + [blk] * nd,
        out_specs=[blk, pl.BlockSpec((1, LANES), lambda i: (0, 0))],
        out_shape=[_sds((t_len, ch), BF16), _sds((1, LANES), F32)],
        compiler_params=_params("arbitrary"),
    )(z, g, *d_list)


def _alibi_bias(n_heads, dilation):
    slopes = 2.0 ** (-ALIBI_MAX_BIAS * jnp.arange(1, n_heads + 1, dtype=F32) / n_heads)
    qi = jnp.arange(ATT_BLOCK)[:, None]
    kj = jnp.arange(ATT_BLOCK)[None, :]
    dist_cur = (qi - kj).astype(F32)
    dist_prev = (ATT_BLOCK + qi - kj).astype(F32)
    cur = jnp.where((qi >= kj)[None], -slopes[:, None, None] * (dilation * dist_cur)[None], MASKED)
    prev = jnp.where((kj >= qi)[None], -slopes[:, None, None] * (dilation * dist_prev)[None], MASKED)
    return jnp.concatenate([prev, cur], axis=-1).astype(F32)


def _stack_heads(val, low, high):
    return jnp.concatenate([jnp.where(low, val, 0.0), jnp.where(high, val, 0.0)], axis=0).astype(BF16)


def _head_rows(val, low, high):
    other = pltpu.roll(val, HEAD_DIM, axis=1)
    rows = jnp.concatenate([jnp.where(low, val, other), jnp.where(high, val, other)], axis=0)
    return jnp.concatenate([rows, rows], axis=1)


def _unit_scores(q2, k2, b_ref, has_prev):
    s = lax.dot_general(q2, k2, _NT, preferred_element_type=F32)
    s = s + b_ref[...].reshape(2 * ATT_BLOCK, 2 * ATT_BLOCK)
    penalty = jnp.where(has_prev, 0.0, MASKED)
    return jnp.concatenate([s[:, :ATT_BLOCK] + penalty, s[:, ATT_BLOCK:]], axis=1)


def _strided_rows(r, dilation):
    per = ATT_CHUNK // dilation
    return pl.ds(r, per, stride=dilation) if dilation > 1 else pl.ds(0, per)


def _deinterleave(dst, src_ref, dilation, base=None, dtype=None):
    per = ATT_CHUNK // dilation
    for r in range(dilation):
        val = src_ref[_strided_rows(r, dilation), :]
        val = val if dtype is None else val.astype(dtype)
        if base is None:
            dst[r * per:(r + 1) * per, :] = val
        else:
            dst[pl.ds(pl.multiple_of(base + r * per, ATT_BLOCK), per), :] = val


def _unit_rows(u, c, nb, base, pbase):
    in_chunk = lax.rem(u, jnp.int32(nb)) > 0
    has_prev = jnp.logical_or(in_chunk, c > 0)
    urow = pl.multiple_of(u * ATT_BLOCK, ATT_BLOCK)
    crow = pl.multiple_of(base + u * ATT_BLOCK, ATT_BLOCK)
    prow = pl.multiple_of(jnp.where(in_chunk, base + (u - 1) * ATT_BLOCK,
                                    pbase + (u + nb - 1) * ATT_BLOCK), ATT_BLOCK)
    return in_chunk, has_prev, urow, crow, prow


def _interleave(dst_ref, src, dilation, base=None):
    per = ATT_CHUNK // dilation
    for r in range(dilation):
        if base is None:
            val = src[r * per:(r + 1) * per, :]
        else:
            val = src[pl.ds(pl.multiple_of(base + r * per, ATT_BLOCK), per), :]
        dst_ref[_strided_rows(r, dilation), :] = val


def attn_fwd(qk, z, dilation, ch, name):
    t_len = qk.shape[0]
    pairs = ch // LANES
    nc = t_len // ATT_CHUNK
    nb = ATT_UNITS // dilation
    scale = 1.0 / math.sqrt(HEAD_DIM)
    bias = _alibi_bias(2 * pairs, dilation)

    def body(q_ref, k_ref, v_ref, b_ref, o_ref, l_ref, qd, kx, vx, od, ld):
        c = pl.program_id(1)
        slot = lax.rem(c, jnp.int32(2))
        base, pbase = slot * ATT_CHUNK, (1 - slot) * ATT_CHUNK

        @pl.when(c == 0)
        def _():
            kx[...] = jnp.zeros_like(kx)
            vx[...] = jnp.zeros_like(vx)

        _deinterleave(qd, q_ref, dilation)
        _deinterleave(kx, k_ref, dilation, base, BF16)
        _deinterleave(vx, v_ref, dilation, base, BF16)
        low, high = _head_masks(ATT_BLOCK)

        def unit(u, carry):
            _, has_prev, urow, crow, prow = _unit_rows(u, c, nb, base, pbase)
            q2 = _stack_heads(qd[pl.ds(urow, ATT_BLOCK), :] * scale, low, high)
            k2 = jnp.concatenate([kx[pl.ds(prow, ATT_BLOCK), :], kx[pl.ds(crow, ATT_BLOCK), :]], axis=0)
            v2 = jnp.concatenate([vx[pl.ds(prow, ATT_BLOCK), :], vx[pl.ds(crow, ATT_BLOCK), :]], axis=0)
            s = _unit_scores(q2, k2, b_ref, has_prev)
            mx = jnp.max(s, axis=-1, keepdims=True)
            e = jnp.exp(s - mx)
            den = jnp.sum(e, axis=-1, keepdims=True)
            acc = lax.dot_general(e.astype(BF16), v2, _NN, preferred_element_type=F32) / den
            lse = jnp.broadcast_to(mx + jnp.log(den), acc.shape)
            od[pl.ds(urow, ATT_BLOCK), :] = jnp.where(low, acc[:ATT_BLOCK], acc[ATT_BLOCK:])
            ld[pl.ds(urow, ATT_BLOCK), :] = jnp.where(low, lse[:ATT_BLOCK], lse[ATT_BLOCK:])
            return carry

        lax.fori_loop(0, ATT_UNITS, unit, 0, unroll=8)
        _interleave(o_ref, od, dilation)
        _interleave(l_ref, ld, dilation)

    blk = (ATT_CHUNK, LANES)
    bias_spec = pl.BlockSpec((2, ATT_BLOCK, 2 * ATT_BLOCK), lambda p, c: (p, 0, 0))
    out_spec = pl.BlockSpec(blk, lambda p, c: (c, p))
    return pl.pallas_call(
        body, name=name, grid=(pairs, nc),
        in_specs=[pl.BlockSpec(blk, lambda p, c: (c, p)),
                  pl.BlockSpec(blk, lambda p, c: (c, pairs + p)),
                  pl.BlockSpec(blk, lambda p, c: (c, 4 * pairs + p)),
                  bias_spec],
        out_specs=[out_spec, out_spec],
        out_shape=[_sds((t_len, ch), F32)] * 2,
        scratch_shapes=[pltpu.VMEM((ATT_CHUNK, LANES), F32),
                        pltpu.VMEM((2 * ATT_CHUNK, LANES), BF16), pltpu.VMEM((2 * ATT_CHUNK, LANES), BF16),
                        pltpu.VMEM((ATT_CHUNK, LANES), F32), pltpu.VMEM((ATT_CHUNK, LANES), F32)],
        compiler_params=_params("arbitrary", "arbitrary"),
    )(qk, qk, z, bias)


def attn_combine(outs, lses, y_conv, name):
    t_len, ch = outs[0].shape
    tm = _row_tile(t_len, 512)

    def body(o1, o2, o3, l1, l2, l3, yc_ref, out_ref, cat_ref, lg_ref):
        a, b, c = l1[...], l2[...], l3[...]
        mx = jnp.maximum(jnp.maximum(a, b), c)
        tot = mx + jnp.log(jnp.exp(a - mx) + jnp.exp(b - mx) + jnp.exp(c - mx))
        val = jnp.exp(a - tot) * o1[...] + jnp.exp(b - tot) * o2[...] + jnp.exp(c - tot) * o3[...]
        out_ref[...] = val
        cat_ref[:, :ch] = yc_ref[...]
        cat_ref[:, ch:] = val.astype(BF16)
        lg_ref[...] = tot

    row = pl.BlockSpec((tm, ch), lambda i: (i, 0))
    return pl.pallas_call(
        body, name=name, grid=(t_len // tm,),
        in_specs=[row] * 7, out_specs=[row, pl.BlockSpec((tm, 2 * ch), lambda i: (i, 0)), row],
        out_shape=[_sds((t_len, ch), F32), _sds((t_len, 2 * ch), BF16), _sds((t_len, ch), F32)],
        compiler_params=_params("parallel"),
    )(*outs, *lses, y_conv)


def attn_bwd(qk, z, dy_cat, out, lg, dilation, ch, name):
    t_len = qk.shape[0]
    pairs = ch // LANES
    nc = t_len // ATT_CHUNK
    nb = ATT_UNITS // dilation
    scale = 1.0 / math.sqrt(HEAD_DIM)
    bias = _alibi_bias(2 * pairs, dilation)

    def body(q_ref, k_ref, v_ref, do_ref, out_ref, lg_ref, b_ref, dq_ref, dk_ref, dv_ref,
             qd, dod, lgd, dld, dl_nat, kx, vx, dkx, dvx, dqd):
        c = pl.program_id(1)
        slot = lax.rem(c, jnp.int32(2))
        base, pbase = slot * ATT_CHUNK, (1 - slot) * ATT_CHUNK

        @pl.when(c == 0)
        def _():
            for ref in (kx, vx, dkx, dvx):
                ref[...] = jnp.zeros_like(ref)

        @pl.when(c < nc)
        def _():
            low_all, _ = _head_masks(ATT_CHUNK)
            dl_nat[...] = _per_head_mean(do_ref[...] * out_ref[...], low_all) * float(HEAD_DIM)
            _deinterleave(qd, q_ref, dilation)
            _deinterleave(dod, do_ref, dilation)
            _deinterleave(lgd, lg_ref, dilation)
            _deinterleave(dld, dl_nat, dilation)
            _deinterleave(kx, k_ref, dilation, base, BF16)
            _deinterleave(vx, v_ref, dilation, base, BF16)
            low, high = _head_masks(ATT_BLOCK)

            def unit(u, carry):
                _, has_prev, urow, crow, prow = _unit_rows(u, c, nb, base, pbase)
                rows = pl.ds(urow, ATT_BLOCK)
                q2 = _stack_heads(qd[rows, :] * scale, low, high)
                do2 = _stack_heads(dod[rows, :], low, high)
                lse = _head_rows(lgd[rows, :], low, high)
                delta = _head_rows(dld[rows, :], low, high)
                k2 = jnp.concatenate([kx[pl.ds(prow, ATT_BLOCK), :], kx[pl.ds(crow, ATT_BLOCK), :]], axis=0)
                v2 = jnp.concatenate([vx[pl.ds(prow, ATT_BLOCK), :], vx[pl.ds(crow, ATT_BLOCK), :]], axis=0)
                prob = jnp.exp(_unit_scores(q2, k2, b_ref, has_prev) - lse)
                dp = lax.dot_general(do2, v2, _NT, preferred_element_type=F32)
                ds = (prob * (dp - delta)).astype(BF16)
                dq2 = lax.dot_general(ds, k2, _NN, preferred_element_type=F32)
                dk2 = lax.dot_general(ds, q2, _TN, preferred_element_type=F32)
                dv2 = lax.dot_general(prob.astype(BF16), do2, _TN, preferred_element_type=F32)
                dqd[rows, :] = scale * jnp.where(low, dq2[:ATT_BLOCK], dq2[ATT_BLOCK:])
                dkx[pl.ds(prow, ATT_BLOCK), :] += dk2[:ATT_BLOCK]
                dkx[pl.ds(crow, ATT_BLOCK), :] = dk2[ATT_BLOCK:]
                dvx[pl.ds(prow, ATT_BLOCK), :] += dv2[:ATT_BLOCK]
                dvx[pl.ds(crow, ATT_BLOCK), :] = dv2[ATT_BLOCK:]
                return carry

            lax.fori_loop(0, ATT_UNITS, unit, 0, unroll=8)
            _interleave(dq_ref, dqd, dilation)

        @pl.when(c > 0)
        def _():
            _interleave(dk_ref, dkx, dilation, pbase)
            _interleave(dv_ref, dvx, dilation, pbase)

    blk = (ATT_CHUNK, LANES)
    here = lambda c: jnp.minimum(c, nc - 1)
    spec = lambda off: pl.BlockSpec(blk, lambda p, c: (here(c), off + p))
    late = pl.BlockSpec(blk, lambda p, c: (jnp.maximum(c - 1, 0), p))
    bias_spec = pl.BlockSpec((2, ATT_BLOCK, 2 * ATT_BLOCK), lambda p, c: (p, 0, 0))
    f32_chunk = pltpu.VMEM((ATT_CHUNK, LANES), F32)
    return pl.pallas_call(
        body, name=name, grid=(pairs, nc + 1),
        in_specs=[spec(0), spec(pairs), spec(4 * pairs), spec(pairs), spec(0), spec(0), bias_spec],
        out_specs=[spec(0), late, late],
        out_shape=[_sds((t_len, ch), F32)] * 3,
        scratch_shapes=[f32_chunk] * 5
                       + [pltpu.VMEM((2 * ATT_CHUNK, LANES), BF16)] * 2
                       + [pltpu.VMEM((2 * ATT_CHUNK, LANES), F32)] * 2 + [f32_chunk],
        compiler_params=_params("arbitrary", "arbitrary"),
    )(qk, qk, z, dy_cat, out, lg, bias)


def pack_dz(pieces, dvs, name):
    t_len, ch = pieces[0].shape
    tm = _row_tile(t_len, 512)
    n_p = len(pieces)

    def body(*refs):
        o_ref = refs[-1]
        for k in range(n_p):
            o_ref[:, k * ch:(k + 1) * ch] = refs[k][...]
        a_ref, b_ref, c_ref = refs[n_p:n_p + 3]
        o_ref[:, n_p * ch:(n_p + 1) * ch] = (a_ref[...] + b_ref[...] + c_ref[...]).astype(BF16)

    row = pl.BlockSpec((tm, ch), lambda i: (i, 0))
    return pl.pallas_call(
        body, name=name, grid=(t_len // tm,), in_specs=[row] * (n_p + 3),
        out_specs=pl.BlockSpec((tm, (n_p + 1) * ch), lambda i: (i, 0)),
        out_shape=_sds((t_len, (n_p + 1) * ch), BF16), compiler_params=_params("parallel"),
    )(*pieces, *dvs)


def _blk3(rows, cols):
    return pl.BlockSpec((None, rows, cols), lambda j, t: (j, 0, 0))


def ffn_up(h, wg, wu, name, riders):
    t_len, d = h.shape
    n_blk, _, fj = wg.shape

    def epilogue(accs, e_refs, o_refs, cols):
        gate, up = accs
        o_refs[0][:, cols] = gate.astype(BF16)
        o_refs[1][:, cols] = up.astype(BF16)
        o_refs[2][:, cols] = (gate * _sigmoid(gate) * up).astype(BF16)

    act = lambda tm: pl.BlockSpec((None, tm, fj), lambda j, t: (j, t, 0))
    return mm_cols(name, h, [wg, wu], [_blk3(d, fj)] * 2, False, [], lambda tm: [],
                   [_sds((n_blk, t_len, fj), BF16)] * 3, lambda tm: [act(tm)] * 3, epilogue, n_blk, riders,
                   tm_want=1024)


def ffn_gate(h, wg, name, riders):
    t_len, d = h.shape
    n_blk, _, fj = wg.shape

    def epilogue(accs, e_refs, o_refs, cols):
        o_refs[0][:, cols] = accs[0].astype(BF16)

    act = lambda tm: pl.BlockSpec((None, tm, fj), lambda j, t: (j, t, 0))
    return mm_cols(name, h, [wg], [_blk3(d, fj)], False, [], lambda tm: [],
                   [_sds((n_blk, t_len, fj), BF16)], lambda tm: [act(tm)], epilogue, n_blk, riders, tm_want=1024)


def ffn_up_after_gate(h, wu, gate, name, riders):
    t_len, d = h.shape
    n_blk, _, fj = wu.shape

    def epilogue(accs, e_refs, o_refs, cols):
        gv = e_refs[0][:, cols].astype(F32)
        o_refs[0][:, cols] = accs[0].astype(BF16)
        o_refs[1][:, cols] = (gv * _sigmoid(gv) * accs[0]).astype(BF16)

    act = lambda tm: pl.BlockSpec((None, tm, fj), lambda j, t: (j, t, 0))
    return mm_cols(name, h, [wu], [_blk3(d, fj)], False, [gate], lambda tm: [act(tm)],
                   [_sds((n_blk, t_len, fj), BF16)] * 2, lambda tm: [act(tm)] * 2, epilogue, n_blk, riders,
                   tm_want=1024)


def ffn_down(act, wd, res, name, riders):
    n_blk, t_len, fj = act.shape
    d = wd.shape[2]
    return mm_jsum(name, [act], [wd], False, res, 0.5, d, riders)


def ffn_bwd(ht, gate, up, act, wg, wu, wd, dyb, name):
    d, t_len = ht.shape
    n_blk, _, fj = act.shape

    def epilogue(accs, e_refs, o_refs, cols):
        d_act = 0.5 * accs[0]
        gv, uv = e_refs[0][:, cols].astype(F32), e_refs[1][:, cols].astype(F32)
        sg = _sigmoid(gv)
        o_refs[0][:, cols] = (d_act * uv * (sg * (1.0 + gv * (1.0 - sg)))).astype(BF16)
        o_refs[1][:, cols] = (d_act * gv * sg).astype(BF16)

    act_jt = lambda tm: pl.BlockSpec((None, tm, fj), lambda j, t: (j, t, 0))
    (d_gate, d_up), _ = mm_cols(name + "_dact", dyb, [wd], [_blk3(fj, d)], True, [gate, up],
                                lambda tm: [act_jt(tm)] * 2, [_sds((n_blk, t_len, fj), BF16)] * 2,
                                lambda tm: [act_jt(tm)] * 2, epilogue, n_blk, tm_want=1024)

    (d_wd,), _ = mm_tn(name + "_dwd", act, act_jt, [dyb],
                       lambda tt: [pl.BlockSpec((tt, d), lambda j, t: (t, 0))],
                       [_sds((n_blk, fj, d), BF16)], [_blk3(fj, d)], 0.5, t_len, n_blk)

    whole_t = pl.BlockSpec((None, t_len, fj), lambda j, m: (j, 0, 0))
    d_wg, (recv_wd,) = mm_xt(name + "_dwg", ht, d_gate, whole_t, fj, n_blk, [Rider("scatter", d_wd)])
    d_wu, (recv_wg,) = mm_xt(name + "_dwu", ht, d_up, whole_t, fj, n_blk, [Rider("scatter", d_wg)])

    dh, (recv_wu,) = mm_jsum(name + "_dh", [d_gate, d_up], [wg, wu], True, None, 1.0, d,
                             [Rider("scatter", d_wu)])
    return dh, recv_wg, recv_wu, recv_wd


def local_step(x, target, g1, wg1, wu1_s, wd1_s, gmix, win_s, conv_w, conv_b, ln_g, ln_b, gq, gk, wout_s, g3,
               wg2_s, wu2_s, wd2_s):
    t_len, d = x.shape
    ch = d // 2
    ij = win_s.shape[1]
    oj = wout_s.shape[0]
    n_blk = N_DEV

    h1, h1t = rms_fwd(x, g1, "rms1")
    (gate1,), (wu1, wout) = ffn_gate(h1, wg1, "ffn1_gate", [Rider("gather", wu1_s), Rider("gather", wout_s)])
    (up1, act1), (wd1, wg2) = ffn_up_after_gate(h1, wu1, gate1, "ffn1_up",
                                                [Rider("gather", wd1_s), Rider("gather", wg2_s)])
    x1, (win, wd2) = ffn_down(act1, wd1, x, "ffn1_down",
                              [Rider("gather_cols", win_s), Rider("gather", wd2_s)])

    h2, h2t = rms_fwd(x1, gmix, "rms_mix")

    def store_f32(accs, e_refs, o_refs, cols):
        o_refs[0][:, cols] = accs[0]

    d_in = n_blk * ij
    win_full = win.reshape(1, d, d_in)
    wide = d_in // W_IN_SPLIT
    (z,), (wu2,) = mm_cols(
        "w_in", h2, [win_full], [pl.BlockSpec((None, d, wide), lambda j, t: (0, 0, j))], False, [], lambda tm: [],
        [_sds((t_len, d_in), F32)],
        lambda tm: [pl.BlockSpec((tm, wide), lambda j, t: (t, j))], store_f32, W_IN_SPLIT,
        [Rider("gather", wu2_s)], tm_want=1024)

    conv_w32 = jnp.pad(conv_w, ((0, 32 - CONV_WIDTH), (0, 0)))
    y_conv, conv_pre = conv_fwd(z, conv_w32, conv_b, ln_g, ln_b, "conv_fwd")

    g2 = jnp.concatenate([jnp.tile(gq, (1, ch // HEAD_DIM)), jnp.tile(gk, (1, ch // HEAD_DIM))], axis=1)
    qk = qk_norm_fwd(z, g2, ch, "qk_norm")
    branch = [attn_fwd(qk, z, dil, ch, "attn_fwd_d%d" % dil) for dil in DILATIONS]
    att, y_cat, lg = attn_combine([o for o, _ in branch], [l for _, l in branch], y_conv, "attn_combine")

    wout_full = wout.reshape(1, n_blk * oj, d)
    x2, _ = mm_reduce(
        "w_out", [y_cat], lambda tm: [pl.BlockSpec((tm, n_blk * oj), lambda t, j: (t, 0))],
        [wout_full], [pl.BlockSpec((None, n_blk * oj, d), lambda t, j: (0, 0, 0))], False, x1, 1.0,
        t_len, d, 1)

    h3, h3t = rms_fwd(x2, g3, "rms3")
    (gate2, up2, act2), _ = ffn_up(h3, wg2, wu2, "ffn2_up", [])
    y, _ = ffn_down(act2, wd2, x2, "ffn2_down", [])

    loss_tile, dy, dyb = loss_head(y, target, "loss")

    dh3, recv_wg2, recv_wu2, recv_wd2 = ffn_bwd(h3t, gate2, up2, act2, wg2, wu2, wd2, dyb, "ffn2")
    dx2, dx2b, d_g3 = rms_bwd(x2, g3, dh3, dy, "rms3_bwd")

    (dy_cat,), _ = mm_cols("w_out_dy", dx2b, [wout_full], [_blk3(n_blk * oj, d)], True, [], lambda tm: [],
                           [_sds((t_len, n_blk * oj), F32)],
                           lambda tm: [pl.BlockSpec((tm, n_blk * oj), lambda j, t: (t, 0))], store_f32, 1,
                           tm_want=1024)
    (d_wout,), _ = mm_tn("w_out_dw", y_cat, lambda tt: pl.BlockSpec((tt, oj), lambda j, t: (t, j)),
                         [dx2b], lambda tt: [pl.BlockSpec((tt, d), lambda j, t: (t, 0))],
                         [_sds((n_blk, oj, d), BF16)], [_blk3(oj, d)], 1.0, t_len, n_blk)

    dc, d_lg, d_lb, d_cb = conv_bwd_norm(conv_pre, dy_cat, ln_g, ln_b, "conv_bwd_norm")
    dz_a, dz_g, d_cw8 = conv_bwd_taps(z, dc, conv_w32, "conv_bwd_taps")
    d_cw = jnp.sum(d_cw8, axis=1)[:CONV_WIDTH]

    grads = [attn_bwd(qk, z, dy_cat, att, lg, dil, ch, "attn_bwd_d%d" % dil) for dil in DILATIONS]
    gq_t = jnp.tile(gq, (1, LANES // HEAD_DIM))
    gk_t = jnp.tile(gk, (1, LANES // HEAD_DIM))
    dz_q, d_gq2 = qk_norm_bwd(z, gq_t, [g[0] for g in grads], 2, ch, "q_norm_bwd")
    dz_k, d_gk2 = qk_norm_bwd(z, gk_t, [g[1] for g in grads], 3, ch, "k_norm_bwd")
    d_gq = d_gq2[:, :HEAD_DIM] + d_gq2[:, HEAD_DIM:]
    d_gk = d_gk2[:, :HEAD_DIM] + d_gk2[:, HEAD_DIM:]
    dzb = pack_dz([dz_a, dz_g, dz_q, dz_k], [g[2] for g in grads], "dz_pack")

    d_win, (recv_wout,) = mm_xt("w_in_dw", h2t, dzb, pl.BlockSpec((t_len, ij), lambda j, m: (0, j)), ij, n_blk,
                                [Rider("scatter", d_wout)])
    tall = d // W_IN_SPLIT
    (dh2,), (recv_win,) = mm_cols(
        "w_in_dh", dzb, [win_full], [pl.BlockSpec((None, tall, d_in), lambda j, t: (0, j, 0))], True, [],
        lambda tm: [], [_sds((t_len, d), F32)],
        lambda tm: [pl.BlockSpec((tm, tall), lambda j, t: (t, j))], store_f32, W_IN_SPLIT,
        [Rider("scatter", d_win)], tm_want=1024)
    dx1, dx1b, d_gmix = rms_bwd(x1, gmix, dh2, dx2, "rms_mix_bwd")

    dh1, recv_wg1, recv_wu1, recv_wd1 = ffn_bwd(h1t, gate1, up1, act1, wg1, wu1, wd1, dx1b, "ffn1")
    grad_x, _, d_g1 = rms_bwd(x, g1, dh1, dx1, "rms1_bwd")

    big = dict(ffn1_w_gate=recv_wg1, ffn1_w_up=recv_wu1, ffn1_w_down=recv_wd1, w_in=recv_win, w_out=recv_wout,
               ffn2_w_gate=recv_wg2, ffn2_w_up=recv_wu2, ffn2_w_down=recv_wd2)
    small = dict(g1=d_g1, gmix=d_gmix, g3=d_g3, conv_b=d_cb, ln_g=d_lg, ln_b=d_lb, gq=d_gq, gk=d_gk, conv_w=d_cw)
    return loss_tile[0, 0], grad_x, big, small


SMALL_ROWS = 48


def _pack_small(ch, g1, gmix, g3, conv_b, ln_g, ln_b, gq, gk, conv_w):
    pad_head = lambda v: jnp.pad(v, ((0, 0), (0, ch - v.shape[1])))
    rows = [g1.reshape(2, ch), gmix.reshape(2, ch), g3.reshape(2, ch), conv_b, ln_g, ln_b,
            pad_head(gq), pad_head(gk), conv_w]
    packed = jnp.concatenate(rows, axis=0)
    return jnp.pad(packed, ((0, SMALL_ROWS - packed.shape[0]), (0, 0)))


def _unpack_small(packed, d):
    return dict(g1=packed[0:2].reshape(1, d), gmix=packed[2:4].reshape(1, d), g3=packed[4:6].reshape(1, d),
                conv_b=packed[6:7], ln_g=packed[7:8], ln_b=packed[8:9],
                gq=packed[9:10, :HEAD_DIM], gk=packed[10:11, :HEAD_DIM])


def kernel(x, ffn1_norm_g, ffn1_w_gate, ffn1_w_up, ffn1_w_down, mix_norm_g, w_in, conv_w_dw, conv_b_dw, conv_ln_g, conv_ln_b, q_norm_g, k_norm_g, w_out, ffn2_norm_g, ffn2_w_gate, ffn2_w_up, ffn2_w_down, loss_target, m_ffn1_norm_g, m_ffn1_w_gate, m_ffn1_w_up, m_ffn1_w_down, m_mix_norm_g, m_w_in, m_conv_w_dw, m_conv_b_dw, m_conv_ln_g, m_conv_ln_b, m_q_norm_g, m_k_norm_g, m_w_out, m_ffn2_norm_g, m_ffn2_w_gate, m_ffn2_w_up, m_ffn2_w_down, v_ffn1_norm_g, v_ffn1_w_gate, v_ffn1_w_up, v_ffn1_w_down, v_mix_norm_g, v_w_in, v_conv_w_dw, v_conv_b_dw, v_conv_ln_g, v_conv_ln_b, v_q_norm_g, v_k_norm_g, v_w_out, v_ffn2_norm_g, v_ffn2_w_gate, v_ffn2_w_up, v_ffn2_w_down):
    d = x.shape[-1]
    ch = d // 2
    me = 4 * lax.axis_index("x") + 2 * lax.axis_index("y") + lax.axis_index("c")

    shard = lambda w: w[0].astype(BF16)
    wg1 = all_gather(shard(ffn1_w_gate), "ag_wg1")
    cw_all = all_gather(conv_w_dw[0], "ag_convw")
    conv_w = jnp.transpose(cw_all, (1, 0, 2)).reshape(CONV_WIDTH, ch)

    loss_part, grad_x, big, small = local_step(
        x[0], loss_target[0], ffn1_norm_g, wg1, shard(ffn1_w_up), shard(ffn1_w_down), mix_norm_g, shard(w_in),
        conv_w, conv_b_dw, conv_ln_g, conv_ln_b, q_norm_g, k_norm_g, shard(w_out), ffn2_norm_g,
        shard(ffn2_w_gate), shard(ffn2_w_up), shard(ffn2_w_down))
    loss = lax.psum(loss_part, MESH_AXES)

    state = dict(
        ffn1_w_gate=(ffn1_w_gate, m_ffn1_w_gate, v_ffn1_w_gate), ffn1_w_up=(ffn1_w_up, m_ffn1_w_up, v_ffn1_w_up),
        ffn1_w_down=(ffn1_w_down, m_ffn1_w_down, v_ffn1_w_down), w_in=(w_in, m_w_in, v_w_in),
        w_out=(w_out, m_w_out, v_w_out),
        ffn2_w_gate=(ffn2_w_gate, m_ffn2_w_gate, v_ffn2_w_gate), ffn2_w_up=(ffn2_w_up, m_ffn2_w_up, v_ffn2_w_up),
        ffn2_w_down=(ffn2_w_down, m_ffn2_w_down, v_ffn2_w_down))
    out = {}
    for pname, (w, m, v) in state.items():
        out[pname] = [r[None] for r in adamw(w[0], m[0], v[0], big[pname], "adamw_" + pname)]

    zero_taps = jnp.zeros((CONV_WIDTH, ch), F32)
    pack = lambda g1, gm, g3, cb, lg, lb, gq, gk: _pack_small(ch, g1, gm, g3, cb, lg, lb, gq, gk, zero_taps)
    small_parts = all_gather(_pack_small(ch, **small), "ag_small_grads")
    s_res = adamw(
        pack(ffn1_norm_g, mix_norm_g, ffn2_norm_g, conv_b_dw, conv_ln_g, conv_ln_b, q_norm_g, k_norm_g),
        pack(m_ffn1_norm_g, m_mix_norm_g, m_ffn2_norm_g, m_conv_b_dw, m_conv_ln_g, m_conv_ln_b, m_q_norm_g, m_k_norm_g),
        pack(v_ffn1_norm_g, v_mix_norm_g, v_ffn2_norm_g, v_conv_b_dw, v_conv_ln_g, v_conv_ln_b, v_q_norm_g, v_k_norm_g),
        small_parts, "adamw_small")
    s_out = [_unpack_small(r, d) for r in s_res]
    names = dict(g1="ffn1_norm_g", gmix="mix_norm_g", g3="ffn2_norm_g", conv_b="conv_b_dw", ln_g="conv_ln_g",
                 ln_b="conv_ln_b", gq="q_norm_g", gk="k_norm_g")
    for key, full in names.items():
        out[full] = [r[key] for r in s_out]

    cshard = ch // N_DEV
    taps_sum = s_res[0][11:11 + CONV_WIDTH]
    taps_mine = lax.dynamic_slice(taps_sum, (0, me * cshard), (CONV_WIDTH, cshard))
    pad_taps = lambda a: jnp.pad(a, ((0, 32 - CONV_WIDTH), (0, 0)))
    c_res = adamw(pad_taps(conv_w_dw[0]), pad_taps(m_conv_w_dw[0]), pad_taps(v_conv_w_dw[0]),
                  pad_taps(taps_mine)[None], "adamw_convw")
    out["conv_w_dw"] = [r[:CONV_WIDTH][None] for r in c_res]

    order = ["ffn1_norm_g", "ffn1_w_gate", "ffn1_w_up", "ffn1_w_down", "mix_norm_g", "w_in", "conv_w_dw",
             "conv_b_dw", "conv_ln_g", "conv_ln_b", "q_norm_g", "k_norm_g", "w_out", "ffn2_norm_g",
             "ffn2_w_gate", "ffn2_w_up", "ffn2_w_down"]
    result = [loss, grad_x[None]]
    for kind in range(4):
        result += [out[n][kind] for n in order]
    return tuple(result)
```

```python
import math
from typing import NamedTuple

import jax
import jax.numpy as jnp
from jax import lax
from jax.experimental import pallas as pl
from jax.experimental.pallas import tpu as pltpu

F32 = jnp.float32
BF16 = jnp.bfloat16

N_DEV = 8
EPS = 1e-6
HEAD_DIM = 64
LANES = 128
MXU_WIDTH = 256
W_IN_SPLIT = 4
CONV_WIDTH = 31
SUBLANES = 8
HALO = 32
ROW_CHUNK = 32
ATT_BLOCK = 128
DILATIONS = (1, 4, 16)
ATT_UNITS = 16
ATT_CHUNK = ATT_UNITS * ATT_BLOCK
ALIBI_MAX_BIAS = 8.0
MASKED = -1e30
VMEM_LIMIT = 56 * 1024 * 1024

ADAM_LR = 0.001
ADAM_B1 = 0.9
ADAM_B2 = 0.999
ADAM_EPS = 1e-08
ADAM_WD = 0.01
ADAM_STEP = 10

MESH_AXES = ("x", "y", "c")
ANY = pl.BlockSpec(memory_space=pl.ANY)


def _sds(shape, dtype):
    return jax.ShapeDtypeStruct(tuple(shape), dtype)


def _params(*sem):
    return pltpu.CompilerParams(dimension_semantics=sem, vmem_limit_bytes=VMEM_LIMIT)


def _sigmoid(v):
    return 1.0 / (1.0 + jnp.exp(-v))


def _row_tile(t, want):
    for cand in range(min(want, t) // 8 * 8, 0, -8):
        if t % cand == 0:
            return cand
    return t


def _mesh_pos():
    return lax.axis_index("x"), lax.axis_index("y"), lax.axis_index("c")


def _comm_sems():
    return [pltpu.SemaphoreType.DMA((7,)), pltpu.SemaphoreType.DMA((7,)), pltpu.SemaphoreType.DMA(())]


def _gather_phases(x_ref, out_ref, send_sems, recv_sems, local_sem, by_cols=False):
    x, y, c = _mesh_pos()
    me, sibling = (x, y, c), (x, y, 1 - c)
    chips = [(1 - x, y), (x, 1 - y), (1 - x, 1 - y)]

    def slot(px, py, pc):
        idx = 4 * px + 2 * py + pc
        if not by_cols:
            return out_ref.at[idx]
        width = x_ref.shape[1]
        return out_ref.at[:, pl.ds(pl.multiple_of(idx * width, LANES), width)]

    def copy(k, block, to, src=None):
        return pltpu.make_async_remote_copy(
            src_ref=slot(*block) if src is None else src, dst_ref=slot(*block),
            send_sem=send_sems.at[k], recv_sem=recv_sems.at[k],
            device_id=to, device_id_type=pl.DeviceIdType.MESH)

    mine = pltpu.make_async_copy(x_ref, slot(*me), local_sem)
    first = [copy(0, me, sibling, src=x_ref)]
    first += [copy(1 + j, me, (*chip, c), src=x_ref) for j, chip in enumerate(chips)]
    passed = [copy(4 + j, (*chip, c), sibling) for j, chip in enumerate(chips)]

    def start():
        mine.start()
        for cp in first:
            cp.start()

    def forward():
        for j, chip in enumerate(chips):
            copy(1 + j, (*chip, c), me).wait_recv()
            passed[j].start()

    def finish():
        copy(0, sibling, me).wait_recv()
        for j, chip in enumerate(chips):
            copy(4 + j, (*chip, 1 - c), me).wait_recv()
        for cp in first + passed:
            cp.wait_send()
        mine.wait()

    return start, forward, finish


def _scatter_phases(p_ref, out_ref, send_sems, recv_sems, local_sem):
    x, y, c = _mesh_pos()
    me = 4 * x + 2 * y + c
    flips = [(fx, fy, fc) for fx in (0, 1) for fy in (0, 1) for fc in (0, 1)][1:]

    def copy(k, flip, receiving):
        px, py, pc = (1 - x if flip[0] else x, 1 - y if flip[1] else y, 1 - c if flip[2] else c)
        them = 4 * px + 2 * py + pc
        return pltpu.make_async_remote_copy(
            src_ref=p_ref.at[them], dst_ref=out_ref.at[them if receiving else me],
            send_sem=send_sems.at[k], recv_sem=recv_sems.at[k],
            device_id=(px, py, pc), device_id_type=pl.DeviceIdType.MESH)

    mine = pltpu.make_async_copy(p_ref.at[me], out_ref.at[me], local_sem)

    def start():
        mine.start()
        for k, flip in enumerate(flips):
            copy(k, flip, False).start()

    def finish():
        for k, flip in enumerate(flips):
            copy(k, flip, True).wait_recv()
            copy(k, flip, False).wait_send()
        mine.wait()

    return start, None, finish


class Rider(NamedTuple):
    kind: str
    src: jax.Array

    def out_shape(self):
        if self.kind == "gather":
            shape = (N_DEV,) + self.src.shape
        elif self.kind == "gather_cols":
            shape = (self.src.shape[0], N_DEV * self.src.shape[1])
        else:
            shape = self.src.shape
        return _sds(shape, self.src.dtype)

    def phases(self, in_ref, out_ref, sems):
        if self.kind == "scatter":
            return _scatter_phases(in_ref, out_ref, *sems)
        return _gather_phases(in_ref, out_ref, *sems, by_cols=self.kind == "gather_cols")


def _rider_hooks(riders, in_refs, out_refs, sem_refs, step, n_steps):
    phases = [r.phases(in_refs[i], out_refs[i], sem_refs[3 * i:3 * i + 3]) for i, r in enumerate(riders)]

    def begin():
        for start, forward, _ in phases:
            pl.when(step == 0)(start)
            if forward is not None:
                pl.when(step == (7 * n_steps) // 8)(forward)

    def end():
        for _, _, finish in phases:
            pl.when(step == n_steps - 1)(finish)

    return begin, end


def _split_refs(refs, n_in, n_out, n_scratch, n_riders):
    pos, parts = 0, []
    for n in (n_in, n_riders, n_out, n_riders, n_scratch, 3 * n_riders):
        parts.append(refs[pos:pos + n])
        pos += n
    return parts


def all_gather(shard, name):
    def body(x_ref, out_ref, send_sems, recv_sems, local_sem):
        start, forward, finish = _gather_phases(x_ref, out_ref, send_sems, recv_sems, local_sem)
        start()
        forward()
        finish()

    return pl.pallas_call(
        body, name=name, out_shape=_sds((N_DEV,) + shard.shape, shard.dtype),
        in_specs=[ANY], out_specs=ANY, scratch_shapes=_comm_sems(),
    )(shard)


def adamw(w, m, v, parts, name):
    n_parts, rows, cols = parts.shape
    tr = _row_tile(rows, 128)
    c1 = 1.0 - ADAM_B1 ** ADAM_STEP
    c2 = 1.0 - ADAM_B2 ** ADAM_STEP

    def body(w_ref, m_ref, v_ref, p_ref, g_ref, d_ref, nm_ref, nv_ref):
        g = p_ref[0].astype(F32)
        for s in range(1, n_parts):
            g = g + p_ref[s].astype(F32)
        nm = ADAM_B1 * m_ref[...] + (1.0 - ADAM_B1) * g
        nv = ADAM_B2 * v_ref[...] + (1.0 - ADAM_B2) * (g * g)
        delta = -ADAM_LR * ((nm / c1) / (jnp.sqrt(nv / c2) + ADAM_EPS) + ADAM_WD * w_ref[...])
        g_ref[...] = g
        d_ref[...] = delta
        nm_ref[...] = nm
        nv_ref[...] = nv

    mat = pl.BlockSpec((tr, cols), lambda i: (i, 0))
    return pl.pallas_call(
        body, name=name, grid=(rows // tr,),
        in_specs=[mat, mat, mat, pl.BlockSpec((n_parts, tr, cols), lambda i: (0, i, 0))],
        out_specs=[mat, mat, mat, mat],
        out_shape=[_sds((rows, cols), F32)] * 4,
        compiler_params=_params("parallel"),
    )(w, m, v, parts)


_NN = (((1,), (0,)), ((), ()))
_NT = (((1,), (1,)), ((), ()))
_TN = (((0,), (0,)), ((), ()))


def mm_cols(name, a, b_list, b_specs, nt, extras, extra_specs, out_shapes, out_specs, epilogue, n_blk, riders=(),
            tm_want=512):
    t_len, k_len = a.shape
    tm = _row_tile(t_len, tm_want)
    nb, ne, n_out, nr = len(b_list), len(extras), len(out_shapes), len(riders)
    t_steps = t_len // tm
    n_cols = b_specs[0].block_shape[-2 if nt else -1]

    def body(*refs):
        ins, r_in, outs, r_out, _, r_sem = _split_refs(refs, 1 + nb + ne, n_out, 0, nr)
        step = pl.program_id(0) * t_steps + pl.program_id(1)
        begin, end = _rider_hooks(riders, r_in, r_out, r_sem, step, n_blk * t_steps)
        begin()
        av = ins[0][...]
        for c0 in range(0, n_cols, MXU_WIDTH):
            cols = slice(c0, min(c0 + MXU_WIDTH, n_cols))
            accs = [lax.dot_general(av, br[cols, :] if nt else br[:, cols], _NT if nt else _NN,
                                    preferred_element_type=F32) for br in ins[1:1 + nb]]
            epilogue(accs, ins[1 + nb:], outs, cols)
        end()

    res = pl.pallas_call(
        body, name=name, grid=(n_blk, t_steps),
        in_specs=([pl.BlockSpec((tm, k_len), lambda j, t: (t, 0))] + list(b_specs) + list(extra_specs(tm))
                  + [ANY] * nr),
        out_specs=list(out_specs(tm)) + [ANY] * nr,
        out_shape=list(out_shapes) + [r.out_shape() for r in riders],
        scratch_shapes=_comm_sems() * nr,
        compiler_params=_params("arbitrary", "arbitrary"),
    )(a, *b_list, *extras, *[r.src for r in riders])
    return res[:n_out], res[n_out:]


def mm_reduce(name, a_list, a_specs, b_list, b_specs, nt, res, scale, t_len, n_len, n_blk, riders=(), tm_want=512):
    tm = _row_tile(t_len, tm_want)
    na, nr = len(a_list), len(riders)
    has_res = res is not None
    t_steps = t_len // tm

    def body(*refs):
        ins, r_in, outs, r_out, _, r_sem = _split_refs(refs, 2 * na + has_res, 1, 0, nr)
        o_ref = outs[0]
        j = pl.program_id(1)
        step = pl.program_id(0) * n_blk + j
        begin, end = _rider_hooks(riders, r_in, r_out, r_sem, step, t_steps * n_blk)
        begin()

        part = None
        for ar, br in zip(ins[:na], ins[na:2 * na]):
            d = lax.dot_general(ar[...], br[...], _NT if nt else _NN, preferred_element_type=F32)
            part = d if part is None else part + d

        @pl.when(j == 0)
        def _():
            o_ref[...] = part

        @pl.when(j > 0)
        def _():
            o_ref[...] += part

        if has_res or scale != 1.0:
            @pl.when(j == n_blk - 1)
            def _():
                val = o_ref[...] * scale if scale != 1.0 else o_ref[...]
                o_ref[...] = ins[2 * na][...] + val if has_res else val

        end()

    row = pl.BlockSpec((tm, n_len), lambda t, j: (t, 0))
    out = pl.pallas_call(
        body, name=name, grid=(t_steps, n_blk),
        in_specs=list(a_specs(tm)) + list(b_specs) + ([row] if has_res else []) + [ANY] * nr,
        out_specs=[row] + [ANY] * nr,
        out_shape=[_sds((t_len, n_len), F32)] + [r.out_shape() for r in riders],
        scratch_shapes=_comm_sems() * nr,
        compiler_params=_params("arbitrary", "arbitrary"),
    )(*a_list, *b_list, *([res] if has_res else []), *[r.src for r in riders])
    return out[0], out[1:]


def mm_jsum(name, a_list, b_list, nt, res, scale, n_len, riders=(), tm_want=512, tn=512):
    n_j, t_len, k_j = a_list[0].shape
    tm = _row_tile(t_len, tm_want)
    tn = min(tn, n_len)
    na, nr = len(a_list), len(riders)
    has_res = res is not None
    t_steps, n_steps = t_len // tm, n_len // tn

    def body(*refs):
        ins, r_in, outs, r_out, _, r_sem = _split_refs(refs, 2 * na + has_res, 1, 0, nr)
        o_ref = outs[0]
        step = pl.program_id(0) * t_steps + pl.program_id(1)
        begin, end = _rider_hooks(riders, r_in, r_out, r_sem, step, n_steps * t_steps)
        begin()
        for c0 in range(0, tn, MXU_WIDTH):
            cols = slice(c0, min(c0 + MXU_WIDTH, tn))
            acc = None
            for ar, br in zip(ins[:na], ins[na:2 * na]):
                for j in range(n_j):
                    bj = br[j, cols, :] if nt else br[j, :, cols]
                    d = lax.dot_general(ar[j], bj, _NT if nt else _NN, preferred_element_type=F32)
                    acc = d if acc is None else acc + d
            val = acc * scale if scale != 1.0 else acc
            o_ref[:, cols] = ins[2 * na][:, cols] + val if has_res else val
        end()

    a_spec = pl.BlockSpec((n_j, tm, k_j), lambda n, t: (0, t, 0))
    b_spec = (pl.BlockSpec((n_j, tn, k_j), lambda n, t: (0, n, 0)) if nt
              else pl.BlockSpec((n_j, k_j, tn), lambda n, t: (0, 0, n)))
    tile = pl.BlockSpec((tm, tn), lambda n, t: (t, n))
    out = pl.pallas_call(
        body, name=name, grid=(n_steps, t_steps),
        in_specs=[a_spec] * na + [b_spec] * na + ([tile] if has_res else []) + [ANY] * nr,
        out_specs=[tile] + [ANY] * nr,
        out_shape=[_sds((t_len, n_len), F32)] + [r.out_shape() for r in riders],
        scratch_shapes=_comm_sems() * nr,
        compiler_params=_params("arbitrary", "arbitrary"),
    )(*a_list, *b_list, *([res] if has_res else []), *[r.src for r in riders])
    return out[0], out[1:]


def mm_xt(name, xt, dy, dy_spec, n_cols, n_blk, riders=(), tm_want=512):
    m_len, t_len = xt.shape
    tm = _row_tile(m_len, tm_want)
    m_steps = m_len // tm
    nr = len(riders)

    def body(*refs):
        ins, r_in, outs, r_out, _, r_sem = _split_refs(refs, 2, 1, 0, nr)
        step = pl.program_id(0) * m_steps + pl.program_id(1)
        begin, end = _rider_hooks(riders, r_in, r_out, r_sem, step, n_blk * m_steps)
        begin()
        outs[0][...] = lax.dot_general(ins[0][...], ins[1][...], _NN, preferred_element_type=F32).astype(BF16)
        end()

    res = pl.pallas_call(
        body, name=name, grid=(n_blk, m_steps),
        in_specs=[pl.BlockSpec((tm, t_len), lambda j, m: (m, 0)), dy_spec] + [ANY] * nr,
        out_specs=[pl.BlockSpec((None, tm, n_cols), lambda j, m: (j, m, 0))] + [ANY] * nr,
        out_shape=[_sds((n_blk, m_len, n_cols), BF16)] + [r.out_shape() for r in riders],
        scratch_shapes=_comm_sems() * nr,
        compiler_params=_params("arbitrary", "arbitrary"),
    )(xt, dy, *[r.src for r in riders])
    return res[0], res[1:]


def mm_tn(name, x, x_spec, dy_list, dy_specs, out_shapes, out_specs, scale, t_len, n_blk, riders=()):
    tt = _row_tile(t_len, 2048)
    nd, nr = len(dy_list), len(riders)
    t_steps = t_len // tt
    acc_shapes = [pltpu.VMEM(spec.block_shape[-2:], F32) for spec in out_specs]

    def body(*refs):
        ins, r_in, outs, r_out, accs, r_sem = _split_refs(refs, 1 + nd, nd, nd, nr)
        t = pl.program_id(1)
        step = pl.program_id(0) * t_steps + t
        begin, end = _rider_hooks(riders, r_in, r_out, r_sem, step, n_blk * t_steps)
        begin()
        xv = ins[0][...]
        for dr, acc in zip(ins[1:], accs):
            d = lax.dot_general(xv, dr[...], _TN, preferred_element_type=F32)

            @pl.when(t == 0)
            def _():
                acc[...] = d

            @pl.when(t > 0)
            def _():
                acc[...] += d

        @pl.when(t == t_steps - 1)
        def _():
            for acc, orf in zip(accs, outs):
                val = acc[...] * scale if scale != 1.0 else acc[...]
                orf[...] = val.astype(orf.dtype)

        end()

    res = pl.pallas_call(
        body, name=name, grid=(n_blk, t_steps),
        in_specs=[x_spec(tt)] + list(dy_specs(tt)) + [ANY] * nr,
        out_specs=list(out_specs) + [ANY] * nr,
        out_shape=list(out_shapes) + [r.out_shape() for r in riders],
        scratch_shapes=acc_shapes + _comm_sems() * nr,
        compiler_params=_params("arbitrary", "arbitrary"),
    )(x, *dy_list, *[r.src for r in riders])
    return res[:nd], res[nd:]


def rms_fwd(x, g, name):
    t_len, d = x.shape
    tm = _row_tile(t_len, 512)

    def body(x_ref, g_ref, h_ref, ht_ref):
        xv = x_ref[...]
        r = lax.rsqrt(jnp.mean(xv * xv, axis=-1, keepdims=True) + EPS)
        hv = xv * r * g_ref[...]
        h_ref[...] = hv.astype(BF16)
        ht_ref[...] = hv.T.astype(BF16)

    row = pl.BlockSpec((tm, d), lambda i: (i, 0))
    return pl.pallas_call(
        body, name=name, grid=(t_len // tm,),
        in_specs=[row, pl.BlockSpec((1, d), lambda i: (0, 0))],
        out_specs=[row, pl.BlockSpec((d, tm), lambda i: (0, i))],
        out_shape=[_sds((t_len, d), BF16), _sds((d, t_len), BF16)],
        compiler_params=_params("parallel"),
    )(x, g)


def rms_bwd(x, g, dh, dres, name):
    t_len, d = x.shape
    tm = _row_tile(t_len, 512)

    def body(x_ref, g_ref, dh_ref, dr_ref, dx_ref, dxb_ref, dg_ref):
        i = pl.program_id(0)
        xv = x_ref[...]
        r = lax.rsqrt(jnp.mean(xv * xv, axis=-1, keepdims=True) + EPS)
        xh = xv * r
        dhv = dh_ref[...]

        @pl.when(i == 0)
        def _():
            dg_ref[...] = jnp.zeros_like(dg_ref)

        dg_ref[...] += jnp.sum(dhv * xh, axis=0, keepdims=True)
        dxh = dhv * g_ref[...]
        dx = dr_ref[...] + r * (dxh - xh * jnp.mean(dxh * xh, axis=-1, keepdims=True))
        dx_ref[...] = dx
        dxb_ref[...] = dx.astype(BF16)

    row = pl.BlockSpec((tm, d), lambda i: (i, 0))
    vec = pl.BlockSpec((1, d), lambda i: (0, 0))
    return pl.pallas_call(
        body, name=name, grid=(t_len // tm,),
        in_specs=[row, vec, row, row],
        out_specs=[row, row, vec],
        out_shape=[_sds((t_len, d), F32), _sds((t_len, d), BF16), _sds((1, d), F32)],
        compiler_params=_params("arbitrary"),
    )(x, g, dh, dres)


def loss_head(y, target, name):
    t_len, d = y.shape
    tm = _row_tile(t_len, 512)

    def body(y_ref, t_ref, l_ref, dy_ref, dyb_ref):
        i = pl.program_id(0)
        err = y_ref[...] - t_ref[...]

        @pl.when(i == 0)
        def _():
            l_ref[...] = jnp.zeros_like(l_ref)

        rows = jnp.sum(err * err, axis=-1, keepdims=True) * (1.0 / d)
        l_ref[...] += 0.5 * jnp.sum(rows, axis=0, keepdims=True)
        dy = err * (1.0 / d)
        dy_ref[...] = dy
        dyb_ref[...] = dy.astype(BF16)

    row = pl.BlockSpec((tm, d), lambda i: (i, 0))
    return pl.pallas_call(
        body, name=name, grid=(t_len // tm,),
        in_specs=[row, row],
        out_specs=[pl.BlockSpec((8, LANES), lambda i: (0, 0)), row, row],
        out_shape=[_sds((8, LANES), F32), _sds((t_len, d), F32), _sds((t_len, d), BF16)],
        compiler_params=_params("arbitrary"),
    )(y, target)


def _conv_specs(tm, ch):
    per = tm // HALO
    cur = lambda cb: pl.BlockSpec((tm, ch), lambda i: (i, cb))
    prev = lambda cb: pl.BlockSpec((HALO, ch), lambda i: (jnp.maximum(i * per - 1, 0), cb))
    return [cur(0), cur(1), prev(0), prev(1)]


def _tap_scratch(rows, ch):
    return pltpu.VMEM((SUBLANES, rows + SUBLANES, ch), F32)


def _shifted_copies(buf, rows):
    buf[0, rows:rows + SUBLANES, :] = jnp.zeros((SUBLANES, buf.shape[2]), F32)
    for s in range(1, SUBLANES):
        buf[s, 0:rows, :] = buf[0, pl.ds(s, rows), :]


def _tap_rows(buf, off):
    shift = off % SUBLANES
    return buf[shift, off - shift:off - shift + ROW_CHUNK, :]


def _fill_glu(ext, a_ref, gt_ref, ap_ref, gp_ref, i, tm):
    vp = ap_ref[...] * _sigmoid(gp_ref[...])
    ext[0, 0:HALO, :] = jnp.where(i > 0, vp, 0.0)
    ext[0, HALO:HALO + tm, :] = a_ref[...] * _sigmoid(gt_ref[...])
    _shifted_copies(ext, HALO + tm)


def _conv_rows(ext, w_ref, b_ref, r0):
    acc = jnp.broadcast_to(b_ref[...], (ROW_CHUNK, b_ref.shape[1]))
    for k in range(CONV_WIDTH):
        acc = acc + w_ref[k:k + 1, :] * _tap_rows(ext, r0 + HALO - (CONV_WIDTH - 1) + k)
    return acc


def _layer_norm(yv):
    mu = jnp.mean(yv, axis=-1, keepdims=True)
    cen = yv - mu
    var = jnp.mean(cen * cen, axis=-1, keepdims=True)
    rstd = lax.rsqrt(var + EPS)
    return cen * rstd, rstd


def conv_fwd(z, w, b, lg, lb, name):
    t_len = z.shape[0]
    ch = w.shape[1]
    tm = _row_tile(t_len, 256)

    def body(a_ref, gt_ref, ap_ref, gp_ref, w_ref, b_ref, lg_ref, lb_ref, y_ref, pre_ref, ext):
        i = pl.program_id(0)
        _fill_glu(ext, a_ref, gt_ref, ap_ref, gp_ref, i, tm)
        for r0 in range(0, tm, ROW_CHUNK):
            pre = _conv_rows(ext, w_ref, b_ref, r0)
            pre_ref[r0:r0 + ROW_CHUNK, :] = pre
            xh, _ = _layer_norm(pre)
            u = xh * lg_ref[...] + lb_ref[...]
            y_ref[r0:r0 + ROW_CHUNK, :] = (u * _sigmoid(u)).astype(BF16)

    vec = pl.BlockSpec((1, ch), lambda i: (0, 0))
    row = pl.BlockSpec((tm, ch), lambda i: (i, 0))
    return pl.pallas_call(
        body, name=name, grid=(t_len // tm,),
        in_specs=_conv_specs(tm, ch) + [pl.BlockSpec((32, ch), lambda i: (0, 0)), vec, vec, vec],
        out_specs=[row, row],
        out_shape=[_sds((t_len, ch), BF16), _sds((t_len, ch), F32)],
        scratch_shapes=[_tap_scratch(HALO + tm, ch)],
        compiler_params=_params("parallel"),
    )(z, z, z, z, w, b, lg, lb)


def conv_bwd_norm(pre, dy_cat, lg, lb, name):
    t_len, ch = pre.shape
    tm = _row_tile(t_len, 256)

    def body(pre_ref, dy_ref, lg_ref, lb_ref, dc_ref, dlg_ref, dlb_ref, db_ref):
        i = pl.program_id(0)

        @pl.when(i == 0)
        def _():
            dlg_ref[...] = jnp.zeros_like(dlg_ref)
            dlb_ref[...] = jnp.zeros_like(dlb_ref)
            db_ref[...] = jnp.zeros_like(db_ref)

        for r0 in range(0, tm, ROW_CHUNK):
            xh, rstd = _layer_norm(pre_ref[r0:r0 + ROW_CHUNK, :])
            u = xh * lg_ref[...] + lb_ref[...]
            sg = _sigmoid(u)
            du = dy_ref[r0:r0 + ROW_CHUNK, :] * (sg * (1.0 + u * (1.0 - sg)))
            dlg_ref[...] += jnp.sum(du * xh, axis=0, keepdims=True)
            dlb_ref[...] += jnp.sum(du, axis=0, keepdims=True)
            dxh = du * lg_ref[...]
            dc = rstd * (dxh - jnp.mean(dxh, axis=-1, keepdims=True)
                         - xh * jnp.mean(dxh * xh, axis=-1, keepdims=True))
            db_ref[...] += jnp.sum(dc, axis=0, keepdims=True)
            dc_ref[r0:r0 + ROW_CHUNK, :] = dc

    vec = pl.BlockSpec((1, ch), lambda i: (0, 0))
    row = pl.BlockSpec((tm, ch), lambda i: (i, 0))
    return pl.pallas_call(
        body, name=name, grid=(t_len // tm,),
        in_specs=[row, row, vec, vec],
        out_specs=[row, vec, vec, vec],
        out_shape=[_sds((t_len, ch), F32)] + [_sds((1, ch), F32)] * 3,
        compiler_params=_params("arbitrary"),
    )(pre, dy_cat, lg, lb)


def conv_bwd_taps(z, dc, w, name):
    t_len = z.shape[0]
    ch = w.shape[1]
    tm = _row_tile(t_len, 256)
    per = tm // HALO
    n_tiles = t_len // tm
    last_halo = t_len // HALO - 1

    def body(a_ref, gt_ref, ap_ref, gp_ref, dc_ref, dn_ref, w_ref, dz_a_ref, dz_g_ref, dw_ref, ext, dext):
        i = pl.program_id(0)
        _fill_glu(ext, a_ref, gt_ref, ap_ref, gp_ref, i, tm)
        dext[0, 0:tm, :] = dc_ref[...]
        dext[0, tm:tm + HALO, :] = jnp.where(i < n_tiles - 1, dn_ref[...], 0.0)
        _shifted_copies(dext, tm + HALO)

        @pl.when(i == 0)
        def _():
            dw_ref[...] = jnp.zeros_like(dw_ref)

        for r0 in range(0, tm, ROW_CHUNK):
            dcv = dext[0, r0:r0 + ROW_CHUNK, :]
            dv = jnp.zeros((ROW_CHUNK, ch), F32)
            for k in range(CONV_WIDTH):
                dv = dv + w_ref[k:k + 1, :] * _tap_rows(dext, r0 + (CONV_WIDTH - 1) - k)
                prod = dcv * _tap_rows(ext, r0 + HALO - (CONV_WIDTH - 1) + k)
                fold = prod[0:8]
                for s in range(8, ROW_CHUNK, 8):
                    fold = fold + prod[s:s + 8]
                dw_ref[k] += fold
            av = a_ref[r0:r0 + ROW_CHUNK, :]
            sg = _sigmoid(gt_ref[r0:r0 + ROW_CHUNK, :])
            dz_a_ref[r0:r0 + ROW_CHUNK, :] = (dv * sg).astype(BF16)
            dz_g_ref[r0:r0 + ROW_CHUNK, :] = (dv * av * sg * (1.0 - sg)).astype(BF16)

    row = pl.BlockSpec((tm, ch), lambda i: (i, 0))
    nxt = pl.BlockSpec((HALO, ch), lambda i: (jnp.minimum((i + 1) * per, last_halo), 0))
    return pl.pallas_call(
        body, name=name, grid=(n_tiles,),
        in_specs=_conv_specs(tm, ch) + [row, nxt, pl.BlockSpec((32, ch), lambda i: (0, 0))],
        out_specs=[row, row, pl.BlockSpec((32, 8, ch), lambda i: (0, 0, 0))],
        out_shape=[_sds((t_len, ch), BF16), _sds((t_len, ch), BF16), _sds((32, 8, ch), F32)],
        scratch_shapes=[_tap_scratch(HALO + tm, ch), _tap_scratch(tm + HALO, ch)],
        compiler_params=_params("arbitrary"),
    )(z, z, z, z, dc, dc, w)


def _head_masks(rows):
    lane = lax.broadcasted_iota(jnp.int32, (rows, LANES), 1)
    low = lane < HEAD_DIM
    return low, jnp.logical_not(low)


def _per_head_mean(val, low):
    s_low = jnp.sum(jnp.where(low, val, 0.0), axis=-1, keepdims=True)
    s_high = jnp.sum(jnp.where(low, 0.0, val), axis=-1, keepdims=True)
    return jnp.where(low, s_low, s_high) * (1.0 / HEAD_DIM)


def qk_norm_fwd(z, g2, ch, name):
    t_len = z.shape[0]
    tm = _row_tile(t_len, 512)

    def body(z_ref, g_ref, o_ref):
        low, _ = _head_masks(tm)
        for c0 in range(0, ch, LANES):
            cols = slice(c0, c0 + LANES)
            xv = z_ref[:, cols]
            r = lax.rsqrt(_per_head_mean(xv * xv, low) + EPS)
            o_ref[:, cols] = xv * r * g_ref[:, cols]

    return pl.pallas_call(
        body, name=name, grid=(t_len // tm, 2),
        in_specs=[pl.BlockSpec((tm, ch), lambda i, w: (i, 2 + w)),
                  pl.BlockSpec((1, ch), lambda i, w: (0, w))],
        out_specs=pl.BlockSpec((tm, ch), lambda i, w: (i, w)),
        out_shape=_sds((t_len, 2 * ch), F32),
        compiler_params=_params("parallel", "parallel"),
    )(z, g2)


def qk_norm_bwd(z, g, d_list, z_off, ch, name):
    t_len = z.shape[0]
    tm = _row_tile(t_len, 512)
    nd = len(d_list)

    def body(*refs):
        z_ref, g_ref, d_refs = refs[0], refs[1], refs[2:2 + nd]
        dz_ref, dg_ref = refs[2 + nd], refs[3 + nd]
        low, _ = _head_masks(tm)

        @pl.when(pl.program_id(0) == 0)
        def _():
            dg_ref[...] = jnp.zeros_like(dg_ref)

        for c0 in range(0, ch, LANES):
            cols = slice(c0, c0 + LANES)
            xv = z_ref[:, cols]
            r = lax.rsqrt(_per_head_mean(xv * xv, low) + EPS)
            xh = xv * r
            dy = d_refs[0][:, cols]
            for dr in d_refs[1:]:
                dy = dy + dr[:, cols]
            dg_ref[...] += jnp.sum(dy * xh, axis=0, keepdims=True)
            dxh = dy * g_ref[...]
            dz_ref[:, cols] = (r * (dxh - xh * _per_head_mean(dxh * xh, low))).astype(BF16)

    blk = pl.BlockSpec((tm, ch), lambda i: (i, 0))
    return pl.pallas_call(
        body, name=name, grid=(t_len // tm,),
        in_specs=[pl.BlockSpec((tm, ch), lambda i: (i, z_off)),
                  pl.BlockSpec((1, LANES), lambda i: (0, 0))] + [blk] * nd,
        out_specs=[blk, pl.BlockSpec((1, LANES), lambda i: (0, 0))],
        out_shape=[_sds((t_len, ch), BF16), _sds((1, LANES), F32)],
        compiler_params=_params("arbitrary"),
    )(z, g, *d_list)


def _alibi_bias(n_heads, dilation):
    slopes = 2.0 ** (-ALIBI_MAX_BIAS * jnp.arange(1, n_heads + 1, dtype=F32) / n_heads)
    qi = jnp.arange(ATT_BLOCK)[:, None]
    kj = jnp.arange(ATT_BLOCK)[None, :]
    dist_cur = (qi - kj).astype(F32)
    dist_prev = (ATT_BLOCK + qi - kj).astype(F32)
    cur = jnp.where((qi >= kj)[None], -slopes[:, None, None] * (dilation * dist_cur)[None], MASKED)
    prev = jnp.where((kj >= qi)[None], -slopes[:, None, None] * (dilation * dist_prev)[None], MASKED)
    return jnp.concatenate([prev, cur], axis=-1).astype(F32)


def _stack_heads(val, low, high):
    return jnp.concatenate([jnp.where(low, val, 0.0), jnp.where(high, val, 0.0)], axis=0).astype(BF16)


def _head_rows(val, low, high):
    other = pltpu.roll(val, HEAD_DIM, axis=1)
    rows = jnp.concatenate([jnp.where(low, val, other), jnp.where(high, val, other)], axis=0)
    return jnp.concatenate([rows, rows], axis=1)


def _unit_scores(q2, k2, b_ref, has_prev):
    s = lax.dot_general(q2, k2, _NT, preferred_element_type=F32)
    s = s + b_ref[...].reshape(2 * ATT_BLOCK, 2 * ATT_BLOCK)
    penalty = jnp.where(has_prev, 0.0, MASKED)
    return jnp.concatenate([s[:, :ATT_BLOCK] + penalty, s[:, ATT_BLOCK:]], axis=1)


def _strided_rows(r, dilation):
    per = ATT_CHUNK // dilation
    return pl.ds(r, per, stride=dilation) if dilation > 1 else pl.ds(0, per)


def _deinterleave(dst, src_ref, dilation, base=None, dtype=None):
    per = ATT_CHUNK // dilation
    for r in range(dilation):
        val = src_ref[_strided_rows(r, dilation), :]
        val = val if dtype is None else val.astype(dtype)
        if base is None:
            dst[r * per:(r + 1) * per, :] = val
        else:
            dst[pl.ds(pl.multiple_of(base + r * per, ATT_BLOCK), per), :] = val


def _unit_rows(u, c, nb, base, pbase):
    in_chunk = lax.rem(u, jnp.int32(nb)) > 0
    has_prev = jnp.logical_or(in_chunk, c > 0)
    urow = pl.multiple_of(u * ATT_BLOCK, ATT_BLOCK)
    crow = pl.multiple_of(base + u * ATT_BLOCK, ATT_BLOCK)
    prow = pl.multiple_of(jnp.where(in_chunk, base + (u - 1) * ATT_BLOCK,
                                    pbase + (u + nb - 1) * ATT_BLOCK), ATT_BLOCK)
    return in_chunk, has_prev, urow, crow, prow


def _interleave(dst_ref, src, dilation, base=None):
    per = ATT_CHUNK // dilation
    for r in range(dilation):
        if base is None:
            val = src[r * per:(r + 1) * per, :]
        else:
            val = src[pl.ds(pl.multiple_of(base + r * per, ATT_BLOCK), per), :]
        dst_ref[_strided_rows(r, dilation), :] = val


def attn_fwd(qk, z, dilation, ch, name):
    t_len = qk.shape[0]
    pairs = ch // LANES
    nc = t_len // ATT_CHUNK
    nb = ATT_UNITS // dilation
    scale = 1.0 / math.sqrt(HEAD_DIM)
    bias = _alibi_bias(2 * pairs, dilation)

    def body(q_ref, k_ref, v_ref, b_ref, o_ref, l_ref, qd, kx, vx, od, ld):
        c = pl.program_id(1)
        slot = lax.rem(c, jnp.int32(2))
        base, pbase = slot * ATT_CHUNK, (1 - slot) * ATT_CHUNK

        @pl.when(c == 0)
        def _():
            kx[...] = jnp.zeros_like(kx)
            vx[...] = jnp.zeros_like(vx)

        _deinterleave(qd, q_ref, dilation)
        _deinterleave(kx, k_ref, dilation, base, BF16)
        _deinterleave(vx, v_ref, dilation, base, BF16)
        low, high = _head_masks(ATT_BLOCK)

        def unit(u, carry):
            _, has_prev, urow, crow, prow = _unit_rows(u, c, nb, base, pbase)
            q2 = _stack_heads(qd[pl.ds(urow, ATT_BLOCK), :] * scale, low, high)
            k2 = jnp.concatenate([kx[pl.ds(prow, ATT_BLOCK), :], kx[pl.ds(crow, ATT_BLOCK), :]], axis=0)
            v2 = jnp.concatenate([vx[pl.ds(prow, ATT_BLOCK), :], vx[pl.ds(crow, ATT_BLOCK), :]], axis=0)
            s = _unit_scores(q2, k2, b_ref, has_prev)
            mx = jnp.max(s, axis=-1, keepdims=True)
            e = jnp.exp(s - mx)
            den = jnp.sum(e, axis=-1, keepdims=True)
            acc = lax.dot_general(e.astype(BF16), v2, _NN, preferred_element_type=F32) / den
            lse = jnp.broadcast_to(mx + jnp.log(den), acc.shape)
            od[pl.ds(urow, ATT_BLOCK), :] = jnp.where(low, acc[:ATT_BLOCK], acc[ATT_BLOCK:])
            ld[pl.ds(urow, ATT_BLOCK), :] = jnp.where(low, lse[:ATT_BLOCK], lse[ATT_BLOCK:])
            return carry

        lax.fori_loop(0, ATT_UNITS, unit, 0, unroll=8)
        _interleave(o_ref, od, dilation)
        _interleave(l_ref, ld, dilation)

    blk = (ATT_CHUNK, LANES)
    bias_spec = pl.BlockSpec((2, ATT_BLOCK, 2 * ATT_BLOCK), lambda p, c: (p, 0, 0))
    out_spec = pl.BlockSpec(blk, lambda p, c: (c, p))
    return pl.pallas_call(
        body, name=name, grid=(pairs, nc),
        in_specs=[pl.BlockSpec(blk, lambda p, c: (c, p)),
                  pl.BlockSpec(blk, lambda p, c: (c, pairs + p)),
                  pl.BlockSpec(blk, lambda p, c: (c, 4 * pairs + p)),
                  bias_spec],
        out_specs=[out_spec, out_spec],
        out_shape=[_sds((t_len, ch), F32)] * 2,
        scratch_shapes=[pltpu.VMEM((ATT_CHUNK, LANES), F32),
                        pltpu.VMEM((2 * ATT_CHUNK, LANES), BF16), pltpu.VMEM((2 * ATT_CHUNK, LANES), BF16),
                        pltpu.VMEM((ATT_CHUNK, LANES), F32), pltpu.VMEM((ATT_CHUNK, LANES), F32)],
        compiler_params=_params("arbitrary", "arbitrary"),
    )(qk, qk, z, bias)


def attn_combine(outs, lses, y_conv, name):
    t_len, ch = outs[0].shape
    tm = _row_tile(t_len, 512)

    def body(o1, o2, o3, l1, l2, l3, yc_ref, out_ref, cat_ref, lg_ref):
        a, b, c = l1[...], l2[...], l3[...]
        mx = jnp.maximum(jnp.maximum(a, b), c)
        tot = mx + jnp.log(jnp.exp(a - mx) + jnp.exp(b - mx) + jnp.exp(c - mx))
        val = jnp.exp(a - tot) * o1[...] + jnp.exp(b - tot) * o2[...] + jnp.exp(c - tot) * o3[...]
        out_ref[...] = val
        cat_ref[:, :ch] = yc_ref[...]
        cat_ref[:, ch:] = val.astype(BF16)
        lg_ref[...] = tot

    row = pl.BlockSpec((tm, ch), lambda i: (i, 0))
    return pl.pallas_call(
        body, name=name, grid=(t_len // tm,),
        in_specs=[row] * 7, out_specs=[row, pl.BlockSpec((tm, 2 * ch), lambda i: (i, 0)), row],
        out_shape=[_sds((t_len, ch), F32), _sds((t_len, 2 * ch), BF16), _sds((t_len, ch), F32)],
        compiler_params=_params("parallel"),
    )(*outs, *lses, y_conv)


def attn_bwd(qk, z, dy_cat, out, lg, dilation, ch, name):
    t_len = qk.shape[0]
    pairs = ch // LANES
    nc = t_len // ATT_CHUNK
    nb = ATT_UNITS // dilation
    scale = 1.0 / math.sqrt(HEAD_DIM)
    bias = _alibi_bias(2 * pairs, dilation)

    def body(q_ref, k_ref, v_ref, do_ref, out_ref, lg_ref, b_ref, dq_ref, dk_ref, dv_ref,
             qd, dod, lgd, dld, dl_nat, kx, vx, dkx, dvx, dqd):
        c = pl.program_id(1)
        slot = lax.rem(c, jnp.int32(2))
        base, pbase = slot * ATT_CHUNK, (1 - slot) * ATT_CHUNK

        @pl.when(c == 0)
        def _():
            for ref in (kx, vx, dkx, dvx):
                ref[...] = jnp.zeros_like(ref)

        @pl.when(c < nc)
        def _():
            low_all, _ = _head_masks(ATT_CHUNK)
            dl_nat[...] = _per_head_mean(do_ref[...] * out_ref[...], low_all) * float(HEAD_DIM)
            _deinterleave(qd, q_ref, dilation)
            _deinterleave(dod, do_ref, dilation)
            _deinterleave(lgd, lg_ref, dilation)
            _deinterleave(dld, dl_nat, dilation)
            _deinterleave(kx, k_ref, dilation, base, BF16)
            _deinterleave(vx, v_ref, dilation, base, BF16)
            low, high = _head_masks(ATT_BLOCK)

            def unit(u, carry):
                _, has_prev, urow, crow, prow = _unit_rows(u, c, nb, base, pbase)
                rows = pl.ds(urow, ATT_BLOCK)
                q2 = _stack_heads(qd[rows, :] * scale, low, high)
                do2 = _stack_heads(dod[rows, :], low, high)
                lse = _head_rows(lgd[rows, :], low, high)
                delta = _head_rows(dld[rows, :], low, high)
                k2 = jnp.concatenate([kx[pl.ds(prow, ATT_BLOCK), :], kx[pl.ds(crow, ATT_BLOCK), :]], axis=0)
                v2 = jnp.concatenate([vx[pl.ds(prow, ATT_BLOCK), :], vx[pl.ds(crow, ATT_BLOCK), :]], axis=0)
                prob = jnp.exp(_unit_scores(q2, k2, b_ref, has_prev) - lse)
                dp = lax.dot_general(do2, v2, _NT, preferred_element_type=F32)
                ds = (prob * (dp - delta)).astype(BF16)
                dq2 = lax.dot_general(ds, k2, _NN, preferred_element_type=F32)
                dk2 = lax.dot_general(ds, q2, _TN, preferred_element_type=F32)
                dv2 = lax.dot_general(prob.astype(BF16), do2, _TN, preferred_element_type=F32)
                dqd[rows, :] = scale * jnp.where(low, dq2[:ATT_BLOCK], dq2[ATT_BLOCK:])
                dkx[pl.ds(prow, ATT_BLOCK), :] += dk2[:ATT_BLOCK]
                dkx[pl.ds(crow, ATT_BLOCK), :] = dk2[ATT_BLOCK:]
                dvx[pl.ds(prow, ATT_BLOCK), :] += dv2[:ATT_BLOCK]
                dvx[pl.ds(crow, ATT_BLOCK), :] = dv2[ATT_BLOCK:]
                return carry

            lax.fori_loop(0, ATT_UNITS, unit, 0, unroll=8)
            _interleave(dq_ref, dqd, dilation)

        @pl.when(c > 0)
        def _():
            _interleave(dk_ref, dkx, dilation, pbase)
            _interleave(dv_ref, dvx, dilation, pbase)

    blk = (ATT_CHUNK, LANES)
    here = lambda c: jnp.minimum(c, nc - 1)
    spec = lambda off: pl.BlockSpec(blk, lambda p, c: (here(c), off + p))
    late = pl.BlockSpec(blk, lambda p, c: (jnp.maximum(c - 1, 0), p))
    bias_spec = pl.BlockSpec((2, ATT_BLOCK, 2 * ATT_BLOCK), lambda p, c: (p, 0, 0))
    f32_chunk = pltpu.VMEM((ATT_CHUNK, LANES), F32)
    return pl.pallas_call(
        body, name=name, grid=(pairs, nc + 1),
        in_specs=[spec(0), spec(pairs), spec(4 * pairs), spec(pairs), spec(0), spec(0), bias_spec],
        out_specs=[spec(0), late, late],
        out_shape=[_sds((t_len, ch), F32)] * 3,
        scratch_shapes=[f32_chunk] * 5
                       + [pltpu.VMEM((2 * ATT_CHUNK, LANES), BF16)] * 2
                       + [pltpu.VMEM((2 * ATT_CHUNK, LANES), F32)] * 2 + [f32_chunk],
        compiler_params=_params("arbitrary", "arbitrary"),
    )(qk, qk, z, dy_cat, out, lg, bias)


def pack_dz(pieces, dvs, name):
    t_len, ch = pieces[0].shape
    tm = _row_tile(t_len, 512)
    n_p = len(pieces)

    def body(*refs):
        o_ref = refs[-1]
        for k in range(n_p):
            o_ref[:, k * ch:(k + 1) * ch] = refs[k][...]
        a_ref, b_ref, c_ref = refs[n_p:n_p + 3]
        o_ref[:, n_p * ch:(n_p + 1) * ch] = (a_ref[...] + b_ref[...] + c_ref[...]).astype(BF16)

    row = pl.BlockSpec((tm, ch), lambda i: (i, 0))
    return pl.pallas_call(
        body, name=name, grid=(t_len // tm,), in_specs=[row] * (n_p + 3),
        out_specs=pl.BlockSpec((tm, (n_p + 1) * ch), lambda i: (i, 0)),
        out_shape=_sds((t_len, (n_p + 1) * ch), BF16), compiler_params=_params("parallel"),
    )(*pieces, *dvs)


def _blk3(rows, cols):
    return pl.BlockSpec((None, rows, cols), lambda j, t: (j, 0, 0))


def ffn_up(h, wg, wu, name, riders):
    t_len, d = h.shape
    n_blk, _, fj = wg.shape

    def epilogue(accs, e_refs, o_refs, cols):
        gate, up = accs
        o_refs[0][:, cols] = gate.astype(BF16)
        o_refs[1][:, cols] = up.astype(BF16)
        o_refs[2][:, cols] = (gate * _sigmoid(gate) * up).astype(BF16)

    act = lambda tm: pl.BlockSpec((None, tm, fj), lambda j, t: (j, t, 0))
    return mm_cols(name, h, [wg, wu], [_blk3(d, fj)] * 2, False, [], lambda tm: [],
                   [_sds((n_blk, t_len, fj), BF16)] * 3, lambda tm: [act(tm)] * 3, epilogue, n_blk, riders,
                   tm_want=1024)


def ffn_gate(h, wg, name, riders):
    t_len, d = h.shape
    n_blk, _, fj = wg.shape

    def epilogue(accs, e_refs, o_refs, cols):
        o_refs[0][:, cols] = accs[0].astype(BF16)

    act = lambda tm: pl.BlockSpec((None, tm, fj), lambda j, t: (j, t, 0))
    return mm_cols(name, h, [wg], [_blk3(d, fj)], False, [], lambda tm: [],
                   [_sds((n_blk, t_len, fj), BF16)], lambda tm: [act(tm)], epilogue, n_blk, riders, tm_want=1024)


def ffn_up_after_gate(h, wu, gate, name, riders):
    t_len, d = h.shape
    n_blk, _, fj = wu.shape

    def epilogue(accs, e_refs, o_refs, cols):
        gv = e_refs[0][:, cols].astype(F32)
        o_refs[0][:, cols] = accs[0].astype(BF16)
        o_refs[1][:, cols] = (gv * _sigmoid(gv) * accs[0]).astype(BF16)

    act = lambda tm: pl.BlockSpec((None, tm, fj), lambda j, t: (j, t, 0))
    return mm_cols(name, h, [wu], [_blk3(d, fj)], False, [gate], lambda tm: [act(tm)],
                   [_sds((n_blk, t_len, fj), BF16)] * 2, lambda tm: [act(tm)] * 2, epilogue, n_blk, riders,
                   tm_want=1024)


def ffn_down(act, wd, res, name, riders):
    n_blk, t_len, fj = act.shape
    d = wd.shape[2]
    return mm_jsum(name, [act], [wd], False, res, 0.5, d, riders)


def ffn_bwd(ht, gate, up, act, wg, wu, wd, dyb, name):
    d, t_len = ht.shape
    n_blk, _, fj = act.shape

    def epilogue(accs, e_refs, o_refs, cols):
        d_act = 0.5 * accs[0]
        gv, uv = e_refs[0][:, cols].astype(F32), e_refs[1][:, cols].astype(F32)
        sg = _sigmoid(gv)
        o_refs[0][:, cols] = (d_act * uv * (sg * (1.0 + gv * (1.0 - sg)))).astype(BF16)
        o_refs[1][:, cols] = (d_act * gv * sg).astype(BF16)

    act_jt = lambda tm: pl.BlockSpec((None, tm, fj), lambda j, t: (j, t, 0))
    (d_wd,), _ = mm_tn(name + "_dwd", act, act_jt, [dyb],
                       lambda tt: [pl.BlockSpec((tt, d), lambda j, t: (t, 0))],
                       [_sds((n_blk, fj, d), BF16)], [_blk3(fj, d)], 0.5, t_len, n_blk)

    (d_gate, d_up), (recv_wd,) = mm_cols(name + "_dact", dyb, [wd], [_blk3(fj, d)], True, [gate, up],
                                         lambda tm: [act_jt(tm)] * 2, [_sds((n_blk, t_len, fj), BF16)] * 2,
                                         lambda tm: [act_jt(tm)] * 2, epilogue, n_blk, [Rider("scatter", d_wd)],
                                         tm_want=1024)

    whole_t = pl.BlockSpec((None, t_len, fj), lambda j, m: (j, 0, 0))
    d_wg, _ = mm_xt(name + "_dwg", ht, d_gate, whole_t, fj, n_blk)
    d_wu, (recv_wg,) = mm_xt(name + "_dwu", ht, d_up, whole_t, fj, n_blk, [Rider("scatter", d_wg)])

    dh, (recv_wu,) = mm_jsum(name + "_dh", [d_gate, d_up], [wg, wu], True, None, 1.0, d,
                             [Rider("scatter", d_wu)])
    return dh, recv_wg, recv_wu, recv_wd


def local_step(x, target, g1, wg1, wu1_s, wd1_s, gmix, win_s, conv_w, conv_b, ln_g, ln_b, gq, gk, wout_s, g3,
               wg2_s, wu2_s, wd2_s):
    t_len, d = x.shape
    ch = d // 2
    ij = win_s.shape[1]
    oj = wout_s.shape[0]
    n_blk = N_DEV

    h1, h1t = rms_fwd(x, g1, "rms1")
    (gate1,), (wu1, wout) = ffn_gate(h1, wg1, "ffn1_gate", [Rider("gather", wu1_s), Rider("gather", wout_s)])
    (up1, act1), (wd1, wg2) = ffn_up_after_gate(h1, wu1, gate1, "ffn1_up",
                                                [Rider("gather", wd1_s), Rider("gather", wg2_s)])
    x1, (win, wd2) = ffn_down(act1, wd1, x, "ffn1_down",
                              [Rider("gather_cols", win_s), Rider("gather", wd2_s)])

    h2, h2t = rms_fwd(x1, gmix, "rms_mix")

    def store_f32(accs, e_refs, o_refs, cols):
        o_refs[0][:, cols] = accs[0]

    d_in = n_blk * ij
    win_full = win.reshape(1, d, d_in)
    wide = d_in // W_IN_SPLIT
    (z,), (wu2,) = mm_cols(
        "w_in", h2, [win_full], [pl.BlockSpec((None, d, wide), lambda j, t: (0, 0, j))], False, [], lambda tm: [],
        [_sds((t_len, d_in), F32)],
        lambda tm: [pl.BlockSpec((tm, wide), lambda j, t: (t, j))], store_f32, W_IN_SPLIT,
        [Rider("gather", wu2_s)], tm_want=1024)

    conv_w32 = jnp.pad(conv_w, ((0, 32 - CONV_WIDTH), (0, 0)))
    y_conv, conv_pre = conv_fwd(z, conv_w32, conv_b, ln_g, ln_b, "conv_fwd")

    g2 = jnp.concatenate([jnp.tile(gq, (1, ch // HEAD_DIM)), jnp.tile(gk, (1, ch // HEAD_DIM))], axis=1)
    qk = qk_norm_fwd(z, g2, ch, "qk_norm")
    branch = [attn_fwd(qk, z, dil, ch, "attn_fwd_d%d" % dil) for dil in DILATIONS]
    att, y_cat, lg = attn_combine([o for o, _ in branch], [l for _, l in branch], y_conv, "attn_combine")

    wout_full = wout.reshape(1, n_blk * oj, d)
    x2, _ = mm_reduce(
        "w_out", [y_cat], lambda tm: [pl.BlockSpec((tm, n_blk * oj), lambda t, j: (t, 0))],
        [wout_full], [pl.BlockSpec((None, n_blk * oj, d), lambda t, j: (0, 0, 0))], False, x1, 1.0,
        t_len, d, 1)

    h3, h3t = rms_fwd(x2, g3, "rms3")
    (gate2, up2, act2), _ = ffn_up(h3, wg2, wu2, "ffn2_up", [])
    y, _ = ffn_down(act2, wd2, x2, "ffn2_down", [])

    loss_tile, dy, dyb = loss_head(y, target, "loss")

    dh3, recv_wg2, recv_wu2, recv_wd2 = ffn_bwd(h3t, gate2, up2, act2, wg2, wu2, wd2, dyb, "ffn2")
    dx2, dx2b, d_g3 = rms_bwd(x2, g3, dh3, dy, "rms3_bwd")

    (dy_cat,), _ = mm_cols("w_out_dy", dx2b, [wout_full], [_blk3(n_blk * oj, d)], True, [], lambda tm: [],
                           [_sds((t_len, n_blk * oj), F32)],
                           lambda tm: [pl.BlockSpec((tm, n_blk * oj), lambda j, t: (t, 0))], store_f32, 1,
                           tm_want=1024)
    (d_wout,), _ = mm_tn("w_out_dw", y_cat, lambda tt: pl.BlockSpec((tt, oj), lambda j, t: (t, j)),
                         [dx2b], lambda tt: [pl.BlockSpec((tt, d), lambda j, t: (t, 0))],
                         [_sds((n_blk, oj, d), BF16)], [_blk3(oj, d)], 1.0, t_len, n_blk)

    dc, d_lg, d_lb, d_cb = conv_bwd_norm(conv_pre, dy_cat, ln_g, ln_b, "conv_bwd_norm")
    dz_a, dz_g, d_cw8 = conv_bwd_taps(z, dc, conv_w32, "conv_bwd_taps")
    d_cw = jnp.sum(d_cw8, axis=1)[:CONV_WIDTH]

    grads = [attn_bwd(qk, z, dy_cat, att, lg, dil, ch, "attn_bwd_d%d" % dil) for dil in DILATIONS]
    gq_t = jnp.tile(gq, (1, LANES // HEAD_DIM))
    gk_t = jnp.tile(gk, (1, LANES // HEAD_DIM))
    dz_q, d_gq2 = qk_norm_bwd(z, gq_t, [g[0] for g in grads], 2, ch, "q_norm_bwd")
    dz_k, d_gk2 = qk_norm_bwd(z, gk_t, [g[1] for g in grads], 3, ch, "k_norm_bwd")
    d_gq = d_gq2[:, :HEAD_DIM] + d_gq2[:, HEAD_DIM:]
    d_gk = d_gk2[:, :HEAD_DIM] + d_gk2[:, HEAD_DIM:]
    dzb = pack_dz([dz_a, dz_g, dz_q, dz_k], [g[2] for g in grads], "dz_pack")

    d_win, (recv_wout,) = mm_xt("w_in_dw", h2t, dzb, pl.BlockSpec((t_len, ij), lambda j, m: (0, j)), ij, n_blk,
                                [Rider("scatter", d_wout)])
    tall = d // W_IN_SPLIT
    (dh2,), (recv_win,) = mm_cols(
        "w_in_dh", dzb, [win_full], [pl.BlockSpec((None, tall, d_in), lambda j, t: (0, j, 0))], True, [],
        lambda tm: [], [_sds((t_len, d), F32)],
        lambda tm: [pl.BlockSpec((tm, tall), lambda j, t: (t, j))], store_f32, W_IN_SPLIT,
        [Rider("scatter", d_win)], tm_want=1024)
    dx1, dx1b, d_gmix = rms_bwd(x1, gmix, dh2, dx2, "rms_mix_bwd")

    dh1, recv_wg1, recv_wu1, recv_wd1 = ffn_bwd(h1t, gate1, up1, act1, wg1, wu1, wd1, dx1b, "ffn1")
    grad_x, _, d_g1 = rms_bwd(x, g1, dh1, dx1, "rms1_bwd")

    big = dict(ffn1_w_gate=recv_wg1, ffn1_w_up=recv_wu1, ffn1_w_down=recv_wd1, w_in=recv_win, w_out=recv_wout,
               ffn2_w_gate=recv_wg2, ffn2_w_up=recv_wu2, ffn2_w_down=recv_wd2)
    small = dict(g1=d_g1, gmix=d_gmix, g3=d_g3, conv_b=d_cb, ln_g=d_lg, ln_b=d_lb, gq=d_gq, gk=d_gk, conv_w=d_cw)
    return loss_tile[0, 0], grad_x, big, small


SMALL_ROWS = 48


def _pack_small(ch, g1, gmix, g3, conv_b, ln_g, ln_b, gq, gk, conv_w):
    pad_head = lambda v: jnp.pad(v, ((0, 0), (0, ch - v.shape[1])))
    rows = [g1.reshape(2, ch), gmix.reshape(2, ch), g3.reshape(2, ch), conv_b, ln_g, ln_b,
            pad_head(gq), pad_head(gk), conv_w]
    packed = jnp.concatenate(rows, axis=0)
    return jnp.pad(packed, ((0, SMALL_ROWS - packed.shape[0]), (0, 0)))


def _unpack_small(packed, d):
    return dict(g1=packed[0:2].reshape(1, d), gmix=packed[2:4].reshape(1, d), g3=packed[4:6].reshape(1, d),
                conv_b=packed[6:7], ln_g=packed[7:8], ln_b=packed[8:9],
                gq=packed[9:10, :HEAD_DIM], gk=packed[10:11, :HEAD_DIM])


def kernel(x, ffn1_norm_g, ffn1_w_gate, ffn1_w_up, ffn1_w_down, mix_norm_g, w_in, conv_w_dw, conv_b_dw, conv_ln_g, conv_ln_b, q_norm_g, k_norm_g, w_out, ffn2_norm_g, ffn2_w_gate, ffn2_w_up, ffn2_w_down, loss_target, m_ffn1_norm_g, m_ffn1_w_gate, m_ffn1_w_up, m_ffn1_w_down, m_mix_norm_g, m_w_in, m_conv_w_dw, m_conv_b_dw, m_conv_ln_g, m_conv_ln_b, m_q_norm_g, m_k_norm_g, m_w_out, m_ffn2_norm_g, m_ffn2_w_gate, m_ffn2_w_up, m_ffn2_w_down, v_ffn1_norm_g, v_ffn1_w_gate, v_ffn1_w_up, v_ffn1_w_down, v_mix_norm_g, v_w_in, v_conv_w_dw, v_conv_b_dw, v_conv_ln_g, v_conv_ln_b, v_q_norm_g, v_k_norm_g, v_w_out, v_ffn2_norm_g, v_ffn2_w_gate, v_ffn2_w_up, v_ffn2_w_down):
    d = x.shape[-1]
    ch = d // 2
    me = 4 * lax.axis_index("x") + 2 * lax.axis_index("y") + lax.axis_index("c")

    shard = lambda w: w[0].astype(BF16)
    wg1 = all_gather(shard(ffn1_w_gate), "ag_wg1")
    cw_all = all_gather(conv_w_dw[0], "ag_convw")
    conv_w = jnp.transpose(cw_all, (1, 0, 2)).reshape(CONV_WIDTH, ch)

    loss_part, grad_x, big, small = local_step(
        x[0], loss_target[0], ffn1_norm_g, wg1, shard(ffn1_w_up), shard(ffn1_w_down), mix_norm_g, shard(w_in),
        conv_w, conv_b_dw, conv_ln_g, conv_ln_b, q_norm_g, k_norm_g, shard(w_out), ffn2_norm_g,
        shard(ffn2_w_gate), shard(ffn2_w_up), shard(ffn2_w_down))
    loss = lax.psum(loss_part, MESH_AXES)

    state = dict(
        ffn1_w_gate=(ffn1_w_gate, m_ffn1_w_gate, v_ffn1_w_gate), ffn1_w_up=(ffn1_w_up, m_ffn1_w_up, v_ffn1_w_up),
        ffn1_w_down=(ffn1_w_down, m_ffn1_w_down, v_ffn1_w_down), w_in=(w_in, m_w_in, v_w_in),
        w_out=(w_out, m_w_out, v_w_out),
        ffn2_w_gate=(ffn2_w_gate, m_ffn2_w_gate, v_ffn2_w_gate), ffn2_w_up=(ffn2_w_up, m_ffn2_w_up, v_ffn2_w_up),
        ffn2_w_down=(ffn2_w_down, m_ffn2_w_down, v_ffn2_w_down))
    out = {}
    for pname, (w, m, v) in state.items():
        out[pname] = [r[None] for r in adamw(w[0], m[0], v[0], big[pname], "adamw_" + pname)]

    zero_taps = jnp.zeros((CONV_WIDTH, ch), F32)
    pack = lambda g1, gm, g3, cb, lg, lb, gq, gk: _pack_small(ch, g1, gm, g3, cb, lg, lb, gq, gk, zero_taps)
    small_parts = all_gather(_pack_small(ch, **small), "ag_small_grads")
    s_res = adamw(
        pack(ffn1_norm_g, mix_norm_g, ffn2_norm_g, conv_b_dw, conv_ln_g, conv_ln_b, q_norm_g, k_norm_g),
        pack(m_ffn1_norm_g, m_mix_norm_g, m_ffn2_norm_g, m_conv_b_dw, m_conv_ln_g, m_conv_ln_b, m_q_norm_g, m_k_norm_g),
        pack(v_ffn1_norm_g, v_mix_norm_g, v_ffn2_norm_g, v_conv_b_dw, v_conv_ln_g, v_conv_ln_b, v_q_norm_g, v_k_norm_g),
        small_parts, "adamw_small")
    s_out = [_unpack_small(r, d) for r in s_res]
    names = dict(g1="ffn1_norm_g", gmix="mix_norm_g", g3="ffn2_norm_g", conv_b="conv_b_dw", ln_g="conv_ln_g",
                 ln_b="conv_ln_b", gq="q_norm_g", gk="k_norm_g")
    for key, full in names.items():
        out[full] = [r[key] for r in s_out]

    cshard = ch // N_DEV
    taps_sum = s_res[0][11:11 + CONV_WIDTH]
    taps_mine = lax.dynamic_slice(taps_sum, (0, me * cshard), (CONV_WIDTH, cshard))
    pad_taps = lambda a: jnp.pad(a, ((0, 32 - CONV_WIDTH), (0, 0)))
    c_res = adamw(pad_taps(conv_w_dw[0]), pad_taps(m_conv_w_dw[0]), pad_taps(v_conv_w_dw[0]),
                  pad_taps(taps_mine)[None], "adamw_convw")
    out["conv_w_dw"] = [r[:CONV_WIDTH][None] for r in c_res]

    order = ["ffn1_norm_g", "ffn1_w_gate", "ffn1_w_up", "ffn1_w_down", "mix_norm_g", "w_in", "conv_w_dw",
             "conv_b_dw", "conv_ln_g", "conv_ln_b", "q_norm_g", "k_norm_g", "w_out", "ffn2_norm_g",
             "ffn2_w_gate", "ffn2_w_up", "ffn2_w_down"]
    result = [loss, grad_x[None]]
    for kind in range(4):
        result += [out[n][kind] for n in order]
    return tuple(result)
```
